```python
import math
import jax, jax.numpy as jnp
from jax import lax
import numpy as np

D_MODEL = 2048
BATCH = 8
SEQ = 8192
DEPTH = 1

D_MIX = D_MODEL
D_LRU = D_MIX // 2
D_SC = D_MIX - D_LRU
N_LRU_HEADS = 8
LRU_HEAD_DIM = D_LRU // N_LRU_HEADS
N_SC_HEADS = 8
LRU_CONV_WIDTH = 4
SC_CONV_WIDTH = 3
LRU_C = 8.0
D_FF = 5632
FFN_RESIDUAL_SCALE = 0.5
NORM_EPS = 1e-6
D_IN_PROJ = 2 * D_LRU + 3 * D_SC

kernel_name = "hawk_shortconv_macaron_hybrid"


def rms_norm(x, gain):
    xf = x.astype(jnp.float32)
    var = jnp.mean(xf * xf, axis=-1, keepdims=True)
    return (xf * lax.rsqrt(var + NORM_EPS) * gain.astype(jnp.float32)).astype(x.dtype)


def swiglu_ffn(x, w_gate, w_up, w_down):
    return (jax.nn.silu(x @ w_gate) * (x @ w_up)) @ w_down


def causal_depthwise_conv(x, w):
    K = w.shape[0]
    S = x.shape[1]
    xp = jnp.pad(x, ((0, 0), (K - 1, 0), (0, 0)))
    y = xp[:, 0:S] * w[0]
    for k in range(1, K):
        y = y + xp[:, k:k + S] * w[k]
    return y


def _lru_combine(left, right):
    a_l, b_l = left
    a_r, b_r = right
    return a_l * a_r, a_r * b_l + b_r


def rg_lru(x, w_a, b_a, w_i, b_i, lam):
    Bsz, S, W = x.shape
    xh = x.reshape(Bsz, S, N_LRU_HEADS, LRU_HEAD_DIM)
    r = jax.nn.sigmoid(jnp.einsum('bshi,hij->bshj', xh, w_a) + b_a).reshape(Bsz, S, W)
    i = jax.nn.sigmoid(jnp.einsum('bshi,hij->bshj', xh, w_i) + b_i).reshape(Bsz, S, W)
    log_a = -LRU_C * r.astype(jnp.float32) * jax.nn.softplus(-lam.astype(jnp.float32))
    a = jnp.exp(log_a)
    mult = jnp.sqrt(-jnp.expm1(2.0 * log_a))
    u = mult * (i * x).astype(jnp.float32)
    _, h = lax.associative_scan(_lru_combine, (a, u), axis=1)
    return h.astype(x.dtype)


def _fwd_setup_inputs(seed: int = 0) -> dict:
    key = jax.random.key(seed)
    ks = jax.random.split(key, 32)
    f32 = jnp.float32

    def normal(k, shape, fan_in):
        return jax.random.normal(k, shape, f32) * (fan_in ** -0.5)

    def gain(k, shape):
        return 1.0 + 0.02 * jax.random.normal(k, shape, f32)

    def small(k, shape):
        return 0.01 * jax.random.normal(k, shape, f32)

    L = DEPTH
    a0 = jax.random.uniform(ks[13], (L, D_LRU), f32, 0.9, 0.999) ** (1.0 / LRU_C)
    lru_lambda = jnp.log(a0) - jnp.log1p(-a0)
    return {
        "x": jax.random.normal(ks[0], (BATCH, SEQ, D_MODEL), f32),
        "ffn1_norm": gain(ks[1], (L, D_MODEL)),
        "ffn1_w_gate": normal(ks[2], (L, D_MODEL, D_FF), D_MODEL),
        "ffn1_w_up": normal(ks[3], (L, D_MODEL, D_FF), D_MODEL),
        "ffn1_w_down": normal(ks[4], (L, D_FF, D_MODEL), D_FF),
        "mix_norm": gain(ks[5], (L, D_MODEL)),
        "w_in": normal(ks[6], (L, D_MODEL, D_IN_PROJ), D_MODEL),
        "lru_conv_w": normal(ks[7], (L, LRU_CONV_WIDTH, D_LRU), LRU_CONV_WIDTH),
        "lru_conv_b": small(ks[8], (L, D_LRU)),
        "lru_w_a": normal(ks[9], (L, N_LRU_HEADS, LRU_HEAD_DIM, LRU_HEAD_DIM), LRU_HEAD_DIM),
        "lru_b_a": small(ks[10], (L, N_LRU_HEADS, LRU_HEAD_DIM)),
        "lru_w_i": normal(ks[11], (L, N_LRU_HEADS, LRU_HEAD_DIM, LRU_HEAD_DIM), LRU_HEAD_DIM),
        "lru_b_i": small(ks[12], (L, N_LRU_HEADS, LRU_HEAD_DIM)),
        "lru_lambda": lru_lambda,
        "sc_conv_w": normal(ks[14], (L, SC_CONV_WIDTH, D_SC), SC_CONV_WIDTH),
        "lru_out_norm": gain(ks[15], (L, D_LRU)),
        "sc_out_norm": gain(ks[16], (L, D_SC)),
        "w_out": normal(ks[17], (L, D_MIX, D_MODEL), D_MIX),
        "ffn2_norm": gain(ks[18], (L, D_MODEL)),
        "ffn2_w_gate": normal(ks[19], (L, D_MODEL, D_FF), D_MODEL),
        "ffn2_w_up": normal(ks[20], (L, D_MODEL, D_FF), D_MODEL),
        "ffn2_w_down": normal(ks[21], (L, D_FF, D_MODEL), D_FF),
        "final_norm": gain(ks[22], (D_MODEL,)),
    }


def _fwd_reference(x, ffn1_norm, ffn1_w_gate, ffn1_w_up, ffn1_w_down, mix_norm, w_in,
              lru_conv_w, lru_conv_b, lru_w_a, lru_b_a, lru_w_i, lru_b_i, lru_lambda,
              sc_conv_w, lru_out_norm, sc_out_norm, w_out,
              ffn2_norm, ffn2_w_gate, ffn2_w_up, ffn2_w_down, final_norm):
    for l in range(DEPTH):
        x = x + FFN_RESIDUAL_SCALE * swiglu_ffn(rms_norm(x, ffn1_norm[l]),
                                                ffn1_w_gate[l], ffn1_w_up[l], ffn1_w_down[l])
        z = rms_norm(x, mix_norm[l]) @ w_in[l]
        o = 0
        lru_x = z[..., o:o + D_LRU]; o += D_LRU
        lru_gate = z[..., o:o + D_LRU]; o += D_LRU
        sc_b = z[..., o:o + D_SC]; o += D_SC
        sc_c = z[..., o:o + D_SC]; o += D_SC
        sc_x = z[..., o:o + D_SC]
        xc = causal_depthwise_conv(lru_x, lru_conv_w[l]) + lru_conv_b[l]
        h = rg_lru(xc, lru_w_a[l], lru_b_a[l], lru_w_i[l], lru_b_i[l], lru_lambda[l])
        y_lru = h * jax.nn.gelu(lru_gate, approximate=True)
        y_sc = sc_b * causal_depthwise_conv(sc_c * sc_x, sc_conv_w[l])
        y = jnp.concatenate([rms_norm(y_lru, lru_out_norm[l]),
                             rms_norm(y_sc, sc_out_norm[l])], axis=-1)
        x = x + y @ w_out[l]
        x = x + FFN_RESIDUAL_SCALE * swiglu_ffn(rms_norm(x, ffn2_norm[l]),
                                                ffn2_w_gate[l], ffn2_w_up[l], ffn2_w_down[l])
    return rms_norm(x, final_norm)


import jax as _jax
import jax.numpy as _jnp

TWIN_FORMAT = 'train_step'
FWD_PARAMS = ['x', 'ffn1_norm', 'ffn1_w_gate', 'ffn1_w_up', 'ffn1_w_down', 'mix_norm', 'w_in', 'lru_conv_w', 'lru_conv_b', 'lru_w_a', 'lru_b_a', 'lru_w_i', 'lru_b_i', 'lru_lambda', 'sc_conv_w', 'lru_out_norm', 'sc_out_norm', 'w_out', 'ffn2_norm', 'ffn2_w_gate', 'ffn2_w_up', 'ffn2_w_down', 'final_norm']
TWIN_WEIGHTS = ['ffn1_norm', 'ffn1_w_gate', 'ffn1_w_up', 'ffn1_w_down', 'mix_norm', 'w_in', 'lru_conv_w', 'lru_conv_b', 'lru_w_a', 'lru_b_a', 'lru_w_i', 'lru_b_i', 'lru_lambda', 'sc_conv_w', 'lru_out_norm', 'sc_out_norm', 'w_out', 'ffn2_norm', 'ffn2_w_gate', 'ffn2_w_up', 'ffn2_w_down', 'final_norm']
TWIN_DIFF_INPUT = 'x'
TWIN_INPUTS = ['x', 'ffn1_norm', 'ffn1_w_gate', 'ffn1_w_up', 'ffn1_w_down', 'mix_norm', 'w_in', 'lru_conv_w', 'lru_conv_b', 'lru_w_a', 'lru_b_a', 'lru_w_i', 'lru_b_i', 'lru_lambda', 'sc_conv_w', 'lru_out_norm', 'sc_out_norm', 'w_out', 'ffn2_norm', 'ffn2_w_gate', 'ffn2_w_up', 'ffn2_w_down', 'final_norm', 'loss_target', 'm_ffn1_norm', 'm_ffn1_w_gate', 'm_ffn1_w_up', 'm_ffn1_w_down', 'm_mix_norm', 'm_w_in', 'm_lru_conv_w', 'm_lru_conv_b', 'm_lru_w_a', 'm_lru_b_a', 'm_lru_w_i', 'm_lru_b_i', 'm_lru_lambda', 'm_sc_conv_w', 'm_lru_out_norm', 'm_sc_out_norm', 'm_w_out', 'm_ffn2_norm', 'm_ffn2_w_gate', 'm_ffn2_w_up', 'm_ffn2_w_down', 'm_final_norm', 'v_ffn1_norm', 'v_ffn1_w_gate', 'v_ffn1_w_up', 'v_ffn1_w_down', 'v_mix_norm', 'v_w_in', 'v_lru_conv_w', 'v_lru_conv_b', 'v_lru_w_a', 'v_lru_b_a', 'v_lru_w_i', 'v_lru_b_i', 'v_lru_lambda', 'v_sc_conv_w', 'v_lru_out_norm', 'v_sc_out_norm', 'v_w_out', 'v_ffn2_norm', 'v_ffn2_w_gate', 'v_ffn2_w_up', 'v_ffn2_w_down', 'v_final_norm']
TWIN_OUTPUTS = ['loss', 'grad_x', 'grad_ffn1_norm', 'grad_ffn1_w_gate', 'grad_ffn1_w_up', 'grad_ffn1_w_down', 'grad_mix_norm', 'grad_w_in', 'grad_lru_conv_w', 'grad_lru_conv_b', 'grad_lru_w_a', 'grad_lru_b_a', 'grad_lru_w_i', 'grad_lru_b_i', 'grad_lru_lambda', 'grad_sc_conv_w', 'grad_lru_out_norm', 'grad_sc_out_norm', 'grad_w_out', 'grad_ffn2_norm', 'grad_ffn2_w_gate', 'grad_ffn2_w_up', 'grad_ffn2_w_down', 'grad_final_norm', 'delta_ffn1_norm', 'delta_ffn1_w_gate', 'delta_ffn1_w_up', 'delta_ffn1_w_down', 'delta_mix_norm', 'delta_w_in', 'delta_lru_conv_w', 'delta_lru_conv_b', 'delta_lru_w_a', 'delta_lru_b_a', 'delta_lru_w_i', 'delta_lru_b_i', 'delta_lru_lambda', 'delta_sc_conv_w', 'delta_lru_out_norm', 'delta_sc_out_norm', 'delta_w_out', 'delta_ffn2_norm', 'delta_ffn2_w_gate', 'delta_ffn2_w_up', 'delta_ffn2_w_down', 'delta_final_norm', 'new_m_ffn1_norm', 'new_m_ffn1_w_gate', 'new_m_ffn1_w_up', 'new_m_ffn1_w_down', 'new_m_mix_norm', 'new_m_w_in', 'new_m_lru_conv_w', 'new_m_lru_conv_b', 'new_m_lru_w_a', 'new_m_lru_b_a', 'new_m_lru_w_i', 'new_m_lru_b_i', 'new_m_lru_lambda', 'new_m_sc_conv_w', 'new_m_lru_out_norm', 'new_m_sc_out_norm', 'new_m_w_out', 'new_m_ffn2_norm', 'new_m_ffn2_w_gate', 'new_m_ffn2_w_up', 'new_m_ffn2_w_down', 'new_m_final_norm', 'new_v_ffn1_norm', 'new_v_ffn1_w_gate', 'new_v_ffn1_w_up', 'new_v_ffn1_w_down', 'new_v_mix_norm', 'new_v_w_in', 'new_v_lru_conv_w', 'new_v_lru_conv_b', 'new_v_lru_w_a', 'new_v_lru_b_a', 'new_v_lru_w_i', 'new_v_lru_b_i', 'new_v_lru_lambda', 'new_v_sc_conv_w', 'new_v_lru_out_norm', 'new_v_sc_out_norm', 'new_v_w_out', 'new_v_ffn2_norm', 'new_v_ffn2_w_gate', 'new_v_ffn2_w_up', 'new_v_ffn2_w_down', 'new_v_final_norm']
TWIN_LEAF_KINDS = {'loss': 'loss', 'grad_x': 'grad_x', 'grad_ffn1_norm': 'grad_w', 'grad_ffn1_w_gate': 'grad_w', 'grad_ffn1_w_up': 'grad_w', 'grad_ffn1_w_down': 'grad_w', 'grad_mix_norm': 'grad_w', 'grad_w_in': 'grad_w', 'grad_lru_conv_w': 'grad_w', 'grad_lru_conv_b': 'grad_w', 'grad_lru_w_a': 'grad_w', 'grad_lru_b_a': 'grad_w', 'grad_lru_w_i': 'grad_w', 'grad_lru_b_i': 'grad_w', 'grad_lru_lambda': 'grad_w', 'grad_sc_conv_w': 'grad_w', 'grad_lru_out_norm': 'grad_w', 'grad_sc_out_norm': 'grad_w', 'grad_w_out': 'grad_w', 'grad_ffn2_norm': 'grad_w', 'grad_ffn2_w_gate': 'grad_w', 'grad_ffn2_w_up': 'grad_w', 'grad_ffn2_w_down': 'grad_w', 'grad_final_norm': 'grad_w', 'delta_ffn1_norm': 'delta_w', 'delta_ffn1_w_gate': 'delta_w', 'delta_ffn1_w_up': 'delta_w', 'delta_ffn1_w_down': 'delta_w', 'delta_mix_norm': 'delta_w', 'delta_w_in': 'delta_w', 'delta_lru_conv_w': 'delta_w', 'delta_lru_conv_b': 'delta_w', 'delta_lru_w_a': 'delta_w', 'delta_lru_b_a': 'delta_w', 'delta_lru_w_i': 'delta_w', 'delta_lru_b_i': 'delta_w', 'delta_lru_lambda': 'delta_w', 'delta_sc_conv_w': 'delta_w', 'delta_lru_out_norm': 'delta_w', 'delta_sc_out_norm': 'delta_w', 'delta_w_out': 'delta_w', 'delta_ffn2_norm': 'delta_w', 'delta_ffn2_w_gate': 'delta_w', 'delta_ffn2_w_up': 'delta_w', 'delta_ffn2_w_down': 'delta_w', 'delta_final_norm': 'delta_w', 'new_m_ffn1_norm': 'new_m', 'new_m_ffn1_w_gate': 'new_m', 'new_m_ffn1_w_up': 'new_m', 'new_m_ffn1_w_down': 'new_m', 'new_m_mix_norm': 'new_m', 'new_m_w_in': 'new_m', 'new_m_lru_conv_w': 'new_m', 'new_m_lru_conv_b': 'new_m', 'new_m_lru_w_a': 'new_m', 'new_m_lru_b_a': 'new_m', 'new_m_lru_w_i': 'new_m', 'new_m_lru_b_i': 'new_m', 'new_m_lru_lambda': 'new_m', 'new_m_sc_conv_w': 'new_m', 'new_m_lru_out_norm': 'new_m', 'new_m_sc_out_norm': 'new_m', 'new_m_w_out': 'new_m', 'new_m_ffn2_norm': 'new_m', 'new_m_ffn2_w_gate': 'new_m', 'new_m_ffn2_w_up': 'new_m', 'new_m_ffn2_w_down': 'new_m', 'new_m_final_norm': 'new_m', 'new_v_ffn1_norm': 'new_v', 'new_v_ffn1_w_gate': 'new_v', 'new_v_ffn1_w_up': 'new_v', 'new_v_ffn1_w_down': 'new_v', 'new_v_mix_norm': 'new_v', 'new_v_w_in': 'new_v', 'new_v_lru_conv_w': 'new_v', 'new_v_lru_conv_b': 'new_v', 'new_v_lru_w_a': 'new_v', 'new_v_lru_b_a': 'new_v', 'new_v_lru_w_i': 'new_v', 'new_v_lru_b_i': 'new_v', 'new_v_lru_lambda': 'new_v', 'new_v_sc_conv_w': 'new_v', 'new_v_lru_out_norm': 'new_v', 'new_v_sc_out_norm': 'new_v', 'new_v_w_out': 'new_v', 'new_v_ffn2_norm': 'new_v', 'new_v_ffn2_w_gate': 'new_v', 'new_v_ffn2_w_up': 'new_v', 'new_v_ffn2_w_down': 'new_v', 'new_v_final_norm': 'new_v'}


def _forward(args):
    return _fwd_reference(*[args[k] for k in FWD_PARAMS])


def _output_shape():
    def fwd():
        inp = _fwd_setup_inputs(0)
        return _fwd_reference(*[inp[k] for k in FWD_PARAMS])
    out = _jax.eval_shape(fwd)
    return out.shape, out.dtype

N_MICROBATCH = 1
ADAM_LR = 0.001
ADAM_B1 = 0.9
ADAM_B2 = 0.999
ADAM_EPS = 1e-08
ADAM_WD = 0.01
ADAM_STEP = 10
PER_EXAMPLE_BATCH_AXIS = {'x': 0, 'loss_target': 0}
SHARED_INPUTS = []
_WEIGHT_DTYPES = {'ffn1_norm': _jnp.float32, 'ffn1_w_gate': _jnp.float32, 'ffn1_w_up': _jnp.float32, 'ffn1_w_down': _jnp.float32, 'mix_norm': _jnp.float32, 'w_in': _jnp.float32, 'lru_conv_w': _jnp.float32, 'lru_conv_b': _jnp.float32, 'lru_w_a': _jnp.float32, 'lru_b_a': _jnp.float32, 'lru_w_i': _jnp.float32, 'lru_b_i': _jnp.float32, 'lru_lambda': _jnp.float32, 'sc_conv_w': _jnp.float32, 'lru_out_norm': _jnp.float32, 'sc_out_norm': _jnp.float32, 'w_out': _jnp.float32, 'ffn2_norm': _jnp.float32, 'ffn2_w_gate': _jnp.float32, 'ffn2_w_up': _jnp.float32, 'ffn2_w_down': _jnp.float32, 'final_norm': _jnp.float32}
MOMENT_SCALE = {'ffn1_norm': 6.978920e-02, 'ffn1_w_gate': 3.020124e-02, 'ffn1_w_up': 2.923539e-02, 'ffn1_w_down': 4.852070e-02, 'mix_norm': 1.437040e-01, 'w_in': 8.993179e-02, 'lru_conv_w': 9.816641e-02, 'lru_conv_b': 9.889910e-01, 'lru_w_a': 2.958217e-02, 'lru_b_a': 2.419408e-02, 'lru_w_i': 5.337639e-02, 'lru_b_i': 3.307069e-02, 'lru_lambda': 5.117747e-02, 'sc_conv_w': 9.296191e-02, 'lru_out_norm': 1.038777e-01, 'sc_out_norm': 9.300869e-02, 'w_out': 8.917583e-02, 'ffn2_norm': 3.558320e-02, 'ffn2_w_gate': 1.580960e-02, 'ffn2_w_up': 1.529151e-02, 'ffn2_w_down': 2.538026e-02, 'final_norm': 3.198578e+01}


def _to_microbatches(a, axis):
    t = _jnp.moveaxis(a, axis, 0)
    t = t.reshape((N_MICROBATCH, t.shape[0] // N_MICROBATCH) + t.shape[1:])
    return _jnp.moveaxis(t, 1, axis + 1)


def setup_inputs(seed: int = 0) -> dict:
    inp = _fwd_setup_inputs(seed)
    key = _jax.random.fold_in(_jax.random.key(seed), 7919)
    shape, _ = _output_shape()
    out = dict(inp)
    out["loss_target"] = _jax.random.normal(_jax.random.fold_in(key, 0), shape, _jnp.float32)
    for i, name in enumerate(TWIN_WEIGHTS):
        w = inp[name].astype(_jnp.float32)
        if MOMENT_SCALE is None:
            s = _jnp.sqrt(_jnp.mean(_jnp.square(w)) + 1e-30)
        else:
            s = MOMENT_SCALE[name]
        km, kv = _jax.random.split(_jax.random.fold_in(key, i + 1))
        out[name] = w
        out["m_" + name] = s * _jax.random.normal(km, w.shape, _jnp.float32)
        out["v_" + name] = (s * s) * _jax.random.uniform(kv, w.shape, _jnp.float32, 0.5, 1.5)
    if N_MICROBATCH > 1:
        for name, axis in PER_EXAMPLE_BATCH_AXIS.items():
            out[name] = _to_microbatches(out[name], axis)
    return {'x': out['x'], 'ffn1_norm': out['ffn1_norm'], 'ffn1_w_gate': out['ffn1_w_gate'], 'ffn1_w_up': out['ffn1_w_up'], 'ffn1_w_down': out['ffn1_w_down'], 'mix_norm': out['mix_norm'], 'w_in': out['w_in'], 'lru_conv_w': out['lru_conv_w'], 'lru_conv_b': out['lru_conv_b'], 'lru_w_a': out['lru_w_a'], 'lru_b_a': out['lru_b_a'], 'lru_w_i': out['lru_w_i'], 'lru_b_i': out['lru_b_i'], 'lru_lambda': out['lru_lambda'], 'sc_conv_w': out['sc_conv_w'], 'lru_out_norm': out['lru_out_norm'], 'sc_out_norm': out['sc_out_norm'], 'w_out': out['w_out'], 'ffn2_norm': out['ffn2_norm'], 'ffn2_w_gate': out['ffn2_w_gate'], 'ffn2_w_up': out['ffn2_w_up'], 'ffn2_w_down': out['ffn2_w_down'], 'final_norm': out['final_norm'], 'loss_target': out['loss_target'], 'm_ffn1_norm': out['m_ffn1_norm'], 'm_ffn1_w_gate': out['m_ffn1_w_gate'], 'm_ffn1_w_up': out['m_ffn1_w_up'], 'm_ffn1_w_down': out['m_ffn1_w_down'], 'm_mix_norm': out['m_mix_norm'], 'm_w_in': out['m_w_in'], 'm_lru_conv_w': out['m_lru_conv_w'], 'm_lru_conv_b': out['m_lru_conv_b'], 'm_lru_w_a': out['m_lru_w_a'], 'm_lru_b_a': out['m_lru_b_a'], 'm_lru_w_i': out['m_lru_w_i'], 'm_lru_b_i': out['m_lru_b_i'], 'm_lru_lambda': out['m_lru_lambda'], 'm_sc_conv_w': out['m_sc_conv_w'], 'm_lru_out_norm': out['m_lru_out_norm'], 'm_sc_out_norm': out['m_sc_out_norm'], 'm_w_out': out['m_w_out'], 'm_ffn2_norm': out['m_ffn2_norm'], 'm_ffn2_w_gate': out['m_ffn2_w_gate'], 'm_ffn2_w_up': out['m_ffn2_w_up'], 'm_ffn2_w_down': out['m_ffn2_w_down'], 'm_final_norm': out['m_final_norm'], 'v_ffn1_norm': out['v_ffn1_norm'], 'v_ffn1_w_gate': out['v_ffn1_w_gate'], 'v_ffn1_w_up': out['v_ffn1_w_up'], 'v_ffn1_w_down': out['v_ffn1_w_down'], 'v_mix_norm': out['v_mix_norm'], 'v_w_in': out['v_w_in'], 'v_lru_conv_w': out['v_lru_conv_w'], 'v_lru_conv_b': out['v_lru_conv_b'], 'v_lru_w_a': out['v_lru_w_a'], 'v_lru_b_a': out['v_lru_b_a'], 'v_lru_w_i': out['v_lru_w_i'], 'v_lru_b_i': out['v_lru_b_i'], 'v_lru_lambda': out['v_lru_lambda'], 'v_sc_conv_w': out['v_sc_conv_w'], 'v_lru_out_norm': out['v_lru_out_norm'], 'v_sc_out_norm': out['v_sc_out_norm'], 'v_w_out': out['v_w_out'], 'v_ffn2_norm': out['v_ffn2_norm'], 'v_ffn2_w_gate': out['v_ffn2_w_gate'], 'v_ffn2_w_up': out['v_ffn2_w_up'], 'v_ffn2_w_down': out['v_ffn2_w_down'], 'v_final_norm': out['v_final_norm']}


def _loss(weights, diff, rest, loss_target):
    with _jax.named_scope("forward"):
        args = {**rest, TWIN_DIFF_INPUT: diff, **{k: w.astype(_WEIGHT_DTYPES[k]) for k, w in weights.items()}}
        y = _forward(args)
    with _jax.named_scope("loss_head"):
        err = _jnp.square(y.astype(_jnp.float32) - loss_target)
        return 0.5 * _jnp.sum(_jnp.mean(err, axis=-1)) if err.ndim else 0.5 * err


def _adamw(w, g, m, v):
    m = ADAM_B1 * m + (1.0 - ADAM_B1) * g
    v = ADAM_B2 * v + (1.0 - ADAM_B2) * _jnp.square(g)
    m_hat = m / (1.0 - ADAM_B1 ** ADAM_STEP)
    v_hat = v / (1.0 - ADAM_B2 ** ADAM_STEP)
    delta = -ADAM_LR * (m_hat / (_jnp.sqrt(v_hat) + ADAM_EPS) + ADAM_WD * w)
    return delta, m, v


def reference(x, ffn1_norm, ffn1_w_gate, ffn1_w_up, ffn1_w_down, mix_norm, w_in, lru_conv_w, lru_conv_b, lru_w_a, lru_b_a, lru_w_i, lru_b_i, lru_lambda, sc_conv_w, lru_out_norm, sc_out_norm, w_out, ffn2_norm, ffn2_w_gate, ffn2_w_up, ffn2_w_down, final_norm, loss_target, m_ffn1_norm, m_ffn1_w_gate, m_ffn1_w_up, m_ffn1_w_down, m_mix_norm, m_w_in, m_lru_conv_w, m_lru_conv_b, m_lru_w_a, m_lru_b_a, m_lru_w_i, m_lru_b_i, m_lru_lambda, m_sc_conv_w, m_lru_out_norm, m_sc_out_norm, m_w_out, m_ffn2_norm, m_ffn2_w_gate, m_ffn2_w_up, m_ffn2_w_down, m_final_norm, v_ffn1_norm, v_ffn1_w_gate, v_ffn1_w_up, v_ffn1_w_down, v_mix_norm, v_w_in, v_lru_conv_w, v_lru_conv_b, v_lru_w_a, v_lru_b_a, v_lru_w_i, v_lru_b_i, v_lru_lambda, v_sc_conv_w, v_lru_out_norm, v_sc_out_norm, v_w_out, v_ffn2_norm, v_ffn2_w_gate, v_ffn2_w_up, v_ffn2_w_down, v_final_norm):
    given = dict(x=x, ffn1_norm=ffn1_norm, ffn1_w_gate=ffn1_w_gate, ffn1_w_up=ffn1_w_up, ffn1_w_down=ffn1_w_down, mix_norm=mix_norm, w_in=w_in, lru_conv_w=lru_conv_w, lru_conv_b=lru_conv_b, lru_w_a=lru_w_a, lru_b_a=lru_b_a, lru_w_i=lru_w_i, lru_b_i=lru_b_i, lru_lambda=lru_lambda, sc_conv_w=sc_conv_w, lru_out_norm=lru_out_norm, sc_out_norm=sc_out_norm, w_out=w_out, ffn2_norm=ffn2_norm, ffn2_w_gate=ffn2_w_gate, ffn2_w_up=ffn2_w_up, ffn2_w_down=ffn2_w_down, final_norm=final_norm, loss_target=loss_target, m_ffn1_norm=m_ffn1_norm, m_ffn1_w_gate=m_ffn1_w_gate, m_ffn1_w_up=m_ffn1_w_up, m_ffn1_w_down=m_ffn1_w_down, m_mix_norm=m_mix_norm, m_w_in=m_w_in, m_lru_conv_w=m_lru_conv_w, m_lru_conv_b=m_lru_conv_b, m_lru_w_a=m_lru_w_a, m_lru_b_a=m_lru_b_a, m_lru_w_i=m_lru_w_i, m_lru_b_i=m_lru_b_i, m_lru_lambda=m_lru_lambda, m_sc_conv_w=m_sc_conv_w, m_lru_out_norm=m_lru_out_norm, m_sc_out_norm=m_sc_out_norm, m_w_out=m_w_out, m_ffn2_norm=m_ffn2_norm, m_ffn2_w_gate=m_ffn2_w_gate, m_ffn2_w_up=m_ffn2_w_up, m_ffn2_w_down=m_ffn2_w_down, m_final_norm=m_final_norm, v_ffn1_norm=v_ffn1_norm, v_ffn1_w_gate=v_ffn1_w_gate, v_ffn1_w_up=v_ffn1_w_up, v_ffn1_w_down=v_ffn1_w_down, v_mix_norm=v_mix_norm, v_w_in=v_w_in, v_lru_conv_w=v_lru_conv_w, v_lru_conv_b=v_lru_conv_b, v_lru_w_a=v_lru_w_a, v_lru_b_a=v_lru_b_a, v_lru_w_i=v_lru_w_i, v_lru_b_i=v_lru_b_i, v_lru_lambda=v_lru_lambda, v_sc_conv_w=v_sc_conv_w, v_lru_out_norm=v_lru_out_norm, v_sc_out_norm=v_sc_out_norm, v_w_out=v_w_out, v_ffn2_norm=v_ffn2_norm, v_ffn2_w_gate=v_ffn2_w_gate, v_ffn2_w_up=v_ffn2_w_up, v_ffn2_w_down=v_ffn2_w_down, v_final_norm=v_final_norm)
    weights = {n: given[n] for n in TWIN_WEIGHTS}
    shared = {n: given[n] for n in SHARED_INPUTS}
    per_example = {n: given[n] for n in ['x']}
    grad_fn = _jax.value_and_grad(_loss, argnums=(0, 1))

    def one_microbatch(ex, loss_target):
        ex = dict(ex)
        diff = ex.pop(TWIN_DIFF_INPUT)
        return grad_fn(weights, diff, {**shared, **ex}, loss_target)

    if N_MICROBATCH == 1:
        loss, (grad_w, grad_x) = one_microbatch(per_example, given["loss_target"])
    else:
        def body(carry, xs):
            loss_sum, grad_sum = carry
            l_k, (gw_k, gx_k) = one_microbatch(xs[0], xs[1])
            with _jax.named_scope("update"):
                return (loss_sum + l_k, _jax.tree.map(_jnp.add, grad_sum, gw_k)), gx_k

        init = (_jnp.zeros((), _jnp.float32), _jax.tree.map(_jnp.zeros_like, weights))
        (loss, grad_w), grad_x = _jax.lax.scan(body, init, (per_example, given["loss_target"]))
    with _jax.named_scope("update"):
        delta_w, new_m, new_v = {}, {}, {}
        for n in TWIN_WEIGHTS:
            delta_w[n], new_m[n], new_v[n] = _adamw(weights[n], grad_w[n], given["m_" + n], given["v_" + n])
    return (loss, grad_x, *[grad_w[n] for n in TWIN_WEIGHTS], *[delta_w[n] for n in TWIN_WEIGHTS],
            *[new_m[n] for n in TWIN_WEIGHTS], *[new_v[n] for n in TWIN_WEIGHTS])
```

```python
import functools
import math

import jax
import jax.numpy as jnp
from jax import lax
from jax.experimental import pallas as pl
from jax.experimental.pallas import tpu as pltpu

F32 = jnp.float32
BF16 = jnp.bfloat16
SDS = jax.ShapeDtypeStruct
MESH = pl.DeviceIdType.MESH

NORM_EPS = 1e-6
LRU_C = 8.0
N_DEV = 8
N_CHIP = 4
ADAM_LR, ADAM_B1, ADAM_B2, ADAM_EPS, ADAM_WD, ADAM_STEP = 0.001, 0.9, 0.999, 1e-08, 0.01, 10

NN = (((1,), (0,)), ((), ()))
NT = (((1,), (1,)), ((), ()))
TN = (((0,), (0,)), ((), ()))

SUBLANES = 8
BF16_ROWS = 16
LANES = 128
MIB = 1 << 20


def _dot(a, b, dims):
    return lax.dot_general(a, b, dims, preferred_element_type=F32)


def _blk(n, pref, align):
    if n <= pref:
        return n
    b = (pref // align) * align
    while b >= align:
        if n % b == 0:
            return b
        b -= align
    raise ValueError(f"no block of {n} aligned to {align} under {pref}")


def _cp(sem, vmem_mib):
    return pltpu.CompilerParams(dimension_semantics=sem, vmem_limit_bytes=vmem_mib * MIB)


ROW_CHUNK = 128


def _chunk_rows(c):
    return pl.ds(pl.multiple_of(c * ROW_CHUNK, ROW_CHUNK), ROW_CHUNK)


def _rstd(xv):
    return lax.rsqrt(jnp.mean(xv * xv, axis=-1, keepdims=True) + NORM_EPS)


def _rms_bwd(xv, g, dn):
    r = _rstd(xv)
    xr = xv * r
    gd = g * dn
    dx = r * (gd - xr * jnp.mean(gd * xr, axis=-1, keepdims=True))
    return dx, jnp.sum(dn * xr, axis=0, keepdims=True)


def _log1p(e):
    u = 1.0 + e
    return jnp.where(u == 1.0, e, jnp.log(u) * (e / (u - 1.0)))


def _expm1(v):
    u = jnp.exp(v)
    return jnp.where(u == 1.0, v, jnp.where(u == 0.0, -1.0, (u - 1.0) * (v / jnp.log(u))))


def _gelu_parts(g):
    k0 = math.sqrt(2.0 / math.pi)
    g2 = g * g
    t = jnp.tanh(k0 * (g + 0.044715 * g * g2))
    gel = 0.5 * g * (1.0 + t)
    gelp = 0.5 * (1.0 + t) + 0.5 * g * (1.0 - t * t) * (k0 * (1.0 + 3.0 * 0.044715 * g2))
    return gel, gelp


def _norm_proj(x, gain, w_list, out_dtypes, swiglu, name):
    T, D = x.shape
    N = w_list[0].shape[0]
    nw = len(w_list)
    bm = _blk(T, 1024, BF16_ROWS)
    bn = _blk(N, 512, LANES)

    def body(*refs):
        x_ref, g_ref = refs[:2]
        w_refs = refs[2:2 + nw]
        n_ref = refs[2 + nw]
        o_refs = refs[3 + nw:3 + 2 * nw]
        act_ref = refs[3 + 2 * nw] if swiglu else None
        n_sc = refs[-1]

        @pl.when(pl.program_id(1) == 0)
        def _():
            def chunk(c, _):
                r = _chunk_rows(c)
                xv = x_ref[r, :]
                nb = (xv * _rstd(xv) * g_ref[...]).astype(BF16)
                n_sc[r, :] = nb
                n_ref[r, :] = nb
                return 0
            lax.fori_loop(0, bm // ROW_CHUNK, chunk, 0)

        n = n_sc[...]
        outs = [_dot(n, w_ref[...], NT) for w_ref in w_refs]
        for o_ref, o in zip(o_refs, outs):
            o_ref[...] = o.astype(o_ref.dtype)
        if swiglu:
            hg, hu = outs
            act_ref[...] = (hg * jax.nn.sigmoid(hg) * hu).astype(BF16)

    row = pl.BlockSpec((bm, D), lambda i, j: (i, 0))
    tile = pl.BlockSpec((bm, bn), lambda i, j: (i, j))
    n_extra = 1 if swiglu else 0
    return pl.pallas_call(
        body, name=name, grid=(T // bm, N // bn),
        in_specs=[row, pl.BlockSpec((1, D), lambda i, j: (0, 0))] + [pl.BlockSpec((bn, D), lambda i, j: (j, 0))] * nw,
        out_specs=[row] + [tile] * (nw + n_extra),
        out_shape=[SDS((T, D), BF16)] + [SDS((T, N), dt) for dt in out_dtypes] + [SDS((T, N), BF16)] * n_extra,
        scratch_shapes=[pltpu.VMEM((bm, D), BF16)],
        compiler_params=_cp(("arbitrary", "arbitrary"), 52),
    )(x, gain, *w_list)


def _mm_res(a, b, x, scale, name):
    T, K = a.shape
    D = b.shape[1]
    bm = _blk(T, 1024, BF16_ROWS)
    bk = _blk(K, 512, LANES)
    nk = K // bk

    def body(a_ref, b_ref, x_ref, o_ref):
        k = pl.program_id(1)
        p = _dot(a_ref[...], b_ref[...], NN)

        @pl.when(k == 0)
        def _():
            o_ref[...] = p

        @pl.when(k > 0)
        def _():
            o_ref[...] += p

        @pl.when(k == nk - 1)
        def _():
            def chunk(c, _):
                r = _chunk_rows(c)
                o_ref[r, :] = x_ref[r, :] + scale * o_ref[r, :]
                return 0
            lax.fori_loop(0, bm // ROW_CHUNK, chunk, 0)

    row = pl.BlockSpec((bm, D), lambda i, k: (i, 0))
    return pl.pallas_call(
        body, name=name, grid=(T // bm, nk),
        in_specs=[pl.BlockSpec((bm, bk), lambda i, k: (i, k)), pl.BlockSpec((bk, D), lambda i, k: (k, 0)), row],
        out_specs=row, out_shape=SDS((T, D), F32),
        compiler_params=_cp(("arbitrary", "arbitrary"), 52),
    )(a, b, x)


def _mm_nt(a, b, name):
    T, K = a.shape
    N = b.shape[0]
    bm = _blk(T, 1024, BF16_ROWS)
    bn = _blk(N, 512, LANES)

    def body(a_ref, b_ref, o_ref):
        o_ref[...] = _dot(a_ref[...], b_ref[...], NT)

    return pl.pallas_call(
        body, name=name, grid=(T // bm, N // bn),
        in_specs=[pl.BlockSpec((bm, K), lambda i, j: (i, 0)), pl.BlockSpec((bn, K), lambda i, j: (j, 0))],
        out_specs=pl.BlockSpec((bm, bn), lambda i, j: (i, j)), out_shape=SDS((T, N), F32),
        compiler_params=_cp(("arbitrary", "arbitrary"), 40),
    )(a, b)


def _ffn_bwd_act(dfb, wd, hg, hu, name):
    T, D = dfb.shape
    F = wd.shape[0]
    bm = _blk(T, 1024, BF16_ROWS)
    bn = _blk(F, 512, LANES)

    def body(df_ref, wd_ref, hg_ref, hu_ref, dhg_ref, dhu_ref):
        dact = _dot(df_ref[...], wd_ref[...], NT)
        hgv = hg_ref[...].astype(F32)
        huv = hu_ref[...].astype(F32)
        s = jax.nn.sigmoid(hgv)
        dhu_ref[...] = (dact * (hgv * s)).astype(BF16)
        dhg_ref[...] = (dact * huv * (s * (1.0 + hgv * (1.0 - s)))).astype(BF16)

    tile = pl.BlockSpec((bm, bn), lambda i, j: (i, j))
    return pl.pallas_call(
        body, name=name, grid=(T // bm, F // bn),
        in_specs=[pl.BlockSpec((bm, D), lambda i, j: (i, 0)), pl.BlockSpec((bn, D), lambda i, j: (j, 0)), tile, tile],
        out_specs=[tile, tile], out_shape=[SDS((T, F), BF16)] * 2,
        compiler_params=_cp(("arbitrary", "arbitrary"), 40),
    )(dfb, wd, hg, hu)


def _dw_tn(a, b, bm_pref, name):
    T, M = a.shape
    N = b.shape[1]
    bm = _blk(M, bm_pref, LANES)
    tk = _blk(T, 512, BF16_ROWS)
    nk = T // tk

    def body(a_ref, b_ref, o_ref, acc):
        k = pl.program_id(1)
        p = _dot(a_ref[...], b_ref[...], TN)

        @pl.when(k == 0)
        def _():
            acc[...] = p

        @pl.when(k > 0)
        def _():
            acc[...] += p

        @pl.when(k == nk - 1)
        def _():
            o_ref[...] = acc[...].astype(BF16)

    return pl.pallas_call(
        body, name=name, grid=(M // bm, nk),
        in_specs=[pl.BlockSpec((tk, bm), lambda i, k: (k, i)), pl.BlockSpec((tk, N), lambda i, k: (k, 0))],
        out_specs=pl.BlockSpec((bm, N), lambda i, k: (i, 0)), out_shape=SDS((M, N), BF16),
        scratch_shapes=[pltpu.VMEM((bm, N), F32)],
        compiler_params=_cp(("arbitrary", "arbitrary"), 48),
    )(a, b)


def _mm_rmsbwd(pairs, x, gain, dx_in, bscale, name):
    T, D = x.shape
    K = pairs[0][0].shape[1]
    npair = len(pairs)
    bm = _blk(T, 512, BF16_ROWS)
    bk = _blk(K, 512, LANES)
    nk = K // bk

    def body(*refs):
        ab = refs[:2 * npair]
        x_ref, g_ref, dxin_ref, dx_ref, dxb_ref, dg_ref, acc = refs[2 * npair:]
        i = pl.program_id(0)
        k = pl.program_id(1)
        p = _dot(ab[0][...], ab[1][...], NN)
        for q in range(1, npair):
            p = p + _dot(ab[2 * q][...], ab[2 * q + 1][...], NN)

        @pl.when(k == 0)
        def _():
            acc[...] = p

        @pl.when(k > 0)
        def _():
            acc[...] += p

        @pl.when(k == nk - 1)
        def _():
            @pl.when(i == 0)
            def _():
                dg_ref[...] = jnp.zeros_like(dg_ref)

            def chunk(c, _):
                r = _chunk_rows(c)
                dx, dg = _rms_bwd(x_ref[r, :], g_ref[...], acc[r, :])
                dxo = dxin_ref[r, :] + dx
                dx_ref[r, :] = dxo
                dxb_ref[r, :] = (bscale * dxo).astype(BF16)
                dg_ref[...] += dg
                return 0
            lax.fori_loop(0, bm // ROW_CHUNK, chunk, 0)

    row = pl.BlockSpec((bm, D), lambda i, k: (i, 0))
    vec = pl.BlockSpec((1, D), lambda i, k: (0, 0))
    in_specs = []
    args = []
    for a, b in pairs:
        in_specs += [pl.BlockSpec((bm, bk), lambda i, k: (i, k)), pl.BlockSpec((bk, D), lambda i, k: (k, 0))]
        args += [a, b]
    return pl.pallas_call(
        body, name=name, grid=(T // bm, nk),
        in_specs=in_specs + [row, vec, row], out_specs=[row, row, vec],
        out_shape=[SDS((T, D), F32), SDS((T, D), BF16), SDS((1, D), F32)],
        scratch_shapes=[pltpu.VMEM((bm, D), F32)],
        compiler_params=_cp(("arbitrary", "arbitrary"), 52),
    )(*args, x, gain, dx_in)


def _loss_head(x3, gain, tgt, name):
    T, D = x3.shape
    bm = _blk(T, 256, BF16_ROWS)

    def body(x_ref, g_ref, t_ref, dx_ref, dxb_ref, dg_ref, loss_ref):
        i = pl.program_id(0)
        xv = x_ref[...]
        g = g_ref[...]
        out = xv * _rstd(xv) * g
        e = out - t_ref[...]
        part = 0.5 * jnp.sum(jnp.mean(e * e, axis=-1, keepdims=True), axis=0, keepdims=True)
        dx, dg = _rms_bwd(xv, g, e * (1.0 / D))
        dx_ref[...] = dx
        dxb_ref[...] = (0.5 * dx).astype(BF16)

        @pl.when(i == 0)
        def _():
            dg_ref[...] = dg
            loss_ref[...] = jnp.broadcast_to(part, loss_ref.shape)

        @pl.when(i > 0)
        def _():
            dg_ref[...] += dg
            loss_ref[...] += jnp.broadcast_to(part, loss_ref.shape)

    row = pl.BlockSpec((bm, D), lambda i: (i, 0))
    vec = pl.BlockSpec((1, D), lambda i: (0, 0))
    return pl.pallas_call(
        body, name=name, grid=(T // bm,),
        in_specs=[row, vec, row], out_specs=[row, row, vec, pl.BlockSpec((SUBLANES, LANES), lambda i: (0, 0))],
        out_shape=[SDS((T, D), F32), SDS((T, D), BF16), SDS((1, D), F32), SDS((SUBLANES, LANES), F32)],
        compiler_params=_cp(("arbitrary",), 40),
    )(x3, gain, tgt)


R_CW, R_CB, R_BA, R_BI, R_LAM, R_SW, R_GLO, R_GSO, SMALL_ROWS = 0, 4, 5, 6, 7, 8, 11, 12, 16


def _rows(g):
    return pl.ds(pl.multiple_of(g * SUBLANES, SUBLANES), SUBLANES)


def _shift_back(prev, cur, d):
    row = lax.broadcasted_iota(jnp.int32, cur.shape, 0)
    return pltpu.roll(jnp.where(row >= SUBLANES - d, prev, cur), d, 0)


def _shift_fwd(cur, nxt, d):
    row = lax.broadcasted_iota(jnp.int32, cur.shape, 0)
    return pltpu.roll(jnp.where(row < d, nxt, cur), SUBLANES - d, 0)


def _causal_conv(ext, g, taps_ref, ntap):
    prev = ext[_rows(g), :]
    cur = ext[_rows(g + 1), :]
    out = _shift_back(prev, cur, ntap - 1) * taps_ref[0:1, :]
    for k in range(1, ntap - 1):
        out = out + _shift_back(prev, cur, ntap - 1 - k) * taps_ref[k:k + 1, :]
    return out + cur * taps_ref[ntap - 1:ntap, :]


def _scan8(A, U, reverse):
    row = lax.broadcasted_iota(jnp.int32, A.shape, 0)
    for s in (1, 2, 4):
        if reverse:
            A_sh = pltpu.roll(A, SUBLANES - s, 0)
            U_sh = pltpu.roll(U, SUBLANES - s, 0)
            m = row < SUBLANES - s
        else:
            A_sh = pltpu.roll(A, s, 0)
            U_sh = pltpu.roll(U, s, 0)
            m = row >= s
        U = jnp.where(m, A * U_sh + U, U)
        A = jnp.where(m, A * A_sh, A)
    return A, U


def _gate_pre(xc_s, w_ref, out_s, H, hd):
    for h in range(H):
        cs = slice(h * hd, (h + 1) * hd)
        out_s[:, cs] = _dot(xc_s[:, cs].astype(BF16), w_ref[h], NN)


def _lru_coeffs(pa, pi, xc, ba, bi, sp):
    ra = jax.nn.sigmoid(pa + ba)
    ri = jax.nn.sigmoid(pi + bi)
    log_a = (-LRU_C * ra) * sp
    a = jnp.exp(log_a)
    mult = jnp.sqrt(-_expm1(2.0 * log_a))
    return ra, ri, a, mult


def _softplus_neg(lam):
    v = -lam
    return jnp.maximum(v, 0.0) + _log1p(jnp.exp(-jnp.abs(v)))


def _mix_fwd(z, cw, cb, wa, ba, wi, bi, lam, sw, glo, gso, name):
    T = z.shape[0]
    C = z.shape[1] // 5
    H = wa.shape[0]
    hd = C // H
    tb = _blk(T, 256, BF16_ROWS)
    ng = tb // SUBLANES
    HDR = SUBLANES

    def body(z_ref, cw_ref, cb_ref, wa_ref, ba_ref, wi_ref, bi_ref, lam_ref, sw_ref, glo_ref, gso_ref,
             y_ref, h_ref, xext, pext, xc_s, pa_s, pi_s, y_s, hcar):
        @pl.when(pl.program_id(0) == 0)
        def _():
            xext[0:HDR, :] = jnp.zeros((HDR, C), F32)
            pext[0:HDR, :] = jnp.zeros((HDR, C), F32)
            hcar[...] = jnp.zeros_like(hcar)

        def fill(g, _):
            r = _rows(g)
            re = _rows(g + 1)
            xext[re, :] = z_ref[r, 0:C]
            pext[re, :] = z_ref[r, 3 * C:4 * C] * z_ref[r, 4 * C:5 * C]
            return 0
        lax.fori_loop(0, ng, fill, 0)

        def conv(g, _):
            xc_s[_rows(g), :] = _causal_conv(xext, g, cw_ref, 4) + cb_ref[...]
            return 0
        lax.fori_loop(0, ng, conv, 0)

        _gate_pre(xc_s, wa_ref, pa_s, H, hd)
        _gate_pre(xc_s, wi_ref, pi_s, H, hd)
        sp = _softplus_neg(lam_ref[...])

        def group(g, hprev):
            r = _rows(g)
            xc = xc_s[r, :]
            _, ri, a, mult = _lru_coeffs(pa_s[r, :], pi_s[r, :], xc, ba_ref[...], bi_ref[...], sp)
            A, U = _scan8(a, mult * (ri * xc), reverse=False)
            hh = A * hprev + U
            h_ref[r, :] = hh
            gel, _ = _gelu_parts(z_ref[r, C:2 * C])
            y_lru = hh * gel
            y_s[r, 0:C] = y_lru * _rstd(y_lru) * glo_ref[...]
            y_sc = z_ref[r, 2 * C:3 * C] * _causal_conv(pext, g, sw_ref, 3)
            y_s[r, C:2 * C] = y_sc * _rstd(y_sc) * gso_ref[...]
            return jnp.broadcast_to(hh[SUBLANES - 1:SUBLANES, :], hh.shape)
        hcar[...] = lax.fori_loop(0, ng, group, hcar[...])

        xext[0:HDR, :] = xext[tb:tb + HDR, :]
        pext[0:HDR, :] = pext[tb:tb + HDR, :]

        def cast(g, _):
            r = pl.ds(pl.multiple_of(g * BF16_ROWS, BF16_ROWS), BF16_ROWS)
            y_ref[r, :] = y_s[r, :].astype(BF16)
            return 0
        lax.fori_loop(0, tb // BF16_ROWS, cast, 0)

    full = lambda shape: pl.BlockSpec(shape, lambda i: (0,) * len(shape))
    blk = lambda w: pl.BlockSpec((tb, w), lambda i: (i, 0))
    ext = pltpu.VMEM((tb + HDR, C), F32)
    tile = pltpu.VMEM((tb, C), F32)
    return pl.pallas_call(
        body, name=name, grid=(T // tb,),
        in_specs=[blk(5 * C), full((4, C)), full((1, C)), full((H, hd, hd)), full((1, C)), full((H, hd, hd)),
                  full((1, C)), full((1, C)), full((3, C)), full((1, C)), full((1, C))],
        out_specs=[blk(2 * C), blk(C)],
        out_shape=[SDS((T, 2 * C), BF16), SDS((T, C), F32)],
        scratch_shapes=[ext, ext, tile, tile, tile, pltpu.VMEM((tb, 2 * C), F32), pltpu.VMEM((SUBLANES, C), F32)],
        compiler_params=_cp(("arbitrary",), 40),
    )(z, cw, cb, wa, ba, wi, bi, lam, sw, glo, gso)


def _mix_bwd(z, h, dy, cw, cb, wa, ba, wi, bi, lam, sw, glo, gso, name):
    T = z.shape[0]
    C = z.shape[1] // 5
    H = wa.shape[0]
    hd = C // H
    tb = _blk(T, 256, BF16_ROWS)
    nb = T // tb
    ng = tb // SUBLANES
    HDR = SUBLANES
    N_ACC = 13

    def body(z_ref, zp_ref, h_ref, hp_ref, dy_ref, cw_ref, cb_ref, wa_ref, ba_ref, wi_ref, bi_ref, lam_ref,
             sw_ref, glo_ref, gso_ref, dz_ref, small_ref, dwa_ref, dwi_ref,
             xext, pext, hext, dqext, dxcext, bext, xc_s, pa_s, pi_s, a_s, m_s, ri_s, dh_s, dpa_s, dpi_s,
             dz_s, acc_s, bcar):
        i = pl.program_id(0)
        first_rows = i == nb - 1

        @pl.when(i == 0)
        def _():
            dqext[tb:tb + HDR, :] = jnp.zeros((HDR, C), F32)
            dxcext[tb:tb + HDR, :] = jnp.zeros((HDR, C), F32)
            bcar[...] = jnp.zeros_like(bcar)
            acc_s[...] = jnp.zeros_like(acc_s)
            dwa_ref[...] = jnp.zeros_like(dwa_ref)
            dwi_ref[...] = jnp.zeros_like(dwi_ref)

        zero = jnp.zeros((HDR, C), F32)
        xext[0:HDR, :] = jnp.where(first_rows, zero, zp_ref[:, 0:C])
        pext[0:HDR, :] = jnp.where(first_rows, zero, zp_ref[:, 3 * C:4 * C] * zp_ref[:, 4 * C:5 * C])
        hext[0:HDR, :] = jnp.where(first_rows, zero, hp_ref[...])

        def fill(g, _):
            r = _rows(g)
            re = _rows(g + 1)
            xext[re, :] = z_ref[r, 0:C]
            pext[re, :] = z_ref[r, 3 * C:4 * C] * z_ref[r, 4 * C:5 * C]
            hext[re, :] = h_ref[r, :]
            return 0
        lax.fori_loop(0, ng, fill, 0)

        def conv(g, _):
            xc_s[_rows(g), :] = _causal_conv(xext, g, cw_ref, 4) + cb_ref[...]
            return 0
        lax.fori_loop(0, ng, conv, 0)

        _gate_pre(xc_s, wa_ref, pa_s, H, hd)
        _gate_pre(xc_s, wi_ref, pi_s, H, hd)
        sp = _softplus_neg(lam_ref[...])
        dsp_dlam = -jax.nn.sigmoid(-lam_ref[...])

        def add_acc(k, v):
            acc_s[k] += v

        def p1(g, _):
            r = _rows(g)
            xc = xc_s[r, :]
            _, ri, a, mult = _lru_coeffs(pa_s[r, :], pi_s[r, :], xc, ba_ref[...], bi_ref[...], sp)
            a_s[r, :] = a
            m_s[r, :] = mult
            ri_s[r, :] = ri
            hh = h_ref[r, :]
            gel, gelp = _gelu_parts(z_ref[r, C:2 * C])
            y_lru = hh * gel
            dnl = dy_ref[r, 0:C]
            ylr = y_lru * _rstd(y_lru)
            gd = glo_ref[...] * dnl
            dy_lru = _rstd(y_lru) * (gd - ylr * jnp.mean(gd * ylr, axis=-1, keepdims=True))
            add_acc(R_GLO, dnl * ylr)
            dz_s[r, C:2 * C] = dy_lru * hh * gelp
            dh = dy_lru * gel
            dh_s[r, :] = dh

            q = _causal_conv(pext, g, sw_ref, 3)
            scb = z_ref[r, 2 * C:3 * C]
            y_sc = scb * q
            dns = dy_ref[r, C:2 * C]
            ysr = y_sc * _rstd(y_sc)
            gs = gso_ref[...] * dns
            dy_sc = _rstd(y_sc) * (gs - ysr * jnp.mean(gs * ysr, axis=-1, keepdims=True))
            add_acc(R_GSO, dns * ysr)
            dz_s[r, 2 * C:3 * C] = dy_sc * q
            dqext[r, :] = dy_sc * scb
            return 0
        lax.fori_loop(0, ng, p1, 0)

        bext[tb:tb + HDR, :] = bcar[...]

        def p2(j, carry):
            g = ng - 1 - j
            r = _rows(g)
            a = a_s[r, :]
            A, U = _scan8(a, a * dh_s[r, :], reverse=True)
            bb = A * carry + U
            bext[r, :] = bb
            return jnp.broadcast_to(bb[0:1, :], bb.shape)
        bcar[...] = lax.fori_loop(0, ng, p2, bcar[...])

        def p3(g, _):
            r = _rows(g)
            rn = _rows(g + 1)
            G = dh_s[r, :] + _shift_fwd(bext[r, :], bext[rn, :], 1)
            hm1 = _shift_back(hext[r, :], hext[rn, :], 1)
            a = a_s[r, :]
            mult = m_s[r, :]
            ri = ri_s[r, :]
            xc = xc_s[r, :]
            ra = jax.nn.sigmoid(pa_s[r, :] + ba_ref[...])
            dxcext[r, :] = G * mult * ri
            dri = G * mult * xc
            dmult = G * ri * xc
            dlog_a = (G * hm1) * a - dmult * (a * a) / mult
            add_acc(R_LAM, dlog_a * (-LRU_C * ra) * dsp_dlam)
            dpa = dlog_a * (-LRU_C * sp) * ra * (1.0 - ra)
            dpi = dri * ri * (1.0 - ri)
            add_acc(R_BA, dpa)
            add_acc(R_BI, dpi)
            dpa_s[r, :] = dpa
            dpi_s[r, :] = dpi
            return 0
        lax.fori_loop(0, ng, p3, 0)

        for hh_ in range(H):
            cs = slice(hh_ * hd, (hh_ + 1) * hd)
            dpa_b = dpa_s[:, cs].astype(BF16)
            dpi_b = dpi_s[:, cs].astype(BF16)
            xc_b = xc_s[:, cs].astype(BF16)
            dxcext[0:tb, cs] += _dot(dpa_b, wa_ref[hh_], NT) + _dot(dpi_b, wi_ref[hh_], NT)
            dwa_ref[hh_] += _dot(xc_b, dpa_b, TN)
            dwi_ref[hh_] += _dot(xc_b, dpi_b, TN)

        def p4(g, _):
            r = _rows(g)
            rn = _rows(g + 1)
            dxc = dxcext[r, :]
            dxc_n = dxcext[rn, :]
            x_p = xext[r, :]
            x_c = xext[rn, :]
            add_acc(R_CB, dxc)
            dlx = dxc * cw_ref[3:4, :]
            add_acc(R_CW + 3, dxc * x_c)
            for d in range(1, 4):
                dlx = dlx + _shift_fwd(dxc, dxc_n, d) * cw_ref[3 - d:4 - d, :]
                add_acc(R_CW + 3 - d, dxc * _shift_back(x_p, x_c, d))
            dz_s[r, 0:C] = dlx

            dq = dqext[r, :]
            dq_n = dqext[rn, :]
            p_p = pext[r, :]
            p_c = pext[rn, :]
            dp = dq * sw_ref[2:3, :]
            add_acc(R_SW + 2, dq * p_c)
            for d in range(1, 3):
                dp = dp + _shift_fwd(dq, dq_n, d) * sw_ref[2 - d:3 - d, :]
                add_acc(R_SW + 2 - d, dq * _shift_back(p_p, p_c, d))
            dz_s[r, 3 * C:4 * C] = dp * z_ref[r, 4 * C:5 * C]
            dz_s[r, 4 * C:5 * C] = dp * z_ref[r, 3 * C:4 * C]
            return 0
        lax.fori_loop(0, ng, p4, 0)

        dqext[tb:tb + HDR, :] = dqext[0:HDR, :]
        dxcext[tb:tb + HDR, :] = dxcext[0:HDR, :]

        def cast(g, _):
            r = pl.ds(pl.multiple_of(g * BF16_ROWS, BF16_ROWS), BF16_ROWS)
            dz_ref[r, :] = dz_s[r, :].astype(BF16)
            return 0
        lax.fori_loop(0, tb // BF16_ROWS, cast, 0)

        @pl.when(i == nb - 1)
        def _():
            small_ref[...] = jnp.zeros_like(small_ref)
            for k in range(N_ACC):
                small_ref[k:k + 1, :] = jnp.sum(acc_s[k], axis=0, keepdims=True)

    tpg = tb // SUBLANES
    full = lambda shape: pl.BlockSpec(shape, lambda i: (0,) * len(shape))
    blk = lambda w: pl.BlockSpec((tb, w), lambda i: (nb - 1 - i, 0))
    prev = lambda w: pl.BlockSpec((SUBLANES, w), lambda i: (jnp.maximum((nb - 1 - i) * tpg - 1, 0), 0))
    ext = pltpu.VMEM((tb + HDR, C), F32)
    tile = pltpu.VMEM((tb, C), F32)
    return pl.pallas_call(
        body, name=name, grid=(nb,),
        in_specs=[blk(5 * C), prev(5 * C), blk(C), prev(C), blk(2 * C), full((4, C)), full((1, C)), full((H, hd, hd)),
                  full((1, C)), full((H, hd, hd)), full((1, C)), full((1, C)), full((3, C)), full((1, C)), full((1, C))],
        out_specs=[blk(5 * C), full((SMALL_ROWS, C)), full((H, hd, hd)), full((H, hd, hd))],
        out_shape=[SDS((T, 5 * C), BF16), SDS((SMALL_ROWS, C), F32), SDS((H, hd, hd), F32), SDS((H, hd, hd), F32)],
        scratch_shapes=[ext] * 6 + [tile] * 9 + [pltpu.VMEM((tb, 5 * C), F32), pltpu.VMEM((N_ACC, SUBLANES, C), F32),
                                                pltpu.VMEM((SUBLANES, C), F32)],
        compiler_params=_cp(("arbitrary",), 56),
    )(z, z, h, h, dy, cw, cb, wa, ba, wi, bi, lam, sw, glo, gso)


def _add_slabs(terms, out_dtype, name):
    R, Ccols = terms[0].shape
    br = _blk(R, 512, BF16_ROWS)
    n = len(terms)

    def body(*refs):
        s = refs[0][...].astype(F32)
        for t_ref in refs[1:n]:
            s = s + t_ref[...].astype(F32)
        refs[n][...] = s.astype(out_dtype)

    spec = pl.BlockSpec((br, Ccols), lambda i: (i, 0))
    return pl.pallas_call(
        body, name=name, grid=(R // br,), in_specs=[spec] * n, out_specs=spec, out_shape=SDS((R, Ccols), out_dtype),
        compiler_params=_cp(("arbitrary",), 40),
    )(*terms)


def _final_grad(sb, lb, chip, name):
    _, R, Ccols = sb.shape
    br = _blk(R, 512, BF16_ROWS)

    def body(chip_ref, sb_ref, l0, l1, l2, o_ref):
        s = sb_ref[0].astype(F32)
        for t_ref in (l0, l1, l2):
            s = s + t_ref[0].astype(F32)
        o_ref[...] = s

    lspec = lambda k: pl.BlockSpec((1, br, Ccols), lambda i, c: (k, i, 0))
    return pl.pallas_call(
        body, name=name,
        grid_spec=pltpu.PrefetchScalarGridSpec(
            num_scalar_prefetch=1, grid=(R // br,),
            in_specs=[pl.BlockSpec((1, br, Ccols), lambda i, c: (c[0], i, 0)), lspec(0), lspec(1), lspec(2)],
            out_specs=pl.BlockSpec((br, Ccols), lambda i, c: (i, 0))),
        out_shape=SDS((R, Ccols), F32),
        compiler_params=_cp(("arbitrary",), 40),
    )(chip, sb, lb, lb, lb)


def _adamw(w, g, m, v, name):
    R, Ccols = w.shape
    br = _blk(R, 256, SUBLANES)
    c1 = 1.0 - ADAM_B1 ** ADAM_STEP
    c2 = 1.0 - ADAM_B2 ** ADAM_STEP

    def body(w_ref, g_ref, m_ref, v_ref, d_ref, nm_ref, nv_ref):
        gv = g_ref[...]
        nm = ADAM_B1 * m_ref[...] + (1.0 - ADAM_B1) * gv
        nv = ADAM_B2 * v_ref[...] + (1.0 - ADAM_B2) * (gv * gv)
        nm_ref[...] = nm
        nv_ref[...] = nv
        d_ref[...] = -ADAM_LR * ((nm / c1) / (jnp.sqrt(nv / c2) + ADAM_EPS) + ADAM_WD * w_ref[...])

    spec = pl.BlockSpec((br, Ccols), lambda i: (i, 0))
    return pl.pallas_call(
        body, name=name, grid=(R // br,), in_specs=[spec] * 4, out_specs=[spec] * 3,
        out_shape=[SDS((R, Ccols), F32)] * 3, compiler_params=_cp(("arbitrary",), 40),
    )(w, g, m, v)


HBM_SPEC = pl.BlockSpec(memory_space=pltpu.HBM)


def _place():
    return lax.axis_index("x"), lax.axis_index("y"), lax.axis_index("c")


def _allgather(slab, pieces, name):
    R, W = slab.shape
    n = len(pieces)
    assert sum(rows for _, rows in pieces) == R

    def body(slab_ref, *refs):
        outs = refs[:n]
        send_sems, recv_sems, local_sems = refs[n:]
        x, y, c = _place()
        me, sibling = (x, y, c), (x, y, 1 - c)
        chips = [(1 - x, y), (x, 1 - y), (1 - x, 1 - y)]

        def dst_rows(p, origin):
            rows = pieces[p][1]
            start = (4 * origin[0] + 2 * origin[1] + origin[2]) * rows
            return outs[p].at[pl.ds(start, rows), :]

        def copies(k, origin, to, from_slab):
            out = []
            for p, (off, rows) in enumerate(pieces):
                dst = dst_rows(p, origin)
                src = slab_ref.at[pl.ds(off, rows), :] if from_slab else dst
                out.append(pltpu.make_async_remote_copy(
                    src_ref=src, dst_ref=dst, send_sem=send_sems.at[k, p], recv_sem=recv_sems.at[k, p],
                    device_id=to, device_id_type=MESH))
            return out

        mine = [pltpu.make_async_copy(slab_ref.at[pl.ds(off, rows), :], dst_rows(p, me), local_sems.at[p])
                for p, (off, rows) in enumerate(pieces)]
        for cp in mine:
            cp.start()
        first = copies(0, me, sibling, True)
        for j, chip in enumerate(chips):
            first += copies(1 + j, me, (*chip, c), True)
        for cp in first:
            cp.start()
        passed = []
        for j, chip in enumerate(chips):
            for cp in copies(1 + j, (*chip, c), me, False):
                cp.wait_recv()
            fwd = copies(4 + j, (*chip, c), sibling, False)
            for cp in fwd:
                cp.start()
            passed += fwd
        for cp in copies(0, sibling, me, False):
            cp.wait_recv()
        for j, chip in enumerate(chips):
            for cp in copies(4 + j, (*chip, 1 - c), me, False):
                cp.wait_recv()
        for cp in first + passed:
            cp.wait_send()
        for cp in mine:
            cp.wait()

    return pl.pallas_call(
        body, name=name,
        in_specs=[HBM_SPEC], out_specs=[HBM_SPEC] * n,
        out_shape=[SDS((N_DEV * rows, W), slab.dtype) for _, rows in pieces],
        scratch_shapes=[pltpu.SemaphoreType.DMA((7, n)), pltpu.SemaphoreType.DMA((7, n)), pltpu.SemaphoreType.DMA((n,))],
    )(slab)


def _rs_sibling(grads, pieces, name):
    W = grads[0].shape[1]
    R = sum(rows for _, rows in pieces)
    n = len(pieces)
    dt = grads[0].dtype

    def body(*refs):
        g_refs = refs[:n]
        own_ref, land_ref, send_sems, recv_sems, local_sems = refs[n:]
        x, y, c = _place()
        sibling = (x, y, 1 - c)
        local, remote = [], []
        for q in range(N_CHIP):
            for p, (off, rows) in enumerate(pieces):
                mine = g_refs[p].at[pl.ds((2 * q + c) * rows, rows), :]
                theirs = g_refs[p].at[pl.ds((2 * q + 1 - c) * rows, rows), :]
                local.append(pltpu.make_async_copy(mine, own_ref.at[q, pl.ds(off, rows), :], local_sems.at[q, p]))
                remote.append(pltpu.make_async_remote_copy(
                    src_ref=theirs, dst_ref=land_ref.at[q, pl.ds(off, rows), :],
                    send_sem=send_sems.at[q, p], recv_sem=recv_sems.at[q, p], device_id=sibling, device_id_type=MESH))
        for cp in remote + local:
            cp.start()
        for cp in remote:
            cp.wait_recv()
        for cp in remote:
            cp.wait_send()
        for cp in local:
            cp.wait()

    return pl.pallas_call(
        body, name=name,
        in_specs=[HBM_SPEC] * n, out_specs=[HBM_SPEC] * 2,
        out_shape=[SDS((N_CHIP, R, W), dt)] * 2,
        scratch_shapes=[pltpu.SemaphoreType.DMA((N_CHIP, n))] * 3,
    )(*grads)


def _rs_chips(sb, name):
    _, R, W = sb.shape

    def body(sb_ref, land_ref, send_sems, recv_sems):
        x, y, c = _place()
        chips = [(1 - x, y), (x, 1 - y), (1 - x, 1 - y)]
        cps = [pltpu.make_async_remote_copy(
            src_ref=sb_ref.at[2 * chip[0] + chip[1]], dst_ref=land_ref.at[j],
            send_sem=send_sems.at[j], recv_sem=recv_sems.at[j], device_id=(*chip, c), device_id_type=MESH)
            for j, chip in enumerate(chips)]
        for cp in cps:
            cp.start()
        for cp in cps:
            cp.wait_recv()
        for cp in cps:
            cp.wait_send()

    return pl.pallas_call(
        body, name=name, in_specs=[HBM_SPEC], out_specs=HBM_SPEC, out_shape=SDS((3, R, W), sb.dtype),
        scratch_shapes=[pltpu.SemaphoreType.DMA((3,))] * 2,
    )(sb)


SMALL_NAMES = ("ffn1_norm", "mix_norm", "ffn2_norm", "final_norm", "lru_conv_w", "lru_conv_b", "lru_w_a", "lru_b_a",
               "lru_w_i", "lru_b_i", "lru_lambda", "sc_conv_w", "lru_out_norm", "sc_out_norm")
WEIGHT_NAMES = ("ffn1_norm", "ffn1_w_gate", "ffn1_w_up", "ffn1_w_down", "mix_norm", "w_in", "lru_conv_w", "lru_conv_b",
                "lru_w_a", "lru_b_a", "lru_w_i", "lru_b_i", "lru_lambda", "sc_conv_w", "lru_out_norm", "sc_out_norm",
                "w_out", "ffn2_norm", "ffn2_w_gate", "ffn2_w_up", "ffn2_w_down", "final_norm")
BIG = (("ffn1_w_gate", True), ("ffn1_w_up", True), ("ffn1_w_down", False), ("ffn2_w_gate", True), ("ffn2_w_up", True),
       ("ffn2_w_down", False), ("w_in", True), ("w_out", False))


SLAB_ROW_ALIGN = 256


def _pack_rows(parts, width):
    rows, counts = [], []
    for p in parts:
        flat = p.reshape(-1)
        nr = -(-flat.shape[0] // width)
        nr = -(-nr // SUBLANES) * SUBLANES
        rows.append(jnp.pad(flat, (0, nr * width - flat.shape[0])).reshape(nr, width))
        counts.append(nr)
    total = sum(counts)
    pad = -(-total // SLAB_ROW_ALIGN) * SLAB_ROW_ALIGN - total
    if pad:
        rows.append(jnp.zeros((pad, width), rows[0].dtype))
    return jnp.concatenate(rows, axis=0), counts


def _unpack_rows(slab, counts, shapes):
    out, r = [], 0
    for nr, shape in zip(counts, shapes):
        size = math.prod(shape)
        out.append(slab[r:r + nr].reshape(-1)[:size].reshape(shape))
        r += nr
    return out


def kernel(x, ffn1_norm, ffn1_w_gate, ffn1_w_up, ffn1_w_down, mix_norm, w_in, lru_conv_w, lru_conv_b, lru_w_a, lru_b_a, lru_w_i, lru_b_i, lru_lambda, sc_conv_w, lru_out_norm, sc_out_norm, w_out, ffn2_norm, ffn2_w_gate, ffn2_w_up, ffn2_w_down, final_norm, loss_target, m_ffn1_norm, m_ffn1_w_gate, m_ffn1_w_up, m_ffn1_w_down, m_mix_norm, m_w_in, m_lru_conv_w, m_lru_conv_b, m_lru_w_a, m_lru_b_a, m_lru_w_i, m_lru_b_i, m_lru_lambda, m_sc_conv_w, m_lru_out_norm, m_sc_out_norm, m_w_out, m_ffn2_norm, m_ffn2_w_gate, m_ffn2_w_up, m_ffn2_w_down, m_final_norm, v_ffn1_norm, v_ffn1_w_gate, v_ffn1_w_up, v_ffn1_w_down, v_mix_norm, v_w_in, v_lru_conv_w, v_lru_conv_b, v_lru_w_a, v_lru_b_a, v_lru_w_i, v_lru_b_i, v_lru_lambda, v_sc_conv_w, v_lru_out_norm, v_sc_out_norm, v_w_out, v_ffn2_norm, v_ffn2_w_gate, v_ffn2_w_up, v_ffn2_w_down, v_final_norm):
    a = dict(locals())
    w = {n: a[n] for n in WEIGHT_NAMES}
    m = {n: a["m_" + n] for n in WEIGHT_NAMES}
    v = {n: a["v_" + n] for n in WEIGHT_NAMES}
    ax, ay, ac = _place()
    dev = 4 * ax + 2 * ay + ac
    chip = (2 * ax + ay).astype(jnp.int32).reshape(1)

    x0 = x[0]
    tgt = loss_target[0]
    T, D = x0.shape
    C = D // 2
    H, hd = lru_w_a.shape[1], lru_w_a.shape[2]
    CL = lru_conv_w.shape[2]

    shards = []
    for name, transposed in BIG:
        s = w[name][0]
        shards.append((s.T if transposed else s).astype(BF16))
    taps = jnp.concatenate([lru_conv_w[0], sc_conv_w[0], jnp.zeros((1, CL), F32)], axis=0)
    taps_row = lax.bitcast_convert_type(taps, BF16).reshape(1, -1)
    taps_blk = jnp.pad(taps_row, ((0, BF16_ROWS - 1), (0, D - taps_row.shape[1])))
    slab = jnp.concatenate(shards + [taps_blk], axis=0)
    pieces, off = [], 0
    for s in shards + [taps_blk]:
        pieces.append((off, s.shape[0]))
        off += s.shape[0]
    gathered = _allgather(slab, pieces, "allgather_weights")
    wg1, wu1, wd1, wg2, wu2, wd2, win, wout, taps_all = gathered
    taps_all = taps_all.reshape(N_DEV, BF16_ROWS, D)[:, 0, :2 * SUBLANES * CL].reshape(N_DEV, SUBLANES, CL, 2)
    taps_all = lax.bitcast_convert_type(taps_all, F32)
    taps_all = taps_all.transpose(1, 0, 2).reshape(SUBLANES, N_DEV * CL)
    cw, sw = taps_all[0:4], taps_all[4:7]

    g1, gm, g3 = ffn1_norm, mix_norm, ffn2_norm
    gf = final_norm.reshape(1, D)
    cb = lru_conv_b
    wa, wi = lru_w_a[0].astype(BF16), lru_w_i[0].astype(BF16)
    ba, bi = lru_b_a.reshape(1, C), lru_b_i.reshape(1, C)
    lam, glo, gso = lru_lambda, lru_out_norm, sc_out_norm

    n1, hg1, hu1, act1 = _norm_proj(x0, g1, [wg1, wu1], [BF16, BF16], True, "ffn1_up")
    x1 = _mm_res(act1, wd1, x0, 0.5, "ffn1_down")
    n2, z = _norm_proj(x1, gm, [win], [F32], False, "in_proj")
    y, h = _mix_fwd(z, cw, cb, wa, ba, wi, bi, lam, sw, glo, gso, "mix_fwd")
    x2 = _mm_res(y, wout, x1, 1.0, "out_proj")
    n3, hg2, hu2, act2 = _norm_proj(x2, g3, [wg2, wu2], [BF16, BF16], True, "ffn2_up")
    x3 = _mm_res(act2, wd2, x2, 0.5, "ffn2_down")
    dx3, df2, d_gf, loss_blk = _loss_head(x3, gf, tgt, "loss_head")

    F = wd1.shape[0]
    bm_f = F // 4 if (F // 4) % LANES == 0 else 512
    dhg2, dhu2 = _ffn_bwd_act(df2, wd2, hg2, hu2, "ffn2_bwd_act")
    d_wd2 = _dw_tn(act2, df2, bm_f, "ffn2_dw_down")
    d_wg2 = _dw_tn(dhg2, n3, bm_f, "ffn2_dw_gate")
    d_wu2 = _dw_tn(dhu2, n3, bm_f, "ffn2_dw_up")
    dx2, dx2b, d_g3 = _mm_rmsbwd([(dhg2, wg2), (dhu2, wu2)], x2, g3, dx3, 1.0, "ffn2_bwd_in")
    dy = _mm_nt(dx2b, wout, "out_proj_bwd")
    d_wout = _dw_tn(y, dx2b, 1024, "out_proj_dw")
    dz, small, d_wa, d_wi = _mix_bwd(z, h, dy, cw, cb, wa, ba, wi, bi, lam, sw, glo, gso, "mix_bwd")
    d_win = _dw_tn(dz, n2, 1280, "in_proj_dw")
    dx1, df1, d_gm = _mm_rmsbwd([(dz, win)], x1, gm, dx2, 0.5, "in_proj_bwd")
    dhg1, dhu1 = _ffn_bwd_act(df1, wd1, hg1, hu1, "ffn1_bwd_act")
    d_wd1 = _dw_tn(act1, df1, bm_f, "ffn1_dw_down")
    d_wg1 = _dw_tn(dhg1, n1, bm_f, "ffn1_dw_gate")
    d_wu1 = _dw_tn(dhu1, n1, bm_f, "ffn1_dw_up")
    dx0, _, d_g1 = _mm_rmsbwd([(dhg1, wg1), (dhu1, wu1)], x0, g1, dx1, 1.0, "ffn1_bwd_in")

    big_grads = [d_wg1, d_wu1, d_wd1, d_wg2, d_wu2, d_wd2, d_win, d_wout]
    big_pieces = pieces[:len(BIG)]
    R = sum(rows for _, rows in big_pieces)
    own, landed = _rs_sibling(big_grads, big_pieces, "rs_sibling")
    sb = _add_slabs([own.reshape(N_CHIP * R, D), landed.reshape(N_CHIP * R, D)], BF16, "rs_add_sibling")
    sb = sb.reshape(N_CHIP, R, D)
    lb = _rs_chips(sb, "rs_chips")
    gsum = _final_grad(sb, lb, chip, "rs_final_sum")

    small_parts = [d_g1, d_gm, d_g3, d_gf, small[R_CW:R_CW + 4], small[R_CB], d_wa, small[R_BA], d_wi, small[R_BI],
                   small[R_LAM], small[R_SW:R_SW + 3], small[R_GLO], small[R_GSO]]
    sslab, counts = _pack_rows(small_parts, LANES)
    RS = sslab.shape[0]
    (sg,) = _allgather(sslab, [(0, RS)], "allgather_small_grads")
    ssum = _add_slabs([sg[j * RS:(j + 1) * RS] for j in range(N_DEV)], F32, "small_grads_sum")
    full_shapes = [(1, D), (1, D), (1, D), (D,), (1, 4, C), (1, C), (1, H, hd, hd), (1, H, hd), (1, H, hd, hd), (1, H, hd),
                   (1, C), (1, 3, C), (1, C), (1, C)]
    small_full = dict(zip(SMALL_NAMES, _unpack_rows(ssum, counts, full_shapes)))

    grads = {}
    for (name, transposed), (off, rows) in zip(BIG, big_pieces):
        gblk = gsum[off:off + rows]
        grads[name] = (gblk.T if transposed else gblk)[None]
    for name in SMALL_NAMES:
        gfull = small_full[name]
        if name in ("lru_conv_w", "sc_conv_w"):
            gfull = lax.dynamic_slice_in_dim(gfull, dev * CL, CL, axis=2)
        grads[name] = gfull

    delta, new_m, new_v = {}, {}, {}
    for name, _ in BIG:
        shp = w[name].shape
        d_, m_, v_ = _adamw(w[name][0], grads[name][0], m[name][0], v[name][0], "adamw_" + name)
        delta[name], new_m[name], new_v[name] = d_.reshape(shp), m_.reshape(shp), v_.reshape(shp)
    packs = [_pack_rows([t[n_] for n_ in SMALL_NAMES], LANES) for t in (w, grads, m, v)]
    sd, sm, sv = _adamw(packs[0][0], packs[1][0], packs[2][0], packs[3][0], "adamw_small")
    shapes = [w[n_].shape for n_ in SMALL_NAMES]
    for tgt_dict, slab_ in ((delta, sd), (new_m, sm), (new_v, sv)):
        for n_, val in zip(SMALL_NAMES, _unpack_rows(slab_, packs[0][1], shapes)):
            tgt_dict[n_] = val

    loss = lax.psum(loss_blk[0, 0], ("x", "y", "c"))
    return (loss, dx0[None], *[grads[n_] for n_ in WEIGHT_NAMES], *[delta[n_] for n_ in WEIGHT_NAMES],
            *[new_m[n_] for n_ in WEIGHT_NAMES], *[new_v[n_] for n_ in WEIGHT_NAMES])
```

```python
import functools
import math

import jax
import jax.numpy as jnp
from jax import lax
from jax.experimental import pallas as pl
from jax.experimental.pallas import tpu as pltpu

F32 = jnp.float32
BF16 = jnp.bfloat16
SDS = jax.ShapeDtypeStruct
MESH = pl.DeviceIdType.MESH

NORM_EPS = 1e-6
LRU_C = 8.0
N_DEV = 8
N_CHIP = 4
ADAM_LR, ADAM_B1, ADAM_B2, ADAM_EPS, ADAM_WD, ADAM_STEP = 0.001, 0.9, 0.999, 1e-08, 0.01, 10

NN = (((1,), (0,)), ((), ()))
NT = (((1,), (1,)), ((), ()))
TN = (((0,), (0,)), ((), ()))

SUBLANES = 8
BF16_ROWS = 16
LANES = 128
MIB = 1 << 20


def _dot(a, b, dims):
    return lax.dot_general(a, b, dims, preferred_element_type=F32)


def _blk(n, pref, align):
    if n <= pref:
        return n
    b = (pref // align) * align
    while b >= align:
        if n % b == 0:
            return b
        b -= align
    raise ValueError(f"no block of {n} aligned to {align} under {pref}")


def _cp(sem, vmem_mib):
    return pltpu.CompilerParams(dimension_semantics=sem, vmem_limit_bytes=vmem_mib * MIB)


ROW_CHUNK = 128


def _chunk_rows(c):
    return pl.ds(pl.multiple_of(c * ROW_CHUNK, ROW_CHUNK), ROW_CHUNK)


def _rstd(xv):
    return lax.rsqrt(jnp.mean(xv * xv, axis=-1, keepdims=True) + NORM_EPS)


def _rms_bwd(xv, g, dn):
    r = _rstd(xv)
    xr = xv * r
    gd = g * dn
    dx = r * (gd - xr * jnp.mean(gd * xr, axis=-1, keepdims=True))
    return dx, jnp.sum(dn * xr, axis=0, keepdims=True)


def _log1p(e):
    u = 1.0 + e
    return jnp.where(u == 1.0, e, jnp.log(u) * (e / (u - 1.0)))


def _expm1(v):
    u = jnp.exp(v)
    return jnp.where(u == 1.0, v, jnp.where(u == 0.0, -1.0, (u - 1.0) * (v / jnp.log(u))))


def _gelu_parts(g):
    k0 = math.sqrt(2.0 / math.pi)
    g2 = g * g
    t = jnp.tanh(k0 * (g + 0.044715 * g * g2))
    gel = 0.5 * g * (1.0 + t)
    gelp = 0.5 * (1.0 + t) + 0.5 * g * (1.0 - t * t) * (k0 * (1.0 + 3.0 * 0.044715 * g2))
    return gel, gelp


def _norm_proj(x, gain, w_list, out_dtypes, swiglu, name):
    T, D = x.shape
    N = w_list[0].shape[0]
    nw = len(w_list)
    bm = _blk(T, 1024, BF16_ROWS)
    bn = _blk(N, 512, LANES)

    def body(*refs):
        x_ref, g_ref = refs[:2]
        w_refs = refs[2:2 + nw]
        n_ref = refs[2 + nw]
        o_refs = refs[3 + nw:3 + 2 * nw]
        act_ref = refs[3 + 2 * nw] if swiglu else None
        n_sc = refs[-1]

        @pl.when(pl.program_id(1) == 0)
        def _():
            def chunk(c, _):
                r = _chunk_rows(c)
                xv = x_ref[r, :]
                nb = (xv * _rstd(xv) * g_ref[...]).astype(BF16)
                n_sc[r, :] = nb
                n_ref[r, :] = nb
                return 0
            lax.fori_loop(0, bm // ROW_CHUNK, chunk, 0)

        n = n_sc[...]
        outs = [_dot(n, w_ref[...], NT) for w_ref in w_refs]
        for o_ref, o in zip(o_refs, outs):
            o_ref[...] = o.astype(o_ref.dtype)
        if swiglu:
            hg, hu = outs
            act_ref[...] = (hg * jax.nn.sigmoid(hg) * hu).astype(BF16)

    row = pl.BlockSpec((bm, D), lambda i, j: (i, 0))
    tile = pl.BlockSpec((bm, bn), lambda i, j: (i, j))
    n_extra = 1 if swiglu else 0
    return pl.pallas_call(
        body, name=name, grid=(T // bm, N // bn),
        in_specs=[row, pl.BlockSpec((1, D), lambda i, j: (0, 0))] + [pl.BlockSpec((bn, D), lambda i, j: (j, 0))] * nw,
        out_specs=[row] + [tile] * (nw + n_extra),
        out_shape=[SDS((T, D), BF16)] + [SDS((T, N), dt) for dt in out_dtypes] + [SDS((T, N), BF16)] * n_extra,
        scratch_shapes=[pltpu.VMEM((bm, D), BF16)],
        compiler_params=_cp(("arbitrary", "arbitrary"), 52),
    )(x, gain, *w_list)


def _mm_res(a, b, x, scale, name):
    T, K = a.shape
    D = b.shape[1]
    bm = _blk(T, 1024, BF16_ROWS)
    bk = _blk(K, 512, LANES)
    nk = K // bk

    def body(a_ref, b_ref, x_ref, o_ref):
        k = pl.program_id(1)
        p = _dot(a_ref[...], b_ref[...], NN)

        @pl.when(k == 0)
        def _():
            o_ref[...] = p

        @pl.when(k > 0)
        def _():
            o_ref[...] += p

        @pl.when(k == nk - 1)
        def _():
            def chunk(c, _):
                r = _chunk_rows(c)
                o_ref[r, :] = x_ref[r, :] + scale * o_ref[r, :]
                return 0
            lax.fori_loop(0, bm // ROW_CHUNK, chunk, 0)

    row = pl.BlockSpec((bm, D), lambda i, k: (i, 0))
    return pl.pallas_call(
        body, name=name, grid=(T // bm, nk),
        in_specs=[pl.BlockSpec((bm, bk), lambda i, k: (i, k)), pl.BlockSpec((bk, D), lambda i, k: (k, 0)), row],
        out_specs=row, out_shape=SDS((T, D), F32),
        compiler_params=_cp(("arbitrary", "arbitrary"), 52),
    )(a, b, x)


def _mm_nt(a, b, name):
    T, K = a.shape
    N = b.shape[0]
    bm = _blk(T, 1024, BF16_ROWS)
    bn = _blk(N, 512, LANES)

    def body(a_ref, b_ref, o_ref):
        o_ref[...] = _dot(a_ref[...], b_ref[...], NT)

    return pl.pallas_call(
        body, name=name, grid=(T // bm, N // bn),
        in_specs=[pl.BlockSpec((bm, K), lambda i, j: (i, 0)), pl.BlockSpec((bn, K), lambda i, j: (j, 0))],
        out_specs=pl.BlockSpec((bm, bn), lambda i, j: (i, j)), out_shape=SDS((T, N), F32),
        compiler_params=_cp(("arbitrary", "arbitrary"), 40),
    )(a, b)


def _ffn_bwd_act(dfb, wd, hg, hu, name):
    T, D = dfb.shape
    F = wd.shape[0]
    bm = _blk(T, 1024, BF16_ROWS)
    bn = _blk(F, 512, LANES)

    def body(df_ref, wd_ref, hg_ref, hu_ref, dhg_ref, dhu_ref):
        dact = _dot(df_ref[...], wd_ref[...], NT)
        hgv = hg_ref[...].astype(F32)
        huv = hu_ref[...].astype(F32)
        s = jax.nn.sigmoid(hgv)
        dhu_ref[...] = (dact * (hgv * s)).astype(BF16)
        dhg_ref[...] = (dact * huv * (s * (1.0 + hgv * (1.0 - s)))).astype(BF16)

    tile = pl.BlockSpec((bm, bn), lambda i, j: (i, j))
    return pl.pallas_call(
        body, name=name, grid=(T // bm, F // bn),
        in_specs=[pl.BlockSpec((bm, D), lambda i, j: (i, 0)), pl.BlockSpec((bn, D), lambda i, j: (j, 0)), tile, tile],
        out_specs=[tile, tile], out_shape=[SDS((T, F), BF16)] * 2,
        compiler_params=_cp(("arbitrary", "arbitrary"), 40),
    )(dfb, wd, hg, hu)


def _dw_tn(a, b, bm_pref, name):
    T, M = a.shape
    N = b.shape[1]
    bm = _blk(M, bm_pref, LANES)
    tk = _blk(T, 512, BF16_ROWS)
    nk = T // tk

    def body(a_ref, b_ref, o_ref, acc):
        k = pl.program_id(1)
        p = _dot(a_ref[...], b_ref[...], TN)

        @pl.when(k == 0)
        def _():
            acc[...] = p

        @pl.when(k > 0)
        def _():
            acc[...] += p

        @pl.when(k == nk - 1)
        def _():
            o_ref[...] = acc[...].astype(BF16)

    return pl.pallas_call(
        body, name=name, grid=(M // bm, nk),
        in_specs=[pl.BlockSpec((tk, bm), lambda i, k: (k, i)), pl.BlockSpec((tk, N), lambda i, k: (k, 0))],
        out_specs=pl.BlockSpec((bm, N), lambda i, k: (i, 0)), out_shape=SDS((M, N), BF16),
        scratch_shapes=[pltpu.VMEM((bm, N), F32)],
        compiler_params=_cp(("arbitrary", "arbitrary"), 48),
    )(a, b)


def _mm_rmsbwd(pairs, x, gain, dx_in, bscale, name):
    T, D = x.shape
    K = pairs[0][0].shape[1]
    npair = len(pairs)
    bm = _blk(T, 512, BF16_ROWS)
    bk = _blk(K, 512, LANES)
    nk = K // bk

    def body(*refs):
        ab = refs[:2 * npair]
        x_ref, g_ref, dxin_ref, dx_ref, dxb_ref, dg_ref, acc = refs[2 * npair:]
        i = pl.program_id(0)
        k = pl.program_id(1)
        p = _dot(ab[0][...], ab[1][...], NN)
        for q in range(1, npair):
            p = p + _dot(ab[2 * q][...], ab[2 * q + 1][...], NN)

        @pl.when(k == 0)
        def _():
            acc[...] = p

        @pl.when(k > 0)
        def _():
            acc[...] += p

        @pl.when(k == nk - 1)
        def _():
            @pl.when(i == 0)
            def _():
                dg_ref[...] = jnp.zeros_like(dg_ref)

            def chunk(c, _):
                r = _chunk_rows(c)
                dx, dg = _rms_bwd(x_ref[r, :], g_ref[...], acc[r, :])
                dxo = dxin_ref[r, :] + dx
                dx_ref[r, :] = dxo
                dxb_ref[r, :] = (bscale * dxo).astype(BF16)
                dg_ref[...] += dg
                return 0
            lax.fori_loop(0, bm // ROW_CHUNK, chunk, 0)

    row = pl.BlockSpec((bm, D), lambda i, k: (i, 0))
    vec = pl.BlockSpec((1, D), lambda i, k: (0, 0))
    in_specs = []
    args = []
    for a, b in pairs:
        in_specs += [pl.BlockSpec((bm, bk), lambda i, k: (i, k)), pl.BlockSpec((bk, D), lambda i, k: (k, 0))]
        args += [a, b]
    return pl.pallas_call(
        body, name=name, grid=(T // bm, nk),
        in_specs=in_specs + [row, vec, row], out_specs=[row, row, vec],
        out_shape=[SDS((T, D), F32), SDS((T, D), BF16), SDS((1, D), F32)],
        scratch_shapes=[pltpu.VMEM((bm, D), F32)],
        compiler_params=_cp(("arbitrary", "arbitrary"), 52),
    )(*args, x, gain, dx_in)


def _loss_head(x3, gain, tgt, name):
    T, D = x3.shape
    bm = _blk(T, 256, BF16_ROWS)

    def body(x_ref, g_ref, t_ref, dx_ref, dxb_ref, dg_ref, loss_ref):
        i = pl.program_id(0)
        xv = x_ref[...]
        g = g_ref[...]
        out = xv * _rstd(xv) * g
        e = out - t_ref[...]
        part = 0.5 * jnp.sum(jnp.mean(e * e, axis=-1, keepdims=True), axis=0, keepdims=True)
        dx, dg = _rms_bwd(xv, g, e * (1.0 / D))
        dx_ref[...] = dx
        dxb_ref[...] = (0.5 * dx).astype(BF16)

        @pl.when(i == 0)
        def _():
            dg_ref[...] = dg
            loss_ref[...] = jnp.broadcast_to(part, loss_ref.shape)

        @pl.when(i > 0)
        def _():
            dg_ref[...] += dg
            loss_ref[...] += jnp.broadcast_to(part, loss_ref.shape)

    row = pl.BlockSpec((bm, D), lambda i: (i, 0))
    vec = pl.BlockSpec((1, D), lambda i: (0, 0))
    return pl.pallas_call(
        body, name=name, grid=(T // bm,),
        in_specs=[row, vec, row], out_specs=[row, row, vec, pl.BlockSpec((SUBLANES, LANES), lambda i: (0, 0))],
        out_shape=[SDS((T, D), F32), SDS((T, D), BF16), SDS((1, D), F32), SDS((SUBLANES, LANES), F32)],
        compiler_params=_cp(("arbitrary",), 40),
    )(x3, gain, tgt)


R_CW, R_CB, R_BA, R_BI, R_LAM, R_SW, R_GLO, R_GSO, SMALL_ROWS = 0, 4, 5, 6, 7, 8, 11, 12, 16


def _rows(g):
    return pl.ds(pl.multiple_of(g * SUBLANES, SUBLANES), SUBLANES)


def _shift_back(prev, cur, d):
    row = lax.broadcasted_iota(jnp.int32, cur.shape, 0)
    return pltpu.roll(jnp.where(row >= SUBLANES - d, prev, cur), d, 0)


def _shift_fwd(cur, nxt, d):
    row = lax.broadcasted_iota(jnp.int32, cur.shape, 0)
    return pltpu.roll(jnp.where(row < d, nxt, cur), SUBLANES - d, 0)


def _causal_conv(ext, g, taps_ref, ntap):
    prev = ext[_rows(g), :]
    cur = ext[_rows(g + 1), :]
    out = _shift_back(prev, cur, ntap - 1) * taps_ref[0:1, :]
    for k in range(1, ntap - 1):
        out = out + _shift_back(prev, cur, ntap - 1 - k) * taps_ref[k:k + 1, :]
    return out + cur * taps_ref[ntap - 1:ntap, :]


def _scan8(A, U, reverse):
    row = lax.broadcasted_iota(jnp.int32, A.shape, 0)
    for s in (1, 2, 4):
        if reverse:
            A_sh = pltpu.roll(A, SUBLANES - s, 0)
            U_sh = pltpu.roll(U, SUBLANES - s, 0)
            m = row < SUBLANES - s
        else:
            A_sh = pltpu.roll(A, s, 0)
            U_sh = pltpu.roll(U, s, 0)
            m = row >= s
        U = jnp.where(m, A * U_sh + U, U)
        A = jnp.where(m, A * A_sh, A)
    return A, U


def _gate_pre(xc_s, w_ref, out_s, H, hd):
    for h in range(H):
        cs = slice(h * hd, (h + 1) * hd)
        out_s[:, cs] = _dot(xc_s[:, cs].astype(BF16), w_ref[h], NN)


def _lru_coeffs(pa, pi, xc, ba, bi, sp):
    ra = jax.nn.sigmoid(pa + ba)
    ri = jax.nn.sigmoid(pi + bi)
    log_a = (-LRU_C * ra) * sp
    a = jnp.exp(log_a)
    mult = jnp.sqrt(-_expm1(2.0 * log_a))
    return ra, ri, a, mult


def _softplus_neg(lam):
    v = -lam
    return jnp.maximum(v, 0.0) + _log1p(jnp.exp(-jnp.abs(v)))


def _mix_fwd(z, cw, cb, wa, ba, wi, bi, lam, sw, glo, gso, name):
    T = z.shape[0]
    C = z.shape[1] // 5
    H = wa.shape[0]
    hd = C // H
    tb = _blk(T, 256, BF16_ROWS)
    ng = tb // SUBLANES
    HDR = SUBLANES

    def body(z_ref, cw_ref, cb_ref, wa_ref, ba_ref, wi_ref, bi_ref, lam_ref, sw_ref, glo_ref, gso_ref,
             y_ref, h_ref, xext, pext, xc_s, pa_s, pi_s, y_s, hcar):
        @pl.when(pl.program_id(0) == 0)
        def _():
            xext[0:HDR, :] = jnp.zeros((HDR, C), F32)
            pext[0:HDR, :] = jnp.zeros((HDR, C), F32)
            hcar[...] = jnp.zeros_like(hcar)

        def fill(g, _):
            r = _rows(g)
            re = _rows(g + 1)
            xext[re, :] = z_ref[r, 0:C]
            pext[re, :] = z_ref[r, 3 * C:4 * C] * z_ref[r, 4 * C:5 * C]
            return 0
        lax.fori_loop(0, ng, fill, 0)

        def conv(g, _):
            xc_s[_rows(g), :] = _causal_conv(xext, g, cw_ref, 4) + cb_ref[...]
            return 0
        lax.fori_loop(0, ng, conv, 0)

        _gate_pre(xc_s, wa_ref, pa_s, H, hd)
        _gate_pre(xc_s, wi_ref, pi_s, H, hd)
        sp = _softplus_neg(lam_ref[...])

        def group(g, hprev):
            r = _rows(g)
            xc = xc_s[r, :]
            _, ri, a, mult = _lru_coeffs(pa_s[r, :], pi_s[r, :], xc, ba_ref[...], bi_ref[...], sp)
            A, U = _scan8(a, mult * (ri * xc), reverse=False)
            hh = A * hprev + U
            h_ref[r, :] = hh
            gel, _ = _gelu_parts(z_ref[r, C:2 * C])
            y_lru = hh * gel
            y_s[r, 0:C] = y_lru * _rstd(y_lru) * glo_ref[...]
            y_sc = z_ref[r, 2 * C:3 * C] * _causal_conv(pext, g, sw_ref, 3)
            y_s[r, C:2 * C] = y_sc * _rstd(y_sc) * gso_ref[...]
            return jnp.broadcast_to(hh[SUBLANES - 1:SUBLANES, :], hh.shape)
        hcar[...] = lax.fori_loop(0, ng, group, hcar[...])

        xext[0:HDR, :] = xext[tb:tb + HDR, :]
        pext[0:HDR, :] = pext[tb:tb + HDR, :]

        def cast(g, _):
            r = pl.ds(pl.multiple_of(g * BF16_ROWS, BF16_ROWS), BF16_ROWS)
            y_ref[r, :] = y_s[r, :].astype(BF16)
            return 0
        lax.fori_loop(0, tb // BF16_ROWS, cast, 0)

    full = lambda shape: pl.BlockSpec(shape, lambda i: (0,) * len(shape))
    blk = lambda w: pl.BlockSpec((tb, w), lambda i: (i, 0))
    ext = pltpu.VMEM((tb + HDR, C), F32)
    tile = pltpu.VMEM((tb, C), F32)
    return pl.pallas_call(
        body, name=name, grid=(T // tb,),
        in_specs=[blk(5 * C), full((4, C)), full((1, C)), full((H, hd, hd)), full((1, C)), full((H, hd, hd)),
                  full((1, C)), full((1, C)), full((3, C)), full((1, C)), full((1, C))],
        out_specs=[blk(2 * C), blk(C)],
        out_shape=[SDS((T, 2 * C), BF16), SDS((T, C), F32)],
        scratch_shapes=[ext, ext, tile, tile, tile, pltpu.VMEM((tb, 2 * C), F32), pltpu.VMEM((SUBLANES, C), F32)],
        compiler_params=_cp(("arbitrary",), 40),
    )(z, cw, cb, wa, ba, wi, bi, lam, sw, glo, gso)


def _mix_bwd(z, h, dy, cw, cb, wa, ba, wi, bi, lam, sw, glo, gso, name):
    T = z.shape[0]
    C = z.shape[1] // 5
    H = wa.shape[0]
    hd = C // H
    tb = _blk(T, 256, BF16_ROWS)
    nb = T // tb
    ng = tb // SUBLANES
    HDR = SUBLANES
    N_ACC = 13

    def body(z_ref, zp_ref, h_ref, hp_ref, dy_ref, cw_ref, cb_ref, wa_ref, ba_ref, wi_ref, bi_ref, lam_ref,
             sw_ref, glo_ref, gso_ref, dz_ref, small_ref, dwa_ref, dwi_ref,
             xext, pext, hext, dqext, dxcext, bext, xc_s, pa_s, pi_s, a_s, m_s, ri_s, dh_s, dpa_s, dpi_s,
             dz_s, acc_s, bcar):
        i = pl.program_id(0)
        first_rows = i == nb - 1

        @pl.when(i == 0)
        def _():
            dqext[tb:tb + HDR, :] = jnp.zeros((HDR, C), F32)
            dxcext[tb:tb + HDR, :] = jnp.zeros((HDR, C), F32)
            bcar[...] = jnp.zeros_like(bcar)
            acc_s[...] = jnp.zeros_like(acc_s)
            dwa_ref[...] = jnp.zeros_like(dwa_ref)
            dwi_ref[...] = jnp.zeros_like(dwi_ref)

        zero = jnp.zeros((HDR, C), F32)
        xext[0:HDR, :] = jnp.where(first_rows, zero, zp_ref[:, 0:C])
        pext[0:HDR, :] = jnp.where(first_rows, zero, zp_ref[:, 3 * C:4 * C] * zp_ref[:, 4 * C:5 * C])
        hext[0:HDR, :] = jnp.where(first_rows, zero, hp_ref[...])

        def fill(g, _):
            r = _rows(g)
            re = _rows(g + 1)
            xext[re, :] = z_ref[r, 0:C]
            pext[re, :] = z_ref[r, 3 * C:4 * C] * z_ref[r, 4 * C:5 * C]
            hext[re, :] = h_ref[r, :]
            return 0
        lax.fori_loop(0, ng, fill, 0)

        def conv(g, _):
            xc_s[_rows(g), :] = _causal_conv(xext, g, cw_ref, 4) + cb_ref[...]
            return 0
        lax.fori_loop(0, ng, conv, 0)

        _gate_pre(xc_s, wa_ref, pa_s, H, hd)
        _gate_pre(xc_s, wi_ref, pi_s, H, hd)
        sp = _softplus_neg(lam_ref[...])
        dsp_dlam = -jax.nn.sigmoid(-lam_ref[...])

        def add_acc(k, v):
            acc_s[k] += v

        def p1(g, _):
            r = _rows(g)
            xc = xc_s[r, :]
            _, ri, a, mult = _lru_coeffs(pa_s[r, :], pi_s[r, :], xc, ba_ref[...], bi_ref[...], sp)
            a_s[r, :] = a
            m_s[r, :] = mult
            ri_s[r, :] = ri
            hh = h_ref[r, :]
            gel, gelp = _gelu_parts(z_ref[r, C:2 * C])
            y_lru = hh * gel
            dnl = dy_ref[r, 0:C]
            ylr = y_lru * _rstd(y_lru)
            gd = glo_ref[...] * dnl
            dy_lru = _rstd(y_lru) * (gd - ylr * jnp.mean(gd * ylr, axis=-1, keepdims=True))
            add_acc(R_GLO, dnl * ylr)
            dz_s[r, C:2 * C] = dy_lru * hh * gelp
            dh = dy_lru * gel
            dh_s[r, :] = dh

            q = _causal_conv(pext, g, sw_ref, 3)
            scb = z_ref[r, 2 * C:3 * C]
            y_sc = scb * q
            dns = dy_ref[r, C:2 * C]
            ysr = y_sc * _rstd(y_sc)
            gs = gso_ref[...] * dns
            dy_sc = _rstd(y_sc) * (gs - ysr * jnp.mean(gs * ysr, axis=-1, keepdims=True))
            add_acc(R_GSO, dns * ysr)
            dz_s[r, 2 * C:3 * C] = dy_sc * q
            dqext[r, :] = dy_sc * scb
            return 0
        lax.fori_loop(0, ng, p1, 0)

        bext[tb:tb + HDR, :] = bcar[...]

        def p2(j, carry):
            g = ng - 1 - j
            r = _rows(g)
            a = a_s[r, :]
            A, U = _scan8(a, a * dh_s[r, :], reverse=True)
            bb = A * carry + U
            bext[r, :] = bb
            return jnp.broadcast_to(bb[0:1, :], bb.shape)
        bcar[...] = lax.fori_loop(0, ng, p2, bcar[...])

        def p3(g, _):
            r = _rows(g)
            rn = _rows(g + 1)
            G = dh_s[r, :] + _shift_fwd(bext[r, :], bext[rn, :], 1)
            hm1 = _shift_back(hext[r, :], hext[rn, :], 1)
            a = a_s[r, :]
            mult = m_s[r, :]
            ri = ri_s[r, :]
            xc = xc_s[r, :]
            ra = jax.nn.sigmoid(pa_s[r, :] + ba_ref[...])
            dxcext[r, :] = G * mult * ri
            dri = G * mult * xc
            dmult = G * ri * xc
            dlog_a = (G * hm1) * a - dmult * (a * a) / mult
            add_acc(R_LAM, dlog_a * (-LRU_C * ra) * dsp_dlam)
            dpa = dlog_a * (-LRU_C * sp) * ra * (1.0 - ra)
            dpi = dri * ri * (1.0 - ri)
            add_acc(R_BA, dpa)
            add_acc(R_BI, dpi)
            dpa_s[r, :] = dpa
            dpi_s[r, :] = dpi
            return 0
        lax.fori_loop(0, ng, p3, 0)

        for hh_ in range(H):
            cs = slice(hh_ * hd, (hh_ + 1) * hd)
            dpa_b = dpa_s[:, cs].astype(BF16)
            dpi_b = dpi_s[:, cs].astype(BF16)
            xc_b = xc_s[:, cs].astype(BF16)
            dxcext[0:tb, cs] += _dot(dpa_b, wa_ref[hh_], NT) + _dot(dpi_b, wi_ref[hh_], NT)
            dwa_ref[hh_] += _dot(xc_b, dpa_b, TN)
            dwi_ref[hh_] += _dot(xc_b, dpi_b, TN)

        def p4(g, _):
            r = _rows(g)
            rn = _rows(g + 1)
            dxc = dxcext[r, :]
            dxc_n = dxcext[rn, :]
            x_p = xext[r, :]
            x_c = xext[rn, :]
            add_acc(R_CB, dxc)
            dlx = dxc * cw_ref[3:4, :]
            add_acc(R_CW + 3, dxc * x_c)
            for d in range(1, 4):
                dlx = dlx + _shift_fwd(dxc, dxc_n, d) * cw_ref[3 - d:4 - d, :]
                add_acc(R_CW + 3 - d, dxc * _shift_back(x_p, x_c, d))
            dz_s[r, 0:C] = dlx

            dq = dqext[r, :]
            dq_n = dqext[rn, :]
            p_p = pext[r, :]
            p_c = pext[rn, :]
            dp = dq * sw_ref[2:3, :]
            add_acc(R_SW + 2, dq * p_c)
            for d in range(1, 3):
                dp = dp + _shift_fwd(dq, dq_n, d) * sw_ref[2 - d:3 - d, :]
                add_acc(R_SW + 2 - d, dq * _shift_back(p_p, p_c, d))
            dz_s[r, 3 * C:4 * C] = dp * z_ref[r, 4 * C:5 * C]
            dz_s[r, 4 * C:5 * C] = dp * z_ref[r, 3 * C:4 * C]
            return 0
        lax.fori_loop(0, ng, p4, 0)

        dqext[tb:tb + HDR, :] = dqext[0:HDR, :]
        dxcext[tb:tb + HDR, :] = dxcext[0:HDR, :]

        def cast(g, _):
            r = pl.ds(pl.multiple_of(g * BF16_ROWS, BF16_ROWS), BF16_ROWS)
            dz_ref[r, :] = dz_s[r, :].astype(BF16)
            return 0
        lax.fori_loop(0, tb // BF16_ROWS, cast, 0)

        @pl.when(i == nb - 1)
        def _():
            small_ref[...] = jnp.zeros_like(small_ref)
            for k in range(N_ACC):
                small_ref[k:k + 1, :] = jnp.sum(acc_s[k], axis=0, keepdims=True)

    tpg = tb // SUBLANES
    full = lambda shape: pl.BlockSpec(shape, lambda i: (0,) * len(shape))
    blk = lambda w: pl.BlockSpec((tb, w), lambda i: (nb - 1 - i, 0))
    prev = lambda w: pl.BlockSpec((SUBLANES, w), lambda i: (jnp.maximum((nb - 1 - i) * tpg - 1, 0), 0))
    ext = pltpu.VMEM((tb + HDR, C), F32)
    tile = pltpu.VMEM((tb, C), F32)
    return pl.pallas_call(
        body, name=name, grid=(nb,),
        in_specs=[blk(5 * C), prev(5 * C), blk(C), prev(C), blk(2 * C), full((4, C)), full((1, C)), full((H, hd, hd)),
                  full((1, C)), full((H, hd, hd)), full((1, C)), full((1, C)), full((3, C)), full((1, C)), full((1, C))],
        out_specs=[blk(5 * C), full((SMALL_ROWS, C)), full((H, hd, hd)), full((H, hd, hd))],
        out_shape=[SDS((T, 5 * C), BF16), SDS((SMALL_ROWS, C), F32), SDS((H, hd, hd), F32), SDS((H, hd, hd), F32)],
        scratch_shapes=[ext] * 6 + [tile] * 9 + [pltpu.VMEM((tb, 5 * C), F32), pltpu.VMEM((N_ACC, SUBLANES, C), F32),
                                                pltpu.VMEM((SUBLANES, C), F32)],
        compiler_params=_cp(("arbitrary",), 56),
    )(z, z, h, h, dy, cw, cb, wa, ba, wi, bi, lam, sw, glo, gso)


def _add_slabs(terms, out_dtype, name):
    R, Ccols = terms[0].shape
    br = _blk(R, 512, BF16_ROWS)
    n = len(terms)

    def body(*refs):
        s = refs[0][...].astype(F32)
        for t_ref in refs[1:n]:
            s = s + t_ref[...].astype(F32)
        refs[n][...] = s.astype(out_dtype)

    spec = pl.BlockSpec((br, Ccols), lambda i: (i, 0))
    return pl.pallas_call(
        body, name=name, grid=(R // br,), in_specs=[spec] * n, out_specs=spec, out_shape=SDS((R, Ccols), out_dtype),
        compiler_params=_cp(("arbitrary",), 40),
    )(*terms)


def _final_grad(sb, lb, chip, name):
    _, R, Ccols = sb.shape
    br = _blk(R, 512, BF16_ROWS)

    def body(chip_ref, sb_ref, l0, l1, l2, o_ref):
        s = sb_ref[0].astype(F32)
        for t_ref in (l0, l1, l2):
            s = s + t_ref[0].astype(F32)
        o_ref[...] = s

    lspec = lambda k: pl.BlockSpec((1, br, Ccols), lambda i, c: (k, i, 0))
    return pl.pallas_call(
        body, name=name,
        grid_spec=pltpu.PrefetchScalarGridSpec(
            num_scalar_prefetch=1, grid=(R // br,),
            in_specs=[pl.BlockSpec((1, br, Ccols), lambda i, c: (c[0], i, 0)), lspec(0), lspec(1), lspec(2)],
            out_specs=pl.BlockSpec((br, Ccols), lambda i, c: (i, 0))),
        out_shape=SDS((R, Ccols), F32),
        compiler_params=_cp(("arbitrary",), 40),
    )(chip, sb, lb, lb, lb)


def _adamw(w, g, m, v, name):
    R, Ccols = w.shape
    br = _blk(R, 256, SUBLANES)
    c1 = 1.0 - ADAM_B1 ** ADAM_STEP
    c2 = 1.0 - ADAM_B2 ** ADAM_STEP

    def body(w_ref, g_ref, m_ref, v_ref, d_ref, nm_ref, nv_ref):
        gv = g_ref[...]
        nm = ADAM_B1 * m_ref[...] + (1.0 - ADAM_B1) * gv
        nv = ADAM_B2 * v_ref[...] + (1.0 - ADAM_B2) * (gv * gv)
        nm_ref[...] = nm
        nv_ref[...] = nv
        d_ref[...] = -ADAM_LR * ((nm / c1) / (jnp.sqrt(nv / c2) + ADAM_EPS) + ADAM_WD * w_ref[...])

    spec = pl.BlockSpec((br, Ccols), lambda i: (i, 0))
    return pl.pallas_call(
        body, name=name, grid=(R // br,), in_specs=[spec] * 4, out_specs=[spec] * 3,
        out_shape=[SDS((R, Ccols), F32)] * 3, compiler_params=_cp(("arbitrary",), 40),
    )(w, g, m, v)


HBM_SPEC = pl.BlockSpec(memory_space=pltpu.HBM)


def _place():
    return lax.axis_index("x"), lax.axis_index("y"), lax.axis_index("c")


def _allgather(slab, pieces, name):
    R, W = slab.shape
    n = len(pieces)
    assert sum(rows for _, rows in pieces) == R

    def body(slab_ref, *refs):
        outs = refs[:n]
        send_sems, recv_sems, local_sems = refs[n:]
        x, y, c = _place()
        me, sibling = (x, y, c), (x, y, 1 - c)
        chips = [(1 - x, y), (x, 1 - y), (1 - x, 1 - y)]

        def dst_rows(p, origin):
            rows = pieces[p][1]
            start = (4 * origin[0] + 2 * origin[1] + origin[2]) * rows
            return outs[p].at[pl.ds(start, rows), :]

        def copies(k, origin, to, from_slab):
            out = []
            for p, (off, rows) in enumerate(pieces):
                dst = dst_rows(p, origin)
                src = slab_ref.at[pl.ds(off, rows), :] if from_slab else dst
                out.append(pltpu.make_async_remote_copy(
                    src_ref=src, dst_ref=dst, send_sem=send_sems.at[k, p], recv_sem=recv_sems.at[k, p],
                    device_id=to, device_id_type=MESH))
            return out

        mine = [pltpu.make_async_copy(slab_ref.at[pl.ds(off, rows), :], dst_rows(p, me), local_sems.at[p])
                for p, (off, rows) in enumerate(pieces)]
        for cp in mine:
            cp.start()
        first = copies(0, me, sibling, True)
        for j, chip in enumerate(chips):
            first += copies(1 + j, me, (*chip, c), True)
        for cp in first:
            cp.start()
        passed = []
        for j, chip in enumerate(chips):
            for cp in copies(1 + j, (*chip, c), me, False):
                cp.wait_recv()
            fwd = copies(4 + j, (*chip, c), sibling, False)
            for cp in fwd:
                cp.start()
            passed += fwd
        for cp in copies(0, sibling, me, False):
            cp.wait_recv()
        for j, chip in enumerate(chips):
            for cp in copies(4 + j, (*chip, 1 - c), me, False):
                cp.wait_recv()
        for cp in first + passed:
            cp.wait_send()
        for cp in mine:
            cp.wait()

    return pl.pallas_call(
        body, name=name,
        in_specs=[HBM_SPEC], out_specs=[HBM_SPEC] * n,
        out_shape=[SDS((N_DEV * rows, W), slab.dtype) for _, rows in pieces],
        scratch_shapes=[pltpu.SemaphoreType.DMA((7, n)), pltpu.SemaphoreType.DMA((7, n)), pltpu.SemaphoreType.DMA((n,))],
    )(slab)


def _rs_sibling(grads, pieces, name):
    W = grads[0].shape[1]
    R = sum(rows for _, rows in pieces)
    n = len(pieces)
    dt = grads[0].dtype
    max_rows = max(rows for _, rows in pieces)
    steps = [(q, p) for q in range(N_CHIP) for p in range(n)]
    ns = len(steps)
    ADD_ROWS = 64
    assert all(rows % ADD_ROWS == 0 for _, rows in pieces)

    def body(*refs):
        g_refs = refs[:n]
        sb_ref, mine_buf, send_buf, land_buf, out_buf, in_sems, out_sems, send_sems, recv_sems, credit = refs[n:]
        x, y, c = _place()
        sibling = (x, y, 1 - c)

        def loads(s):
            q, p = steps[s]
            rows = pieces[p][1]
            slot = s % 2
            mine = g_refs[p].at[pl.ds((2 * q + c) * rows, rows), :]
            theirs = g_refs[p].at[pl.ds((2 * q + 1 - c) * rows, rows), :]
            return (pltpu.make_async_copy(mine, mine_buf.at[slot, pl.ds(0, rows), :], in_sems.at[slot, 0]),
                    pltpu.make_async_copy(theirs, send_buf.at[slot, pl.ds(0, rows), :], in_sems.at[slot, 1]))

        def send(s):
            rows = pieces[steps[s][1]][1]
            slot = s % 2
            return pltpu.make_async_remote_copy(
                src_ref=send_buf.at[slot, pl.ds(0, rows), :], dst_ref=land_buf.at[slot, pl.ds(0, rows), :],
                send_sem=send_sems.at[slot], recv_sem=recv_sems.at[slot], device_id=sibling, device_id_type=MESH)

        def store(s):
            q, p = steps[s]
            off, rows = pieces[p]
            slot = s % 2
            return pltpu.make_async_copy(out_buf.at[slot, pl.ds(0, rows), :], sb_ref.at[q, pl.ds(off, rows), :],
                                         out_sems.at[slot])

        for cp in loads(0):
            cp.start()
        for s in range(ns):
            slot = s % 2
            rows = pieces[steps[s][1]][1]
            if s + 1 < ns:
                if s >= 1:
                    send(s - 1).wait_send()
                for cp in loads(s + 1):
                    cp.start()
            for cp in loads(s):
                cp.wait()
            if s >= 2:
                pl.semaphore_wait(credit.at[slot], 1)
            send(s).start()
            send(s).wait_recv()
            if s >= 2:
                store(s - 2).wait()

            def add(k, _, slot=slot):
                r = pl.ds(pl.multiple_of(k * ADD_ROWS, ADD_ROWS), ADD_ROWS)
                out_buf[slot, r, :] = (mine_buf[slot, r, :].astype(F32) + land_buf[slot, r, :].astype(F32)).astype(dt)
                return 0
            lax.fori_loop(0, rows // ADD_ROWS, add, 0)
            if s + 2 < ns:
                pl.semaphore_signal(credit.at[slot], inc=1, device_id=sibling, device_id_type=MESH)
            store(s).start()
        for s in range(max(ns - 2, 0), ns):
            send(s).wait_send()
            store(s).wait()

    buf = pltpu.VMEM((2, max_rows, W), dt)
    return pl.pallas_call(
        body, name=name,
        in_specs=[HBM_SPEC] * n, out_specs=HBM_SPEC,
        out_shape=SDS((N_CHIP, R, W), dt),
        scratch_shapes=[buf, buf, buf, buf, pltpu.SemaphoreType.DMA((2, 2)), pltpu.SemaphoreType.DMA((2,)),
                        pltpu.SemaphoreType.DMA((2,)), pltpu.SemaphoreType.DMA((2,)), pltpu.SemaphoreType.REGULAR((2,))],
        compiler_params=pltpu.CompilerParams(vmem_limit_bytes=40 * MIB),
    )(*grads)


def _rs_chips(sb, name):
    _, R, W = sb.shape

    def body(sb_ref, land_ref, send_sems, recv_sems):
        x, y, c = _place()
        chips = [(1 - x, y), (x, 1 - y), (1 - x, 1 - y)]
        cps = [pltpu.make_async_remote_copy(
            src_ref=sb_ref.at[2 * chip[0] + chip[1]], dst_ref=land_ref.at[j],
            send_sem=send_sems.at[j], recv_sem=recv_sems.at[j], device_id=(*chip, c), device_id_type=MESH)
            for j, chip in enumerate(chips)]
        for cp in cps:
            cp.start()
        for cp in cps:
            cp.wait_recv()
        for cp in cps:
            cp.wait_send()

    return pl.pallas_call(
        body, name=name, in_specs=[HBM_SPEC], out_specs=HBM_SPEC, out_shape=SDS((3, R, W), sb.dtype),
        scratch_shapes=[pltpu.SemaphoreType.DMA((3,))] * 2,
    )(sb)


SMALL_NAMES = ("ffn1_norm", "mix_norm", "ffn2_norm", "final_norm", "lru_conv_w", "lru_conv_b", "lru_w_a", "lru_b_a",
               "lru_w_i", "lru_b_i", "lru_lambda", "sc_conv_w", "lru_out_norm", "sc_out_norm")
WEIGHT_NAMES = ("ffn1_norm", "ffn1_w_gate", "ffn1_w_up", "ffn1_w_down", "mix_norm", "w_in", "lru_conv_w", "lru_conv_b",
                "lru_w_a", "lru_b_a", "lru_w_i", "lru_b_i", "lru_lambda", "sc_conv_w", "lru_out_norm", "sc_out_norm",
                "w_out", "ffn2_norm", "ffn2_w_gate", "ffn2_w_up", "ffn2_w_down", "final_norm")
BIG = (("ffn1_w_gate", True), ("ffn1_w_up", True), ("ffn1_w_down", False), ("ffn2_w_gate", True), ("ffn2_w_up", True),
       ("ffn2_w_down", False), ("w_in", True), ("w_out", False))


SLAB_ROW_ALIGN = 256


def _pack_rows(parts, width):
    rows, counts = [], []
    for p in parts:
        flat = p.reshape(-1)
        nr = -(-flat.shape[0] // width)
        nr = -(-nr // SUBLANES) * SUBLANES
        rows.append(jnp.pad(flat, (0, nr * width - flat.shape[0])).reshape(nr, width))
        counts.append(nr)
    total = sum(counts)
    pad = -(-total // SLAB_ROW_ALIGN) * SLAB_ROW_ALIGN - total
    if pad:
        rows.append(jnp.zeros((pad, width), rows[0].dtype))
    return jnp.concatenate(rows, axis=0), counts


def _unpack_rows(slab, counts, shapes):
    out, r = [], 0
    for nr, shape in zip(counts, shapes):
        size = math.prod(shape)
        out.append(slab[r:r + nr].reshape(-1)[:size].reshape(shape))
        r += nr
    return out


def kernel(x, ffn1_norm, ffn1_w_gate, ffn1_w_up, ffn1_w_down, mix_norm, w_in, lru_conv_w, lru_conv_b, lru_w_a, lru_b_a, lru_w_i, lru_b_i, lru_lambda, sc_conv_w, lru_out_norm, sc_out_norm, w_out, ffn2_norm, ffn2_w_gate, ffn2_w_up, ffn2_w_down, final_norm, loss_target, m_ffn1_norm, m_ffn1_w_gate, m_ffn1_w_up, m_ffn1_w_down, m_mix_norm, m_w_in, m_lru_conv_w, m_lru_conv_b, m_lru_w_a, m_lru_b_a, m_lru_w_i, m_lru_b_i, m_lru_lambda, m_sc_conv_w, m_lru_out_norm, m_sc_out_norm, m_w_out, m_ffn2_norm, m_ffn2_w_gate, m_ffn2_w_up, m_ffn2_w_down, m_final_norm, v_ffn1_norm, v_ffn1_w_gate, v_ffn1_w_up, v_ffn1_w_down, v_mix_norm, v_w_in, v_lru_conv_w, v_lru_conv_b, v_lru_w_a, v_lru_b_a, v_lru_w_i, v_lru_b_i, v_lru_lambda, v_sc_conv_w, v_lru_out_norm, v_sc_out_norm, v_w_out, v_ffn2_norm, v_ffn2_w_gate, v_ffn2_w_up, v_ffn2_w_down, v_final_norm):
    a = dict(locals())
    w = {n: a[n] for n in WEIGHT_NAMES}
    m = {n: a["m_" + n] for n in WEIGHT_NAMES}
    v = {n: a["v_" + n] for n in WEIGHT_NAMES}
    ax, ay, ac = _place()
    dev = 4 * ax + 2 * ay + ac
    chip = (2 * ax + ay).astype(jnp.int32).reshape(1)

    x0 = x[0]
    tgt = loss_target[0]
    T, D = x0.shape
    C = D // 2
    H, hd = lru_w_a.shape[1], lru_w_a.shape[2]
    CL = lru_conv_w.shape[2]

    shards = []
    for name, transposed in BIG:
        s = w[name][0]
        shards.append((s.T if transposed else s).astype(BF16))
    taps = jnp.concatenate([lru_conv_w[0], sc_conv_w[0], jnp.zeros((1, CL), F32)], axis=0)
    taps_row = lax.bitcast_convert_type(taps, BF16).reshape(1, -1)
    taps_blk = jnp.pad(taps_row, ((0, BF16_ROWS - 1), (0, D - taps_row.shape[1])))
    slab = jnp.concatenate(shards + [taps_blk], axis=0)
    pieces, off = [], 0
    for s in shards + [taps_blk]:
        pieces.append((off, s.shape[0]))
        off += s.shape[0]
    gathered = _allgather(slab, pieces, "allgather_weights")
    wg1, wu1, wd1, wg2, wu2, wd2, win, wout, taps_all = gathered
    taps_all = taps_all.reshape(N_DEV, BF16_ROWS, D)[:, 0, :2 * SUBLANES * CL].reshape(N_DEV, SUBLANES, CL, 2)
    taps_all = lax.bitcast_convert_type(taps_all, F32)
    taps_all = taps_all.transpose(1, 0, 2).reshape(SUBLANES, N_DEV * CL)
    cw, sw = taps_all[0:4], taps_all[4:7]

    g1, gm, g3 = ffn1_norm, mix_norm, ffn2_norm
    gf = final_norm.reshape(1, D)
    cb = lru_conv_b
    wa, wi = lru_w_a[0].astype(BF16), lru_w_i[0].astype(BF16)
    ba, bi = lru_b_a.reshape(1, C), lru_b_i.reshape(1, C)
    lam, glo, gso = lru_lambda, lru_out_norm, sc_out_norm

    n1, hg1, hu1, act1 = _norm_proj(x0, g1, [wg1, wu1], [BF16, BF16], True, "ffn1_up")
    x1 = _mm_res(act1, wd1, x0, 0.5, "ffn1_down")
    n2, z = _norm_proj(x1, gm, [win], [F32], False, "in_proj")
    y, h = _mix_fwd(z, cw, cb, wa, ba, wi, bi, lam, sw, glo, gso, "mix_fwd")
    x2 = _mm_res(y, wout, x1, 1.0, "out_proj")
    n3, hg2, hu2, act2 = _norm_proj(x2, g3, [wg2, wu2], [BF16, BF16], True, "ffn2_up")
    x3 = _mm_res(act2, wd2, x2, 0.5, "ffn2_down")
    dx3, df2, d_gf, loss_blk = _loss_head(x3, gf, tgt, "loss_head")

    F = wd1.shape[0]
    bm_f = F // 4 if (F // 4) % LANES == 0 else 512
    dhg2, dhu2 = _ffn_bwd_act(df2, wd2, hg2, hu2, "ffn2_bwd_act")
    d_wd2 = _dw_tn(act2, df2, bm_f, "ffn2_dw_down")
    d_wg2 = _dw_tn(dhg2, n3, bm_f, "ffn2_dw_gate")
    d_wu2 = _dw_tn(dhu2, n3, bm_f, "ffn2_dw_up")
    dx2, dx2b, d_g3 = _mm_rmsbwd([(dhg2, wg2), (dhu2, wu2)], x2, g3, dx3, 1.0, "ffn2_bwd_in")
    dy = _mm_nt(dx2b, wout, "out_proj_bwd")
    d_wout = _dw_tn(y, dx2b, 1024, "out_proj_dw")
    dz, small, d_wa, d_wi = _mix_bwd(z, h, dy, cw, cb, wa, ba, wi, bi, lam, sw, glo, gso, "mix_bwd")
    d_win = _dw_tn(dz, n2, 1280, "in_proj_dw")
    dx1, df1, d_gm = _mm_rmsbwd([(dz, win)], x1, gm, dx2, 0.5, "in_proj_bwd")
    dhg1, dhu1 = _ffn_bwd_act(df1, wd1, hg1, hu1, "ffn1_bwd_act")
    d_wd1 = _dw_tn(act1, df1, bm_f, "ffn1_dw_down")
    d_wg1 = _dw_tn(dhg1, n1, bm_f, "ffn1_dw_gate")
    d_wu1 = _dw_tn(dhu1, n1, bm_f, "ffn1_dw_up")
    dx0, _, d_g1 = _mm_rmsbwd([(dhg1, wg1), (dhu1, wu1)], x0, g1, dx1, 1.0, "ffn1_bwd_in")

    big_grads = [d_wg1, d_wu1, d_wd1, d_wg2, d_wu2, d_wd2, d_win, d_wout]
    big_pieces = pieces[:len(BIG)]
    R = sum(rows for _, rows in big_pieces)
    sb = _rs_sibling(big_grads, big_pieces, "rs_sibling_add")
    lb = _rs_chips(sb, "rs_chips")
    gsum = _final_grad(sb, lb, chip, "rs_final_sum")

    small_parts = [d_g1, d_gm, d_g3, d_gf, small[R_CW:R_CW + 4], small[R_CB], d_wa, small[R_BA], d_wi, small[R_BI],
                   small[R_LAM], small[R_SW:R_SW + 3], small[R_GLO], small[R_GSO]]
    sslab, counts = _pack_rows(small_parts, LANES)
    RS = sslab.shape[0]
    (sg,) = _allgather(sslab, [(0, RS)], "allgather_small_grads")
    ssum = _add_slabs([sg[j * RS:(j + 1) * RS] for j in range(N_DEV)], F32, "small_grads_sum")
    full_shapes = [(1, D), (1, D), (1, D), (D,), (1, 4, C), (1, C), (1, H, hd, hd), (1, H, hd), (1, H, hd, hd), (1, H, hd),
                   (1, C), (1, 3, C), (1, C), (1, C)]
    small_full = dict(zip(SMALL_NAMES, _unpack_rows(ssum, counts, full_shapes)))

    grads = {}
    for (name, transposed), (off, rows) in zip(BIG, big_pieces):
        gblk = gsum[off:off + rows]
        grads[name] = (gblk.T if transposed else gblk)[None]
    for name in SMALL_NAMES:
        gfull = small_full[name]
        if name in ("lru_conv_w", "sc_conv_w"):
            gfull = lax.dynamic_slice_in_dim(gfull, dev * CL, CL, axis=2)
        grads[name] = gfull

    delta, new_m, new_v = {}, {}, {}
    for name, _ in BIG:
        shp = w[name].shape
        d_, m_, v_ = _adamw(w[name][0], grads[name][0], m[name][0], v[name][0], "adamw_" + name)
        delta[name], new_m[name], new_v[name] = d_.reshape(shp), m_.reshape(shp), v_.reshape(shp)
    packs = [_pack_rows([t[n_] for n_ in SMALL_NAMES], LANES) for t in (w, grads, m, v)]
    sd, sm, sv = _adamw(packs[0][0], packs[1][0], packs[2][0], packs[3][0], "adamw_small")
    shapes = [w[n_].shape for n_ in SMALL_NAMES]
    for tgt_dict, slab_ in ((delta, sd), (new_m, sm), (new_v, sv)):
        for n_, val in zip(SMALL_NAMES, _unpack_rows(slab_, packs[0][1], shapes)):
            tgt_dict[n_] = val

    loss = lax.psum(loss_blk[0, 0], ("x", "y", "c"))
    return (loss, dx0[None], *[grads[n_] for n_ in WEIGHT_NAMES], *[delta[n_] for n_ in WEIGHT_NAMES],
            *[new_m[n_] for n_ in WEIGHT_NAMES], *[new_v[n_] for n_ in WEIGHT_NAMES])
```

```python
import functools
import math

import jax
import jax.numpy as jnp
from jax import lax
from jax.experimental import pallas as pl
from jax.experimental.pallas import tpu as pltpu

F32 = jnp.float32
BF16 = jnp.bfloat16
SDS = jax.ShapeDtypeStruct
MESH = pl.DeviceIdType.MESH

NORM_EPS = 1e-6
LRU_C = 8.0
N_DEV = 8
N_CHIP = 4
ADAM_LR, ADAM_B1, ADAM_B2, ADAM_EPS, ADAM_WD, ADAM_STEP = 0.001, 0.9, 0.999, 1e-08, 0.01, 10

NN = (((1,), (0,)), ((), ()))
NT = (((1,), (1,)), ((), ()))
TN = (((0,), (0,)), ((), ()))

SUBLANES = 8
BF16_ROWS = 16
LANES = 128
MIB = 1 << 20


def _dot(a, b, dims):
    return lax.dot_general(a, b, dims, preferred_element_type=F32)


def _blk(n, pref, align):
    if n <= pref:
        return n
    b = (pref // align) * align
    while b >= align:
        if n % b == 0:
            return b
        b -= align
    raise ValueError(f"no block of {n} aligned to {align} under {pref}")


def _cp(sem, vmem_mib):
    return pltpu.CompilerParams(dimension_semantics=sem, vmem_limit_bytes=vmem_mib * MIB)


HBM_SPEC = pl.BlockSpec(memory_space=pltpu.HBM)


class _Carried:
    def __init__(self, inputs, out_shapes, aliases, sem_shapes, build):
        self.inputs, self.out_shapes, self.aliases = list(inputs), list(out_shapes), dict(aliases)
        self.sem_shapes, self.build = list(sem_shapes), build


def _call(body, *, name, grid, in_specs, out_specs, out_shape, scratch_shapes, compiler_params, args, carried=None):
    if carried is None:
        return pl.pallas_call(body, name=name, grid=grid, in_specs=in_specs, out_specs=out_specs, out_shape=out_shape,
                              scratch_shapes=scratch_shapes, compiler_params=compiler_params)(*args)
    n_in, n_out, n_sc = len(in_specs), len(out_shape), len(scratch_shapes)
    c_in, c_out = len(carried.inputs), len(carried.out_shapes)

    def hosted(*refs):
        ins, refs = refs[:n_in], refs[n_in:]
        c_ins, refs = refs[:c_in], refs[c_in:]
        outs, refs = refs[:n_out], refs[n_out:]
        c_outs, refs = refs[:c_out], refs[c_out:]
        scratch, c_sems = refs[:n_sc], refs[n_sc:]
        first = functools.reduce(jnp.logical_and, [pl.program_id(a) == 0 for a in range(len(grid))])
        last = functools.reduce(jnp.logical_and, [pl.program_id(a) == g - 1 for a, g in enumerate(grid)])

        @pl.when(first)
        def _():
            for start in carried.build(c_ins, c_outs, c_sems, True):
                start()

        body(*ins, *outs, *scratch)

        @pl.when(last)
        def _():
            for wait in carried.build(c_ins, c_outs, c_sems, False):
                wait()

    out = pl.pallas_call(
        hosted, name=name, grid=grid, in_specs=list(in_specs) + [HBM_SPEC] * c_in,
        out_specs=list(out_specs) + [HBM_SPEC] * c_out, out_shape=list(out_shape) + carried.out_shapes,
        scratch_shapes=list(scratch_shapes) + carried.sem_shapes,
        input_output_aliases={n_in + a: n_out + b for a, b in carried.aliases.items()},
        compiler_params=compiler_params)(*args, *carried.inputs)
    return out[:n_out], out[n_out:]


ROW_CHUNK = 128


def _chunk_rows(c):
    return pl.ds(pl.multiple_of(c * ROW_CHUNK, ROW_CHUNK), ROW_CHUNK)


def _rstd(xv):
    return lax.rsqrt(jnp.mean(xv * xv, axis=-1, keepdims=True) + NORM_EPS)


def _rms_bwd(xv, g, dn):
    r = _rstd(xv)
    xr = xv * r
    gd = g * dn
    dx = r * (gd - xr * jnp.mean(gd * xr, axis=-1, keepdims=True))
    return dx, jnp.sum(dn * xr, axis=0, keepdims=True)


def _log1p(e):
    u = 1.0 + e
    return jnp.where(u == 1.0, e, jnp.log(u) * (e / (u - 1.0)))


def _expm1(v):
    u = jnp.exp(v)
    return jnp.where(u == 1.0, v, jnp.where(u == 0.0, -1.0, (u - 1.0) * (v / jnp.log(u))))


def _gelu_parts(g):
    k0 = math.sqrt(2.0 / math.pi)
    g2 = g * g
    t = jnp.tanh(k0 * (g + 0.044715 * g * g2))
    gel = 0.5 * g * (1.0 + t)
    gelp = 0.5 * (1.0 + t) + 0.5 * g * (1.0 - t * t) * (k0 * (1.0 + 3.0 * 0.044715 * g2))
    return gel, gelp


def _norm_proj(x, gain, w_list, out_dtypes, swiglu, name, carried=None):
    T, D = x.shape
    N = w_list[0].shape[0]
    nw = len(w_list)
    bm = _blk(T, 1024, BF16_ROWS)
    bn = _blk(N, 512, LANES)

    def body(*refs):
        x_ref, g_ref = refs[:2]
        w_refs = refs[2:2 + nw]
        n_ref = refs[2 + nw]
        o_refs = refs[3 + nw:3 + 2 * nw]
        act_ref = refs[3 + 2 * nw] if swiglu else None
        n_sc = refs[-1]

        @pl.when(pl.program_id(1) == 0)
        def _():
            def chunk(c, _):
                r = _chunk_rows(c)
                xv = x_ref[r, :]
                nb = (xv * _rstd(xv) * g_ref[...]).astype(BF16)
                n_sc[r, :] = nb
                n_ref[r, :] = nb
                return 0
            lax.fori_loop(0, bm // ROW_CHUNK, chunk, 0)

        n = n_sc[...]
        outs = [_dot(n, w_ref[...], NT) for w_ref in w_refs]
        for o_ref, o in zip(o_refs, outs):
            o_ref[...] = o.astype(o_ref.dtype)
        if swiglu:
            hg, hu = outs
            act_ref[...] = (hg * jax.nn.sigmoid(hg) * hu).astype(BF16)

    row = pl.BlockSpec((bm, D), lambda i, j: (i, 0))
    tile = pl.BlockSpec((bm, bn), lambda i, j: (i, j))
    n_extra = 1 if swiglu else 0
    return _call(
        body, name=name, grid=(T // bm, N // bn),
        in_specs=[row, pl.BlockSpec((1, D), lambda i, j: (0, 0))] + [pl.BlockSpec((bn, D), lambda i, j: (j, 0))] * nw,
        out_specs=[row] + [tile] * (nw + n_extra),
        out_shape=[SDS((T, D), BF16)] + [SDS((T, N), dt) for dt in out_dtypes] + [SDS((T, N), BF16)] * n_extra,
        scratch_shapes=[pltpu.VMEM((bm, D), BF16)],
        compiler_params=_cp(("arbitrary", "arbitrary"), 52),
        args=(x, gain, *w_list), carried=carried)


def _mm_res(a, b, x, scale, name, carried=None):
    T, K = a.shape
    D = b.shape[1]
    bm = _blk(T, 1024, BF16_ROWS)
    bk = _blk(K, 512, LANES)
    nk = K // bk

    def body(a_ref, b_ref, x_ref, o_ref):
        k = pl.program_id(1)
        p = _dot(a_ref[...], b_ref[...], NN)

        @pl.when(k == 0)
        def _():
            o_ref[...] = p

        @pl.when(k > 0)
        def _():
            o_ref[...] += p

        @pl.when(k == nk - 1)
        def _():
            def chunk(c, _):
                r = _chunk_rows(c)
                o_ref[r, :] = x_ref[r, :] + scale * o_ref[r, :]
                return 0
            lax.fori_loop(0, bm // ROW_CHUNK, chunk, 0)

    row = pl.BlockSpec((bm, D), lambda i, k: (i, 0))
    out = _call(
        body, name=name, grid=(T // bm, nk),
        in_specs=[pl.BlockSpec((bm, bk), lambda i, k: (i, k)), pl.BlockSpec((bk, D), lambda i, k: (k, 0)), row],
        out_specs=[row], out_shape=[SDS((T, D), F32)], scratch_shapes=[],
        compiler_params=_cp(("arbitrary", "arbitrary"), 52),
        args=(a, b, x), carried=carried)
    return out[0] if carried is None else (out[0][0], out[1])


def _mm_nt(a, b, name):
    T, K = a.shape
    N = b.shape[0]
    bm = _blk(T, 1024, BF16_ROWS)
    bn = _blk(N, 512, LANES)

    def body(a_ref, b_ref, o_ref):
        o_ref[...] = _dot(a_ref[...], b_ref[...], NT)

    return pl.pallas_call(
        body, name=name, grid=(T // bm, N // bn),
        in_specs=[pl.BlockSpec((bm, K), lambda i, j: (i, 0)), pl.BlockSpec((bn, K), lambda i, j: (j, 0))],
        out_specs=pl.BlockSpec((bm, bn), lambda i, j: (i, j)), out_shape=SDS((T, N), F32),
        compiler_params=_cp(("arbitrary", "arbitrary"), 40),
    )(a, b)


def _ffn_bwd_act(dfb, wd, hg, hu, name):
    T, D = dfb.shape
    F = wd.shape[0]
    bm = _blk(T, 1024, BF16_ROWS)
    bn = _blk(F, 512, LANES)

    def body(df_ref, wd_ref, hg_ref, hu_ref, dhg_ref, dhu_ref):
        dact = _dot(df_ref[...], wd_ref[...], NT)
        hgv = hg_ref[...].astype(F32)
        huv = hu_ref[...].astype(F32)
        s = jax.nn.sigmoid(hgv)
        dhu_ref[...] = (dact * (hgv * s)).astype(BF16)
        dhg_ref[...] = (dact * huv * (s * (1.0 + hgv * (1.0 - s)))).astype(BF16)

    tile = pl.BlockSpec((bm, bn), lambda i, j: (i, j))
    return pl.pallas_call(
        body, name=name, grid=(T // bm, F // bn),
        in_specs=[pl.BlockSpec((bm, D), lambda i, j: (i, 0)), pl.BlockSpec((bn, D), lambda i, j: (j, 0)), tile, tile],
        out_specs=[tile, tile], out_shape=[SDS((T, F), BF16)] * 2,
        compiler_params=_cp(("arbitrary", "arbitrary"), 40),
    )(dfb, wd, hg, hu)


def _dw_tn(a, b, bm_pref, name):
    T, M = a.shape
    N = b.shape[1]
    bm = _blk(M, bm_pref, LANES)
    tk = _blk(T, 512, BF16_ROWS)
    nk = T // tk

    def body(a_ref, b_ref, o_ref, acc):
        k = pl.program_id(1)
        p = _dot(a_ref[...], b_ref[...], TN)

        @pl.when(k == 0)
        def _():
            acc[...] = p

        @pl.when(k > 0)
        def _():
            acc[...] += p

        @pl.when(k == nk - 1)
        def _():
            o_ref[...] = acc[...].astype(BF16)

    return pl.pallas_call(
        body, name=name, grid=(M // bm, nk),
        in_specs=[pl.BlockSpec((tk, bm), lambda i, k: (k, i)), pl.BlockSpec((tk, N), lambda i, k: (k, 0))],
        out_specs=pl.BlockSpec((bm, N), lambda i, k: (i, 0)), out_shape=SDS((M, N), BF16),
        scratch_shapes=[pltpu.VMEM((bm, N), F32)],
        compiler_params=_cp(("arbitrary", "arbitrary"), 48),
    )(a, b)


def _mm_rmsbwd(pairs, x, gain, dx_in, bscale, name, carried=None):
    T, D = x.shape
    K = pairs[0][0].shape[1]
    npair = len(pairs)
    bm = _blk(T, 512, BF16_ROWS)
    bk = _blk(K, 512, LANES)
    nk = K // bk

    def body(*refs):
        ab = refs[:2 * npair]
        x_ref, g_ref, dxin_ref, dx_ref, dxb_ref, dg_ref, acc = refs[2 * npair:]
        i = pl.program_id(0)
        k = pl.program_id(1)
        p = _dot(ab[0][...], ab[1][...], NN)
        for q in range(1, npair):
            p = p + _dot(ab[2 * q][...], ab[2 * q + 1][...], NN)

        @pl.when(k == 0)
        def _():
            acc[...] = p

        @pl.when(k > 0)
        def _():
            acc[...] += p

        @pl.when(k == nk - 1)
        def _():
            @pl.when(i == 0)
            def _():
                dg_ref[...] = jnp.zeros_like(dg_ref)

            def chunk(c, _):
                r = _chunk_rows(c)
                dx, dg = _rms_bwd(x_ref[r, :], g_ref[...], acc[r, :])
                dxo = dxin_ref[r, :] + dx
                dx_ref[r, :] = dxo
                dxb_ref[r, :] = (bscale * dxo).astype(BF16)
                dg_ref[...] += dg
                return 0
            lax.fori_loop(0, bm // ROW_CHUNK, chunk, 0)

    row = pl.BlockSpec((bm, D), lambda i, k: (i, 0))
    vec = pl.BlockSpec((1, D), lambda i, k: (0, 0))
    in_specs = []
    args = []
    for a, b in pairs:
        in_specs += [pl.BlockSpec((bm, bk), lambda i, k: (i, k)), pl.BlockSpec((bk, D), lambda i, k: (k, 0))]
        args += [a, b]
    return _call(
        body, name=name, grid=(T // bm, nk),
        in_specs=in_specs + [row, vec, row], out_specs=[row, row, vec],
        out_shape=[SDS((T, D), F32), SDS((T, D), BF16), SDS((1, D), F32)],
        scratch_shapes=[pltpu.VMEM((bm, D), F32)],
        compiler_params=_cp(("arbitrary", "arbitrary"), 52),
        args=(*args, x, gain, dx_in), carried=carried)


def _loss_head(x3, gain, tgt, name):
    T, D = x3.shape
    bm = _blk(T, 256, BF16_ROWS)

    def body(x_ref, g_ref, t_ref, dx_ref, dxb_ref, dg_ref, loss_ref):
        i = pl.program_id(0)
        xv = x_ref[...]
        g = g_ref[...]
        out = xv * _rstd(xv) * g
        e = out - t_ref[...]
        part = 0.5 * jnp.sum(jnp.mean(e * e, axis=-1, keepdims=True), axis=0, keepdims=True)
        dx, dg = _rms_bwd(xv, g, e * (1.0 / D))
        dx_ref[...] = dx
        dxb_ref[...] = (0.5 * dx).astype(BF16)

        @pl.when(i == 0)
        def _():
            dg_ref[...] = dg
            loss_ref[...] = jnp.broadcast_to(part, loss_ref.shape)

        @pl.when(i > 0)
        def _():
            dg_ref[...] += dg
            loss_ref[...] += jnp.broadcast_to(part, loss_ref.shape)

    row = pl.BlockSpec((bm, D), lambda i: (i, 0))
    vec = pl.BlockSpec((1, D), lambda i: (0, 0))
    return pl.pallas_call(
        body, name=name, grid=(T // bm,),
        in_specs=[row, vec, row], out_specs=[row, row, vec, pl.BlockSpec((SUBLANES, LANES), lambda i: (0, 0))],
        out_shape=[SDS((T, D), F32), SDS((T, D), BF16), SDS((1, D), F32), SDS((SUBLANES, LANES), F32)],
        compiler_params=_cp(("arbitrary",), 40),
    )(x3, gain, tgt)


R_CW, R_CB, R_BA, R_BI, R_LAM, R_SW, R_GLO, R_GSO, SMALL_ROWS = 0, 4, 5, 6, 7, 8, 11, 12, 16


def _rows(g):
    return pl.ds(pl.multiple_of(g * SUBLANES, SUBLANES), SUBLANES)


def _shift_back(prev, cur, d):
    row = lax.broadcasted_iota(jnp.int32, cur.shape, 0)
    return pltpu.roll(jnp.where(row >= SUBLANES - d, prev, cur), d, 0)


def _shift_fwd(cur, nxt, d):
    row = lax.broadcasted_iota(jnp.int32, cur.shape, 0)
    return pltpu.roll(jnp.where(row < d, nxt, cur), SUBLANES - d, 0)


def _causal_conv(ext, g, taps_ref, ntap):
    prev = ext[_rows(g), :]
    cur = ext[_rows(g + 1), :]
    out = _shift_back(prev, cur, ntap - 1) * taps_ref[0:1, :]
    for k in range(1, ntap - 1):
        out = out + _shift_back(prev, cur, ntap - 1 - k) * taps_ref[k:k + 1, :]
    return out + cur * taps_ref[ntap - 1:ntap, :]


def _scan8(A, U, reverse):
    row = lax.broadcasted_iota(jnp.int32, A.shape, 0)
    for s in (1, 2, 4):
        if reverse:
            A_sh = pltpu.roll(A, SUBLANES - s, 0)
            U_sh = pltpu.roll(U, SUBLANES - s, 0)
            m = row < SUBLANES - s
        else:
            A_sh = pltpu.roll(A, s, 0)
            U_sh = pltpu.roll(U, s, 0)
            m = row >= s
        U = jnp.where(m, A * U_sh + U, U)
        A = jnp.where(m, A * A_sh, A)
    return A, U


def _gate_pre(xc_s, w_ref, out_s, H, hd):
    for h in range(H):
        cs = slice(h * hd, (h + 1) * hd)
        out_s[:, cs] = _dot(xc_s[:, cs].astype(BF16), w_ref[h], NN)


def _lru_coeffs(pa, pi, xc, ba, bi, sp):
    ra = jax.nn.sigmoid(pa + ba)
    ri = jax.nn.sigmoid(pi + bi)
    log_a = (-LRU_C * ra) * sp
    a = jnp.exp(log_a)
    mult = jnp.sqrt(-_expm1(2.0 * log_a))
    return ra, ri, a, mult


def _softplus_neg(lam):
    v = -lam
    return jnp.maximum(v, 0.0) + _log1p(jnp.exp(-jnp.abs(v)))


def _mix_fwd(z, cw, cb, wa, ba, wi, bi, lam, sw, glo, gso, name):
    T = z.shape[0]
    C = z.shape[1] // 5
    H = wa.shape[0]
    hd = C // H
    tb = _blk(T, 256, BF16_ROWS)
    ng = tb // SUBLANES
    HDR = SUBLANES

    def body(z_ref, cw_ref, cb_ref, wa_ref, ba_ref, wi_ref, bi_ref, lam_ref, sw_ref, glo_ref, gso_ref,
             y_ref, h_ref, xext, pext, xc_s, pa_s, pi_s, y_s, hcar):
        @pl.when(pl.program_id(0) == 0)
        def _():
            xext[0:HDR, :] = jnp.zeros((HDR, C), F32)
            pext[0:HDR, :] = jnp.zeros((HDR, C), F32)
            hcar[...] = jnp.zeros_like(hcar)

        def fill(g, _):
            r = _rows(g)
            re = _rows(g + 1)
            xext[re, :] = z_ref[r, 0:C]
            pext[re, :] = z_ref[r, 3 * C:4 * C] * z_ref[r, 4 * C:5 * C]
            return 0
        lax.fori_loop(0, ng, fill, 0)

        def conv(g, _):
            xc_s[_rows(g), :] = _causal_conv(xext, g, cw_ref, 4) + cb_ref[...]
            return 0
        lax.fori_loop(0, ng, conv, 0)

        _gate_pre(xc_s, wa_ref, pa_s, H, hd)
        _gate_pre(xc_s, wi_ref, pi_s, H, hd)
        sp = _softplus_neg(lam_ref[...])

        def group(g, hprev):
            r = _rows(g)
            xc = xc_s[r, :]
            _, ri, a, mult = _lru_coeffs(pa_s[r, :], pi_s[r, :], xc, ba_ref[...], bi_ref[...], sp)
            A, U = _scan8(a, mult * (ri * xc), reverse=False)
            hh = A * hprev + U
            h_ref[r, :] = hh
            gel, _ = _gelu_parts(z_ref[r, C:2 * C])
            y_lru = hh * gel
            y_s[r, 0:C] = y_lru * _rstd(y_lru) * glo_ref[...]
            y_sc = z_ref[r, 2 * C:3 * C] * _causal_conv(pext, g, sw_ref, 3)
            y_s[r, C:2 * C] = y_sc * _rstd(y_sc) * gso_ref[...]
            return jnp.broadcast_to(hh[SUBLANES - 1:SUBLANES, :], hh.shape)
        hcar[...] = lax.fori_loop(0, ng, group, hcar[...])

        xext[0:HDR, :] = xext[tb:tb + HDR, :]
        pext[0:HDR, :] = pext[tb:tb + HDR, :]

        def cast(g, _):
            r = pl.ds(pl.multiple_of(g * BF16_ROWS, BF16_ROWS), BF16_ROWS)
            y_ref[r, :] = y_s[r, :].astype(BF16)
            return 0
        lax.fori_loop(0, tb // BF16_ROWS, cast, 0)

    full = lambda shape: pl.BlockSpec(shape, lambda i: (0,) * len(shape))
    blk = lambda w: pl.BlockSpec((tb, w), lambda i: (i, 0))
    ext = pltpu.VMEM((tb + HDR, C), F32)
    tile = pltpu.VMEM((tb, C), F32)
    return pl.pallas_call(
        body, name=name, grid=(T // tb,),
        in_specs=[blk(5 * C), full((4, C)), full((1, C)), full((H, hd, hd)), full((1, C)), full((H, hd, hd)),
                  full((1, C)), full((1, C)), full((3, C)), full((1, C)), full((1, C))],
        out_specs=[blk(2 * C), blk(C)],
        out_shape=[SDS((T, 2 * C), BF16), SDS((T, C), F32)],
        scratch_shapes=[ext, ext, tile, tile, tile, pltpu.VMEM((tb, 2 * C), F32), pltpu.VMEM((SUBLANES, C), F32)],
        compiler_params=_cp(("arbitrary",), 40),
    )(z, cw, cb, wa, ba, wi, bi, lam, sw, glo, gso)


def _mix_bwd(z, h, dy, cw, cb, wa, ba, wi, bi, lam, sw, glo, gso, name):
    T = z.shape[0]
    C = z.shape[1] // 5
    H = wa.shape[0]
    hd = C // H
    tb = _blk(T, 256, BF16_ROWS)
    nb = T // tb
    ng = tb // SUBLANES
    HDR = SUBLANES
    N_ACC = 13

    def body(z_ref, zp_ref, h_ref, hp_ref, dy_ref, cw_ref, cb_ref, wa_ref, ba_ref, wi_ref, bi_ref, lam_ref,
             sw_ref, glo_ref, gso_ref, dz_ref, small_ref, dwa_ref, dwi_ref,
             xext, pext, hext, dqext, dxcext, bext, xc_s, pa_s, pi_s, a_s, m_s, ri_s, dh_s, dpa_s, dpi_s,
             dz_s, acc_s, bcar):
        i = pl.program_id(0)
        first_rows = i == nb - 1

        @pl.when(i == 0)
        def _():
            dqext[tb:tb + HDR, :] = jnp.zeros((HDR, C), F32)
            dxcext[tb:tb + HDR, :] = jnp.zeros((HDR, C), F32)
            bcar[...] = jnp.zeros_like(bcar)
            acc_s[...] = jnp.zeros_like(acc_s)
            dwa_ref[...] = jnp.zeros_like(dwa_ref)
            dwi_ref[...] = jnp.zeros_like(dwi_ref)

        zero = jnp.zeros((HDR, C), F32)
        xext[0:HDR, :] = jnp.where(first_rows, zero, zp_ref[:, 0:C])
        pext[0:HDR, :] = jnp.where(first_rows, zero, zp_ref[:, 3 * C:4 * C] * zp_ref[:, 4 * C:5 * C])
        hext[0:HDR, :] = jnp.where(first_rows, zero, hp_ref[...])

        def fill(g, _):
            r = _rows(g)
            re = _rows(g + 1)
            xext[re, :] = z_ref[r, 0:C]
            pext[re, :] = z_ref[r, 3 * C:4 * C] * z_ref[r, 4 * C:5 * C]
            hext[re, :] = h_ref[r, :]
            return 0
        lax.fori_loop(0, ng, fill, 0)

        def conv(g, _):
            xc_s[_rows(g), :] = _causal_conv(xext, g, cw_ref, 4) + cb_ref[...]
            return 0
        lax.fori_loop(0, ng, conv, 0)

        _gate_pre(xc_s, wa_ref, pa_s, H, hd)
        _gate_pre(xc_s, wi_ref, pi_s, H, hd)
        sp = _softplus_neg(lam_ref[...])
        dsp_dlam = -jax.nn.sigmoid(-lam_ref[...])

        def add_acc(k, v):
            acc_s[k] += v

        def p1(g, _):
            r = _rows(g)
            xc = xc_s[r, :]
            _, ri, a, mult = _lru_coeffs(pa_s[r, :], pi_s[r, :], xc, ba_ref[...], bi_ref[...], sp)
            a_s[r, :] = a
            m_s[r, :] = mult
            ri_s[r, :] = ri
            hh = h_ref[r, :]
            gel, gelp = _gelu_parts(z_ref[r, C:2 * C])
            y_lru = hh * gel
            dnl = dy_ref[r, 0:C]
            ylr = y_lru * _rstd(y_lru)
            gd = glo_ref[...] * dnl
            dy_lru = _rstd(y_lru) * (gd - ylr * jnp.mean(gd * ylr, axis=-1, keepdims=True))
            add_acc(R_GLO, dnl * ylr)
            dz_s[r, C:2 * C] = dy_lru * hh * gelp
            dh = dy_lru * gel
            dh_s[r, :] = dh

            q = _causal_conv(pext, g, sw_ref, 3)
            scb = z_ref[r, 2 * C:3 * C]
            y_sc = scb * q
            dns = dy_ref[r, C:2 * C]
            ysr = y_sc * _rstd(y_sc)
            gs = gso_ref[...] * dns
            dy_sc = _rstd(y_sc) * (gs - ysr * jnp.mean(gs * ysr, axis=-1, keepdims=True))
            add_acc(R_GSO, dns * ysr)
            dz_s[r, 2 * C:3 * C] = dy_sc * q
            dqext[r, :] = dy_sc * scb
            return 0
        lax.fori_loop(0, ng, p1, 0)

        bext[tb:tb + HDR, :] = bcar[...]

        def p2(j, carry):
            g = ng - 1 - j
            r = _rows(g)
            a = a_s[r, :]
            A, U = _scan8(a, a * dh_s[r, :], reverse=True)
            bb = A * carry + U
            bext[r, :] = bb
            return jnp.broadcast_to(bb[0:1, :], bb.shape)
        bcar[...] = lax.fori_loop(0, ng, p2, bcar[...])

        def p3(g, _):
            r = _rows(g)
            rn = _rows(g + 1)
            G = dh_s[r, :] + _shift_fwd(bext[r, :], bext[rn, :], 1)
            hm1 = _shift_back(hext[r, :], hext[rn, :], 1)
            a = a_s[r, :]
            mult = m_s[r, :]
            ri = ri_s[r, :]
            xc = xc_s[r, :]
            ra = jax.nn.sigmoid(pa_s[r, :] + ba_ref[...])
            dxcext[r, :] = G * mult * ri
            dri = G * mult * xc
            dmult = G * ri * xc
            dlog_a = (G * hm1) * a - dmult * (a * a) / mult
            add_acc(R_LAM, dlog_a * (-LRU_C * ra) * dsp_dlam)
            dpa = dlog_a * (-LRU_C * sp) * ra * (1.0 - ra)
            dpi = dri * ri * (1.0 - ri)
            add_acc(R_BA, dpa)
            add_acc(R_BI, dpi)
            dpa_s[r, :] = dpa
            dpi_s[r, :] = dpi
            return 0
        lax.fori_loop(0, ng, p3, 0)

        for hh_ in range(H):
            cs = slice(hh_ * hd, (hh_ + 1) * hd)
            dpa_b = dpa_s[:, cs].astype(BF16)
            dpi_b = dpi_s[:, cs].astype(BF16)
            xc_b = xc_s[:, cs].astype(BF16)
            dxcext[0:tb, cs] += _dot(dpa_b, wa_ref[hh_], NT) + _dot(dpi_b, wi_ref[hh_], NT)
            dwa_ref[hh_] += _dot(xc_b, dpa_b, TN)
            dwi_ref[hh_] += _dot(xc_b, dpi_b, TN)

        def p4(g, _):
            r = _rows(g)
            rn = _rows(g + 1)
            dxc = dxcext[r, :]
            dxc_n = dxcext[rn, :]
            x_p = xext[r, :]
            x_c = xext[rn, :]
            add_acc(R_CB, dxc)
            dlx = dxc * cw_ref[3:4, :]
            add_acc(R_CW + 3, dxc * x_c)
            for d in range(1, 4):
                dlx = dlx + _shift_fwd(dxc, dxc_n, d) * cw_ref[3 - d:4 - d, :]
                add_acc(R_CW + 3 - d, dxc * _shift_back(x_p, x_c, d))
            dz_s[r, 0:C] = dlx

            dq = dqext[r, :]
            dq_n = dqext[rn, :]
            p_p = pext[r, :]
            p_c = pext[rn, :]
            dp = dq * sw_ref[2:3, :]
            add_acc(R_SW + 2, dq * p_c)
            for d in range(1, 3):
                dp = dp + _shift_fwd(dq, dq_n, d) * sw_ref[2 - d:3 - d, :]
                add_acc(R_SW + 2 - d, dq * _shift_back(p_p, p_c, d))
            dz_s[r, 3 * C:4 * C] = dp * z_ref[r, 4 * C:5 * C]
            dz_s[r, 4 * C:5 * C] = dp * z_ref[r, 3 * C:4 * C]
            return 0
        lax.fori_loop(0, ng, p4, 0)

        dqext[tb:tb + HDR, :] = dqext[0:HDR, :]
        dxcext[tb:tb + HDR, :] = dxcext[0:HDR, :]

        def cast(g, _):
            r = pl.ds(pl.multiple_of(g * BF16_ROWS, BF16_ROWS), BF16_ROWS)
            dz_ref[r, :] = dz_s[r, :].astype(BF16)
            return 0
        lax.fori_loop(0, tb // BF16_ROWS, cast, 0)

        @pl.when(i == nb - 1)
        def _():
            small_ref[...] = jnp.zeros_like(small_ref)
            for k in range(N_ACC):
                small_ref[k:k + 1, :] = jnp.sum(acc_s[k], axis=0, keepdims=True)

    tpg = tb // SUBLANES
    full = lambda shape: pl.BlockSpec(shape, lambda i: (0,) * len(shape))
    blk = lambda w: pl.BlockSpec((tb, w), lambda i: (nb - 1 - i, 0))
    prev = lambda w: pl.BlockSpec((SUBLANES, w), lambda i: (jnp.maximum((nb - 1 - i) * tpg - 1, 0), 0))
    ext = pltpu.VMEM((tb + HDR, C), F32)
    tile = pltpu.VMEM((tb, C), F32)
    return pl.pallas_call(
        body, name=name, grid=(nb,),
        in_specs=[blk(5 * C), prev(5 * C), blk(C), prev(C), blk(2 * C), full((4, C)), full((1, C)), full((H, hd, hd)),
                  full((1, C)), full((H, hd, hd)), full((1, C)), full((1, C)), full((3, C)), full((1, C)), full((1, C))],
        out_specs=[blk(5 * C), full((SMALL_ROWS, C)), full((H, hd, hd)), full((H, hd, hd))],
        out_shape=[SDS((T, 5 * C), BF16), SDS((SMALL_ROWS, C), F32), SDS((H, hd, hd), F32), SDS((H, hd, hd), F32)],
        scratch_shapes=[ext] * 6 + [tile] * 9 + [pltpu.VMEM((tb, 5 * C), F32), pltpu.VMEM((N_ACC, SUBLANES, C), F32),
                                                pltpu.VMEM((SUBLANES, C), F32)],
        compiler_params=_cp(("arbitrary",), 56),
    )(z, z, h, h, dy, cw, cb, wa, ba, wi, bi, lam, sw, glo, gso)


def _add_slabs(terms, out_dtype, name):
    R, Ccols = terms[0].shape
    br = _blk(R, 512, BF16_ROWS)
    n = len(terms)

    def body(*refs):
        s = refs[0][...].astype(F32)
        for t_ref in refs[1:n]:
            s = s + t_ref[...].astype(F32)
        refs[n][...] = s.astype(out_dtype)

    spec = pl.BlockSpec((br, Ccols), lambda i: (i, 0))
    return pl.pallas_call(
        body, name=name, grid=(R // br,), in_specs=[spec] * n, out_specs=spec, out_shape=SDS((R, Ccols), out_dtype),
        compiler_params=_cp(("arbitrary",), 40),
    )(*terms)


def _final_grad(sb, lb, chip, name):
    _, R, Ccols = sb.shape
    br = _blk(R, 512, BF16_ROWS)

    def body(chip_ref, sb_ref, l0, l1, l2, o_ref):
        s = sb_ref[0].astype(F32)
        for t_ref in (l0, l1, l2):
            s = s + t_ref[0].astype(F32)
        o_ref[...] = s

    lspec = lambda k: pl.BlockSpec((1, br, Ccols), lambda i, c: (k, i, 0))
    return pl.pallas_call(
        body, name=name,
        grid_spec=pltpu.PrefetchScalarGridSpec(
            num_scalar_prefetch=1, grid=(R // br,),
            in_specs=[pl.BlockSpec((1, br, Ccols), lambda i, c: (c[0], i, 0)), lspec(0), lspec(1), lspec(2)],
            out_specs=pl.BlockSpec((br, Ccols), lambda i, c: (i, 0))),
        out_shape=SDS((R, Ccols), F32),
        compiler_params=_cp(("arbitrary",), 40),
    )(chip, sb, lb, lb, lb)


def _adamw(w, g, m, v, name):
    R, Ccols = w.shape
    br = _blk(R, 256, SUBLANES)
    c1 = 1.0 - ADAM_B1 ** ADAM_STEP
    c2 = 1.0 - ADAM_B2 ** ADAM_STEP

    def body(w_ref, g_ref, m_ref, v_ref, d_ref, nm_ref, nv_ref):
        gv = g_ref[...]
        nm = ADAM_B1 * m_ref[...] + (1.0 - ADAM_B1) * gv
        nv = ADAM_B2 * v_ref[...] + (1.0 - ADAM_B2) * (gv * gv)
        nm_ref[...] = nm
        nv_ref[...] = nv
        d_ref[...] = -ADAM_LR * ((nm / c1) / (jnp.sqrt(nv / c2) + ADAM_EPS) + ADAM_WD * w_ref[...])

    spec = pl.BlockSpec((br, Ccols), lambda i: (i, 0))
    return pl.pallas_call(
        body, name=name, grid=(R // br,), in_specs=[spec] * 4, out_specs=[spec] * 3,
        out_shape=[SDS((R, Ccols), F32)] * 3, compiler_params=_cp(("arbitrary",), 40),
    )(w, g, m, v)


def _place():
    return lax.axis_index("x"), lax.axis_index("y"), lax.axis_index("c")


def _dev_rows(ref, dev, rows):
    return ref.at[pl.ds((4 * dev[0] + 2 * dev[1] + dev[2]) * rows, rows), :]


def _remote(src, dst, send_sem, recv_sem, to):
    return pltpu.make_async_remote_copy(src_ref=src, dst_ref=dst, send_sem=send_sem, recv_sem=recv_sem,
                                        device_id=to, device_id_type=MESH)


def _ag_direct_phase(slab, pieces):
    W = slab.shape[1]
    n = len(pieces)

    def build(ins, outs, sems, starting):
        (slab_ref,) = ins
        send_sems, recv_sems, local_sems = sems
        x, y, c = _place()
        me = (x, y, c)
        peers = [(x, y, 1 - c), (1 - x, y, c), (x, 1 - y, c), (1 - x, 1 - y, c)]
        todo = []
        for p, (off, rows) in enumerate(pieces):
            src = slab_ref.at[pl.ds(off, rows), :]
            mine = pltpu.make_async_copy(src, _dev_rows(outs[p], me, rows), local_sems.at[p])
            todo.append(mine.start if starting else mine.wait)
            for k, peer in enumerate(peers):
                snd = _remote(src, _dev_rows(outs[p], me, rows), send_sems.at[k, p], recv_sems.at[k, p], peer)
                if starting:
                    todo.append(snd.start)
                else:
                    theirs = _dev_rows(outs[p], peer, rows)
                    rcv = _remote(theirs, theirs, send_sems.at[k, p], recv_sems.at[k, p], me)
                    todo += [rcv.wait_recv, snd.wait_send]
        return todo

    dma = pltpu.SemaphoreType.DMA
    return _Carried([slab], [SDS((N_DEV * rows, W), slab.dtype) for _, rows in pieces], {},
                    [dma((4, n)), dma((4, n)), dma((n,))], build)


def _ag_forward_phase(gathered, pieces):
    n = len(pieces)

    def build(ins, outs, sems, starting):
        send_sems, recv_sems = sems
        x, y, c = _place()
        me, sibling = (x, y, c), (x, y, 1 - c)
        chips = [(1 - x, y), (x, 1 - y), (1 - x, 1 - y)]
        todo = []
        for p, (_, rows) in enumerate(pieces):
            for j, chip in enumerate(chips):
                snd = _remote(_dev_rows(ins[p], (*chip, c), rows), _dev_rows(outs[p], (*chip, c), rows),
                              send_sems.at[j, p], recv_sems.at[j, p], sibling)
                if starting:
                    todo.append(snd.start)
                else:
                    theirs = _dev_rows(outs[p], (*chip, 1 - c), rows)
                    rcv = _remote(theirs, theirs, send_sems.at[j, p], recv_sems.at[j, p], me)
                    todo += [rcv.wait_recv, snd.wait_send]
        return todo

    dma = pltpu.SemaphoreType.DMA
    return _Carried(gathered, [SDS(g.shape, g.dtype) for g in gathered], {p: p for p in range(n)},
                    [dma((3, n)), dma((3, n))], build)


def _rs_chips_phase(sb):
    _, R, W = sb.shape

    def build(ins, outs, sems, starting):
        (sb_ref,), (land_ref,) = ins, outs
        send_sems, recv_sems = sems
        x, y, c = _place()
        chips = [(1 - x, y), (x, 1 - y), (1 - x, 1 - y)]
        cps = [_remote(sb_ref.at[2 * chip[0] + chip[1]], land_ref.at[j], send_sems.at[j], recv_sems.at[j], (*chip, c))
               for j, chip in enumerate(chips)]
        if starting:
            return [cp.start for cp in cps]
        return [cp.wait_recv for cp in cps] + [cp.wait_send for cp in cps]

    dma = pltpu.SemaphoreType.DMA
    return _Carried([sb], [SDS((3, R, W), sb.dtype)], {}, [dma((3,)), dma((3,))], build)


def _allgather(slab, pieces, name):
    R, W = slab.shape
    n = len(pieces)
    assert sum(rows for _, rows in pieces) == R

    def body(slab_ref, *refs):
        outs = refs[:n]
        send_sems, recv_sems, local_sems = refs[n:]
        x, y, c = _place()
        me, sibling = (x, y, c), (x, y, 1 - c)
        chips = [(1 - x, y), (x, 1 - y), (1 - x, 1 - y)]

        def dst_rows(p, origin):
            rows = pieces[p][1]
            start = (4 * origin[0] + 2 * origin[1] + origin[2]) * rows
            return outs[p].at[pl.ds(start, rows), :]

        def copies(k, origin, to, from_slab):
            out = []
            for p, (off, rows) in enumerate(pieces):
                dst = dst_rows(p, origin)
                src = slab_ref.at[pl.ds(off, rows), :] if from_slab else dst
                out.append(pltpu.make_async_remote_copy(
                    src_ref=src, dst_ref=dst, send_sem=send_sems.at[k, p], recv_sem=recv_sems.at[k, p],
                    device_id=to, device_id_type=MESH))
            return out

        mine = [pltpu.make_async_copy(slab_ref.at[pl.ds(off, rows), :], dst_rows(p, me), local_sems.at[p])
                for p, (off, rows) in enumerate(pieces)]
        for cp in mine:
            cp.start()
        first = copies(0, me, sibling, True)
        for j, chip in enumerate(chips):
            first += copies(1 + j, me, (*chip, c), True)
        for cp in first:
            cp.start()
        passed = []
        for j, chip in enumerate(chips):
            for cp in copies(1 + j, (*chip, c), me, False):
                cp.wait_recv()
            fwd = copies(4 + j, (*chip, c), sibling, False)
            for cp in fwd:
                cp.start()
            passed += fwd
        for cp in copies(0, sibling, me, False):
            cp.wait_recv()
        for j, chip in enumerate(chips):
            for cp in copies(4 + j, (*chip, 1 - c), me, False):
                cp.wait_recv()
        for cp in first + passed:
            cp.wait_send()
        for cp in mine:
            cp.wait()

    return pl.pallas_call(
        body, name=name,
        in_specs=[HBM_SPEC], out_specs=[HBM_SPEC] * n,
        out_shape=[SDS((N_DEV * rows, W), slab.dtype) for _, rows in pieces],
        scratch_shapes=[pltpu.SemaphoreType.DMA((7, n)), pltpu.SemaphoreType.DMA((7, n)), pltpu.SemaphoreType.DMA((n,))],
    )(slab)


def _rs_sibling(grads, pieces, name):
    W = grads[0].shape[1]
    R = sum(rows for _, rows in pieces)
    n = len(pieces)
    dt = grads[0].dtype
    max_rows = max(rows for _, rows in pieces)
    steps = [(q, p) for q in range(N_CHIP) for p in range(n)]
    ns = len(steps)
    ADD_ROWS = 64
    assert all(rows % ADD_ROWS == 0 for _, rows in pieces)

    def body(*refs):
        g_refs = refs[:n]
        sb_ref, mine_buf, send_buf, land_buf, out_buf, in_sems, out_sems, send_sems, recv_sems, credit = refs[n:]
        x, y, c = _place()
        sibling = (x, y, 1 - c)

        def loads(s):
            q, p = steps[s]
            rows = pieces[p][1]
            slot = s % 2
            mine = g_refs[p].at[pl.ds((2 * q + c) * rows, rows), :]
            theirs = g_refs[p].at[pl.ds((2 * q + 1 - c) * rows, rows), :]
            return (pltpu.make_async_copy(mine, mine_buf.at[slot, pl.ds(0, rows), :], in_sems.at[slot, 0]),
                    pltpu.make_async_copy(theirs, send_buf.at[slot, pl.ds(0, rows), :], in_sems.at[slot, 1]))

        def send(s):
            rows = pieces[steps[s][1]][1]
            slot = s % 2
            return pltpu.make_async_remote_copy(
                src_ref=send_buf.at[slot, pl.ds(0, rows), :], dst_ref=land_buf.at[slot, pl.ds(0, rows), :],
                send_sem=send_sems.at[slot], recv_sem=recv_sems.at[slot], device_id=sibling, device_id_type=MESH)

        def store(s):
            q, p = steps[s]
            off, rows = pieces[p]
            slot = s % 2
            return pltpu.make_async_copy(out_buf.at[slot, pl.ds(0, rows), :], sb_ref.at[q, pl.ds(off, rows), :],
                                         out_sems.at[slot])

        for cp in loads(0):
            cp.start()
        for s in range(ns):
            slot = s % 2
            rows = pieces[steps[s][1]][1]
            if s + 1 < ns:
                if s >= 1:
                    send(s - 1).wait_send()
                for cp in loads(s + 1):
                    cp.start()
            for cp in loads(s):
                cp.wait()
            if s >= 2:
                pl.semaphore_wait(credit.at[slot], 1)
            send(s).start()
            send(s).wait_recv()
            if s >= 2:
                store(s - 2).wait()

            def add(k, _, slot=slot):
                r = pl.ds(pl.multiple_of(k * ADD_ROWS, ADD_ROWS), ADD_ROWS)
                out_buf[slot, r, :] = (mine_buf[slot, r, :].astype(F32) + land_buf[slot, r, :].astype(F32)).astype(dt)
                return 0
            lax.fori_loop(0, rows // ADD_ROWS, add, 0)
            if s + 2 < ns:
                pl.semaphore_signal(credit.at[slot], inc=1, device_id=sibling, device_id_type=MESH)
            store(s).start()
        for s in range(max(ns - 2, 0), ns):
            send(s).wait_send()
            store(s).wait()

    buf = pltpu.VMEM((2, max_rows, W), dt)
    return pl.pallas_call(
        body, name=name,
        in_specs=[HBM_SPEC] * n, out_specs=HBM_SPEC,
        out_shape=SDS((N_CHIP, R, W), dt),
        scratch_shapes=[buf, buf, buf, buf, pltpu.SemaphoreType.DMA((2, 2)), pltpu.SemaphoreType.DMA((2,)),
                        pltpu.SemaphoreType.DMA((2,)), pltpu.SemaphoreType.DMA((2,)), pltpu.SemaphoreType.REGULAR((2,))],
        compiler_params=pltpu.CompilerParams(vmem_limit_bytes=40 * MIB),
    )(*grads)


SMALL_NAMES = ("ffn1_norm", "mix_norm", "ffn2_norm", "final_norm", "lru_conv_w", "lru_conv_b", "lru_w_a", "lru_b_a",
               "lru_w_i", "lru_b_i", "lru_lambda", "sc_conv_w", "lru_out_norm", "sc_out_norm")
WEIGHT_NAMES = ("ffn1_norm", "ffn1_w_gate", "ffn1_w_up", "ffn1_w_down", "mix_norm", "w_in", "lru_conv_w", "lru_conv_b",
                "lru_w_a", "lru_b_a", "lru_w_i", "lru_b_i", "lru_lambda", "sc_conv_w", "lru_out_norm", "sc_out_norm",
                "w_out", "ffn2_norm", "ffn2_w_gate", "ffn2_w_up", "ffn2_w_down", "final_norm")
BIG = (("ffn1_w_gate", True), ("ffn1_w_up", True), ("ffn1_w_down", False), ("ffn2_w_gate", True), ("ffn2_w_up", True),
       ("ffn2_w_down", False), ("w_in", True), ("w_out", False))


SLAB_ROW_ALIGN = 256


def _pack_rows(parts, width):
    rows, counts = [], []
    for p in parts:
        flat = p.reshape(-1)
        nr = -(-flat.shape[0] // width)
        nr = -(-nr // SUBLANES) * SUBLANES
        rows.append(jnp.pad(flat, (0, nr * width - flat.shape[0])).reshape(nr, width))
        counts.append(nr)
    total = sum(counts)
    pad = -(-total // SLAB_ROW_ALIGN) * SLAB_ROW_ALIGN - total
    if pad:
        rows.append(jnp.zeros((pad, width), rows[0].dtype))
    return jnp.concatenate(rows, axis=0), counts


def _stack_rows(blocks):
    pieces, off = [], 0
    for b in blocks:
        pieces.append((off, b.shape[0]))
        off += b.shape[0]
    return jnp.concatenate(blocks, axis=0), pieces


def _unpack_rows(slab, counts, shapes):
    out, r = [], 0
    for nr, shape in zip(counts, shapes):
        size = math.prod(shape)
        out.append(slab[r:r + nr].reshape(-1)[:size].reshape(shape))
        r += nr
    return out


def kernel(x, ffn1_norm, ffn1_w_gate, ffn1_w_up, ffn1_w_down, mix_norm, w_in, lru_conv_w, lru_conv_b, lru_w_a, lru_b_a, lru_w_i, lru_b_i, lru_lambda, sc_conv_w, lru_out_norm, sc_out_norm, w_out, ffn2_norm, ffn2_w_gate, ffn2_w_up, ffn2_w_down, final_norm, loss_target, m_ffn1_norm, m_ffn1_w_gate, m_ffn1_w_up, m_ffn1_w_down, m_mix_norm, m_w_in, m_lru_conv_w, m_lru_conv_b, m_lru_w_a, m_lru_b_a, m_lru_w_i, m_lru_b_i, m_lru_lambda, m_sc_conv_w, m_lru_out_norm, m_sc_out_norm, m_w_out, m_ffn2_norm, m_ffn2_w_gate, m_ffn2_w_up, m_ffn2_w_down, m_final_norm, v_ffn1_norm, v_ffn1_w_gate, v_ffn1_w_up, v_ffn1_w_down, v_mix_norm, v_w_in, v_lru_conv_w, v_lru_conv_b, v_lru_w_a, v_lru_b_a, v_lru_w_i, v_lru_b_i, v_lru_lambda, v_sc_conv_w, v_lru_out_norm, v_sc_out_norm, v_w_out, v_ffn2_norm, v_ffn2_w_gate, v_ffn2_w_up, v_ffn2_w_down, v_final_norm):
    a = dict(locals())
    w = {n: a[n] for n in WEIGHT_NAMES}
    m = {n: a["m_" + n] for n in WEIGHT_NAMES}
    v = {n: a["v_" + n] for n in WEIGHT_NAMES}
    ax, ay, ac = _place()
    dev = 4 * ax + 2 * ay + ac
    chip = (2 * ax + ay).astype(jnp.int32).reshape(1)

    x0 = x[0]
    tgt = loss_target[0]
    T, D = x0.shape
    C = D // 2
    H, hd = lru_w_a.shape[1], lru_w_a.shape[2]
    CL = lru_conv_w.shape[2]

    shards = []
    for name, transposed in BIG:
        s = w[name][0]
        shards.append((s.T if transposed else s).astype(BF16))
    taps = jnp.concatenate([lru_conv_w[0], sc_conv_w[0], jnp.zeros((1, CL), F32)], axis=0)
    taps_row = lax.bitcast_convert_type(taps, BF16).reshape(1, -1)
    taps_blk = jnp.pad(taps_row, ((0, BF16_ROWS - 1), (0, D - taps_row.shape[1])))
    slab_a, pieces_a = _stack_rows(shards[:3])
    slab_b, pieces_b = _stack_rows(shards[3:] + [taps_blk])
    wg1, wu1, wd1 = _allgather(slab_a, pieces_a, "allgather_ffn1")
    rest_direct = _ag_direct_phase(slab_b, pieces_b)

    g1, gm, g3 = ffn1_norm, mix_norm, ffn2_norm
    (n1, hg1, hu1, act1), rest = _norm_proj(x0, g1, [wg1, wu1], [BF16, BF16], True, "ffn1_up", carried=rest_direct)
    x1, rest = _mm_res(act1, wd1, x0, 0.5, "ffn1_down", carried=_ag_forward_phase(rest, pieces_b))
    wg2, wu2, wd2, win, wout, taps_all = rest
    taps_all = taps_all.reshape(N_DEV, BF16_ROWS, D)[:, 0, :2 * SUBLANES * CL].reshape(N_DEV, SUBLANES, CL, 2)
    taps_all = lax.bitcast_convert_type(taps_all, F32)
    taps_all = taps_all.transpose(1, 0, 2).reshape(SUBLANES, N_DEV * CL)
    cw, sw = taps_all[0:4], taps_all[4:7]

    gf = final_norm.reshape(1, D)
    cb = lru_conv_b
    wa, wi = lru_w_a[0].astype(BF16), lru_w_i[0].astype(BF16)
    ba, bi = lru_b_a.reshape(1, C), lru_b_i.reshape(1, C)
    lam, glo, gso = lru_lambda, lru_out_norm, sc_out_norm

    n2, z = _norm_proj(x1, gm, [win], [F32], False, "in_proj")
    y, h = _mix_fwd(z, cw, cb, wa, ba, wi, bi, lam, sw, glo, gso, "mix_fwd")
    x2 = _mm_res(y, wout, x1, 1.0, "out_proj")
    n3, hg2, hu2, act2 = _norm_proj(x2, g3, [wg2, wu2], [BF16, BF16], True, "ffn2_up")
    x3 = _mm_res(act2, wd2, x2, 0.5, "ffn2_down")
    dx3, df2, d_gf, loss_blk = _loss_head(x3, gf, tgt, "loss_head")

    F = wd1.shape[0]
    bm_f = F // 4 if (F // 4) % LANES == 0 else 512

    def reduce_group(gs, tag):
        pcs, off = [], 0
        for g_ in gs:
            pcs.append((off, g_.shape[0] // N_DEV))
            off += g_.shape[0] // N_DEV
        sb_ = _rs_sibling(gs, pcs, "rs_sibling_add_" + tag)
        return sb_, pcs

    dhg2, dhu2 = _ffn_bwd_act(df2, wd2, hg2, hu2, "ffn2_bwd_act")
    d_wd2 = _dw_tn(act2, df2, bm_f, "ffn2_dw_down")
    d_wg2 = _dw_tn(dhg2, n3, bm_f, "ffn2_dw_gate")
    d_wu2 = _dw_tn(dhu2, n3, bm_f, "ffn2_dw_up")
    sb_f2, pcs_f2 = reduce_group([d_wg2, d_wu2, d_wd2], "ffn2")
    (dx2, dx2b, d_g3), (lb_f2,) = _mm_rmsbwd([(dhg2, wg2), (dhu2, wu2)], x2, g3, dx3, 1.0, "ffn2_bwd_in",
                                             carried=_rs_chips_phase(sb_f2))
    dy = _mm_nt(dx2b, wout, "out_proj_bwd")
    d_wout = _dw_tn(y, dx2b, 1024, "out_proj_dw")
    dz, small, d_wa, d_wi = _mix_bwd(z, h, dy, cw, cb, wa, ba, wi, bi, lam, sw, glo, gso, "mix_bwd")
    d_win = _dw_tn(dz, n2, 1280, "in_proj_dw")
    sb_mx, pcs_mx = reduce_group([d_win, d_wout], "mix")
    (dx1, df1, d_gm), (lb_mx,) = _mm_rmsbwd([(dz, win)], x1, gm, dx2, 0.5, "in_proj_bwd",
                                            carried=_rs_chips_phase(sb_mx))
    dhg1, dhu1 = _ffn_bwd_act(df1, wd1, hg1, hu1, "ffn1_bwd_act")
    d_wd1 = _dw_tn(act1, df1, bm_f, "ffn1_dw_down")
    d_wg1 = _dw_tn(dhg1, n1, bm_f, "ffn1_dw_gate")
    d_wu1 = _dw_tn(dhu1, n1, bm_f, "ffn1_dw_up")
    sb_f1, pcs_f1 = reduce_group([d_wg1, d_wu1, d_wd1], "ffn1")
    (dx0, _, d_g1), (lb_f1,) = _mm_rmsbwd([(dhg1, wg1), (dhu1, wu1)], x0, g1, dx1, 1.0, "ffn1_bwd_in",
                                          carried=_rs_chips_phase(sb_f1))

    big_sum = {}
    for tag, names, sb_, lb_, pcs in (("ffn2", ("ffn2_w_gate", "ffn2_w_up", "ffn2_w_down"), sb_f2, lb_f2, pcs_f2),
                                      ("mix", ("w_in", "w_out"), sb_mx, lb_mx, pcs_mx),
                                      ("ffn1", ("ffn1_w_gate", "ffn1_w_up", "ffn1_w_down"), sb_f1, lb_f1, pcs_f1)):
        gsum = _final_grad(sb_, lb_, chip, "rs_final_sum_" + tag)
        for name, (off, rows) in zip(names, pcs):
            big_sum[name] = gsum[off:off + rows]

    small_parts = [d_g1, d_gm, d_g3, d_gf, small[R_CW:R_CW + 4], small[R_CB], d_wa, small[R_BA], d_wi, small[R_BI],
                   small[R_LAM], small[R_SW:R_SW + 3], small[R_GLO], small[R_GSO]]
    sslab, counts = _pack_rows(small_parts, LANES)
    RS = sslab.shape[0]
    (sg,) = _allgather(sslab, [(0, RS)], "allgather_small_grads")
    ssum = _add_slabs([sg[j * RS:(j + 1) * RS] for j in range(N_DEV)], F32, "small_grads_sum")
    full_shapes = [(1, D), (1, D), (1, D), (D,), (1, 4, C), (1, C), (1, H, hd, hd), (1, H, hd), (1, H, hd, hd), (1, H, hd),
                   (1, C), (1, 3, C), (1, C), (1, C)]
    small_full = dict(zip(SMALL_NAMES, _unpack_rows(ssum, counts, full_shapes)))

    grads = {}
    for name, transposed in BIG:
        gblk = big_sum[name]
        grads[name] = (gblk.T if transposed else gblk)[None]
    for name in SMALL_NAMES:
        gfull = small_full[name]
        if name in ("lru_conv_w", "sc_conv_w"):
            gfull = lax.dynamic_slice_in_dim(gfull, dev * CL, CL, axis=2)
        grads[name] = gfull

    delta, new_m, new_v = {}, {}, {}
    for name, _ in BIG:
        shp = w[name].shape
        d_, m_, v_ = _adamw(w[name][0], grads[name][0], m[name][0], v[name][0], "adamw_" + name)
        delta[name], new_m[name], new_v[name] = d_.reshape(shp), m_.reshape(shp), v_.reshape(shp)
    packs = [_pack_rows([t[n_] for n_ in SMALL_NAMES], LANES) for t in (w, grads, m, v)]
    sd, sm, sv = _adamw(packs[0][0], packs[1][0], packs[2][0], packs[3][0], "adamw_small")
    shapes = [w[n_].shape for n_ in SMALL_NAMES]
    for tgt_dict, slab_ in ((delta, sd), (new_m, sm), (new_v, sv)):
        for n_, val in zip(SMALL_NAMES, _unpack_rows(slab_, packs[0][1], shapes)):
            tgt_dict[n_] = val

    loss = lax.psum(loss_blk[0, 0], ("x", "y", "c"))
    return (loss, dx0[None], *[grads[n_] for n_ in WEIGHT_NAMES], *[delta[n_] for n_ in WEIGHT_NAMES],
            *[new_m[n_] for n_ in WEIGHT_NAMES], *[new_v[n_] for n_ in WEIGHT_NAMES])
```

```python
import functools
import math

import jax
import jax.numpy as jnp
from jax import lax
from jax.experimental import pallas as pl
from jax.experimental.pallas import tpu as pltpu

F32 = jnp.float32
BF16 = jnp.bfloat16
SDS = jax.ShapeDtypeStruct
MESH = pl.DeviceIdType.MESH

NORM_EPS = 1e-6
LRU_C = 8.0
N_DEV = 8
N_CHIP = 4
ADAM_LR, ADAM_B1, ADAM_B2, ADAM_EPS, ADAM_WD, ADAM_STEP = 0.001, 0.9, 0.999, 1e-08, 0.01, 10

NN = (((1,), (0,)), ((), ()))
NT = (((1,), (1,)), ((), ()))
TN = (((0,), (0,)), ((), ()))

SUBLANES = 8
BF16_ROWS = 16
LANES = 128
MIB = 1 << 20


def _dot(a, b, dims):
    return lax.dot_general(a, b, dims, preferred_element_type=F32)


def _blk(n, pref, align):
    if n <= pref:
        return n
    b = (pref // align) * align
    while b >= align:
        if n % b == 0:
            return b
        b -= align
    raise ValueError(f"no block of {n} aligned to {align} under {pref}")


def _cp(sem, vmem_mib):
    return pltpu.CompilerParams(dimension_semantics=sem, vmem_limit_bytes=vmem_mib * MIB)


HBM_SPEC = pl.BlockSpec(memory_space=pltpu.HBM)


class _Carried:
    def __init__(self, inputs, out_shapes, aliases, sem_shapes, build):
        self.inputs, self.out_shapes, self.aliases = list(inputs), list(out_shapes), dict(aliases)
        self.sem_shapes, self.build = list(sem_shapes), build


def _call(body, *, name, grid, in_specs, out_specs, out_shape, scratch_shapes, compiler_params, args, carried=None):
    if carried is None:
        return pl.pallas_call(body, name=name, grid=grid, in_specs=in_specs, out_specs=out_specs, out_shape=out_shape,
                              scratch_shapes=scratch_shapes, compiler_params=compiler_params)(*args)
    n_in, n_out, n_sc = len(in_specs), len(out_shape), len(scratch_shapes)
    c_in, c_out = len(carried.inputs), len(carried.out_shapes)

    def hosted(*refs):
        ins, refs = refs[:n_in], refs[n_in:]
        c_ins, refs = refs[:c_in], refs[c_in:]
        outs, refs = refs[:n_out], refs[n_out:]
        c_outs, refs = refs[:c_out], refs[c_out:]
        scratch, c_sems = refs[:n_sc], refs[n_sc:]
        first = functools.reduce(jnp.logical_and, [pl.program_id(a) == 0 for a in range(len(grid))])
        last = functools.reduce(jnp.logical_and, [pl.program_id(a) == g - 1 for a, g in enumerate(grid)])

        @pl.when(first)
        def _():
            for start in carried.build(c_ins, c_outs, c_sems, True):
                start()

        body(*ins, *outs, *scratch)

        @pl.when(last)
        def _():
            for wait in carried.build(c_ins, c_outs, c_sems, False):
                wait()

    out = pl.pallas_call(
        hosted, name=name, grid=grid, in_specs=list(in_specs) + [HBM_SPEC] * c_in,
        out_specs=list(out_specs) + [HBM_SPEC] * c_out, out_shape=list(out_shape) + carried.out_shapes,
        scratch_shapes=list(scratch_shapes) + carried.sem_shapes,
        input_output_aliases={n_in + a: n_out + b for a, b in carried.aliases.items()},
        compiler_params=compiler_params)(*args, *carried.inputs)
    return out[:n_out], out[n_out:]


ROW_CHUNK = 128


def _chunk_rows(c):
    return pl.ds(pl.multiple_of(c * ROW_CHUNK, ROW_CHUNK), ROW_CHUNK)


def _rstd(xv):
    return lax.rsqrt(jnp.mean(xv * xv, axis=-1, keepdims=True) + NORM_EPS)


def _rms_bwd(xv, g, dn):
    r = _rstd(xv)
    xr = xv * r
    gd = g * dn
    dx = r * (gd - xr * jnp.mean(gd * xr, axis=-1, keepdims=True))
    return dx, jnp.sum(dn * xr, axis=0, keepdims=True)


def _log1p(e):
    u = 1.0 + e
    return jnp.where(u == 1.0, e, jnp.log(u) * (e / (u - 1.0)))


def _expm1(v):
    u = jnp.exp(v)
    return jnp.where(u == 1.0, v, jnp.where(u == 0.0, -1.0, (u - 1.0) * (v / jnp.log(u))))


def _gelu_parts(g):
    k0 = math.sqrt(2.0 / math.pi)
    g2 = g * g
    t = jnp.tanh(k0 * (g + 0.044715 * g * g2))
    gel = 0.5 * g * (1.0 + t)
    gelp = 0.5 * (1.0 + t) + 0.5 * g * (1.0 - t * t) * (k0 * (1.0 + 3.0 * 0.044715 * g2))
    return gel, gelp


def _norm_proj(x, gain, w_list, out_dtypes, swiglu, name, carried=None):
    T, D = x.shape
    N = w_list[0].shape[0]
    nw = len(w_list)
    bm = _blk(T, 1024, BF16_ROWS)
    bn = _blk(N, 512, LANES)

    def body(*refs):
        x_ref, g_ref = refs[:2]
        w_refs = refs[2:2 + nw]
        n_ref = refs[2 + nw]
        o_refs = refs[3 + nw:3 + 2 * nw]
        act_ref = refs[3 + 2 * nw] if swiglu else None
        n_sc = refs[-1]

        @pl.when(pl.program_id(1) == 0)
        def _():
            def chunk(c, _):
                r = _chunk_rows(c)
                xv = x_ref[r, :]
                nb = (xv * _rstd(xv) * g_ref[...]).astype(BF16)
                n_sc[r, :] = nb
                n_ref[r, :] = nb
                return 0
            lax.fori_loop(0, bm // ROW_CHUNK, chunk, 0)

        n = n_sc[...]
        outs = [_dot(n, w_ref[...], NT) for w_ref in w_refs]
        for o_ref, o in zip(o_refs, outs):
            o_ref[...] = o.astype(o_ref.dtype)
        if swiglu:
            hg, hu = outs
            act_ref[...] = (hg * jax.nn.sigmoid(hg) * hu).astype(BF16)

    row = pl.BlockSpec((bm, D), lambda i, j: (i, 0))
    tile = pl.BlockSpec((bm, bn), lambda i, j: (i, j))
    n_extra = 1 if swiglu else 0
    return _call(
        body, name=name, grid=(T // bm, N // bn),
        in_specs=[row, pl.BlockSpec((1, D), lambda i, j: (0, 0))] + [pl.BlockSpec((bn, D), lambda i, j: (j, 0))] * nw,
        out_specs=[row] + [tile] * (nw + n_extra),
        out_shape=[SDS((T, D), BF16)] + [SDS((T, N), dt) for dt in out_dtypes] + [SDS((T, N), BF16)] * n_extra,
        scratch_shapes=[pltpu.VMEM((bm, D), BF16)],
        compiler_params=_cp(("arbitrary", "arbitrary"), 52),
        args=(x, gain, *w_list), carried=carried)


def _mm_res(a, b, x, scale, name, carried=None):
    T, K = a.shape
    D = b.shape[1]
    bm = _blk(T, 1024, BF16_ROWS)
    bk = _blk(K, 1408, LANES)
    nk = K // bk

    def body(a_ref, b_ref, x_ref, o_ref):
        k = pl.program_id(1)

        @pl.when(k == 0)
        def _():
            o_ref[...] = jnp.zeros_like(o_ref)

        o_ref[...] += _dot(a_ref[...], b_ref[...], NN)

        @pl.when(k == nk - 1)
        def _():
            def chunk(c, _):
                r = _chunk_rows(c)
                o_ref[r, :] = x_ref[r, :] + scale * o_ref[r, :]
                return 0
            lax.fori_loop(0, bm // ROW_CHUNK, chunk, 0)

    row = pl.BlockSpec((bm, D), lambda i, k: (i, 0))
    out = _call(
        body, name=name, grid=(T // bm, nk),
        in_specs=[pl.BlockSpec((bm, bk), lambda i, k: (i, k)), pl.BlockSpec((bk, D), lambda i, k: (k, 0)), row],
        out_specs=[row], out_shape=[SDS((T, D), F32)], scratch_shapes=[],
        compiler_params=_cp(("arbitrary", "arbitrary"), 56),
        args=(a, b, x), carried=carried)
    return out[0] if carried is None else (out[0][0], out[1])


def _mm_nt(a, b, name):
    T, K = a.shape
    N = b.shape[0]
    bm = _blk(T, 1024, BF16_ROWS)
    bn = _blk(N, 512, LANES)

    def body(a_ref, b_ref, o_ref):
        o_ref[...] = _dot(a_ref[...], b_ref[...], NT)

    return pl.pallas_call(
        body, name=name, grid=(T // bm, N // bn),
        in_specs=[pl.BlockSpec((bm, K), lambda i, j: (i, 0)), pl.BlockSpec((bn, K), lambda i, j: (j, 0))],
        out_specs=pl.BlockSpec((bm, bn), lambda i, j: (i, j)), out_shape=SDS((T, N), F32),
        compiler_params=_cp(("arbitrary", "arbitrary"), 40),
    )(a, b)


def _ffn_bwd_act(dfb, wd, hg, hu, name):
    T, D = dfb.shape
    F = wd.shape[0]
    bm = _blk(T, 1024, BF16_ROWS)
    bn = _blk(F, 512, LANES)

    def body(df_ref, wd_ref, hg_ref, hu_ref, dhg_ref, dhu_ref):
        dact = _dot(df_ref[...], wd_ref[...], NT)
        hgv = hg_ref[...].astype(F32)
        huv = hu_ref[...].astype(F32)
        s = jax.nn.sigmoid(hgv)
        dhu_ref[...] = (dact * (hgv * s)).astype(BF16)
        dhg_ref[...] = (dact * huv * (s * (1.0 + hgv * (1.0 - s)))).astype(BF16)

    tile = pl.BlockSpec((bm, bn), lambda i, j: (i, j))
    return pl.pallas_call(
        body, name=name, grid=(T // bm, F // bn),
        in_specs=[pl.BlockSpec((bm, D), lambda i, j: (i, 0)), pl.BlockSpec((bn, D), lambda i, j: (j, 0)), tile, tile],
        out_specs=[tile, tile], out_shape=[SDS((T, F), BF16)] * 2,
        compiler_params=_cp(("arbitrary", "arbitrary"), 40),
    )(dfb, wd, hg, hu)


def _dw_tn(a, b, bm_pref, name):
    T, M = a.shape
    N = b.shape[1]
    bm = _blk(M, bm_pref, LANES)
    tk = _blk(T, 1024, BF16_ROWS)
    nk = T // tk

    def body(a_ref, b_ref, o_ref, acc):
        k = pl.program_id(1)

        @pl.when(k == 0)
        def _():
            acc[...] = jnp.zeros_like(acc)

        acc[...] += _dot(a_ref[...], b_ref[...], TN)

        @pl.when(k == nk - 1)
        def _():
            o_ref[...] = acc[...].astype(BF16)

    return pl.pallas_call(
        body, name=name, grid=(M // bm, nk),
        in_specs=[pl.BlockSpec((tk, bm), lambda i, k: (k, i)), pl.BlockSpec((tk, N), lambda i, k: (k, 0))],
        out_specs=pl.BlockSpec((bm, N), lambda i, k: (i, 0)), out_shape=SDS((M, N), BF16),
        scratch_shapes=[pltpu.VMEM((bm, N), F32)],
        compiler_params=_cp(("arbitrary", "arbitrary"), 48),
    )(a, b)


def _mm_rmsbwd(pairs, x, gain, dx_in, bscale, name, carried=None):
    T, D = x.shape
    K = pairs[0][0].shape[1]
    npair = len(pairs)
    bm = _blk(T, 512, BF16_ROWS)
    bk = _blk(K, 1408, LANES)
    nk = K // bk

    def body(*refs):
        ab = refs[:2 * npair]
        x_ref, g_ref, dxin_ref, dx_ref, dxb_ref, dg_ref, acc = refs[2 * npair:]
        i = pl.program_id(0)
        k = pl.program_id(1)

        @pl.when(k == 0)
        def _():
            acc[...] = jnp.zeros_like(acc)

        for q in range(npair):
            acc[...] += _dot(ab[2 * q][...], ab[2 * q + 1][...], NN)

        @pl.when(k == nk - 1)
        def _():
            @pl.when(i == 0)
            def _():
                dg_ref[...] = jnp.zeros_like(dg_ref)

            def chunk(c, _):
                r = _chunk_rows(c)
                dx, dg = _rms_bwd(x_ref[r, :], g_ref[...], acc[r, :])
                dxo = dxin_ref[r, :] + dx
                dx_ref[r, :] = dxo
                dxb_ref[r, :] = (bscale * dxo).astype(BF16)
                dg_ref[...] += dg
                return 0
            lax.fori_loop(0, bm // ROW_CHUNK, chunk, 0)

    row = pl.BlockSpec((bm, D), lambda i, k: (i, 0))
    row_once = pl.BlockSpec((bm, D), lambda i, k: (i, 0), pipeline_mode=pl.Buffered(1))
    vec = pl.BlockSpec((1, D), lambda i, k: (0, 0))
    in_specs = []
    args = []
    for a, b in pairs:
        in_specs += [pl.BlockSpec((bm, bk), lambda i, k: (i, k)), pl.BlockSpec((bk, D), lambda i, k: (k, 0))]
        args += [a, b]
    return _call(
        body, name=name, grid=(T // bm, nk),
        in_specs=in_specs + [row_once, vec, row_once], out_specs=[row, row, vec],
        out_shape=[SDS((T, D), F32), SDS((T, D), BF16), SDS((1, D), F32)],
        scratch_shapes=[pltpu.VMEM((bm, D), F32)],
        compiler_params=_cp(("arbitrary", "arbitrary"), 58),
        args=(*args, x, gain, dx_in), carried=carried)


def _loss_head(x3, gain, tgt, name):
    T, D = x3.shape
    bm = _blk(T, 256, BF16_ROWS)

    def body(x_ref, g_ref, t_ref, dx_ref, dxb_ref, dg_ref, loss_ref):
        i = pl.program_id(0)
        xv = x_ref[...]
        g = g_ref[...]
        out = xv * _rstd(xv) * g
        e = out - t_ref[...]
        part = 0.5 * jnp.sum(jnp.mean(e * e, axis=-1, keepdims=True), axis=0, keepdims=True)
        dx, dg = _rms_bwd(xv, g, e * (1.0 / D))
        dx_ref[...] = dx
        dxb_ref[...] = (0.5 * dx).astype(BF16)

        @pl.when(i == 0)
        def _():
            dg_ref[...] = dg
            loss_ref[...] = jnp.broadcast_to(part, loss_ref.shape)

        @pl.when(i > 0)
        def _():
            dg_ref[...] += dg
            loss_ref[...] += jnp.broadcast_to(part, loss_ref.shape)

    row = pl.BlockSpec((bm, D), lambda i: (i, 0))
    vec = pl.BlockSpec((1, D), lambda i: (0, 0))
    return pl.pallas_call(
        body, name=name, grid=(T // bm,),
        in_specs=[row, vec, row], out_specs=[row, row, vec, pl.BlockSpec((SUBLANES, LANES), lambda i: (0, 0))],
        out_shape=[SDS((T, D), F32), SDS((T, D), BF16), SDS((1, D), F32), SDS((SUBLANES, LANES), F32)],
        compiler_params=_cp(("arbitrary",), 40),
    )(x3, gain, tgt)


R_CW, R_CB, R_BA, R_BI, R_LAM, R_SW, R_GLO, R_GSO, SMALL_ROWS = 0, 4, 5, 6, 7, 8, 11, 12, 16


def _rows(g):
    return pl.ds(pl.multiple_of(g * SUBLANES, SUBLANES), SUBLANES)


def _shift_back(prev, cur, d):
    row = lax.broadcasted_iota(jnp.int32, cur.shape, 0)
    return pltpu.roll(jnp.where(row >= SUBLANES - d, prev, cur), d, 0)


def _shift_fwd(cur, nxt, d):
    row = lax.broadcasted_iota(jnp.int32, cur.shape, 0)
    return pltpu.roll(jnp.where(row < d, nxt, cur), SUBLANES - d, 0)


def _causal_conv(ext, g, taps_ref, ntap):
    prev = ext[_rows(g), :]
    cur = ext[_rows(g + 1), :]
    out = _shift_back(prev, cur, ntap - 1) * taps_ref[0:1, :]
    for k in range(1, ntap - 1):
        out = out + _shift_back(prev, cur, ntap - 1 - k) * taps_ref[k:k + 1, :]
    return out + cur * taps_ref[ntap - 1:ntap, :]


def _scan8(A, U, reverse):
    row = lax.broadcasted_iota(jnp.int32, A.shape, 0)
    for s in (1, 2, 4):
        if reverse:
            A_sh = pltpu.roll(A, SUBLANES - s, 0)
            U_sh = pltpu.roll(U, SUBLANES - s, 0)
            m = row < SUBLANES - s
        else:
            A_sh = pltpu.roll(A, s, 0)
            U_sh = pltpu.roll(U, s, 0)
            m = row >= s
        U = jnp.where(m, A * U_sh + U, U)
        A = jnp.where(m, A * A_sh, A)
    return A, U


def _gate_pre(xc_s, w_ref, out_s, H, hd):
    for h in range(H):
        cs = slice(h * hd, (h + 1) * hd)
        out_s[:, cs] = _dot(xc_s[:, cs].astype(BF16), w_ref[h], NN)


def _lru_coeffs(pa, pi, xc, ba, bi, sp):
    ra = jax.nn.sigmoid(pa + ba)
    ri = jax.nn.sigmoid(pi + bi)
    log_a = (-LRU_C * ra) * sp
    a = jnp.exp(log_a)
    mult = jnp.sqrt(-_expm1(2.0 * log_a))
    return ra, ri, a, mult


def _softplus_neg(lam):
    v = -lam
    return jnp.maximum(v, 0.0) + _log1p(jnp.exp(-jnp.abs(v)))


def _mix_fwd(z, cw, cb, wa, ba, wi, bi, lam, sw, glo, gso, name):
    T = z.shape[0]
    C = z.shape[1] // 5
    H = wa.shape[0]
    hd = C // H
    tb = _blk(T, 256, BF16_ROWS)
    ng = tb // SUBLANES
    HDR = SUBLANES

    def body(z_ref, cw_ref, cb_ref, wa_ref, ba_ref, wi_ref, bi_ref, lam_ref, sw_ref, glo_ref, gso_ref,
             y_ref, h_ref, xext, pext, xc_s, pa_s, pi_s, y_s, hcar):
        @pl.when(pl.program_id(0) == 0)
        def _():
            xext[0:HDR, :] = jnp.zeros((HDR, C), F32)
            pext[0:HDR, :] = jnp.zeros((HDR, C), F32)
            hcar[...] = jnp.zeros_like(hcar)

        def fill(g, _):
            r = _rows(g)
            re = _rows(g + 1)
            xext[re, :] = z_ref[r, 0:C]
            pext[re, :] = z_ref[r, 3 * C:4 * C] * z_ref[r, 4 * C:5 * C]
            return 0
        lax.fori_loop(0, ng, fill, 0)

        def conv(g, _):
            xc_s[_rows(g), :] = _causal_conv(xext, g, cw_ref, 4) + cb_ref[...]
            return 0
        lax.fori_loop(0, ng, conv, 0)

        _gate_pre(xc_s, wa_ref, pa_s, H, hd)
        _gate_pre(xc_s, wi_ref, pi_s, H, hd)
        sp = _softplus_neg(lam_ref[...])

        def group(g, hprev):
            r = _rows(g)
            xc = xc_s[r, :]
            _, ri, a, mult = _lru_coeffs(pa_s[r, :], pi_s[r, :], xc, ba_ref[...], bi_ref[...], sp)
            A, U = _scan8(a, mult * (ri * xc), reverse=False)
            hh = A * hprev + U
            h_ref[r, :] = hh
            gel, _ = _gelu_parts(z_ref[r, C:2 * C])
            y_lru = hh * gel
            y_s[r, 0:C] = y_lru * _rstd(y_lru) * glo_ref[...]
            y_sc = z_ref[r, 2 * C:3 * C] * _causal_conv(pext, g, sw_ref, 3)
            y_s[r, C:2 * C] = y_sc * _rstd(y_sc) * gso_ref[...]
            return jnp.broadcast_to(hh[SUBLANES - 1:SUBLANES, :], hh.shape)
        hcar[...] = lax.fori_loop(0, ng, group, hcar[...])

        xext[0:HDR, :] = xext[tb:tb + HDR, :]
        pext[0:HDR, :] = pext[tb:tb + HDR, :]

        def cast(g, _):
            r = pl.ds(pl.multiple_of(g * BF16_ROWS, BF16_ROWS), BF16_ROWS)
            y_ref[r, :] = y_s[r, :].astype(BF16)
            return 0
        lax.fori_loop(0, tb // BF16_ROWS, cast, 0)

    full = lambda shape: pl.BlockSpec(shape, lambda i: (0,) * len(shape))
    blk = lambda w: pl.BlockSpec((tb, w), lambda i: (i, 0))
    ext = pltpu.VMEM((tb + HDR, C), F32)
    tile = pltpu.VMEM((tb, C), F32)
    return pl.pallas_call(
        body, name=name, grid=(T // tb,),
        in_specs=[blk(5 * C), full((4, C)), full((1, C)), full((H, hd, hd)), full((1, C)), full((H, hd, hd)),
                  full((1, C)), full((1, C)), full((3, C)), full((1, C)), full((1, C))],
        out_specs=[blk(2 * C), blk(C)],
        out_shape=[SDS((T, 2 * C), BF16), SDS((T, C), F32)],
        scratch_shapes=[ext, ext, tile, tile, tile, pltpu.VMEM((tb, 2 * C), F32), pltpu.VMEM((SUBLANES, C), F32)],
        compiler_params=_cp(("arbitrary",), 40),
    )(z, cw, cb, wa, ba, wi, bi, lam, sw, glo, gso)


def _mix_bwd(z, h, dy, cw, cb, wa, ba, wi, bi, lam, sw, glo, gso, name):
    T = z.shape[0]
    C = z.shape[1] // 5
    H = wa.shape[0]
    hd = C // H
    tb = _blk(T, 256, BF16_ROWS)
    nb = T // tb
    ng = tb // SUBLANES
    HDR = SUBLANES
    N_ACC = 13

    def body(z_ref, zp_ref, h_ref, hp_ref, dy_ref, cw_ref, cb_ref, wa_ref, ba_ref, wi_ref, bi_ref, lam_ref,
             sw_ref, glo_ref, gso_ref, dz_ref, small_ref, dwa_ref, dwi_ref,
             xext, pext, hext, dqext, dxcext, bext, xc_s, pa_s, pi_s, a_s, m_s, ri_s, dh_s, dpa_s, dpi_s,
             dz_s, acc_s, bcar):
        i = pl.program_id(0)
        first_rows = i == nb - 1

        @pl.when(i == 0)
        def _():
            dqext[tb:tb + HDR, :] = jnp.zeros((HDR, C), F32)
            dxcext[tb:tb + HDR, :] = jnp.zeros((HDR, C), F32)
            bcar[...] = jnp.zeros_like(bcar)
            acc_s[...] = jnp.zeros_like(acc_s)
            dwa_ref[...] = jnp.zeros_like(dwa_ref)
            dwi_ref[...] = jnp.zeros_like(dwi_ref)

        zero = jnp.zeros((HDR, C), F32)
        xext[0:HDR, :] = jnp.where(first_rows, zero, zp_ref[:, 0:C])
        pext[0:HDR, :] = jnp.where(first_rows, zero, zp_ref[:, 3 * C:4 * C] * zp_ref[:, 4 * C:5 * C])
        hext[0:HDR, :] = jnp.where(first_rows, zero, hp_ref[...])

        def fill(g, _):
            r = _rows(g)
            re = _rows(g + 1)
            xext[re, :] = z_ref[r, 0:C]
            pext[re, :] = z_ref[r, 3 * C:4 * C] * z_ref[r, 4 * C:5 * C]
            hext[re, :] = h_ref[r, :]
            return 0
        lax.fori_loop(0, ng, fill, 0)

        def conv(g, _):
            xc_s[_rows(g), :] = _causal_conv(xext, g, cw_ref, 4) + cb_ref[...]
            return 0
        lax.fori_loop(0, ng, conv, 0)

        _gate_pre(xc_s, wa_ref, pa_s, H, hd)
        _gate_pre(xc_s, wi_ref, pi_s, H, hd)
        sp = _softplus_neg(lam_ref[...])
        dsp_dlam = -jax.nn.sigmoid(-lam_ref[...])

        def add_acc(k, v):
            acc_s[k] += v

        def p1(g, _):
            r = _rows(g)
            xc = xc_s[r, :]
            _, ri, a, mult = _lru_coeffs(pa_s[r, :], pi_s[r, :], xc, ba_ref[...], bi_ref[...], sp)
            a_s[r, :] = a
            m_s[r, :] = mult
            ri_s[r, :] = ri
            hh = h_ref[r, :]
            gel, gelp = _gelu_parts(z_ref[r, C:2 * C])
            y_lru = hh * gel
            dnl = dy_ref[r, 0:C]
            ylr = y_lru * _rstd(y_lru)
            gd = glo_ref[...] * dnl
            dy_lru = _rstd(y_lru) * (gd - ylr * jnp.mean(gd * ylr, axis=-1, keepdims=True))
            add_acc(R_GLO, dnl * ylr)
            dz_s[r, C:2 * C] = dy_lru * hh * gelp
            dh = dy_lru * gel
            dh_s[r, :] = dh

            q = _causal_conv(pext, g, sw_ref, 3)
            scb = z_ref[r, 2 * C:3 * C]
            y_sc = scb * q
            dns = dy_ref[r, C:2 * C]
            ysr = y_sc * _rstd(y_sc)
            gs = gso_ref[...] * dns
            dy_sc = _rstd(y_sc) * (gs - ysr * jnp.mean(gs * ysr, axis=-1, keepdims=True))
            add_acc(R_GSO, dns * ysr)
            dz_s[r, 2 * C:3 * C] = dy_sc * q
            dqext[r, :] = dy_sc * scb
            return 0
        lax.fori_loop(0, ng, p1, 0)

        bext[tb:tb + HDR, :] = bcar[...]

        def p2(j, carry):
            g = ng - 1 - j
            r = _rows(g)
            a = a_s[r, :]
            A, U = _scan8(a, a * dh_s[r, :], reverse=True)
            bb = A * carry + U
            bext[r, :] = bb
            return jnp.broadcast_to(bb[0:1, :], bb.shape)
        bcar[...] = lax.fori_loop(0, ng, p2, bcar[...])

        def p3(g, _):
            r = _rows(g)
            rn = _rows(g + 1)
            G = dh_s[r, :] + _shift_fwd(bext[r, :], bext[rn, :], 1)
            hm1 = _shift_back(hext[r, :], hext[rn, :], 1)
            a = a_s[r, :]
            mult = m_s[r, :]
            ri = ri_s[r, :]
            xc = xc_s[r, :]
            ra = jax.nn.sigmoid(pa_s[r, :] + ba_ref[...])
            dxcext[r, :] = G * mult * ri
            dri = G * mult * xc
            dmult = G * ri * xc
            dlog_a = (G * hm1) * a - dmult * (a * a) / mult
            add_acc(R_LAM, dlog_a * (-LRU_C * ra) * dsp_dlam)
            dpa = dlog_a * (-LRU_C * sp) * ra * (1.0 - ra)
            dpi = dri * ri * (1.0 - ri)
            add_acc(R_BA, dpa)
            add_acc(R_BI, dpi)
            dpa_s[r, :] = dpa
            dpi_s[r, :] = dpi
            return 0
        lax.fori_loop(0, ng, p3, 0)

        for hh_ in range(H):
            cs = slice(hh_ * hd, (hh_ + 1) * hd)
            dpa_b = dpa_s[:, cs].astype(BF16)
            dpi_b = dpi_s[:, cs].astype(BF16)
            xc_b = xc_s[:, cs].astype(BF16)
            dxcext[0:tb, cs] += _dot(dpa_b, wa_ref[hh_], NT) + _dot(dpi_b, wi_ref[hh_], NT)
            dwa_ref[hh_] += _dot(xc_b, dpa_b, TN)
            dwi_ref[hh_] += _dot(xc_b, dpi_b, TN)

        def p4(g, _):
            r = _rows(g)
            rn = _rows(g + 1)
            dxc = dxcext[r, :]
            dxc_n = dxcext[rn, :]
            x_p = xext[r, :]
            x_c = xext[rn, :]
            add_acc(R_CB, dxc)
            dlx = dxc * cw_ref[3:4, :]
            add_acc(R_CW + 3, dxc * x_c)
            for d in range(1, 4):
                dlx = dlx + _shift_fwd(dxc, dxc_n, d) * cw_ref[3 - d:4 - d, :]
                add_acc(R_CW + 3 - d, dxc * _shift_back(x_p, x_c, d))
            dz_s[r, 0:C] = dlx

            dq = dqext[r, :]
            dq_n = dqext[rn, :]
            p_p = pext[r, :]
            p_c = pext[rn, :]
            dp = dq * sw_ref[2:3, :]
            add_acc(R_SW + 2, dq * p_c)
            for d in range(1, 3):
                dp = dp + _shift_fwd(dq, dq_n, d) * sw_ref[2 - d:3 - d, :]
                add_acc(R_SW + 2 - d, dq * _shift_back(p_p, p_c, d))
            dz_s[r, 3 * C:4 * C] = dp * z_ref[r, 4 * C:5 * C]
            dz_s[r, 4 * C:5 * C] = dp * z_ref[r, 3 * C:4 * C]
            return 0
        lax.fori_loop(0, ng, p4, 0)

        dqext[tb:tb + HDR, :] = dqext[0:HDR, :]
        dxcext[tb:tb + HDR, :] = dxcext[0:HDR, :]

        def cast(g, _):
            r = pl.ds(pl.multiple_of(g * BF16_ROWS, BF16_ROWS), BF16_ROWS)
            dz_ref[r, :] = dz_s[r, :].astype(BF16)
            return 0
        lax.fori_loop(0, tb // BF16_ROWS, cast, 0)

        @pl.when(i == nb - 1)
        def _():
            small_ref[...] = jnp.zeros_like(small_ref)
            for k in range(N_ACC):
                small_ref[k:k + 1, :] = jnp.sum(acc_s[k], axis=0, keepdims=True)

    tpg = tb // SUBLANES
    full = lambda shape: pl.BlockSpec(shape, lambda i: (0,) * len(shape))
    blk = lambda w: pl.BlockSpec((tb, w), lambda i: (nb - 1 - i, 0))
    prev = lambda w: pl.BlockSpec((SUBLANES, w), lambda i: (jnp.maximum((nb - 1 - i) * tpg - 1, 0), 0))
    ext = pltpu.VMEM((tb + HDR, C), F32)
    tile = pltpu.VMEM((tb, C), F32)
    return pl.pallas_call(
        body, name=name, grid=(nb,),
        in_specs=[blk(5 * C), prev(5 * C), blk(C), prev(C), blk(2 * C), full((4, C)), full((1, C)), full((H, hd, hd)),
                  full((1, C)), full((H, hd, hd)), full((1, C)), full((1, C)), full((3, C)), full((1, C)), full((1, C))],
        out_specs=[blk(5 * C), full((SMALL_ROWS, C)), full((H, hd, hd)), full((H, hd, hd))],
        out_shape=[SDS((T, 5 * C), BF16), SDS((SMALL_ROWS, C), F32), SDS((H, hd, hd), F32), SDS((H, hd, hd), F32)],
        scratch_shapes=[ext] * 6 + [tile] * 9 + [pltpu.VMEM((tb, 5 * C), F32), pltpu.VMEM((N_ACC, SUBLANES, C), F32),
                                                pltpu.VMEM((SUBLANES, C), F32)],
        compiler_params=_cp(("arbitrary",), 56),
    )(z, z, h, h, dy, cw, cb, wa, ba, wi, bi, lam, sw, glo, gso)


def _add_slabs(terms, out_dtype, name):
    R, Ccols = terms[0].shape
    br = _blk(R, 512, BF16_ROWS)
    n = len(terms)

    def body(*refs):
        s = refs[0][...].astype(F32)
        for t_ref in refs[1:n]:
            s = s + t_ref[...].astype(F32)
        refs[n][...] = s.astype(out_dtype)

    spec = pl.BlockSpec((br, Ccols), lambda i: (i, 0))
    return pl.pallas_call(
        body, name=name, grid=(R // br,), in_specs=[spec] * n, out_specs=spec, out_shape=SDS((R, Ccols), out_dtype),
        compiler_params=_cp(("arbitrary",), 40),
    )(*terms)


def _final_grad(sb, lb, chip, name):
    _, R, Ccols = sb.shape
    br = _blk(R, 512, BF16_ROWS)

    def body(chip_ref, sb_ref, l0, l1, l2, o_ref):
        s = sb_ref[0].astype(F32)
        for t_ref in (l0, l1, l2):
            s = s + t_ref[0].astype(F32)
        o_ref[...] = s

    lspec = lambda k: pl.BlockSpec((1, br, Ccols), lambda i, c: (k, i, 0))
    return pl.pallas_call(
        body, name=name,
        grid_spec=pltpu.PrefetchScalarGridSpec(
            num_scalar_prefetch=1, grid=(R // br,),
            in_specs=[pl.BlockSpec((1, br, Ccols), lambda i, c: (c[0], i, 0)), lspec(0), lspec(1), lspec(2)],
            out_specs=pl.BlockSpec((br, Ccols), lambda i, c: (i, 0))),
        out_shape=SDS((R, Ccols), F32),
        compiler_params=_cp(("arbitrary",), 40),
    )(chip, sb, lb, lb, lb)


def _adamw(w, g, m, v, name):
    R, Ccols = w.shape
    br = _blk(R, 256, SUBLANES)
    c1 = 1.0 - ADAM_B1 ** ADAM_STEP
    c2 = 1.0 - ADAM_B2 ** ADAM_STEP

    def body(w_ref, g_ref, m_ref, v_ref, d_ref, nm_ref, nv_ref):
        gv = g_ref[...]
        nm = ADAM_B1 * m_ref[...] + (1.0 - ADAM_B1) * gv
        nv = ADAM_B2 * v_ref[...] + (1.0 - ADAM_B2) * (gv * gv)
        nm_ref[...] = nm
        nv_ref[...] = nv
        d_ref[...] = -ADAM_LR * ((nm / c1) / (jnp.sqrt(nv / c2) + ADAM_EPS) + ADAM_WD * w_ref[...])

    spec = pl.BlockSpec((br, Ccols), lambda i: (i, 0))
    return pl.pallas_call(
        body, name=name, grid=(R // br,), in_specs=[spec] * 4, out_specs=[spec] * 3,
        out_shape=[SDS((R, Ccols), F32)] * 3, compiler_params=_cp(("arbitrary",), 40),
    )(w, g, m, v)


def _place():
    return lax.axis_index("x"), lax.axis_index("y"), lax.axis_index("c")


def _dev_rows(ref, dev, rows):
    return ref.at[pl.ds((4 * dev[0] + 2 * dev[1] + dev[2]) * rows, rows), :]


def _remote(src, dst, send_sem, recv_sem, to):
    return pltpu.make_async_remote_copy(src_ref=src, dst_ref=dst, send_sem=send_sem, recv_sem=recv_sem,
                                        device_id=to, device_id_type=MESH)


def _ag_direct_phase(slab, pieces):
    W = slab.shape[1]
    n = len(pieces)

    def build(ins, outs, sems, starting):
        (slab_ref,) = ins
        send_sems, recv_sems, local_sems = sems
        x, y, c = _place()
        me = (x, y, c)
        peers = [(x, y, 1 - c), (1 - x, y, c), (x, 1 - y, c), (1 - x, 1 - y, c)]
        todo = []
        for p, (off, rows) in enumerate(pieces):
            src = slab_ref.at[pl.ds(off, rows), :]
            mine = pltpu.make_async_copy(src, _dev_rows(outs[p], me, rows), local_sems.at[p])
            todo.append(mine.start if starting else mine.wait)
            for k, peer in enumerate(peers):
                snd = _remote(src, _dev_rows(outs[p], me, rows), send_sems.at[k, p], recv_sems.at[k, p], peer)
                if starting:
                    todo.append(snd.start)
                else:
                    theirs = _dev_rows(outs[p], peer, rows)
                    rcv = _remote(theirs, theirs, send_sems.at[k, p], recv_sems.at[k, p], me)
                    todo += [rcv.wait_recv, snd.wait_send]
        return todo

    dma = pltpu.SemaphoreType.DMA
    return _Carried([slab], [SDS((N_DEV * rows, W), slab.dtype) for _, rows in pieces], {},
                    [dma((4, n)), dma((4, n)), dma((n,))], build)


def _ag_forward_phase(gathered, pieces):
    n = len(pieces)

    def build(ins, outs, sems, starting):
        send_sems, recv_sems = sems
        x, y, c = _place()
        me, sibling = (x, y, c), (x, y, 1 - c)
        chips = [(1 - x, y), (x, 1 - y), (1 - x, 1 - y)]
        todo = []
        for p, (_, rows) in enumerate(pieces):
            for j, chip in enumerate(chips):
                snd = _remote(_dev_rows(ins[p], (*chip, c), rows), _dev_rows(outs[p], (*chip, c), rows),
                              send_sems.at[j, p], recv_sems.at[j, p], sibling)
                if starting:
                    todo.append(snd.start)
                else:
                    theirs = _dev_rows(outs[p], (*chip, 1 - c), rows)
                    rcv = _remote(theirs, theirs, send_sems.at[j, p], recv_sems.at[j, p], me)
                    todo += [rcv.wait_recv, snd.wait_send]
        return todo

    dma = pltpu.SemaphoreType.DMA
    return _Carried(gathered, [SDS(g.shape, g.dtype) for g in gathered], {p: p for p in range(n)},
                    [dma((3, n)), dma((3, n))], build)


def _rs_chips_phase(sb):
    _, R, W = sb.shape

    def build(ins, outs, sems, starting):
        (sb_ref,), (land_ref,) = ins, outs
        send_sems, recv_sems = sems
        x, y, c = _place()
        chips = [(1 - x, y), (x, 1 - y), (1 - x, 1 - y)]
        cps = [_remote(sb_ref.at[2 * chip[0] + chip[1]], land_ref.at[j], send_sems.at[j], recv_sems.at[j], (*chip, c))
               for j, chip in enumerate(chips)]
        if starting:
            return [cp.start for cp in cps]
        return [cp.wait_recv for cp in cps] + [cp.wait_send for cp in cps]

    dma = pltpu.SemaphoreType.DMA
    return _Carried([sb], [SDS((3, R, W), sb.dtype)], {}, [dma((3,)), dma((3,))], build)


def _allgather(slab, pieces, name):
    R, W = slab.shape
    n = len(pieces)
    assert sum(rows for _, rows in pieces) == R

    def body(slab_ref, *refs):
        outs = refs[:n]
        send_sems, recv_sems, local_sems = refs[n:]
        x, y, c = _place()
        me, sibling = (x, y, c), (x, y, 1 - c)
        chips = [(1 - x, y), (x, 1 - y), (1 - x, 1 - y)]

        def dst_rows(p, origin):
            rows = pieces[p][1]
            start = (4 * origin[0] + 2 * origin[1] + origin[2]) * rows
            return outs[p].at[pl.ds(start, rows), :]

        def copies(k, origin, to, from_slab):
            out = []
            for p, (off, rows) in enumerate(pieces):
                dst = dst_rows(p, origin)
                src = slab_ref.at[pl.ds(off, rows), :] if from_slab else dst
                out.append(pltpu.make_async_remote_copy(
                    src_ref=src, dst_ref=dst, send_sem=send_sems.at[k, p], recv_sem=recv_sems.at[k, p],
                    device_id=to, device_id_type=MESH))
            return out

        mine = [pltpu.make_async_copy(slab_ref.at[pl.ds(off, rows), :], dst_rows(p, me), local_sems.at[p])
                for p, (off, rows) in enumerate(pieces)]
        for cp in mine:
            cp.start()
        first = copies(0, me, sibling, True)
        for j, chip in enumerate(chips):
            first += copies(1 + j, me, (*chip, c), True)
        for cp in first:
            cp.start()
        passed = []
        for j, chip in enumerate(chips):
            for cp in copies(1 + j, (*chip, c), me, False):
                cp.wait_recv()
            fwd = copies(4 + j, (*chip, c), sibling, False)
            for cp in fwd:
                cp.start()
            passed += fwd
        for cp in copies(0, sibling, me, False):
            cp.wait_recv()
        for j, chip in enumerate(chips):
            for cp in copies(4 + j, (*chip, 1 - c), me, False):
                cp.wait_recv()
        for cp in first + passed:
            cp.wait_send()
        for cp in mine:
            cp.wait()

    return pl.pallas_call(
        body, name=name,
        in_specs=[HBM_SPEC], out_specs=[HBM_SPEC] * n,
        out_shape=[SDS((N_DEV * rows, W), slab.dtype) for _, rows in pieces],
        scratch_shapes=[pltpu.SemaphoreType.DMA((7, n)), pltpu.SemaphoreType.DMA((7, n)), pltpu.SemaphoreType.DMA((n,))],
    )(slab)


def _rs_sibling(grads, pieces, name):
    W = grads[0].shape[1]
    R = sum(rows for _, rows in pieces)
    n = len(pieces)
    dt = grads[0].dtype
    max_rows = max(rows for _, rows in pieces)
    steps = [(q, p) for q in range(N_CHIP) for p in range(n)]
    ns = len(steps)
    ADD_ROWS = 64
    assert all(rows % ADD_ROWS == 0 for _, rows in pieces)

    def body(*refs):
        g_refs = refs[:n]
        sb_ref, mine_buf, send_buf, land_buf, out_buf, in_sems, out_sems, send_sems, recv_sems, credit = refs[n:]
        x, y, c = _place()
        sibling = (x, y, 1 - c)

        def loads(s):
            q, p = steps[s]
            rows = pieces[p][1]
            slot = s % 2
            mine = g_refs[p].at[pl.ds((2 * q + c) * rows, rows), :]
            theirs = g_refs[p].at[pl.ds((2 * q + 1 - c) * rows, rows), :]
            return (pltpu.make_async_copy(mine, mine_buf.at[slot, pl.ds(0, rows), :], in_sems.at[slot, 0]),
                    pltpu.make_async_copy(theirs, send_buf.at[slot, pl.ds(0, rows), :], in_sems.at[slot, 1]))

        def send(s):
            rows = pieces[steps[s][1]][1]
            slot = s % 2
            return pltpu.make_async_remote_copy(
                src_ref=send_buf.at[slot, pl.ds(0, rows), :], dst_ref=land_buf.at[slot, pl.ds(0, rows), :],
                send_sem=send_sems.at[slot], recv_sem=recv_sems.at[slot], device_id=sibling, device_id_type=MESH)

        def store(s):
            q, p = steps[s]
            off, rows = pieces[p]
            slot = s % 2
            return pltpu.make_async_copy(out_buf.at[slot, pl.ds(0, rows), :], sb_ref.at[q, pl.ds(off, rows), :],
                                         out_sems.at[slot])

        for cp in loads(0):
            cp.start()
        for s in range(ns):
            slot = s % 2
            rows = pieces[steps[s][1]][1]
            if s + 1 < ns:
                if s >= 1:
                    send(s - 1).wait_send()
                for cp in loads(s + 1):
                    cp.start()
            for cp in loads(s):
                cp.wait()
            if s >= 2:
                pl.semaphore_wait(credit.at[slot], 1)
            send(s).start()
            send(s).wait_recv()
            if s >= 2:
                store(s - 2).wait()

            def add(k, _, slot=slot):
                r = pl.ds(pl.multiple_of(k * ADD_ROWS, ADD_ROWS), ADD_ROWS)
                out_buf[slot, r, :] = (mine_buf[slot, r, :].astype(F32) + land_buf[slot, r, :].astype(F32)).astype(dt)
                return 0
            lax.fori_loop(0, rows // ADD_ROWS, add, 0)
            if s + 2 < ns:
                pl.semaphore_signal(credit.at[slot], inc=1, device_id=sibling, device_id_type=MESH)
            store(s).start()
        for s in range(max(ns - 2, 0), ns):
            send(s).wait_send()
            store(s).wait()

    buf = pltpu.VMEM((2, max_rows, W), dt)
    return pl.pallas_call(
        body, name=name,
        in_specs=[HBM_SPEC] * n, out_specs=HBM_SPEC,
        out_shape=SDS((N_CHIP, R, W), dt),
        scratch_shapes=[buf, buf, buf, buf, pltpu.SemaphoreType.DMA((2, 2)), pltpu.SemaphoreType.DMA((2,)),
                        pltpu.SemaphoreType.DMA((2,)), pltpu.SemaphoreType.DMA((2,)), pltpu.SemaphoreType.REGULAR((2,))],
        compiler_params=pltpu.CompilerParams(vmem_limit_bytes=40 * MIB),
    )(*grads)


SMALL_NAMES = ("ffn1_norm", "mix_norm", "ffn2_norm", "final_norm", "lru_conv_w", "lru_conv_b", "lru_w_a", "lru_b_a",
               "lru_w_i", "lru_b_i", "lru_lambda", "sc_conv_w", "lru_out_norm", "sc_out_norm")
WEIGHT_NAMES = ("ffn1_norm", "ffn1_w_gate", "ffn1_w_up", "ffn1_w_down", "mix_norm", "w_in", "lru_conv_w", "lru_conv_b",
                "lru_w_a", "lru_b_a", "lru_w_i", "lru_b_i", "lru_lambda", "sc_conv_w", "lru_out_norm", "sc_out_norm",
                "w_out", "ffn2_norm", "ffn2_w_gate", "ffn2_w_up", "ffn2_w_down", "final_norm")
BIG = (("ffn1_w_gate", True), ("ffn1_w_up", True), ("ffn1_w_down", False), ("ffn2_w_gate", True), ("ffn2_w_up", True),
       ("ffn2_w_down", False), ("w_in", True), ("w_out", False))


SLAB_ROW_ALIGN = 256


def _pack_rows(parts, width):
    rows, counts = [], []
    for p in parts:
        flat = p.reshape(-1)
        nr = -(-flat.shape[0] // width)
        nr = -(-nr // SUBLANES) * SUBLANES
        rows.append(jnp.pad(flat, (0, nr * width - flat.shape[0])).reshape(nr, width))
        counts.append(nr)
    total = sum(counts)
    pad = -(-total // SLAB_ROW_ALIGN) * SLAB_ROW_ALIGN - total
    if pad:
        rows.append(jnp.zeros((pad, width), rows[0].dtype))
    return jnp.concatenate(rows, axis=0), counts


def _stack_rows(blocks):
    pieces, off = [], 0
    for b in blocks:
        pieces.append((off, b.shape[0]))
        off += b.shape[0]
    return jnp.concatenate(blocks, axis=0), pieces


def _unpack_rows(slab, counts, shapes):
    out, r = [], 0
    for nr, shape in zip(counts, shapes):
        size = math.prod(shape)
        out.append(slab[r:r + nr].reshape(-1)[:size].reshape(shape))
        r += nr
    return out


def kernel(x, ffn1_norm, ffn1_w_gate, ffn1_w_up, ffn1_w_down, mix_norm, w_in, lru_conv_w, lru_conv_b, lru_w_a, lru_b_a, lru_w_i, lru_b_i, lru_lambda, sc_conv_w, lru_out_norm, sc_out_norm, w_out, ffn2_norm, ffn2_w_gate, ffn2_w_up, ffn2_w_down, final_norm, loss_target, m_ffn1_norm, m_ffn1_w_gate, m_ffn1_w_up, m_ffn1_w_down, m_mix_norm, m_w_in, m_lru_conv_w, m_lru_conv_b, m_lru_w_a, m_lru_b_a, m_lru_w_i, m_lru_b_i, m_lru_lambda, m_sc_conv_w, m_lru_out_norm, m_sc_out_norm, m_w_out, m_ffn2_norm, m_ffn2_w_gate, m_ffn2_w_up, m_ffn2_w_down, m_final_norm, v_ffn1_norm, v_ffn1_w_gate, v_ffn1_w_up, v_ffn1_w_down, v_mix_norm, v_w_in, v_lru_conv_w, v_lru_conv_b, v_lru_w_a, v_lru_b_a, v_lru_w_i, v_lru_b_i, v_lru_lambda, v_sc_conv_w, v_lru_out_norm, v_sc_out_norm, v_w_out, v_ffn2_norm, v_ffn2_w_gate, v_ffn2_w_up, v_ffn2_w_down, v_final_norm):
    a = dict(locals())
    w = {n: a[n] for n in WEIGHT_NAMES}
    m = {n: a["m_" + n] for n in WEIGHT_NAMES}
    v = {n: a["v_" + n] for n in WEIGHT_NAMES}
    ax, ay, ac = _place()
    dev = 4 * ax + 2 * ay + ac
    chip = (2 * ax + ay).astype(jnp.int32).reshape(1)

    x0 = x[0]
    tgt = loss_target[0]
    T, D = x0.shape
    C = D // 2
    H, hd = lru_w_a.shape[1], lru_w_a.shape[2]
    CL = lru_conv_w.shape[2]

    shards = []
    for name, transposed in BIG:
        s = w[name][0]
        shards.append((s.T if transposed else s).astype(BF16))
    taps = jnp.concatenate([lru_conv_w[0], sc_conv_w[0], jnp.zeros((1, CL), F32)], axis=0)
    taps_row = lax.bitcast_convert_type(taps, BF16).reshape(1, -1)
    taps_blk = jnp.pad(taps_row, ((0, BF16_ROWS - 1), (0, D - taps_row.shape[1])))
    slab_a, pieces_a = _stack_rows(shards[:3])
    slab_b, pieces_b = _stack_rows(shards[3:] + [taps_blk])
    wg1, wu1, wd1 = _allgather(slab_a, pieces_a, "allgather_ffn1")
    rest_direct = _ag_direct_phase(slab_b, pieces_b)

    g1, gm, g3 = ffn1_norm, mix_norm, ffn2_norm
    (n1, hg1, hu1, act1), rest = _norm_proj(x0, g1, [wg1, wu1], [BF16, BF16], True, "ffn1_up", carried=rest_direct)
    x1, rest = _mm_res(act1, wd1, x0, 0.5, "ffn1_down", carried=_ag_forward_phase(rest, pieces_b))
    wg2, wu2, wd2, win, wout, taps_all = rest
    taps_all = taps_all.reshape(N_DEV, BF16_ROWS, D)[:, 0, :2 * SUBLANES * CL].reshape(N_DEV, SUBLANES, CL, 2)
    taps_all = lax.bitcast_convert_type(taps_all, F32)
    taps_all = taps_all.transpose(1, 0, 2).reshape(SUBLANES, N_DEV * CL)
    cw, sw = taps_all[0:4], taps_all[4:7]

    gf = final_norm.reshape(1, D)
    cb = lru_conv_b
    wa, wi = lru_w_a[0].astype(BF16), lru_w_i[0].astype(BF16)
    ba, bi = lru_b_a.reshape(1, C), lru_b_i.reshape(1, C)
    lam, glo, gso = lru_lambda, lru_out_norm, sc_out_norm

    n2, z = _norm_proj(x1, gm, [win], [F32], False, "in_proj")
    y, h = _mix_fwd(z, cw, cb, wa, ba, wi, bi, lam, sw, glo, gso, "mix_fwd")
    x2 = _mm_res(y, wout, x1, 1.0, "out_proj")
    n3, hg2, hu2, act2 = _norm_proj(x2, g3, [wg2, wu2], [BF16, BF16], True, "ffn2_up")
    x3 = _mm_res(act2, wd2, x2, 0.5, "ffn2_down")
    dx3, df2, d_gf, loss_blk = _loss_head(x3, gf, tgt, "loss_head")

    F = wd1.shape[0]
    bm_f = F // 4 if (F // 4) % LANES == 0 else 512

    def reduce_group(gs, tag):
        pcs, off = [], 0
        for g_ in gs:
            pcs.append((off, g_.shape[0] // N_DEV))
            off += g_.shape[0] // N_DEV
        sb_ = _rs_sibling(gs, pcs, "rs_sibling_add_" + tag)
        return sb_, pcs

    dhg2, dhu2 = _ffn_bwd_act(df2, wd2, hg2, hu2, "ffn2_bwd_act")
    d_wd2 = _dw_tn(act2, df2, bm_f, "ffn2_dw_down")
    d_wg2 = _dw_tn(dhg2, n3, bm_f, "ffn2_dw_gate")
    d_wu2 = _dw_tn(dhu2, n3, bm_f, "ffn2_dw_up")
    sb_f2, pcs_f2 = reduce_group([d_wg2, d_wu2, d_wd2], "ffn2")
    (dx2, dx2b, d_g3), (lb_f2,) = _mm_rmsbwd([(dhg2, wg2), (dhu2, wu2)], x2, g3, dx3, 1.0, "ffn2_bwd_in",
                                             carried=_rs_chips_phase(sb_f2))
    dy = _mm_nt(dx2b, wout, "out_proj_bwd")
    d_wout = _dw_tn(y, dx2b, 1024, "out_proj_dw")
    dz, small, d_wa, d_wi = _mix_bwd(z, h, dy, cw, cb, wa, ba, wi, bi, lam, sw, glo, gso, "mix_bwd")
    d_win = _dw_tn(dz, n2, 1280, "in_proj_dw")
    sb_mx, pcs_mx = reduce_group([d_win, d_wout], "mix")
    (dx1, df1, d_gm), (lb_mx,) = _mm_rmsbwd([(dz, win)], x1, gm, dx2, 0.5, "in_proj_bwd",
                                            carried=_rs_chips_phase(sb_mx))
    dhg1, dhu1 = _ffn_bwd_act(df1, wd1, hg1, hu1, "ffn1_bwd_act")
    d_wd1 = _dw_tn(act1, df1, bm_f, "ffn1_dw_down")
    d_wg1 = _dw_tn(dhg1, n1, bm_f, "ffn1_dw_gate")
    d_wu1 = _dw_tn(dhu1, n1, bm_f, "ffn1_dw_up")
    sb_f1, pcs_f1 = reduce_group([d_wg1, d_wu1, d_wd1], "ffn1")
    (dx0, _, d_g1), (lb_f1,) = _mm_rmsbwd([(dhg1, wg1), (dhu1, wu1)], x0, g1, dx1, 1.0, "ffn1_bwd_in",
                                          carried=_rs_chips_phase(sb_f1))

    big_sum = {}
    for tag, names, sb_, lb_, pcs in (("ffn2", ("ffn2_w_gate", "ffn2_w_up", "ffn2_w_down"), sb_f2, lb_f2, pcs_f2),
                                      ("mix", ("w_in", "w_out"), sb_mx, lb_mx, pcs_mx),
                                      ("ffn1", ("ffn1_w_gate", "ffn1_w_up", "ffn1_w_down"), sb_f1, lb_f1, pcs_f1)):
        gsum = _final_grad(sb_, lb_, chip, "rs_final_sum_" + tag)
        for name, (off, rows) in zip(names, pcs):
            big_sum[name] = gsum[off:off + rows]

    small_parts = [d_g1, d_gm, d_g3, d_gf, small[R_CW:R_CW + 4], small[R_CB], d_wa, small[R_BA], d_wi, small[R_BI],
                   small[R_LAM], small[R_SW:R_SW + 3], small[R_GLO], small[R_GSO]]
    sslab, counts = _pack_rows(small_parts, LANES)
    RS = sslab.shape[0]
    (sg,) = _allgather(sslab, [(0, RS)], "allgather_small_grads")
    ssum = _add_slabs([sg[j * RS:(j + 1) * RS] for j in range(N_DEV)], F32, "small_grads_sum")
    full_shapes = [(1, D), (1, D), (1, D), (D,), (1, 4, C), (1, C), (1, H, hd, hd), (1, H, hd), (1, H, hd, hd), (1, H, hd),
                   (1, C), (1, 3, C), (1, C), (1, C)]
    small_full = dict(zip(SMALL_NAMES, _unpack_rows(ssum, counts, full_shapes)))

    grads = {}
    for name, transposed in BIG:
        gblk = big_sum[name]
        grads[name] = (gblk.T if transposed else gblk)[None]
    for name in SMALL_NAMES:
        gfull = small_full[name]
        if name in ("lru_conv_w", "sc_conv_w"):
            gfull = lax.dynamic_slice_in_dim(gfull, dev * CL, CL, axis=2)
        grads[name] = gfull

    delta, new_m, new_v = {}, {}, {}
    for name, _ in BIG:
        shp = w[name].shape
        d_, m_, v_ = _adamw(w[name][0], grads[name][0], m[name][0], v[name][0], "adamw_" + name)
        delta[name], new_m[name], new_v[name] = d_.reshape(shp), m_.reshape(shp), v_.reshape(shp)
    packs = [_pack_rows([t[n_] for n_ in SMALL_NAMES], LANES) for t in (w, grads, m, v)]
    sd, sm, sv = _adamw(packs[0][0], packs[1][0], packs[2][0], packs[3][0], "adamw_small")
    shapes = [w[n_].shape for n_ in SMALL_NAMES]
    for tgt_dict, slab_ in ((delta, sd), (new_m, sm), (new_v, sv)):
        for n_, val in zip(SMALL_NAMES, _unpack_rows(slab_, packs[0][1], shapes)):
            tgt_dict[n_] = val

    loss = lax.psum(loss_blk[0, 0], ("x", "y", "c"))
    return (loss, dx0[None], *[grads[n_] for n_ in WEIGHT_NAMES], *[delta[n_] for n_ in WEIGHT_NAMES],
            *[new_m[n_] for n_ in WEIGHT_NAMES], *[new_v[n_] for n_ in WEIGHT_NAMES])
```

```python
import functools
import math

import jax
import jax.numpy as jnp
from jax import lax
from jax.experimental import pallas as pl
from jax.experimental.pallas import tpu as pltpu

F32 = jnp.float32
BF16 = jnp.bfloat16
SDS = jax.ShapeDtypeStruct
MESH = pl.DeviceIdType.MESH

NORM_EPS = 1e-6
LRU_C = 8.0
N_DEV = 8
N_CHIP = 4
ADAM_LR, ADAM_B1, ADAM_B2, ADAM_EPS, ADAM_WD, ADAM_STEP = 0.001, 0.9, 0.999, 1e-08, 0.01, 10

NN = (((1,), (0,)), ((), ()))
NT = (((1,), (1,)), ((), ()))
TN = (((0,), (0,)), ((), ()))

SUBLANES = 8
BF16_ROWS = 16
LANES = 128
MIB = 1 << 20


def _dot(a, b, dims):
    return lax.dot_general(a, b, dims, preferred_element_type=F32)


def _blk(n, pref, align):
    if n <= pref:
        return n
    b = (pref // align) * align
    while b >= align:
        if n % b == 0:
            return b
        b -= align
    raise ValueError(f"no block of {n} aligned to {align} under {pref}")


def _cp(sem, vmem_mib):
    return pltpu.CompilerParams(dimension_semantics=sem, vmem_limit_bytes=vmem_mib * MIB)


HBM_SPEC = pl.BlockSpec(memory_space=pltpu.HBM)


class _Carried:
    def __init__(self, inputs, out_shapes, aliases, sem_shapes, build):
        self.inputs, self.out_shapes, self.aliases = list(inputs), list(out_shapes), dict(aliases)
        self.sem_shapes, self.build = list(sem_shapes), build


def _call(body, *, name, grid, in_specs, out_specs, out_shape, scratch_shapes, compiler_params, args, carried=None):
    if carried is None:
        return pl.pallas_call(body, name=name, grid=grid, in_specs=in_specs, out_specs=out_specs, out_shape=out_shape,
                              scratch_shapes=scratch_shapes, compiler_params=compiler_params)(*args)
    n_in, n_out, n_sc = len(in_specs), len(out_shape), len(scratch_shapes)
    c_in, c_out = len(carried.inputs), len(carried.out_shapes)

    def hosted(*refs):
        ins, refs = refs[:n_in], refs[n_in:]
        c_ins, refs = refs[:c_in], refs[c_in:]
        outs, refs = refs[:n_out], refs[n_out:]
        c_outs, refs = refs[:c_out], refs[c_out:]
        scratch, c_sems = refs[:n_sc], refs[n_sc:]
        first = functools.reduce(jnp.logical_and, [pl.program_id(a) == 0 for a in range(len(grid))])
        last = functools.reduce(jnp.logical_and, [pl.program_id(a) == g - 1 for a, g in enumerate(grid)])

        @pl.when(first)
        def _():
            for start in carried.build(c_ins, c_outs, c_sems, True):
                start()

        body(*ins, *outs, *scratch)

        @pl.when(last)
        def _():
            for wait in carried.build(c_ins, c_outs, c_sems, False):
                wait()

    out = pl.pallas_call(
        hosted, name=name, grid=grid, in_specs=list(in_specs) + [HBM_SPEC] * c_in,
        out_specs=list(out_specs) + [HBM_SPEC] * c_out, out_shape=list(out_shape) + carried.out_shapes,
        scratch_shapes=list(scratch_shapes) + carried.sem_shapes,
        input_output_aliases={n_in + a: n_out + b for a, b in carried.aliases.items()},
        compiler_params=compiler_params)(*args, *carried.inputs)
    return out[:n_out], out[n_out:]


ROW_CHUNK = 128


def _chunk_rows(c):
    return pl.ds(pl.multiple_of(c * ROW_CHUNK, ROW_CHUNK), ROW_CHUNK)


def _rstd(xv):
    return lax.rsqrt(jnp.mean(xv * xv, axis=-1, keepdims=True) + NORM_EPS)


def _rms_bwd(xv, g, dn):
    r = _rstd(xv)
    xr = xv * r
    gd = g * dn
    dx = r * (gd - xr * jnp.mean(gd * xr, axis=-1, keepdims=True))
    return dx, jnp.sum(dn * xr, axis=0, keepdims=True)


def _log1p(e):
    u = 1.0 + e
    return jnp.where(u == 1.0, e, jnp.log(u) * (e / (u - 1.0)))


def _expm1(v):
    u = jnp.exp(v)
    return jnp.where(u == 1.0, v, jnp.where(u == 0.0, -1.0, (u - 1.0) * (v / jnp.log(u))))


def _gelu_parts(g):
    k0 = math.sqrt(2.0 / math.pi)
    g2 = g * g
    t = jnp.tanh(k0 * (g + 0.044715 * g * g2))
    gel = 0.5 * g * (1.0 + t)
    gelp = 0.5 * (1.0 + t) + 0.5 * g * (1.0 - t * t) * (k0 * (1.0 + 3.0 * 0.044715 * g2))
    return gel, gelp


def _norm_proj(x, gain, w_list, out_dtypes, swiglu, name, carried=None):
    T, D = x.shape
    N = w_list[0].shape[0]
    nw = len(w_list)
    bm = _blk(T, 1024, BF16_ROWS)
    bn = _blk(N, 512, LANES)

    def body(*refs):
        x_ref, g_ref = refs[:2]
        w_refs = refs[2:2 + nw]
        n_ref = refs[2 + nw]
        o_refs = refs[3 + nw:3 + 2 * nw]
        act_ref = refs[3 + 2 * nw] if swiglu else None
        n_sc = refs[-1]

        @pl.when(pl.program_id(1) == 0)
        def _():
            def chunk(c, _):
                r = _chunk_rows(c)
                xv = x_ref[r, :]
                nb = (xv * _rstd(xv) * g_ref[...]).astype(BF16)
                n_sc[r, :] = nb
                n_ref[r, :] = nb
                return 0
            lax.fori_loop(0, bm // ROW_CHUNK, chunk, 0)

        n = n_sc[...]
        outs = [_dot(n, w_ref[...], NT) for w_ref in w_refs]
        for o_ref, o in zip(o_refs, outs):
            o_ref[...] = o.astype(o_ref.dtype)
        if swiglu:
            hg, hu = outs
            act_ref[...] = (hg * jax.nn.sigmoid(hg) * hu).astype(BF16)

    row = pl.BlockSpec((bm, D), lambda i, j: (i, 0))
    tile = pl.BlockSpec((bm, bn), lambda i, j: (i, j))
    n_extra = 1 if swiglu else 0
    return _call(
        body, name=name, grid=(T // bm, N // bn),
        in_specs=[row, pl.BlockSpec((1, D), lambda i, j: (0, 0))] + [pl.BlockSpec((bn, D), lambda i, j: (j, 0))] * nw,
        out_specs=[row] + [tile] * (nw + n_extra),
        out_shape=[SDS((T, D), BF16)] + [SDS((T, N), dt) for dt in out_dtypes] + [SDS((T, N), BF16)] * n_extra,
        scratch_shapes=[pltpu.VMEM((bm, D), BF16)],
        compiler_params=_cp(("arbitrary", "arbitrary"), 52),
        args=(x, gain, *w_list), carried=carried)


def _mm_res(a, b, x, scale, name, carried=None):
    T, K = a.shape
    D = b.shape[1]
    bm = _blk(T, 1024, BF16_ROWS)
    bk = _blk(K, 1408, LANES)
    nk = K // bk

    def body(a_ref, b_ref, x_ref, o_ref):
        k = pl.program_id(1)

        @pl.when(k == 0)
        def _():
            o_ref[...] = jnp.zeros_like(o_ref)

        o_ref[...] += _dot(a_ref[...], b_ref[...], NN)

        @pl.when(k == nk - 1)
        def _():
            def chunk(c, _):
                r = _chunk_rows(c)
                o_ref[r, :] = x_ref[r, :] + scale * o_ref[r, :]
                return 0
            lax.fori_loop(0, bm // ROW_CHUNK, chunk, 0)

    row = pl.BlockSpec((bm, D), lambda i, k: (i, 0))
    out = _call(
        body, name=name, grid=(T // bm, nk),
        in_specs=[pl.BlockSpec((bm, bk), lambda i, k: (i, k)), pl.BlockSpec((bk, D), lambda i, k: (k, 0)), row],
        out_specs=[row], out_shape=[SDS((T, D), F32)], scratch_shapes=[],
        compiler_params=_cp(("arbitrary", "arbitrary"), 56),
        args=(a, b, x), carried=carried)
    return out[0] if carried is None else (out[0][0], out[1])


def _mm_nt(a, b, name):
    T, K = a.shape
    N = b.shape[0]
    bm = _blk(T, 1024, BF16_ROWS)
    bn = _blk(N, 512, LANES)

    def body(a_ref, b_ref, o_ref):
        o_ref[...] = _dot(a_ref[...], b_ref[...], NT)

    return pl.pallas_call(
        body, name=name, grid=(T // bm, N // bn),
        in_specs=[pl.BlockSpec((bm, K), lambda i, j: (i, 0)), pl.BlockSpec((bn, K), lambda i, j: (j, 0))],
        out_specs=pl.BlockSpec((bm, bn), lambda i, j: (i, j)), out_shape=SDS((T, N), F32),
        compiler_params=_cp(("arbitrary", "arbitrary"), 40),
    )(a, b)


def _ffn_bwd_act(dfb, wd, hg, hu, name):
    T, D = dfb.shape
    F = wd.shape[0]
    bm = _blk(T, 1024, BF16_ROWS)
    bn = _blk(F, 512, LANES)

    def body(df_ref, wd_ref, hg_ref, hu_ref, dhg_ref, dhu_ref):
        dact = _dot(df_ref[...], wd_ref[...], NT)
        hgv = hg_ref[...].astype(F32)
        huv = hu_ref[...].astype(F32)
        s = jax.nn.sigmoid(hgv)
        dhu_ref[...] = (dact * (hgv * s)).astype(BF16)
        dhg_ref[...] = (dact * huv * (s * (1.0 + hgv * (1.0 - s)))).astype(BF16)

    tile = pl.BlockSpec((bm, bn), lambda i, j: (i, j))
    return pl.pallas_call(
        body, name=name, grid=(T // bm, F // bn),
        in_specs=[pl.BlockSpec((bm, D), lambda i, j: (i, 0)), pl.BlockSpec((bn, D), lambda i, j: (j, 0)), tile, tile],
        out_specs=[tile, tile], out_shape=[SDS((T, F), BF16)] * 2,
        compiler_params=_cp(("arbitrary", "arbitrary"), 40),
    )(dfb, wd, hg, hu)


def _dw_tn(a, b, bm_pref, name):
    T, M = a.shape
    N = b.shape[1]
    bm = _blk(M, bm_pref, LANES)
    tk = _blk(T, 1024, BF16_ROWS)
    nk = T // tk

    def body(a_ref, b_ref, o_ref, acc):
        k = pl.program_id(1)

        @pl.when(k == 0)
        def _():
            acc[...] = jnp.zeros_like(acc)

        acc[...] += _dot(a_ref[...], b_ref[...], TN)

        @pl.when(k == nk - 1)
        def _():
            o_ref[...] = acc[...].astype(BF16)

    return pl.pallas_call(
        body, name=name, grid=(M // bm, nk),
        in_specs=[pl.BlockSpec((tk, bm), lambda i, k: (k, i)), pl.BlockSpec((tk, N), lambda i, k: (k, 0))],
        out_specs=pl.BlockSpec((bm, N), lambda i, k: (i, 0)), out_shape=SDS((M, N), BF16),
        scratch_shapes=[pltpu.VMEM((bm, N), F32)],
        compiler_params=_cp(("arbitrary", "arbitrary"), 48),
    )(a, b)


def _mm_rmsbwd(pairs, x, gain, dx_in, bscale, name, carried=None):
    T, D = x.shape
    K = pairs[0][0].shape[1]
    npair = len(pairs)
    bm = _blk(T, 512, BF16_ROWS)
    bk = _blk(K, 1408, LANES)
    nk = K // bk

    def body(*refs):
        ab = refs[:2 * npair]
        x_ref, g_ref, dxin_ref, dx_ref, dxb_ref, dg_ref, acc = refs[2 * npair:]
        i = pl.program_id(0)
        k = pl.program_id(1)

        @pl.when(k == 0)
        def _():
            acc[...] = jnp.zeros_like(acc)

        for q in range(npair):
            acc[...] += _dot(ab[2 * q][...], ab[2 * q + 1][...], NN)

        @pl.when(k == nk - 1)
        def _():
            @pl.when(i == 0)
            def _():
                dg_ref[...] = jnp.zeros_like(dg_ref)

            def chunk(c, _):
                r = _chunk_rows(c)
                dx, dg = _rms_bwd(x_ref[r, :], g_ref[...], acc[r, :])
                dxo = dxin_ref[r, :] + dx
                dx_ref[r, :] = dxo
                dxb_ref[r, :] = (bscale * dxo).astype(BF16)
                dg_ref[...] += dg
                return 0
            lax.fori_loop(0, bm // ROW_CHUNK, chunk, 0)

    row = pl.BlockSpec((bm, D), lambda i, k: (i, 0))
    row_once = pl.BlockSpec((bm, D), lambda i, k: (i, 0), pipeline_mode=pl.Buffered(1))
    vec = pl.BlockSpec((1, D), lambda i, k: (0, 0))
    in_specs = []
    args = []
    for a, b in pairs:
        in_specs += [pl.BlockSpec((bm, bk), lambda i, k: (i, k)), pl.BlockSpec((bk, D), lambda i, k: (k, 0))]
        args += [a, b]
    return _call(
        body, name=name, grid=(T // bm, nk),
        in_specs=in_specs + [row_once, vec, row_once], out_specs=[row, row, vec],
        out_shape=[SDS((T, D), F32), SDS((T, D), BF16), SDS((1, D), F32)],
        scratch_shapes=[pltpu.VMEM((bm, D), F32)],
        compiler_params=_cp(("arbitrary", "arbitrary"), 58),
        args=(*args, x, gain, dx_in), carried=carried)


def _loss_head(x3, gain, tgt, name):
    T, D = x3.shape
    bm = _blk(T, 256, BF16_ROWS)

    def body(x_ref, g_ref, t_ref, dx_ref, dxb_ref, dg_ref, loss_ref):
        i = pl.program_id(0)
        xv = x_ref[...]
        g = g_ref[...]
        out = xv * _rstd(xv) * g
        e = out - t_ref[...]
        part = 0.5 * jnp.sum(jnp.mean(e * e, axis=-1, keepdims=True), axis=0, keepdims=True)
        dx, dg = _rms_bwd(xv, g, e * (1.0 / D))
        dx_ref[...] = dx
        dxb_ref[...] = (0.5 * dx).astype(BF16)

        @pl.when(i == 0)
        def _():
            dg_ref[...] = dg
            loss_ref[...] = jnp.broadcast_to(part, loss_ref.shape)

        @pl.when(i > 0)
        def _():
            dg_ref[...] += dg
            loss_ref[...] += jnp.broadcast_to(part, loss_ref.shape)

    row = pl.BlockSpec((bm, D), lambda i: (i, 0))
    vec = pl.BlockSpec((1, D), lambda i: (0, 0))
    return pl.pallas_call(
        body, name=name, grid=(T // bm,),
        in_specs=[row, vec, row], out_specs=[row, row, vec, pl.BlockSpec((SUBLANES, LANES), lambda i: (0, 0))],
        out_shape=[SDS((T, D), F32), SDS((T, D), BF16), SDS((1, D), F32), SDS((SUBLANES, LANES), F32)],
        compiler_params=_cp(("arbitrary",), 40),
    )(x3, gain, tgt)


R_CW, R_CB, R_BA, R_BI, R_LAM, R_SW, R_GLO, R_GSO, SMALL_ROWS = 0, 4, 5, 6, 7, 8, 11, 12, 16


def _rows(g):
    return pl.ds(pl.multiple_of(g * SUBLANES, SUBLANES), SUBLANES)


def _shift_back(prev, cur, d):
    row = lax.broadcasted_iota(jnp.int32, cur.shape, 0)
    return pltpu.roll(jnp.where(row >= SUBLANES - d, prev, cur), d, 0)


def _shift_fwd(cur, nxt, d):
    row = lax.broadcasted_iota(jnp.int32, cur.shape, 0)
    return pltpu.roll(jnp.where(row < d, nxt, cur), SUBLANES - d, 0)


def _causal_conv(ext, g, taps_ref, ntap):
    prev = ext[_rows(g), :]
    cur = ext[_rows(g + 1), :]
    out = _shift_back(prev, cur, ntap - 1) * taps_ref[0:1, :]
    for k in range(1, ntap - 1):
        out = out + _shift_back(prev, cur, ntap - 1 - k) * taps_ref[k:k + 1, :]
    return out + cur * taps_ref[ntap - 1:ntap, :]


def _scan8(A, U, reverse):
    row = lax.broadcasted_iota(jnp.int32, A.shape, 0)
    for s in (1, 2, 4):
        if reverse:
            A_sh = pltpu.roll(A, SUBLANES - s, 0)
            U_sh = pltpu.roll(U, SUBLANES - s, 0)
            m = row < SUBLANES - s
        else:
            A_sh = pltpu.roll(A, s, 0)
            U_sh = pltpu.roll(U, s, 0)
            m = row >= s
        U = jnp.where(m, A * U_sh + U, U)
        A = jnp.where(m, A * A_sh, A)
    return A, U


def _gate_pre(xc_s, w_ref, out_s, H, hd):
    for h in range(H):
        cs = slice(h * hd, (h + 1) * hd)
        out_s[:, cs] = _dot(xc_s[:, cs].astype(BF16), w_ref[h], NN)


def _lru_coeffs(pa, pi, xc, ba, bi, sp):
    ra = jax.nn.sigmoid(pa + ba)
    ri = jax.nn.sigmoid(pi + bi)
    log_a = (-LRU_C * ra) * sp
    a = jnp.exp(log_a)
    mult = jnp.sqrt(-_expm1(2.0 * log_a))
    return ra, ri, a, mult


def _softplus_neg(lam):
    v = -lam
    return jnp.maximum(v, 0.0) + _log1p(jnp.exp(-jnp.abs(v)))


def _mix_fwd(z, cw, cb, wa, ba, wi, bi, lam, sw, glo, gso, name):
    T = z.shape[0]
    C = z.shape[1] // 5
    H = wa.shape[0]
    hd = C // H
    tb = _blk(T, 256, BF16_ROWS)
    ng = tb // SUBLANES
    HDR = SUBLANES

    def body(z_ref, cw_ref, cb_ref, wa_ref, ba_ref, wi_ref, bi_ref, lam_ref, sw_ref, glo_ref, gso_ref,
             y_ref, h_ref, xext, pext, xc_s, pa_s, pi_s, y_s, hcar):
        @pl.when(pl.program_id(0) == 0)
        def _():
            xext[0:HDR, :] = jnp.zeros((HDR, C), F32)
            pext[0:HDR, :] = jnp.zeros((HDR, C), F32)
            hcar[...] = jnp.zeros_like(hcar)

        def fill(g, _):
            r = _rows(g)
            re = _rows(g + 1)
            xext[re, :] = z_ref[r, 0:C]
            pext[re, :] = z_ref[r, 3 * C:4 * C] * z_ref[r, 4 * C:5 * C]
            return 0
        lax.fori_loop(0, ng, fill, 0)

        def conv(g, _):
            xc_s[_rows(g), :] = _causal_conv(xext, g, cw_ref, 4) + cb_ref[...]
            return 0
        lax.fori_loop(0, ng, conv, 0)

        _gate_pre(xc_s, wa_ref, pa_s, H, hd)
        _gate_pre(xc_s, wi_ref, pi_s, H, hd)
        sp = _softplus_neg(lam_ref[...])

        def group(g, hprev):
            r = _rows(g)
            xc = xc_s[r, :]
            _, ri, a, mult = _lru_coeffs(pa_s[r, :], pi_s[r, :], xc, ba_ref[...], bi_ref[...], sp)
            A, U = _scan8(a, mult * (ri * xc), reverse=False)
            hh = A * hprev + U
            h_ref[r, :] = hh
            gel, _ = _gelu_parts(z_ref[r, C:2 * C])
            y_lru = hh * gel
            y_s[r, 0:C] = y_lru * _rstd(y_lru) * glo_ref[...]
            y_sc = z_ref[r, 2 * C:3 * C] * _causal_conv(pext, g, sw_ref, 3)
            y_s[r, C:2 * C] = y_sc * _rstd(y_sc) * gso_ref[...]
            return jnp.broadcast_to(hh[SUBLANES - 1:SUBLANES, :], hh.shape)
        hcar[...] = lax.fori_loop(0, ng, group, hcar[...])

        xext[0:HDR, :] = xext[tb:tb + HDR, :]
        pext[0:HDR, :] = pext[tb:tb + HDR, :]

        def cast(g, _):
            r = pl.ds(pl.multiple_of(g * BF16_ROWS, BF16_ROWS), BF16_ROWS)
            y_ref[r, :] = y_s[r, :].astype(BF16)
            return 0
        lax.fori_loop(0, tb // BF16_ROWS, cast, 0)

    full = lambda shape: pl.BlockSpec(shape, lambda i: (0,) * len(shape))
    blk = lambda w: pl.BlockSpec((tb, w), lambda i: (i, 0))
    ext = pltpu.VMEM((tb + HDR, C), F32)
    tile = pltpu.VMEM((tb, C), F32)
    return pl.pallas_call(
        body, name=name, grid=(T // tb,),
        in_specs=[blk(5 * C), full((4, C)), full((1, C)), full((H, hd, hd)), full((1, C)), full((H, hd, hd)),
                  full((1, C)), full((1, C)), full((3, C)), full((1, C)), full((1, C))],
        out_specs=[blk(2 * C), blk(C)],
        out_shape=[SDS((T, 2 * C), BF16), SDS((T, C), F32)],
        scratch_shapes=[ext, ext, tile, tile, tile, pltpu.VMEM((tb, 2 * C), F32), pltpu.VMEM((SUBLANES, C), F32)],
        compiler_params=_cp(("arbitrary",), 40),
    )(z, cw, cb, wa, ba, wi, bi, lam, sw, glo, gso)


def _mix_bwd(z, h, dy, cw, cb, wa, ba, wi, bi, lam, sw, glo, gso, name):
    T = z.shape[0]
    C = z.shape[1] // 5
    H = wa.shape[0]
    hd = C // H
    tb = _blk(T, 256, BF16_ROWS)
    nb = T // tb
    ng = tb // SUBLANES
    HDR = SUBLANES
    N_ACC = 13

    def body(z_ref, zp_ref, h_ref, hp_ref, dy_ref, cw_ref, cb_ref, wa_ref, ba_ref, wi_ref, bi_ref, lam_ref,
             sw_ref, glo_ref, gso_ref, dz_ref, small_ref, dwa_ref, dwi_ref,
             xext, pext, hext, dqext, dxcext, bext, xc_s, pa_s, pi_s, a_s, m_s, ri_s, dh_s, dpa_s, dpi_s,
             dz_s, acc_s, bcar):
        i = pl.program_id(0)
        first_rows = i == nb - 1

        @pl.when(i == 0)
        def _():
            dqext[tb:tb + HDR, :] = jnp.zeros((HDR, C), F32)
            dxcext[tb:tb + HDR, :] = jnp.zeros((HDR, C), F32)
            bcar[...] = jnp.zeros_like(bcar)
            acc_s[...] = jnp.zeros_like(acc_s)
            dwa_ref[...] = jnp.zeros_like(dwa_ref)
            dwi_ref[...] = jnp.zeros_like(dwi_ref)

        zero = jnp.zeros((HDR, C), F32)
        xext[0:HDR, :] = jnp.where(first_rows, zero, zp_ref[:, 0:C])
        pext[0:HDR, :] = jnp.where(first_rows, zero, zp_ref[:, 3 * C:4 * C] * zp_ref[:, 4 * C:5 * C])
        hext[0:HDR, :] = jnp.where(first_rows, zero, hp_ref[...])

        def fill(g, _):
            r = _rows(g)
            re = _rows(g + 1)
            xext[re, :] = z_ref[r, 0:C]
            pext[re, :] = z_ref[r, 3 * C:4 * C] * z_ref[r, 4 * C:5 * C]
            hext[re, :] = h_ref[r, :]
            return 0
        lax.fori_loop(0, ng, fill, 0)

        def conv(g, _):
            xc_s[_rows(g), :] = _causal_conv(xext, g, cw_ref, 4) + cb_ref[...]
            return 0
        lax.fori_loop(0, ng, conv, 0)

        _gate_pre(xc_s, wa_ref, pa_s, H, hd)
        _gate_pre(xc_s, wi_ref, pi_s, H, hd)
        sp = _softplus_neg(lam_ref[...])
        dsp_dlam = -jax.nn.sigmoid(-lam_ref[...])

        def add_acc(k, v):
            acc_s[k] += v

        def p1(g, _):
            r = _rows(g)
            xc = xc_s[r, :]
            _, ri, a, mult = _lru_coeffs(pa_s[r, :], pi_s[r, :], xc, ba_ref[...], bi_ref[...], sp)
            a_s[r, :] = a
            m_s[r, :] = mult
            ri_s[r, :] = ri
            hh = h_ref[r, :]
            gel, gelp = _gelu_parts(z_ref[r, C:2 * C])
            y_lru = hh * gel
            dnl = dy_ref[r, 0:C]
            ylr = y_lru * _rstd(y_lru)
            gd = glo_ref[...] * dnl
            dy_lru = _rstd(y_lru) * (gd - ylr * jnp.mean(gd * ylr, axis=-1, keepdims=True))
            add_acc(R_GLO, dnl * ylr)
            dz_s[r, C:2 * C] = dy_lru * hh * gelp
            dh = dy_lru * gel
            dh_s[r, :] = dh

            q = _causal_conv(pext, g, sw_ref, 3)
            scb = z_ref[r, 2 * C:3 * C]
            y_sc = scb * q
            dns = dy_ref[r, C:2 * C]
            ysr = y_sc * _rstd(y_sc)
            gs = gso_ref[...] * dns
            dy_sc = _rstd(y_sc) * (gs - ysr * jnp.mean(gs * ysr, axis=-1, keepdims=True))
            add_acc(R_GSO, dns * ysr)
            dz_s[r, 2 * C:3 * C] = dy_sc * q
            dqext[r, :] = dy_sc * scb
            return 0
        lax.fori_loop(0, ng, p1, 0)

        bext[tb:tb + HDR, :] = bcar[...]

        def p2(j, carry):
            g = ng - 1 - j
            r = _rows(g)
            a = a_s[r, :]
            A, U = _scan8(a, a * dh_s[r, :], reverse=True)
            bb = A * carry + U
            bext[r, :] = bb
            return jnp.broadcast_to(bb[0:1, :], bb.shape)
        bcar[...] = lax.fori_loop(0, ng, p2, bcar[...])

        def p3(g, _):
            r = _rows(g)
            rn = _rows(g + 1)
            G = dh_s[r, :] + _shift_fwd(bext[r, :], bext[rn, :], 1)
            hm1 = _shift_back(hext[r, :], hext[rn, :], 1)
            a = a_s[r, :]
            mult = m_s[r, :]
            ri = ri_s[r, :]
            xc = xc_s[r, :]
            ra = jax.nn.sigmoid(pa_s[r, :] + ba_ref[...])
            dxcext[r, :] = G * mult * ri
            dri = G * mult * xc
            dmult = G * ri * xc
            dlog_a = (G * hm1) * a - dmult * (a * a) / mult
            add_acc(R_LAM, dlog_a * (-LRU_C * ra) * dsp_dlam)
            dpa = dlog_a * (-LRU_C * sp) * ra * (1.0 - ra)
            dpi = dri * ri * (1.0 - ri)
            add_acc(R_BA, dpa)
            add_acc(R_BI, dpi)
            dpa_s[r, :] = dpa
            dpi_s[r, :] = dpi
            return 0
        lax.fori_loop(0, ng, p3, 0)

        for hh_ in range(H):
            cs = slice(hh_ * hd, (hh_ + 1) * hd)
            dpa_b = dpa_s[:, cs].astype(BF16)
            dpi_b = dpi_s[:, cs].astype(BF16)
            xc_b = xc_s[:, cs].astype(BF16)
            dxcext[0:tb, cs] += _dot(dpa_b, wa_ref[hh_], NT) + _dot(dpi_b, wi_ref[hh_], NT)
            dwa_ref[hh_] += _dot(xc_b, dpa_b, TN)
            dwi_ref[hh_] += _dot(xc_b, dpi_b, TN)

        def p4(g, _):
            r = _rows(g)
            rn = _rows(g + 1)
            dxc = dxcext[r, :]
            dxc_n = dxcext[rn, :]
            x_p = xext[r, :]
            x_c = xext[rn, :]
            add_acc(R_CB, dxc)
            dlx = dxc * cw_ref[3:4, :]
            add_acc(R_CW + 3, dxc * x_c)
            for d in range(1, 4):
                dlx = dlx + _shift_fwd(dxc, dxc_n, d) * cw_ref[3 - d:4 - d, :]
                add_acc(R_CW + 3 - d, dxc * _shift_back(x_p, x_c, d))
            dz_s[r, 0:C] = dlx

            dq = dqext[r, :]
            dq_n = dqext[rn, :]
            p_p = pext[r, :]
            p_c = pext[rn, :]
            dp = dq * sw_ref[2:3, :]
            add_acc(R_SW + 2, dq * p_c)
            for d in range(1, 3):
                dp = dp + _shift_fwd(dq, dq_n, d) * sw_ref[2 - d:3 - d, :]
                add_acc(R_SW + 2 - d, dq * _shift_back(p_p, p_c, d))
            dz_s[r, 3 * C:4 * C] = dp * z_ref[r, 4 * C:5 * C]
            dz_s[r, 4 * C:5 * C] = dp * z_ref[r, 3 * C:4 * C]
            return 0
        lax.fori_loop(0, ng, p4, 0)

        dqext[tb:tb + HDR, :] = dqext[0:HDR, :]
        dxcext[tb:tb + HDR, :] = dxcext[0:HDR, :]

        def cast(g, _):
            r = pl.ds(pl.multiple_of(g * BF16_ROWS, BF16_ROWS), BF16_ROWS)
            dz_ref[r, :] = dz_s[r, :].astype(BF16)
            return 0
        lax.fori_loop(0, tb // BF16_ROWS, cast, 0)

        @pl.when(i == nb - 1)
        def _():
            small_ref[...] = jnp.zeros_like(small_ref)
            for k in range(N_ACC):
                small_ref[k:k + 1, :] = jnp.sum(acc_s[k], axis=0, keepdims=True)

    tpg = tb // SUBLANES
    full = lambda shape: pl.BlockSpec(shape, lambda i: (0,) * len(shape))
    blk = lambda w: pl.BlockSpec((tb, w), lambda i: (nb - 1 - i, 0))
    prev = lambda w: pl.BlockSpec((SUBLANES, w), lambda i: (jnp.maximum((nb - 1 - i) * tpg - 1, 0), 0))
    ext = pltpu.VMEM((tb + HDR, C), F32)
    tile = pltpu.VMEM((tb, C), F32)
    return pl.pallas_call(
        body, name=name, grid=(nb,),
        in_specs=[blk(5 * C), prev(5 * C), blk(C), prev(C), blk(2 * C), full((4, C)), full((1, C)), full((H, hd, hd)),
                  full((1, C)), full((H, hd, hd)), full((1, C)), full((1, C)), full((3, C)), full((1, C)), full((1, C))],
        out_specs=[blk(5 * C), full((SMALL_ROWS, C)), full((H, hd, hd)), full((H, hd, hd))],
        out_shape=[SDS((T, 5 * C), BF16), SDS((SMALL_ROWS, C), F32), SDS((H, hd, hd), F32), SDS((H, hd, hd), F32)],
        scratch_shapes=[ext] * 6 + [tile] * 9 + [pltpu.VMEM((tb, 5 * C), F32), pltpu.VMEM((N_ACC, SUBLANES, C), F32),
                                                pltpu.VMEM((SUBLANES, C), F32)],
        compiler_params=_cp(("arbitrary",), 56),
    )(z, z, h, h, dy, cw, cb, wa, ba, wi, bi, lam, sw, glo, gso)


def _add_slabs(terms, out_dtype, name):
    R, Ccols = terms[0].shape
    br = _blk(R, 512, BF16_ROWS)
    n = len(terms)

    def body(*refs):
        s = refs[0][...].astype(F32)
        for t_ref in refs[1:n]:
            s = s + t_ref[...].astype(F32)
        refs[n][...] = s.astype(out_dtype)

    spec = pl.BlockSpec((br, Ccols), lambda i: (i, 0))
    return pl.pallas_call(
        body, name=name, grid=(R // br,), in_specs=[spec] * n, out_specs=spec, out_shape=SDS((R, Ccols), out_dtype),
        compiler_params=_cp(("arbitrary",), 40),
    )(*terms)


def _final_grad(sb, lb, chip, name):
    _, R, Ccols = sb.shape
    br = _blk(R, 512, BF16_ROWS)

    def body(chip_ref, sb_ref, l0, l1, l2, o_ref):
        s = sb_ref[0].astype(F32)
        for t_ref in (l0, l1, l2):
            s = s + t_ref[0].astype(F32)
        o_ref[...] = s

    lspec = lambda k: pl.BlockSpec((1, br, Ccols), lambda i, c: (k, i, 0))
    return pl.pallas_call(
        body, name=name,
        grid_spec=pltpu.PrefetchScalarGridSpec(
            num_scalar_prefetch=1, grid=(R // br,),
            in_specs=[pl.BlockSpec((1, br, Ccols), lambda i, c: (c[0], i, 0)), lspec(0), lspec(1), lspec(2)],
            out_specs=pl.BlockSpec((br, Ccols), lambda i, c: (i, 0))),
        out_shape=SDS((R, Ccols), F32),
        compiler_params=_cp(("arbitrary",), 40),
    )(chip, sb, lb, lb, lb)


def _adamw(w, g, m, v, name):
    R, Ccols = w.shape
    br = _blk(R, 256, SUBLANES)
    c1 = 1.0 - ADAM_B1 ** ADAM_STEP
    c2 = 1.0 - ADAM_B2 ** ADAM_STEP

    def body(w_ref, g_ref, m_ref, v_ref, d_ref, nm_ref, nv_ref):
        gv = g_ref[...]
        nm = ADAM_B1 * m_ref[...] + (1.0 - ADAM_B1) * gv
        nv = ADAM_B2 * v_ref[...] + (1.0 - ADAM_B2) * (gv * gv)
        nm_ref[...] = nm
        nv_ref[...] = nv
        d_ref[...] = -ADAM_LR * ((nm / c1) / (jnp.sqrt(nv / c2) + ADAM_EPS) + ADAM_WD * w_ref[...])

    spec = pl.BlockSpec((br, Ccols), lambda i: (i, 0))
    return pl.pallas_call(
        body, name=name, grid=(R // br,), in_specs=[spec] * 4, out_specs=[spec] * 3,
        out_shape=[SDS((R, Ccols), F32)] * 3, compiler_params=_cp(("arbitrary",), 40),
    )(w, g, m, v)


def _place():
    return lax.axis_index("x"), lax.axis_index("y"), lax.axis_index("c")


def _dev_rows(ref, dev, rows):
    return ref.at[pl.ds((4 * dev[0] + 2 * dev[1] + dev[2]) * rows, rows), :]


def _remote(src, dst, send_sem, recv_sem, to):
    return pltpu.make_async_remote_copy(src_ref=src, dst_ref=dst, send_sem=send_sem, recv_sem=recv_sem,
                                        device_id=to, device_id_type=MESH)


SAME_CORE_AND_SIBLING = ((0, 0, 1), (1, 0, 0), (0, 1, 0), (1, 1, 0))
ALL_OTHERS = SAME_CORE_AND_SIBLING + ((1, 0, 1), (0, 1, 1), (1, 1, 1))


def _merge_phases(a, b):
    na_in, na_out, na_sem = len(a.inputs), len(a.out_shapes), len(a.sem_shapes)

    def build(ins, outs, sems, starting):
        return (a.build(ins[:na_in], outs[:na_out], sems[:na_sem], starting)
                + b.build(ins[na_in:], outs[na_out:], sems[na_sem:], starting))

    aliases = dict(a.aliases)
    aliases.update({na_in + i: na_out + o for i, o in b.aliases.items()})
    return _Carried(a.inputs + b.inputs, a.out_shapes + b.out_shapes, aliases, a.sem_shapes + b.sem_shapes, build)


def _ag_direct_phase(slab, pieces, flips):
    W = slab.shape[1]
    n = len(pieces)
    npeer = len(flips)

    def build(ins, outs, sems, starting):
        (slab_ref,) = ins
        send_sems, recv_sems, local_sems = sems
        x, y, c = _place()
        me = (x, y, c)
        peers = [tuple(1 - v if f else v for v, f in zip(me, flip)) for flip in flips]
        todo = []
        for p, (off, rows) in enumerate(pieces):
            src = slab_ref.at[pl.ds(off, rows), :]
            mine = pltpu.make_async_copy(src, _dev_rows(outs[p], me, rows), local_sems.at[p])
            todo.append(mine.start if starting else mine.wait)
            for k, peer in enumerate(peers):
                snd = _remote(src, _dev_rows(outs[p], me, rows), send_sems.at[k, p], recv_sems.at[k, p], peer)
                if starting:
                    todo.append(snd.start)
                else:
                    theirs = _dev_rows(outs[p], peer, rows)
                    rcv = _remote(theirs, theirs, send_sems.at[k, p], recv_sems.at[k, p], me)
                    todo += [rcv.wait_recv, snd.wait_send]
        return todo

    dma = pltpu.SemaphoreType.DMA
    return _Carried([slab], [SDS((N_DEV * rows, W), slab.dtype) for _, rows in pieces], {},
                    [dma((npeer, n)), dma((npeer, n)), dma((n,))], build)


def _ag_forward_phase(gathered, pieces):
    n = len(pieces)

    def build(ins, outs, sems, starting):
        send_sems, recv_sems = sems
        x, y, c = _place()
        me, sibling = (x, y, c), (x, y, 1 - c)
        chips = [(1 - x, y), (x, 1 - y), (1 - x, 1 - y)]
        todo = []
        for p, (_, rows) in enumerate(pieces):
            for j, chip in enumerate(chips):
                snd = _remote(_dev_rows(ins[p], (*chip, c), rows), _dev_rows(outs[p], (*chip, c), rows),
                              send_sems.at[j, p], recv_sems.at[j, p], sibling)
                if starting:
                    todo.append(snd.start)
                else:
                    theirs = _dev_rows(outs[p], (*chip, 1 - c), rows)
                    rcv = _remote(theirs, theirs, send_sems.at[j, p], recv_sems.at[j, p], me)
                    todo += [rcv.wait_recv, snd.wait_send]
        return todo

    dma = pltpu.SemaphoreType.DMA
    return _Carried(gathered, [SDS(g.shape, g.dtype) for g in gathered], {p: p for p in range(n)},
                    [dma((3, n)), dma((3, n))], build)


def _rs_chips_phase(sb):
    _, R, W = sb.shape

    def build(ins, outs, sems, starting):
        (sb_ref,), (land_ref,) = ins, outs
        send_sems, recv_sems = sems
        x, y, c = _place()
        chips = [(1 - x, y), (x, 1 - y), (1 - x, 1 - y)]
        cps = [_remote(sb_ref.at[2 * chip[0] + chip[1]], land_ref.at[j], send_sems.at[j], recv_sems.at[j], (*chip, c))
               for j, chip in enumerate(chips)]
        if starting:
            return [cp.start for cp in cps]
        return [cp.wait_recv for cp in cps] + [cp.wait_send for cp in cps]

    dma = pltpu.SemaphoreType.DMA
    return _Carried([sb], [SDS((3, R, W), sb.dtype)], {}, [dma((3,)), dma((3,))], build)


def _allgather(slab, pieces, name):
    R, W = slab.shape
    n = len(pieces)
    assert sum(rows for _, rows in pieces) == R

    def body(slab_ref, *refs):
        outs = refs[:n]
        send_sems, recv_sems, local_sems = refs[n:]
        x, y, c = _place()
        me, sibling = (x, y, c), (x, y, 1 - c)
        chips = [(1 - x, y), (x, 1 - y), (1 - x, 1 - y)]

        def dst_rows(p, origin):
            rows = pieces[p][1]
            start = (4 * origin[0] + 2 * origin[1] + origin[2]) * rows
            return outs[p].at[pl.ds(start, rows), :]

        def copies(k, origin, to, from_slab):
            out = []
            for p, (off, rows) in enumerate(pieces):
                dst = dst_rows(p, origin)
                src = slab_ref.at[pl.ds(off, rows), :] if from_slab else dst
                out.append(pltpu.make_async_remote_copy(
                    src_ref=src, dst_ref=dst, send_sem=send_sems.at[k, p], recv_sem=recv_sems.at[k, p],
                    device_id=to, device_id_type=MESH))
            return out

        mine = [pltpu.make_async_copy(slab_ref.at[pl.ds(off, rows), :], dst_rows(p, me), local_sems.at[p])
                for p, (off, rows) in enumerate(pieces)]
        for cp in mine:
            cp.start()
        first = copies(0, me, sibling, True)
        for j, chip in enumerate(chips):
            first += copies(1 + j, me, (*chip, c), True)
        for cp in first:
            cp.start()
        passed = []
        for j, chip in enumerate(chips):
            for cp in copies(1 + j, (*chip, c), me, False):
                cp.wait_recv()
            fwd = copies(4 + j, (*chip, c), sibling, False)
            for cp in fwd:
                cp.start()
            passed += fwd
        for cp in copies(0, sibling, me, False):
            cp.wait_recv()
        for j, chip in enumerate(chips):
            for cp in copies(4 + j, (*chip, 1 - c), me, False):
                cp.wait_recv()
        for cp in first + passed:
            cp.wait_send()
        for cp in mine:
            cp.wait()

    return pl.pallas_call(
        body, name=name,
        in_specs=[HBM_SPEC], out_specs=[HBM_SPEC] * n,
        out_shape=[SDS((N_DEV * rows, W), slab.dtype) for _, rows in pieces],
        scratch_shapes=[pltpu.SemaphoreType.DMA((7, n)), pltpu.SemaphoreType.DMA((7, n)), pltpu.SemaphoreType.DMA((n,))],
    )(slab)


def _rs_sibling(grads, pieces, name):
    W = grads[0].shape[1]
    R = sum(rows for _, rows in pieces)
    n = len(pieces)
    dt = grads[0].dtype
    max_rows = max(rows for _, rows in pieces)
    steps = [(q, p) for q in range(N_CHIP) for p in range(n)]
    ns = len(steps)
    ADD_ROWS = 64
    assert all(rows % ADD_ROWS == 0 for _, rows in pieces)

    def body(*refs):
        g_refs = refs[:n]
        sb_ref, mine_buf, send_buf, land_buf, out_buf, in_sems, out_sems, send_sems, recv_sems, credit = refs[n:]
        x, y, c = _place()
        sibling = (x, y, 1 - c)

        def loads(s):
            q, p = steps[s]
            rows = pieces[p][1]
            slot = s % 2
            mine = g_refs[p].at[pl.ds((2 * q + c) * rows, rows), :]
            theirs = g_refs[p].at[pl.ds((2 * q + 1 - c) * rows, rows), :]
            return (pltpu.make_async_copy(mine, mine_buf.at[slot, pl.ds(0, rows), :], in_sems.at[slot, 0]),
                    pltpu.make_async_copy(theirs, send_buf.at[slot, pl.ds(0, rows), :], in_sems.at[slot, 1]))

        def send(s):
            rows = pieces[steps[s][1]][1]
            slot = s % 2
            return pltpu.make_async_remote_copy(
                src_ref=send_buf.at[slot, pl.ds(0, rows), :], dst_ref=land_buf.at[slot, pl.ds(0, rows), :],
                send_sem=send_sems.at[slot], recv_sem=recv_sems.at[slot], device_id=sibling, device_id_type=MESH)

        def store(s):
            q, p = steps[s]
            off, rows = pieces[p]
            slot = s % 2
            return pltpu.make_async_copy(out_buf.at[slot, pl.ds(0, rows), :], sb_ref.at[q, pl.ds(off, rows), :],
                                         out_sems.at[slot])

        for cp in loads(0):
            cp.start()
        for s in range(ns):
            slot = s % 2
            rows = pieces[steps[s][1]][1]
            if s + 1 < ns:
                if s >= 1:
                    send(s - 1).wait_send()
                for cp in loads(s + 1):
                    cp.start()
            for cp in loads(s):
                cp.wait()
            if s >= 2:
                pl.semaphore_wait(credit.at[slot], 1)
            send(s).start()
            send(s).wait_recv()
            if s >= 2:
                store(s - 2).wait()

            def add(k, _, slot=slot):
                r = pl.ds(pl.multiple_of(k * ADD_ROWS, ADD_ROWS), ADD_ROWS)
                out_buf[slot, r, :] = (mine_buf[slot, r, :].astype(F32) + land_buf[slot, r, :].astype(F32)).astype(dt)
                return 0
            lax.fori_loop(0, rows // ADD_ROWS, add, 0)
            if s + 2 < ns:
                pl.semaphore_signal(credit.at[slot], inc=1, device_id=sibling, device_id_type=MESH)
            store(s).start()
        for s in range(max(ns - 2, 0), ns):
            send(s).wait_send()
            store(s).wait()

    buf = pltpu.VMEM((2, max_rows, W), dt)
    return pl.pallas_call(
        body, name=name,
        in_specs=[HBM_SPEC] * n, out_specs=HBM_SPEC,
        out_shape=SDS((N_CHIP, R, W), dt),
        scratch_shapes=[buf, buf, buf, buf, pltpu.SemaphoreType.DMA((2, 2)), pltpu.SemaphoreType.DMA((2,)),
                        pltpu.SemaphoreType.DMA((2,)), pltpu.SemaphoreType.DMA((2,)), pltpu.SemaphoreType.REGULAR((2,))],
        compiler_params=pltpu.CompilerParams(vmem_limit_bytes=40 * MIB),
    )(*grads)


SMALL_NAMES = ("ffn1_norm", "mix_norm", "ffn2_norm", "final_norm", "lru_conv_w", "lru_conv_b", "lru_w_a", "lru_b_a",
               "lru_w_i", "lru_b_i", "lru_lambda", "sc_conv_w", "lru_out_norm", "sc_out_norm")
WEIGHT_NAMES = ("ffn1_norm", "ffn1_w_gate", "ffn1_w_up", "ffn1_w_down", "mix_norm", "w_in", "lru_conv_w", "lru_conv_b",
                "lru_w_a", "lru_b_a", "lru_w_i", "lru_b_i", "lru_lambda", "sc_conv_w", "lru_out_norm", "sc_out_norm",
                "w_out", "ffn2_norm", "ffn2_w_gate", "ffn2_w_up", "ffn2_w_down", "final_norm")
BIG = (("ffn1_w_gate", True), ("ffn1_w_up", True), ("ffn1_w_down", False), ("ffn2_w_gate", True), ("ffn2_w_up", True),
       ("ffn2_w_down", False), ("w_in", True), ("w_out", False))


SLAB_ROW_ALIGN = 256


def _pack_rows(parts, width):
    rows, counts = [], []
    for p in parts:
        flat = p.reshape(-1)
        nr = -(-flat.shape[0] // width)
        nr = -(-nr // SUBLANES) * SUBLANES
        rows.append(jnp.pad(flat, (0, nr * width - flat.shape[0])).reshape(nr, width))
        counts.append(nr)
    total = sum(counts)
    pad = -(-total // SLAB_ROW_ALIGN) * SLAB_ROW_ALIGN - total
    if pad:
        rows.append(jnp.zeros((pad, width), rows[0].dtype))
    return jnp.concatenate(rows, axis=0), counts


def _stack_rows(blocks):
    pieces, off = [], 0
    for b in blocks:
        pieces.append((off, b.shape[0]))
        off += b.shape[0]
    return jnp.concatenate(blocks, axis=0), pieces


def _unpack_rows(slab, counts, shapes):
    out, r = [], 0
    for nr, shape in zip(counts, shapes):
        size = math.prod(shape)
        out.append(slab[r:r + nr].reshape(-1)[:size].reshape(shape))
        r += nr
    return out


def kernel(x, ffn1_norm, ffn1_w_gate, ffn1_w_up, ffn1_w_down, mix_norm, w_in, lru_conv_w, lru_conv_b, lru_w_a, lru_b_a, lru_w_i, lru_b_i, lru_lambda, sc_conv_w, lru_out_norm, sc_out_norm, w_out, ffn2_norm, ffn2_w_gate, ffn2_w_up, ffn2_w_down, final_norm, loss_target, m_ffn1_norm, m_ffn1_w_gate, m_ffn1_w_up, m_ffn1_w_down, m_mix_norm, m_w_in, m_lru_conv_w, m_lru_conv_b, m_lru_w_a, m_lru_b_a, m_lru_w_i, m_lru_b_i, m_lru_lambda, m_sc_conv_w, m_lru_out_norm, m_sc_out_norm, m_w_out, m_ffn2_norm, m_ffn2_w_gate, m_ffn2_w_up, m_ffn2_w_down, m_final_norm, v_ffn1_norm, v_ffn1_w_gate, v_ffn1_w_up, v_ffn1_w_down, v_mix_norm, v_w_in, v_lru_conv_w, v_lru_conv_b, v_lru_w_a, v_lru_b_a, v_lru_w_i, v_lru_b_i, v_lru_lambda, v_sc_conv_w, v_lru_out_norm, v_sc_out_norm, v_w_out, v_ffn2_norm, v_ffn2_w_gate, v_ffn2_w_up, v_ffn2_w_down, v_final_norm):
    a = dict(locals())
    w = {n: a[n] for n in WEIGHT_NAMES}
    m = {n: a["m_" + n] for n in WEIGHT_NAMES}
    v = {n: a["v_" + n] for n in WEIGHT_NAMES}
    ax, ay, ac = _place()
    dev = 4 * ax + 2 * ay + ac
    chip = (2 * ax + ay).astype(jnp.int32).reshape(1)

    x0 = x[0]
    tgt = loss_target[0]
    T, D = x0.shape
    C = D // 2
    H, hd = lru_w_a.shape[1], lru_w_a.shape[2]
    CL = lru_conv_w.shape[2]

    shards = []
    for name, transposed in BIG:
        s = w[name][0]
        shards.append((s.T if transposed else s).astype(BF16))
    taps = jnp.concatenate([lru_conv_w[0], sc_conv_w[0], jnp.zeros((1, CL), F32)], axis=0)
    taps_row = lax.bitcast_convert_type(taps, BF16).reshape(1, -1)
    taps_blk = jnp.pad(taps_row, ((0, BF16_ROWS - 1), (0, D - taps_row.shape[1])))
    s_wg1, s_wu1, s_wd1, s_wg2, s_wu2, s_wd2, s_win, s_wout = shards
    slab_gu1, pcs_gu1 = _stack_rows([s_wg1, s_wu1])
    slab_d1, pcs_d1 = _stack_rows([s_wd1])
    slab_mw, pcs_mw = _stack_rows([s_win, s_wout, taps_blk])
    slab_gu2, pcs_gu2 = _stack_rows([s_wg2, s_wu2])
    slab_d2, pcs_d2 = _stack_rows([s_wd2])
    wg1, wu1 = _allgather(slab_gu1, pcs_gu1, "allgather_ffn1_up")

    g1, gm, g3 = ffn1_norm, mix_norm, ffn2_norm
    phase = _merge_phases(_ag_direct_phase(slab_d1, pcs_d1, ALL_OTHERS),
                          _ag_direct_phase(slab_mw, pcs_mw, SAME_CORE_AND_SIBLING))
    (n1, hg1, hu1, act1), got = _norm_proj(x0, g1, [wg1, wu1], [BF16, BF16], True, "ffn1_up", carried=phase)
    wd1, mixw = got[0], got[1:]
    phase = _merge_phases(_ag_forward_phase(mixw, pcs_mw), _ag_direct_phase(slab_gu2, pcs_gu2, SAME_CORE_AND_SIBLING))
    x1, got = _mm_res(act1, wd1, x0, 0.5, "ffn1_down", carried=phase)
    (win, wout, taps_all), gu2 = got[:3], got[3:]
    phase = _merge_phases(_ag_forward_phase(gu2, pcs_gu2), _ag_direct_phase(slab_d2, pcs_d2, SAME_CORE_AND_SIBLING))
    (n2, z), got = _norm_proj(x1, gm, [win], [F32], False, "in_proj", carried=phase)
    (wg2, wu2), d2 = got[:2], got[2:]
    taps_all = taps_all.reshape(N_DEV, BF16_ROWS, D)[:, 0, :2 * SUBLANES * CL].reshape(N_DEV, SUBLANES, CL, 2)
    taps_all = lax.bitcast_convert_type(taps_all, F32)
    taps_all = taps_all.transpose(1, 0, 2).reshape(SUBLANES, N_DEV * CL)
    cw, sw = taps_all[0:4], taps_all[4:7]

    gf = final_norm.reshape(1, D)
    cb = lru_conv_b
    wa, wi = lru_w_a[0].astype(BF16), lru_w_i[0].astype(BF16)
    ba, bi = lru_b_a.reshape(1, C), lru_b_i.reshape(1, C)
    lam, glo, gso = lru_lambda, lru_out_norm, sc_out_norm

    y, h = _mix_fwd(z, cw, cb, wa, ba, wi, bi, lam, sw, glo, gso, "mix_fwd")
    x2, (wd2,) = _mm_res(y, wout, x1, 1.0, "out_proj", carried=_ag_forward_phase(d2, pcs_d2))
    n3, hg2, hu2, act2 = _norm_proj(x2, g3, [wg2, wu2], [BF16, BF16], True, "ffn2_up")
    x3 = _mm_res(act2, wd2, x2, 0.5, "ffn2_down")
    dx3, df2, d_gf, loss_blk = _loss_head(x3, gf, tgt, "loss_head")

    F = wd1.shape[0]
    bm_f = F // 4 if (F // 4) % LANES == 0 else 512

    def reduce_group(gs, tag):
        pcs, off = [], 0
        for g_ in gs:
            pcs.append((off, g_.shape[0] // N_DEV))
            off += g_.shape[0] // N_DEV
        sb_ = _rs_sibling(gs, pcs, "rs_sibling_add_" + tag)
        return sb_, pcs

    dhg2, dhu2 = _ffn_bwd_act(df2, wd2, hg2, hu2, "ffn2_bwd_act")
    d_wd2 = _dw_tn(act2, df2, bm_f, "ffn2_dw_down")
    d_wg2 = _dw_tn(dhg2, n3, bm_f, "ffn2_dw_gate")
    d_wu2 = _dw_tn(dhu2, n3, bm_f, "ffn2_dw_up")
    sb_f2, pcs_f2 = reduce_group([d_wg2, d_wu2, d_wd2], "ffn2")
    (dx2, dx2b, d_g3), (lb_f2,) = _mm_rmsbwd([(dhg2, wg2), (dhu2, wu2)], x2, g3, dx3, 1.0, "ffn2_bwd_in",
                                             carried=_rs_chips_phase(sb_f2))
    dy = _mm_nt(dx2b, wout, "out_proj_bwd")
    d_wout = _dw_tn(y, dx2b, 1024, "out_proj_dw")
    dz, small, d_wa, d_wi = _mix_bwd(z, h, dy, cw, cb, wa, ba, wi, bi, lam, sw, glo, gso, "mix_bwd")
    d_win = _dw_tn(dz, n2, 1280, "in_proj_dw")
    sb_mx, pcs_mx = reduce_group([d_win, d_wout], "mix")
    (dx1, df1, d_gm), (lb_mx,) = _mm_rmsbwd([(dz, win)], x1, gm, dx2, 0.5, "in_proj_bwd",
                                            carried=_rs_chips_phase(sb_mx))
    dhg1, dhu1 = _ffn_bwd_act(df1, wd1, hg1, hu1, "ffn1_bwd_act")
    d_wd1 = _dw_tn(act1, df1, bm_f, "ffn1_dw_down")
    d_wg1 = _dw_tn(dhg1, n1, bm_f, "ffn1_dw_gate")
    d_wu1 = _dw_tn(dhu1, n1, bm_f, "ffn1_dw_up")
    sb_f1, pcs_f1 = reduce_group([d_wg1, d_wu1, d_wd1], "ffn1")
    (dx0, _, d_g1), (lb_f1,) = _mm_rmsbwd([(dhg1, wg1), (dhu1, wu1)], x0, g1, dx1, 1.0, "ffn1_bwd_in",
                                          carried=_rs_chips_phase(sb_f1))

    big_sum = {}
    for tag, names, sb_, lb_, pcs in (("ffn2", ("ffn2_w_gate", "ffn2_w_up", "ffn2_w_down"), sb_f2, lb_f2, pcs_f2),
                                      ("mix", ("w_in", "w_out"), sb_mx, lb_mx, pcs_mx),
                                      ("ffn1", ("ffn1_w_gate", "ffn1_w_up", "ffn1_w_down"), sb_f1, lb_f1, pcs_f1)):
        gsum = _final_grad(sb_, lb_, chip, "rs_final_sum_" + tag)
        for name, (off, rows) in zip(names, pcs):
            big_sum[name] = gsum[off:off + rows]

    small_parts = [d_g1, d_gm, d_g3, d_gf, small[R_CW:R_CW + 4], small[R_CB], d_wa, small[R_BA], d_wi, small[R_BI],
                   small[R_LAM], small[R_SW:R_SW + 3], small[R_GLO], small[R_GSO]]
    sslab, counts = _pack_rows(small_parts, LANES)
    RS = sslab.shape[0]
    (sg,) = _allgather(sslab, [(0, RS)], "allgather_small_grads")
    ssum = _add_slabs([sg[j * RS:(j + 1) * RS] for j in range(N_DEV)], F32, "small_grads_sum")
    full_shapes = [(1, D), (1, D), (1, D), (D,), (1, 4, C), (1, C), (1, H, hd, hd), (1, H, hd), (1, H, hd, hd), (1, H, hd),
                   (1, C), (1, 3, C), (1, C), (1, C)]
    small_full = dict(zip(SMALL_NAMES, _unpack_rows(ssum, counts, full_shapes)))

    grads = {}
    for name, transposed in BIG:
        gblk = big_sum[name]
        grads[name] = (gblk.T if transposed else gblk)[None]
    for name in SMALL_NAMES:
        gfull = small_full[name]
        if name in ("lru_conv_w", "sc_conv_w"):
            gfull = lax.dynamic_slice_in_dim(gfull, dev * CL, CL, axis=2)
        grads[name] = gfull

    delta, new_m, new_v = {}, {}, {}
    for name, transposed in BIG:
        flip = transposed and w[name].shape[2] % LANES != 0
        view = (lambda t: t[0].T) if flip else (lambda t: t[0])
        back = (lambda t: t.T[None]) if flip else (lambda t: t[None])
        gview = big_sum[name] if flip else grads[name][0]
        d_, m_, v_ = _adamw(view(w[name]), gview, view(m[name]), view(v[name]), "adamw_" + name)
        delta[name], new_m[name], new_v[name] = back(d_), back(m_), back(v_)
    packs = [_pack_rows([t[n_] for n_ in SMALL_NAMES], LANES) for t in (w, grads, m, v)]
    sd, sm, sv = _adamw(packs[0][0], packs[1][0], packs[2][0], packs[3][0], "adamw_small")
    shapes = [w[n_].shape for n_ in SMALL_NAMES]
    for tgt_dict, slab_ in ((delta, sd), (new_m, sm), (new_v, sv)):
        for n_, val in zip(SMALL_NAMES, _unpack_rows(slab_, packs[0][1], shapes)):
            tgt_dict[n_] = val

    loss = lax.psum(loss_blk[0, 0], ("x", "y", "c"))
    return (loss, dx0[None], *[grads[n_] for n_ in WEIGHT_NAMES], *[delta[n_] for n_ in WEIGHT_NAMES],
            *[new_m[n_] for n_ in WEIGHT_NAMES], *[new_v[n_] for n_ in WEIGHT_NAMES])
```

```python
import functools
import math

import jax
import jax.numpy as jnp
from jax import lax
from jax.experimental import pallas as pl
from jax.experimental.pallas import tpu as pltpu

F32 = jnp.float32
BF16 = jnp.bfloat16
SDS = jax.ShapeDtypeStruct
MESH = pl.DeviceIdType.MESH

NORM_EPS = 1e-6
LRU_C = 8.0
N_DEV = 8
N_CHIP = 4
ADAM_LR, ADAM_B1, ADAM_B2, ADAM_EPS, ADAM_WD, ADAM_STEP = 0.001, 0.9, 0.999, 1e-08, 0.01, 10

NN = (((1,), (0,)), ((), ()))
NT = (((1,), (1,)), ((), ()))
TN = (((0,), (0,)), ((), ()))

SUBLANES = 8
BF16_ROWS = 16
LANES = 128
MIB = 1 << 20


def _dot(a, b, dims):
    return lax.dot_general(a, b, dims, preferred_element_type=F32)


def _blk(n, pref, align):
    if n <= pref:
        return n
    b = (pref // align) * align
    while b >= align:
        if n % b == 0:
            return b
        b -= align
    raise ValueError(f"no block of {n} aligned to {align} under {pref}")


def _cp(sem, vmem_mib):
    return pltpu.CompilerParams(dimension_semantics=sem, vmem_limit_bytes=vmem_mib * MIB)


HBM_SPEC = pl.BlockSpec(memory_space=pltpu.HBM)


class _Carried:
    def __init__(self, inputs, out_shapes, aliases, sem_shapes, build):
        self.inputs, self.out_shapes, self.aliases = list(inputs), list(out_shapes), dict(aliases)
        self.sem_shapes, self.build = list(sem_shapes), build


def _call(body, *, name, grid, in_specs, out_specs, out_shape, scratch_shapes, compiler_params, args, carried=None):
    if carried is None:
        return pl.pallas_call(body, name=name, grid=grid, in_specs=in_specs, out_specs=out_specs, out_shape=out_shape,
                              scratch_shapes=scratch_shapes, compiler_params=compiler_params)(*args)
    n_in, n_out, n_sc = len(in_specs), len(out_shape), len(scratch_shapes)
    c_in, c_out = len(carried.inputs), len(carried.out_shapes)

    def hosted(*refs):
        ins, refs = refs[:n_in], refs[n_in:]
        c_ins, refs = refs[:c_in], refs[c_in:]
        outs, refs = refs[:n_out], refs[n_out:]
        c_outs, refs = refs[:c_out], refs[c_out:]
        scratch, c_sems = refs[:n_sc], refs[n_sc:]
        first = functools.reduce(jnp.logical_and, [pl.program_id(a) == 0 for a in range(len(grid))])
        last = functools.reduce(jnp.logical_and, [pl.program_id(a) == g - 1 for a, g in enumerate(grid)])

        @pl.when(first)
        def _():
            for start in carried.build(c_ins, c_outs, c_sems, True):
                start()

        body(*ins, *outs, *scratch)

        @pl.when(last)
        def _():
            for wait in carried.build(c_ins, c_outs, c_sems, False):
                wait()

    out = pl.pallas_call(
        hosted, name=name, grid=grid, in_specs=list(in_specs) + [HBM_SPEC] * c_in,
        out_specs=list(out_specs) + [HBM_SPEC] * c_out, out_shape=list(out_shape) + carried.out_shapes,
        scratch_shapes=list(scratch_shapes) + carried.sem_shapes,
        input_output_aliases={n_in + a: n_out + b for a, b in carried.aliases.items()},
        compiler_params=compiler_params)(*args, *carried.inputs)
    return out[:n_out], out[n_out:]


ROW_CHUNK = 128


def _chunk_rows(c):
    return pl.ds(pl.multiple_of(c * ROW_CHUNK, ROW_CHUNK), ROW_CHUNK)


def _rstd(xv):
    return lax.rsqrt(jnp.mean(xv * xv, axis=-1, keepdims=True) + NORM_EPS)


def _rms_bwd(xv, g, dn):
    r = _rstd(xv)
    xr = xv * r
    gd = g * dn
    dx = r * (gd - xr * jnp.mean(gd * xr, axis=-1, keepdims=True))
    return dx, jnp.sum(dn * xr, axis=0, keepdims=True)


def _log1p(e):
    u = 1.0 + e
    return jnp.where(u == 1.0, e, jnp.log(u) * (e / (u - 1.0)))


def _one_minus_exp(v, exp_half_v):
    series = 1.0 / 5040.0
    for coeff in (1.0 / 720.0, 1.0 / 120.0, 1.0 / 24.0, 1.0 / 6.0, 0.5, 1.0):
        series = series * v + coeff
    return jnp.where(v > -0.5, -v * series, 1.0 - exp_half_v * exp_half_v)


def _sigmoid(v):
    return 0.5 * jnp.tanh(0.5 * v) + 0.5


def _gelu_parts(g):
    k0 = math.sqrt(2.0 / math.pi)
    g2 = g * g
    t = jnp.tanh(k0 * (g + 0.044715 * g * g2))
    gel = 0.5 * g * (1.0 + t)
    gelp = 0.5 * (1.0 + t) + 0.5 * g * (1.0 - t * t) * (k0 * (1.0 + 3.0 * 0.044715 * g2))
    return gel, gelp


def _norm_proj(x, gain, w_list, out_dtypes, swiglu, name, carried=None):
    T, D = x.shape
    N = w_list[0].shape[0]
    nw = len(w_list)
    bm = _blk(T, 1024, BF16_ROWS)
    bn = _blk(N, 512, LANES)

    def body(*refs):
        x_ref, g_ref = refs[:2]
        w_refs = refs[2:2 + nw]
        n_ref = refs[2 + nw]
        o_refs = refs[3 + nw:3 + 2 * nw]
        act_ref = refs[3 + 2 * nw] if swiglu else None
        n_sc = refs[-1]

        @pl.when(pl.program_id(1) == 0)
        def _():
            def chunk(c, _):
                r = _chunk_rows(c)
                xv = x_ref[r, :]
                nb = (xv * _rstd(xv) * g_ref[...]).astype(BF16)
                n_sc[r, :] = nb
                n_ref[r, :] = nb
                return 0
            lax.fori_loop(0, bm // ROW_CHUNK, chunk, 0)

        n = n_sc[...]
        outs = [_dot(n, w_ref[...], NT) for w_ref in w_refs]
        for o_ref, o in zip(o_refs, outs):
            o_ref[...] = o.astype(o_ref.dtype)
        if swiglu:
            hg, hu = outs
            act_ref[...] = (hg * _sigmoid(hg) * hu).astype(BF16)

    row = pl.BlockSpec((bm, D), lambda i, j: (i, 0))
    tile = pl.BlockSpec((bm, bn), lambda i, j: (i, j))
    n_extra = 1 if swiglu else 0
    return _call(
        body, name=name, grid=(T // bm, N // bn),
        in_specs=[row, pl.BlockSpec((1, D), lambda i, j: (0, 0))] + [pl.BlockSpec((bn, D), lambda i, j: (j, 0))] * nw,
        out_specs=[row] + [tile] * (nw + n_extra),
        out_shape=[SDS((T, D), BF16)] + [SDS((T, N), dt) for dt in out_dtypes] + [SDS((T, N), BF16)] * n_extra,
        scratch_shapes=[pltpu.VMEM((bm, D), BF16)],
        compiler_params=_cp(("arbitrary", "arbitrary"), 52),
        args=(x, gain, *w_list), carried=carried)


def _mm_res(a, b, x, scale, name, carried=None):
    T, K = a.shape
    D = b.shape[1]
    bm = _blk(T, 1024, BF16_ROWS)
    bk = _blk(K, 1408, LANES)
    nk = K // bk

    def body(a_ref, b_ref, x_ref, o_ref):
        k = pl.program_id(1)

        @pl.when(k == 0)
        def _():
            o_ref[...] = jnp.zeros_like(o_ref)

        o_ref[...] += _dot(a_ref[...], b_ref[...], NN)

        @pl.when(k == nk - 1)
        def _():
            def chunk(c, _):
                r = _chunk_rows(c)
                o_ref[r, :] = x_ref[r, :] + scale * o_ref[r, :]
                return 0
            lax.fori_loop(0, bm // ROW_CHUNK, chunk, 0)

    row = pl.BlockSpec((bm, D), lambda i, k: (i, 0))
    out = _call(
        body, name=name, grid=(T // bm, nk),
        in_specs=[pl.BlockSpec((bm, bk), lambda i, k: (i, k)), pl.BlockSpec((bk, D), lambda i, k: (k, 0)), row],
        out_specs=[row], out_shape=[SDS((T, D), F32)], scratch_shapes=[],
        compiler_params=_cp(("arbitrary", "arbitrary"), 56),
        args=(a, b, x), carried=carried)
    return out[0] if carried is None else (out[0][0], out[1])


def _mm_nt(a, b, name):
    T, K = a.shape
    N = b.shape[0]
    bm = _blk(T, 1024, BF16_ROWS)
    bn = _blk(N, 512, LANES)

    def body(a_ref, b_ref, o_ref):
        o_ref[...] = _dot(a_ref[...], b_ref[...], NT)

    return pl.pallas_call(
        body, name=name, grid=(T // bm, N // bn),
        in_specs=[pl.BlockSpec((bm, K), lambda i, j: (i, 0)), pl.BlockSpec((bn, K), lambda i, j: (j, 0))],
        out_specs=pl.BlockSpec((bm, bn), lambda i, j: (i, j)), out_shape=SDS((T, N), F32),
        compiler_params=_cp(("arbitrary", "arbitrary"), 40),
    )(a, b)


def _ffn_bwd_act(dfb, wd, hg, hu, name):
    T, D = dfb.shape
    F = wd.shape[0]
    bm = _blk(T, 1024, BF16_ROWS)
    bn = _blk(F, 512, LANES)

    def body(df_ref, wd_ref, hg_ref, hu_ref, dhg_ref, dhu_ref):
        dact = _dot(df_ref[...], wd_ref[...], NT)
        hgv = hg_ref[...].astype(F32)
        huv = hu_ref[...].astype(F32)
        s = _sigmoid(hgv)
        dhu_ref[...] = (dact * (hgv * s)).astype(BF16)
        dhg_ref[...] = (dact * huv * (s * (1.0 + hgv * (1.0 - s)))).astype(BF16)

    tile = pl.BlockSpec((bm, bn), lambda i, j: (i, j))
    return pl.pallas_call(
        body, name=name, grid=(T // bm, F // bn),
        in_specs=[pl.BlockSpec((bm, D), lambda i, j: (i, 0)), pl.BlockSpec((bn, D), lambda i, j: (j, 0)), tile, tile],
        out_specs=[tile, tile], out_shape=[SDS((T, F), BF16)] * 2,
        compiler_params=_cp(("arbitrary", "arbitrary"), 40),
    )(dfb, wd, hg, hu)


def _dw_tn(a, b, bm_pref, name):
    T, M = a.shape
    N = b.shape[1]
    bm = _blk(M, bm_pref, LANES)
    tk = _blk(T, 1024, BF16_ROWS)
    nk = T // tk

    def body(a_ref, b_ref, o_ref, acc):
        k = pl.program_id(1)

        @pl.when(k == 0)
        def _():
            acc[...] = jnp.zeros_like(acc)

        acc[...] += _dot(a_ref[...], b_ref[...], TN)

        @pl.when(k == nk - 1)
        def _():
            o_ref[...] = acc[...].astype(BF16)

    return pl.pallas_call(
        body, name=name, grid=(M // bm, nk),
        in_specs=[pl.BlockSpec((tk, bm), lambda i, k: (k, i)), pl.BlockSpec((tk, N), lambda i, k: (k, 0))],
        out_specs=pl.BlockSpec((bm, N), lambda i, k: (i, 0)), out_shape=SDS((M, N), BF16),
        scratch_shapes=[pltpu.VMEM((bm, N), F32)],
        compiler_params=_cp(("arbitrary", "arbitrary"), 48),
    )(a, b)


def _mm_rmsbwd(pairs, x, gain, dx_in, bscale, name, carried=None):
    T, D = x.shape
    K = pairs[0][0].shape[1]
    npair = len(pairs)
    bm = _blk(T, 1024, BF16_ROWS)
    bk = _blk(K, 1024 // npair, LANES)
    nk = K // bk

    def body(*refs):
        ab = refs[:2 * npair]
        x_ref, g_ref, dxin_ref, dx_ref, dxb_ref, dg_ref = refs[2 * npair:]
        i = pl.program_id(0)
        k = pl.program_id(1)

        @pl.when(k == 0)
        def _():
            dx_ref[...] = jnp.zeros_like(dx_ref)

        for q in range(npair):
            dx_ref[...] += _dot(ab[2 * q][...], ab[2 * q + 1][...], NN)

        @pl.when(k == nk - 1)
        def _():
            @pl.when(i == 0)
            def _():
                dg_ref[...] = jnp.zeros_like(dg_ref)

            def chunk(c, _):
                r = _chunk_rows(c)
                dx, dg = _rms_bwd(x_ref[r, :], g_ref[...], dx_ref[r, :])
                dxo = dxin_ref[r, :] + dx
                dx_ref[r, :] = dxo
                dxb_ref[r, :] = (bscale * dxo).astype(BF16)
                dg_ref[...] += dg
                return 0
            lax.fori_loop(0, bm // ROW_CHUNK, chunk, 0)

    row = pl.BlockSpec((bm, D), lambda i, k: (i, 0))
    row_once = pl.BlockSpec((bm, D), lambda i, k: (i, 0), pipeline_mode=pl.Buffered(1))
    vec = pl.BlockSpec((1, D), lambda i, k: (0, 0))
    in_specs = []
    args = []
    for a, b in pairs:
        in_specs += [pl.BlockSpec((bm, bk), lambda i, k: (i, k)), pl.BlockSpec((bk, D), lambda i, k: (k, 0))]
        args += [a, b]
    return _call(
        body, name=name, grid=(T // bm, nk),
        in_specs=in_specs + [row_once, vec, row_once], out_specs=[row, row, vec],
        out_shape=[SDS((T, D), F32), SDS((T, D), BF16), SDS((1, D), F32)],
        scratch_shapes=[],
        compiler_params=_cp(("arbitrary", "arbitrary"), 58),
        args=(*args, x, gain, dx_in), carried=carried)


def _loss_head(x3, gain, tgt, name):
    T, D = x3.shape
    bm = _blk(T, 256, BF16_ROWS)

    def body(x_ref, g_ref, t_ref, dx_ref, dxb_ref, dg_ref, loss_ref):
        i = pl.program_id(0)
        xv = x_ref[...]
        g = g_ref[...]
        out = xv * _rstd(xv) * g
        e = out - t_ref[...]
        part = 0.5 * jnp.sum(jnp.mean(e * e, axis=-1, keepdims=True), axis=0, keepdims=True)
        dx, dg = _rms_bwd(xv, g, e * (1.0 / D))
        dx_ref[...] = dx
        dxb_ref[...] = (0.5 * dx).astype(BF16)

        @pl.when(i == 0)
        def _():
            dg_ref[...] = dg
            loss_ref[...] = jnp.broadcast_to(part, loss_ref.shape)

        @pl.when(i > 0)
        def _():
            dg_ref[...] += dg
            loss_ref[...] += jnp.broadcast_to(part, loss_ref.shape)

    row = pl.BlockSpec((bm, D), lambda i: (i, 0))
    vec = pl.BlockSpec((1, D), lambda i: (0, 0))
    return pl.pallas_call(
        body, name=name, grid=(T // bm,),
        in_specs=[row, vec, row], out_specs=[row, row, vec, pl.BlockSpec((SUBLANES, LANES), lambda i: (0, 0))],
        out_shape=[SDS((T, D), F32), SDS((T, D), BF16), SDS((1, D), F32), SDS((SUBLANES, LANES), F32)],
        compiler_params=_cp(("arbitrary",), 40),
    )(x3, gain, tgt)


R_CW, R_CB, R_BA, R_BI, R_LAM, R_SW, R_GLO, R_GSO, SMALL_ROWS = 0, 4, 5, 6, 7, 8, 11, 12, 16


def _rows(g):
    return pl.ds(pl.multiple_of(g * SUBLANES, SUBLANES), SUBLANES)


def _shift_back(prev, cur, d):
    row = lax.broadcasted_iota(jnp.int32, cur.shape, 0)
    return pltpu.roll(jnp.where(row >= SUBLANES - d, prev, cur), d, 0)


def _shift_fwd(cur, nxt, d):
    row = lax.broadcasted_iota(jnp.int32, cur.shape, 0)
    return pltpu.roll(jnp.where(row < d, nxt, cur), SUBLANES - d, 0)


def _causal_conv(ext, g, taps_ref, ntap):
    prev = ext[_rows(g), :]
    cur = ext[_rows(g + 1), :]
    out = _shift_back(prev, cur, ntap - 1) * taps_ref[0:1, :]
    for k in range(1, ntap - 1):
        out = out + _shift_back(prev, cur, ntap - 1 - k) * taps_ref[k:k + 1, :]
    return out + cur * taps_ref[ntap - 1:ntap, :]


def _scan8(A, U, reverse):
    row = lax.broadcasted_iota(jnp.int32, A.shape, 0)
    for s in (1, 2, 4):
        if reverse:
            A_sh = pltpu.roll(A, SUBLANES - s, 0)
            U_sh = pltpu.roll(U, SUBLANES - s, 0)
            m = row < SUBLANES - s
        else:
            A_sh = pltpu.roll(A, s, 0)
            U_sh = pltpu.roll(U, s, 0)
            m = row >= s
        U = jnp.where(m, A * U_sh + U, U)
        A = jnp.where(m, A * A_sh, A)
    return A, U


def _gate_pre(xc_s, w_ref, out_s, H, hd):
    for h in range(H):
        cs = slice(h * hd, (h + 1) * hd)
        out_s[:, cs] = _dot(xc_s[:, cs].astype(BF16), w_ref[h], NN)


def _lru_coeffs(pa, pi, xc, ba, bi, sp):
    ra = _sigmoid(pa + ba)
    ri = _sigmoid(pi + bi)
    log_a = (-LRU_C * ra) * sp
    a = jnp.exp(log_a)
    mult = jnp.sqrt(_one_minus_exp(2.0 * log_a, a))
    return ra, ri, a, mult


def _softplus_neg(lam):
    v = -lam
    return jnp.maximum(v, 0.0) + _log1p(jnp.exp(-jnp.abs(v)))


def _mix_fwd(z, cw, cb, wa, ba, wi, bi, lam, sw, glo, gso, name):
    T = z.shape[0]
    C = z.shape[1] // 5
    H = wa.shape[0]
    hd = C // H
    tb = _blk(T, 256, BF16_ROWS)
    ng = tb // SUBLANES
    HDR = SUBLANES

    def body(z_ref, cw_ref, cb_ref, wa_ref, ba_ref, wi_ref, bi_ref, lam_ref, sw_ref, glo_ref, gso_ref,
             y_ref, h_ref, xext, pext, xc_s, pa_s, pi_s, y_s, hcar):
        @pl.when(pl.program_id(0) == 0)
        def _():
            xext[0:HDR, :] = jnp.zeros((HDR, C), F32)
            pext[0:HDR, :] = jnp.zeros((HDR, C), F32)
            hcar[...] = jnp.zeros_like(hcar)

        def fill(g, _):
            r = _rows(g)
            re = _rows(g + 1)
            xext[re, :] = z_ref[r, 0:C]
            pext[re, :] = z_ref[r, 3 * C:4 * C] * z_ref[r, 4 * C:5 * C]
            return 0
        lax.fori_loop(0, ng, fill, 0)

        def conv(g, _):
            xc_s[_rows(g), :] = _causal_conv(xext, g, cw_ref, 4) + cb_ref[...]
            return 0
        lax.fori_loop(0, ng, conv, 0)

        _gate_pre(xc_s, wa_ref, pa_s, H, hd)
        _gate_pre(xc_s, wi_ref, pi_s, H, hd)
        sp = _softplus_neg(lam_ref[...])

        def group(g, hprev):
            r = _rows(g)
            xc = xc_s[r, :]
            _, ri, a, mult = _lru_coeffs(pa_s[r, :], pi_s[r, :], xc, ba_ref[...], bi_ref[...], sp)
            A, U = _scan8(a, mult * (ri * xc), reverse=False)
            hh = A * hprev + U
            h_ref[r, :] = hh
            gel, _ = _gelu_parts(z_ref[r, C:2 * C])
            y_lru = hh * gel
            y_s[r, 0:C] = y_lru * _rstd(y_lru) * glo_ref[...]
            y_sc = z_ref[r, 2 * C:3 * C] * _causal_conv(pext, g, sw_ref, 3)
            y_s[r, C:2 * C] = y_sc * _rstd(y_sc) * gso_ref[...]
            return jnp.broadcast_to(hh[SUBLANES - 1:SUBLANES, :], hh.shape)
        hcar[...] = lax.fori_loop(0, ng, group, hcar[...])

        xext[0:HDR, :] = xext[tb:tb + HDR, :]
        pext[0:HDR, :] = pext[tb:tb + HDR, :]

        def cast(g, _):
            r = pl.ds(pl.multiple_of(g * BF16_ROWS, BF16_ROWS), BF16_ROWS)
            y_ref[r, :] = y_s[r, :].astype(BF16)
            return 0
        lax.fori_loop(0, tb // BF16_ROWS, cast, 0)

    full = lambda shape: pl.BlockSpec(shape, lambda i: (0,) * len(shape))
    blk = lambda w: pl.BlockSpec((tb, w), lambda i: (i, 0))
    ext = pltpu.VMEM((tb + HDR, C), F32)
    tile = pltpu.VMEM((tb, C), F32)
    return pl.pallas_call(
        body, name=name, grid=(T // tb,),
        in_specs=[blk(5 * C), full((4, C)), full((1, C)), full((H, hd, hd)), full((1, C)), full((H, hd, hd)),
                  full((1, C)), full((1, C)), full((3, C)), full((1, C)), full((1, C))],
        out_specs=[blk(2 * C), blk(C)],
        out_shape=[SDS((T, 2 * C), BF16), SDS((T, C), F32)],
        scratch_shapes=[ext, ext, tile, tile, tile, pltpu.VMEM((tb, 2 * C), F32), pltpu.VMEM((SUBLANES, C), F32)],
        compiler_params=_cp(("arbitrary",), 40),
    )(z, cw, cb, wa, ba, wi, bi, lam, sw, glo, gso)


def _mix_bwd(z, h, dy, cw, cb, wa, ba, wi, bi, lam, sw, glo, gso, name):
    T = z.shape[0]
    C = z.shape[1] // 5
    H = wa.shape[0]
    hd = C // H
    tb = _blk(T, 256, BF16_ROWS)
    nb = T // tb
    ng = tb // SUBLANES
    HDR = SUBLANES
    N_ACC = 13

    def body(z_ref, zp_ref, h_ref, hp_ref, dy_ref, cw_ref, cb_ref, wa_ref, ba_ref, wi_ref, bi_ref, lam_ref,
             sw_ref, glo_ref, gso_ref, dz_ref, small_ref, dwa_ref, dwi_ref,
             xext, pext, hext, dqext, dxcext, bext, xc_s, pa_s, pi_s, a_s, m_s, ri_s, dh_s, dpa_s, dpi_s,
             dz_s, acc_s, bcar):
        i = pl.program_id(0)
        first_rows = i == nb - 1

        @pl.when(i == 0)
        def _():
            dqext[tb:tb + HDR, :] = jnp.zeros((HDR, C), F32)
            dxcext[tb:tb + HDR, :] = jnp.zeros((HDR, C), F32)
            bcar[...] = jnp.zeros_like(bcar)
            acc_s[...] = jnp.zeros_like(acc_s)
            dwa_ref[...] = jnp.zeros_like(dwa_ref)
            dwi_ref[...] = jnp.zeros_like(dwi_ref)

        zero = jnp.zeros((HDR, C), F32)
        xext[0:HDR, :] = jnp.where(first_rows, zero, zp_ref[:, 0:C])
        pext[0:HDR, :] = jnp.where(first_rows, zero, zp_ref[:, 3 * C:4 * C] * zp_ref[:, 4 * C:5 * C])
        hext[0:HDR, :] = jnp.where(first_rows, zero, hp_ref[...])

        def fill(g, _):
            r = _rows(g)
            re = _rows(g + 1)
            xext[re, :] = z_ref[r, 0:C]
            pext[re, :] = z_ref[r, 3 * C:4 * C] * z_ref[r, 4 * C:5 * C]
            hext[re, :] = h_ref[r, :]
            return 0
        lax.fori_loop(0, ng, fill, 0)

        def conv(g, _):
            xc_s[_rows(g), :] = _causal_conv(xext, g, cw_ref, 4) + cb_ref[...]
            return 0
        lax.fori_loop(0, ng, conv, 0)

        _gate_pre(xc_s, wa_ref, pa_s, H, hd)
        _gate_pre(xc_s, wi_ref, pi_s, H, hd)
        sp = _softplus_neg(lam_ref[...])
        dsp_dlam = -jax.nn.sigmoid(-lam_ref[...])

        def add_acc(k, v):
            acc_s[k] += v

        def p1(g, _):
            r = _rows(g)
            xc = xc_s[r, :]
            _, ri, a, mult = _lru_coeffs(pa_s[r, :], pi_s[r, :], xc, ba_ref[...], bi_ref[...], sp)
            a_s[r, :] = a
            m_s[r, :] = mult
            ri_s[r, :] = ri
            hh = h_ref[r, :]
            gel, gelp = _gelu_parts(z_ref[r, C:2 * C])
            y_lru = hh * gel
            dnl = dy_ref[r, 0:C]
            rl = _rstd(y_lru)
            ylr = y_lru * rl
            gd = glo_ref[...] * dnl
            dy_lru = rl * (gd - ylr * jnp.mean(gd * ylr, axis=-1, keepdims=True))
            add_acc(R_GLO, dnl * ylr)
            dz_s[r, C:2 * C] = dy_lru * hh * gelp
            dh = dy_lru * gel
            dh_s[r, :] = dh

            q = _causal_conv(pext, g, sw_ref, 3)
            scb = z_ref[r, 2 * C:3 * C]
            y_sc = scb * q
            dns = dy_ref[r, C:2 * C]
            rs = _rstd(y_sc)
            ysr = y_sc * rs
            gs = gso_ref[...] * dns
            dy_sc = rs * (gs - ysr * jnp.mean(gs * ysr, axis=-1, keepdims=True))
            add_acc(R_GSO, dns * ysr)
            dz_s[r, 2 * C:3 * C] = dy_sc * q
            dqext[r, :] = dy_sc * scb
            return 0
        lax.fori_loop(0, ng, p1, 0)

        bext[tb:tb + HDR, :] = bcar[...]

        def p2(j, carry):
            g = ng - 1 - j
            r = _rows(g)
            a = a_s[r, :]
            A, U = _scan8(a, a * dh_s[r, :], reverse=True)
            bb = A * carry + U
            bext[r, :] = bb
            return jnp.broadcast_to(bb[0:1, :], bb.shape)
        bcar[...] = lax.fori_loop(0, ng, p2, bcar[...])

        def p3(g, _):
            r = _rows(g)
            rn = _rows(g + 1)
            G = dh_s[r, :] + _shift_fwd(bext[r, :], bext[rn, :], 1)
            hm1 = _shift_back(hext[r, :], hext[rn, :], 1)
            a = a_s[r, :]
            mult = m_s[r, :]
            ri = ri_s[r, :]
            xc = xc_s[r, :]
            ra = _sigmoid(pa_s[r, :] + ba_ref[...])
            dxcext[r, :] = G * mult * ri
            dri = G * mult * xc
            dmult = G * ri * xc
            dlog_a = (G * hm1) * a - dmult * (a * a) / mult
            add_acc(R_LAM, dlog_a * (-LRU_C * ra) * dsp_dlam)
            dpa = dlog_a * (-LRU_C * sp) * ra * (1.0 - ra)
            dpi = dri * ri * (1.0 - ri)
            add_acc(R_BA, dpa)
            add_acc(R_BI, dpi)
            dpa_s[r, :] = dpa
            dpi_s[r, :] = dpi
            return 0
        lax.fori_loop(0, ng, p3, 0)

        for hh_ in range(H):
            cs = slice(hh_ * hd, (hh_ + 1) * hd)
            dpa_b = dpa_s[:, cs].astype(BF16)
            dpi_b = dpi_s[:, cs].astype(BF16)
            xc_b = xc_s[:, cs].astype(BF16)
            dxcext[0:tb, cs] += _dot(dpa_b, wa_ref[hh_], NT) + _dot(dpi_b, wi_ref[hh_], NT)
            dwa_ref[hh_] += _dot(xc_b, dpa_b, TN)
            dwi_ref[hh_] += _dot(xc_b, dpi_b, TN)

        def p4(g, _):
            r = _rows(g)
            rn = _rows(g + 1)
            dxc = dxcext[r, :]
            dxc_n = dxcext[rn, :]
            x_p = xext[r, :]
            x_c = xext[rn, :]
            add_acc(R_CB, dxc)
            dlx = dxc * cw_ref[3:4, :]
            add_acc(R_CW + 3, dxc * x_c)
            for d in range(1, 4):
                dlx = dlx + _shift_fwd(dxc, dxc_n, d) * cw_ref[3 - d:4 - d, :]
                add_acc(R_CW + 3 - d, dxc * _shift_back(x_p, x_c, d))
            dz_s[r, 0:C] = dlx

            dq = dqext[r, :]
            dq_n = dqext[rn, :]
            p_p = pext[r, :]
            p_c = pext[rn, :]
            dp = dq * sw_ref[2:3, :]
            add_acc(R_SW + 2, dq * p_c)
            for d in range(1, 3):
                dp = dp + _shift_fwd(dq, dq_n, d) * sw_ref[2 - d:3 - d, :]
                add_acc(R_SW + 2 - d, dq * _shift_back(p_p, p_c, d))
            dz_s[r, 3 * C:4 * C] = dp * z_ref[r, 4 * C:5 * C]
            dz_s[r, 4 * C:5 * C] = dp * z_ref[r, 3 * C:4 * C]
            return 0
        lax.fori_loop(0, ng, p4, 0)

        dqext[tb:tb + HDR, :] = dqext[0:HDR, :]
        dxcext[tb:tb + HDR, :] = dxcext[0:HDR, :]

        def cast(g, _):
            r = pl.ds(pl.multiple_of(g * BF16_ROWS, BF16_ROWS), BF16_ROWS)
            dz_ref[r, :] = dz_s[r, :].astype(BF16)
            return 0
        lax.fori_loop(0, tb // BF16_ROWS, cast, 0)

        @pl.when(i == nb - 1)
        def _():
            small_ref[...] = jnp.zeros_like(small_ref)
            for k in range(N_ACC):
                small_ref[k:k + 1, :] = jnp.sum(acc_s[k], axis=0, keepdims=True)

    tpg = tb // SUBLANES
    full = lambda shape: pl.BlockSpec(shape, lambda i: (0,) * len(shape))
    blk = lambda w: pl.BlockSpec((tb, w), lambda i: (nb - 1 - i, 0))
    prev = lambda w: pl.BlockSpec((SUBLANES, w), lambda i: (jnp.maximum((nb - 1 - i) * tpg - 1, 0), 0))
    ext = pltpu.VMEM((tb + HDR, C), F32)
    tile = pltpu.VMEM((tb, C), F32)
    return pl.pallas_call(
        body, name=name, grid=(nb,),
        in_specs=[blk(5 * C), prev(5 * C), blk(C), prev(C), blk(2 * C), full((4, C)), full((1, C)), full((H, hd, hd)),
                  full((1, C)), full((H, hd, hd)), full((1, C)), full((1, C)), full((3, C)), full((1, C)), full((1, C))],
        out_specs=[blk(5 * C), full((SMALL_ROWS, C)), full((H, hd, hd)), full((H, hd, hd))],
        out_shape=[SDS((T, 5 * C), BF16), SDS((SMALL_ROWS, C), F32), SDS((H, hd, hd), F32), SDS((H, hd, hd), F32)],
        scratch_shapes=[ext] * 6 + [tile] * 9 + [pltpu.VMEM((tb, 5 * C), F32), pltpu.VMEM((N_ACC, SUBLANES, C), F32),
                                                pltpu.VMEM((SUBLANES, C), F32)],
        compiler_params=_cp(("arbitrary",), 56),
    )(z, z, h, h, dy, cw, cb, wa, ba, wi, bi, lam, sw, glo, gso)


def _add_slabs(terms, out_dtype, name):
    R, Ccols = terms[0].shape
    br = _blk(R, 512, BF16_ROWS)
    n = len(terms)

    def body(*refs):
        s = refs[0][...].astype(F32)
        for t_ref in refs[1:n]:
            s = s + t_ref[...].astype(F32)
        refs[n][...] = s.astype(out_dtype)

    spec = pl.BlockSpec((br, Ccols), lambda i: (i, 0))
    return pl.pallas_call(
        body, name=name, grid=(R // br,), in_specs=[spec] * n, out_specs=spec, out_shape=SDS((R, Ccols), out_dtype),
        compiler_params=_cp(("arbitrary",), 40),
    )(*terms)


def _final_grad(sb, lb, chip, name):
    _, R, Ccols = sb.shape
    br = _blk(R, 512, BF16_ROWS)

    def body(chip_ref, sb_ref, l0, l1, l2, o_ref):
        s = sb_ref[0].astype(F32)
        for t_ref in (l0, l1, l2):
            s = s + t_ref[0].astype(F32)
        o_ref[...] = s

    lspec = lambda k: pl.BlockSpec((1, br, Ccols), lambda i, c: (k, i, 0))
    return pl.pallas_call(
        body, name=name,
        grid_spec=pltpu.PrefetchScalarGridSpec(
            num_scalar_prefetch=1, grid=(R // br,),
            in_specs=[pl.BlockSpec((1, br, Ccols), lambda i, c: (c[0], i, 0)), lspec(0), lspec(1), lspec(2)],
            out_specs=pl.BlockSpec((br, Ccols), lambda i, c: (i, 0))),
        out_shape=SDS((R, Ccols), F32),
        compiler_params=_cp(("arbitrary",), 40),
    )(chip, sb, lb, lb, lb)


def _adamw(w, g, m, v, name):
    R, Ccols = w.shape
    br = _blk(R, 256, SUBLANES)
    c1 = 1.0 - ADAM_B1 ** ADAM_STEP
    c2 = 1.0 - ADAM_B2 ** ADAM_STEP

    def body(w_ref, g_ref, m_ref, v_ref, d_ref, nm_ref, nv_ref):
        gv = g_ref[...]
        nm = ADAM_B1 * m_ref[...] + (1.0 - ADAM_B1) * gv
        nv = ADAM_B2 * v_ref[...] + (1.0 - ADAM_B2) * (gv * gv)
        nm_ref[...] = nm
        nv_ref[...] = nv
        d_ref[...] = -ADAM_LR * ((nm / c1) / (jnp.sqrt(nv / c2) + ADAM_EPS) + ADAM_WD * w_ref[...])

    spec = pl.BlockSpec((br, Ccols), lambda i: (i, 0))
    return pl.pallas_call(
        body, name=name, grid=(R // br,), in_specs=[spec] * 4, out_specs=[spec] * 3,
        out_shape=[SDS((R, Ccols), F32)] * 3, compiler_params=_cp(("arbitrary",), 40),
    )(w, g, m, v)


def _place():
    return lax.axis_index("x"), lax.axis_index("y"), lax.axis_index("c")


def _dev_rows(ref, dev, rows):
    return ref.at[pl.ds((4 * dev[0] + 2 * dev[1] + dev[2]) * rows, rows), :]


def _remote(src, dst, send_sem, recv_sem, to):
    return pltpu.make_async_remote_copy(src_ref=src, dst_ref=dst, send_sem=send_sem, recv_sem=recv_sem,
                                        device_id=to, device_id_type=MESH)


SAME_CORE_AND_SIBLING = ((0, 0, 1), (1, 0, 0), (0, 1, 0), (1, 1, 0))
ALL_OTHERS = SAME_CORE_AND_SIBLING + ((1, 0, 1), (0, 1, 1), (1, 1, 1))


def _merge_phases(a, b):
    na_in, na_out, na_sem = len(a.inputs), len(a.out_shapes), len(a.sem_shapes)

    def build(ins, outs, sems, starting):
        return (a.build(ins[:na_in], outs[:na_out], sems[:na_sem], starting)
                + b.build(ins[na_in:], outs[na_out:], sems[na_sem:], starting))

    aliases = dict(a.aliases)
    aliases.update({na_in + i: na_out + o for i, o in b.aliases.items()})
    return _Carried(a.inputs + b.inputs, a.out_shapes + b.out_shapes, aliases, a.sem_shapes + b.sem_shapes, build)


def _ag_direct_phase(slab, pieces, flips):
    W = slab.shape[1]
    n = len(pieces)
    npeer = len(flips)

    def build(ins, outs, sems, starting):
        (slab_ref,) = ins
        send_sems, recv_sems, local_sems = sems
        x, y, c = _place()
        me = (x, y, c)
        peers = [tuple(1 - v if f else v for v, f in zip(me, flip)) for flip in flips]
        todo = []
        for p, (off, rows) in enumerate(pieces):
            src = slab_ref.at[pl.ds(off, rows), :]
            mine = pltpu.make_async_copy(src, _dev_rows(outs[p], me, rows), local_sems.at[p])
            todo.append(mine.start if starting else mine.wait)
            for k, peer in enumerate(peers):
                snd = _remote(src, _dev_rows(outs[p], me, rows), send_sems.at[k, p], recv_sems.at[k, p], peer)
                if starting:
                    todo.append(snd.start)
                else:
                    theirs = _dev_rows(outs[p], peer, rows)
                    rcv = _remote(theirs, theirs, send_sems.at[k, p], recv_sems.at[k, p], me)
                    todo += [rcv.wait_recv, snd.wait_send]
        return todo

    dma = pltpu.SemaphoreType.DMA
    return _Carried([slab], [SDS((N_DEV * rows, W), slab.dtype) for _, rows in pieces], {},
                    [dma((npeer, n)), dma((npeer, n)), dma((n,))], build)


def _ag_forward_phase(gathered, pieces):
    n = len(pieces)

    def build(ins, outs, sems, starting):
        send_sems, recv_sems = sems
        x, y, c = _place()
        me, sibling = (x, y, c), (x, y, 1 - c)
        chips = [(1 - x, y), (x, 1 - y), (1 - x, 1 - y)]
        todo = []
        for p, (_, rows) in enumerate(pieces):
            for j, chip in enumerate(chips):
                snd = _remote(_dev_rows(ins[p], (*chip, c), rows), _dev_rows(outs[p], (*chip, c), rows),
                              send_sems.at[j, p], recv_sems.at[j, p], sibling)
                if starting:
                    todo.append(snd.start)
                else:
                    theirs = _dev_rows(outs[p], (*chip, 1 - c), rows)
                    rcv = _remote(theirs, theirs, send_sems.at[j, p], recv_sems.at[j, p], me)
                    todo += [rcv.wait_recv, snd.wait_send]
        return todo

    dma = pltpu.SemaphoreType.DMA
    return _Carried(gathered, [SDS(g.shape, g.dtype) for g in gathered], {p: p for p in range(n)},
                    [dma((3, n)), dma((3, n))], build)


def _rs_chips_phase(sb):
    _, R, W = sb.shape

    def build(ins, outs, sems, starting):
        (sb_ref,), (land_ref,) = ins, outs
        send_sems, recv_sems = sems
        x, y, c = _place()
        chips = [(1 - x, y), (x, 1 - y), (1 - x, 1 - y)]
        cps = [_remote(sb_ref.at[2 * chip[0] + chip[1]], land_ref.at[j], send_sems.at[j], recv_sems.at[j], (*chip, c))
               for j, chip in enumerate(chips)]
        if starting:
            return [cp.start for cp in cps]
        return [cp.wait_recv for cp in cps] + [cp.wait_send for cp in cps]

    dma = pltpu.SemaphoreType.DMA
    return _Carried([sb], [SDS((3, R, W), sb.dtype)], {}, [dma((3,)), dma((3,))], build)


def _allgather(slab, pieces, name):
    R, W = slab.shape
    n = len(pieces)
    assert sum(rows for _, rows in pieces) == R

    def body(slab_ref, *refs):
        outs = refs[:n]
        send_sems, recv_sems, local_sems = refs[n:]
        x, y, c = _place()
        me, sibling = (x, y, c), (x, y, 1 - c)
        chips = [(1 - x, y), (x, 1 - y), (1 - x, 1 - y)]

        def dst_rows(p, origin):
            rows = pieces[p][1]
            start = (4 * origin[0] + 2 * origin[1] + origin[2]) * rows
            return outs[p].at[pl.ds(start, rows), :]

        def copies(k, origin, to, from_slab):
            out = []
            for p, (off, rows) in enumerate(pieces):
                dst = dst_rows(p, origin)
                src = slab_ref.at[pl.ds(off, rows), :] if from_slab else dst
                out.append(pltpu.make_async_remote_copy(
                    src_ref=src, dst_ref=dst, send_sem=send_sems.at[k, p], recv_sem=recv_sems.at[k, p],
                    device_id=to, device_id_type=MESH))
            return out

        mine = [pltpu.make_async_copy(slab_ref.at[pl.ds(off, rows), :], dst_rows(p, me), local_sems.at[p])
                for p, (off, rows) in enumerate(pieces)]
        for cp in mine:
            cp.start()
        first = copies(0, me, sibling, True)
        for j, chip in enumerate(chips):
            first += copies(1 + j, me, (*chip, c), True)
        for cp in first:
            cp.start()
        passed = []
        for j, chip in enumerate(chips):
            for cp in copies(1 + j, (*chip, c), me, False):
                cp.wait_recv()
            fwd = copies(4 + j, (*chip, c), sibling, False)
            for cp in fwd:
                cp.start()
            passed += fwd
        for cp in copies(0, sibling, me, False):
            cp.wait_recv()
        for j, chip in enumerate(chips):
            for cp in copies(4 + j, (*chip, 1 - c), me, False):
                cp.wait_recv()
        for cp in first + passed:
            cp.wait_send()
        for cp in mine:
            cp.wait()

    return pl.pallas_call(
        body, name=name,
        in_specs=[HBM_SPEC], out_specs=[HBM_SPEC] * n,
        out_shape=[SDS((N_DEV * rows, W), slab.dtype) for _, rows in pieces],
        scratch_shapes=[pltpu.SemaphoreType.DMA((7, n)), pltpu.SemaphoreType.DMA((7, n)), pltpu.SemaphoreType.DMA((n,))],
    )(slab)


def _rs_sibling(grads, pieces, name):
    W = grads[0].shape[1]
    R = sum(rows for _, rows in pieces)
    n = len(pieces)
    dt = grads[0].dtype
    max_rows = max(rows for _, rows in pieces)
    steps = [(q, p) for q in range(N_CHIP) for p in range(n)]
    ns = len(steps)
    ADD_ROWS = 64
    assert all(rows % ADD_ROWS == 0 for _, rows in pieces)

    def body(*refs):
        g_refs = refs[:n]
        sb_ref, mine_buf, send_buf, land_buf, out_buf, in_sems, out_sems, send_sems, recv_sems, credit = refs[n:]
        x, y, c = _place()
        sibling = (x, y, 1 - c)

        def loads(s):
            q, p = steps[s]
            rows = pieces[p][1]
            slot = s % 2
            mine = g_refs[p].at[pl.ds((2 * q + c) * rows, rows), :]
            theirs = g_refs[p].at[pl.ds((2 * q + 1 - c) * rows, rows), :]
            return (pltpu.make_async_copy(mine, mine_buf.at[slot, pl.ds(0, rows), :], in_sems.at[slot, 0]),
                    pltpu.make_async_copy(theirs, send_buf.at[slot, pl.ds(0, rows), :], in_sems.at[slot, 1]))

        def send(s):
            rows = pieces[steps[s][1]][1]
            slot = s % 2
            return pltpu.make_async_remote_copy(
                src_ref=send_buf.at[slot, pl.ds(0, rows), :], dst_ref=land_buf.at[slot, pl.ds(0, rows), :],
                send_sem=send_sems.at[slot], recv_sem=recv_sems.at[slot], device_id=sibling, device_id_type=MESH)

        def store(s):
            q, p = steps[s]
            off, rows = pieces[p]
            slot = s % 2
            return pltpu.make_async_copy(out_buf.at[slot, pl.ds(0, rows), :], sb_ref.at[q, pl.ds(off, rows), :],
                                         out_sems.at[slot])

        for cp in loads(0):
            cp.start()
        for s in range(ns):
            slot = s % 2
            rows = pieces[steps[s][1]][1]
            if s + 1 < ns:
                if s >= 1:
                    send(s - 1).wait_send()
                for cp in loads(s + 1):
                    cp.start()
            for cp in loads(s):
                cp.wait()
            if s >= 2:
                pl.semaphore_wait(credit.at[slot], 1)
            send(s).start()
            send(s).wait_recv()
            if s >= 2:
                store(s - 2).wait()

            def add(k, _, slot=slot):
                r = pl.ds(pl.multiple_of(k * ADD_ROWS, ADD_ROWS), ADD_ROWS)
                out_buf[slot, r, :] = (mine_buf[slot, r, :].astype(F32) + land_buf[slot, r, :].astype(F32)).astype(dt)
                return 0
            lax.fori_loop(0, rows // ADD_ROWS, add, 0)
            if s + 2 < ns:
                pl.semaphore_signal(credit.at[slot], inc=1, device_id=sibling, device_id_type=MESH)
            store(s).start()
        for s in range(max(ns - 2, 0), ns):
            send(s).wait_send()
            store(s).wait()

    buf = pltpu.VMEM((2, max_rows, W), dt)
    return pl.pallas_call(
        body, name=name,
        in_specs=[HBM_SPEC] * n, out_specs=HBM_SPEC,
        out_shape=SDS((N_CHIP, R, W), dt),
        scratch_shapes=[buf, buf, buf, buf, pltpu.SemaphoreType.DMA((2, 2)), pltpu.SemaphoreType.DMA((2,)),
                        pltpu.SemaphoreType.DMA((2,)), pltpu.SemaphoreType.DMA((2,)), pltpu.SemaphoreType.REGULAR((2,))],
        compiler_params=pltpu.CompilerParams(vmem_limit_bytes=40 * MIB),
    )(*grads)


SMALL_NAMES = ("ffn1_norm", "mix_norm", "ffn2_norm", "final_norm", "lru_conv_w", "lru_conv_b", "lru_w_a", "lru_b_a",
               "lru_w_i", "lru_b_i", "lru_lambda", "sc_conv_w", "lru_out_norm", "sc_out_norm")
WEIGHT_NAMES = ("ffn1_norm", "ffn1_w_gate", "ffn1_w_up", "ffn1_w_down", "mix_norm", "w_in", "lru_conv_w", "lru_conv_b",
                "lru_w_a", "lru_b_a", "lru_w_i", "lru_b_i", "lru_lambda", "sc_conv_w", "lru_out_norm", "sc_out_norm",
                "w_out", "ffn2_norm", "ffn2_w_gate", "ffn2_w_up", "ffn2_w_down", "final_norm")
BIG = (("ffn1_w_gate", True), ("ffn1_w_up", True), ("ffn1_w_down", False), ("ffn2_w_gate", True), ("ffn2_w_up", True),
       ("ffn2_w_down", False), ("w_in", True), ("w_out", False))


SLAB_ROW_ALIGN = 256


def _pack_rows(parts, width):
    rows, counts = [], []
    for p in parts:
        flat = p.reshape(-1)
        nr = -(-flat.shape[0] // width)
        nr = -(-nr // SUBLANES) * SUBLANES
        rows.append(jnp.pad(flat, (0, nr * width - flat.shape[0])).reshape(nr, width))
        counts.append(nr)
    total = sum(counts)
    pad = -(-total // SLAB_ROW_ALIGN) * SLAB_ROW_ALIGN - total
    if pad:
        rows.append(jnp.zeros((pad, width), rows[0].dtype))
    return jnp.concatenate(rows, axis=0), counts


def _stack_rows(blocks):
    pieces, off = [], 0
    for b in blocks:
        pieces.append((off, b.shape[0]))
        off += b.shape[0]
    return jnp.concatenate(blocks, axis=0), pieces


def _unpack_rows(slab, counts, shapes):
    out, r = [], 0
    for nr, shape in zip(counts, shapes):
        size = math.prod(shape)
        out.append(slab[r:r + nr].reshape(-1)[:size].reshape(shape))
        r += nr
    return out


def kernel(x, ffn1_norm, ffn1_w_gate, ffn1_w_up, ffn1_w_down, mix_norm, w_in, lru_conv_w, lru_conv_b, lru_w_a, lru_b_a, lru_w_i, lru_b_i, lru_lambda, sc_conv_w, lru_out_norm, sc_out_norm, w_out, ffn2_norm, ffn2_w_gate, ffn2_w_up, ffn2_w_down, final_norm, loss_target, m_ffn1_norm, m_ffn1_w_gate, m_ffn1_w_up, m_ffn1_w_down, m_mix_norm, m_w_in, m_lru_conv_w, m_lru_conv_b, m_lru_w_a, m_lru_b_a, m_lru_w_i, m_lru_b_i, m_lru_lambda, m_sc_conv_w, m_lru_out_norm, m_sc_out_norm, m_w_out, m_ffn2_norm, m_ffn2_w_gate, m_ffn2_w_up, m_ffn2_w_down, m_final_norm, v_ffn1_norm, v_ffn1_w_gate, v_ffn1_w_up, v_ffn1_w_down, v_mix_norm, v_w_in, v_lru_conv_w, v_lru_conv_b, v_lru_w_a, v_lru_b_a, v_lru_w_i, v_lru_b_i, v_lru_lambda, v_sc_conv_w, v_lru_out_norm, v_sc_out_norm, v_w_out, v_ffn2_norm, v_ffn2_w_gate, v_ffn2_w_up, v_ffn2_w_down, v_final_norm):
    a = dict(locals())
    w = {n: a[n] for n in WEIGHT_NAMES}
    m = {n: a["m_" + n] for n in WEIGHT_NAMES}
    v = {n: a["v_" + n] for n in WEIGHT_NAMES}
    ax, ay, ac = _place()
    dev = 4 * ax + 2 * ay + ac
    chip = (2 * ax + ay).astype(jnp.int32).reshape(1)

    x0 = x[0]
    tgt = loss_target[0]
    T, D = x0.shape
    C = D // 2
    H, hd = lru_w_a.shape[1], lru_w_a.shape[2]
    CL = lru_conv_w.shape[2]

    shards = []
    for name, transposed in BIG:
        s = w[name][0]
        shards.append((s.T if transposed else s).astype(BF16))
    taps = jnp.concatenate([lru_conv_w[0], sc_conv_w[0], jnp.zeros((1, CL), F32)], axis=0)
    taps_row = lax.bitcast_convert_type(taps, BF16).reshape(1, -1)
    taps_blk = jnp.pad(taps_row, ((0, BF16_ROWS - 1), (0, D - taps_row.shape[1])))
    s_wg1, s_wu1, s_wd1, s_wg2, s_wu2, s_wd2, s_win, s_wout = shards
    slab_gu1, pcs_gu1 = _stack_rows([s_wg1, s_wu1])
    slab_d1, pcs_d1 = _stack_rows([s_wd1])
    slab_mw, pcs_mw = _stack_rows([s_win, s_wout, taps_blk])
    slab_gu2, pcs_gu2 = _stack_rows([s_wg2, s_wu2])
    slab_d2, pcs_d2 = _stack_rows([s_wd2])
    wg1, wu1 = _allgather(slab_gu1, pcs_gu1, "allgather_ffn1_up")

    g1, gm, g3 = ffn1_norm, mix_norm, ffn2_norm
    phase = _merge_phases(_ag_direct_phase(slab_d1, pcs_d1, ALL_OTHERS),
                          _ag_direct_phase(slab_mw, pcs_mw, SAME_CORE_AND_SIBLING))
    (n1, hg1, hu1, act1), got = _norm_proj(x0, g1, [wg1, wu1], [BF16, BF16], True, "ffn1_up", carried=phase)
    wd1, mixw = got[0], got[1:]
    phase = _merge_phases(_ag_forward_phase(mixw, pcs_mw), _ag_direct_phase(slab_gu2, pcs_gu2, SAME_CORE_AND_SIBLING))
    x1, got = _mm_res(act1, wd1, x0, 0.5, "ffn1_down", carried=phase)
    (win, wout, taps_all), gu2 = got[:3], got[3:]
    phase = _merge_phases(_ag_forward_phase(gu2, pcs_gu2), _ag_direct_phase(slab_d2, pcs_d2, SAME_CORE_AND_SIBLING))
    (n2, z), got = _norm_proj(x1, gm, [win], [F32], False, "in_proj", carried=phase)
    (wg2, wu2), d2 = got[:2], got[2:]
    taps_all = taps_all.reshape(N_DEV, BF16_ROWS, D)[:, 0, :2 * SUBLANES * CL].reshape(N_DEV, SUBLANES, CL, 2)
    taps_all = lax.bitcast_convert_type(taps_all, F32)
    taps_all = taps_all.transpose(1, 0, 2).reshape(SUBLANES, N_DEV * CL)
    cw, sw = taps_all[0:4], taps_all[4:7]

    gf = final_norm.reshape(1, D)
    cb = lru_conv_b
    wa, wi = lru_w_a[0].astype(BF16), lru_w_i[0].astype(BF16)
    ba, bi = lru_b_a.reshape(1, C), lru_b_i.reshape(1, C)
    lam, glo, gso = lru_lambda, lru_out_norm, sc_out_norm

    y, h = _mix_fwd(z, cw, cb, wa, ba, wi, bi, lam, sw, glo, gso, "mix_fwd")
    x2, (wd2,) = _mm_res(y, wout, x1, 1.0, "out_proj", carried=_ag_forward_phase(d2, pcs_d2))
    n3, hg2, hu2, act2 = _norm_proj(x2, g3, [wg2, wu2], [BF16, BF16], True, "ffn2_up")
    x3 = _mm_res(act2, wd2, x2, 0.5, "ffn2_down")
    dx3, df2, d_gf, loss_blk = _loss_head(x3, gf, tgt, "loss_head")

    F = wd1.shape[0]
    bm_f = F // 4 if (F // 4) % LANES == 0 else 512

    def reduce_group(gs, tag):
        pcs, off = [], 0
        for g_ in gs:
            pcs.append((off, g_.shape[0] // N_DEV))
            off += g_.shape[0] // N_DEV
        sb_ = _rs_sibling(gs, pcs, "rs_sibling_add_" + tag)
        return sb_, pcs

    dhg2, dhu2 = _ffn_bwd_act(df2, wd2, hg2, hu2, "ffn2_bwd_act")
    d_wd2 = _dw_tn(act2, df2, bm_f, "ffn2_dw_down")
    d_wg2 = _dw_tn(dhg2, n3, bm_f, "ffn2_dw_gate")
    d_wu2 = _dw_tn(dhu2, n3, bm_f, "ffn2_dw_up")
    sb_f2, pcs_f2 = reduce_group([d_wg2, d_wu2, d_wd2], "ffn2")
    (dx2, dx2b, d_g3), (lb_f2,) = _mm_rmsbwd([(dhg2, wg2), (dhu2, wu2)], x2, g3, dx3, 1.0, "ffn2_bwd_in",
                                             carried=_rs_chips_phase(sb_f2))
    dy = _mm_nt(dx2b, wout, "out_proj_bwd")
    d_wout = _dw_tn(y, dx2b, 1024, "out_proj_dw")
    dz, small, d_wa, d_wi = _mix_bwd(z, h, dy, cw, cb, wa, ba, wi, bi, lam, sw, glo, gso, "mix_bwd")
    d_win = _dw_tn(dz, n2, 1280, "in_proj_dw")
    sb_mx, pcs_mx = reduce_group([d_win, d_wout], "mix")
    (dx1, df1, d_gm), (lb_mx,) = _mm_rmsbwd([(dz, win)], x1, gm, dx2, 0.5, "in_proj_bwd",
                                            carried=_rs_chips_phase(sb_mx))
    dhg1, dhu1 = _ffn_bwd_act(df1, wd1, hg1, hu1, "ffn1_bwd_act")
    d_wd1 = _dw_tn(act1, df1, bm_f, "ffn1_dw_down")
    d_wg1 = _dw_tn(dhg1, n1, bm_f, "ffn1_dw_gate")
    d_wu1 = _dw_tn(dhu1, n1, bm_f, "ffn1_dw_up")
    sb_f1, pcs_f1 = reduce_group([d_wg1, d_wu1, d_wd1], "ffn1")
    (dx0, _, d_g1), (lb_f1,) = _mm_rmsbwd([(dhg1, wg1), (dhu1, wu1)], x0, g1, dx1, 1.0, "ffn1_bwd_in",
                                          carried=_rs_chips_phase(sb_f1))

    big_sum = {}
    for tag, names, sb_, lb_, pcs in (("ffn2", ("ffn2_w_gate", "ffn2_w_up", "ffn2_w_down"), sb_f2, lb_f2, pcs_f2),
                                      ("mix", ("w_in", "w_out"), sb_mx, lb_mx, pcs_mx),
                                      ("ffn1", ("ffn1_w_gate", "ffn1_w_up", "ffn1_w_down"), sb_f1, lb_f1, pcs_f1)):
        gsum = _final_grad(sb_, lb_, chip, "rs_final_sum_" + tag)
        for name, (off, rows) in zip(names, pcs):
            big_sum[name] = gsum[off:off + rows]

    small_parts = [d_g1, d_gm, d_g3, d_gf, small[R_CW:R_CW + 4], small[R_CB], d_wa, small[R_BA], d_wi, small[R_BI],
                   small[R_LAM], small[R_SW:R_SW + 3], small[R_GLO], small[R_GSO]]
    sslab, counts = _pack_rows(small_parts, LANES)
    RS = sslab.shape[0]
    (sg,) = _allgather(sslab, [(0, RS)], "allgather_small_grads")
    ssum = _add_slabs([sg[j * RS:(j + 1) * RS] for j in range(N_DEV)], F32, "small_grads_sum")
    full_shapes = [(1, D), (1, D), (1, D), (D,), (1, 4, C), (1, C), (1, H, hd, hd), (1, H, hd), (1, H, hd, hd), (1, H, hd),
                   (1, C), (1, 3, C), (1, C), (1, C)]
    small_full = dict(zip(SMALL_NAMES, _unpack_rows(ssum, counts, full_shapes)))

    grads = {}
    for name, transposed in BIG:
        gblk = big_sum[name]
        grads[name] = (gblk.T if transposed else gblk)[None]
    for name in SMALL_NAMES:
        gfull = small_full[name]
        if name in ("lru_conv_w", "sc_conv_w"):
            gfull = lax.dynamic_slice_in_dim(gfull, dev * CL, CL, axis=2)
        grads[name] = gfull

    delta, new_m, new_v = {}, {}, {}
    for name, transposed in BIG:
        flip = transposed and w[name].shape[2] % LANES != 0
        view = (lambda t: t[0].T) if flip else (lambda t: t[0])
        back = (lambda t: t.T[None]) if flip else (lambda t: t[None])
        gview = big_sum[name] if flip else grads[name][0]
        d_, m_, v_ = _adamw(view(w[name]), gview, view(m[name]), view(v[name]), "adamw_" + name)
        delta[name], new_m[name], new_v[name] = back(d_), back(m_), back(v_)
    packs = [_pack_rows([t[n_] for n_ in SMALL_NAMES], LANES) for t in (w, grads, m, v)]
    sd, sm, sv = _adamw(packs[0][0], packs[1][0], packs[2][0], packs[3][0], "adamw_small")
    shapes = [w[n_].shape for n_ in SMALL_NAMES]
    for tgt_dict, slab_ in ((delta, sd), (new_m, sm), (new_v, sv)):
        for n_, val in zip(SMALL_NAMES, _unpack_rows(slab_, packs[0][1], shapes)):
            tgt_dict[n_] = val

    loss = lax.psum(loss_blk[0, 0], ("x", "y", "c"))
    return (loss, dx0[None], *[grads[n_] for n_ in WEIGHT_NAMES], *[delta[n_] for n_ in WEIGHT_NAMES],
            *[new_m[n_] for n_ in WEIGHT_NAMES], *[new_v[n_] for n_ in WEIGHT_NAMES])
```

```python
import functools
import math

import jax
import jax.numpy as jnp
from jax import lax
from jax.experimental import pallas as pl
from jax.experimental.pallas import tpu as pltpu

F32 = jnp.float32
BF16 = jnp.bfloat16
SDS = jax.ShapeDtypeStruct
MESH = pl.DeviceIdType.MESH

NORM_EPS = 1e-6
LRU_C = 8.0
N_DEV = 8
N_CHIP = 4
ADAM_LR, ADAM_B1, ADAM_B2, ADAM_EPS, ADAM_WD, ADAM_STEP = 0.001, 0.9, 0.999, 1e-08, 0.01, 10

NN = (((1,), (0,)), ((), ()))
NT = (((1,), (1,)), ((), ()))
TN = (((0,), (0,)), ((), ()))

SUBLANES = 8
BF16_ROWS = 16
LANES = 128
MIB = 1 << 20


def _dot(a, b, dims):
    return lax.dot_general(a, b, dims, preferred_element_type=F32)


def _blk(n, pref, align):
    if n <= pref:
        return n
    b = (pref // align) * align
    while b >= align:
        if n % b == 0:
            return b
        b -= align
    raise ValueError(f"no block of {n} aligned to {align} under {pref}")


def _cp(sem, vmem_mib):
    return pltpu.CompilerParams(dimension_semantics=sem, vmem_limit_bytes=vmem_mib * MIB)


HBM_SPEC = pl.BlockSpec(memory_space=pltpu.HBM)


class _Carried:
    def __init__(self, inputs, out_shapes, aliases, sem_shapes, build):
        self.inputs, self.out_shapes, self.aliases = list(inputs), list(out_shapes), dict(aliases)
        self.sem_shapes, self.build = list(sem_shapes), build


def _call(body, *, name, grid, in_specs, out_specs, out_shape, scratch_shapes, compiler_params, args, carried=None):
    if carried is None:
        return pl.pallas_call(body, name=name, grid=grid, in_specs=in_specs, out_specs=out_specs, out_shape=out_shape,
                              scratch_shapes=scratch_shapes, compiler_params=compiler_params)(*args)
    n_in, n_out, n_sc = len(in_specs), len(out_shape), len(scratch_shapes)
    c_in, c_out = len(carried.inputs), len(carried.out_shapes)

    def hosted(*refs):
        ins, refs = refs[:n_in], refs[n_in:]
        c_ins, refs = refs[:c_in], refs[c_in:]
        outs, refs = refs[:n_out], refs[n_out:]
        c_outs, refs = refs[:c_out], refs[c_out:]
        scratch, c_sems = refs[:n_sc], refs[n_sc:]
        first = functools.reduce(jnp.logical_and, [pl.program_id(a) == 0 for a in range(len(grid))])
        last = functools.reduce(jnp.logical_and, [pl.program_id(a) == g - 1 for a, g in enumerate(grid)])

        @pl.when(first)
        def _():
            for start in carried.build(c_ins, c_outs, c_sems, True):
                start()

        body(*ins, *outs, *scratch)

        @pl.when(last)
        def _():
            for wait in carried.build(c_ins, c_outs, c_sems, False):
                wait()

    out = pl.pallas_call(
        hosted, name=name, grid=grid, in_specs=list(in_specs) + [HBM_SPEC] * c_in,
        out_specs=list(out_specs) + [HBM_SPEC] * c_out, out_shape=list(out_shape) + carried.out_shapes,
        scratch_shapes=list(scratch_shapes) + carried.sem_shapes,
        input_output_aliases={n_in + a: n_out + b for a, b in carried.aliases.items()},
        compiler_params=compiler_params)(*args, *carried.inputs)
    return out[:n_out], out[n_out:]


ROW_CHUNK = 128


EPILOGUE_ROWS = 128


def _chunk_rows(c):
    return pl.ds(pl.multiple_of(c * ROW_CHUNK, ROW_CHUNK), ROW_CHUNK)


def _rstd(xv):
    return lax.rsqrt(jnp.mean(xv * xv, axis=-1, keepdims=True) + NORM_EPS)


def _rms_bwd(xv, g, dn):
    r = _rstd(xv)
    xr = xv * r
    gd = g * dn
    dx = r * (gd - xr * jnp.mean(gd * xr, axis=-1, keepdims=True))
    return dx, jnp.sum(dn * xr, axis=0, keepdims=True)


def _log1p(e):
    u = 1.0 + e
    return jnp.where(u == 1.0, e, jnp.log(u) * (e / (u - 1.0)))


def _one_minus_exp(v, exp_half_v):
    series = 1.0 / 5040.0
    for coeff in (1.0 / 720.0, 1.0 / 120.0, 1.0 / 24.0, 1.0 / 6.0, 0.5, 1.0):
        series = series * v + coeff
    return jnp.where(v > -0.5, -v * series, 1.0 - exp_half_v * exp_half_v)


def _sigmoid(v):
    return 0.5 * jnp.tanh(0.5 * v) + 0.5


def _gelu_parts(g):
    k0 = math.sqrt(2.0 / math.pi)
    g2 = g * g
    t = jnp.tanh(k0 * (g + 0.044715 * g * g2))
    gel = 0.5 * g * (1.0 + t)
    gelp = 0.5 * (1.0 + t) + 0.5 * g * (1.0 - t * t) * (k0 * (1.0 + 3.0 * 0.044715 * g2))
    return gel, gelp


def _norm_proj(x, gain, w_list, out_dtypes, swiglu, name, carried=None):
    T, D = x.shape
    N = w_list[0].shape[0]
    nw = len(w_list)
    bm = _blk(T, 1024, BF16_ROWS)
    bn = _blk(N, 512, LANES)

    def body(*refs):
        x_ref, g_ref = refs[:2]
        w_refs = refs[2:2 + nw]
        n_ref = refs[2 + nw]
        o_refs = refs[3 + nw:3 + 2 * nw]
        act_ref = refs[3 + 2 * nw] if swiglu else None
        n_sc = refs[-1]

        @pl.when(pl.program_id(1) == 0)
        def _():
            def piece(p, _):
                r = pl.ds(pl.multiple_of(p * EPILOGUE_ROWS, EPILOGUE_ROWS), EPILOGUE_ROWS)
                xv = x_ref[r, :]
                nb = (xv * _rstd(xv) * g_ref[...]).astype(BF16)
                n_sc[r, :] = nb
                n_ref[r, :] = nb
                return 0
            lax.fori_loop(0, bm // EPILOGUE_ROWS, piece, 0)

        n = n_sc[...]
        outs = [_dot(n, w_ref[...], NT) for w_ref in w_refs]
        for o_ref, o in zip(o_refs, outs):
            o_ref[...] = o.astype(o_ref.dtype)
        if swiglu:
            hg, hu = outs
            act_ref[...] = (hg * _sigmoid(hg) * hu).astype(BF16)

    row = pl.BlockSpec((bm, D), lambda i, j: (i, 0))
    tile = pl.BlockSpec((bm, bn), lambda i, j: (i, j))
    n_extra = 1 if swiglu else 0
    return _call(
        body, name=name, grid=(T // bm, N // bn),
        in_specs=[row, pl.BlockSpec((1, D), lambda i, j: (0, 0))] + [pl.BlockSpec((bn, D), lambda i, j: (j, 0))] * nw,
        out_specs=[row] + [tile] * (nw + n_extra),
        out_shape=[SDS((T, D), BF16)] + [SDS((T, N), dt) for dt in out_dtypes] + [SDS((T, N), BF16)] * n_extra,
        scratch_shapes=[pltpu.VMEM((bm, D), BF16)],
        compiler_params=_cp(("arbitrary", "arbitrary"), 52),
        args=(x, gain, *w_list), carried=carried)


def _mm_res(a, b, x, scale, name, carried=None):
    T, K = a.shape
    D = b.shape[1]
    bm = _blk(T, 1024, BF16_ROWS)
    bk = _blk(K, 1408, LANES)
    nk = K // bk

    def body(a_ref, b_ref, x_ref, o_ref):
        k = pl.program_id(1)

        @pl.when(k == 0)
        def _():
            o_ref[...] = jnp.zeros_like(o_ref)

        o_ref[...] += _dot(a_ref[...], b_ref[...], NN)

        @pl.when(k == nk - 1)
        def _():
            def chunk(c, _):
                r = _chunk_rows(c)
                o_ref[r, :] = x_ref[r, :] + scale * o_ref[r, :]
                return 0
            lax.fori_loop(0, bm // ROW_CHUNK, chunk, 0)

    row = pl.BlockSpec((bm, D), lambda i, k: (i, 0))
    out = _call(
        body, name=name, grid=(T // bm, nk),
        in_specs=[pl.BlockSpec((bm, bk), lambda i, k: (i, k)), pl.BlockSpec((bk, D), lambda i, k: (k, 0)), row],
        out_specs=[row], out_shape=[SDS((T, D), F32)], scratch_shapes=[],
        compiler_params=_cp(("arbitrary", "arbitrary"), 56),
        args=(a, b, x), carried=carried)
    return out[0] if carried is None else (out[0][0], out[1])


def _mm_nt(a, b, name):
    T, K = a.shape
    N = b.shape[0]
    bm = _blk(T, 1024, BF16_ROWS)
    bn = _blk(N, 512, LANES)

    def body(a_ref, b_ref, o_ref):
        o_ref[...] = _dot(a_ref[...], b_ref[...], NT)

    return pl.pallas_call(
        body, name=name, grid=(T // bm, N // bn),
        in_specs=[pl.BlockSpec((bm, K), lambda i, j: (i, 0)), pl.BlockSpec((bn, K), lambda i, j: (j, 0))],
        out_specs=pl.BlockSpec((bm, bn), lambda i, j: (i, j)), out_shape=SDS((T, N), F32),
        compiler_params=_cp(("arbitrary", "arbitrary"), 40),
    )(a, b)


def _ffn_bwd_act(dfb, wd, hg, hu, name):
    T, D = dfb.shape
    F = wd.shape[0]
    bm = _blk(T, 1024, BF16_ROWS)
    bn = _blk(F, 512, LANES)

    def body(df_ref, wd_ref, hg_ref, hu_ref, dhg_ref, dhu_ref):
        dact = _dot(df_ref[...], wd_ref[...], NT)
        hgv = hg_ref[...].astype(F32)
        huv = hu_ref[...].astype(F32)
        s = _sigmoid(hgv)
        dhu_ref[...] = (dact * (hgv * s)).astype(BF16)
        dhg_ref[...] = (dact * huv * (s * (1.0 + hgv * (1.0 - s)))).astype(BF16)

    tile = pl.BlockSpec((bm, bn), lambda i, j: (i, j))
    return pl.pallas_call(
        body, name=name, grid=(T // bm, F // bn),
        in_specs=[pl.BlockSpec((bm, D), lambda i, j: (i, 0)), pl.BlockSpec((bn, D), lambda i, j: (j, 0)), tile, tile],
        out_specs=[tile, tile], out_shape=[SDS((T, F), BF16)] * 2,
        compiler_params=_cp(("arbitrary", "arbitrary"), 40),
    )(dfb, wd, hg, hu)


def _dw_tn(a, b, bm_pref, name):
    T, M = a.shape
    N = b.shape[1]
    bm = _blk(M, bm_pref, LANES)
    tk = _blk(T, 1024, BF16_ROWS)
    nk = T // tk

    def body(a_ref, b_ref, o_ref, acc):
        k = pl.program_id(1)

        @pl.when(k == 0)
        def _():
            acc[...] = jnp.zeros_like(acc)

        acc[...] += _dot(a_ref[...], b_ref[...], TN)

        @pl.when(k == nk - 1)
        def _():
            o_ref[...] = acc[...].astype(BF16)

    return pl.pallas_call(
        body, name=name, grid=(M // bm, nk),
        in_specs=[pl.BlockSpec((tk, bm), lambda i, k: (k, i)), pl.BlockSpec((tk, N), lambda i, k: (k, 0))],
        out_specs=pl.BlockSpec((bm, N), lambda i, k: (i, 0)), out_shape=SDS((M, N), BF16),
        scratch_shapes=[pltpu.VMEM((bm, N), F32)],
        compiler_params=_cp(("arbitrary", "arbitrary"), 48),
    )(a, b)


def _mm_rmsbwd(pairs, x, gain, dx_in, bscale, name, carried=None):
    T, D = x.shape
    K = pairs[0][0].shape[1]
    npair = len(pairs)
    bm = _blk(T, 1024, BF16_ROWS)
    bk = _blk(K, 1024 // npair, LANES)
    nk = K // bk

    nchunk = bm // ROW_CHUNK

    def body(*refs):
        ab = refs[:2 * npair]
        x_hbm, g_ref, dxin_hbm, dx_ref, dxb_ref, dg_ref, x_buf, dxin_buf, sems = refs[2 * npair:]
        i = pl.program_id(0)
        k = pl.program_id(1)

        def fetch(c, slot):
            rows = pl.ds(i * bm + c * ROW_CHUNK, ROW_CHUNK)
            return (pltpu.make_async_copy(x_hbm.at[rows, :], x_buf.at[slot], sems.at[slot, 0]),
                    pltpu.make_async_copy(dxin_hbm.at[rows, :], dxin_buf.at[slot], sems.at[slot, 1]))

        @pl.when(k == 0)
        def _():
            dx_ref[...] = jnp.zeros_like(dx_ref)

        @pl.when(k == nk - 1)
        def _():
            for cp in fetch(0, 0):
                cp.start()

        for q in range(npair):
            dx_ref[...] += _dot(ab[2 * q][...], ab[2 * q + 1][...], NN)

        @pl.when(k == nk - 1)
        def _():
            @pl.when(i == 0)
            def _():
                dg_ref[...] = jnp.zeros_like(dg_ref)

            def chunk(c, _):
                slot = c % 2

                @pl.when(c + 1 < nchunk)
                def _():
                    for cp in fetch(c + 1, 1 - slot):
                        cp.start()

                for cp in fetch(c, slot):
                    cp.wait()

                def piece(p, _):
                    rb = pl.ds(pl.multiple_of(p * EPILOGUE_ROWS, EPILOGUE_ROWS), EPILOGUE_ROWS)
                    r = pl.ds(pl.multiple_of(c * ROW_CHUNK + p * EPILOGUE_ROWS, EPILOGUE_ROWS), EPILOGUE_ROWS)
                    dx, dg = _rms_bwd(x_buf[slot, rb, :], g_ref[...], dx_ref[r, :])
                    dxo = dxin_buf[slot, rb, :] + dx
                    dx_ref[r, :] = dxo
                    dxb_ref[r, :] = (bscale * dxo).astype(BF16)
                    dg_ref[...] += dg
                    return 0
                lax.fori_loop(0, ROW_CHUNK // EPILOGUE_ROWS, piece, 0)
                return 0
            lax.fori_loop(0, nchunk, chunk, 0)

    row = pl.BlockSpec((bm, D), lambda i, k: (i, 0))
    anywhere = pl.BlockSpec(memory_space=pl.ANY)
    vec = pl.BlockSpec((1, D), lambda i, k: (0, 0))
    in_specs = []
    args = []
    for a, b in pairs:
        in_specs += [pl.BlockSpec((bm, bk), lambda i, k: (i, k)), pl.BlockSpec((bk, D), lambda i, k: (k, 0))]
        args += [a, b]
    return _call(
        body, name=name, grid=(T // bm, nk),
        in_specs=in_specs + [anywhere, vec, anywhere], out_specs=[row, row, vec],
        out_shape=[SDS((T, D), F32), SDS((T, D), BF16), SDS((1, D), F32)],
        scratch_shapes=[pltpu.VMEM((2, ROW_CHUNK, D), F32), pltpu.VMEM((2, ROW_CHUNK, D), F32),
                        pltpu.SemaphoreType.DMA((2, 2))],
        compiler_params=_cp(("arbitrary", "arbitrary"), 52),
        args=(*args, x, gain, dx_in), carried=carried)


def _loss_head(x3, gain, tgt, name):
    T, D = x3.shape
    bm = _blk(T, 256, BF16_ROWS)

    def body(x_ref, g_ref, t_ref, dx_ref, dxb_ref, dg_ref, loss_ref):
        i = pl.program_id(0)
        xv = x_ref[...]
        g = g_ref[...]
        out = xv * _rstd(xv) * g
        e = out - t_ref[...]
        part = 0.5 * jnp.sum(jnp.mean(e * e, axis=-1, keepdims=True), axis=0, keepdims=True)
        dx, dg = _rms_bwd(xv, g, e * (1.0 / D))
        dx_ref[...] = dx
        dxb_ref[...] = (0.5 * dx).astype(BF16)

        @pl.when(i == 0)
        def _():
            dg_ref[...] = dg
            loss_ref[...] = jnp.broadcast_to(part, loss_ref.shape)

        @pl.when(i > 0)
        def _():
            dg_ref[...] += dg
            loss_ref[...] += jnp.broadcast_to(part, loss_ref.shape)

    row = pl.BlockSpec((bm, D), lambda i: (i, 0))
    vec = pl.BlockSpec((1, D), lambda i: (0, 0))
    return pl.pallas_call(
        body, name=name, grid=(T // bm,),
        in_specs=[row, vec, row], out_specs=[row, row, vec, pl.BlockSpec((SUBLANES, LANES), lambda i: (0, 0))],
        out_shape=[SDS((T, D), F32), SDS((T, D), BF16), SDS((1, D), F32), SDS((SUBLANES, LANES), F32)],
        compiler_params=_cp(("arbitrary",), 40),
    )(x3, gain, tgt)


R_CW, R_CB, R_BA, R_BI, R_LAM, R_SW, R_GLO, R_GSO, SMALL_ROWS = 0, 4, 5, 6, 7, 8, 11, 12, 16


def _rows(g):
    return pl.ds(pl.multiple_of(g * SUBLANES, SUBLANES), SUBLANES)


def _shift_back(prev, cur, d):
    row = lax.broadcasted_iota(jnp.int32, cur.shape, 0)
    return pltpu.roll(jnp.where(row >= SUBLANES - d, prev, cur), d, 0)


def _shift_fwd(cur, nxt, d):
    row = lax.broadcasted_iota(jnp.int32, cur.shape, 0)
    return pltpu.roll(jnp.where(row < d, nxt, cur), SUBLANES - d, 0)


def _causal_conv(ext, g, taps_ref, ntap):
    prev = ext[_rows(g), :]
    cur = ext[_rows(g + 1), :]
    out = _shift_back(prev, cur, ntap - 1) * taps_ref[0:1, :]
    for k in range(1, ntap - 1):
        out = out + _shift_back(prev, cur, ntap - 1 - k) * taps_ref[k:k + 1, :]
    return out + cur * taps_ref[ntap - 1:ntap, :]


def _scan8(A, U, reverse):
    row = lax.broadcasted_iota(jnp.int32, A.shape, 0)
    for s in (1, 2, 4):
        if reverse:
            A_sh = pltpu.roll(A, SUBLANES - s, 0)
            U_sh = pltpu.roll(U, SUBLANES - s, 0)
            m = row < SUBLANES - s
        else:
            A_sh = pltpu.roll(A, s, 0)
            U_sh = pltpu.roll(U, s, 0)
            m = row >= s
        U = jnp.where(m, A * U_sh + U, U)
        A = jnp.where(m, A * A_sh, A)
    return A, U


def _gate_pre(xc_s, w_ref, out_s, H, hd):
    for h in range(H):
        cs = slice(h * hd, (h + 1) * hd)
        out_s[:, cs] = _dot(xc_s[:, cs].astype(BF16), w_ref[h], NN)


def _lru_coeffs(pa, pi, xc, ba, bi, sp):
    ra = _sigmoid(pa + ba)
    ri = _sigmoid(pi + bi)
    log_a = (-LRU_C * ra) * sp
    a = jnp.exp(log_a)
    mult = jnp.sqrt(_one_minus_exp(2.0 * log_a, a))
    return ra, ri, a, mult


def _softplus_neg(lam):
    v = -lam
    return jnp.maximum(v, 0.0) + _log1p(jnp.exp(-jnp.abs(v)))


def _mix_fwd(z, cw, cb, wa, ba, wi, bi, lam, sw, glo, gso, name):
    T = z.shape[0]
    C = z.shape[1] // 5
    H = wa.shape[0]
    hd = C // H
    tb = _blk(T, 256, BF16_ROWS)
    ng = tb // SUBLANES
    HDR = SUBLANES

    def body(z_ref, cw_ref, cb_ref, wa_ref, ba_ref, wi_ref, bi_ref, lam_ref, sw_ref, glo_ref, gso_ref,
             y_ref, h_ref, xext, pext, xc_s, pa_s, pi_s, y_s, hcar):
        @pl.when(pl.program_id(0) == 0)
        def _():
            xext[0:HDR, :] = jnp.zeros((HDR, C), F32)
            pext[0:HDR, :] = jnp.zeros((HDR, C), F32)
            hcar[...] = jnp.zeros_like(hcar)

        def fill(g, _):
            r = _rows(g)
            re = _rows(g + 1)
            xext[re, :] = z_ref[r, 0:C]
            pext[re, :] = z_ref[r, 3 * C:4 * C] * z_ref[r, 4 * C:5 * C]
            return 0
        lax.fori_loop(0, ng, fill, 0)

        def conv(g, _):
            xc_s[_rows(g), :] = _causal_conv(xext, g, cw_ref, 4) + cb_ref[...]
            return 0
        lax.fori_loop(0, ng, conv, 0)

        _gate_pre(xc_s, wa_ref, pa_s, H, hd)
        _gate_pre(xc_s, wi_ref, pi_s, H, hd)
        sp = _softplus_neg(lam_ref[...])

        def group(g, hprev):
            r = _rows(g)
            xc = xc_s[r, :]
            _, ri, a, mult = _lru_coeffs(pa_s[r, :], pi_s[r, :], xc, ba_ref[...], bi_ref[...], sp)
            A, U = _scan8(a, mult * (ri * xc), reverse=False)
            hh = A * hprev + U
            h_ref[r, :] = hh
            gel, _ = _gelu_parts(z_ref[r, C:2 * C])
            y_lru = hh * gel
            y_s[r, 0:C] = y_lru * _rstd(y_lru) * glo_ref[...]
            y_sc = z_ref[r, 2 * C:3 * C] * _causal_conv(pext, g, sw_ref, 3)
            y_s[r, C:2 * C] = y_sc * _rstd(y_sc) * gso_ref[...]
            return jnp.broadcast_to(hh[SUBLANES - 1:SUBLANES, :], hh.shape)
        hcar[...] = lax.fori_loop(0, ng, group, hcar[...], unroll=2)

        xext[0:HDR, :] = xext[tb:tb + HDR, :]
        pext[0:HDR, :] = pext[tb:tb + HDR, :]

        def cast(g, _):
            r = pl.ds(pl.multiple_of(g * BF16_ROWS, BF16_ROWS), BF16_ROWS)
            y_ref[r, :] = y_s[r, :].astype(BF16)
            return 0
        lax.fori_loop(0, tb // BF16_ROWS, cast, 0)

    full = lambda shape: pl.BlockSpec(shape, lambda i: (0,) * len(shape))
    blk = lambda w: pl.BlockSpec((tb, w), lambda i: (i, 0))
    ext = pltpu.VMEM((tb + HDR, C), F32)
    tile = pltpu.VMEM((tb, C), F32)
    return pl.pallas_call(
        body, name=name, grid=(T // tb,),
        in_specs=[blk(5 * C), full((4, C)), full((1, C)), full((H, hd, hd)), full((1, C)), full((H, hd, hd)),
                  full((1, C)), full((1, C)), full((3, C)), full((1, C)), full((1, C))],
        out_specs=[blk(2 * C), blk(C)],
        out_shape=[SDS((T, 2 * C), BF16), SDS((T, C), F32)],
        scratch_shapes=[ext, ext, tile, tile, tile, pltpu.VMEM((tb, 2 * C), F32), pltpu.VMEM((SUBLANES, C), F32)],
        compiler_params=_cp(("arbitrary",), 40),
    )(z, cw, cb, wa, ba, wi, bi, lam, sw, glo, gso)


def _mix_bwd(z, h, dy, cw, cb, wa, ba, wi, bi, lam, sw, glo, gso, name):
    T = z.shape[0]
    C = z.shape[1] // 5
    H = wa.shape[0]
    hd = C // H
    tb = _blk(T, 256, BF16_ROWS)
    nb = T // tb
    ng = tb // SUBLANES
    HDR = SUBLANES
    N_ACC = 13

    def body(z_ref, zp_ref, h_ref, hp_ref, dy_ref, cw_ref, cb_ref, wa_ref, ba_ref, wi_ref, bi_ref, lam_ref,
             sw_ref, glo_ref, gso_ref, dz_ref, small_ref, dwa_ref, dwi_ref,
             xext, pext, hext, dqext, dxcext, bext, xc_s, pa_s, pi_s, a_s, m_s, ri_s, dh_s, dpa_s, dpi_s,
             dz_s, acc_s, bcar):
        i = pl.program_id(0)
        first_rows = i == nb - 1

        @pl.when(i == 0)
        def _():
            dqext[tb:tb + HDR, :] = jnp.zeros((HDR, C), F32)
            dxcext[tb:tb + HDR, :] = jnp.zeros((HDR, C), F32)
            bcar[...] = jnp.zeros_like(bcar)
            acc_s[...] = jnp.zeros_like(acc_s)
            dwa_ref[...] = jnp.zeros_like(dwa_ref)
            dwi_ref[...] = jnp.zeros_like(dwi_ref)

        zero = jnp.zeros((HDR, C), F32)
        xext[0:HDR, :] = jnp.where(first_rows, zero, zp_ref[:, 0:C])
        pext[0:HDR, :] = jnp.where(first_rows, zero, zp_ref[:, 3 * C:4 * C] * zp_ref[:, 4 * C:5 * C])
        hext[0:HDR, :] = jnp.where(first_rows, zero, hp_ref[...])

        def fill(g, _):
            r = _rows(g)
            re = _rows(g + 1)
            xext[re, :] = z_ref[r, 0:C]
            pext[re, :] = z_ref[r, 3 * C:4 * C] * z_ref[r, 4 * C:5 * C]
            hext[re, :] = h_ref[r, :]
            return 0
        lax.fori_loop(0, ng, fill, 0)

        def conv(g, _):
            xc_s[_rows(g), :] = _causal_conv(xext, g, cw_ref, 4) + cb_ref[...]
            return 0
        lax.fori_loop(0, ng, conv, 0)

        _gate_pre(xc_s, wa_ref, pa_s, H, hd)
        _gate_pre(xc_s, wi_ref, pi_s, H, hd)
        sp = _softplus_neg(lam_ref[...])
        dsp_dlam = -jax.nn.sigmoid(-lam_ref[...])

        def add_acc(k, v):
            acc_s[k] += v

        def p1(g, _):
            r = _rows(g)
            xc = xc_s[r, :]
            _, ri, a, mult = _lru_coeffs(pa_s[r, :], pi_s[r, :], xc, ba_ref[...], bi_ref[...], sp)
            a_s[r, :] = a
            m_s[r, :] = mult
            ri_s[r, :] = ri
            hh = h_ref[r, :]
            gel, gelp = _gelu_parts(z_ref[r, C:2 * C])
            y_lru = hh * gel
            dnl = dy_ref[r, 0:C]
            rl = _rstd(y_lru)
            ylr = y_lru * rl
            gd = glo_ref[...] * dnl
            dy_lru = rl * (gd - ylr * jnp.mean(gd * ylr, axis=-1, keepdims=True))
            add_acc(R_GLO, dnl * ylr)
            dz_s[r, C:2 * C] = dy_lru * hh * gelp
            dh = dy_lru * gel
            dh_s[r, :] = dh

            q = _causal_conv(pext, g, sw_ref, 3)
            scb = z_ref[r, 2 * C:3 * C]
            y_sc = scb * q
            dns = dy_ref[r, C:2 * C]
            rs = _rstd(y_sc)
            ysr = y_sc * rs
            gs = gso_ref[...] * dns
            dy_sc = rs * (gs - ysr * jnp.mean(gs * ysr, axis=-1, keepdims=True))
            add_acc(R_GSO, dns * ysr)
            dz_s[r, 2 * C:3 * C] = dy_sc * q
            dqext[r, :] = dy_sc * scb
            return 0
        lax.fori_loop(0, ng, p1, 0, unroll=2)

        bext[tb:tb + HDR, :] = bcar[...]

        def p2(j, carry):
            g = ng - 1 - j
            r = _rows(g)
            a = a_s[r, :]
            A, U = _scan8(a, a * dh_s[r, :], reverse=True)
            bb = A * carry + U
            bext[r, :] = bb
            return jnp.broadcast_to(bb[0:1, :], bb.shape)
        bcar[...] = lax.fori_loop(0, ng, p2, bcar[...])

        def p3(g, _):
            r = _rows(g)
            rn = _rows(g + 1)
            G = dh_s[r, :] + _shift_fwd(bext[r, :], bext[rn, :], 1)
            hm1 = _shift_back(hext[r, :], hext[rn, :], 1)
            a = a_s[r, :]
            mult = m_s[r, :]
            ri = ri_s[r, :]
            xc = xc_s[r, :]
            ra = _sigmoid(pa_s[r, :] + ba_ref[...])
            dxcext[r, :] = G * mult * ri
            dri = G * mult * xc
            dmult = G * ri * xc
            dlog_a = (G * hm1) * a - dmult * (a * a) / mult
            add_acc(R_LAM, dlog_a * (-LRU_C * ra) * dsp_dlam)
            dpa = dlog_a * (-LRU_C * sp) * ra * (1.0 - ra)
            dpi = dri * ri * (1.0 - ri)
            add_acc(R_BA, dpa)
            add_acc(R_BI, dpi)
            dpa_s[r, :] = dpa
            dpi_s[r, :] = dpi
            return 0
        lax.fori_loop(0, ng, p3, 0)

        for hh_ in range(H):
            cs = slice(hh_ * hd, (hh_ + 1) * hd)
            dpa_b = dpa_s[:, cs].astype(BF16)
            dpi_b = dpi_s[:, cs].astype(BF16)
            xc_b = xc_s[:, cs].astype(BF16)
            dxcext[0:tb, cs] += _dot(dpa_b, wa_ref[hh_], NT) + _dot(dpi_b, wi_ref[hh_], NT)
            dwa_ref[hh_] += _dot(xc_b, dpa_b, TN)
            dwi_ref[hh_] += _dot(xc_b, dpi_b, TN)

        def p4(g, _):
            r = _rows(g)
            rn = _rows(g + 1)
            dxc = dxcext[r, :]
            dxc_n = dxcext[rn, :]
            x_p = xext[r, :]
            x_c = xext[rn, :]
            add_acc(R_CB, dxc)
            dlx = dxc * cw_ref[3:4, :]
            add_acc(R_CW + 3, dxc * x_c)
            for d in range(1, 4):
                dlx = dlx + _shift_fwd(dxc, dxc_n, d) * cw_ref[3 - d:4 - d, :]
                add_acc(R_CW + 3 - d, dxc * _shift_back(x_p, x_c, d))
            dz_s[r, 0:C] = dlx

            dq = dqext[r, :]
            dq_n = dqext[rn, :]
            p_p = pext[r, :]
            p_c = pext[rn, :]
            dp = dq * sw_ref[2:3, :]
            add_acc(R_SW + 2, dq * p_c)
            for d in range(1, 3):
                dp = dp + _shift_fwd(dq, dq_n, d) * sw_ref[2 - d:3 - d, :]
                add_acc(R_SW + 2 - d, dq * _shift_back(p_p, p_c, d))
            dz_s[r, 3 * C:4 * C] = dp * z_ref[r, 4 * C:5 * C]
            dz_s[r, 4 * C:5 * C] = dp * z_ref[r, 3 * C:4 * C]
            return 0
        lax.fori_loop(0, ng, p4, 0)

        dqext[tb:tb + HDR, :] = dqext[0:HDR, :]
        dxcext[tb:tb + HDR, :] = dxcext[0:HDR, :]

        def cast(g, _):
            r = pl.ds(pl.multiple_of(g * BF16_ROWS, BF16_ROWS), BF16_ROWS)
            dz_ref[r, :] = dz_s[r, :].astype(BF16)
            return 0
        lax.fori_loop(0, tb // BF16_ROWS, cast, 0)

        @pl.when(i == nb - 1)
        def _():
            small_ref[...] = jnp.zeros_like(small_ref)
            for k in range(N_ACC):
                small_ref[k:k + 1, :] = jnp.sum(acc_s[k], axis=0, keepdims=True)

    tpg = tb // SUBLANES
    full = lambda shape: pl.BlockSpec(shape, lambda i: (0,) * len(shape))
    blk = lambda w: pl.BlockSpec((tb, w), lambda i: (nb - 1 - i, 0))
    prev = lambda w: pl.BlockSpec((SUBLANES, w), lambda i: (jnp.maximum((nb - 1 - i) * tpg - 1, 0), 0))
    ext = pltpu.VMEM((tb + HDR, C), F32)
    tile = pltpu.VMEM((tb, C), F32)
    return pl.pallas_call(
        body, name=name, grid=(nb,),
        in_specs=[blk(5 * C), prev(5 * C), blk(C), prev(C), blk(2 * C), full((4, C)), full((1, C)), full((H, hd, hd)),
                  full((1, C)), full((H, hd, hd)), full((1, C)), full((1, C)), full((3, C)), full((1, C)), full((1, C))],
        out_specs=[blk(5 * C), full((SMALL_ROWS, C)), full((H, hd, hd)), full((H, hd, hd))],
        out_shape=[SDS((T, 5 * C), BF16), SDS((SMALL_ROWS, C), F32), SDS((H, hd, hd), F32), SDS((H, hd, hd), F32)],
        scratch_shapes=[ext] * 6 + [tile] * 9 + [pltpu.VMEM((tb, 5 * C), F32), pltpu.VMEM((N_ACC, SUBLANES, C), F32),
                                                pltpu.VMEM((SUBLANES, C), F32)],
        compiler_params=_cp(("arbitrary",), 56),
    )(z, z, h, h, dy, cw, cb, wa, ba, wi, bi, lam, sw, glo, gso)


def _add_slabs(terms, out_dtype, name):
    R, Ccols = terms[0].shape
    br = _blk(R, 512, BF16_ROWS)
    n = len(terms)

    def body(*refs):
        s = refs[0][...].astype(F32)
        for t_ref in refs[1:n]:
            s = s + t_ref[...].astype(F32)
        refs[n][...] = s.astype(out_dtype)

    spec = pl.BlockSpec((br, Ccols), lambda i: (i, 0))
    return pl.pallas_call(
        body, name=name, grid=(R // br,), in_specs=[spec] * n, out_specs=spec, out_shape=SDS((R, Ccols), out_dtype),
        compiler_params=_cp(("arbitrary",), 40),
    )(*terms)


def _final_grad(sb, lb, chip, name):
    _, R, Ccols = sb.shape
    br = _blk(R, 512, BF16_ROWS)

    def body(chip_ref, sb_ref, l0, l1, l2, o_ref):
        s = sb_ref[0].astype(F32)
        for t_ref in (l0, l1, l2):
            s = s + t_ref[0].astype(F32)
        o_ref[...] = s

    lspec = lambda k: pl.BlockSpec((1, br, Ccols), lambda i, c: (k, i, 0))
    return pl.pallas_call(
        body, name=name,
        grid_spec=pltpu.PrefetchScalarGridSpec(
            num_scalar_prefetch=1, grid=(R // br,),
            in_specs=[pl.BlockSpec((1, br, Ccols), lambda i, c: (c[0], i, 0)), lspec(0), lspec(1), lspec(2)],
            out_specs=pl.BlockSpec((br, Ccols), lambda i, c: (i, 0))),
        out_shape=SDS((R, Ccols), F32),
        compiler_params=_cp(("arbitrary",), 40),
    )(chip, sb, lb, lb, lb)


def _adamw(w, g, m, v, name):
    R, Ccols = w.shape
    br = _blk(R, 256, SUBLANES)
    c1 = 1.0 - ADAM_B1 ** ADAM_STEP
    c2 = 1.0 - ADAM_B2 ** ADAM_STEP

    def body(w_ref, g_ref, m_ref, v_ref, d_ref, nm_ref, nv_ref):
        gv = g_ref[...]
        nm = ADAM_B1 * m_ref[...] + (1.0 - ADAM_B1) * gv
        nv = ADAM_B2 * v_ref[...] + (1.0 - ADAM_B2) * (gv * gv)
        nm_ref[...] = nm
        nv_ref[...] = nv
        d_ref[...] = -ADAM_LR * ((nm / c1) / (jnp.sqrt(nv / c2) + ADAM_EPS) + ADAM_WD * w_ref[...])

    spec = pl.BlockSpec((br, Ccols), lambda i: (i, 0))
    return pl.pallas_call(
        body, name=name, grid=(R // br,), in_specs=[spec] * 4, out_specs=[spec] * 3,
        out_shape=[SDS((R, Ccols), F32)] * 3, compiler_params=_cp(("arbitrary",), 40),
    )(w, g, m, v)


def _place():
    return lax.axis_index("x"), lax.axis_index("y"), lax.axis_index("c")


def _dev_rows(ref, dev, rows):
    return ref.at[pl.ds((4 * dev[0] + 2 * dev[1] + dev[2]) * rows, rows), :]


def _remote(src, dst, send_sem, recv_sem, to):
    return pltpu.make_async_remote_copy(src_ref=src, dst_ref=dst, send_sem=send_sem, recv_sem=recv_sem,
                                        device_id=to, device_id_type=MESH)


SAME_CORE_AND_SIBLING = ((0, 0, 1), (1, 0, 0), (0, 1, 0), (1, 1, 0))
ALL_OTHERS = SAME_CORE_AND_SIBLING + ((1, 0, 1), (0, 1, 1), (1, 1, 1))


def _merge_phases(a, b):
    na_in, na_out, na_sem = len(a.inputs), len(a.out_shapes), len(a.sem_shapes)

    def build(ins, outs, sems, starting):
        return (a.build(ins[:na_in], outs[:na_out], sems[:na_sem], starting)
                + b.build(ins[na_in:], outs[na_out:], sems[na_sem:], starting))

    aliases = dict(a.aliases)
    aliases.update({na_in + i: na_out + o for i, o in b.aliases.items()})
    return _Carried(a.inputs + b.inputs, a.out_shapes + b.out_shapes, aliases, a.sem_shapes + b.sem_shapes, build)


def _ag_direct_phase(slab, pieces, flips):
    W = slab.shape[1]
    n = len(pieces)
    npeer = len(flips)

    def build(ins, outs, sems, starting):
        (slab_ref,) = ins
        send_sems, recv_sems, local_sems = sems
        x, y, c = _place()
        me = (x, y, c)
        peers = [tuple(1 - v if f else v for v, f in zip(me, flip)) for flip in flips]
        todo = []
        for p, (off, rows) in enumerate(pieces):
            src = slab_ref.at[pl.ds(off, rows), :]
            mine = pltpu.make_async_copy(src, _dev_rows(outs[p], me, rows), local_sems.at[p])
            todo.append(mine.start if starting else mine.wait)
            for k, peer in enumerate(peers):
                snd = _remote(src, _dev_rows(outs[p], me, rows), send_sems.at[k, p], recv_sems.at[k, p], peer)
                if starting:
                    todo.append(snd.start)
                else:
                    theirs = _dev_rows(outs[p], peer, rows)
                    rcv = _remote(theirs, theirs, send_sems.at[k, p], recv_sems.at[k, p], me)
                    todo += [rcv.wait_recv, snd.wait_send]
        return todo

    dma = pltpu.SemaphoreType.DMA
    return _Carried([slab], [SDS((N_DEV * rows, W), slab.dtype) for _, rows in pieces], {},
                    [dma((npeer, n)), dma((npeer, n)), dma((n,))], build)


def _ag_forward_phase(gathered, pieces):
    n = len(pieces)

    def build(ins, outs, sems, starting):
        send_sems, recv_sems = sems
        x, y, c = _place()
        me, sibling = (x, y, c), (x, y, 1 - c)
        chips = [(1 - x, y), (x, 1 - y), (1 - x, 1 - y)]
        todo = []
        for p, (_, rows) in enumerate(pieces):
            for j, chip in enumerate(chips):
                snd = _remote(_dev_rows(ins[p], (*chip, c), rows), _dev_rows(outs[p], (*chip, c), rows),
                              send_sems.at[j, p], recv_sems.at[j, p], sibling)
                if starting:
                    todo.append(snd.start)
                else:
                    theirs = _dev_rows(outs[p], (*chip, 1 - c), rows)
                    rcv = _remote(theirs, theirs, send_sems.at[j, p], recv_sems.at[j, p], me)
                    todo += [rcv.wait_recv, snd.wait_send]
        return todo

    dma = pltpu.SemaphoreType.DMA
    return _Carried(gathered, [SDS(g.shape, g.dtype) for g in gathered], {p: p for p in range(n)},
                    [dma((3, n)), dma((3, n))], build)


def _rs_chips_phase(sb):
    _, R, W = sb.shape

    def build(ins, outs, sems, starting):
        (sb_ref,), (land_ref,) = ins, outs
        send_sems, recv_sems = sems
        x, y, c = _place()
        chips = [(1 - x, y), (x, 1 - y), (1 - x, 1 - y)]
        cps = [_remote(sb_ref.at[2 * chip[0] + chip[1]], land_ref.at[j], send_sems.at[j], recv_sems.at[j], (*chip, c))
               for j, chip in enumerate(chips)]
        if starting:
            return [cp.start for cp in cps]
        return [cp.wait_recv for cp in cps] + [cp.wait_send for cp in cps]

    dma = pltpu.SemaphoreType.DMA
    return _Carried([sb], [SDS((3, R, W), sb.dtype)], {}, [dma((3,)), dma((3,))], build)


def _allgather(slab, pieces, name):
    R, W = slab.shape
    n = len(pieces)
    assert sum(rows for _, rows in pieces) == R

    def body(slab_ref, *refs):
        outs = refs[:n]
        send_sems, recv_sems, local_sems = refs[n:]
        x, y, c = _place()
        me, sibling = (x, y, c), (x, y, 1 - c)
        chips = [(1 - x, y), (x, 1 - y), (1 - x, 1 - y)]

        def dst_rows(p, origin):
            rows = pieces[p][1]
            start = (4 * origin[0] + 2 * origin[1] + origin[2]) * rows
            return outs[p].at[pl.ds(start, rows), :]

        def copies(k, origin, to, from_slab):
            out = []
            for p, (off, rows) in enumerate(pieces):
                dst = dst_rows(p, origin)
                src = slab_ref.at[pl.ds(off, rows), :] if from_slab else dst
                out.append(pltpu.make_async_remote_copy(
                    src_ref=src, dst_ref=dst, send_sem=send_sems.at[k, p], recv_sem=recv_sems.at[k, p],
                    device_id=to, device_id_type=MESH))
            return out

        mine = [pltpu.make_async_copy(slab_ref.at[pl.ds(off, rows), :], dst_rows(p, me), local_sems.at[p])
                for p, (off, rows) in enumerate(pieces)]
        for cp in mine:
            cp.start()
        first = copies(0, me, sibling, True)
        for j, chip in enumerate(chips):
            first += copies(1 + j, me, (*chip, c), True)
        for cp in first:
            cp.start()
        passed = []
        for j, chip in enumerate(chips):
            for cp in copies(1 + j, (*chip, c), me, False):
                cp.wait_recv()
            fwd = copies(4 + j, (*chip, c), sibling, False)
            for cp in fwd:
                cp.start()
            passed += fwd
        for cp in copies(0, sibling, me, False):
            cp.wait_recv()
        for j, chip in enumerate(chips):
            for cp in copies(4 + j, (*chip, 1 - c), me, False):
                cp.wait_recv()
        for cp in first + passed:
            cp.wait_send()
        for cp in mine:
            cp.wait()

    return pl.pallas_call(
        body, name=name,
        in_specs=[HBM_SPEC], out_specs=[HBM_SPEC] * n,
        out_shape=[SDS((N_DEV * rows, W), slab.dtype) for _, rows in pieces],
        scratch_shapes=[pltpu.SemaphoreType.DMA((7, n)), pltpu.SemaphoreType.DMA((7, n)), pltpu.SemaphoreType.DMA((n,))],
    )(slab)


def _rs_sibling(grads, pieces, name):
    W = grads[0].shape[1]
    R = sum(rows for _, rows in pieces)
    n = len(pieces)
    dt = grads[0].dtype
    max_rows = max(rows for _, rows in pieces)
    steps = [(q, p) for q in range(N_CHIP) for p in range(n)]
    ns = len(steps)
    ADD_ROWS = 64
    assert all(rows % ADD_ROWS == 0 for _, rows in pieces)

    def body(*refs):
        g_refs = refs[:n]
        sb_ref, mine_buf, send_buf, land_buf, out_buf, in_sems, out_sems, send_sems, recv_sems, credit = refs[n:]
        x, y, c = _place()
        sibling = (x, y, 1 - c)

        def loads(s):
            q, p = steps[s]
            rows = pieces[p][1]
            slot = s % 2
            mine = g_refs[p].at[pl.ds((2 * q + c) * rows, rows), :]
            theirs = g_refs[p].at[pl.ds((2 * q + 1 - c) * rows, rows), :]
            return (pltpu.make_async_copy(mine, mine_buf.at[slot, pl.ds(0, rows), :], in_sems.at[slot, 0]),
                    pltpu.make_async_copy(theirs, send_buf.at[slot, pl.ds(0, rows), :], in_sems.at[slot, 1]))

        def send(s):
            rows = pieces[steps[s][1]][1]
            slot = s % 2
            return pltpu.make_async_remote_copy(
                src_ref=send_buf.at[slot, pl.ds(0, rows), :], dst_ref=land_buf.at[slot, pl.ds(0, rows), :],
                send_sem=send_sems.at[slot], recv_sem=recv_sems.at[slot], device_id=sibling, device_id_type=MESH)

        def store(s):
            q, p = steps[s]
            off, rows = pieces[p]
            slot = s % 2
            return pltpu.make_async_copy(out_buf.at[slot, pl.ds(0, rows), :], sb_ref.at[q, pl.ds(off, rows), :],
                                         out_sems.at[slot])

        for cp in loads(0):
            cp.start()
        for s in range(ns):
            slot = s % 2
            rows = pieces[steps[s][1]][1]
            if s + 1 < ns:
                if s >= 1:
                    send(s - 1).wait_send()
                for cp in loads(s + 1):
                    cp.start()
            for cp in loads(s):
                cp.wait()
            if s >= 2:
                pl.semaphore_wait(credit.at[slot], 1)
            send(s).start()
            send(s).wait_recv()
            if s >= 2:
                store(s - 2).wait()

            def add(k, _, slot=slot):
                r = pl.ds(pl.multiple_of(k * ADD_ROWS, ADD_ROWS), ADD_ROWS)
                out_buf[slot, r, :] = (mine_buf[slot, r, :].astype(F32) + land_buf[slot, r, :].astype(F32)).astype(dt)
                return 0
            lax.fori_loop(0, rows // ADD_ROWS, add, 0)
            if s + 2 < ns:
                pl.semaphore_signal(credit.at[slot], inc=1, device_id=sibling, device_id_type=MESH)
            store(s).start()
        for s in range(max(ns - 2, 0), ns):
            send(s).wait_send()
            store(s).wait()

    buf = pltpu.VMEM((2, max_rows, W), dt)
    return pl.pallas_call(
        body, name=name,
        in_specs=[HBM_SPEC] * n, out_specs=HBM_SPEC,
        out_shape=SDS((N_CHIP, R, W), dt),
        scratch_shapes=[buf, buf, buf, buf, pltpu.SemaphoreType.DMA((2, 2)), pltpu.SemaphoreType.DMA((2,)),
                        pltpu.SemaphoreType.DMA((2,)), pltpu.SemaphoreType.DMA((2,)), pltpu.SemaphoreType.REGULAR((2,))],
        compiler_params=pltpu.CompilerParams(vmem_limit_bytes=40 * MIB),
    )(*grads)


SMALL_NAMES = ("ffn1_norm", "mix_norm", "ffn2_norm", "final_norm", "lru_conv_w", "lru_conv_b", "lru_w_a", "lru_b_a",
               "lru_w_i", "lru_b_i", "lru_lambda", "sc_conv_w", "lru_out_norm", "sc_out_norm")
WEIGHT_NAMES = ("ffn1_norm", "ffn1_w_gate", "ffn1_w_up", "ffn1_w_down", "mix_norm", "w_in", "lru_conv_w", "lru_conv_b",
                "lru_w_a", "lru_b_a", "lru_w_i", "lru_b_i", "lru_lambda", "sc_conv_w", "lru_out_norm", "sc_out_norm",
                "w_out", "ffn2_norm", "ffn2_w_gate", "ffn2_w_up", "ffn2_w_down", "final_norm")
BIG = (("ffn1_w_gate", True), ("ffn1_w_up", True), ("ffn1_w_down", False), ("ffn2_w_gate", True), ("ffn2_w_up", True),
       ("ffn2_w_down", False), ("w_in", True), ("w_out", False))


SLAB_ROW_ALIGN = 256


def _pack_rows(parts, width):
    rows, counts = [], []
    for p in parts:
        flat = p.reshape(-1)
        nr = -(-flat.shape[0] // width)
        nr = -(-nr // SUBLANES) * SUBLANES
        rows.append(jnp.pad(flat, (0, nr * width - flat.shape[0])).reshape(nr, width))
        counts.append(nr)
    total = sum(counts)
    pad = -(-total // SLAB_ROW_ALIGN) * SLAB_ROW_ALIGN - total
    if pad:
        rows.append(jnp.zeros((pad, width), rows[0].dtype))
    return jnp.concatenate(rows, axis=0), counts


def _stack_rows(blocks):
    pieces, off = [], 0
    for b in blocks:
        pieces.append((off, b.shape[0]))
        off += b.shape[0]
    return jnp.concatenate(blocks, axis=0), pieces


def _unpack_rows(slab, counts, shapes):
    out, r = [], 0
    for nr, shape in zip(counts, shapes):
        size = math.prod(shape)
        out.append(slab[r:r + nr].reshape(-1)[:size].reshape(shape))
        r += nr
    return out


def kernel(x, ffn1_norm, ffn1_w_gate, ffn1_w_up, ffn1_w_down, mix_norm, w_in, lru_conv_w, lru_conv_b, lru_w_a, lru_b_a, lru_w_i, lru_b_i, lru_lambda, sc_conv_w, lru_out_norm, sc_out_norm, w_out, ffn2_norm, ffn2_w_gate, ffn2_w_up, ffn2_w_down, final_norm, loss_target, m_ffn1_norm, m_ffn1_w_gate, m_ffn1_w_up, m_ffn1_w_down, m_mix_norm, m_w_in, m_lru_conv_w, m_lru_conv_b, m_lru_w_a, m_lru_b_a, m_lru_w_i, m_lru_b_i, m_lru_lambda, m_sc_conv_w, m_lru_out_norm, m_sc_out_norm, m_w_out, m_ffn2_norm, m_ffn2_w_gate, m_ffn2_w_up, m_ffn2_w_down, m_final_norm, v_ffn1_norm, v_ffn1_w_gate, v_ffn1_w_up, v_ffn1_w_down, v_mix_norm, v_w_in, v_lru_conv_w, v_lru_conv_b, v_lru_w_a, v_lru_b_a, v_lru_w_i, v_lru_b_i, v_lru_lambda, v_sc_conv_w, v_lru_out_norm, v_sc_out_norm, v_w_out, v_ffn2_norm, v_ffn2_w_gate, v_ffn2_w_up, v_ffn2_w_down, v_final_norm):
    a = dict(locals())
    w = {n: a[n] for n in WEIGHT_NAMES}
    m = {n: a["m_" + n] for n in WEIGHT_NAMES}
    v = {n: a["v_" + n] for n in WEIGHT_NAMES}
    ax, ay, ac = _place()
    dev = 4 * ax + 2 * ay + ac
    chip = (2 * ax + ay).astype(jnp.int32).reshape(1)

    x0 = x[0]
    tgt = loss_target[0]
    T, D = x0.shape
    C = D // 2
    H, hd = lru_w_a.shape[1], lru_w_a.shape[2]
    CL = lru_conv_w.shape[2]

    shards = []
    for name, transposed in BIG:
        s = w[name][0]
        shards.append((s.T if transposed else s).astype(BF16))
    taps = jnp.concatenate([lru_conv_w[0], sc_conv_w[0], jnp.zeros((1, CL), F32)], axis=0)
    taps_row = lax.bitcast_convert_type(taps, BF16).reshape(1, -1)
    taps_blk = jnp.pad(taps_row, ((0, BF16_ROWS - 1), (0, D - taps_row.shape[1])))
    s_wg1, s_wu1, s_wd1, s_wg2, s_wu2, s_wd2, s_win, s_wout = shards
    slab_gu1, pcs_gu1 = _stack_rows([s_wg1, s_wu1])
    slab_d1, pcs_d1 = _stack_rows([s_wd1])
    slab_mw, pcs_mw = _stack_rows([s_win, s_wout, taps_blk])
    slab_gu2, pcs_gu2 = _stack_rows([s_wg2, s_wu2])
    slab_d2, pcs_d2 = _stack_rows([s_wd2])
    wg1, wu1 = _allgather(slab_gu1, pcs_gu1, "allgather_ffn1_up")

    g1, gm, g3 = ffn1_norm, mix_norm, ffn2_norm
    phase = _merge_phases(_ag_direct_phase(slab_d1, pcs_d1, ALL_OTHERS),
                          _ag_direct_phase(slab_mw, pcs_mw, SAME_CORE_AND_SIBLING))
    (n1, hg1, hu1, act1), got = _norm_proj(x0, g1, [wg1, wu1], [BF16, BF16], True, "ffn1_up", carried=phase)
    wd1, mixw = got[0], got[1:]
    phase = _merge_phases(_ag_forward_phase(mixw, pcs_mw), _ag_direct_phase(slab_gu2, pcs_gu2, SAME_CORE_AND_SIBLING))
    x1, got = _mm_res(act1, wd1, x0, 0.5, "ffn1_down", carried=phase)
    (win, wout, taps_all), gu2 = got[:3], got[3:]
    phase = _merge_phases(_ag_forward_phase(gu2, pcs_gu2), _ag_direct_phase(slab_d2, pcs_d2, SAME_CORE_AND_SIBLING))
    (n2, z), got = _norm_proj(x1, gm, [win], [F32], False, "in_proj", carried=phase)
    (wg2, wu2), d2 = got[:2], got[2:]
    taps_all = taps_all.reshape(N_DEV, BF16_ROWS, D)[:, 0, :2 * SUBLANES * CL].reshape(N_DEV, SUBLANES, CL, 2)
    taps_all = lax.bitcast_convert_type(taps_all, F32)
    taps_all = taps_all.transpose(1, 0, 2).reshape(SUBLANES, N_DEV * CL)
    cw, sw = taps_all[0:4], taps_all[4:7]

    gf = final_norm.reshape(1, D)
    cb = lru_conv_b
    wa, wi = lru_w_a[0].astype(BF16), lru_w_i[0].astype(BF16)
    ba, bi = lru_b_a.reshape(1, C), lru_b_i.reshape(1, C)
    lam, glo, gso = lru_lambda, lru_out_norm, sc_out_norm

    y, h = _mix_fwd(z, cw, cb, wa, ba, wi, bi, lam, sw, glo, gso, "mix_fwd")
    x2, (wd2,) = _mm_res(y, wout, x1, 1.0, "out_proj", carried=_ag_forward_phase(d2, pcs_d2))
    n3, hg2, hu2, act2 = _norm_proj(x2, g3, [wg2, wu2], [BF16, BF16], True, "ffn2_up")
    x3 = _mm_res(act2, wd2, x2, 0.5, "ffn2_down")
    dx3, df2, d_gf, loss_blk = _loss_head(x3, gf, tgt, "loss_head")

    F = wd1.shape[0]
    bm_f = F // 4 if (F // 4) % LANES == 0 else 512

    def reduce_group(gs, tag):
        pcs, off = [], 0
        for g_ in gs:
            pcs.append((off, g_.shape[0] // N_DEV))
            off += g_.shape[0] // N_DEV
        sb_ = _rs_sibling(gs, pcs, "rs_sibling_add_" + tag)
        return sb_, pcs

    dhg2, dhu2 = _ffn_bwd_act(df2, wd2, hg2, hu2, "ffn2_bwd_act")
    d_wd2 = _dw_tn(act2, df2, bm_f, "ffn2_dw_down")
    d_wg2 = _dw_tn(dhg2, n3, bm_f, "ffn2_dw_gate")
    d_wu2 = _dw_tn(dhu2, n3, bm_f, "ffn2_dw_up")
    sb_f2, pcs_f2 = reduce_group([d_wg2, d_wu2, d_wd2], "ffn2")
    (dx2, dx2b, d_g3), (lb_f2,) = _mm_rmsbwd([(dhg2, wg2), (dhu2, wu2)], x2, g3, dx3, 1.0, "ffn2_bwd_in",
                                             carried=_rs_chips_phase(sb_f2))
    dy = _mm_nt(dx2b, wout, "out_proj_bwd")
    d_wout = _dw_tn(y, dx2b, 1024, "out_proj_dw")
    dz, small, d_wa, d_wi = _mix_bwd(z, h, dy, cw, cb, wa, ba, wi, bi, lam, sw, glo, gso, "mix_bwd")
    d_win = _dw_tn(dz, n2, 1280, "in_proj_dw")
    sb_mx, pcs_mx = reduce_group([d_win, d_wout], "mix")
    (dx1, df1, d_gm), (lb_mx,) = _mm_rmsbwd([(dz, win)], x1, gm, dx2, 0.5, "in_proj_bwd",
                                            carried=_rs_chips_phase(sb_mx))
    dhg1, dhu1 = _ffn_bwd_act(df1, wd1, hg1, hu1, "ffn1_bwd_act")
    d_wd1 = _dw_tn(act1, df1, bm_f, "ffn1_dw_down")
    d_wg1 = _dw_tn(dhg1, n1, bm_f, "ffn1_dw_gate")
    d_wu1 = _dw_tn(dhu1, n1, bm_f, "ffn1_dw_up")
    sb_f1, pcs_f1 = reduce_group([d_wg1, d_wu1, d_wd1], "ffn1")
    (dx0, _, d_g1), (lb_f1,) = _mm_rmsbwd([(dhg1, wg1), (dhu1, wu1)], x0, g1, dx1, 1.0, "ffn1_bwd_in",
                                          carried=_rs_chips_phase(sb_f1))

    big_sum = {}
    for tag, names, sb_, lb_, pcs in (("ffn2", ("ffn2_w_gate", "ffn2_w_up", "ffn2_w_down"), sb_f2, lb_f2, pcs_f2),
                                      ("mix", ("w_in", "w_out"), sb_mx, lb_mx, pcs_mx),
                                      ("ffn1", ("ffn1_w_gate", "ffn1_w_up", "ffn1_w_down"), sb_f1, lb_f1, pcs_f1)):
        gsum = _final_grad(sb_, lb_, chip, "rs_final_sum_" + tag)
        for name, (off, rows) in zip(names, pcs):
            big_sum[name] = gsum[off:off + rows]

    small_parts = [d_g1, d_gm, d_g3, d_gf, small[R_CW:R_CW + 4], small[R_CB], d_wa, small[R_BA], d_wi, small[R_BI],
                   small[R_LAM], small[R_SW:R_SW + 3], small[R_GLO], small[R_GSO]]
    sslab, counts = _pack_rows(small_parts, LANES)
    RS = sslab.shape[0]
    (sg,) = _allgather(sslab, [(0, RS)], "allgather_small_grads")
    ssum = _add_slabs([sg[j * RS:(j + 1) * RS] for j in range(N_DEV)], F32, "small_grads_sum")
    full_shapes = [(1, D), (1, D), (1, D), (D,), (1, 4, C), (1, C), (1, H, hd, hd), (1, H, hd), (1, H, hd, hd), (1, H, hd),
                   (1, C), (1, 3, C), (1, C), (1, C)]
    small_full = dict(zip(SMALL_NAMES, _unpack_rows(ssum, counts, full_shapes)))

    grads = {}
    for name, transposed in BIG:
        gblk = big_sum[name]
        grads[name] = (gblk.T if transposed else gblk)[None]
    for name in SMALL_NAMES:
        gfull = small_full[name]
        if name in ("lru_conv_w", "sc_conv_w"):
            gfull = lax.dynamic_slice_in_dim(gfull, dev * CL, CL, axis=2)
        grads[name] = gfull

    delta, new_m, new_v = {}, {}, {}
    for name, transposed in BIG:
        flip = transposed and w[name].shape[2] % LANES != 0
        view = (lambda t: t[0].T) if flip else (lambda t: t[0])
        back = (lambda t: t.T[None]) if flip else (lambda t: t[None])
        gview = big_sum[name] if flip else grads[name][0]
        d_, m_, v_ = _adamw(view(w[name]), gview, view(m[name]), view(v[name]), "adamw_" + name)
        delta[name], new_m[name], new_v[name] = back(d_), back(m_), back(v_)
    packs = [_pack_rows([t[n_] for n_ in SMALL_NAMES], LANES) for t in (w, grads, m, v)]
    sd, sm, sv = _adamw(packs[0][0], packs[1][0], packs[2][0], packs[3][0], "adamw_small")
    shapes = [w[n_].shape for n_ in SMALL_NAMES]
    for tgt_dict, slab_ in ((delta, sd), (new_m, sm), (new_v, sv)):
        for n_, val in zip(SMALL_NAMES, _unpack_rows(slab_, packs[0][1], shapes)):
            tgt_dict[n_] = val

    loss = lax.psum(loss_blk[0, 0], ("x", "y", "c"))
    return (loss, dx0[None], *[grads[n_] for n_ in WEIGHT_NAMES], *[delta[n_] for n_ in WEIGHT_NAMES],
            *[new_m[n_] for n_ in WEIGHT_NAMES], *[new_v[n_] for n_ in WEIGHT_NAMES])
```

```python
import functools
import math

import jax
import jax.numpy as jnp
from jax import lax
from jax.experimental import pallas as pl
from jax.experimental.pallas import tpu as pltpu

F32 = jnp.float32
BF16 = jnp.bfloat16
SDS = jax.ShapeDtypeStruct
MESH = pl.DeviceIdType.MESH

NORM_EPS = 1e-6
LRU_C = 8.0
N_DEV = 8
N_CHIP = 4
ADAM_LR, ADAM_B1, ADAM_B2, ADAM_EPS, ADAM_WD, ADAM_STEP = 0.001, 0.9, 0.999, 1e-08, 0.01, 10

NN = (((1,), (0,)), ((), ()))
NT = (((1,), (1,)), ((), ()))
TN = (((0,), (0,)), ((), ()))

SUBLANES = 8
BF16_ROWS = 16
LANES = 128
MIB = 1 << 20


def _dot(a, b, dims):
    return lax.dot_general(a, b, dims, preferred_element_type=F32)


def _blk(n, pref, align):
    if n <= pref:
        return n
    b = (pref // align) * align
    while b >= align:
        if n % b == 0:
            return b
        b -= align
    raise ValueError(f"no block of {n} aligned to {align} under {pref}")


def _cp(sem, vmem_mib):
    return pltpu.CompilerParams(dimension_semantics=sem, vmem_limit_bytes=vmem_mib * MIB)


HBM_SPEC = pl.BlockSpec(memory_space=pltpu.HBM)


class _Carried:
    def __init__(self, inputs, out_shapes, aliases, sem_shapes, build):
        self.inputs, self.out_shapes, self.aliases = list(inputs), list(out_shapes), dict(aliases)
        self.sem_shapes, self.build = list(sem_shapes), build


def _call(body, *, name, grid, in_specs, out_specs, out_shape, scratch_shapes, compiler_params, args, carried=None):
    if carried is None:
        return pl.pallas_call(body, name=name, grid=grid, in_specs=in_specs, out_specs=out_specs, out_shape=out_shape,
                              scratch_shapes=scratch_shapes, compiler_params=compiler_params)(*args)
    n_in, n_out, n_sc = len(in_specs), len(out_shape), len(scratch_shapes)
    c_in, c_out = len(carried.inputs), len(carried.out_shapes)

    def hosted(*refs):
        ins, refs = refs[:n_in], refs[n_in:]
        c_ins, refs = refs[:c_in], refs[c_in:]
        outs, refs = refs[:n_out], refs[n_out:]
        c_outs, refs = refs[:c_out], refs[c_out:]
        scratch, c_sems = refs[:n_sc], refs[n_sc:]
        first = functools.reduce(jnp.logical_and, [pl.program_id(a) == 0 for a in range(len(grid))])
        last = functools.reduce(jnp.logical_and, [pl.program_id(a) == g - 1 for a, g in enumerate(grid)])

        @pl.when(first)
        def _():
            for start in carried.build(c_ins, c_outs, c_sems, True):
                start()

        body(*ins, *outs, *scratch)

        @pl.when(last)
        def _():
            for wait in carried.build(c_ins, c_outs, c_sems, False):
                wait()

    out = pl.pallas_call(
        hosted, name=name, grid=grid, in_specs=list(in_specs) + [HBM_SPEC] * c_in,
        out_specs=list(out_specs) + [HBM_SPEC] * c_out, out_shape=list(out_shape) + carried.out_shapes,
        scratch_shapes=list(scratch_shapes) + carried.sem_shapes,
        input_output_aliases={n_in + a: n_out + b for a, b in carried.aliases.items()},
        compiler_params=compiler_params)(*args, *carried.inputs)
    return out[:n_out], out[n_out:]


ROW_CHUNK = 128


EPILOGUE_ROWS = 128


def _chunk_rows(c):
    return pl.ds(pl.multiple_of(c * ROW_CHUNK, ROW_CHUNK), ROW_CHUNK)


def _rstd(xv):
    return lax.rsqrt(jnp.mean(xv * xv, axis=-1, keepdims=True) + NORM_EPS)


def _rms_bwd(xv, g, dn):
    r = _rstd(xv)
    xr = xv * r
    gd = g * dn
    dx = r * (gd - xr * jnp.mean(gd * xr, axis=-1, keepdims=True))
    return dx, jnp.sum(dn * xr, axis=0, keepdims=True)


def _log1p(e):
    u = 1.0 + e
    return jnp.where(u == 1.0, e, jnp.log(u) * (e / (u - 1.0)))


def _one_minus_exp(v, exp_half_v):
    series = 1.0 / 5040.0
    for coeff in (1.0 / 720.0, 1.0 / 120.0, 1.0 / 24.0, 1.0 / 6.0, 0.5, 1.0):
        series = series * v + coeff
    return jnp.where(v > -0.5, -v * series, 1.0 - exp_half_v * exp_half_v)


def _sigmoid(v):
    return 0.5 * jnp.tanh(0.5 * v) + 0.5


def _gelu_parts(g):
    k0 = math.sqrt(2.0 / math.pi)
    g2 = g * g
    t = jnp.tanh(k0 * (g + 0.044715 * g * g2))
    gel = 0.5 * g * (1.0 + t)
    gelp = 0.5 * (1.0 + t) + 0.5 * g * (1.0 - t * t) * (k0 * (1.0 + 3.0 * 0.044715 * g2))
    return gel, gelp


def _norm_proj(x, gain, w_list, out_dtypes, swiglu, name, carried=None):
    T, D = x.shape
    N = w_list[0].shape[0]
    nw = len(w_list)
    bm = _blk(T, 1024, BF16_ROWS)
    bn = _blk(N, 512, LANES)

    def body(*refs):
        x_ref, g_ref = refs[:2]
        w_refs = refs[2:2 + nw]
        n_ref = refs[2 + nw]
        o_refs = refs[3 + nw:3 + 2 * nw]
        act_ref = refs[3 + 2 * nw] if swiglu else None
        n_sc = refs[-1]

        @pl.when(pl.program_id(1) == 0)
        def _():
            def piece(p, _):
                r = pl.ds(pl.multiple_of(p * EPILOGUE_ROWS, EPILOGUE_ROWS), EPILOGUE_ROWS)
                xv = x_ref[r, :]
                nb = (xv * _rstd(xv) * g_ref[...]).astype(BF16)
                n_sc[r, :] = nb
                n_ref[r, :] = nb
                return 0
            lax.fori_loop(0, bm // EPILOGUE_ROWS, piece, 0)

        n = n_sc[...]
        outs = [_dot(n, w_ref[...], NT) for w_ref in w_refs]
        for o_ref, o in zip(o_refs, outs):
            o_ref[...] = o.astype(o_ref.dtype)
        if swiglu:
            hg, hu = outs
            act_ref[...] = (hg * _sigmoid(hg) * hu).astype(BF16)

    row = pl.BlockSpec((bm, D), lambda i, j: (i, 0))
    tile = pl.BlockSpec((bm, bn), lambda i, j: (i, j))
    n_extra = 1 if swiglu else 0
    return _call(
        body, name=name, grid=(T // bm, N // bn),
        in_specs=[row, pl.BlockSpec((1, D), lambda i, j: (0, 0))] + [pl.BlockSpec((bn, D), lambda i, j: (j, 0))] * nw,
        out_specs=[row] + [tile] * (nw + n_extra),
        out_shape=[SDS((T, D), BF16)] + [SDS((T, N), dt) for dt in out_dtypes] + [SDS((T, N), BF16)] * n_extra,
        scratch_shapes=[pltpu.VMEM((bm, D), BF16)],
        compiler_params=_cp(("arbitrary", "arbitrary"), 52),
        args=(x, gain, *w_list), carried=carried)


def _mm_res(a, b, x, scale, name, carried=None):
    T, K = a.shape
    D = b.shape[1]
    bm = _blk(T, 1024, BF16_ROWS)
    bk = _blk(K, 1408, LANES)
    nk = K // bk

    def body(a_ref, b_ref, x_ref, o_ref):
        k = pl.program_id(1)

        @pl.when(k == 0)
        def _():
            o_ref[...] = jnp.zeros_like(o_ref)

        o_ref[...] += _dot(a_ref[...], b_ref[...], NN)

        @pl.when(k == nk - 1)
        def _():
            def chunk(c, _):
                r = _chunk_rows(c)
                o_ref[r, :] = x_ref[r, :] + scale * o_ref[r, :]
                return 0
            lax.fori_loop(0, bm // ROW_CHUNK, chunk, 0)

    row = pl.BlockSpec((bm, D), lambda i, k: (i, 0))
    out = _call(
        body, name=name, grid=(T // bm, nk),
        in_specs=[pl.BlockSpec((bm, bk), lambda i, k: (i, k)), pl.BlockSpec((bk, D), lambda i, k: (k, 0)), row],
        out_specs=[row], out_shape=[SDS((T, D), F32)], scratch_shapes=[],
        compiler_params=_cp(("arbitrary", "arbitrary"), 56),
        args=(a, b, x), carried=carried)
    return out[0] if carried is None else (out[0][0], out[1])


def _mm_nt(a, b, name):
    T, K = a.shape
    N = b.shape[0]
    bm = _blk(T, 1024, BF16_ROWS)
    bn = _blk(N, 512, LANES)

    def body(a_ref, b_ref, o_ref):
        o_ref[...] = _dot(a_ref[...], b_ref[...], NT)

    return pl.pallas_call(
        body, name=name, grid=(T // bm, N // bn),
        in_specs=[pl.BlockSpec((bm, K), lambda i, j: (i, 0)), pl.BlockSpec((bn, K), lambda i, j: (j, 0))],
        out_specs=pl.BlockSpec((bm, bn), lambda i, j: (i, j)), out_shape=SDS((T, N), F32),
        compiler_params=_cp(("arbitrary", "arbitrary"), 40),
    )(a, b)


def _ffn_bwd_act(dfb, wd, hg, hu, name):
    T, D = dfb.shape
    F = wd.shape[0]
    bm = _blk(T, 1024, BF16_ROWS)
    bn = _blk(F, 512, LANES)

    def body(df_ref, wd_ref, hg_ref, hu_ref, dhg_ref, dhu_ref):
        dact = _dot(df_ref[...], wd_ref[...], NT)
        hgv = hg_ref[...].astype(F32)
        huv = hu_ref[...].astype(F32)
        s = _sigmoid(hgv)
        dhu_ref[...] = (dact * (hgv * s)).astype(BF16)
        dhg_ref[...] = (dact * huv * (s * (1.0 + hgv * (1.0 - s)))).astype(BF16)

    tile = pl.BlockSpec((bm, bn), lambda i, j: (i, j))
    return pl.pallas_call(
        body, name=name, grid=(T // bm, F // bn),
        in_specs=[pl.BlockSpec((bm, D), lambda i, j: (i, 0)), pl.BlockSpec((bn, D), lambda i, j: (j, 0)), tile, tile],
        out_specs=[tile, tile], out_shape=[SDS((T, F), BF16)] * 2,
        compiler_params=_cp(("arbitrary", "arbitrary"), 40),
    )(dfb, wd, hg, hu)


def _dw_tn(a, b, bm_pref, name):
    T, M = a.shape
    N = b.shape[1]
    bm = _blk(M, bm_pref, LANES)
    tk = _blk(T, 1024, BF16_ROWS)
    nk = T // tk

    def body(a_ref, b_ref, o_ref, acc):
        k = pl.program_id(1)

        @pl.when(k == 0)
        def _():
            acc[...] = jnp.zeros_like(acc)

        acc[...] += _dot(a_ref[...], b_ref[...], TN)

        @pl.when(k == nk - 1)
        def _():
            o_ref[...] = acc[...].astype(BF16)

    return pl.pallas_call(
        body, name=name, grid=(M // bm, nk),
        in_specs=[pl.BlockSpec((tk, bm), lambda i, k: (k, i)), pl.BlockSpec((tk, N), lambda i, k: (k, 0))],
        out_specs=pl.BlockSpec((bm, N), lambda i, k: (i, 0)), out_shape=SDS((M, N), BF16),
        scratch_shapes=[pltpu.VMEM((bm, N), F32)],
        compiler_params=_cp(("arbitrary", "arbitrary"), 48),
    )(a, b)


def _mm_rmsbwd(pairs, x, gain, dx_in, bscale, name, carried=None):
    T, D = x.shape
    K = pairs[0][0].shape[1]
    npair = len(pairs)
    bm = _blk(T, 1024, BF16_ROWS)
    bk = _blk(K, 1024 // npair, LANES)
    nk = K // bk

    nchunk = bm // ROW_CHUNK

    def body(*refs):
        ab = refs[:2 * npair]
        x_hbm, g_ref, dxin_hbm, dx_ref, dxb_ref, dg_ref, x_buf, dxin_buf, sems = refs[2 * npair:]
        i = pl.program_id(0)
        k = pl.program_id(1)

        def fetch(c, slot):
            rows = pl.ds(i * bm + c * ROW_CHUNK, ROW_CHUNK)
            return (pltpu.make_async_copy(x_hbm.at[rows, :], x_buf.at[slot], sems.at[slot, 0]),
                    pltpu.make_async_copy(dxin_hbm.at[rows, :], dxin_buf.at[slot], sems.at[slot, 1]))

        @pl.when(k == 0)
        def _():
            dx_ref[...] = jnp.zeros_like(dx_ref)

        @pl.when(k == nk - 1)
        def _():
            for cp in fetch(0, 0):
                cp.start()

        for q in range(npair):
            dx_ref[...] += _dot(ab[2 * q][...], ab[2 * q + 1][...], NN)

        @pl.when(k == nk - 1)
        def _():
            @pl.when(i == 0)
            def _():
                dg_ref[...] = jnp.zeros_like(dg_ref)

            def chunk(c, _):
                slot = c % 2

                @pl.when(c + 1 < nchunk)
                def _():
                    for cp in fetch(c + 1, 1 - slot):
                        cp.start()

                for cp in fetch(c, slot):
                    cp.wait()

                def piece(p, _):
                    rb = pl.ds(pl.multiple_of(p * EPILOGUE_ROWS, EPILOGUE_ROWS), EPILOGUE_ROWS)
                    r = pl.ds(pl.multiple_of(c * ROW_CHUNK + p * EPILOGUE_ROWS, EPILOGUE_ROWS), EPILOGUE_ROWS)
                    dx, dg = _rms_bwd(x_buf[slot, rb, :], g_ref[...], dx_ref[r, :])
                    dxo = dxin_buf[slot, rb, :] + dx
                    dx_ref[r, :] = dxo
                    dxb_ref[r, :] = (bscale * dxo).astype(BF16)
                    dg_ref[...] += dg
                    return 0
                lax.fori_loop(0, ROW_CHUNK // EPILOGUE_ROWS, piece, 0)
                return 0
            lax.fori_loop(0, nchunk, chunk, 0)

    row = pl.BlockSpec((bm, D), lambda i, k: (i, 0))
    anywhere = pl.BlockSpec(memory_space=pl.ANY)
    vec = pl.BlockSpec((1, D), lambda i, k: (0, 0))
    in_specs = []
    args = []
    for a, b in pairs:
        in_specs += [pl.BlockSpec((bm, bk), lambda i, k: (i, k)), pl.BlockSpec((bk, D), lambda i, k: (k, 0))]
        args += [a, b]
    return _call(
        body, name=name, grid=(T // bm, nk),
        in_specs=in_specs + [anywhere, vec, anywhere], out_specs=[row, row, vec],
        out_shape=[SDS((T, D), F32), SDS((T, D), BF16), SDS((1, D), F32)],
        scratch_shapes=[pltpu.VMEM((2, ROW_CHUNK, D), F32), pltpu.VMEM((2, ROW_CHUNK, D), F32),
                        pltpu.SemaphoreType.DMA((2, 2))],
        compiler_params=_cp(("arbitrary", "arbitrary"), 52),
        args=(*args, x, gain, dx_in), carried=carried)


def _loss_head(x3, gain, tgt, name):
    T, D = x3.shape
    bm = _blk(T, 256, BF16_ROWS)

    def body(x_ref, g_ref, t_ref, dx_ref, dxb_ref, dg_ref, loss_ref):
        i = pl.program_id(0)
        xv = x_ref[...]
        g = g_ref[...]
        out = xv * _rstd(xv) * g
        e = out - t_ref[...]
        part = 0.5 * jnp.sum(jnp.mean(e * e, axis=-1, keepdims=True), axis=0, keepdims=True)
        dx, dg = _rms_bwd(xv, g, e * (1.0 / D))
        dx_ref[...] = dx
        dxb_ref[...] = (0.5 * dx).astype(BF16)

        @pl.when(i == 0)
        def _():
            dg_ref[...] = dg
            loss_ref[...] = jnp.broadcast_to(part, loss_ref.shape)

        @pl.when(i > 0)
        def _():
            dg_ref[...] += dg
            loss_ref[...] += jnp.broadcast_to(part, loss_ref.shape)

    row = pl.BlockSpec((bm, D), lambda i: (i, 0))
    vec = pl.BlockSpec((1, D), lambda i: (0, 0))
    return pl.pallas_call(
        body, name=name, grid=(T // bm,),
        in_specs=[row, vec, row], out_specs=[row, row, vec, pl.BlockSpec((SUBLANES, LANES), lambda i: (0, 0))],
        out_shape=[SDS((T, D), F32), SDS((T, D), BF16), SDS((1, D), F32), SDS((SUBLANES, LANES), F32)],
        compiler_params=_cp(("arbitrary",), 40),
    )(x3, gain, tgt)


R_CW, R_CB, R_BA, R_BI, R_LAM, R_SW, R_GLO, R_GSO, SMALL_ROWS = 0, 4, 5, 6, 7, 8, 11, 12, 16


def _rows(g):
    return pl.ds(pl.multiple_of(g * SUBLANES, SUBLANES), SUBLANES)


def _shift_back(prev, cur, d):
    row = lax.broadcasted_iota(jnp.int32, cur.shape, 0)
    return pltpu.roll(jnp.where(row >= SUBLANES - d, prev, cur), d, 0)


def _shift_fwd(cur, nxt, d):
    row = lax.broadcasted_iota(jnp.int32, cur.shape, 0)
    return pltpu.roll(jnp.where(row < d, nxt, cur), SUBLANES - d, 0)


def _causal_conv(ext, g, taps_ref, ntap):
    prev = ext[_rows(g), :]
    cur = ext[_rows(g + 1), :]
    out = _shift_back(prev, cur, ntap - 1) * taps_ref[0:1, :]
    for k in range(1, ntap - 1):
        out = out + _shift_back(prev, cur, ntap - 1 - k) * taps_ref[k:k + 1, :]
    return out + cur * taps_ref[ntap - 1:ntap, :]


def _scan8(A, U, reverse):
    row = lax.broadcasted_iota(jnp.int32, A.shape, 0)
    for s in (1, 2, 4):
        if reverse:
            A_sh = pltpu.roll(A, SUBLANES - s, 0)
            U_sh = pltpu.roll(U, SUBLANES - s, 0)
            m = row < SUBLANES - s
        else:
            A_sh = pltpu.roll(A, s, 0)
            U_sh = pltpu.roll(U, s, 0)
            m = row >= s
        U = jnp.where(m, A * U_sh + U, U)
        A = jnp.where(m, A * A_sh, A)
    return A, U


def _gate_pre(xc_s, w_ref, out_s, H, hd):
    for h in range(H):
        cs = slice(h * hd, (h + 1) * hd)
        out_s[:, cs] = _dot(xc_s[:, cs].astype(BF16), w_ref[h], NN)


def _lru_coeffs(pa, pi, xc, ba, bi, sp):
    ra = _sigmoid(pa + ba)
    ri = _sigmoid(pi + bi)
    log_a = (-LRU_C * ra) * sp
    a = jnp.exp(log_a)
    mult = jnp.sqrt(_one_minus_exp(2.0 * log_a, a))
    return ra, ri, a, mult


def _softplus_neg(lam):
    v = -lam
    return jnp.maximum(v, 0.0) + _log1p(jnp.exp(-jnp.abs(v)))


def _mix_fwd(z, cw, cb, wa, ba, wi, bi, lam, sw, glo, gso, name):
    T = z.shape[0]
    C = z.shape[1] // 5
    H = wa.shape[0]
    hd = C // H
    tb = _blk(T, 256, BF16_ROWS)
    ng = tb // SUBLANES
    HDR = SUBLANES

    def body(z_ref, cw_ref, cb_ref, wa_ref, ba_ref, wi_ref, bi_ref, lam_ref, sw_ref, glo_ref, gso_ref,
             y_ref, h_ref, xext, pext, xc_s, pa_s, pi_s, y_s, hcar):
        @pl.when(pl.program_id(0) == 0)
        def _():
            xext[0:HDR, :] = jnp.zeros((HDR, C), F32)
            pext[0:HDR, :] = jnp.zeros((HDR, C), F32)
            hcar[...] = jnp.zeros_like(hcar)

        def fill(g, _):
            r = _rows(g)
            re = _rows(g + 1)
            xext[re, :] = z_ref[r, 0:C]
            pext[re, :] = z_ref[r, 3 * C:4 * C] * z_ref[r, 4 * C:5 * C]
            return 0
        lax.fori_loop(0, ng, fill, 0)

        def conv(g, _):
            xc_s[_rows(g), :] = _causal_conv(xext, g, cw_ref, 4) + cb_ref[...]
            return 0
        lax.fori_loop(0, ng, conv, 0)

        _gate_pre(xc_s, wa_ref, pa_s, H, hd)
        _gate_pre(xc_s, wi_ref, pi_s, H, hd)
        sp = _softplus_neg(lam_ref[...])

        def group(g, hprev):
            r = _rows(g)
            xc = xc_s[r, :]
            _, ri, a, mult = _lru_coeffs(pa_s[r, :], pi_s[r, :], xc, ba_ref[...], bi_ref[...], sp)
            A, U = _scan8(a, mult * (ri * xc), reverse=False)
            hh = A * hprev + U
            h_ref[r, :] = hh
            gel, _ = _gelu_parts(z_ref[r, C:2 * C])
            y_lru = hh * gel
            y_s[r, 0:C] = y_lru * _rstd(y_lru) * glo_ref[...]
            y_sc = z_ref[r, 2 * C:3 * C] * _causal_conv(pext, g, sw_ref, 3)
            y_s[r, C:2 * C] = y_sc * _rstd(y_sc) * gso_ref[...]
            return jnp.broadcast_to(hh[SUBLANES - 1:SUBLANES, :], hh.shape)
        hcar[...] = lax.fori_loop(0, ng // 2, lambda t, hp: group(2 * t + 1, group(2 * t, hp)), hcar[...])

        xext[0:HDR, :] = xext[tb:tb + HDR, :]
        pext[0:HDR, :] = pext[tb:tb + HDR, :]

        def cast(g, _):
            r = pl.ds(pl.multiple_of(g * BF16_ROWS, BF16_ROWS), BF16_ROWS)
            y_ref[r, :] = y_s[r, :].astype(BF16)
            return 0
        lax.fori_loop(0, tb // BF16_ROWS, cast, 0)

    full = lambda shape: pl.BlockSpec(shape, lambda i: (0,) * len(shape))
    blk = lambda w: pl.BlockSpec((tb, w), lambda i: (i, 0))
    ext = pltpu.VMEM((tb + HDR, C), F32)
    tile = pltpu.VMEM((tb, C), F32)
    return pl.pallas_call(
        body, name=name, grid=(T // tb,),
        in_specs=[blk(5 * C), full((4, C)), full((1, C)), full((H, hd, hd)), full((1, C)), full((H, hd, hd)),
                  full((1, C)), full((1, C)), full((3, C)), full((1, C)), full((1, C))],
        out_specs=[blk(2 * C), blk(C)],
        out_shape=[SDS((T, 2 * C), BF16), SDS((T, C), F32)],
        scratch_shapes=[ext, ext, tile, tile, tile, pltpu.VMEM((tb, 2 * C), F32), pltpu.VMEM((SUBLANES, C), F32)],
        compiler_params=_cp(("arbitrary",), 40),
    )(z, cw, cb, wa, ba, wi, bi, lam, sw, glo, gso)


def _mix_bwd(z, h, dy, cw, cb, wa, ba, wi, bi, lam, sw, glo, gso, name):
    T = z.shape[0]
    C = z.shape[1] // 5
    H = wa.shape[0]
    hd = C // H
    tb = _blk(T, 256, BF16_ROWS)
    nb = T // tb
    ng = tb // SUBLANES
    HDR = SUBLANES
    N_ACC = 13

    def body(z_ref, zp_ref, h_ref, hp_ref, dy_ref, cw_ref, cb_ref, wa_ref, ba_ref, wi_ref, bi_ref, lam_ref,
             sw_ref, glo_ref, gso_ref, dz_ref, small_ref, dwa_ref, dwi_ref,
             xext, pext, hext, dqext, dxcext, bext, xc_s, pa_s, pi_s, a_s, m_s, ri_s, dh_s, dpa_s, dpi_s,
             dz_s, acc_s, bcar):
        i = pl.program_id(0)
        first_rows = i == nb - 1

        @pl.when(i == 0)
        def _():
            dqext[tb:tb + HDR, :] = jnp.zeros((HDR, C), F32)
            dxcext[tb:tb + HDR, :] = jnp.zeros((HDR, C), F32)
            bcar[...] = jnp.zeros_like(bcar)
            acc_s[...] = jnp.zeros_like(acc_s)
            dwa_ref[...] = jnp.zeros_like(dwa_ref)
            dwi_ref[...] = jnp.zeros_like(dwi_ref)

        zero = jnp.zeros((HDR, C), F32)
        xext[0:HDR, :] = jnp.where(first_rows, zero, zp_ref[:, 0:C])
        pext[0:HDR, :] = jnp.where(first_rows, zero, zp_ref[:, 3 * C:4 * C] * zp_ref[:, 4 * C:5 * C])
        hext[0:HDR, :] = jnp.where(first_rows, zero, hp_ref[...])

        def fill(g, _):
            r = _rows(g)
            re = _rows(g + 1)
            xext[re, :] = z_ref[r, 0:C]
            pext[re, :] = z_ref[r, 3 * C:4 * C] * z_ref[r, 4 * C:5 * C]
            hext[re, :] = h_ref[r, :]
            return 0
        lax.fori_loop(0, ng, fill, 0)

        def conv(g, _):
            xc_s[_rows(g), :] = _causal_conv(xext, g, cw_ref, 4) + cb_ref[...]
            return 0
        lax.fori_loop(0, ng, conv, 0)

        _gate_pre(xc_s, wa_ref, pa_s, H, hd)
        _gate_pre(xc_s, wi_ref, pi_s, H, hd)
        sp = _softplus_neg(lam_ref[...])
        dsp_dlam = -jax.nn.sigmoid(-lam_ref[...])

        def add_acc(k, v):
            acc_s[k] += v

        def p1(g, _):
            r = _rows(g)
            xc = xc_s[r, :]
            _, ri, a, mult = _lru_coeffs(pa_s[r, :], pi_s[r, :], xc, ba_ref[...], bi_ref[...], sp)
            a_s[r, :] = a
            m_s[r, :] = mult
            ri_s[r, :] = ri
            hh = h_ref[r, :]
            gel, gelp = _gelu_parts(z_ref[r, C:2 * C])
            y_lru = hh * gel
            dnl = dy_ref[r, 0:C]
            rl = _rstd(y_lru)
            ylr = y_lru * rl
            gd = glo_ref[...] * dnl
            dy_lru = rl * (gd - ylr * jnp.mean(gd * ylr, axis=-1, keepdims=True))
            add_acc(R_GLO, dnl * ylr)
            dz_s[r, C:2 * C] = dy_lru * hh * gelp
            dh = dy_lru * gel
            dh_s[r, :] = dh

            q = _causal_conv(pext, g, sw_ref, 3)
            scb = z_ref[r, 2 * C:3 * C]
            y_sc = scb * q
            dns = dy_ref[r, C:2 * C]
            rs = _rstd(y_sc)
            ysr = y_sc * rs
            gs = gso_ref[...] * dns
            dy_sc = rs * (gs - ysr * jnp.mean(gs * ysr, axis=-1, keepdims=True))
            add_acc(R_GSO, dns * ysr)
            dz_s[r, 2 * C:3 * C] = dy_sc * q
            dqext[r, :] = dy_sc * scb
            return 0
        lax.fori_loop(0, ng, p1, 0, unroll=2)

        bext[tb:tb + HDR, :] = bcar[...]

        def p2(j, carry):
            g = ng - 1 - j
            r = _rows(g)
            a = a_s[r, :]
            A, U = _scan8(a, a * dh_s[r, :], reverse=True)
            bb = A * carry + U
            bext[r, :] = bb
            return jnp.broadcast_to(bb[0:1, :], bb.shape)
        bcar[...] = lax.fori_loop(0, ng, p2, bcar[...])

        def p3(g, _):
            r = _rows(g)
            rn = _rows(g + 1)
            G = dh_s[r, :] + _shift_fwd(bext[r, :], bext[rn, :], 1)
            hm1 = _shift_back(hext[r, :], hext[rn, :], 1)
            a = a_s[r, :]
            mult = m_s[r, :]
            ri = ri_s[r, :]
            xc = xc_s[r, :]
            ra = _sigmoid(pa_s[r, :] + ba_ref[...])
            dxcext[r, :] = G * mult * ri
            dri = G * mult * xc
            dmult = G * ri * xc
            dlog_a = (G * hm1) * a - dmult * (a * a) / mult
            add_acc(R_LAM, dlog_a * (-LRU_C * ra) * dsp_dlam)
            dpa = dlog_a * (-LRU_C * sp) * ra * (1.0 - ra)
            dpi = dri * ri * (1.0 - ri)
            add_acc(R_BA, dpa)
            add_acc(R_BI, dpi)
            dpa_s[r, :] = dpa
            dpi_s[r, :] = dpi
            return 0
        lax.fori_loop(0, ng, p3, 0)

        for hh_ in range(H):
            cs = slice(hh_ * hd, (hh_ + 1) * hd)
            dpa_b = dpa_s[:, cs].astype(BF16)
            dpi_b = dpi_s[:, cs].astype(BF16)
            xc_b = xc_s[:, cs].astype(BF16)
            dxcext[0:tb, cs] += _dot(dpa_b, wa_ref[hh_], NT) + _dot(dpi_b, wi_ref[hh_], NT)
            dwa_ref[hh_] += _dot(xc_b, dpa_b, TN)
            dwi_ref[hh_] += _dot(xc_b, dpi_b, TN)

        def p4(g, _):
            r = _rows(g)
            rn = _rows(g + 1)
            dxc = dxcext[r, :]
            dxc_n = dxcext[rn, :]
            x_p = xext[r, :]
            x_c = xext[rn, :]
            add_acc(R_CB, dxc)
            dlx = dxc * cw_ref[3:4, :]
            add_acc(R_CW + 3, dxc * x_c)
            for d in range(1, 4):
                dlx = dlx + _shift_fwd(dxc, dxc_n, d) * cw_ref[3 - d:4 - d, :]
                add_acc(R_CW + 3 - d, dxc * _shift_back(x_p, x_c, d))
            dz_s[r, 0:C] = dlx

            dq = dqext[r, :]
            dq_n = dqext[rn, :]
            p_p = pext[r, :]
            p_c = pext[rn, :]
            dp = dq * sw_ref[2:3, :]
            add_acc(R_SW + 2, dq * p_c)
            for d in range(1, 3):
                dp = dp + _shift_fwd(dq, dq_n, d) * sw_ref[2 - d:3 - d, :]
                add_acc(R_SW + 2 - d, dq * _shift_back(p_p, p_c, d))
            dz_s[r, 3 * C:4 * C] = dp * z_ref[r, 4 * C:5 * C]
            dz_s[r, 4 * C:5 * C] = dp * z_ref[r, 3 * C:4 * C]
            return 0
        lax.fori_loop(0, ng, p4, 0)

        dqext[tb:tb + HDR, :] = dqext[0:HDR, :]
        dxcext[tb:tb + HDR, :] = dxcext[0:HDR, :]

        def cast(g, _):
            r = pl.ds(pl.multiple_of(g * BF16_ROWS, BF16_ROWS), BF16_ROWS)
            dz_ref[r, :] = dz_s[r, :].astype(BF16)
            return 0
        lax.fori_loop(0, tb // BF16_ROWS, cast, 0)

        @pl.when(i == nb - 1)
        def _():
            small_ref[...] = jnp.zeros_like(small_ref)
            for k in range(N_ACC):
                small_ref[k:k + 1, :] = jnp.sum(acc_s[k], axis=0, keepdims=True)

    tpg = tb // SUBLANES
    full = lambda shape: pl.BlockSpec(shape, lambda i: (0,) * len(shape))
    blk = lambda w: pl.BlockSpec((tb, w), lambda i: (nb - 1 - i, 0))
    prev = lambda w: pl.BlockSpec((SUBLANES, w), lambda i: (jnp.maximum((nb - 1 - i) * tpg - 1, 0), 0))
    ext = pltpu.VMEM((tb + HDR, C), F32)
    tile = pltpu.VMEM((tb, C), F32)
    return pl.pallas_call(
        body, name=name, grid=(nb,),
        in_specs=[blk(5 * C), prev(5 * C), blk(C), prev(C), blk(2 * C), full((4, C)), full((1, C)), full((H, hd, hd)),
                  full((1, C)), full((H, hd, hd)), full((1, C)), full((1, C)), full((3, C)), full((1, C)), full((1, C))],
        out_specs=[blk(5 * C), full((SMALL_ROWS, C)), full((H, hd, hd)), full((H, hd, hd))],
        out_shape=[SDS((T, 5 * C), BF16), SDS((SMALL_ROWS, C), F32), SDS((H, hd, hd), F32), SDS((H, hd, hd), F32)],
        scratch_shapes=[ext] * 6 + [tile] * 9 + [pltpu.VMEM((tb, 5 * C), F32), pltpu.VMEM((N_ACC, SUBLANES, C), F32),
                                                pltpu.VMEM((SUBLANES, C), F32)],
        compiler_params=_cp(("arbitrary",), 56),
    )(z, z, h, h, dy, cw, cb, wa, ba, wi, bi, lam, sw, glo, gso)


def _add_slabs(terms, out_dtype, name):
    R, Ccols = terms[0].shape
    br = _blk(R, 512, BF16_ROWS)
    n = len(terms)

    def body(*refs):
        s = refs[0][...].astype(F32)
        for t_ref in refs[1:n]:
            s = s + t_ref[...].astype(F32)
        refs[n][...] = s.astype(out_dtype)

    spec = pl.BlockSpec((br, Ccols), lambda i: (i, 0))
    return pl.pallas_call(
        body, name=name, grid=(R // br,), in_specs=[spec] * n, out_specs=spec, out_shape=SDS((R, Ccols), out_dtype),
        compiler_params=_cp(("arbitrary",), 40),
    )(*terms)


def _final_grad(sb, lb, chip, name):
    _, R, Ccols = sb.shape
    br = _blk(R, 512, BF16_ROWS)

    def body(chip_ref, sb_ref, l0, l1, l2, o_ref):
        s = sb_ref[0].astype(F32)
        for t_ref in (l0, l1, l2):
            s = s + t_ref[0].astype(F32)
        o_ref[...] = s

    lspec = lambda k: pl.BlockSpec((1, br, Ccols), lambda i, c: (k, i, 0))
    return pl.pallas_call(
        body, name=name,
        grid_spec=pltpu.PrefetchScalarGridSpec(
            num_scalar_prefetch=1, grid=(R // br,),
            in_specs=[pl.BlockSpec((1, br, Ccols), lambda i, c: (c[0], i, 0)), lspec(0), lspec(1), lspec(2)],
            out_specs=pl.BlockSpec((br, Ccols), lambda i, c: (i, 0))),
        out_shape=SDS((R, Ccols), F32),
        compiler_params=_cp(("arbitrary",), 40),
    )(chip, sb, lb, lb, lb)


def _adamw(w, g, m, v, name):
    R, Ccols = w.shape
    br = _blk(R, 256, SUBLANES)
    c1 = 1.0 - ADAM_B1 ** ADAM_STEP
    c2 = 1.0 - ADAM_B2 ** ADAM_STEP

    def body(w_ref, g_ref, m_ref, v_ref, d_ref, nm_ref, nv_ref):
        gv = g_ref[...]
        nm = ADAM_B1 * m_ref[...] + (1.0 - ADAM_B1) * gv
        nv = ADAM_B2 * v_ref[...] + (1.0 - ADAM_B2) * (gv * gv)
        nm_ref[...] = nm
        nv_ref[...] = nv
        d_ref[...] = -ADAM_LR * ((nm / c1) / (jnp.sqrt(nv / c2) + ADAM_EPS) + ADAM_WD * w_ref[...])

    spec = pl.BlockSpec((br, Ccols), lambda i: (i, 0))
    return pl.pallas_call(
        body, name=name, grid=(R // br,), in_specs=[spec] * 4, out_specs=[spec] * 3,
        out_shape=[SDS((R, Ccols), F32)] * 3, compiler_params=_cp(("arbitrary",), 40),
    )(w, g, m, v)


def _place():
    return lax.axis_index("x"), lax.axis_index("y"), lax.axis_index("c")


def _dev_rows(ref, dev, rows):
    return ref.at[pl.ds((4 * dev[0] + 2 * dev[1] + dev[2]) * rows, rows), :]


def _remote(src, dst, send_sem, recv_sem, to):
    return pltpu.make_async_remote_copy(src_ref=src, dst_ref=dst, send_sem=send_sem, recv_sem=recv_sem,
                                        device_id=to, device_id_type=MESH)


SAME_CORE_AND_SIBLING = ((0, 0, 1), (1, 0, 0), (0, 1, 0), (1, 1, 0))
ALL_OTHERS = SAME_CORE_AND_SIBLING + ((1, 0, 1), (0, 1, 1), (1, 1, 1))


def _merge_phases(a, b):
    na_in, na_out, na_sem = len(a.inputs), len(a.out_shapes), len(a.sem_shapes)

    def build(ins, outs, sems, starting):
        return (a.build(ins[:na_in], outs[:na_out], sems[:na_sem], starting)
                + b.build(ins[na_in:], outs[na_out:], sems[na_sem:], starting))

    aliases = dict(a.aliases)
    aliases.update({na_in + i: na_out + o for i, o in b.aliases.items()})
    return _Carried(a.inputs + b.inputs, a.out_shapes + b.out_shapes, aliases, a.sem_shapes + b.sem_shapes, build)


def _ag_direct_phase(slab, pieces, flips):
    W = slab.shape[1]
    n = len(pieces)
    npeer = len(flips)

    def build(ins, outs, sems, starting):
        (slab_ref,) = ins
        send_sems, recv_sems, local_sems = sems
        x, y, c = _place()
        me = (x, y, c)
        peers = [tuple(1 - v if f else v for v, f in zip(me, flip)) for flip in flips]
        todo = []
        for p, (off, rows) in enumerate(pieces):
            src = slab_ref.at[pl.ds(off, rows), :]
            mine = pltpu.make_async_copy(src, _dev_rows(outs[p], me, rows), local_sems.at[p])
            todo.append(mine.start if starting else mine.wait)
            for k, peer in enumerate(peers):
                snd = _remote(src, _dev_rows(outs[p], me, rows), send_sems.at[k, p], recv_sems.at[k, p], peer)
                if starting:
                    todo.append(snd.start)
                else:
                    theirs = _dev_rows(outs[p], peer, rows)
                    rcv = _remote(theirs, theirs, send_sems.at[k, p], recv_sems.at[k, p], me)
                    todo += [rcv.wait_recv, snd.wait_send]
        return todo

    dma = pltpu.SemaphoreType.DMA
    return _Carried([slab], [SDS((N_DEV * rows, W), slab.dtype) for _, rows in pieces], {},
                    [dma((npeer, n)), dma((npeer, n)), dma((n,))], build)


def _ag_forward_phase(gathered, pieces):
    n = len(pieces)

    def build(ins, outs, sems, starting):
        send_sems, recv_sems = sems
        x, y, c = _place()
        me, sibling = (x, y, c), (x, y, 1 - c)
        chips = [(1 - x, y), (x, 1 - y), (1 - x, 1 - y)]
        todo = []
        for p, (_, rows) in enumerate(pieces):
            for j, chip in enumerate(chips):
                snd = _remote(_dev_rows(ins[p], (*chip, c), rows), _dev_rows(outs[p], (*chip, c), rows),
                              send_sems.at[j, p], recv_sems.at[j, p], sibling)
                if starting:
                    todo.append(snd.start)
                else:
                    theirs = _dev_rows(outs[p], (*chip, 1 - c), rows)
                    rcv = _remote(theirs, theirs, send_sems.at[j, p], recv_sems.at[j, p], me)
                    todo += [rcv.wait_recv, snd.wait_send]
        return todo

    dma = pltpu.SemaphoreType.DMA
    return _Carried(gathered, [SDS(g.shape, g.dtype) for g in gathered], {p: p for p in range(n)},
                    [dma((3, n)), dma((3, n))], build)


def _rs_chips_phase(sb):
    _, R, W = sb.shape

    def build(ins, outs, sems, starting):
        (sb_ref,), (land_ref,) = ins, outs
        send_sems, recv_sems = sems
        x, y, c = _place()
        chips = [(1 - x, y), (x, 1 - y), (1 - x, 1 - y)]
        cps = [_remote(sb_ref.at[2 * chip[0] + chip[1]], land_ref.at[j], send_sems.at[j], recv_sems.at[j], (*chip, c))
               for j, chip in enumerate(chips)]
        if starting:
            return [cp.start for cp in cps]
        return [cp.wait_recv for cp in cps] + [cp.wait_send for cp in cps]

    dma = pltpu.SemaphoreType.DMA
    return _Carried([sb], [SDS((3, R, W), sb.dtype)], {}, [dma((3,)), dma((3,))], build)


def _allgather(slab, pieces, name):
    R, W = slab.shape
    n = len(pieces)
    assert sum(rows for _, rows in pieces) == R

    def body(slab_ref, *refs):
        outs = refs[:n]
        send_sems, recv_sems, local_sems = refs[n:]
        x, y, c = _place()
        me, sibling = (x, y, c), (x, y, 1 - c)
        chips = [(1 - x, y), (x, 1 - y), (1 - x, 1 - y)]

        def dst_rows(p, origin):
            rows = pieces[p][1]
            start = (4 * origin[0] + 2 * origin[1] + origin[2]) * rows
            return outs[p].at[pl.ds(start, rows), :]

        def copies(k, origin, to, from_slab):
            out = []
            for p, (off, rows) in enumerate(pieces):
                dst = dst_rows(p, origin)
                src = slab_ref.at[pl.ds(off, rows), :] if from_slab else dst
                out.append(pltpu.make_async_remote_copy(
                    src_ref=src, dst_ref=dst, send_sem=send_sems.at[k, p], recv_sem=recv_sems.at[k, p],
                    device_id=to, device_id_type=MESH))
            return out

        mine = [pltpu.make_async_copy(slab_ref.at[pl.ds(off, rows), :], dst_rows(p, me), local_sems.at[p])
                for p, (off, rows) in enumerate(pieces)]
        for cp in mine:
            cp.start()
        first = copies(0, me, sibling, True)
        for j, chip in enumerate(chips):
            first += copies(1 + j, me, (*chip, c), True)
        for cp in first:
            cp.start()
        passed = []
        for j, chip in enumerate(chips):
            for cp in copies(1 + j, (*chip, c), me, False):
                cp.wait_recv()
            fwd = copies(4 + j, (*chip, c), sibling, False)
            for cp in fwd:
                cp.start()
            passed += fwd
        for cp in copies(0, sibling, me, False):
            cp.wait_recv()
        for j, chip in enumerate(chips):
            for cp in copies(4 + j, (*chip, 1 - c), me, False):
                cp.wait_recv()
        for cp in first + passed:
            cp.wait_send()
        for cp in mine:
            cp.wait()

    return pl.pallas_call(
        body, name=name,
        in_specs=[HBM_SPEC], out_specs=[HBM_SPEC] * n,
        out_shape=[SDS((N_DEV * rows, W), slab.dtype) for _, rows in pieces],
        scratch_shapes=[pltpu.SemaphoreType.DMA((7, n)), pltpu.SemaphoreType.DMA((7, n)), pltpu.SemaphoreType.DMA((n,))],
    )(slab)


def _rs_sibling(grads, pieces, name):
    W = grads[0].shape[1]
    R = sum(rows for _, rows in pieces)
    n = len(pieces)
    dt = grads[0].dtype
    max_rows = max(rows for _, rows in pieces)
    steps = [(q, p) for q in range(N_CHIP) for p in range(n)]
    ns = len(steps)
    ADD_ROWS = 64
    assert all(rows % ADD_ROWS == 0 for _, rows in pieces)

    def body(*refs):
        g_refs = refs[:n]
        sb_ref, mine_buf, send_buf, land_buf, out_buf, in_sems, out_sems, send_sems, recv_sems, credit = refs[n:]
        x, y, c = _place()
        sibling = (x, y, 1 - c)

        def loads(s):
            q, p = steps[s]
            rows = pieces[p][1]
            slot = s % 2
            mine = g_refs[p].at[pl.ds((2 * q + c) * rows, rows), :]
            theirs = g_refs[p].at[pl.ds((2 * q + 1 - c) * rows, rows), :]
            return (pltpu.make_async_copy(mine, mine_buf.at[slot, pl.ds(0, rows), :], in_sems.at[slot, 0]),
                    pltpu.make_async_copy(theirs, send_buf.at[slot, pl.ds(0, rows), :], in_sems.at[slot, 1]))

        def send(s):
            rows = pieces[steps[s][1]][1]
            slot = s % 2
            return pltpu.make_async_remote_copy(
                src_ref=send_buf.at[slot, pl.ds(0, rows), :], dst_ref=land_buf.at[slot, pl.ds(0, rows), :],
                send_sem=send_sems.at[slot], recv_sem=recv_sems.at[slot], device_id=sibling, device_id_type=MESH)

        def store(s):
            q, p = steps[s]
            off, rows = pieces[p]
            slot = s % 2
            return pltpu.make_async_copy(out_buf.at[slot, pl.ds(0, rows), :], sb_ref.at[q, pl.ds(off, rows), :],
                                         out_sems.at[slot])

        def start_send(s):
            for cp in loads(s):
                cp.wait()
            if s >= 2:
                pl.semaphore_wait(credit.at[s % 2], 1)
            send(s).start()

        for s in range(min(2, ns)):
            for cp in loads(s):
                cp.start()
        start_send(0)
        for s in range(ns):
            slot = s % 2
            rows = pieces[steps[s][1]][1]
            if s + 1 < ns:
                start_send(s + 1)
            send(s).wait_recv()
            if s >= 2:
                store(s - 2).wait()

            def add(k, _, slot=slot):
                r = pl.ds(pl.multiple_of(k * ADD_ROWS, ADD_ROWS), ADD_ROWS)
                out_buf[slot, r, :] = (mine_buf[slot, r, :].astype(F32) + land_buf[slot, r, :].astype(F32)).astype(dt)
                return 0
            lax.fori_loop(0, rows // ADD_ROWS, add, 0)
            if s + 2 < ns:
                pl.semaphore_signal(credit.at[slot], inc=1, device_id=sibling, device_id_type=MESH)
            store(s).start()
            send(s).wait_send()
            if s + 2 < ns:
                for cp in loads(s + 2):
                    cp.start()
        for s in range(max(ns - 2, 0), ns):
            store(s).wait()

    buf = pltpu.VMEM((2, max_rows, W), dt)
    return pl.pallas_call(
        body, name=name,
        in_specs=[HBM_SPEC] * n, out_specs=HBM_SPEC,
        out_shape=SDS((N_CHIP, R, W), dt),
        scratch_shapes=[buf, buf, buf, buf, pltpu.SemaphoreType.DMA((2, 2)), pltpu.SemaphoreType.DMA((2,)),
                        pltpu.SemaphoreType.DMA((2,)), pltpu.SemaphoreType.DMA((2,)), pltpu.SemaphoreType.REGULAR((2,))],
        compiler_params=pltpu.CompilerParams(vmem_limit_bytes=40 * MIB),
    )(*grads)


SMALL_NAMES = ("ffn1_norm", "mix_norm", "ffn2_norm", "final_norm", "lru_conv_w", "lru_conv_b", "lru_w_a", "lru_b_a",
               "lru_w_i", "lru_b_i", "lru_lambda", "sc_conv_w", "lru_out_norm", "sc_out_norm")
WEIGHT_NAMES = ("ffn1_norm", "ffn1_w_gate", "ffn1_w_up", "ffn1_w_down", "mix_norm", "w_in", "lru_conv_w", "lru_conv_b",
                "lru_w_a", "lru_b_a", "lru_w_i", "lru_b_i", "lru_lambda", "sc_conv_w", "lru_out_norm", "sc_out_norm",
                "w_out", "ffn2_norm", "ffn2_w_gate", "ffn2_w_up", "ffn2_w_down", "final_norm")
BIG = (("ffn1_w_gate", True), ("ffn1_w_up", True), ("ffn1_w_down", False), ("ffn2_w_gate", True), ("ffn2_w_up", True),
       ("ffn2_w_down", False), ("w_in", True), ("w_out", False))


SLAB_ROW_ALIGN = 256


def _pack_rows(parts, width):
    rows, counts = [], []
    for p in parts:
        flat = p.reshape(-1)
        nr = -(-flat.shape[0] // width)
        nr = -(-nr // SUBLANES) * SUBLANES
        rows.append(jnp.pad(flat, (0, nr * width - flat.shape[0])).reshape(nr, width))
        counts.append(nr)
    total = sum(counts)
    pad = -(-total // SLAB_ROW_ALIGN) * SLAB_ROW_ALIGN - total
    if pad:
        rows.append(jnp.zeros((pad, width), rows[0].dtype))
    return jnp.concatenate(rows, axis=0), counts


def _stack_rows(blocks):
    pieces, off = [], 0
    for b in blocks:
        pieces.append((off, b.shape[0]))
        off += b.shape[0]
    return jnp.concatenate(blocks, axis=0), pieces


def _unpack_rows(slab, counts, shapes):
    out, r = [], 0
    for nr, shape in zip(counts, shapes):
        size = math.prod(shape)
        out.append(slab[r:r + nr].reshape(-1)[:size].reshape(shape))
        r += nr
    return out


def kernel(x, ffn1_norm, ffn1_w_gate, ffn1_w_up, ffn1_w_down, mix_norm, w_in, lru_conv_w, lru_conv_b, lru_w_a, lru_b_a, lru_w_i, lru_b_i, lru_lambda, sc_conv_w, lru_out_norm, sc_out_norm, w_out, ffn2_norm, ffn2_w_gate, ffn2_w_up, ffn2_w_down, final_norm, loss_target, m_ffn1_norm, m_ffn1_w_gate, m_ffn1_w_up, m_ffn1_w_down, m_mix_norm, m_w_in, m_lru_conv_w, m_lru_conv_b, m_lru_w_a, m_lru_b_a, m_lru_w_i, m_lru_b_i, m_lru_lambda, m_sc_conv_w, m_lru_out_norm, m_sc_out_norm, m_w_out, m_ffn2_norm, m_ffn2_w_gate, m_ffn2_w_up, m_ffn2_w_down, m_final_norm, v_ffn1_norm, v_ffn1_w_gate, v_ffn1_w_up, v_ffn1_w_down, v_mix_norm, v_w_in, v_lru_conv_w, v_lru_conv_b, v_lru_w_a, v_lru_b_a, v_lru_w_i, v_lru_b_i, v_lru_lambda, v_sc_conv_w, v_lru_out_norm, v_sc_out_norm, v_w_out, v_ffn2_norm, v_ffn2_w_gate, v_ffn2_w_up, v_ffn2_w_down, v_final_norm):
    a = dict(locals())
    w = {n: a[n] for n in WEIGHT_NAMES}
    m = {n: a["m_" + n] for n in WEIGHT_NAMES}
    v = {n: a["v_" + n] for n in WEIGHT_NAMES}
    ax, ay, ac = _place()
    dev = 4 * ax + 2 * ay + ac
    chip = (2 * ax + ay).astype(jnp.int32).reshape(1)

    x0 = x[0]
    tgt = loss_target[0]
    T, D = x0.shape
    C = D // 2
    H, hd = lru_w_a.shape[1], lru_w_a.shape[2]
    CL = lru_conv_w.shape[2]

    shards = []
    for name, transposed in BIG:
        s = w[name][0]
        shards.append((s.T if transposed else s).astype(BF16))
    taps = jnp.concatenate([lru_conv_w[0], sc_conv_w[0], jnp.zeros((1, CL), F32)], axis=0)
    taps_row = lax.bitcast_convert_type(taps, BF16).reshape(1, -1)
    taps_blk = jnp.pad(taps_row, ((0, BF16_ROWS - 1), (0, D - taps_row.shape[1])))
    s_wg1, s_wu1, s_wd1, s_wg2, s_wu2, s_wd2, s_win, s_wout = shards
    slab_gu1, pcs_gu1 = _stack_rows([s_wg1, s_wu1])
    slab_d1, pcs_d1 = _stack_rows([s_wd1])
    slab_mw, pcs_mw = _stack_rows([s_win, s_wout, taps_blk])
    slab_gu2, pcs_gu2 = _stack_rows([s_wg2, s_wu2])
    slab_d2, pcs_d2 = _stack_rows([s_wd2])
    wg1, wu1 = _allgather(slab_gu1, pcs_gu1, "allgather_ffn1_up")

    g1, gm, g3 = ffn1_norm, mix_norm, ffn2_norm
    phase = _merge_phases(_ag_direct_phase(slab_d1, pcs_d1, ALL_OTHERS),
                          _ag_direct_phase(slab_mw, pcs_mw, SAME_CORE_AND_SIBLING))
    (n1, hg1, hu1, act1), got = _norm_proj(x0, g1, [wg1, wu1], [BF16, BF16], True, "ffn1_up", carried=phase)
    wd1, mixw = got[0], got[1:]
    phase = _merge_phases(_ag_forward_phase(mixw, pcs_mw), _ag_direct_phase(slab_gu2, pcs_gu2, SAME_CORE_AND_SIBLING))
    x1, got = _mm_res(act1, wd1, x0, 0.5, "ffn1_down", carried=phase)
    (win, wout, taps_all), gu2 = got[:3], got[3:]
    phase = _merge_phases(_ag_forward_phase(gu2, pcs_gu2), _ag_direct_phase(slab_d2, pcs_d2, SAME_CORE_AND_SIBLING))
    (n2, z), got = _norm_proj(x1, gm, [win], [F32], False, "in_proj", carried=phase)
    (wg2, wu2), d2 = got[:2], got[2:]
    taps_all = taps_all.reshape(N_DEV, BF16_ROWS, D)[:, 0, :2 * SUBLANES * CL].reshape(N_DEV, SUBLANES, CL, 2)
    taps_all = lax.bitcast_convert_type(taps_all, F32)
    taps_all = taps_all.transpose(1, 0, 2).reshape(SUBLANES, N_DEV * CL)
    cw, sw = taps_all[0:4], taps_all[4:7]

    gf = final_norm.reshape(1, D)
    cb = lru_conv_b
    wa, wi = lru_w_a[0].astype(BF16), lru_w_i[0].astype(BF16)
    ba, bi = lru_b_a.reshape(1, C), lru_b_i.reshape(1, C)
    lam, glo, gso = lru_lambda, lru_out_norm, sc_out_norm

    y, h = _mix_fwd(z, cw, cb, wa, ba, wi, bi, lam, sw, glo, gso, "mix_fwd")
    x2, (wd2,) = _mm_res(y, wout, x1, 1.0, "out_proj", carried=_ag_forward_phase(d2, pcs_d2))
    n3, hg2, hu2, act2 = _norm_proj(x2, g3, [wg2, wu2], [BF16, BF16], True, "ffn2_up")
    x3 = _mm_res(act2, wd2, x2, 0.5, "ffn2_down")
    dx3, df2, d_gf, loss_blk = _loss_head(x3, gf, tgt, "loss_head")

    F = wd1.shape[0]
    bm_f = F // 4 if (F // 4) % LANES == 0 else 512

    def reduce_group(gs, tag):
        pcs, off = [], 0
        for g_ in gs:
            pcs.append((off, g_.shape[0] // N_DEV))
            off += g_.shape[0] // N_DEV
        sb_ = _rs_sibling(gs, pcs, "rs_sibling_add_" + tag)
        return sb_, pcs

    dhg2, dhu2 = _ffn_bwd_act(df2, wd2, hg2, hu2, "ffn2_bwd_act")
    d_wd2 = _dw_tn(act2, df2, bm_f, "ffn2_dw_down")
    d_wg2 = _dw_tn(dhg2, n3, bm_f, "ffn2_dw_gate")
    d_wu2 = _dw_tn(dhu2, n3, bm_f, "ffn2_dw_up")
    sb_f2, pcs_f2 = reduce_group([d_wg2, d_wu2, d_wd2], "ffn2")
    (dx2, dx2b, d_g3), (lb_f2,) = _mm_rmsbwd([(dhg2, wg2), (dhu2, wu2)], x2, g3, dx3, 1.0, "ffn2_bwd_in",
                                             carried=_rs_chips_phase(sb_f2))
    dy = _mm_nt(dx2b, wout, "out_proj_bwd")
    d_wout = _dw_tn(y, dx2b, 1024, "out_proj_dw")
    dz, small, d_wa, d_wi = _mix_bwd(z, h, dy, cw, cb, wa, ba, wi, bi, lam, sw, glo, gso, "mix_bwd")
    d_win = _dw_tn(dz, n2, 1280, "in_proj_dw")
    sb_mx, pcs_mx = reduce_group([d_win, d_wout], "mix")
    (dx1, df1, d_gm), (lb_mx,) = _mm_rmsbwd([(dz, win)], x1, gm, dx2, 0.5, "in_proj_bwd",
                                            carried=_rs_chips_phase(sb_mx))
    dhg1, dhu1 = _ffn_bwd_act(df1, wd1, hg1, hu1, "ffn1_bwd_act")
    d_wd1 = _dw_tn(act1, df1, bm_f, "ffn1_dw_down")
    d_wg1 = _dw_tn(dhg1, n1, bm_f, "ffn1_dw_gate")
    d_wu1 = _dw_tn(dhu1, n1, bm_f, "ffn1_dw_up")
    sb_f1, pcs_f1 = reduce_group([d_wg1, d_wu1, d_wd1], "ffn1")
    (dx0, _, d_g1), (lb_f1,) = _mm_rmsbwd([(dhg1, wg1), (dhu1, wu1)], x0, g1, dx1, 1.0, "ffn1_bwd_in",
                                          carried=_rs_chips_phase(sb_f1))

    big_sum = {}
    for tag, names, sb_, lb_, pcs in (("ffn2", ("ffn2_w_gate", "ffn2_w_up", "ffn2_w_down"), sb_f2, lb_f2, pcs_f2),
                                      ("mix", ("w_in", "w_out"), sb_mx, lb_mx, pcs_mx),
                                      ("ffn1", ("ffn1_w_gate", "ffn1_w_up", "ffn1_w_down"), sb_f1, lb_f1, pcs_f1)):
        gsum = _final_grad(sb_, lb_, chip, "rs_final_sum_" + tag)
        for name, (off, rows) in zip(names, pcs):
            big_sum[name] = gsum[off:off + rows]

    small_parts = [d_g1, d_gm, d_g3, d_gf, small[R_CW:R_CW + 4], small[R_CB], d_wa, small[R_BA], d_wi, small[R_BI],
                   small[R_LAM], small[R_SW:R_SW + 3], small[R_GLO], small[R_GSO]]
    sslab, counts = _pack_rows(small_parts, LANES)
    RS = sslab.shape[0]
    (sg,) = _allgather(sslab, [(0, RS)], "allgather_small_grads")
    ssum = _add_slabs([sg[j * RS:(j + 1) * RS] for j in range(N_DEV)], F32, "small_grads_sum")
    full_shapes = [(1, D), (1, D), (1, D), (D,), (1, 4, C), (1, C), (1, H, hd, hd), (1, H, hd), (1, H, hd, hd), (1, H, hd),
                   (1, C), (1, 3, C), (1, C), (1, C)]
    small_full = dict(zip(SMALL_NAMES, _unpack_rows(ssum, counts, full_shapes)))

    grads = {}
    for name, transposed in BIG:
        gblk = big_sum[name]
        grads[name] = (gblk.T if transposed else gblk)[None]
    for name in SMALL_NAMES:
        gfull = small_full[name]
        if name in ("lru_conv_w", "sc_conv_w"):
            gfull = lax.dynamic_slice_in_dim(gfull, dev * CL, CL, axis=2)
        grads[name] = gfull

    delta, new_m, new_v = {}, {}, {}
    for name, transposed in BIG:
        flip = transposed and w[name].shape[2] % LANES != 0
        view = (lambda t: t[0].T) if flip else (lambda t: t[0])
        back = (lambda t: t.T[None]) if flip else (lambda t: t[None])
        gview = big_sum[name] if flip else grads[name][0]
        d_, m_, v_ = _adamw(view(w[name]), gview, view(m[name]), view(v[name]), "adamw_" + name)
        delta[name], new_m[name], new_v[name] = back(d_), back(m_), back(v_)
    packs = [_pack_rows([t[n_] for n_ in SMALL_NAMES], LANES) for t in (w, grads, m, v)]
    sd, sm, sv = _adamw(packs[0][0], packs[1][0], packs[2][0], packs[3][0], "adamw_small")
    shapes = [w[n_].shape for n_ in SMALL_NAMES]
    for tgt_dict, slab_ in ((delta, sd), (new_m, sm), (new_v, sv)):
        for n_, val in zip(SMALL_NAMES, _unpack_rows(slab_, packs[0][1], shapes)):
            tgt_dict[n_] = val

    loss = lax.psum(loss_blk[0, 0], ("x", "y", "c"))
    return (loss, dx0[None], *[grads[n_] for n_ in WEIGHT_NAMES], *[delta[n_] for n_ in WEIGHT_NAMES],
            *[new_m[n_] for n_ in WEIGHT_NAMES], *[new_v[n_] for n_ in WEIGHT_NAMES])
```

```python
import functools
import math

import jax
import jax.numpy as jnp
from jax import lax
from jax.experimental import pallas as pl
from jax.experimental.pallas import tpu as pltpu

F32 = jnp.float32
BF16 = jnp.bfloat16
SDS = jax.ShapeDtypeStruct
MESH = pl.DeviceIdType.MESH

NORM_EPS = 1e-6
LRU_C = 8.0
N_DEV = 8
N_CHIP = 4
ADAM_LR, ADAM_B1, ADAM_B2, ADAM_EPS, ADAM_WD, ADAM_STEP = 0.001, 0.9, 0.999, 1e-08, 0.01, 10

NN = (((1,), (0,)), ((), ()))
NT = (((1,), (1,)), ((), ()))
TN = (((0,), (0,)), ((), ()))

SUBLANES = 8
BF16_ROWS = 16
LANES = 128
MIB = 1 << 20


def _dot(a, b, dims):
    return lax.dot_general(a, b, dims, preferred_element_type=F32)


def _blk(n, pref, align):
    if n <= pref:
        return n
    b = (pref // align) * align
    while b >= align:
        if n % b == 0:
            return b
        b -= align
    raise ValueError(f"no block of {n} aligned to {align} under {pref}")


def _cp(sem, vmem_mib):
    return pltpu.CompilerParams(dimension_semantics=sem, vmem_limit_bytes=vmem_mib * MIB)


HBM_SPEC = pl.BlockSpec(memory_space=pltpu.HBM)
MID_EIGHTHS = 5


class _Carried:
    def __init__(self, inputs, out_shapes, aliases, sem_shapes, build, has_mid=False):
        self.inputs, self.out_shapes, self.aliases = list(inputs), list(out_shapes), dict(aliases)
        self.sem_shapes, self.build, self.has_mid = list(sem_shapes), build, has_mid


def _call(body, *, name, grid, in_specs, out_specs, out_shape, scratch_shapes, compiler_params, args, carried=None):
    if carried is None:
        return pl.pallas_call(body, name=name, grid=grid, in_specs=in_specs, out_specs=out_specs, out_shape=out_shape,
                              scratch_shapes=scratch_shapes, compiler_params=compiler_params)(*args)
    n_in, n_out, n_sc = len(in_specs), len(out_shape), len(scratch_shapes)
    c_in, c_out = len(carried.inputs), len(carried.out_shapes)

    def hosted(*refs):
        ins, refs = refs[:n_in], refs[n_in:]
        c_ins, refs = refs[:c_in], refs[c_in:]
        outs, refs = refs[:n_out], refs[n_out:]
        c_outs, refs = refs[:c_out], refs[c_out:]
        scratch, c_sems = refs[:n_sc], refs[n_sc:]
        first = functools.reduce(jnp.logical_and, [pl.program_id(a) == 0 for a in range(len(grid))])
        last = functools.reduce(jnp.logical_and, [pl.program_id(a) == g - 1 for a, g in enumerate(grid)])

        @pl.when(first)
        def _():
            for start in carried.build(c_ins, c_outs, c_sems, "start"):
                start()

        if carried.has_mid:
            mid = functools.reduce(jnp.logical_and, [pl.program_id(0) == (grid[0] * MID_EIGHTHS) // 8]
                                   + [pl.program_id(a) == 0 for a in range(1, len(grid))])

            @pl.when(mid)
            def _():
                for step in carried.build(c_ins, c_outs, c_sems, "mid"):
                    step()

        body(*ins, *outs, *scratch)

        @pl.when(last)
        def _():
            for wait in carried.build(c_ins, c_outs, c_sems, "end"):
                wait()

    out = pl.pallas_call(
        hosted, name=name, grid=grid, in_specs=list(in_specs) + [HBM_SPEC] * c_in,
        out_specs=list(out_specs) + [HBM_SPEC] * c_out, out_shape=list(out_shape) + carried.out_shapes,
        scratch_shapes=list(scratch_shapes) + carried.sem_shapes,
        input_output_aliases={n_in + a: n_out + b for a, b in carried.aliases.items()},
        compiler_params=compiler_params)(*args, *carried.inputs)
    return out[:n_out], out[n_out:]


ROW_CHUNK = 128


EPILOGUE_ROWS = 128


def _chunk_rows(c):
    return pl.ds(pl.multiple_of(c * ROW_CHUNK, ROW_CHUNK), ROW_CHUNK)


def _rstd(xv):
    return lax.rsqrt(jnp.mean(xv * xv, axis=-1, keepdims=True) + NORM_EPS)


def _rms_bwd(xv, g, dn):
    r = _rstd(xv)
    xr = xv * r
    gd = g * dn
    dx = r * (gd - xr * jnp.mean(gd * xr, axis=-1, keepdims=True))
    return dx, jnp.sum(dn * xr, axis=0, keepdims=True)


def _log1p(e):
    u = 1.0 + e
    return jnp.where(u == 1.0, e, jnp.log(u) * (e / (u - 1.0)))


def _one_minus_exp(v, exp_half_v):
    series = 1.0 / 5040.0
    for coeff in (1.0 / 720.0, 1.0 / 120.0, 1.0 / 24.0, 1.0 / 6.0, 0.5, 1.0):
        series = series * v + coeff
    return jnp.where(v > -0.5, -v * series, 1.0 - exp_half_v * exp_half_v)


def _sigmoid(v):
    return 0.5 * jnp.tanh(0.5 * v) + 0.5


def _gelu_parts(g):
    k0 = math.sqrt(2.0 / math.pi)
    g2 = g * g
    t = jnp.tanh(k0 * (g + 0.044715 * g * g2))
    gel = 0.5 * g * (1.0 + t)
    gelp = 0.5 * (1.0 + t) + 0.5 * g * (1.0 - t * t) * (k0 * (1.0 + 3.0 * 0.044715 * g2))
    return gel, gelp


def _norm_proj(x, gain, w_list, out_dtypes, swiglu, name, carried=None):
    T, D = x.shape
    N = w_list[0].shape[0]
    nw = len(w_list)
    bm = _blk(T, 1024, BF16_ROWS)
    bn = _blk(N, 512, LANES)

    def body(*refs):
        x_ref, g_ref = refs[:2]
        w_refs = refs[2:2 + nw]
        n_ref = refs[2 + nw]
        o_refs = refs[3 + nw:3 + 2 * nw]
        act_ref = refs[3 + 2 * nw] if swiglu else None
        n_sc = refs[-1]

        @pl.when(pl.program_id(1) == 0)
        def _():
            def piece(p, _):
                r = pl.ds(pl.multiple_of(p * EPILOGUE_ROWS, EPILOGUE_ROWS), EPILOGUE_ROWS)
                xv = x_ref[r, :]
                nb = (xv * _rstd(xv) * g_ref[...]).astype(BF16)
                n_sc[r, :] = nb
                n_ref[r, :] = nb
                return 0
            lax.fori_loop(0, bm // EPILOGUE_ROWS, piece, 0)

        n = n_sc[...]
        outs = [_dot(n, w_ref[...], NT) for w_ref in w_refs]
        for o_ref, o in zip(o_refs, outs):
            o_ref[...] = o.astype(o_ref.dtype)
        if swiglu:
            hg, hu = outs
            act_ref[...] = (hg * _sigmoid(hg) * hu).astype(BF16)

    row = pl.BlockSpec((bm, D), lambda i, j: (i, 0))
    tile = pl.BlockSpec((bm, bn), lambda i, j: (i, j))
    n_extra = 1 if swiglu else 0
    return _call(
        body, name=name, grid=(T // bm, N // bn),
        in_specs=[row, pl.BlockSpec((1, D), lambda i, j: (0, 0))] + [pl.BlockSpec((bn, D), lambda i, j: (j, 0))] * nw,
        out_specs=[row] + [tile] * (nw + n_extra),
        out_shape=[SDS((T, D), BF16)] + [SDS((T, N), dt) for dt in out_dtypes] + [SDS((T, N), BF16)] * n_extra,
        scratch_shapes=[pltpu.VMEM((bm, D), BF16)],
        compiler_params=_cp(("arbitrary", "arbitrary"), 52),
        args=(x, gain, *w_list), carried=carried)


def _up_act(n, wu, hg, name, carried=None):
    T, D = n.shape
    F = wu.shape[0]
    bm = _blk(T, 1024, BF16_ROWS)
    bn = _blk(F, 512, LANES)

    def body(n_ref, wu_ref, hg_ref, hu_ref, act_ref):
        hu = _dot(n_ref[...], wu_ref[...], NT)
        hg = hg_ref[...].astype(F32)
        hu_ref[...] = hu.astype(BF16)
        act_ref[...] = (hg * _sigmoid(hg) * hu).astype(BF16)

    tile = pl.BlockSpec((bm, bn), lambda i, j: (i, j))
    return _call(
        body, name=name, grid=(T // bm, F // bn),
        in_specs=[pl.BlockSpec((bm, D), lambda i, j: (i, 0)), pl.BlockSpec((bn, D), lambda i, j: (j, 0)), tile],
        out_specs=[tile, tile], out_shape=[SDS((T, F), BF16)] * 2, scratch_shapes=[],
        compiler_params=_cp(("arbitrary", "arbitrary"), 40),
        args=(n, wu, hg), carried=carried)


def _mm_res(a, b, x, scale, name, carried=None):
    T, K = a.shape
    D = b.shape[1]
    bm = _blk(T, 1024, BF16_ROWS)
    bk = _blk(K, 1408, LANES)
    nk = K // bk

    def body(a_ref, b_ref, x_ref, o_ref):
        k = pl.program_id(1)

        @pl.when(k == 0)
        def _():
            o_ref[...] = jnp.zeros_like(o_ref)

        o_ref[...] += _dot(a_ref[...], b_ref[...], NN)

        @pl.when(k == nk - 1)
        def _():
            def chunk(c, _):
                r = _chunk_rows(c)
                o_ref[r, :] = x_ref[r, :] + scale * o_ref[r, :]
                return 0
            lax.fori_loop(0, bm // ROW_CHUNK, chunk, 0)

    row = pl.BlockSpec((bm, D), lambda i, k: (i, 0))
    out = _call(
        body, name=name, grid=(T // bm, nk),
        in_specs=[pl.BlockSpec((bm, bk), lambda i, k: (i, k)), pl.BlockSpec((bk, D), lambda i, k: (k, 0)), row],
        out_specs=[row], out_shape=[SDS((T, D), F32)], scratch_shapes=[],
        compiler_params=_cp(("arbitrary", "arbitrary"), 56),
        args=(a, b, x), carried=carried)
    return out[0] if carried is None else (out[0][0], out[1])


def _mm_nt(a, b, name):
    T, K = a.shape
    N = b.shape[0]
    bm = _blk(T, 1024, BF16_ROWS)
    bn = _blk(N, 512, LANES)

    def body(a_ref, b_ref, o_ref):
        o_ref[...] = _dot(a_ref[...], b_ref[...], NT)

    return pl.pallas_call(
        body, name=name, grid=(T // bm, N // bn),
        in_specs=[pl.BlockSpec((bm, K), lambda i, j: (i, 0)), pl.BlockSpec((bn, K), lambda i, j: (j, 0))],
        out_specs=pl.BlockSpec((bm, bn), lambda i, j: (i, j)), out_shape=SDS((T, N), F32),
        compiler_params=_cp(("arbitrary", "arbitrary"), 40),
    )(a, b)


def _ffn_bwd_act(dfb, wd, hg, hu, name):
    T, D = dfb.shape
    F = wd.shape[0]
    bm = _blk(T, 1024, BF16_ROWS)
    bn = _blk(F, 512, LANES)

    def body(df_ref, wd_ref, hg_ref, hu_ref, dhg_ref, dhu_ref):
        dact = _dot(df_ref[...], wd_ref[...], NT)
        hgv = hg_ref[...].astype(F32)
        huv = hu_ref[...].astype(F32)
        s = _sigmoid(hgv)
        dhu_ref[...] = (dact * (hgv * s)).astype(BF16)
        dhg_ref[...] = (dact * huv * (s * (1.0 + hgv * (1.0 - s)))).astype(BF16)

    tile = pl.BlockSpec((bm, bn), lambda i, j: (i, j))
    return pl.pallas_call(
        body, name=name, grid=(T // bm, F // bn),
        in_specs=[pl.BlockSpec((bm, D), lambda i, j: (i, 0)), pl.BlockSpec((bn, D), lambda i, j: (j, 0)), tile, tile],
        out_specs=[tile, tile], out_shape=[SDS((T, F), BF16)] * 2,
        compiler_params=_cp(("arbitrary", "arbitrary"), 40),
    )(dfb, wd, hg, hu)


def _dw_tn(a, b, bm_pref, name):
    T, M = a.shape
    N = b.shape[1]
    bm = _blk(M, bm_pref, LANES)
    tk = _blk(T, 1024, BF16_ROWS)
    nk = T // tk

    def body(a_ref, b_ref, o_ref, acc):
        k = pl.program_id(1)

        @pl.when(k == 0)
        def _():
            acc[...] = jnp.zeros_like(acc)

        acc[...] += _dot(a_ref[...], b_ref[...], TN)

        @pl.when(k == nk - 1)
        def _():
            o_ref[...] = acc[...].astype(BF16)

    return pl.pallas_call(
        body, name=name, grid=(M // bm, nk),
        in_specs=[pl.BlockSpec((tk, bm), lambda i, k: (k, i)), pl.BlockSpec((tk, N), lambda i, k: (k, 0))],
        out_specs=pl.BlockSpec((bm, N), lambda i, k: (i, 0)), out_shape=SDS((M, N), BF16),
        scratch_shapes=[pltpu.VMEM((bm, N), F32)],
        compiler_params=_cp(("arbitrary", "arbitrary"), 48),
    )(a, b)


def _mm_rmsbwd(pairs, x, gain, dx_in, bscale, name, carried=None):
    T, D = x.shape
    K = pairs[0][0].shape[1]
    npair = len(pairs)
    bm = _blk(T, 1024, BF16_ROWS)
    bk = _blk(K, 1024 // npair, LANES)
    nk = K // bk

    nchunk = bm // ROW_CHUNK

    def body(*refs):
        ab = refs[:2 * npair]
        x_hbm, g_ref, dxin_hbm, dx_ref, dxb_ref, dg_ref, x_buf, dxin_buf, sems = refs[2 * npair:]
        i = pl.program_id(0)
        k = pl.program_id(1)

        def fetch(c, slot):
            rows = pl.ds(i * bm + c * ROW_CHUNK, ROW_CHUNK)
            return (pltpu.make_async_copy(x_hbm.at[rows, :], x_buf.at[slot], sems.at[slot, 0]),
                    pltpu.make_async_copy(dxin_hbm.at[rows, :], dxin_buf.at[slot], sems.at[slot, 1]))

        @pl.when(k == 0)
        def _():
            dx_ref[...] = jnp.zeros_like(dx_ref)

        @pl.when(k == nk - 1)
        def _():
            for cp in fetch(0, 0):
                cp.start()

        for q in range(npair):
            dx_ref[...] += _dot(ab[2 * q][...], ab[2 * q + 1][...], NN)

        @pl.when(k == nk - 1)
        def _():
            @pl.when(i == 0)
            def _():
                dg_ref[...] = jnp.zeros_like(dg_ref)

            def chunk(c, _):
                slot = c % 2

                @pl.when(c + 1 < nchunk)
                def _():
                    for cp in fetch(c + 1, 1 - slot):
                        cp.start()

                for cp in fetch(c, slot):
                    cp.wait()

                def piece(p, _):
                    rb = pl.ds(pl.multiple_of(p * EPILOGUE_ROWS, EPILOGUE_ROWS), EPILOGUE_ROWS)
                    r = pl.ds(pl.multiple_of(c * ROW_CHUNK + p * EPILOGUE_ROWS, EPILOGUE_ROWS), EPILOGUE_ROWS)
                    dx, dg = _rms_bwd(x_buf[slot, rb, :], g_ref[...], dx_ref[r, :])
                    dxo = dxin_buf[slot, rb, :] + dx
                    dx_ref[r, :] = dxo
                    dxb_ref[r, :] = (bscale * dxo).astype(BF16)
                    dg_ref[...] += dg
                    return 0
                lax.fori_loop(0, ROW_CHUNK // EPILOGUE_ROWS, piece, 0)
                return 0
            lax.fori_loop(0, nchunk, chunk, 0)

    row = pl.BlockSpec((bm, D), lambda i, k: (i, 0))
    anywhere = pl.BlockSpec(memory_space=pl.ANY)
    vec = pl.BlockSpec((1, D), lambda i, k: (0, 0))
    in_specs = []
    args = []
    for a, b in pairs:
        in_specs += [pl.BlockSpec((bm, bk), lambda i, k: (i, k)), pl.BlockSpec((bk, D), lambda i, k: (k, 0))]
        args += [a, b]
    return _call(
        body, name=name, grid=(T // bm, nk),
        in_specs=in_specs + [anywhere, vec, anywhere], out_specs=[row, row, vec],
        out_shape=[SDS((T, D), F32), SDS((T, D), BF16), SDS((1, D), F32)],
        scratch_shapes=[pltpu.VMEM((2, ROW_CHUNK, D), F32), pltpu.VMEM((2, ROW_CHUNK, D), F32),
                        pltpu.SemaphoreType.DMA((2, 2))],
        compiler_params=_cp(("arbitrary", "arbitrary"), 52),
        args=(*args, x, gain, dx_in), carried=carried)


def _loss_head(x3, gain, tgt, name):
    T, D = x3.shape
    bm = _blk(T, 256, BF16_ROWS)

    def body(x_ref, g_ref, t_ref, dx_ref, dxb_ref, dg_ref, loss_ref):
        i = pl.program_id(0)
        xv = x_ref[...]
        g = g_ref[...]
        out = xv * _rstd(xv) * g
        e = out - t_ref[...]
        part = 0.5 * jnp.sum(jnp.mean(e * e, axis=-1, keepdims=True), axis=0, keepdims=True)
        dx, dg = _rms_bwd(xv, g, e * (1.0 / D))
        dx_ref[...] = dx
        dxb_ref[...] = (0.5 * dx).astype(BF16)

        @pl.when(i == 0)
        def _():
            dg_ref[...] = dg
            loss_ref[...] = jnp.broadcast_to(part, loss_ref.shape)

        @pl.when(i > 0)
        def _():
            dg_ref[...] += dg
            loss_ref[...] += jnp.broadcast_to(part, loss_ref.shape)

    row = pl.BlockSpec((bm, D), lambda i: (i, 0))
    vec = pl.BlockSpec((1, D), lambda i: (0, 0))
    return pl.pallas_call(
        body, name=name, grid=(T // bm,),
        in_specs=[row, vec, row], out_specs=[row, row, vec, pl.BlockSpec((SUBLANES, LANES), lambda i: (0, 0))],
        out_shape=[SDS((T, D), F32), SDS((T, D), BF16), SDS((1, D), F32), SDS((SUBLANES, LANES), F32)],
        compiler_params=_cp(("arbitrary",), 40),
    )(x3, gain, tgt)


R_CW, R_CB, R_BA, R_BI, R_LAM, R_SW, R_GLO, R_GSO, SMALL_ROWS = 0, 4, 5, 6, 7, 8, 11, 12, 16


def _rows(g):
    return pl.ds(pl.multiple_of(g * SUBLANES, SUBLANES), SUBLANES)


def _shift_back(prev, cur, d):
    row = lax.broadcasted_iota(jnp.int32, cur.shape, 0)
    return pltpu.roll(jnp.where(row >= SUBLANES - d, prev, cur), d, 0)


def _shift_fwd(cur, nxt, d):
    row = lax.broadcasted_iota(jnp.int32, cur.shape, 0)
    return pltpu.roll(jnp.where(row < d, nxt, cur), SUBLANES - d, 0)


def _causal_conv(ext, g, taps_ref, ntap):
    prev = ext[_rows(g), :]
    cur = ext[_rows(g + 1), :]
    out = _shift_back(prev, cur, ntap - 1) * taps_ref[0:1, :]
    for k in range(1, ntap - 1):
        out = out + _shift_back(prev, cur, ntap - 1 - k) * taps_ref[k:k + 1, :]
    return out + cur * taps_ref[ntap - 1:ntap, :]


def _scan8(A, U, reverse):
    row = lax.broadcasted_iota(jnp.int32, A.shape, 0)
    for s in (1, 2, 4):
        if reverse:
            A_sh = pltpu.roll(A, SUBLANES - s, 0)
            U_sh = pltpu.roll(U, SUBLANES - s, 0)
            m = row < SUBLANES - s
        else:
            A_sh = pltpu.roll(A, s, 0)
            U_sh = pltpu.roll(U, s, 0)
            m = row >= s
        U = jnp.where(m, A * U_sh + U, U)
        A = jnp.where(m, A * A_sh, A)
    return A, U


def _gate_pre(xc_s, w_ref, out_s, H, hd):
    for h in range(H):
        cs = slice(h * hd, (h + 1) * hd)
        out_s[:, cs] = _dot(xc_s[:, cs].astype(BF16), w_ref[h], NN)


def _lru_coeffs(pa, pi, xc, ba, bi, sp):
    ra = _sigmoid(pa + ba)
    ri = _sigmoid(pi + bi)
    log_a = (-LRU_C * ra) * sp
    a = jnp.exp(log_a)
    mult = jnp.sqrt(_one_minus_exp(2.0 * log_a, a))
    return ra, ri, a, mult


def _softplus_neg(lam):
    v = -lam
    return jnp.maximum(v, 0.0) + _log1p(jnp.exp(-jnp.abs(v)))


def _mix_fwd(z, cw, cb, wa, ba, wi, bi, lam, sw, glo, gso, name):
    T = z.shape[0]
    C = z.shape[1] // 5
    H = wa.shape[0]
    hd = C // H
    tb = _blk(T, 256, BF16_ROWS)
    ng = tb // SUBLANES
    HDR = SUBLANES

    def body(z_ref, cw_ref, cb_ref, wa_ref, ba_ref, wi_ref, bi_ref, lam_ref, sw_ref, glo_ref, gso_ref,
             y_ref, h_ref, xext, pext, xc_s, pa_s, pi_s, y_s, hcar):
        @pl.when(pl.program_id(0) == 0)
        def _():
            xext[0:HDR, :] = jnp.zeros((HDR, C), F32)
            pext[0:HDR, :] = jnp.zeros((HDR, C), F32)
            hcar[...] = jnp.zeros_like(hcar)

        def fill(g, _):
            r = _rows(g)
            re = _rows(g + 1)
            xext[re, :] = z_ref[r, 0:C]
            pext[re, :] = z_ref[r, 3 * C:4 * C] * z_ref[r, 4 * C:5 * C]
            return 0
        lax.fori_loop(0, ng, fill, 0)

        def conv(g, _):
            xc_s[_rows(g), :] = _causal_conv(xext, g, cw_ref, 4) + cb_ref[...]
            return 0
        lax.fori_loop(0, ng, conv, 0)

        _gate_pre(xc_s, wa_ref, pa_s, H, hd)
        _gate_pre(xc_s, wi_ref, pi_s, H, hd)
        sp = _softplus_neg(lam_ref[...])

        def group(g, hprev):
            r = _rows(g)
            xc = xc_s[r, :]
            _, ri, a, mult = _lru_coeffs(pa_s[r, :], pi_s[r, :], xc, ba_ref[...], bi_ref[...], sp)
            A, U = _scan8(a, mult * (ri * xc), reverse=False)
            hh = A * hprev + U
            h_ref[r, :] = hh
            gel, _ = _gelu_parts(z_ref[r, C:2 * C])
            y_lru = hh * gel
            y_s[r, 0:C] = y_lru * _rstd(y_lru) * glo_ref[...]
            y_sc = z_ref[r, 2 * C:3 * C] * _causal_conv(pext, g, sw_ref, 3)
            y_s[r, C:2 * C] = y_sc * _rstd(y_sc) * gso_ref[...]
            return jnp.broadcast_to(hh[SUBLANES - 1:SUBLANES, :], hh.shape)
        hcar[...] = lax.fori_loop(0, ng // 2, lambda t, hp: group(2 * t + 1, group(2 * t, hp)), hcar[...])

        xext[0:HDR, :] = xext[tb:tb + HDR, :]
        pext[0:HDR, :] = pext[tb:tb + HDR, :]

        def cast(g, _):
            r = pl.ds(pl.multiple_of(g * BF16_ROWS, BF16_ROWS), BF16_ROWS)
            y_ref[r, :] = y_s[r, :].astype(BF16)
            return 0
        lax.fori_loop(0, tb // BF16_ROWS, cast, 0)

    full = lambda shape: pl.BlockSpec(shape, lambda i: (0,) * len(shape))
    blk = lambda w: pl.BlockSpec((tb, w), lambda i: (i, 0))
    ext = pltpu.VMEM((tb + HDR, C), F32)
    tile = pltpu.VMEM((tb, C), F32)
    return pl.pallas_call(
        body, name=name, grid=(T // tb,),
        in_specs=[blk(5 * C), full((4, C)), full((1, C)), full((H, hd, hd)), full((1, C)), full((H, hd, hd)),
                  full((1, C)), full((1, C)), full((3, C)), full((1, C)), full((1, C))],
        out_specs=[blk(2 * C), blk(C)],
        out_shape=[SDS((T, 2 * C), BF16), SDS((T, C), F32)],
        scratch_shapes=[ext, ext, tile, tile, tile, pltpu.VMEM((tb, 2 * C), F32), pltpu.VMEM((SUBLANES, C), F32)],
        compiler_params=_cp(("arbitrary",), 40),
    )(z, cw, cb, wa, ba, wi, bi, lam, sw, glo, gso)


def _mix_bwd(z, h, dy, cw, cb, wa, ba, wi, bi, lam, sw, glo, gso, name):
    T = z.shape[0]
    C = z.shape[1] // 5
    H = wa.shape[0]
    hd = C // H
    tb = _blk(T, 256, BF16_ROWS)
    nb = T // tb
    ng = tb // SUBLANES
    HDR = SUBLANES
    N_ACC = 13

    def body(z_ref, zp_ref, h_ref, hp_ref, dy_ref, cw_ref, cb_ref, wa_ref, ba_ref, wi_ref, bi_ref, lam_ref,
             sw_ref, glo_ref, gso_ref, dz_ref, small_ref, dwa_ref, dwi_ref,
             xext, pext, hext, dqext, dxcext, bext, xc_s, pa_s, pi_s, a_s, m_s, ri_s, dh_s, dpa_s, dpi_s,
             dz_s, acc_s, bcar):
        i = pl.program_id(0)
        first_rows = i == nb - 1

        @pl.when(i == 0)
        def _():
            dqext[tb:tb + HDR, :] = jnp.zeros((HDR, C), F32)
            dxcext[tb:tb + HDR, :] = jnp.zeros((HDR, C), F32)
            bcar[...] = jnp.zeros_like(bcar)
            acc_s[...] = jnp.zeros_like(acc_s)
            dwa_ref[...] = jnp.zeros_like(dwa_ref)
            dwi_ref[...] = jnp.zeros_like(dwi_ref)

        zero = jnp.zeros((HDR, C), F32)
        xext[0:HDR, :] = jnp.where(first_rows, zero, zp_ref[:, 0:C])
        pext[0:HDR, :] = jnp.where(first_rows, zero, zp_ref[:, 3 * C:4 * C] * zp_ref[:, 4 * C:5 * C])
        hext[0:HDR, :] = jnp.where(first_rows, zero, hp_ref[...])

        def fill(g, _):
            r = _rows(g)
            re = _rows(g + 1)
            xext[re, :] = z_ref[r, 0:C]
            pext[re, :] = z_ref[r, 3 * C:4 * C] * z_ref[r, 4 * C:5 * C]
            hext[re, :] = h_ref[r, :]
            return 0
        lax.fori_loop(0, ng, fill, 0)

        def conv(g, _):
            xc_s[_rows(g), :] = _causal_conv(xext, g, cw_ref, 4) + cb_ref[...]
            return 0
        lax.fori_loop(0, ng, conv, 0)

        _gate_pre(xc_s, wa_ref, pa_s, H, hd)
        _gate_pre(xc_s, wi_ref, pi_s, H, hd)
        sp = _softplus_neg(lam_ref[...])
        dsp_dlam = -jax.nn.sigmoid(-lam_ref[...])

        def add_acc(k, v):
            acc_s[k] += v

        def p1(g, _):
            r = _rows(g)
            xc = xc_s[r, :]
            _, ri, a, mult = _lru_coeffs(pa_s[r, :], pi_s[r, :], xc, ba_ref[...], bi_ref[...], sp)
            a_s[r, :] = a
            m_s[r, :] = mult
            ri_s[r, :] = ri
            hh = h_ref[r, :]
            gel, gelp = _gelu_parts(z_ref[r, C:2 * C])
            y_lru = hh * gel
            dnl = dy_ref[r, 0:C]
            rl = _rstd(y_lru)
            ylr = y_lru * rl
            gd = glo_ref[...] * dnl
            dy_lru = rl * (gd - ylr * jnp.mean(gd * ylr, axis=-1, keepdims=True))
            add_acc(R_GLO, dnl * ylr)
            dz_s[r, C:2 * C] = dy_lru * hh * gelp
            dh = dy_lru * gel
            dh_s[r, :] = dh

            q = _causal_conv(pext, g, sw_ref, 3)
            scb = z_ref[r, 2 * C:3 * C]
            y_sc = scb * q
            dns = dy_ref[r, C:2 * C]
            rs = _rstd(y_sc)
            ysr = y_sc * rs
            gs = gso_ref[...] * dns
            dy_sc = rs * (gs - ysr * jnp.mean(gs * ysr, axis=-1, keepdims=True))
            add_acc(R_GSO, dns * ysr)
            dz_s[r, 2 * C:3 * C] = dy_sc * q
            dqext[r, :] = dy_sc * scb
            return 0
        lax.fori_loop(0, ng, p1, 0, unroll=2)

        bext[tb:tb + HDR, :] = bcar[...]

        def p2(j, carry):
            g = ng - 1 - j
            r = _rows(g)
            a = a_s[r, :]
            A, U = _scan8(a, a * dh_s[r, :], reverse=True)
            bb = A * carry + U
            bext[r, :] = bb
            return jnp.broadcast_to(bb[0:1, :], bb.shape)
        bcar[...] = lax.fori_loop(0, ng, p2, bcar[...])

        def p3(g, _):
            r = _rows(g)
            rn = _rows(g + 1)
            G = dh_s[r, :] + _shift_fwd(bext[r, :], bext[rn, :], 1)
            hm1 = _shift_back(hext[r, :], hext[rn, :], 1)
            a = a_s[r, :]
            mult = m_s[r, :]
            ri = ri_s[r, :]
            xc = xc_s[r, :]
            ra = _sigmoid(pa_s[r, :] + ba_ref[...])
            dxcext[r, :] = G * mult * ri
            dri = G * mult * xc
            dmult = G * ri * xc
            dlog_a = (G * hm1) * a - dmult * (a * a) / mult
            add_acc(R_LAM, dlog_a * (-LRU_C * ra) * dsp_dlam)
            dpa = dlog_a * (-LRU_C * sp) * ra * (1.0 - ra)
            dpi = dri * ri * (1.0 - ri)
            add_acc(R_BA, dpa)
            add_acc(R_BI, dpi)
            dpa_s[r, :] = dpa
            dpi_s[r, :] = dpi
            return 0
        lax.fori_loop(0, ng, p3, 0)

        for hh_ in range(H):
            cs = slice(hh_ * hd, (hh_ + 1) * hd)
            dpa_b = dpa_s[:, cs].astype(BF16)
            dpi_b = dpi_s[:, cs].astype(BF16)
            xc_b = xc_s[:, cs].astype(BF16)
            dxcext[0:tb, cs] += _dot(dpa_b, wa_ref[hh_], NT) + _dot(dpi_b, wi_ref[hh_], NT)
            dwa_ref[hh_] += _dot(xc_b, dpa_b, TN)
            dwi_ref[hh_] += _dot(xc_b, dpi_b, TN)

        def p4(g, _):
            r = _rows(g)
            rn = _rows(g + 1)
            dxc = dxcext[r, :]
            dxc_n = dxcext[rn, :]
            x_p = xext[r, :]
            x_c = xext[rn, :]
            add_acc(R_CB, dxc)
            dlx = dxc * cw_ref[3:4, :]
            add_acc(R_CW + 3, dxc * x_c)
            for d in range(1, 4):
                dlx = dlx + _shift_fwd(dxc, dxc_n, d) * cw_ref[3 - d:4 - d, :]
                add_acc(R_CW + 3 - d, dxc * _shift_back(x_p, x_c, d))
            dz_s[r, 0:C] = dlx

            dq = dqext[r, :]
            dq_n = dqext[rn, :]
            p_p = pext[r, :]
            p_c = pext[rn, :]
            dp = dq * sw_ref[2:3, :]
            add_acc(R_SW + 2, dq * p_c)
            for d in range(1, 3):
                dp = dp + _shift_fwd(dq, dq_n, d) * sw_ref[2 - d:3 - d, :]
                add_acc(R_SW + 2 - d, dq * _shift_back(p_p, p_c, d))
            dz_s[r, 3 * C:4 * C] = dp * z_ref[r, 4 * C:5 * C]
            dz_s[r, 4 * C:5 * C] = dp * z_ref[r, 3 * C:4 * C]
            return 0
        lax.fori_loop(0, ng, p4, 0)

        dqext[tb:tb + HDR, :] = dqext[0:HDR, :]
        dxcext[tb:tb + HDR, :] = dxcext[0:HDR, :]

        def cast(g, _):
            r = pl.ds(pl.multiple_of(g * BF16_ROWS, BF16_ROWS), BF16_ROWS)
            dz_ref[r, :] = dz_s[r, :].astype(BF16)
            return 0
        lax.fori_loop(0, tb // BF16_ROWS, cast, 0)

        @pl.when(i == nb - 1)
        def _():
            small_ref[...] = jnp.zeros_like(small_ref)
            for k in range(N_ACC):
                small_ref[k:k + 1, :] = jnp.sum(acc_s[k], axis=0, keepdims=True)

    tpg = tb // SUBLANES
    full = lambda shape: pl.BlockSpec(shape, lambda i: (0,) * len(shape))
    blk = lambda w: pl.BlockSpec((tb, w), lambda i: (nb - 1 - i, 0))
    prev = lambda w: pl.BlockSpec((SUBLANES, w), lambda i: (jnp.maximum((nb - 1 - i) * tpg - 1, 0), 0))
    ext = pltpu.VMEM((tb + HDR, C), F32)
    tile = pltpu.VMEM((tb, C), F32)
    return pl.pallas_call(
        body, name=name, grid=(nb,),
        in_specs=[blk(5 * C), prev(5 * C), blk(C), prev(C), blk(2 * C), full((4, C)), full((1, C)), full((H, hd, hd)),
                  full((1, C)), full((H, hd, hd)), full((1, C)), full((1, C)), full((3, C)), full((1, C)), full((1, C))],
        out_specs=[blk(5 * C), full((SMALL_ROWS, C)), full((H, hd, hd)), full((H, hd, hd))],
        out_shape=[SDS((T, 5 * C), BF16), SDS((SMALL_ROWS, C), F32), SDS((H, hd, hd), F32), SDS((H, hd, hd), F32)],
        scratch_shapes=[ext] * 6 + [tile] * 9 + [pltpu.VMEM((tb, 5 * C), F32), pltpu.VMEM((N_ACC, SUBLANES, C), F32),
                                                pltpu.VMEM((SUBLANES, C), F32)],
        compiler_params=_cp(("arbitrary",), 56),
    )(z, z, h, h, dy, cw, cb, wa, ba, wi, bi, lam, sw, glo, gso)


def _add_slabs(terms, out_dtype, name):
    R, Ccols = terms[0].shape
    br = _blk(R, 512, BF16_ROWS)
    n = len(terms)

    def body(*refs):
        s = refs[0][...].astype(F32)
        for t_ref in refs[1:n]:
            s = s + t_ref[...].astype(F32)
        refs[n][...] = s.astype(out_dtype)

    spec = pl.BlockSpec((br, Ccols), lambda i: (i, 0))
    return pl.pallas_call(
        body, name=name, grid=(R // br,), in_specs=[spec] * n, out_specs=spec, out_shape=SDS((R, Ccols), out_dtype),
        compiler_params=_cp(("arbitrary",), 40),
    )(*terms)


def _final_grad(sb, lb, chip, name):
    _, R, Ccols = sb.shape
    br = _blk(R, 512, BF16_ROWS)

    def body(chip_ref, sb_ref, l0, l1, l2, o_ref):
        s = sb_ref[0].astype(F32)
        for t_ref in (l0, l1, l2):
            s = s + t_ref[0].astype(F32)
        o_ref[...] = s

    lspec = lambda k: pl.BlockSpec((1, br, Ccols), lambda i, c: (k, i, 0))
    return pl.pallas_call(
        body, name=name,
        grid_spec=pltpu.PrefetchScalarGridSpec(
            num_scalar_prefetch=1, grid=(R // br,),
            in_specs=[pl.BlockSpec((1, br, Ccols), lambda i, c: (c[0], i, 0)), lspec(0), lspec(1), lspec(2)],
            out_specs=pl.BlockSpec((br, Ccols), lambda i, c: (i, 0))),
        out_shape=SDS((R, Ccols), F32),
        compiler_params=_cp(("arbitrary",), 40),
    )(chip, sb, lb, lb, lb)


def _adamw(w, g, m, v, name):
    R, Ccols = w.shape
    br = _blk(R, 256, SUBLANES)
    c1 = 1.0 - ADAM_B1 ** ADAM_STEP
    c2 = 1.0 - ADAM_B2 ** ADAM_STEP

    def body(w_ref, g_ref, m_ref, v_ref, d_ref, nm_ref, nv_ref):
        gv = g_ref[...]
        nm = ADAM_B1 * m_ref[...] + (1.0 - ADAM_B1) * gv
        nv = ADAM_B2 * v_ref[...] + (1.0 - ADAM_B2) * (gv * gv)
        nm_ref[...] = nm
        nv_ref[...] = nv
        d_ref[...] = -ADAM_LR * ((nm / c1) / (jnp.sqrt(nv / c2) + ADAM_EPS) + ADAM_WD * w_ref[...])

    spec = pl.BlockSpec((br, Ccols), lambda i: (i, 0))
    return pl.pallas_call(
        body, name=name, grid=(R // br,), in_specs=[spec] * 4, out_specs=[spec] * 3,
        out_shape=[SDS((R, Ccols), F32)] * 3, compiler_params=_cp(("arbitrary",), 40),
    )(w, g, m, v)


def _place():
    return lax.axis_index("x"), lax.axis_index("y"), lax.axis_index("c")


def _dev_rows(ref, dev, rows):
    return ref.at[pl.ds((4 * dev[0] + 2 * dev[1] + dev[2]) * rows, rows), :]


def _remote(src, dst, send_sem, recv_sem, to):
    return pltpu.make_async_remote_copy(src_ref=src, dst_ref=dst, send_sem=send_sem, recv_sem=recv_sem,
                                        device_id=to, device_id_type=MESH)


SAME_CORE_AND_SIBLING = ((0, 0, 1), (1, 0, 0), (0, 1, 0), (1, 1, 0))
ALL_OTHERS = SAME_CORE_AND_SIBLING + ((1, 0, 1), (0, 1, 1), (1, 1, 1))


def _merge_phases(a, b):
    na_in, na_out, na_sem = len(a.inputs), len(a.out_shapes), len(a.sem_shapes)

    def build(ins, outs, sems, stage):
        return (a.build(ins[:na_in], outs[:na_out], sems[:na_sem], stage)
                + b.build(ins[na_in:], outs[na_out:], sems[na_sem:], stage))

    aliases = dict(a.aliases)
    aliases.update({na_in + i: na_out + o for i, o in b.aliases.items()})
    return _Carried(a.inputs + b.inputs, a.out_shapes + b.out_shapes, aliases, a.sem_shapes + b.sem_shapes, build,
                    has_mid=a.has_mid or b.has_mid)


def _ag_direct_phase(slab, pieces, flips):
    W = slab.shape[1]
    n = len(pieces)
    npeer = len(flips)

    def build(ins, outs, sems, stage):
        if stage == "mid":
            return []
        starting = stage == "start"
        (slab_ref,) = ins
        send_sems, recv_sems, local_sems = sems
        x, y, c = _place()
        me = (x, y, c)
        peers = [tuple(1 - v if f else v for v, f in zip(me, flip)) for flip in flips]
        todo = []
        for p, (off, rows) in enumerate(pieces):
            src = slab_ref.at[pl.ds(off, rows), :]
            mine = pltpu.make_async_copy(src, _dev_rows(outs[p], me, rows), local_sems.at[p])
            todo.append(mine.start if starting else mine.wait)
            for k, peer in enumerate(peers):
                snd = _remote(src, _dev_rows(outs[p], me, rows), send_sems.at[k, p], recv_sems.at[k, p], peer)
                if starting:
                    todo.append(snd.start)
                else:
                    theirs = _dev_rows(outs[p], peer, rows)
                    rcv = _remote(theirs, theirs, send_sems.at[k, p], recv_sems.at[k, p], me)
                    todo += [rcv.wait_recv, snd.wait_send]
        return todo

    dma = pltpu.SemaphoreType.DMA
    return _Carried([slab], [SDS((N_DEV * rows, W), slab.dtype) for _, rows in pieces], {},
                    [dma((npeer, n)), dma((npeer, n)), dma((n,))], build)


def _ag_two_level_phase(slab, pieces):
    W = slab.shape[1]
    n = len(pieces)

    def build(ins, outs, sems, stage):
        (slab_ref,) = ins
        send_sems, recv_sems, local_sems = sems
        x, y, c = _place()
        me, sibling = (x, y, c), (x, y, 1 - c)
        chips = [(1 - x, y), (x, 1 - y), (1 - x, 1 - y)]
        todo = []
        for p, (off, rows) in enumerate(pieces):
            src = slab_ref.at[pl.ds(off, rows), :]
            own = _dev_rows(outs[p], me, rows)
            landed = [_dev_rows(outs[p], (*chip, c), rows) for chip in chips]

            def mine():
                return pltpu.make_async_copy(src, own, local_sems.at[p])

            def first():
                return [_remote(src, own, send_sems.at[k, p], recv_sems.at[k, p], to)
                        for k, to in enumerate([sibling] + [(*chip, c) for chip in chips])]

            def passed():
                return [_remote(blk, blk, send_sems.at[4 + j, p], recv_sems.at[4 + j, p], sibling)
                        for j, blk in enumerate(landed)]

            def arrival(k, blk):
                return _remote(blk, blk, send_sems.at[k, p], recv_sems.at[k, p], me).wait_recv

            if stage == "start":
                todo += [mine().start] + [cp.start for cp in first()]
            elif stage == "mid":
                for j, (blk, fwd) in enumerate(zip(landed, passed())):
                    todo += [arrival(1 + j, blk), fwd.start]
            else:
                theirs = [_dev_rows(outs[p], sibling, rows)] + [_dev_rows(outs[p], (*chip, 1 - c), rows) for chip in chips]
                todo += [arrival(k, blk) for k, blk in zip((0, 4, 5, 6), theirs)]
                todo += [cp.wait_send for cp in first() + passed()] + [mine().wait]
        return todo

    dma = pltpu.SemaphoreType.DMA
    return _Carried([slab], [SDS((N_DEV * rows, W), slab.dtype) for _, rows in pieces], {},
                    [dma((7, n)), dma((7, n)), dma((n,))], build, has_mid=True)


def _ag_forward_phase(gathered, pieces):
    n = len(pieces)

    def build(ins, outs, sems, stage):
        if stage == "mid":
            return []
        starting = stage == "start"
        send_sems, recv_sems = sems
        x, y, c = _place()
        me, sibling = (x, y, c), (x, y, 1 - c)
        chips = [(1 - x, y), (x, 1 - y), (1 - x, 1 - y)]
        todo = []
        for p, (_, rows) in enumerate(pieces):
            for j, chip in enumerate(chips):
                snd = _remote(_dev_rows(ins[p], (*chip, c), rows), _dev_rows(outs[p], (*chip, c), rows),
                              send_sems.at[j, p], recv_sems.at[j, p], sibling)
                if starting:
                    todo.append(snd.start)
                else:
                    theirs = _dev_rows(outs[p], (*chip, 1 - c), rows)
                    rcv = _remote(theirs, theirs, send_sems.at[j, p], recv_sems.at[j, p], me)
                    todo += [rcv.wait_recv, snd.wait_send]
        return todo

    dma = pltpu.SemaphoreType.DMA
    return _Carried(gathered, [SDS(g.shape, g.dtype) for g in gathered], {p: p for p in range(n)},
                    [dma((3, n)), dma((3, n))], build)


def _rs_chips_phase(sb):
    _, R, W = sb.shape

    def build(ins, outs, sems, stage):
        if stage == "mid":
            return []
        (sb_ref,), (land_ref,) = ins, outs
        send_sems, recv_sems = sems
        x, y, c = _place()
        chips = [(1 - x, y), (x, 1 - y), (1 - x, 1 - y)]
        cps = [_remote(sb_ref.at[2 * chip[0] + chip[1]], land_ref.at[j], send_sems.at[j], recv_sems.at[j], (*chip, c))
               for j, chip in enumerate(chips)]
        if stage == "start":
            return [cp.start for cp in cps]
        return [cp.wait_recv for cp in cps] + [cp.wait_send for cp in cps]

    dma = pltpu.SemaphoreType.DMA
    return _Carried([sb], [SDS((3, R, W), sb.dtype)], {}, [dma((3,)), dma((3,))], build)


def _allgather(slab, pieces, name):
    R, W = slab.shape
    n = len(pieces)
    assert sum(rows for _, rows in pieces) == R

    def body(slab_ref, *refs):
        outs = refs[:n]
        send_sems, recv_sems, local_sems = refs[n:]
        x, y, c = _place()
        me, sibling = (x, y, c), (x, y, 1 - c)
        chips = [(1 - x, y), (x, 1 - y), (1 - x, 1 - y)]

        def dst_rows(p, origin):
            rows = pieces[p][1]
            start = (4 * origin[0] + 2 * origin[1] + origin[2]) * rows
            return outs[p].at[pl.ds(start, rows), :]

        def copies(k, origin, to, from_slab):
            out = []
            for p, (off, rows) in enumerate(pieces):
                dst = dst_rows(p, origin)
                src = slab_ref.at[pl.ds(off, rows), :] if from_slab else dst
                out.append(pltpu.make_async_remote_copy(
                    src_ref=src, dst_ref=dst, send_sem=send_sems.at[k, p], recv_sem=recv_sems.at[k, p],
                    device_id=to, device_id_type=MESH))
            return out

        mine = [pltpu.make_async_copy(slab_ref.at[pl.ds(off, rows), :], dst_rows(p, me), local_sems.at[p])
                for p, (off, rows) in enumerate(pieces)]
        for cp in mine:
            cp.start()
        first = copies(0, me, sibling, True)
        for j, chip in enumerate(chips):
            first += copies(1 + j, me, (*chip, c), True)
        for cp in first:
            cp.start()
        passed = []
        for j, chip in enumerate(chips):
            for cp in copies(1 + j, (*chip, c), me, False):
                cp.wait_recv()
            fwd = copies(4 + j, (*chip, c), sibling, False)
            for cp in fwd:
                cp.start()
            passed += fwd
        for cp in copies(0, sibling, me, False):
            cp.wait_recv()
        for j, chip in enumerate(chips):
            for cp in copies(4 + j, (*chip, 1 - c), me, False):
                cp.wait_recv()
        for cp in first + passed:
            cp.wait_send()
        for cp in mine:
            cp.wait()

    return pl.pallas_call(
        body, name=name,
        in_specs=[HBM_SPEC], out_specs=[HBM_SPEC] * n,
        out_shape=[SDS((N_DEV * rows, W), slab.dtype) for _, rows in pieces],
        scratch_shapes=[pltpu.SemaphoreType.DMA((7, n)), pltpu.SemaphoreType.DMA((7, n)), pltpu.SemaphoreType.DMA((n,))],
    )(slab)


def _rs_sibling(grads, pieces, name):
    W = grads[0].shape[1]
    R = sum(rows for _, rows in pieces)
    n = len(pieces)
    dt = grads[0].dtype
    max_rows = max(rows for _, rows in pieces)
    steps = [(q, p) for q in range(N_CHIP) for p in range(n)]
    ns = len(steps)
    ADD_ROWS = 64
    assert all(rows % ADD_ROWS == 0 for _, rows in pieces)

    def body(*refs):
        g_refs = refs[:n]
        sb_ref, mine_buf, send_buf, land_buf, out_buf, in_sems, out_sems, send_sems, recv_sems, credit = refs[n:]
        x, y, c = _place()
        sibling = (x, y, 1 - c)

        def loads(s):
            q, p = steps[s]
            rows = pieces[p][1]
            slot = s % 2
            mine = g_refs[p].at[pl.ds((2 * q + c) * rows, rows), :]
            theirs = g_refs[p].at[pl.ds((2 * q + 1 - c) * rows, rows), :]
            return (pltpu.make_async_copy(mine, mine_buf.at[slot, pl.ds(0, rows), :], in_sems.at[slot, 0]),
                    pltpu.make_async_copy(theirs, send_buf.at[slot, pl.ds(0, rows), :], in_sems.at[slot, 1]))

        def send(s):
            rows = pieces[steps[s][1]][1]
            slot = s % 2
            return pltpu.make_async_remote_copy(
                src_ref=send_buf.at[slot, pl.ds(0, rows), :], dst_ref=land_buf.at[slot, pl.ds(0, rows), :],
                send_sem=send_sems.at[slot], recv_sem=recv_sems.at[slot], device_id=sibling, device_id_type=MESH)

        def store(s):
            q, p = steps[s]
            off, rows = pieces[p]
            slot = s % 2
            return pltpu.make_async_copy(out_buf.at[slot, pl.ds(0, rows), :], sb_ref.at[q, pl.ds(off, rows), :],
                                         out_sems.at[slot])

        def start_send(s):
            for cp in loads(s):
                cp.wait()
            if s >= 2:
                pl.semaphore_wait(credit.at[s % 2], 1)
            send(s).start()

        for s in range(min(2, ns)):
            for cp in loads(s):
                cp.start()
        start_send(0)
        for s in range(ns):
            slot = s % 2
            rows = pieces[steps[s][1]][1]
            if s + 1 < ns:
                start_send(s + 1)
            send(s).wait_recv()
            if s >= 2:
                store(s - 2).wait()

            def add(k, _, slot=slot):
                r = pl.ds(pl.multiple_of(k * ADD_ROWS, ADD_ROWS), ADD_ROWS)
                out_buf[slot, r, :] = (mine_buf[slot, r, :].astype(F32) + land_buf[slot, r, :].astype(F32)).astype(dt)
                return 0
            lax.fori_loop(0, rows // ADD_ROWS, add, 0)
            if s + 2 < ns:
                pl.semaphore_signal(credit.at[slot], inc=1, device_id=sibling, device_id_type=MESH)
            store(s).start()
            send(s).wait_send()
            if s + 2 < ns:
                for cp in loads(s + 2):
                    cp.start()
        for s in range(max(ns - 2, 0), ns):
            store(s).wait()

    buf = pltpu.VMEM((2, max_rows, W), dt)
    return pl.pallas_call(
        body, name=name,
        in_specs=[HBM_SPEC] * n, out_specs=HBM_SPEC,
        out_shape=SDS((N_CHIP, R, W), dt),
        scratch_shapes=[buf, buf, buf, buf, pltpu.SemaphoreType.DMA((2, 2)), pltpu.SemaphoreType.DMA((2,)),
                        pltpu.SemaphoreType.DMA((2,)), pltpu.SemaphoreType.DMA((2,)), pltpu.SemaphoreType.REGULAR((2,))],
        compiler_params=pltpu.CompilerParams(vmem_limit_bytes=40 * MIB),
    )(*grads)


SMALL_NAMES = ("ffn1_norm", "mix_norm", "ffn2_norm", "final_norm", "lru_conv_w", "lru_conv_b", "lru_w_a", "lru_b_a",
               "lru_w_i", "lru_b_i", "lru_lambda", "sc_conv_w", "lru_out_norm", "sc_out_norm")
WEIGHT_NAMES = ("ffn1_norm", "ffn1_w_gate", "ffn1_w_up", "ffn1_w_down", "mix_norm", "w_in", "lru_conv_w", "lru_conv_b",
                "lru_w_a", "lru_b_a", "lru_w_i", "lru_b_i", "lru_lambda", "sc_conv_w", "lru_out_norm", "sc_out_norm",
                "w_out", "ffn2_norm", "ffn2_w_gate", "ffn2_w_up", "ffn2_w_down", "final_norm")
BIG = (("ffn1_w_gate", True), ("ffn1_w_up", True), ("ffn1_w_down", False), ("ffn2_w_gate", True), ("ffn2_w_up", True),
       ("ffn2_w_down", False), ("w_in", True), ("w_out", False))


SLAB_ROW_ALIGN = 256


def _pack_rows(parts, width):
    rows, counts = [], []
    for p in parts:
        flat = p.reshape(-1)
        nr = -(-flat.shape[0] // width)
        nr = -(-nr // SUBLANES) * SUBLANES
        rows.append(jnp.pad(flat, (0, nr * width - flat.shape[0])).reshape(nr, width))
        counts.append(nr)
    total = sum(counts)
    pad = -(-total // SLAB_ROW_ALIGN) * SLAB_ROW_ALIGN - total
    if pad:
        rows.append(jnp.zeros((pad, width), rows[0].dtype))
    return jnp.concatenate(rows, axis=0), counts


def _stack_rows(blocks):
    pieces, off = [], 0
    for b in blocks:
        pieces.append((off, b.shape[0]))
        off += b.shape[0]
    return jnp.concatenate(blocks, axis=0), pieces


def _unpack_rows(slab, counts, shapes):
    out, r = [], 0
    for nr, shape in zip(counts, shapes):
        size = math.prod(shape)
        out.append(slab[r:r + nr].reshape(-1)[:size].reshape(shape))
        r += nr
    return out


def kernel(x, ffn1_norm, ffn1_w_gate, ffn1_w_up, ffn1_w_down, mix_norm, w_in, lru_conv_w, lru_conv_b, lru_w_a, lru_b_a, lru_w_i, lru_b_i, lru_lambda, sc_conv_w, lru_out_norm, sc_out_norm, w_out, ffn2_norm, ffn2_w_gate, ffn2_w_up, ffn2_w_down, final_norm, loss_target, m_ffn1_norm, m_ffn1_w_gate, m_ffn1_w_up, m_ffn1_w_down, m_mix_norm, m_w_in, m_lru_conv_w, m_lru_conv_b, m_lru_w_a, m_lru_b_a, m_lru_w_i, m_lru_b_i, m_lru_lambda, m_sc_conv_w, m_lru_out_norm, m_sc_out_norm, m_w_out, m_ffn2_norm, m_ffn2_w_gate, m_ffn2_w_up, m_ffn2_w_down, m_final_norm, v_ffn1_norm, v_ffn1_w_gate, v_ffn1_w_up, v_ffn1_w_down, v_mix_norm, v_w_in, v_lru_conv_w, v_lru_conv_b, v_lru_w_a, v_lru_b_a, v_lru_w_i, v_lru_b_i, v_lru_lambda, v_sc_conv_w, v_lru_out_norm, v_sc_out_norm, v_w_out, v_ffn2_norm, v_ffn2_w_gate, v_ffn2_w_up, v_ffn2_w_down, v_final_norm):
    a = dict(locals())
    w = {n: a[n] for n in WEIGHT_NAMES}
    m = {n: a["m_" + n] for n in WEIGHT_NAMES}
    v = {n: a["v_" + n] for n in WEIGHT_NAMES}
    ax, ay, ac = _place()
    dev = 4 * ax + 2 * ay + ac
    chip = (2 * ax + ay).astype(jnp.int32).reshape(1)

    x0 = x[0]
    tgt = loss_target[0]
    T, D = x0.shape
    C = D // 2
    H, hd = lru_w_a.shape[1], lru_w_a.shape[2]
    CL = lru_conv_w.shape[2]

    shards = []
    for name, transposed in BIG:
        s = w[name][0]
        shards.append((s.T if transposed else s).astype(BF16))
    taps = jnp.concatenate([lru_conv_w[0], sc_conv_w[0], jnp.zeros((1, CL), F32)], axis=0)
    taps_row = lax.bitcast_convert_type(taps, BF16).reshape(1, -1)
    taps_blk = jnp.pad(taps_row, ((0, BF16_ROWS - 1), (0, D - taps_row.shape[1])))
    s_wg1, s_wu1, s_wd1, s_wg2, s_wu2, s_wd2, s_win, s_wout = shards
    slab_g1, pcs_g1 = _stack_rows([s_wg1])
    slab_u1, pcs_u1 = _stack_rows([s_wu1])
    slab_d1, pcs_d1 = _stack_rows([s_wd1])
    slab_mw, pcs_mw = _stack_rows([s_win, s_wout, taps_blk])
    slab_gu2, pcs_gu2 = _stack_rows([s_wg2, s_wu2])
    slab_d2, pcs_d2 = _stack_rows([s_wd2])
    (wg1,) = _allgather(slab_g1, pcs_g1, "allgather_ffn1_gate")

    g1, gm, g3 = ffn1_norm, mix_norm, ffn2_norm
    phase = _merge_phases(_ag_two_level_phase(slab_u1, pcs_u1), _ag_direct_phase(slab_d1, pcs_d1, SAME_CORE_AND_SIBLING))
    (n1, hg1), got = _norm_proj(x0, g1, [wg1], [BF16], False, "ffn1_gate", carried=phase)
    wu1, d1 = got[0], got[1:]
    phase = _merge_phases(_ag_forward_phase(d1, pcs_d1), _ag_two_level_phase(slab_mw, pcs_mw))
    (hu1, act1), got = _up_act(n1, wu1, hg1, "ffn1_up", carried=phase)
    wd1, (win, wout, taps_all) = got[0], got[1:]
    x1, gu2 = _mm_res(act1, wd1, x0, 0.5, "ffn1_down", carried=_ag_direct_phase(slab_gu2, pcs_gu2, SAME_CORE_AND_SIBLING))
    phase = _merge_phases(_ag_forward_phase(gu2, pcs_gu2), _ag_direct_phase(slab_d2, pcs_d2, SAME_CORE_AND_SIBLING))
    (n2, z), got = _norm_proj(x1, gm, [win], [F32], False, "in_proj", carried=phase)
    (wg2, wu2), d2 = got[:2], got[2:]
    taps_all = taps_all.reshape(N_DEV, BF16_ROWS, D)[:, 0, :2 * SUBLANES * CL].reshape(N_DEV, SUBLANES, CL, 2)
    taps_all = lax.bitcast_convert_type(taps_all, F32)
    taps_all = taps_all.transpose(1, 0, 2).reshape(SUBLANES, N_DEV * CL)
    cw, sw = taps_all[0:4], taps_all[4:7]

    gf = final_norm.reshape(1, D)
    cb = lru_conv_b
    wa, wi = lru_w_a[0].astype(BF16), lru_w_i[0].astype(BF16)
    ba, bi = lru_b_a.reshape(1, C), lru_b_i.reshape(1, C)
    lam, glo, gso = lru_lambda, lru_out_norm, sc_out_norm

    y, h = _mix_fwd(z, cw, cb, wa, ba, wi, bi, lam, sw, glo, gso, "mix_fwd")
    x2, (wd2,) = _mm_res(y, wout, x1, 1.0, "out_proj", carried=_ag_forward_phase(d2, pcs_d2))
    n3, hg2, hu2, act2 = _norm_proj(x2, g3, [wg2, wu2], [BF16, BF16], True, "ffn2_up")
    x3 = _mm_res(act2, wd2, x2, 0.5, "ffn2_down")
    dx3, df2, d_gf, loss_blk = _loss_head(x3, gf, tgt, "loss_head")

    F = wd1.shape[0]
    bm_f = F // 4 if (F // 4) % LANES == 0 else 512

    def reduce_group(gs, tag):
        pcs, off = [], 0
        for g_ in gs:
            pcs.append((off, g_.shape[0] // N_DEV))
            off += g_.shape[0] // N_DEV
        sb_ = _rs_sibling(gs, pcs, "rs_sibling_add_" + tag)
        return sb_, pcs

    dhg2, dhu2 = _ffn_bwd_act(df2, wd2, hg2, hu2, "ffn2_bwd_act")
    d_wd2 = _dw_tn(act2, df2, bm_f, "ffn2_dw_down")
    d_wg2 = _dw_tn(dhg2, n3, bm_f, "ffn2_dw_gate")
    d_wu2 = _dw_tn(dhu2, n3, bm_f, "ffn2_dw_up")
    sb_f2, pcs_f2 = reduce_group([d_wg2, d_wu2, d_wd2], "ffn2")
    (dx2, dx2b, d_g3), (lb_f2,) = _mm_rmsbwd([(dhg2, wg2), (dhu2, wu2)], x2, g3, dx3, 1.0, "ffn2_bwd_in",
                                             carried=_rs_chips_phase(sb_f2))
    dy = _mm_nt(dx2b, wout, "out_proj_bwd")
    d_wout = _dw_tn(y, dx2b, 1024, "out_proj_dw")
    dz, small, d_wa, d_wi = _mix_bwd(z, h, dy, cw, cb, wa, ba, wi, bi, lam, sw, glo, gso, "mix_bwd")
    d_win = _dw_tn(dz, n2, 1280, "in_proj_dw")
    sb_mx, pcs_mx = reduce_group([d_win, d_wout], "mix")
    (dx1, df1, d_gm), (lb_mx,) = _mm_rmsbwd([(dz, win)], x1, gm, dx2, 0.5, "in_proj_bwd",
                                            carried=_rs_chips_phase(sb_mx))
    dhg1, dhu1 = _ffn_bwd_act(df1, wd1, hg1, hu1, "ffn1_bwd_act")
    d_wd1 = _dw_tn(act1, df1, bm_f, "ffn1_dw_down")
    d_wg1 = _dw_tn(dhg1, n1, bm_f, "ffn1_dw_gate")
    d_wu1 = _dw_tn(dhu1, n1, bm_f, "ffn1_dw_up")
    sb_f1, pcs_f1 = reduce_group([d_wg1, d_wu1, d_wd1], "ffn1")
    (dx0, _, d_g1), (lb_f1,) = _mm_rmsbwd([(dhg1, wg1), (dhu1, wu1)], x0, g1, dx1, 1.0, "ffn1_bwd_in",
                                          carried=_rs_chips_phase(sb_f1))

    big_sum = {}
    for tag, names, sb_, lb_, pcs in (("ffn2", ("ffn2_w_gate", "ffn2_w_up", "ffn2_w_down"), sb_f2, lb_f2, pcs_f2),
                                      ("mix", ("w_in", "w_out"), sb_mx, lb_mx, pcs_mx),
                                      ("ffn1", ("ffn1_w_gate", "ffn1_w_up", "ffn1_w_down"), sb_f1, lb_f1, pcs_f1)):
        gsum = _final_grad(sb_, lb_, chip, "rs_final_sum_" + tag)
        for name, (off, rows) in zip(names, pcs):
            big_sum[name] = gsum[off:off + rows]

    small_parts = [d_g1, d_gm, d_g3, d_gf, small[R_CW:R_CW + 4], small[R_CB], d_wa, small[R_BA], d_wi, small[R_BI],
                   small[R_LAM], small[R_SW:R_SW + 3], small[R_GLO], small[R_GSO]]
    sslab, counts = _pack_rows(small_parts, LANES)
    RS = sslab.shape[0]
    (sg,) = _allgather(sslab, [(0, RS)], "allgather_small_grads")
    ssum = _add_slabs([sg[j * RS:(j + 1) * RS] for j in range(N_DEV)], F32, "small_grads_sum")
    full_shapes = [(1, D), (1, D), (1, D), (D,), (1, 4, C), (1, C), (1, H, hd, hd), (1, H, hd), (1, H, hd, hd), (1, H, hd),
                   (1, C), (1, 3, C), (1, C), (1, C)]
    small_full = dict(zip(SMALL_NAMES, _unpack_rows(ssum, counts, full_shapes)))

    grads = {}
    for name, transposed in BIG:
        gblk = big_sum[name]
        grads[name] = (gblk.T if transposed else gblk)[None]
    for name in SMALL_NAMES:
        gfull = small_full[name]
        if name in ("lru_conv_w", "sc_conv_w"):
            gfull = lax.dynamic_slice_in_dim(gfull, dev * CL, CL, axis=2)
        grads[name] = gfull

    delta, new_m, new_v = {}, {}, {}
    for name, transposed in BIG:
        flip = transposed and w[name].shape[2] % LANES != 0
        view = (lambda t: t[0].T) if flip else (lambda t: t[0])
        back = (lambda t: t.T[None]) if flip else (lambda t: t[None])
        gview = big_sum[name] if flip else grads[name][0]
        d_, m_, v_ = _adamw(view(w[name]), gview, view(m[name]), view(v[name]), "adamw_" + name)
        delta[name], new_m[name], new_v[name] = back(d_), back(m_), back(v_)
    packs = [_pack_rows([t[n_] for n_ in SMALL_NAMES], LANES) for t in (w, grads, m, v)]
    sd, sm, sv = _adamw(packs[0][0], packs[1][0], packs[2][0], packs[3][0], "adamw_small")
    shapes = [w[n_].shape for n_ in SMALL_NAMES]
    for tgt_dict, slab_ in ((delta, sd), (new_m, sm), (new_v, sv)):
        for n_, val in zip(SMALL_NAMES, _unpack_rows(slab_, packs[0][1], shapes)):
            tgt_dict[n_] = val

    loss = lax.psum(loss_blk[0, 0], ("x", "y", "c"))
    return (loss, dx0[None], *[grads[n_] for n_ in WEIGHT_NAMES], *[delta[n_] for n_ in WEIGHT_NAMES],
            *[new_m[n_] for n_ in WEIGHT_NAMES], *[new_v[n_] for n_ in WEIGHT_NAMES])
```

```python
import functools
import math

import jax
import jax.numpy as jnp
from jax import lax
from jax.experimental import pallas as pl
from jax.experimental.pallas import tpu as pltpu

F32 = jnp.float32
BF16 = jnp.bfloat16
SDS = jax.ShapeDtypeStruct
MESH = pl.DeviceIdType.MESH

NORM_EPS = 1e-6
LRU_C = 8.0
N_DEV = 8
N_CHIP = 4
ADAM_LR, ADAM_B1, ADAM_B2, ADAM_EPS, ADAM_WD, ADAM_STEP = 0.001, 0.9, 0.999, 1e-08, 0.01, 10

NN = (((1,), (0,)), ((), ()))
NT = (((1,), (1,)), ((), ()))
TN = (((0,), (0,)), ((), ()))

SUBLANES = 8
BF16_ROWS = 16
LANES = 128
MIB = 1 << 20


def _dot(a, b, dims):
    return lax.dot_general(a, b, dims, preferred_element_type=F32)


def _blk(n, pref, align):
    if n <= pref:
        return n
    b = (pref // align) * align
    while b >= align:
        if n % b == 0:
            return b
        b -= align
    raise ValueError(f"no block of {n} aligned to {align} under {pref}")


def _cp(sem, vmem_mib):
    return pltpu.CompilerParams(dimension_semantics=sem, vmem_limit_bytes=vmem_mib * MIB)


HBM_SPEC = pl.BlockSpec(memory_space=pltpu.HBM)
MID_EIGHTHS = 5


class _Carried:
    def __init__(self, inputs, out_shapes, aliases, sem_shapes, build, has_mid=False):
        self.inputs, self.out_shapes, self.aliases = list(inputs), list(out_shapes), dict(aliases)
        self.sem_shapes, self.build, self.has_mid = list(sem_shapes), build, has_mid


def _call(body, *, name, grid, in_specs, out_specs, out_shape, scratch_shapes, compiler_params, args, carried=None):
    if carried is None:
        return pl.pallas_call(body, name=name, grid=grid, in_specs=in_specs, out_specs=out_specs, out_shape=out_shape,
                              scratch_shapes=scratch_shapes, compiler_params=compiler_params)(*args)
    n_in, n_out, n_sc = len(in_specs), len(out_shape), len(scratch_shapes)
    c_in, c_out = len(carried.inputs), len(carried.out_shapes)

    def hosted(*refs):
        ins, refs = refs[:n_in], refs[n_in:]
        c_ins, refs = refs[:c_in], refs[c_in:]
        outs, refs = refs[:n_out], refs[n_out:]
        c_outs, refs = refs[:c_out], refs[c_out:]
        scratch, c_sems = refs[:n_sc], refs[n_sc:]
        first = functools.reduce(jnp.logical_and, [pl.program_id(a) == 0 for a in range(len(grid))])
        last = functools.reduce(jnp.logical_and, [pl.program_id(a) == g - 1 for a, g in enumerate(grid)])

        @pl.when(first)
        def _():
            for start in carried.build(c_ins, c_outs, c_sems, "start"):
                start()

        if carried.has_mid:
            mid = functools.reduce(jnp.logical_and, [pl.program_id(0) == (grid[0] * MID_EIGHTHS) // 8]
                                   + [pl.program_id(a) == 0 for a in range(1, len(grid))])

            @pl.when(mid)
            def _():
                for step in carried.build(c_ins, c_outs, c_sems, "mid"):
                    step()

        body(*ins, *outs, *scratch)

        @pl.when(last)
        def _():
            for wait in carried.build(c_ins, c_outs, c_sems, "end"):
                wait()

    out = pl.pallas_call(
        hosted, name=name, grid=grid, in_specs=list(in_specs) + [HBM_SPEC] * c_in,
        out_specs=list(out_specs) + [HBM_SPEC] * c_out, out_shape=list(out_shape) + carried.out_shapes,
        scratch_shapes=list(scratch_shapes) + carried.sem_shapes,
        input_output_aliases={n_in + a: n_out + b for a, b in carried.aliases.items()},
        compiler_params=compiler_params)(*args, *carried.inputs)
    return out[:n_out], out[n_out:]


ROW_CHUNK = 128


EPILOGUE_ROWS = 128


def _chunk_rows(c):
    return pl.ds(pl.multiple_of(c * ROW_CHUNK, ROW_CHUNK), ROW_CHUNK)


def _rstd(xv):
    return lax.rsqrt(jnp.mean(xv * xv, axis=-1, keepdims=True) + NORM_EPS)


def _rms_bwd(xv, g, dn):
    r = _rstd(xv)
    xr = xv * r
    gd = g * dn
    dx = r * (gd - xr * jnp.mean(gd * xr, axis=-1, keepdims=True))
    return dx, jnp.sum(dn * xr, axis=0, keepdims=True)


def _log1p(e):
    u = 1.0 + e
    return jnp.where(u == 1.0, e, jnp.log(u) * (e / (u - 1.0)))


def _one_minus_exp(v, exp_half_v):
    series = 1.0 / 5040.0
    for coeff in (1.0 / 720.0, 1.0 / 120.0, 1.0 / 24.0, 1.0 / 6.0, 0.5, 1.0):
        series = series * v + coeff
    return jnp.where(v > -0.5, -v * series, 1.0 - exp_half_v * exp_half_v)


def _sigmoid(v):
    return 0.5 * jnp.tanh(0.5 * v) + 0.5


def _gelu_parts(g):
    k0 = math.sqrt(2.0 / math.pi)
    g2 = g * g
    t = jnp.tanh(k0 * (g + 0.044715 * g * g2))
    gel = 0.5 * g * (1.0 + t)
    gelp = 0.5 * (1.0 + t) + 0.5 * g * (1.0 - t * t) * (k0 * (1.0 + 3.0 * 0.044715 * g2))
    return gel, gelp


def _norm_proj(x, gain, w_list, out_dtypes, swiglu, name, carried=None):
    T, D = x.shape
    N = w_list[0].shape[0]
    nw = len(w_list)
    bm = _blk(T, 1024, BF16_ROWS)
    bn = _blk(N, 512, LANES)

    def body(*refs):
        x_ref, g_ref = refs[:2]
        w_refs = refs[2:2 + nw]
        n_ref = refs[2 + nw]
        o_refs = refs[3 + nw:3 + 2 * nw]
        act_ref = refs[3 + 2 * nw] if swiglu else None
        n_sc = refs[-1]

        @pl.when(pl.program_id(1) == 0)
        def _():
            def piece(p, _):
                r = pl.ds(pl.multiple_of(p * EPILOGUE_ROWS, EPILOGUE_ROWS), EPILOGUE_ROWS)
                xv = x_ref[r, :]
                nb = (xv * _rstd(xv) * g_ref[...]).astype(BF16)
                n_sc[r, :] = nb
                n_ref[r, :] = nb
                return 0
            lax.fori_loop(0, bm // EPILOGUE_ROWS, piece, 0)

        n = n_sc[...]
        outs = [_dot(n, w_ref[...], NT) for w_ref in w_refs]
        for o_ref, o in zip(o_refs, outs):
            o_ref[...] = o.astype(o_ref.dtype)
        if swiglu:
            hg, hu = outs
            act_ref[...] = (hg * _sigmoid(hg) * hu).astype(BF16)

    row = pl.BlockSpec((bm, D), lambda i, j: (i, 0))
    tile = pl.BlockSpec((bm, bn), lambda i, j: (i, j))
    n_extra = 1 if swiglu else 0
    return _call(
        body, name=name, grid=(T // bm, N // bn),
        in_specs=[row, pl.BlockSpec((1, D), lambda i, j: (0, 0))] + [pl.BlockSpec((bn, D), lambda i, j: (j, 0))] * nw,
        out_specs=[row] + [tile] * (nw + n_extra),
        out_shape=[SDS((T, D), BF16)] + [SDS((T, N), dt) for dt in out_dtypes] + [SDS((T, N), BF16)] * n_extra,
        scratch_shapes=[pltpu.VMEM((bm, D), BF16)],
        compiler_params=_cp(("arbitrary", "arbitrary"), 52),
        args=(x, gain, *w_list), carried=carried)


def _up_act(n, wu, hg, name, carried=None):
    T, D = n.shape
    F = wu.shape[0]
    bm = _blk(T, 1024, BF16_ROWS)
    bn = _blk(F, 512, LANES)

    def body(n_ref, wu_ref, hg_ref, hu_ref, act_ref):
        hu = _dot(n_ref[...], wu_ref[...], NT)
        hg = hg_ref[...].astype(F32)
        hu_ref[...] = hu.astype(BF16)
        act_ref[...] = (hg * _sigmoid(hg) * hu).astype(BF16)

    tile = pl.BlockSpec((bm, bn), lambda i, j: (i, j))
    return _call(
        body, name=name, grid=(T // bm, F // bn),
        in_specs=[pl.BlockSpec((bm, D), lambda i, j: (i, 0)), pl.BlockSpec((bn, D), lambda i, j: (j, 0)), tile],
        out_specs=[tile, tile], out_shape=[SDS((T, F), BF16)] * 2, scratch_shapes=[],
        compiler_params=_cp(("arbitrary", "arbitrary"), 40),
        args=(n, wu, hg), carried=carried)


def _mm_res(a, b, x, scale, name, carried=None):
    T, K = a.shape
    D = b.shape[1]
    bm = _blk(T, 1024, BF16_ROWS)
    bk = _blk(K, 1408, LANES)
    nk = K // bk

    def body(a_ref, b_ref, x_ref, o_ref):
        k = pl.program_id(1)

        @pl.when(k == 0)
        def _():
            o_ref[...] = jnp.zeros_like(o_ref)

        o_ref[...] += _dot(a_ref[...], b_ref[...], NN)

        @pl.when(k == nk - 1)
        def _():
            def chunk(c, _):
                r = _chunk_rows(c)
                o_ref[r, :] = x_ref[r, :] + scale * o_ref[r, :]
                return 0
            lax.fori_loop(0, bm // ROW_CHUNK, chunk, 0)

    row = pl.BlockSpec((bm, D), lambda i, k: (i, 0))
    out = _call(
        body, name=name, grid=(T // bm, nk),
        in_specs=[pl.BlockSpec((bm, bk), lambda i, k: (i, k)), pl.BlockSpec((bk, D), lambda i, k: (k, 0)), row],
        out_specs=[row], out_shape=[SDS((T, D), F32)], scratch_shapes=[],
        compiler_params=_cp(("arbitrary", "arbitrary"), 56),
        args=(a, b, x), carried=carried)
    return out[0] if carried is None else (out[0][0], out[1])


def _mm_nt(a, b, name):
    T, K = a.shape
    N = b.shape[0]
    bm = _blk(T, 1024, BF16_ROWS)
    bn = _blk(N, 512, LANES)

    def body(a_ref, b_ref, o_ref):
        o_ref[...] = _dot(a_ref[...], b_ref[...], NT)

    return pl.pallas_call(
        body, name=name, grid=(T // bm, N // bn),
        in_specs=[pl.BlockSpec((bm, K), lambda i, j: (i, 0)), pl.BlockSpec((bn, K), lambda i, j: (j, 0))],
        out_specs=pl.BlockSpec((bm, bn), lambda i, j: (i, j)), out_shape=SDS((T, N), F32),
        compiler_params=_cp(("arbitrary", "arbitrary"), 40),
    )(a, b)


def _ffn_bwd_act(dfb, wd, hg, hu, name):
    T, D = dfb.shape
    F = wd.shape[0]
    bm = _blk(T, 1024, BF16_ROWS)
    bn = _blk(F, 512, LANES)

    def body(df_ref, wd_ref, hg_ref, hu_ref, dhg_ref, dhu_ref):
        dact = _dot(df_ref[...], wd_ref[...], NT)
        hgv = hg_ref[...].astype(F32)
        huv = hu_ref[...].astype(F32)
        s = _sigmoid(hgv)
        dhu_ref[...] = (dact * (hgv * s)).astype(BF16)
        dhg_ref[...] = (dact * huv * (s * (1.0 + hgv * (1.0 - s)))).astype(BF16)

    tile = pl.BlockSpec((bm, bn), lambda i, j: (i, j))
    return pl.pallas_call(
        body, name=name, grid=(T // bm, F // bn),
        in_specs=[pl.BlockSpec((bm, D), lambda i, j: (i, 0)), pl.BlockSpec((bn, D), lambda i, j: (j, 0)), tile, tile],
        out_specs=[tile, tile], out_shape=[SDS((T, F), BF16)] * 2,
        compiler_params=_cp(("arbitrary", "arbitrary"), 40),
    )(dfb, wd, hg, hu)


def _dw_tn(a, b, bm_pref, name):
    T, M = a.shape
    N = b.shape[1]
    bm = _blk(M, bm_pref, LANES)
    tk = _blk(T, 1024, BF16_ROWS)
    nk = T // tk

    def body(a_ref, b_ref, o_ref, acc):
        k = pl.program_id(1)

        @pl.when(k == 0)
        def _():
            acc[...] = jnp.zeros_like(acc)

        acc[...] += _dot(a_ref[...], b_ref[...], TN)

        @pl.when(k == nk - 1)
        def _():
            o_ref[...] = acc[...].astype(BF16)

    return pl.pallas_call(
        body, name=name, grid=(M // bm, nk),
        in_specs=[pl.BlockSpec((tk, bm), lambda i, k: (k, i)), pl.BlockSpec((tk, N), lambda i, k: (k, 0))],
        out_specs=pl.BlockSpec((bm, N), lambda i, k: (i, 0)), out_shape=SDS((M, N), BF16),
        scratch_shapes=[pltpu.VMEM((bm, N), F32)],
        compiler_params=_cp(("arbitrary", "arbitrary"), 48),
    )(a, b)


def _mm_rmsbwd(pairs, x, gain, dx_in, bscale, name, carried=None):
    T, D = x.shape
    K = pairs[0][0].shape[1]
    npair = len(pairs)
    bm = _blk(T, 1024, BF16_ROWS)
    bk = _blk(K, 1024 // npair, LANES)
    nk = K // bk

    nchunk = bm // ROW_CHUNK

    def body(*refs):
        ab = refs[:2 * npair]
        x_hbm, g_ref, dxin_hbm, dx_ref, dxb_ref, dg_ref, x_buf, dxin_buf, sems = refs[2 * npair:]
        i = pl.program_id(0)
        k = pl.program_id(1)

        def fetch(c, slot):
            rows = pl.ds(i * bm + c * ROW_CHUNK, ROW_CHUNK)
            return (pltpu.make_async_copy(x_hbm.at[rows, :], x_buf.at[slot], sems.at[slot, 0]),
                    pltpu.make_async_copy(dxin_hbm.at[rows, :], dxin_buf.at[slot], sems.at[slot, 1]))

        @pl.when(k == 0)
        def _():
            dx_ref[...] = jnp.zeros_like(dx_ref)

        @pl.when(k == nk - 1)
        def _():
            for cp in fetch(0, 0):
                cp.start()

        for q in range(npair):
            dx_ref[...] += _dot(ab[2 * q][...], ab[2 * q + 1][...], NN)

        @pl.when(k == nk - 1)
        def _():
            @pl.when(i == 0)
            def _():
                dg_ref[...] = jnp.zeros_like(dg_ref)

            def chunk(c, _):
                slot = c % 2

                @pl.when(c + 1 < nchunk)
                def _():
                    for cp in fetch(c + 1, 1 - slot):
                        cp.start()

                for cp in fetch(c, slot):
                    cp.wait()

                def piece(p, _):
                    rb = pl.ds(pl.multiple_of(p * EPILOGUE_ROWS, EPILOGUE_ROWS), EPILOGUE_ROWS)
                    r = pl.ds(pl.multiple_of(c * ROW_CHUNK + p * EPILOGUE_ROWS, EPILOGUE_ROWS), EPILOGUE_ROWS)
                    dx, dg = _rms_bwd(x_buf[slot, rb, :], g_ref[...], dx_ref[r, :])
                    dxo = dxin_buf[slot, rb, :] + dx
                    dx_ref[r, :] = dxo
                    dxb_ref[r, :] = (bscale * dxo).astype(BF16)
                    dg_ref[...] += dg
                    return 0
                lax.fori_loop(0, ROW_CHUNK // EPILOGUE_ROWS, piece, 0)
                return 0
            lax.fori_loop(0, nchunk, chunk, 0)

    row = pl.BlockSpec((bm, D), lambda i, k: (i, 0))
    anywhere = pl.BlockSpec(memory_space=pl.ANY)
    vec = pl.BlockSpec((1, D), lambda i, k: (0, 0))
    in_specs = []
    args = []
    for a, b in pairs:
        in_specs += [pl.BlockSpec((bm, bk), lambda i, k: (i, k)), pl.BlockSpec((bk, D), lambda i, k: (k, 0))]
        args += [a, b]
    return _call(
        body, name=name, grid=(T // bm, nk),
        in_specs=in_specs + [anywhere, vec, anywhere], out_specs=[row, row, vec],
        out_shape=[SDS((T, D), F32), SDS((T, D), BF16), SDS((1, D), F32)],
        scratch_shapes=[pltpu.VMEM((2, ROW_CHUNK, D), F32), pltpu.VMEM((2, ROW_CHUNK, D), F32),
                        pltpu.SemaphoreType.DMA((2, 2))],
        compiler_params=_cp(("arbitrary", "arbitrary"), 52),
        args=(*args, x, gain, dx_in), carried=carried)


def _loss_head(x3, gain, tgt, name):
    T, D = x3.shape
    bm = _blk(T, 256, BF16_ROWS)

    def body(x_ref, g_ref, t_ref, dx_ref, dxb_ref, dg_ref, loss_ref):
        i = pl.program_id(0)
        xv = x_ref[...]
        g = g_ref[...]
        out = xv * _rstd(xv) * g
        e = out - t_ref[...]
        part = 0.5 * jnp.sum(jnp.mean(e * e, axis=-1, keepdims=True), axis=0, keepdims=True)
        dx, dg = _rms_bwd(xv, g, e * (1.0 / D))
        dx_ref[...] = dx
        dxb_ref[...] = (0.5 * dx).astype(BF16)

        @pl.when(i == 0)
        def _():
            dg_ref[...] = dg
            loss_ref[...] = jnp.broadcast_to(part, loss_ref.shape)

        @pl.when(i > 0)
        def _():
            dg_ref[...] += dg
            loss_ref[...] += jnp.broadcast_to(part, loss_ref.shape)

    row = pl.BlockSpec((bm, D), lambda i: (i, 0))
    vec = pl.BlockSpec((1, D), lambda i: (0, 0))
    return pl.pallas_call(
        body, name=name, grid=(T // bm,),
        in_specs=[row, vec, row], out_specs=[row, row, vec, pl.BlockSpec((SUBLANES, LANES), lambda i: (0, 0))],
        out_shape=[SDS((T, D), F32), SDS((T, D), BF16), SDS((1, D), F32), SDS((SUBLANES, LANES), F32)],
        compiler_params=_cp(("arbitrary",), 40),
    )(x3, gain, tgt)


R_CW, R_CB, R_BA, R_BI, R_LAM, R_SW, R_GLO, R_GSO, SMALL_ROWS = 0, 4, 5, 6, 7, 8, 11, 12, 16


def _rows(g):
    return pl.ds(pl.multiple_of(g * SUBLANES, SUBLANES), SUBLANES)


def _shift_back(prev, cur, d):
    row = lax.broadcasted_iota(jnp.int32, cur.shape, 0)
    return pltpu.roll(jnp.where(row >= SUBLANES - d, prev, cur), d, 0)


def _shift_fwd(cur, nxt, d):
    row = lax.broadcasted_iota(jnp.int32, cur.shape, 0)
    return pltpu.roll(jnp.where(row < d, nxt, cur), SUBLANES - d, 0)


def _causal_conv(ext, g, taps_ref, ntap):
    prev = ext[_rows(g), :]
    cur = ext[_rows(g + 1), :]
    out = _shift_back(prev, cur, ntap - 1) * taps_ref[0:1, :]
    for k in range(1, ntap - 1):
        out = out + _shift_back(prev, cur, ntap - 1 - k) * taps_ref[k:k + 1, :]
    return out + cur * taps_ref[ntap - 1:ntap, :]


def _scan8(A, U, reverse):
    row = lax.broadcasted_iota(jnp.int32, A.shape, 0)
    for s in (1, 2, 4):
        if reverse:
            A_sh = pltpu.roll(A, SUBLANES - s, 0)
            U_sh = pltpu.roll(U, SUBLANES - s, 0)
            m = row < SUBLANES - s
        else:
            A_sh = pltpu.roll(A, s, 0)
            U_sh = pltpu.roll(U, s, 0)
            m = row >= s
        U = jnp.where(m, A * U_sh + U, U)
        A = jnp.where(m, A * A_sh, A)
    return A, U


def _gate_pre(xc_s, w_ref, out_s, H, hd):
    for h in range(H):
        cs = slice(h * hd, (h + 1) * hd)
        out_s[:, cs] = _dot(xc_s[:, cs].astype(BF16), w_ref[h], NN)


def _lru_coeffs(pa, pi, xc, ba, bi, sp):
    ra = _sigmoid(pa + ba)
    ri = _sigmoid(pi + bi)
    log_a = (-LRU_C * ra) * sp
    a = jnp.exp(log_a)
    mult = jnp.sqrt(_one_minus_exp(2.0 * log_a, a))
    return ra, ri, a, mult


def _softplus_neg(lam):
    v = -lam
    return jnp.maximum(v, 0.0) + _log1p(jnp.exp(-jnp.abs(v)))


def _mix_fwd(z, cw, cb, wa, ba, wi, bi, lam, sw, glo, gso, name):
    T = z.shape[0]
    C = z.shape[1] // 5
    H = wa.shape[0]
    hd = C // H
    tb = _blk(T, 256, BF16_ROWS)
    ng = tb // SUBLANES
    HDR = SUBLANES

    def body(z_ref, cw_ref, cb_ref, wa_ref, ba_ref, wi_ref, bi_ref, lam_ref, sw_ref, glo_ref, gso_ref,
             y_ref, h_ref, xext, pext, xc_s, pa_s, pi_s, y_s, hcar):
        @pl.when(pl.program_id(0) == 0)
        def _():
            xext[0:HDR, :] = jnp.zeros((HDR, C), F32)
            pext[0:HDR, :] = jnp.zeros((HDR, C), F32)
            hcar[...] = jnp.zeros_like(hcar)

        def fill(g, _):
            r = _rows(g)
            re = _rows(g + 1)
            xext[re, :] = z_ref[r, 0:C]
            pext[re, :] = z_ref[r, 3 * C:4 * C] * z_ref[r, 4 * C:5 * C]
            return 0
        lax.fori_loop(0, ng, fill, 0)

        def conv(g, _):
            xc_s[_rows(g), :] = _causal_conv(xext, g, cw_ref, 4) + cb_ref[...]
            return 0
        lax.fori_loop(0, ng, conv, 0)

        _gate_pre(xc_s, wa_ref, pa_s, H, hd)
        _gate_pre(xc_s, wi_ref, pi_s, H, hd)
        sp = _softplus_neg(lam_ref[...])

        def group(g, hprev):
            r = _rows(g)
            xc = xc_s[r, :]
            _, ri, a, mult = _lru_coeffs(pa_s[r, :], pi_s[r, :], xc, ba_ref[...], bi_ref[...], sp)
            A, U = _scan8(a, mult * (ri * xc), reverse=False)
            hh = A * hprev + U
            h_ref[r, :] = hh
            gel, _ = _gelu_parts(z_ref[r, C:2 * C])
            y_lru = hh * gel
            y_s[r, 0:C] = y_lru * _rstd(y_lru) * glo_ref[...]
            y_sc = z_ref[r, 2 * C:3 * C] * _causal_conv(pext, g, sw_ref, 3)
            y_s[r, C:2 * C] = y_sc * _rstd(y_sc) * gso_ref[...]
            return jnp.broadcast_to(hh[SUBLANES - 1:SUBLANES, :], hh.shape)
        hcar[...] = lax.fori_loop(0, ng // 2, lambda t, hp: group(2 * t + 1, group(2 * t, hp)), hcar[...])

        xext[0:HDR, :] = xext[tb:tb + HDR, :]
        pext[0:HDR, :] = pext[tb:tb + HDR, :]

        def cast(g, _):
            r = pl.ds(pl.multiple_of(g * BF16_ROWS, BF16_ROWS), BF16_ROWS)
            y_ref[r, :] = y_s[r, :].astype(BF16)
            return 0
        lax.fori_loop(0, tb // BF16_ROWS, cast, 0)

    full = lambda shape: pl.BlockSpec(shape, lambda i: (0,) * len(shape))
    blk = lambda w: pl.BlockSpec((tb, w), lambda i: (i, 0))
    ext = pltpu.VMEM((tb + HDR, C), F32)
    tile = pltpu.VMEM((tb, C), F32)
    return pl.pallas_call(
        body, name=name, grid=(T // tb,),
        in_specs=[blk(5 * C), full((4, C)), full((1, C)), full((H, hd, hd)), full((1, C)), full((H, hd, hd)),
                  full((1, C)), full((1, C)), full((3, C)), full((1, C)), full((1, C))],
        out_specs=[blk(2 * C), blk(C)],
        out_shape=[SDS((T, 2 * C), BF16), SDS((T, C), F32)],
        scratch_shapes=[ext, ext, tile, tile, tile, pltpu.VMEM((tb, 2 * C), F32), pltpu.VMEM((SUBLANES, C), F32)],
        compiler_params=_cp(("arbitrary",), 40),
    )(z, cw, cb, wa, ba, wi, bi, lam, sw, glo, gso)


def _mix_bwd(z, h, dy, cw, cb, wa, ba, wi, bi, lam, sw, glo, gso, name):
    T = z.shape[0]
    C = z.shape[1] // 5
    H = wa.shape[0]
    hd = C // H
    tb = _blk(T, 256, BF16_ROWS)
    nb = T // tb
    ng = tb // SUBLANES
    HDR = SUBLANES
    N_ACC = 13

    def body(z_ref, zp_ref, h_ref, hp_ref, dy_ref, cw_ref, cb_ref, wa_ref, ba_ref, wi_ref, bi_ref, lam_ref,
             sw_ref, glo_ref, gso_ref, dz_ref, small_ref, dwa_ref, dwi_ref,
             xext, pext, hext, dqext, dxcext, bext, xc_s, pa_s, pi_s, a_s, m_s, ri_s, dh_s, dpa_s, dpi_s,
             dz_s, acc_s, bcar):
        i = pl.program_id(0)
        first_rows = i == nb - 1

        @pl.when(i == 0)
        def _():
            dqext[tb:tb + HDR, :] = jnp.zeros((HDR, C), F32)
            dxcext[tb:tb + HDR, :] = jnp.zeros((HDR, C), F32)
            bcar[...] = jnp.zeros_like(bcar)
            acc_s[...] = jnp.zeros_like(acc_s)
            dwa_ref[...] = jnp.zeros_like(dwa_ref)
            dwi_ref[...] = jnp.zeros_like(dwi_ref)

        zero = jnp.zeros((HDR, C), F32)
        xext[0:HDR, :] = jnp.where(first_rows, zero, zp_ref[:, 0:C])
        pext[0:HDR, :] = jnp.where(first_rows, zero, zp_ref[:, 3 * C:4 * C] * zp_ref[:, 4 * C:5 * C])
        hext[0:HDR, :] = jnp.where(first_rows, zero, hp_ref[...])

        def fill(g, _):
            r = _rows(g)
            re = _rows(g + 1)
            xext[re, :] = z_ref[r, 0:C]
            pext[re, :] = z_ref[r, 3 * C:4 * C] * z_ref[r, 4 * C:5 * C]
            hext[re, :] = h_ref[r, :]
            return 0
        lax.fori_loop(0, ng, fill, 0)

        def conv(g, _):
            xc_s[_rows(g), :] = _causal_conv(xext, g, cw_ref, 4) + cb_ref[...]
            return 0
        lax.fori_loop(0, ng, conv, 0)

        _gate_pre(xc_s, wa_ref, pa_s, H, hd)
        _gate_pre(xc_s, wi_ref, pi_s, H, hd)
        sp = _softplus_neg(lam_ref[...])
        dsp_dlam = -jax.nn.sigmoid(-lam_ref[...])

        def add_acc(k, v):
            acc_s[k] += v

        def p1(g, _):
            r = _rows(g)
            xc = xc_s[r, :]
            _, ri, a, mult = _lru_coeffs(pa_s[r, :], pi_s[r, :], xc, ba_ref[...], bi_ref[...], sp)
            a_s[r, :] = a
            m_s[r, :] = mult
            ri_s[r, :] = ri
            hh = h_ref[r, :]
            gel, gelp = _gelu_parts(z_ref[r, C:2 * C])
            y_lru = hh * gel
            dnl = dy_ref[r, 0:C]
            rl = _rstd(y_lru)
            ylr = y_lru * rl
            gd = glo_ref[...] * dnl
            dy_lru = rl * (gd - ylr * jnp.mean(gd * ylr, axis=-1, keepdims=True))
            add_acc(R_GLO, dnl * ylr)
            dz_s[r, C:2 * C] = dy_lru * hh * gelp
            dh = dy_lru * gel
            dh_s[r, :] = dh

            q = _causal_conv(pext, g, sw_ref, 3)
            scb = z_ref[r, 2 * C:3 * C]
            y_sc = scb * q
            dns = dy_ref[r, C:2 * C]
            rs = _rstd(y_sc)
            ysr = y_sc * rs
            gs = gso_ref[...] * dns
            dy_sc = rs * (gs - ysr * jnp.mean(gs * ysr, axis=-1, keepdims=True))
            add_acc(R_GSO, dns * ysr)
            dz_s[r, 2 * C:3 * C] = dy_sc * q
            dqext[r, :] = dy_sc * scb
            return 0
        lax.fori_loop(0, ng, p1, 0, unroll=2)

        bext[tb:tb + HDR, :] = bcar[...]

        def p2(j, carry):
            g = ng - 1 - j
            r = _rows(g)
            a = a_s[r, :]
            A, U = _scan8(a, a * dh_s[r, :], reverse=True)
            bb = A * carry + U
            bext[r, :] = bb
            return jnp.broadcast_to(bb[0:1, :], bb.shape)
        bcar[...] = lax.fori_loop(0, ng, p2, bcar[...])

        def p3(g, _):
            r = _rows(g)
            rn = _rows(g + 1)
            G = dh_s[r, :] + _shift_fwd(bext[r, :], bext[rn, :], 1)
            hm1 = _shift_back(hext[r, :], hext[rn, :], 1)
            a = a_s[r, :]
            mult = m_s[r, :]
            ri = ri_s[r, :]
            xc = xc_s[r, :]
            ra = _sigmoid(pa_s[r, :] + ba_ref[...])
            dxcext[r, :] = G * mult * ri
            dri = G * mult * xc
            dmult = G * ri * xc
            dlog_a = (G * hm1) * a - dmult * (a * a) / mult
            add_acc(R_LAM, dlog_a * (-LRU_C * ra) * dsp_dlam)
            dpa = dlog_a * (-LRU_C * sp) * ra * (1.0 - ra)
            dpi = dri * ri * (1.0 - ri)
            add_acc(R_BA, dpa)
            add_acc(R_BI, dpi)
            dpa_s[r, :] = dpa
            dpi_s[r, :] = dpi
            return 0
        lax.fori_loop(0, ng, p3, 0)

        for hh_ in range(H):
            cs = slice(hh_ * hd, (hh_ + 1) * hd)
            dpa_b = dpa_s[:, cs].astype(BF16)
            dpi_b = dpi_s[:, cs].astype(BF16)
            xc_b = xc_s[:, cs].astype(BF16)
            dxcext[0:tb, cs] += _dot(dpa_b, wa_ref[hh_], NT) + _dot(dpi_b, wi_ref[hh_], NT)
            dwa_ref[hh_] += _dot(xc_b, dpa_b, TN)
            dwi_ref[hh_] += _dot(xc_b, dpi_b, TN)

        def p4(g, _):
            r = _rows(g)
            rn = _rows(g + 1)
            dxc = dxcext[r, :]
            dxc_n = dxcext[rn, :]
            x_p = xext[r, :]
            x_c = xext[rn, :]
            add_acc(R_CB, dxc)
            dlx = dxc * cw_ref[3:4, :]
            add_acc(R_CW + 3, dxc * x_c)
            for d in range(1, 4):
                dlx = dlx + _shift_fwd(dxc, dxc_n, d) * cw_ref[3 - d:4 - d, :]
                add_acc(R_CW + 3 - d, dxc * _shift_back(x_p, x_c, d))
            dz_s[r, 0:C] = dlx

            dq = dqext[r, :]
            dq_n = dqext[rn, :]
            p_p = pext[r, :]
            p_c = pext[rn, :]
            dp = dq * sw_ref[2:3, :]
            add_acc(R_SW + 2, dq * p_c)
            for d in range(1, 3):
                dp = dp + _shift_fwd(dq, dq_n, d) * sw_ref[2 - d:3 - d, :]
                add_acc(R_SW + 2 - d, dq * _shift_back(p_p, p_c, d))
            dz_s[r, 3 * C:4 * C] = dp * z_ref[r, 4 * C:5 * C]
            dz_s[r, 4 * C:5 * C] = dp * z_ref[r, 3 * C:4 * C]
            return 0
        lax.fori_loop(0, ng, p4, 0)

        dqext[tb:tb + HDR, :] = dqext[0:HDR, :]
        dxcext[tb:tb + HDR, :] = dxcext[0:HDR, :]

        def cast(g, _):
            r = pl.ds(pl.multiple_of(g * BF16_ROWS, BF16_ROWS), BF16_ROWS)
            dz_ref[r, :] = dz_s[r, :].astype(BF16)
            return 0
        lax.fori_loop(0, tb // BF16_ROWS, cast, 0)

        @pl.when(i == nb - 1)
        def _():
            small_ref[...] = jnp.zeros_like(small_ref)
            for k in range(N_ACC):
                small_ref[k:k + 1, :] = jnp.sum(acc_s[k], axis=0, keepdims=True)

    tpg = tb // SUBLANES
    full = lambda shape: pl.BlockSpec(shape, lambda i: (0,) * len(shape))
    blk = lambda w: pl.BlockSpec((tb, w), lambda i: (nb - 1 - i, 0))
    prev = lambda w: pl.BlockSpec((SUBLANES, w), lambda i: (jnp.maximum((nb - 1 - i) * tpg - 1, 0), 0))
    ext = pltpu.VMEM((tb + HDR, C), F32)
    tile = pltpu.VMEM((tb, C), F32)
    return pl.pallas_call(
        body, name=name, grid=(nb,),
        in_specs=[blk(5 * C), prev(5 * C), blk(C), prev(C), blk(2 * C), full((4, C)), full((1, C)), full((H, hd, hd)),
                  full((1, C)), full((H, hd, hd)), full((1, C)), full((1, C)), full((3, C)), full((1, C)), full((1, C))],
        out_specs=[blk(5 * C), full((SMALL_ROWS, C)), full((H, hd, hd)), full((H, hd, hd))],
        out_shape=[SDS((T, 5 * C), BF16), SDS((SMALL_ROWS, C), F32), SDS((H, hd, hd), F32), SDS((H, hd, hd), F32)],
        scratch_shapes=[ext] * 6 + [tile] * 9 + [pltpu.VMEM((tb, 5 * C), F32), pltpu.VMEM((N_ACC, SUBLANES, C), F32),
                                                pltpu.VMEM((SUBLANES, C), F32)],
        compiler_params=_cp(("arbitrary",), 56),
    )(z, z, h, h, dy, cw, cb, wa, ba, wi, bi, lam, sw, glo, gso)


def _add_slabs(terms, out_dtype, name):
    R, Ccols = terms[0].shape
    br = _blk(R, 512, BF16_ROWS)
    n = len(terms)

    def body(*refs):
        s = refs[0][...].astype(F32)
        for t_ref in refs[1:n]:
            s = s + t_ref[...].astype(F32)
        refs[n][...] = s.astype(out_dtype)

    spec = pl.BlockSpec((br, Ccols), lambda i: (i, 0))
    return pl.pallas_call(
        body, name=name, grid=(R // br,), in_specs=[spec] * n, out_specs=spec, out_shape=SDS((R, Ccols), out_dtype),
        compiler_params=_cp(("arbitrary",), 40),
    )(*terms)


def _final_grad(sb, lb, chip, name):
    _, R, Ccols = sb.shape
    br = _blk(R, 512, BF16_ROWS)

    def body(chip_ref, sb_ref, l0, l1, l2, o_ref):
        s = sb_ref[0].astype(F32)
        for t_ref in (l0, l1, l2):
            s = s + t_ref[0].astype(F32)
        o_ref[...] = s

    lspec = lambda k: pl.BlockSpec((1, br, Ccols), lambda i, c: (k, i, 0))
    return pl.pallas_call(
        body, name=name,
        grid_spec=pltpu.PrefetchScalarGridSpec(
            num_scalar_prefetch=1, grid=(R // br,),
            in_specs=[pl.BlockSpec((1, br, Ccols), lambda i, c: (c[0], i, 0)), lspec(0), lspec(1), lspec(2)],
            out_specs=pl.BlockSpec((br, Ccols), lambda i, c: (i, 0))),
        out_shape=SDS((R, Ccols), F32),
        compiler_params=_cp(("arbitrary",), 40),
    )(chip, sb, lb, lb, lb)


def _adamw(w, g, m, v, name):
    R, Ccols = w.shape
    br = _blk(R, 256, SUBLANES)
    c1 = 1.0 - ADAM_B1 ** ADAM_STEP
    c2 = 1.0 - ADAM_B2 ** ADAM_STEP

    def body(w_ref, g_ref, m_ref, v_ref, d_ref, nm_ref, nv_ref):
        gv = g_ref[...]
        nm = ADAM_B1 * m_ref[...] + (1.0 - ADAM_B1) * gv
        nv = ADAM_B2 * v_ref[...] + (1.0 - ADAM_B2) * (gv * gv)
        nm_ref[...] = nm
        nv_ref[...] = nv
        d_ref[...] = -ADAM_LR * ((nm / c1) / (jnp.sqrt(nv / c2) + ADAM_EPS) + ADAM_WD * w_ref[...])

    spec = pl.BlockSpec((br, Ccols), lambda i: (i, 0))
    return pl.pallas_call(
        body, name=name, grid=(R // br,), in_specs=[spec] * 4, out_specs=[spec] * 3,
        out_shape=[SDS((R, Ccols), F32)] * 3, compiler_params=_cp(("arbitrary",), 40),
    )(w, g, m, v)


def _place():
    return lax.axis_index("x"), lax.axis_index("y"), lax.axis_index("c")


def _dev_rows(ref, dev, rows):
    return ref.at[pl.ds((4 * dev[0] + 2 * dev[1] + dev[2]) * rows, rows), :]


def _remote(src, dst, send_sem, recv_sem, to):
    return pltpu.make_async_remote_copy(src_ref=src, dst_ref=dst, send_sem=send_sem, recv_sem=recv_sem,
                                        device_id=to, device_id_type=MESH)


SAME_CORE_AND_SIBLING = ((0, 0, 1), (1, 0, 0), (0, 1, 0), (1, 1, 0))
ALL_OTHERS = SAME_CORE_AND_SIBLING + ((1, 0, 1), (0, 1, 1), (1, 1, 1))


def _merge_phases(a, b):
    na_in, na_out, na_sem = len(a.inputs), len(a.out_shapes), len(a.sem_shapes)

    def build(ins, outs, sems, stage):
        return (a.build(ins[:na_in], outs[:na_out], sems[:na_sem], stage)
                + b.build(ins[na_in:], outs[na_out:], sems[na_sem:], stage))

    aliases = dict(a.aliases)
    aliases.update({na_in + i: na_out + o for i, o in b.aliases.items()})
    return _Carried(a.inputs + b.inputs, a.out_shapes + b.out_shapes, aliases, a.sem_shapes + b.sem_shapes, build,
                    has_mid=a.has_mid or b.has_mid)


def _ag_direct_phase(slab, pieces, flips):
    W = slab.shape[1]
    n = len(pieces)
    npeer = len(flips)

    def build(ins, outs, sems, stage):
        if stage == "mid":
            return []
        starting = stage == "start"
        (slab_ref,) = ins
        send_sems, recv_sems, local_sems = sems
        x, y, c = _place()
        me = (x, y, c)
        peers = [tuple(1 - v if f else v for v, f in zip(me, flip)) for flip in flips]
        todo = []
        for p, (off, rows) in enumerate(pieces):
            src = slab_ref.at[pl.ds(off, rows), :]
            mine = pltpu.make_async_copy(src, _dev_rows(outs[p], me, rows), local_sems.at[p])
            todo.append(mine.start if starting else mine.wait)
            for k, peer in enumerate(peers):
                snd = _remote(src, _dev_rows(outs[p], me, rows), send_sems.at[k, p], recv_sems.at[k, p], peer)
                if starting:
                    todo.append(snd.start)
                else:
                    theirs = _dev_rows(outs[p], peer, rows)
                    rcv = _remote(theirs, theirs, send_sems.at[k, p], recv_sems.at[k, p], me)
                    todo += [rcv.wait_recv, snd.wait_send]
        return todo

    dma = pltpu.SemaphoreType.DMA
    return _Carried([slab], [SDS((N_DEV * rows, W), slab.dtype) for _, rows in pieces], {},
                    [dma((npeer, n)), dma((npeer, n)), dma((n,))], build)


def _ag_two_level_phase(slab, pieces):
    W = slab.shape[1]
    n = len(pieces)

    def build(ins, outs, sems, stage):
        (slab_ref,) = ins
        send_sems, recv_sems, local_sems = sems
        x, y, c = _place()
        me, sibling = (x, y, c), (x, y, 1 - c)
        chips = [(1 - x, y), (x, 1 - y), (1 - x, 1 - y)]
        todo = []
        for p, (off, rows) in enumerate(pieces):
            src = slab_ref.at[pl.ds(off, rows), :]
            own = _dev_rows(outs[p], me, rows)
            landed = [_dev_rows(outs[p], (*chip, c), rows) for chip in chips]

            def mine():
                return pltpu.make_async_copy(src, own, local_sems.at[p])

            def first():
                return [_remote(src, own, send_sems.at[k, p], recv_sems.at[k, p], to)
                        for k, to in enumerate([sibling] + [(*chip, c) for chip in chips])]

            def passed():
                return [_remote(blk, blk, send_sems.at[4 + j, p], recv_sems.at[4 + j, p], sibling)
                        for j, blk in enumerate(landed)]

            def arrival(k, blk):
                return _remote(blk, blk, send_sems.at[k, p], recv_sems.at[k, p], me).wait_recv

            if stage == "start":
                todo += [mine().start] + [cp.start for cp in first()]
            elif stage == "mid":
                for j, (blk, fwd) in enumerate(zip(landed, passed())):
                    todo += [arrival(1 + j, blk), fwd.start]
            else:
                theirs = [_dev_rows(outs[p], sibling, rows)] + [_dev_rows(outs[p], (*chip, 1 - c), rows) for chip in chips]
                todo += [arrival(k, blk) for k, blk in zip((0, 4, 5, 6), theirs)]
                todo += [cp.wait_send for cp in first() + passed()] + [mine().wait]
        return todo

    dma = pltpu.SemaphoreType.DMA
    return _Carried([slab], [SDS((N_DEV * rows, W), slab.dtype) for _, rows in pieces], {},
                    [dma((7, n)), dma((7, n)), dma((n,))], build, has_mid=True)


def _ag_forward_phase(gathered, pieces):
    n = len(pieces)

    def build(ins, outs, sems, stage):
        if stage == "mid":
            return []
        starting = stage == "start"
        send_sems, recv_sems = sems
        x, y, c = _place()
        me, sibling = (x, y, c), (x, y, 1 - c)
        chips = [(1 - x, y), (x, 1 - y), (1 - x, 1 - y)]
        todo = []
        for p, (_, rows) in enumerate(pieces):
            for j, chip in enumerate(chips):
                snd = _remote(_dev_rows(ins[p], (*chip, c), rows), _dev_rows(outs[p], (*chip, c), rows),
                              send_sems.at[j, p], recv_sems.at[j, p], sibling)
                if starting:
                    todo.append(snd.start)
                else:
                    theirs = _dev_rows(outs[p], (*chip, 1 - c), rows)
                    rcv = _remote(theirs, theirs, send_sems.at[j, p], recv_sems.at[j, p], me)
                    todo += [rcv.wait_recv, snd.wait_send]
        return todo

    dma = pltpu.SemaphoreType.DMA
    return _Carried(gathered, [SDS(g.shape, g.dtype) for g in gathered], {p: p for p in range(n)},
                    [dma((3, n)), dma((3, n))], build)


def _rs_chips_phase(sb):
    _, R, W = sb.shape

    def build(ins, outs, sems, stage):
        if stage == "mid":
            return []
        (sb_ref,), (land_ref,) = ins, outs
        send_sems, recv_sems = sems
        x, y, c = _place()
        chips = [(1 - x, y), (x, 1 - y), (1 - x, 1 - y)]
        cps = [_remote(sb_ref.at[2 * chip[0] + chip[1]], land_ref.at[j], send_sems.at[j], recv_sems.at[j], (*chip, c))
               for j, chip in enumerate(chips)]
        if stage == "start":
            return [cp.start for cp in cps]
        return [cp.wait_recv for cp in cps] + [cp.wait_send for cp in cps]

    dma = pltpu.SemaphoreType.DMA
    return _Carried([sb], [SDS((3, R, W), sb.dtype)], {}, [dma((3,)), dma((3,))], build)


def _allgather(slab, pieces, name):
    R, W = slab.shape
    n = len(pieces)
    assert sum(rows for _, rows in pieces) == R

    def body(slab_ref, *refs):
        outs = refs[:n]
        send_sems, recv_sems, local_sems = refs[n:]
        x, y, c = _place()
        me, sibling = (x, y, c), (x, y, 1 - c)
        chips = [(1 - x, y), (x, 1 - y), (1 - x, 1 - y)]

        def dst_rows(p, origin):
            rows = pieces[p][1]
            start = (4 * origin[0] + 2 * origin[1] + origin[2]) * rows
            return outs[p].at[pl.ds(start, rows), :]

        def copies(k, origin, to, from_slab):
            out = []
            for p, (off, rows) in enumerate(pieces):
                dst = dst_rows(p, origin)
                src = slab_ref.at[pl.ds(off, rows), :] if from_slab else dst
                out.append(pltpu.make_async_remote_copy(
                    src_ref=src, dst_ref=dst, send_sem=send_sems.at[k, p], recv_sem=recv_sems.at[k, p],
                    device_id=to, device_id_type=MESH))
            return out

        mine = [pltpu.make_async_copy(slab_ref.at[pl.ds(off, rows), :], dst_rows(p, me), local_sems.at[p])
                for p, (off, rows) in enumerate(pieces)]
        for cp in mine:
            cp.start()
        first = copies(0, me, sibling, True)
        for j, chip in enumerate(chips):
            first += copies(1 + j, me, (*chip, c), True)
        for cp in first:
            cp.start()
        passed = []
        for j, chip in enumerate(chips):
            for cp in copies(1 + j, (*chip, c), me, False):
                cp.wait_recv()
            fwd = copies(4 + j, (*chip, c), sibling, False)
            for cp in fwd:
                cp.start()
            passed += fwd
        for cp in copies(0, sibling, me, False):
            cp.wait_recv()
        for j, chip in enumerate(chips):
            for cp in copies(4 + j, (*chip, 1 - c), me, False):
                cp.wait_recv()
        for cp in first + passed:
            cp.wait_send()
        for cp in mine:
            cp.wait()

    return pl.pallas_call(
        body, name=name,
        in_specs=[HBM_SPEC], out_specs=[HBM_SPEC] * n,
        out_shape=[SDS((N_DEV * rows, W), slab.dtype) for _, rows in pieces],
        scratch_shapes=[pltpu.SemaphoreType.DMA((7, n)), pltpu.SemaphoreType.DMA((7, n)), pltpu.SemaphoreType.DMA((n,))],
    )(slab)


def _rs_sibling(grads, pieces, name):
    W = grads[0].shape[1]
    R = sum(rows for _, rows in pieces)
    n = len(pieces)
    dt = grads[0].dtype
    max_rows = max(rows for _, rows in pieces)
    steps = [(q, p) for q in range(N_CHIP) for p in range(n)]
    ns = len(steps)
    ADD_ROWS = 64
    assert all(rows % ADD_ROWS == 0 for _, rows in pieces)

    def body(*refs):
        g_refs = refs[:n]
        sb_ref, mine_buf, send_buf, land_buf, out_buf, in_sems, out_sems, send_sems, recv_sems, credit = refs[n:]
        x, y, c = _place()
        sibling = (x, y, 1 - c)

        def loads(s):
            q, p = steps[s]
            rows = pieces[p][1]
            slot = s % 2
            mine = g_refs[p].at[pl.ds((2 * q + c) * rows, rows), :]
            theirs = g_refs[p].at[pl.ds((2 * q + 1 - c) * rows, rows), :]
            return (pltpu.make_async_copy(mine, mine_buf.at[slot, pl.ds(0, rows), :], in_sems.at[slot, 0]),
                    pltpu.make_async_copy(theirs, send_buf.at[slot, pl.ds(0, rows), :], in_sems.at[slot, 1]))

        def send(s):
            rows = pieces[steps[s][1]][1]
            slot = s % 2
            return pltpu.make_async_remote_copy(
                src_ref=send_buf.at[slot, pl.ds(0, rows), :], dst_ref=land_buf.at[slot, pl.ds(0, rows), :],
                send_sem=send_sems.at[slot], recv_sem=recv_sems.at[slot], device_id=sibling, device_id_type=MESH)

        def store(s):
            q, p = steps[s]
            off, rows = pieces[p]
            slot = s % 2
            return pltpu.make_async_copy(out_buf.at[slot, pl.ds(0, rows), :], sb_ref.at[q, pl.ds(off, rows), :],
                                         out_sems.at[slot])

        def start_send(s):
            for cp in loads(s):
                cp.wait()
            if s >= 2:
                pl.semaphore_wait(credit.at[s % 2], 1)
            send(s).start()

        for s in range(min(2, ns)):
            for cp in loads(s):
                cp.start()
        start_send(0)
        for s in range(ns):
            slot = s % 2
            rows = pieces[steps[s][1]][1]
            if s + 1 < ns:
                start_send(s + 1)
            send(s).wait_recv()
            if s >= 2:
                store(s - 2).wait()

            def add(k, _, slot=slot):
                r = pl.ds(pl.multiple_of(k * ADD_ROWS, ADD_ROWS), ADD_ROWS)
                out_buf[slot, r, :] = (mine_buf[slot, r, :].astype(F32) + land_buf[slot, r, :].astype(F32)).astype(dt)
                return 0
            lax.fori_loop(0, rows // ADD_ROWS, add, 0)
            if s + 2 < ns:
                pl.semaphore_signal(credit.at[slot], inc=1, device_id=sibling, device_id_type=MESH)
            store(s).start()
            send(s).wait_send()
            if s + 2 < ns:
                for cp in loads(s + 2):
                    cp.start()
        for s in range(max(ns - 2, 0), ns):
            store(s).wait()

    buf = pltpu.VMEM((2, max_rows, W), dt)
    return pl.pallas_call(
        body, name=name,
        in_specs=[HBM_SPEC] * n, out_specs=HBM_SPEC,
        out_shape=SDS((N_CHIP, R, W), dt),
        scratch_shapes=[buf, buf, buf, buf, pltpu.SemaphoreType.DMA((2, 2)), pltpu.SemaphoreType.DMA((2,)),
                        pltpu.SemaphoreType.DMA((2,)), pltpu.SemaphoreType.DMA((2,)), pltpu.SemaphoreType.REGULAR((2,))],
        compiler_params=pltpu.CompilerParams(vmem_limit_bytes=40 * MIB),
    )(*grads)


SMALL_NAMES = ("ffn1_norm", "mix_norm", "ffn2_norm", "final_norm", "lru_conv_w", "lru_conv_b", "lru_w_a", "lru_b_a",
               "lru_w_i", "lru_b_i", "lru_lambda", "sc_conv_w", "lru_out_norm", "sc_out_norm")
WEIGHT_NAMES = ("ffn1_norm", "ffn1_w_gate", "ffn1_w_up", "ffn1_w_down", "mix_norm", "w_in", "lru_conv_w", "lru_conv_b",
                "lru_w_a", "lru_b_a", "lru_w_i", "lru_b_i", "lru_lambda", "sc_conv_w", "lru_out_norm", "sc_out_norm",
                "w_out", "ffn2_norm", "ffn2_w_gate", "ffn2_w_up", "ffn2_w_down", "final_norm")
BIG = (("ffn1_w_gate", True), ("ffn1_w_up", True), ("ffn1_w_down", False), ("ffn2_w_gate", True), ("ffn2_w_up", True),
       ("ffn2_w_down", False), ("w_in", True), ("w_out", False))


SLAB_ROW_ALIGN = 256


def _pack_rows(parts, width):
    rows, counts = [], []
    for p in parts:
        flat = p.reshape(-1)
        nr = -(-flat.shape[0] // width)
        nr = -(-nr // SUBLANES) * SUBLANES
        rows.append(jnp.pad(flat, (0, nr * width - flat.shape[0])).reshape(nr, width))
        counts.append(nr)
    total = sum(counts)
    pad = -(-total // SLAB_ROW_ALIGN) * SLAB_ROW_ALIGN - total
    if pad:
        rows.append(jnp.zeros((pad, width), rows[0].dtype))
    return jnp.concatenate(rows, axis=0), counts


def _stack_rows(blocks):
    pieces, off = [], 0
    for b in blocks:
        pieces.append((off, b.shape[0]))
        off += b.shape[0]
    return jnp.concatenate(blocks, axis=0), pieces


def _unpack_rows(slab, counts, shapes):
    out, r = [], 0
    for nr, shape in zip(counts, shapes):
        size = math.prod(shape)
        out.append(slab[r:r + nr].reshape(-1)[:size].reshape(shape))
        r += nr
    return out


def kernel(x, ffn1_norm, ffn1_w_gate, ffn1_w_up, ffn1_w_down, mix_norm, w_in, lru_conv_w, lru_conv_b, lru_w_a, lru_b_a, lru_w_i, lru_b_i, lru_lambda, sc_conv_w, lru_out_norm, sc_out_norm, w_out, ffn2_norm, ffn2_w_gate, ffn2_w_up, ffn2_w_down, final_norm, loss_target, m_ffn1_norm, m_ffn1_w_gate, m_ffn1_w_up, m_ffn1_w_down, m_mix_norm, m_w_in, m_lru_conv_w, m_lru_conv_b, m_lru_w_a, m_lru_b_a, m_lru_w_i, m_lru_b_i, m_lru_lambda, m_sc_conv_w, m_lru_out_norm, m_sc_out_norm, m_w_out, m_ffn2_norm, m_ffn2_w_gate, m_ffn2_w_up, m_ffn2_w_down, m_final_norm, v_ffn1_norm, v_ffn1_w_gate, v_ffn1_w_up, v_ffn1_w_down, v_mix_norm, v_w_in, v_lru_conv_w, v_lru_conv_b, v_lru_w_a, v_lru_b_a, v_lru_w_i, v_lru_b_i, v_lru_lambda, v_sc_conv_w, v_lru_out_norm, v_sc_out_norm, v_w_out, v_ffn2_norm, v_ffn2_w_gate, v_ffn2_w_up, v_ffn2_w_down, v_final_norm):
    a = dict(locals())
    w = {n: a[n] for n in WEIGHT_NAMES}
    m = {n: a["m_" + n] for n in WEIGHT_NAMES}
    v = {n: a["v_" + n] for n in WEIGHT_NAMES}
    ax, ay, ac = _place()
    dev = 4 * ax + 2 * ay + ac
    chip = (2 * ax + ay).astype(jnp.int32).reshape(1)

    x0 = x[0]
    tgt = loss_target[0]
    T, D = x0.shape
    C = D // 2
    H, hd = lru_w_a.shape[1], lru_w_a.shape[2]
    CL = lru_conv_w.shape[2]

    shards = []
    for name, transposed in BIG:
        s = w[name][0]
        shards.append((s.T if transposed else s).astype(BF16))
    taps = jnp.concatenate([lru_conv_w[0], sc_conv_w[0], jnp.zeros((1, CL), F32)], axis=0)
    taps_row = lax.bitcast_convert_type(taps, BF16).reshape(1, -1)
    taps_blk = jnp.pad(taps_row, ((0, BF16_ROWS - 1), (0, D - taps_row.shape[1])))
    s_wg1, s_wu1, s_wd1, s_wg2, s_wu2, s_wd2, s_win, s_wout = shards
    slab_g1, pcs_g1 = _stack_rows([s_wg1])
    slab_u1, pcs_u1 = _stack_rows([s_wu1])
    slab_d1, pcs_d1 = _stack_rows([s_wd1])
    slab_mw, pcs_mw = _stack_rows([s_win, s_wout, taps_blk])
    slab_gu2, pcs_gu2 = _stack_rows([s_wg2, s_wu2])
    slab_d2, pcs_d2 = _stack_rows([s_wd2])
    (wg1,) = _allgather(slab_g1, pcs_g1, "allgather_ffn1_gate")

    g1, gm, g3 = ffn1_norm, mix_norm, ffn2_norm
    phase = _merge_phases(_ag_two_level_phase(slab_u1, pcs_u1), _ag_direct_phase(slab_d1, pcs_d1, SAME_CORE_AND_SIBLING))
    (n1, hg1), got = _norm_proj(x0, g1, [wg1], [BF16], False, "ffn1_gate", carried=phase)
    wu1, d1 = got[0], got[1:]
    phase = _merge_phases(_ag_forward_phase(d1, pcs_d1), _ag_direct_phase(slab_mw, pcs_mw, SAME_CORE_AND_SIBLING))
    (hu1, act1), got = _up_act(n1, wu1, hg1, "ffn1_up", carried=phase)
    wd1, mixw = got[0], got[1:]
    phase = _merge_phases(_ag_forward_phase(mixw, pcs_mw), _ag_direct_phase(slab_gu2, pcs_gu2, SAME_CORE_AND_SIBLING))
    x1, got = _mm_res(act1, wd1, x0, 0.5, "ffn1_down", carried=phase)
    (win, wout, taps_all), gu2 = got[:3], got[3:]
    phase = _merge_phases(_ag_forward_phase(gu2, pcs_gu2), _ag_direct_phase(slab_d2, pcs_d2, SAME_CORE_AND_SIBLING))
    (n2, z), got = _norm_proj(x1, gm, [win], [F32], False, "in_proj", carried=phase)
    (wg2, wu2), d2 = got[:2], got[2:]
    taps_all = taps_all.reshape(N_DEV, BF16_ROWS, D)[:, 0, :2 * SUBLANES * CL].reshape(N_DEV, SUBLANES, CL, 2)
    taps_all = lax.bitcast_convert_type(taps_all, F32)
    taps_all = taps_all.transpose(1, 0, 2).reshape(SUBLANES, N_DEV * CL)
    cw, sw = taps_all[0:4], taps_all[4:7]

    gf = final_norm.reshape(1, D)
    cb = lru_conv_b
    wa, wi = lru_w_a[0].astype(BF16), lru_w_i[0].astype(BF16)
    ba, bi = lru_b_a.reshape(1, C), lru_b_i.reshape(1, C)
    lam, glo, gso = lru_lambda, lru_out_norm, sc_out_norm

    y, h = _mix_fwd(z, cw, cb, wa, ba, wi, bi, lam, sw, glo, gso, "mix_fwd")
    x2, (wd2,) = _mm_res(y, wout, x1, 1.0, "out_proj", carried=_ag_forward_phase(d2, pcs_d2))
    n3, hg2, hu2, act2 = _norm_proj(x2, g3, [wg2, wu2], [BF16, BF16], True, "ffn2_up")
    x3 = _mm_res(act2, wd2, x2, 0.5, "ffn2_down")
    dx3, df2, d_gf, loss_blk = _loss_head(x3, gf, tgt, "loss_head")

    F = wd1.shape[0]
    bm_f = F // 4 if (F // 4) % LANES == 0 else 512

    def reduce_group(gs, tag):
        pcs, off = [], 0
        for g_ in gs:
            pcs.append((off, g_.shape[0] // N_DEV))
            off += g_.shape[0] // N_DEV
        sb_ = _rs_sibling(gs, pcs, "rs_sibling_add_" + tag)
        return sb_, pcs

    dhg2, dhu2 = _ffn_bwd_act(df2, wd2, hg2, hu2, "ffn2_bwd_act")
    d_wd2 = _dw_tn(act2, df2, bm_f, "ffn2_dw_down")
    d_wg2 = _dw_tn(dhg2, n3, bm_f, "ffn2_dw_gate")
    d_wu2 = _dw_tn(dhu2, n3, bm_f, "ffn2_dw_up")
    sb_f2, pcs_f2 = reduce_group([d_wg2, d_wu2, d_wd2], "ffn2")
    (dx2, dx2b, d_g3), (lb_f2,) = _mm_rmsbwd([(dhg2, wg2), (dhu2, wu2)], x2, g3, dx3, 1.0, "ffn2_bwd_in",
                                             carried=_rs_chips_phase(sb_f2))
    dy = _mm_nt(dx2b, wout, "out_proj_bwd")
    d_wout = _dw_tn(y, dx2b, 1024, "out_proj_dw")
    dz, small, d_wa, d_wi = _mix_bwd(z, h, dy, cw, cb, wa, ba, wi, bi, lam, sw, glo, gso, "mix_bwd")
    d_win = _dw_tn(dz, n2, 1280, "in_proj_dw")
    sb_mx, pcs_mx = reduce_group([d_win, d_wout], "mix")
    (dx1, df1, d_gm), (lb_mx,) = _mm_rmsbwd([(dz, win)], x1, gm, dx2, 0.5, "in_proj_bwd",
                                            carried=_rs_chips_phase(sb_mx))
    dhg1, dhu1 = _ffn_bwd_act(df1, wd1, hg1, hu1, "ffn1_bwd_act")
    d_wd1 = _dw_tn(act1, df1, bm_f, "ffn1_dw_down")
    d_wg1 = _dw_tn(dhg1, n1, bm_f, "ffn1_dw_gate")
    d_wu1 = _dw_tn(dhu1, n1, bm_f, "ffn1_dw_up")
    sb_f1, pcs_f1 = reduce_group([d_wg1, d_wu1, d_wd1], "ffn1")
    (dx0, _, d_g1), (lb_f1,) = _mm_rmsbwd([(dhg1, wg1), (dhu1, wu1)], x0, g1, dx1, 1.0, "ffn1_bwd_in",
                                          carried=_rs_chips_phase(sb_f1))

    big_sum = {}
    for tag, names, sb_, lb_, pcs in (("ffn2", ("ffn2_w_gate", "ffn2_w_up", "ffn2_w_down"), sb_f2, lb_f2, pcs_f2),
                                      ("mix", ("w_in", "w_out"), sb_mx, lb_mx, pcs_mx),
                                      ("ffn1", ("ffn1_w_gate", "ffn1_w_up", "ffn1_w_down"), sb_f1, lb_f1, pcs_f1)):
        gsum = _final_grad(sb_, lb_, chip, "rs_final_sum_" + tag)
        for name, (off, rows) in zip(names, pcs):
            big_sum[name] = gsum[off:off + rows]

    small_parts = [d_g1, d_gm, d_g3, d_gf, small[R_CW:R_CW + 4], small[R_CB], d_wa, small[R_BA], d_wi, small[R_BI],
                   small[R_LAM], small[R_SW:R_SW + 3], small[R_GLO], small[R_GSO]]
    sslab, counts = _pack_rows(small_parts, LANES)
    RS = sslab.shape[0]
    (sg,) = _allgather(sslab, [(0, RS)], "allgather_small_grads")
    ssum = _add_slabs([sg[j * RS:(j + 1) * RS] for j in range(N_DEV)], F32, "small_grads_sum")
    full_shapes = [(1, D), (1, D), (1, D), (D,), (1, 4, C), (1, C), (1, H, hd, hd), (1, H, hd), (1, H, hd, hd), (1, H, hd),
                   (1, C), (1, 3, C), (1, C), (1, C)]
    small_full = dict(zip(SMALL_NAMES, _unpack_rows(ssum, counts, full_shapes)))

    grads = {}
    for name, transposed in BIG:
        gblk = big_sum[name]
        grads[name] = (gblk.T if transposed else gblk)[None]
    for name in SMALL_NAMES:
        gfull = small_full[name]
        if name in ("lru_conv_w", "sc_conv_w"):
            gfull = lax.dynamic_slice_in_dim(gfull, dev * CL, CL, axis=2)
        grads[name] = gfull

    delta, new_m, new_v = {}, {}, {}
    for name, transposed in BIG:
        flip = transposed and w[name].shape[2] % LANES != 0
        view = (lambda t: t[0].T) if flip else (lambda t: t[0])
        back = (lambda t: t.T[None]) if flip else (lambda t: t[None])
        gview = big_sum[name] if flip else grads[name][0]
        d_, m_, v_ = _adamw(view(w[name]), gview, view(m[name]), view(v[name]), "adamw_" + name)
        delta[name], new_m[name], new_v[name] = back(d_), back(m_), back(v_)
    packs = [_pack_rows([t[n_] for n_ in SMALL_NAMES], LANES) for t in (w, grads, m, v)]
    sd, sm, sv = _adamw(packs[0][0], packs[1][0], packs[2][0], packs[3][0], "adamw_small")
    shapes = [w[n_].shape for n_ in SMALL_NAMES]
    for tgt_dict, slab_ in ((delta, sd), (new_m, sm), (new_v, sv)):
        for n_, val in zip(SMALL_NAMES, _unpack_rows(slab_, packs[0][1], shapes)):
            tgt_dict[n_] = val

    loss = lax.psum(loss_blk[0, 0], ("x", "y", "c"))
    return (loss, dx0[None], *[grads[n_] for n_ in WEIGHT_NAMES], *[delta[n_] for n_ in WEIGHT_NAMES],
            *[new_m[n_] for n_ in WEIGHT_NAMES], *[new_v[n_] for n_ in WEIGHT_NAMES])
```

```python
import functools
import math

import jax
import jax.numpy as jnp
from jax import lax
from jax.experimental import pallas as pl
from jax.experimental.pallas import tpu as pltpu

F32 = jnp.float32
BF16 = jnp.bfloat16
SDS = jax.ShapeDtypeStruct
MESH = pl.DeviceIdType.MESH

NORM_EPS = 1e-6
LRU_C = 8.0
N_DEV = 8
N_CHIP = 4
ADAM_LR, ADAM_B1, ADAM_B2, ADAM_EPS, ADAM_WD, ADAM_STEP = 0.001, 0.9, 0.999, 1e-08, 0.01, 10

NN = (((1,), (0,)), ((), ()))
NT = (((1,), (1,)), ((), ()))
TN = (((0,), (0,)), ((), ()))

SUBLANES = 8
BF16_ROWS = 16
LANES = 128
MIB = 1 << 20


def _dot(a, b, dims):
    return lax.dot_general(a, b, dims, preferred_element_type=F32)


def _blk(n, pref, align):
    if n <= pref:
        return n
    b = (pref // align) * align
    while b >= align:
        if n % b == 0:
            return b
        b -= align
    raise ValueError(f"no block of {n} aligned to {align} under {pref}")


def _cp(sem, vmem_mib):
    return pltpu.CompilerParams(dimension_semantics=sem, vmem_limit_bytes=vmem_mib * MIB)


HBM_SPEC = pl.BlockSpec(memory_space=pltpu.HBM)
MID_EIGHTHS = 5

class _Carried:
    def __init__(self, inputs, out_shapes, aliases, sem_shapes, build, has_mid=False):
        self.inputs, self.out_shapes, self.aliases = list(inputs), list(out_shapes), dict(aliases)
        self.sem_shapes, self.build, self.has_mid = list(sem_shapes), build, has_mid


def _call(body, *, name, grid, in_specs, out_specs, out_shape, scratch_shapes, compiler_params, args, carried=None):
    if carried is None:
        return pl.pallas_call(body, name=name, grid=grid, in_specs=in_specs, out_specs=out_specs, out_shape=out_shape,
                              scratch_shapes=scratch_shapes, compiler_params=compiler_params)(*args)
    n_in, n_out, n_sc = len(in_specs), len(out_shape), len(scratch_shapes)
    c_in, c_out = len(carried.inputs), len(carried.out_shapes)

    def hosted(*refs):
        ins, refs = refs[:n_in], refs[n_in:]
        c_ins, refs = refs[:c_in], refs[c_in:]
        outs, refs = refs[:n_out], refs[n_out:]
        c_outs, refs = refs[:c_out], refs[c_out:]
        scratch, c_sems = refs[:n_sc], refs[n_sc:]
        first = functools.reduce(jnp.logical_and, [pl.program_id(a) == 0 for a in range(len(grid))])
        last = functools.reduce(jnp.logical_and, [pl.program_id(a) == g - 1 for a, g in enumerate(grid)])

        @pl.when(first)
        def _():
            for start in carried.build(c_ins, c_outs, c_sems, "start"):
                start()

        if carried.has_mid:
            mid = functools.reduce(jnp.logical_and, [pl.program_id(0) == (grid[0] * MID_EIGHTHS) // 8]
                                   + [pl.program_id(a) == 0 for a in range(1, len(grid))])

            @pl.when(mid)
            def _():
                for step in carried.build(c_ins, c_outs, c_sems, "mid"):
                    step()

        body(*ins, *outs, *scratch)

        @pl.when(last)
        def _():
            for wait in carried.build(c_ins, c_outs, c_sems, "end"):
                wait()

    out = pl.pallas_call(
        hosted, name=name, grid=grid, in_specs=list(in_specs) + [HBM_SPEC] * c_in,
        out_specs=list(out_specs) + [HBM_SPEC] * c_out, out_shape=list(out_shape) + carried.out_shapes,
        scratch_shapes=list(scratch_shapes) + carried.sem_shapes,
        input_output_aliases={n_in + a: n_out + b for a, b in carried.aliases.items()},
        compiler_params=compiler_params)(*args, *carried.inputs)
    return out[:n_out], out[n_out:]


ROW_CHUNK = 128


EPILOGUE_ROWS = 128


def _chunk_rows(c):
    return pl.ds(pl.multiple_of(c * ROW_CHUNK, ROW_CHUNK), ROW_CHUNK)


def _rstd(xv):
    return lax.rsqrt(jnp.mean(xv * xv, axis=-1, keepdims=True) + NORM_EPS)


def _rms_bwd(xv, g, dn):
    r = _rstd(xv)
    xr = xv * r
    gd = g * dn
    dx = r * (gd - xr * jnp.mean(gd * xr, axis=-1, keepdims=True))
    return dx, jnp.sum(dn * xr, axis=0, keepdims=True)


def _log1p(e):
    u = 1.0 + e
    return jnp.where(u == 1.0, e, jnp.log(u) * (e / (u - 1.0)))


def _one_minus_exp(v, exp_half_v):
    series = 1.0 / 5040.0
    for coeff in (1.0 / 720.0, 1.0 / 120.0, 1.0 / 24.0, 1.0 / 6.0, 0.5, 1.0):
        series = series * v + coeff
    return jnp.where(v > -0.5, -v * series, 1.0 - exp_half_v * exp_half_v)


def _sigmoid(v):
    return 0.5 * jnp.tanh(0.5 * v) + 0.5


def _gelu_parts(g):
    k0 = math.sqrt(2.0 / math.pi)
    g2 = g * g
    t = jnp.tanh(k0 * (g + 0.044715 * g * g2))
    gel = 0.5 * g * (1.0 + t)
    gelp = 0.5 * (1.0 + t) + 0.5 * g * (1.0 - t * t) * (k0 * (1.0 + 3.0 * 0.044715 * g2))
    return gel, gelp


def _norm_proj(x, gain, w_list, out_dtypes, swiglu, name, carried=None):
    T, D = x.shape
    N = w_list[0].shape[0]
    nw = len(w_list)
    bm = _blk(T, 1024, BF16_ROWS)
    bn = _blk(N, 512, LANES)

    def body(*refs):
        x_ref, g_ref = refs[:2]
        w_refs = refs[2:2 + nw]
        n_ref = refs[2 + nw]
        o_refs = refs[3 + nw:3 + 2 * nw]
        act_ref = refs[3 + 2 * nw] if swiglu else None
        n_sc = refs[-1]

        @pl.when(pl.program_id(1) == 0)
        def _():
            def piece(p, _):
                r = pl.ds(pl.multiple_of(p * EPILOGUE_ROWS, EPILOGUE_ROWS), EPILOGUE_ROWS)
                xv = x_ref[r, :]
                nb = (xv * _rstd(xv) * g_ref[...]).astype(BF16)
                n_sc[r, :] = nb
                n_ref[r, :] = nb
                return 0
            lax.fori_loop(0, bm // EPILOGUE_ROWS, piece, 0)

        n = n_sc[...]
        outs = [_dot(n, w_ref[...], NT) for w_ref in w_refs]
        for o_ref, o in zip(o_refs, outs):
            o_ref[...] = o.astype(o_ref.dtype)
        if swiglu:
            hg, hu = outs
            act_ref[...] = (hg * _sigmoid(hg) * hu).astype(BF16)

    row = pl.BlockSpec((bm, D), lambda i, j: (i, 0))
    tile = pl.BlockSpec((bm, bn), lambda i, j: (i, j))
    n_extra = 1 if swiglu else 0
    return _call(
        body, name=name, grid=(T // bm, N // bn),
        in_specs=[row, pl.BlockSpec((1, D), lambda i, j: (0, 0))] + [pl.BlockSpec((bn, D), lambda i, j: (j, 0))] * nw,
        out_specs=[row] + [tile] * (nw + n_extra),
        out_shape=[SDS((T, D), BF16)] + [SDS((T, N), dt) for dt in out_dtypes] + [SDS((T, N), BF16)] * n_extra,
        scratch_shapes=[pltpu.VMEM((bm, D), BF16)],
        compiler_params=_cp(("arbitrary", "arbitrary"), 52),
        args=(x, gain, *w_list), carried=carried)


def _up_act(n, wu, hg, name, carried=None):
    T, D = n.shape
    F = wu.shape[0]
    bm = _blk(T, 1024, BF16_ROWS)
    bn = _blk(F, 512, LANES)

    def body(n_ref, wu_ref, hg_ref, hu_ref, act_ref):
        hu = _dot(n_ref[...], wu_ref[...], NT)
        hg = hg_ref[...].astype(F32)
        hu_ref[...] = hu.astype(BF16)
        act_ref[...] = (hg * _sigmoid(hg) * hu).astype(BF16)

    tile = pl.BlockSpec((bm, bn), lambda i, j: (i, j))
    return _call(
        body, name=name, grid=(T // bm, F // bn),
        in_specs=[pl.BlockSpec((bm, D), lambda i, j: (i, 0)), pl.BlockSpec((bn, D), lambda i, j: (j, 0)), tile],
        out_specs=[tile, tile], out_shape=[SDS((T, F), BF16)] * 2, scratch_shapes=[],
        compiler_params=_cp(("arbitrary", "arbitrary"), 40),
        args=(n, wu, hg), carried=carried)


def _mm_res(a, b, x, scale, name, carried=None):
    T, K = a.shape
    D = b.shape[1]
    bm = _blk(T, 1024, BF16_ROWS)
    bk = _blk(K, 1408, LANES)
    nk = K // bk

    def body(a_ref, b_ref, x_ref, o_ref):
        k = pl.program_id(1)

        @pl.when(k == 0)
        def _():
            o_ref[...] = jnp.zeros_like(o_ref)

        o_ref[...] += _dot(a_ref[...], b_ref[...], NN)

        @pl.when(k == nk - 1)
        def _():
            def chunk(c, _):
                r = _chunk_rows(c)
                o_ref[r, :] = x_ref[r, :] + scale * o_ref[r, :]
                return 0
            lax.fori_loop(0, bm // ROW_CHUNK, chunk, 0)

    row = pl.BlockSpec((bm, D), lambda i, k: (i, 0))
    out = _call(
        body, name=name, grid=(T // bm, nk),
        in_specs=[pl.BlockSpec((bm, bk), lambda i, k: (i, k)), pl.BlockSpec((bk, D), lambda i, k: (k, 0)), row],
        out_specs=[row], out_shape=[SDS((T, D), F32)], scratch_shapes=[],
        compiler_params=_cp(("arbitrary", "arbitrary"), 56),
        args=(a, b, x), carried=carried)
    return out[0] if carried is None else (out[0][0], out[1])


def _mm_nt(a, b, name):
    T, K = a.shape
    N = b.shape[0]
    bm = _blk(T, 1024, BF16_ROWS)
    bn = _blk(N, 512, LANES)

    def body(a_ref, b_ref, o_ref):
        o_ref[...] = _dot(a_ref[...], b_ref[...], NT)

    return pl.pallas_call(
        body, name=name, grid=(T // bm, N // bn),
        in_specs=[pl.BlockSpec((bm, K), lambda i, j: (i, 0)), pl.BlockSpec((bn, K), lambda i, j: (j, 0))],
        out_specs=pl.BlockSpec((bm, bn), lambda i, j: (i, j)), out_shape=SDS((T, N), F32),
        compiler_params=_cp(("arbitrary", "arbitrary"), 40),
    )(a, b)


def _ffn_bwd_act(dfb, wd, hg, hu, name):
    T, D = dfb.shape
    F = wd.shape[0]
    bm = _blk(T, 2048, BF16_ROWS)
    bn = _blk(F, 256, LANES)

    def body(df_ref, wd_ref, hg_ref, hu_ref, dhg_ref, dhu_ref):
        dact = _dot(df_ref[...], wd_ref[...], NT)
        hgv = hg_ref[...].astype(F32)
        huv = hu_ref[...].astype(F32)
        s = _sigmoid(hgv)
        dhu_ref[...] = (dact * (hgv * s)).astype(BF16)
        dhg_ref[...] = (dact * huv * (s * (1.0 + hgv * (1.0 - s)))).astype(BF16)

    tile = pl.BlockSpec((bm, bn), lambda i, j: (i, j))
    return pl.pallas_call(
        body, name=name, grid=(T // bm, F // bn),
        in_specs=[pl.BlockSpec((bm, D), lambda i, j: (i, 0)), pl.BlockSpec((bn, D), lambda i, j: (j, 0)), tile, tile],
        out_specs=[tile, tile], out_shape=[SDS((T, F), BF16)] * 2,
        compiler_params=_cp(("arbitrary", "arbitrary"), 40),
    )(dfb, wd, hg, hu)


def _dw_tn(a, b, bm_pref, name):
    T, M = a.shape
    N = b.shape[1]
    bm = _blk(M, bm_pref, LANES)
    tk = _blk(T, 1024, BF16_ROWS)
    nk = T // tk

    def body(a_ref, b_ref, o_ref, acc):
        k = pl.program_id(1)

        @pl.when(k == 0)
        def _():
            acc[...] = jnp.zeros_like(acc)

        acc[...] += _dot(a_ref[...], b_ref[...], TN)

        @pl.when(k == nk - 1)
        def _():
            o_ref[...] = acc[...].astype(BF16)

    return pl.pallas_call(
        body, name=name, grid=(M // bm, nk),
        in_specs=[pl.BlockSpec((tk, bm), lambda i, k: (k, i)), pl.BlockSpec((tk, N), lambda i, k: (k, 0))],
        out_specs=pl.BlockSpec((bm, N), lambda i, k: (i, 0)), out_shape=SDS((M, N), BF16),
        scratch_shapes=[pltpu.VMEM((bm, N), F32)],
        compiler_params=_cp(("arbitrary", "arbitrary"), 48),
    )(a, b)


def _mm_rmsbwd(pairs, x, gain, dx_in, bscale, name, carried=None):
    T, D = x.shape
    K = pairs[0][0].shape[1]
    npair = len(pairs)
    bm = _blk(T, 1024, BF16_ROWS)
    bk = _blk(K, 1024 // npair, LANES)
    nk = K // bk

    nchunk = bm // ROW_CHUNK

    def body(*refs):
        ab = refs[:2 * npair]
        x_hbm, g_ref, dxin_hbm, dx_ref, dxb_ref, dg_ref, x_buf, dxin_buf, sems = refs[2 * npair:]
        i = pl.program_id(0)
        k = pl.program_id(1)

        def fetch(c, slot):
            rows = pl.ds(i * bm + c * ROW_CHUNK, ROW_CHUNK)
            return (pltpu.make_async_copy(x_hbm.at[rows, :], x_buf.at[slot], sems.at[slot, 0]),
                    pltpu.make_async_copy(dxin_hbm.at[rows, :], dxin_buf.at[slot], sems.at[slot, 1]))

        @pl.when(k == 0)
        def _():
            dx_ref[...] = jnp.zeros_like(dx_ref)

        @pl.when(k == nk - 1)
        def _():
            for cp in fetch(0, 0):
                cp.start()

        for q in range(npair):
            dx_ref[...] += _dot(ab[2 * q][...], ab[2 * q + 1][...], NN)

        @pl.when(k == nk - 1)
        def _():
            @pl.when(i == 0)
            def _():
                dg_ref[...] = jnp.zeros_like(dg_ref)

            def chunk(c, _):
                slot = c % 2

                @pl.when(c + 1 < nchunk)
                def _():
                    for cp in fetch(c + 1, 1 - slot):
                        cp.start()

                for cp in fetch(c, slot):
                    cp.wait()

                def piece(p, _):
                    rb = pl.ds(pl.multiple_of(p * EPILOGUE_ROWS, EPILOGUE_ROWS), EPILOGUE_ROWS)
                    r = pl.ds(pl.multiple_of(c * ROW_CHUNK + p * EPILOGUE_ROWS, EPILOGUE_ROWS), EPILOGUE_ROWS)
                    dx, dg = _rms_bwd(x_buf[slot, rb, :], g_ref[...], dx_ref[r, :])
                    dxo = dxin_buf[slot, rb, :] + dx
                    dx_ref[r, :] = dxo
                    dxb_ref[r, :] = (bscale * dxo).astype(BF16)
                    dg_ref[...] += dg
                    return 0
                lax.fori_loop(0, ROW_CHUNK // EPILOGUE_ROWS, piece, 0)
                return 0
            lax.fori_loop(0, nchunk, chunk, 0)

    row = pl.BlockSpec((bm, D), lambda i, k: (i, 0))
    anywhere = pl.BlockSpec(memory_space=pl.ANY)
    vec = pl.BlockSpec((1, D), lambda i, k: (0, 0))
    in_specs = []
    args = []
    for a, b in pairs:
        in_specs += [pl.BlockSpec((bm, bk), lambda i, k: (i, k)), pl.BlockSpec((bk, D), lambda i, k: (k, 0))]
        args += [a, b]
    return _call(
        body, name=name, grid=(T // bm, nk),
        in_specs=in_specs + [anywhere, vec, anywhere], out_specs=[row, row, vec],
        out_shape=[SDS((T, D), F32), SDS((T, D), BF16), SDS((1, D), F32)],
        scratch_shapes=[pltpu.VMEM((2, ROW_CHUNK, D), F32), pltpu.VMEM((2, ROW_CHUNK, D), F32),
                        pltpu.SemaphoreType.DMA((2, 2))],
        compiler_params=_cp(("arbitrary", "arbitrary"), 52),
        args=(*args, x, gain, dx_in), carried=carried)


def _loss_head(x3, gain, tgt, name):
    T, D = x3.shape
    bm = _blk(T, 256, BF16_ROWS)

    def body(x_ref, g_ref, t_ref, dx_ref, dxb_ref, dg_ref, loss_ref):
        i = pl.program_id(0)
        xv = x_ref[...]
        g = g_ref[...]
        out = xv * _rstd(xv) * g
        e = out - t_ref[...]
        part = 0.5 * jnp.sum(jnp.mean(e * e, axis=-1, keepdims=True), axis=0, keepdims=True)
        dx, dg = _rms_bwd(xv, g, e * (1.0 / D))
        dx_ref[...] = dx
        dxb_ref[...] = (0.5 * dx).astype(BF16)

        @pl.when(i == 0)
        def _():
            dg_ref[...] = dg
            loss_ref[...] = jnp.broadcast_to(part, loss_ref.shape)

        @pl.when(i > 0)
        def _():
            dg_ref[...] += dg
            loss_ref[...] += jnp.broadcast_to(part, loss_ref.shape)

    row = pl.BlockSpec((bm, D), lambda i: (i, 0))
    vec = pl.BlockSpec((1, D), lambda i: (0, 0))
    return pl.pallas_call(
        body, name=name, grid=(T // bm,),
        in_specs=[row, vec, row], out_specs=[row, row, vec, pl.BlockSpec((SUBLANES, LANES), lambda i: (0, 0))],
        out_shape=[SDS((T, D), F32), SDS((T, D), BF16), SDS((1, D), F32), SDS((SUBLANES, LANES), F32)],
        compiler_params=_cp(("arbitrary",), 40),
    )(x3, gain, tgt)


R_CW, R_CB, R_BA, R_BI, R_LAM, R_SW, R_GLO, R_GSO, SMALL_ROWS = 0, 4, 5, 6, 7, 8, 11, 12, 16


def _rows(g):
    return pl.ds(pl.multiple_of(g * SUBLANES, SUBLANES), SUBLANES)


def _shift_back(prev, cur, d):
    row = lax.broadcasted_iota(jnp.int32, cur.shape, 0)
    return pltpu.roll(jnp.where(row >= SUBLANES - d, prev, cur), d, 0)


def _shift_fwd(cur, nxt, d):
    row = lax.broadcasted_iota(jnp.int32, cur.shape, 0)
    return pltpu.roll(jnp.where(row < d, nxt, cur), SUBLANES - d, 0)


def _causal_conv(ext, g, taps_ref, ntap):
    prev = ext[_rows(g), :]
    cur = ext[_rows(g + 1), :]
    out = _shift_back(prev, cur, ntap - 1) * taps_ref[0:1, :]
    for k in range(1, ntap - 1):
        out = out + _shift_back(prev, cur, ntap - 1 - k) * taps_ref[k:k + 1, :]
    return out + cur * taps_ref[ntap - 1:ntap, :]


def _scan8(A, U, reverse):
    row = lax.broadcasted_iota(jnp.int32, A.shape, 0)
    for s in (1, 2, 4):
        if reverse:
            A_sh = pltpu.roll(A, SUBLANES - s, 0)
            U_sh = pltpu.roll(U, SUBLANES - s, 0)
            m = row < SUBLANES - s
        else:
            A_sh = pltpu.roll(A, s, 0)
            U_sh = pltpu.roll(U, s, 0)
            m = row >= s
        U = jnp.where(m, A * U_sh + U, U)
        A = jnp.where(m, A * A_sh, A)
    return A, U


def _gate_pre(xc_s, w_ref, out_s, H, hd):
    for h in range(H):
        cs = slice(h * hd, (h + 1) * hd)
        out_s[:, cs] = _dot(xc_s[:, cs].astype(BF16), w_ref[h], NN)


def _lru_coeffs(pa, pi, xc, ba, bi, sp):
    ra = _sigmoid(pa + ba)
    ri = _sigmoid(pi + bi)
    log_a = (-LRU_C * ra) * sp
    a = jnp.exp(log_a)
    mult = jnp.sqrt(_one_minus_exp(2.0 * log_a, a))
    return ra, ri, a, mult


def _softplus_neg(lam):
    v = -lam
    return jnp.maximum(v, 0.0) + _log1p(jnp.exp(-jnp.abs(v)))


def _mix_fwd(z, cw, cb, wa, ba, wi, bi, lam, sw, glo, gso, name):
    T = z.shape[0]
    C = z.shape[1] // 5
    H = wa.shape[0]
    hd = C // H
    tb = _blk(T, 256, BF16_ROWS)
    ng = tb // SUBLANES
    HDR = SUBLANES

    def body(z_ref, cw_ref, cb_ref, wa_ref, ba_ref, wi_ref, bi_ref, lam_ref, sw_ref, glo_ref, gso_ref,
             y_ref, h_ref, xext, pext, xc_s, pa_s, pi_s, y_s, hcar):
        @pl.when(pl.program_id(0) == 0)
        def _():
            xext[0:HDR, :] = jnp.zeros((HDR, C), F32)
            pext[0:HDR, :] = jnp.zeros((HDR, C), F32)
            hcar[...] = jnp.zeros_like(hcar)

        def fill(g, _):
            r = _rows(g)
            re = _rows(g + 1)
            xext[re, :] = z_ref[r, 0:C]
            pext[re, :] = z_ref[r, 3 * C:4 * C] * z_ref[r, 4 * C:5 * C]
            return 0
        lax.fori_loop(0, ng, fill, 0)

        def conv(g, _):
            xc_s[_rows(g), :] = _causal_conv(xext, g, cw_ref, 4) + cb_ref[...]
            return 0
        lax.fori_loop(0, ng, conv, 0)

        _gate_pre(xc_s, wa_ref, pa_s, H, hd)
        _gate_pre(xc_s, wi_ref, pi_s, H, hd)
        sp = _softplus_neg(lam_ref[...])

        def group(g, hprev):
            r = _rows(g)
            xc = xc_s[r, :]
            _, ri, a, mult = _lru_coeffs(pa_s[r, :], pi_s[r, :], xc, ba_ref[...], bi_ref[...], sp)
            A, U = _scan8(a, mult * (ri * xc), reverse=False)
            hh = A * hprev + U
            h_ref[r, :] = hh
            gel, _ = _gelu_parts(z_ref[r, C:2 * C])
            y_lru = hh * gel
            y_s[r, 0:C] = y_lru * _rstd(y_lru) * glo_ref[...]
            y_sc = z_ref[r, 2 * C:3 * C] * _causal_conv(pext, g, sw_ref, 3)
            y_s[r, C:2 * C] = y_sc * _rstd(y_sc) * gso_ref[...]
            return jnp.broadcast_to(hh[SUBLANES - 1:SUBLANES, :], hh.shape)
        hcar[...] = lax.fori_loop(0, ng // 2, lambda t, hp: group(2 * t + 1, group(2 * t, hp)), hcar[...])

        xext[0:HDR, :] = xext[tb:tb + HDR, :]
        pext[0:HDR, :] = pext[tb:tb + HDR, :]

        def cast(g, _):
            r = pl.ds(pl.multiple_of(g * BF16_ROWS, BF16_ROWS), BF16_ROWS)
            y_ref[r, :] = y_s[r, :].astype(BF16)
            return 0
        lax.fori_loop(0, tb // BF16_ROWS, cast, 0)

    full = lambda shape: pl.BlockSpec(shape, lambda i: (0,) * len(shape))
    blk = lambda w: pl.BlockSpec((tb, w), lambda i: (i, 0))
    ext = pltpu.VMEM((tb + HDR, C), F32)
    tile = pltpu.VMEM((tb, C), F32)
    return pl.pallas_call(
        body, name=name, grid=(T // tb,),
        in_specs=[blk(5 * C), full((4, C)), full((1, C)), full((H, hd, hd)), full((1, C)), full((H, hd, hd)),
                  full((1, C)), full((1, C)), full((3, C)), full((1, C)), full((1, C))],
        out_specs=[blk(2 * C), blk(C)],
        out_shape=[SDS((T, 2 * C), BF16), SDS((T, C), F32)],
        scratch_shapes=[ext, ext, tile, tile, tile, pltpu.VMEM((tb, 2 * C), F32), pltpu.VMEM((SUBLANES, C), F32)],
        compiler_params=_cp(("arbitrary",), 40),
    )(z, cw, cb, wa, ba, wi, bi, lam, sw, glo, gso)


def _mix_bwd(z, h, dy, cw, cb, wa, ba, wi, bi, lam, sw, glo, gso, name):
    T = z.shape[0]
    C = z.shape[1] // 5
    H = wa.shape[0]
    hd = C // H
    tb = _blk(T, 256, BF16_ROWS)
    nb = T // tb
    ng = tb // SUBLANES
    HDR = SUBLANES
    N_ACC = 13

    def body(z_ref, zp_ref, h_ref, hp_ref, dy_ref, cw_ref, cb_ref, wa_ref, ba_ref, wi_ref, bi_ref, lam_ref,
             sw_ref, glo_ref, gso_ref, dz_ref, small_ref, dwa_ref, dwi_ref,
             xext, pext, hext, dqext, dxcext, bext, xc_s, pa_s, pi_s, a_s, m_s, ri_s, dh_s, dpa_s, dpi_s,
             dz_s, acc_s, bcar):
        i = pl.program_id(0)
        first_rows = i == nb - 1

        @pl.when(i == 0)
        def _():
            dqext[tb:tb + HDR, :] = jnp.zeros((HDR, C), F32)
            dxcext[tb:tb + HDR, :] = jnp.zeros((HDR, C), F32)
            bcar[...] = jnp.zeros_like(bcar)
            acc_s[...] = jnp.zeros_like(acc_s)
            dwa_ref[...] = jnp.zeros_like(dwa_ref)
            dwi_ref[...] = jnp.zeros_like(dwi_ref)

        zero = jnp.zeros((HDR, C), F32)
        xext[0:HDR, :] = jnp.where(first_rows, zero, zp_ref[:, 0:C])
        pext[0:HDR, :] = jnp.where(first_rows, zero, zp_ref[:, 3 * C:4 * C] * zp_ref[:, 4 * C:5 * C])
        hext[0:HDR, :] = jnp.where(first_rows, zero, hp_ref[...])

        def fill(g, _):
            r = _rows(g)
            re = _rows(g + 1)
            xext[re, :] = z_ref[r, 0:C]
            pext[re, :] = z_ref[r, 3 * C:4 * C] * z_ref[r, 4 * C:5 * C]
            hext[re, :] = h_ref[r, :]
            return 0
        lax.fori_loop(0, ng, fill, 0)

        def conv(g, _):
            xc_s[_rows(g), :] = _causal_conv(xext, g, cw_ref, 4) + cb_ref[...]
            return 0
        lax.fori_loop(0, ng, conv, 0)

        _gate_pre(xc_s, wa_ref, pa_s, H, hd)
        _gate_pre(xc_s, wi_ref, pi_s, H, hd)
        sp = _softplus_neg(lam_ref[...])
        dsp_dlam = -jax.nn.sigmoid(-lam_ref[...])

        def add_acc(k, v):
            acc_s[k] += v

        def p1(g, _):
            r = _rows(g)
            xc = xc_s[r, :]
            _, ri, a, mult = _lru_coeffs(pa_s[r, :], pi_s[r, :], xc, ba_ref[...], bi_ref[...], sp)
            a_s[r, :] = a
            m_s[r, :] = mult
            ri_s[r, :] = ri
            hh = h_ref[r, :]
            gel, gelp = _gelu_parts(z_ref[r, C:2 * C])
            y_lru = hh * gel
            dnl = dy_ref[r, 0:C]
            rl = _rstd(y_lru)
            ylr = y_lru * rl
            gd = glo_ref[...] * dnl
            dy_lru = rl * (gd - ylr * jnp.mean(gd * ylr, axis=-1, keepdims=True))
            add_acc(R_GLO, dnl * ylr)
            dz_s[r, C:2 * C] = dy_lru * hh * gelp
            dh = dy_lru * gel
            dh_s[r, :] = dh

            q = _causal_conv(pext, g, sw_ref, 3)
            scb = z_ref[r, 2 * C:3 * C]
            y_sc = scb * q
            dns = dy_ref[r, C:2 * C]
            rs = _rstd(y_sc)
            ysr = y_sc * rs
            gs = gso_ref[...] * dns
            dy_sc = rs * (gs - ysr * jnp.mean(gs * ysr, axis=-1, keepdims=True))
            add_acc(R_GSO, dns * ysr)
            dz_s[r, 2 * C:3 * C] = dy_sc * q
            dqext[r, :] = dy_sc * scb
            return 0
        lax.fori_loop(0, ng, p1, 0, unroll=2)

        bext[tb:tb + HDR, :] = bcar[...]

        def p2(j, carry):
            g = ng - 1 - j
            r = _rows(g)
            a = a_s[r, :]
            A, U = _scan8(a, a * dh_s[r, :], reverse=True)
            bb = A * carry + U
            bext[r, :] = bb
            return jnp.broadcast_to(bb[0:1, :], bb.shape)
        bcar[...] = lax.fori_loop(0, ng, p2, bcar[...])

        def p3(g, _):
            r = _rows(g)
            rn = _rows(g + 1)
            G = dh_s[r, :] + _shift_fwd(bext[r, :], bext[rn, :], 1)
            hm1 = _shift_back(hext[r, :], hext[rn, :], 1)
            a = a_s[r, :]
            mult = m_s[r, :]
            ri = ri_s[r, :]
            xc = xc_s[r, :]
            ra = _sigmoid(pa_s[r, :] + ba_ref[...])
            dxcext[r, :] = G * mult * ri
            dri = G * mult * xc
            dmult = G * ri * xc
            dlog_a = (G * hm1) * a - dmult * (a * a) / mult
            add_acc(R_LAM, dlog_a * (-LRU_C * ra) * dsp_dlam)
            dpa = dlog_a * (-LRU_C * sp) * ra * (1.0 - ra)
            dpi = dri * ri * (1.0 - ri)
            add_acc(R_BA, dpa)
            add_acc(R_BI, dpi)
            dpa_s[r, :] = dpa
            dpi_s[r, :] = dpi
            return 0
        lax.fori_loop(0, ng, p3, 0)

        for hh_ in range(H):
            cs = slice(hh_ * hd, (hh_ + 1) * hd)
            dpa_b = dpa_s[:, cs].astype(BF16)
            dpi_b = dpi_s[:, cs].astype(BF16)
            xc_b = xc_s[:, cs].astype(BF16)
            dxcext[0:tb, cs] += _dot(dpa_b, wa_ref[hh_], NT) + _dot(dpi_b, wi_ref[hh_], NT)
            dwa_ref[hh_] += _dot(xc_b, dpa_b, TN)
            dwi_ref[hh_] += _dot(xc_b, dpi_b, TN)

        def p4(g, _):
            r = _rows(g)
            rn = _rows(g + 1)
            dxc = dxcext[r, :]
            dxc_n = dxcext[rn, :]
            x_p = xext[r, :]
            x_c = xext[rn, :]
            add_acc(R_CB, dxc)
            dlx = dxc * cw_ref[3:4, :]
            add_acc(R_CW + 3, dxc * x_c)
            for d in range(1, 4):
                dlx = dlx + _shift_fwd(dxc, dxc_n, d) * cw_ref[3 - d:4 - d, :]
                add_acc(R_CW + 3 - d, dxc * _shift_back(x_p, x_c, d))
            dz_s[r, 0:C] = dlx

            dq = dqext[r, :]
            dq_n = dqext[rn, :]
            p_p = pext[r, :]
            p_c = pext[rn, :]
            dp = dq * sw_ref[2:3, :]
            add_acc(R_SW + 2, dq * p_c)
            for d in range(1, 3):
                dp = dp + _shift_fwd(dq, dq_n, d) * sw_ref[2 - d:3 - d, :]
                add_acc(R_SW + 2 - d, dq * _shift_back(p_p, p_c, d))
            dz_s[r, 3 * C:4 * C] = dp * z_ref[r, 4 * C:5 * C]
            dz_s[r, 4 * C:5 * C] = dp * z_ref[r, 3 * C:4 * C]
            return 0
        lax.fori_loop(0, ng, p4, 0)

        dqext[tb:tb + HDR, :] = dqext[0:HDR, :]
        dxcext[tb:tb + HDR, :] = dxcext[0:HDR, :]

        def cast(g, _):
            r = pl.ds(pl.multiple_of(g * BF16_ROWS, BF16_ROWS), BF16_ROWS)
            dz_ref[r, :] = dz_s[r, :].astype(BF16)
            return 0
        lax.fori_loop(0, tb // BF16_ROWS, cast, 0)

        @pl.when(i == nb - 1)
        def _():
            small_ref[...] = jnp.zeros_like(small_ref)
            for k in range(N_ACC):
                small_ref[k:k + 1, :] = jnp.sum(acc_s[k], axis=0, keepdims=True)

    tpg = tb // SUBLANES
    full = lambda shape: pl.BlockSpec(shape, lambda i: (0,) * len(shape))
    blk = lambda w: pl.BlockSpec((tb, w), lambda i: (nb - 1 - i, 0))
    prev = lambda w: pl.BlockSpec((SUBLANES, w), lambda i: (jnp.maximum((nb - 1 - i) * tpg - 1, 0), 0))
    ext = pltpu.VMEM((tb + HDR, C), F32)
    tile = pltpu.VMEM((tb, C), F32)
    return pl.pallas_call(
        body, name=name, grid=(nb,),
        in_specs=[blk(5 * C), prev(5 * C), blk(C), prev(C), blk(2 * C), full((4, C)), full((1, C)), full((H, hd, hd)),
                  full((1, C)), full((H, hd, hd)), full((1, C)), full((1, C)), full((3, C)), full((1, C)), full((1, C))],
        out_specs=[blk(5 * C), full((SMALL_ROWS, C)), full((H, hd, hd)), full((H, hd, hd))],
        out_shape=[SDS((T, 5 * C), BF16), SDS((SMALL_ROWS, C), F32), SDS((H, hd, hd), F32), SDS((H, hd, hd), F32)],
        scratch_shapes=[ext] * 6 + [tile] * 9 + [pltpu.VMEM((tb, 5 * C), F32), pltpu.VMEM((N_ACC, SUBLANES, C), F32),
                                                pltpu.VMEM((SUBLANES, C), F32)],
        compiler_params=_cp(("arbitrary",), 56),
    )(z, z, h, h, dy, cw, cb, wa, ba, wi, bi, lam, sw, glo, gso)


def _add_slabs(terms, out_dtype, name):
    R, Ccols = terms[0].shape
    br = _blk(R, 512, BF16_ROWS)
    n = len(terms)

    def body(*refs):
        s = refs[0][...].astype(F32)
        for t_ref in refs[1:n]:
            s = s + t_ref[...].astype(F32)
        refs[n][...] = s.astype(out_dtype)

    spec = pl.BlockSpec((br, Ccols), lambda i: (i, 0))
    return pl.pallas_call(
        body, name=name, grid=(R // br,), in_specs=[spec] * n, out_specs=spec, out_shape=SDS((R, Ccols), out_dtype),
        compiler_params=_cp(("arbitrary",), 40),
    )(*terms)


def _final_grad(sb, lb, chip, name):
    _, R, Ccols = sb.shape
    br = _blk(R, 512, BF16_ROWS)

    def body(chip_ref, sb_ref, l0, l1, l2, o_ref):
        s = sb_ref[0].astype(F32)
        for t_ref in (l0, l1, l2):
            s = s + t_ref[0].astype(F32)
        o_ref[...] = s

    lspec = lambda k: pl.BlockSpec((1, br, Ccols), lambda i, c: (k, i, 0))
    return pl.pallas_call(
        body, name=name,
        grid_spec=pltpu.PrefetchScalarGridSpec(
            num_scalar_prefetch=1, grid=(R // br,),
            in_specs=[pl.BlockSpec((1, br, Ccols), lambda i, c: (c[0], i, 0)), lspec(0), lspec(1), lspec(2)],
            out_specs=pl.BlockSpec((br, Ccols), lambda i, c: (i, 0))),
        out_shape=SDS((R, Ccols), F32),
        compiler_params=_cp(("arbitrary",), 40),
    )(chip, sb, lb, lb, lb)


def _adamw(w, g, m, v, name):
    R, Ccols = w.shape
    br = _blk(R, 256, SUBLANES)
    c1 = 1.0 - ADAM_B1 ** ADAM_STEP
    c2 = 1.0 - ADAM_B2 ** ADAM_STEP

    def body(w_ref, g_ref, m_ref, v_ref, d_ref, nm_ref, nv_ref):
        gv = g_ref[...]
        nm = ADAM_B1 * m_ref[...] + (1.0 - ADAM_B1) * gv
        nv = ADAM_B2 * v_ref[...] + (1.0 - ADAM_B2) * (gv * gv)
        nm_ref[...] = nm
        nv_ref[...] = nv
        d_ref[...] = -ADAM_LR * ((nm / c1) / (jnp.sqrt(nv / c2) + ADAM_EPS) + ADAM_WD * w_ref[...])

    spec = pl.BlockSpec((br, Ccols), lambda i: (i, 0))
    return pl.pallas_call(
        body, name=name, grid=(R // br,), in_specs=[spec] * 4, out_specs=[spec] * 3,
        out_shape=[SDS((R, Ccols), F32)] * 3, compiler_params=_cp(("arbitrary",), 40),
    )(w, g, m, v)


def _place():
    return lax.axis_index("x"), lax.axis_index("y"), lax.axis_index("c")


def _dev_rows(ref, dev, rows):
    return ref.at[pl.ds((4 * dev[0] + 2 * dev[1] + dev[2]) * rows, rows), :]


def _remote(src, dst, send_sem, recv_sem, to):
    return pltpu.make_async_remote_copy(src_ref=src, dst_ref=dst, send_sem=send_sem, recv_sem=recv_sem,
                                        device_id=to, device_id_type=MESH)


SAME_CORE_AND_SIBLING = ((0, 0, 1), (1, 0, 0), (0, 1, 0), (1, 1, 0))
ALL_OTHERS = SAME_CORE_AND_SIBLING + ((1, 0, 1), (0, 1, 1), (1, 1, 1))


def _merge_phases(a, b):
    na_in, na_out, na_sem = len(a.inputs), len(a.out_shapes), len(a.sem_shapes)

    def build(ins, outs, sems, stage):
        return (a.build(ins[:na_in], outs[:na_out], sems[:na_sem], stage)
                + b.build(ins[na_in:], outs[na_out:], sems[na_sem:], stage))

    aliases = dict(a.aliases)
    aliases.update({na_in + i: na_out + o for i, o in b.aliases.items()})
    return _Carried(a.inputs + b.inputs, a.out_shapes + b.out_shapes, aliases, a.sem_shapes + b.sem_shapes, build,
                    has_mid=a.has_mid or b.has_mid)


def _ag_direct_phase(slab, pieces, flips):
    W = slab.shape[1]
    n = len(pieces)
    npeer = len(flips)

    def build(ins, outs, sems, stage):
        if stage == "mid":
            return []
        starting = stage == "start"
        (slab_ref,) = ins
        send_sems, recv_sems, local_sems = sems
        x, y, c = _place()
        me = (x, y, c)
        peers = [tuple(1 - v if f else v for v, f in zip(me, flip)) for flip in flips]
        todo = []
        for p, (off, rows) in enumerate(pieces):
            src = slab_ref.at[pl.ds(off, rows), :]
            mine = pltpu.make_async_copy(src, _dev_rows(outs[p], me, rows), local_sems.at[p])
            todo.append(mine.start if starting else mine.wait)
            for k, peer in enumerate(peers):
                snd = _remote(src, _dev_rows(outs[p], me, rows), send_sems.at[k, p], recv_sems.at[k, p], peer)
                if starting:
                    todo.append(snd.start)
                else:
                    theirs = _dev_rows(outs[p], peer, rows)
                    rcv = _remote(theirs, theirs, send_sems.at[k, p], recv_sems.at[k, p], me)
                    todo += [rcv.wait_recv, snd.wait_send]
        return todo

    dma = pltpu.SemaphoreType.DMA
    return _Carried([slab], [SDS((N_DEV * rows, W), slab.dtype) for _, rows in pieces], {},
                    [dma((npeer, n)), dma((npeer, n)), dma((n,))], build)


def _ag_two_level_phase(slab, pieces):
    W = slab.shape[1]
    n = len(pieces)

    def build(ins, outs, sems, stage):
        (slab_ref,) = ins
        send_sems, recv_sems, local_sems = sems
        x, y, c = _place()
        me, sibling = (x, y, c), (x, y, 1 - c)
        chips = [(1 - x, y), (x, 1 - y), (1 - x, 1 - y)]
        todo = []
        for p, (off, rows) in enumerate(pieces):
            src = slab_ref.at[pl.ds(off, rows), :]
            own = _dev_rows(outs[p], me, rows)
            landed = [_dev_rows(outs[p], (*chip, c), rows) for chip in chips]

            def mine():
                return pltpu.make_async_copy(src, own, local_sems.at[p])

            def first():
                return [_remote(src, own, send_sems.at[k, p], recv_sems.at[k, p], to)
                        for k, to in enumerate([sibling] + [(*chip, c) for chip in chips])]

            def passed():
                return [_remote(blk, blk, send_sems.at[4 + j, p], recv_sems.at[4 + j, p], sibling)
                        for j, blk in enumerate(landed)]

            def arrival(k, blk):
                return _remote(blk, blk, send_sems.at[k, p], recv_sems.at[k, p], me).wait_recv

            if stage == "start":
                todo += [mine().start] + [cp.start for cp in first()]
            elif stage == "mid":
                for j, (blk, fwd) in enumerate(zip(landed, passed())):
                    todo += [arrival(1 + j, blk), fwd.start]
            else:
                theirs = [_dev_rows(outs[p], sibling, rows)] + [_dev_rows(outs[p], (*chip, 1 - c), rows) for chip in chips]
                todo += [arrival(k, blk) for k, blk in zip((0, 4, 5, 6), theirs)]
                todo += [cp.wait_send for cp in first() + passed()] + [mine().wait]
        return todo

    dma = pltpu.SemaphoreType.DMA
    return _Carried([slab], [SDS((N_DEV * rows, W), slab.dtype) for _, rows in pieces], {},
                    [dma((7, n)), dma((7, n)), dma((n,))], build, has_mid=True)


def _ag_forward_phase(gathered, pieces):
    n = len(pieces)

    def build(ins, outs, sems, stage):
        if stage == "mid":
            return []
        starting = stage == "start"
        send_sems, recv_sems = sems
        x, y, c = _place()
        me, sibling = (x, y, c), (x, y, 1 - c)
        chips = [(1 - x, y), (x, 1 - y), (1 - x, 1 - y)]
        todo = []
        for p, (_, rows) in enumerate(pieces):
            for j, chip in enumerate(chips):
                snd = _remote(_dev_rows(ins[p], (*chip, c), rows), _dev_rows(outs[p], (*chip, c), rows),
                              send_sems.at[j, p], recv_sems.at[j, p], sibling)
                if starting:
                    todo.append(snd.start)
                else:
                    theirs = _dev_rows(outs[p], (*chip, 1 - c), rows)
                    rcv = _remote(theirs, theirs, send_sems.at[j, p], recv_sems.at[j, p], me)
                    todo += [rcv.wait_recv, snd.wait_send]
        return todo

    dma = pltpu.SemaphoreType.DMA
    return _Carried(gathered, [SDS(g.shape, g.dtype) for g in gathered], {p: p for p in range(n)},
                    [dma((3, n)), dma((3, n))], build)


def _rs_chips_phase(sb):
    _, R, W = sb.shape

    def build(ins, outs, sems, stage):
        if stage == "mid":
            return []
        (sb_ref,), (land_ref,) = ins, outs
        send_sems, recv_sems = sems
        x, y, c = _place()
        chips = [(1 - x, y), (x, 1 - y), (1 - x, 1 - y)]
        cps = [_remote(sb_ref.at[2 * chip[0] + chip[1]], land_ref.at[j], send_sems.at[j], recv_sems.at[j], (*chip, c))
               for j, chip in enumerate(chips)]
        if stage == "start":
            return [cp.start for cp in cps]
        return [cp.wait_recv for cp in cps] + [cp.wait_send for cp in cps]

    dma = pltpu.SemaphoreType.DMA
    return _Carried([sb], [SDS((3, R, W), sb.dtype)], {}, [dma((3,)), dma((3,))], build)


def _allgather(slab, pieces, name):
    R, W = slab.shape
    n = len(pieces)
    assert sum(rows for _, rows in pieces) == R

    def body(slab_ref, *refs):
        outs = refs[:n]
        send_sems, recv_sems, local_sems = refs[n:]
        x, y, c = _place()
        me, sibling = (x, y, c), (x, y, 1 - c)
        chips = [(1 - x, y), (x, 1 - y), (1 - x, 1 - y)]

        def dst_rows(p, origin):
            rows = pieces[p][1]
            start = (4 * origin[0] + 2 * origin[1] + origin[2]) * rows
            return outs[p].at[pl.ds(start, rows), :]

        def copies(k, origin, to, from_slab):
            out = []
            for p, (off, rows) in enumerate(pieces):
                dst = dst_rows(p, origin)
                src = slab_ref.at[pl.ds(off, rows), :] if from_slab else dst
                out.append(pltpu.make_async_remote_copy(
                    src_ref=src, dst_ref=dst, send_sem=send_sems.at[k, p], recv_sem=recv_sems.at[k, p],
                    device_id=to, device_id_type=MESH))
            return out

        mine = [pltpu.make_async_copy(slab_ref.at[pl.ds(off, rows), :], dst_rows(p, me), local_sems.at[p])
                for p, (off, rows) in enumerate(pieces)]
        for cp in mine:
            cp.start()
        first = copies(0, me, sibling, True)
        for j, chip in enumerate(chips):
            first += copies(1 + j, me, (*chip, c), True)
        for cp in first:
            cp.start()
        passed = []
        for j, chip in enumerate(chips):
            for cp in copies(1 + j, (*chip, c), me, False):
                cp.wait_recv()
            fwd = copies(4 + j, (*chip, c), sibling, False)
            for cp in fwd:
                cp.start()
            passed += fwd
        for cp in copies(0, sibling, me, False):
            cp.wait_recv()
        for j, chip in enumerate(chips):
            for cp in copies(4 + j, (*chip, 1 - c), me, False):
                cp.wait_recv()
        for cp in first + passed:
            cp.wait_send()
        for cp in mine:
            cp.wait()

    return pl.pallas_call(
        body, name=name,
        in_specs=[HBM_SPEC], out_specs=[HBM_SPEC] * n,
        out_shape=[SDS((N_DEV * rows, W), slab.dtype) for _, rows in pieces],
        scratch_shapes=[pltpu.SemaphoreType.DMA((7, n)), pltpu.SemaphoreType.DMA((7, n)), pltpu.SemaphoreType.DMA((n,))],
    )(slab)


def _rs_sibling(grads, pieces, name):
    W = grads[0].shape[1]
    R = sum(rows for _, rows in pieces)
    n = len(pieces)
    dt = grads[0].dtype
    max_rows = max(rows for _, rows in pieces)
    steps = [(q, p) for q in range(N_CHIP) for p in range(n)]
    ns = len(steps)
    ADD_ROWS = 64
    assert all(rows % ADD_ROWS == 0 for _, rows in pieces)

    def body(*refs):
        g_refs = refs[:n]
        sb_ref, mine_buf, send_buf, land_buf, out_buf, in_sems, out_sems, send_sems, recv_sems, credit = refs[n:]
        x, y, c = _place()
        sibling = (x, y, 1 - c)

        def loads(s):
            q, p = steps[s]
            rows = pieces[p][1]
            slot = s % 2
            mine = g_refs[p].at[pl.ds((2 * q + c) * rows, rows), :]
            theirs = g_refs[p].at[pl.ds((2 * q + 1 - c) * rows, rows), :]
            return (pltpu.make_async_copy(mine, mine_buf.at[slot, pl.ds(0, rows), :], in_sems.at[slot, 0]),
                    pltpu.make_async_copy(theirs, send_buf.at[slot, pl.ds(0, rows), :], in_sems.at[slot, 1]))

        def send(s):
            rows = pieces[steps[s][1]][1]
            slot = s % 2
            return pltpu.make_async_remote_copy(
                src_ref=send_buf.at[slot, pl.ds(0, rows), :], dst_ref=land_buf.at[slot, pl.ds(0, rows), :],
                send_sem=send_sems.at[slot], recv_sem=recv_sems.at[slot], device_id=sibling, device_id_type=MESH)

        def store(s):
            q, p = steps[s]
            off, rows = pieces[p]
            slot = s % 2
            return pltpu.make_async_copy(out_buf.at[slot, pl.ds(0, rows), :], sb_ref.at[q, pl.ds(off, rows), :],
                                         out_sems.at[slot])

        def start_send(s):
            for cp in loads(s):
                cp.wait()
            if s >= 2:
                pl.semaphore_wait(credit.at[s % 2], 1)
            send(s).start()

        for s in range(min(2, ns)):
            for cp in loads(s):
                cp.start()
        start_send(0)
        for s in range(ns):
            slot = s % 2
            rows = pieces[steps[s][1]][1]
            if s + 1 < ns:
                start_send(s + 1)
            send(s).wait_recv()
            if s >= 2:
                store(s - 2).wait()

            def add(k, _, slot=slot):
                r = pl.ds(pl.multiple_of(k * ADD_ROWS, ADD_ROWS), ADD_ROWS)
                out_buf[slot, r, :] = (mine_buf[slot, r, :].astype(F32) + land_buf[slot, r, :].astype(F32)).astype(dt)
                return 0
            lax.fori_loop(0, rows // ADD_ROWS, add, 0)
            if s + 2 < ns:
                pl.semaphore_signal(credit.at[slot], inc=1, device_id=sibling, device_id_type=MESH)
            store(s).start()
            send(s).wait_send()
            if s + 2 < ns:
                for cp in loads(s + 2):
                    cp.start()
        for s in range(max(ns - 2, 0), ns):
            store(s).wait()

    buf = pltpu.VMEM((2, max_rows, W), dt)
    return pl.pallas_call(
        body, name=name,
        in_specs=[HBM_SPEC] * n, out_specs=HBM_SPEC,
        out_shape=SDS((N_CHIP, R, W), dt),
        scratch_shapes=[buf, buf, buf, buf, pltpu.SemaphoreType.DMA((2, 2)), pltpu.SemaphoreType.DMA((2,)),
                        pltpu.SemaphoreType.DMA((2,)), pltpu.SemaphoreType.DMA((2,)), pltpu.SemaphoreType.REGULAR((2,))],
        compiler_params=pltpu.CompilerParams(vmem_limit_bytes=40 * MIB),
    )(*grads)


SMALL_NAMES = ("ffn1_norm", "mix_norm", "ffn2_norm", "final_norm", "lru_conv_w", "lru_conv_b", "lru_w_a", "lru_b_a",
               "lru_w_i", "lru_b_i", "lru_lambda", "sc_conv_w", "lru_out_norm", "sc_out_norm")
WEIGHT_NAMES = ("ffn1_norm", "ffn1_w_gate", "ffn1_w_up", "ffn1_w_down", "mix_norm", "w_in", "lru_conv_w", "lru_conv_b",
                "lru_w_a", "lru_b_a", "lru_w_i", "lru_b_i", "lru_lambda", "sc_conv_w", "lru_out_norm", "sc_out_norm",
                "w_out", "ffn2_norm", "ffn2_w_gate", "ffn2_w_up", "ffn2_w_down", "final_norm")
BIG = (("ffn1_w_gate", True), ("ffn1_w_up", True), ("ffn1_w_down", False), ("ffn2_w_gate", True), ("ffn2_w_up", True),
       ("ffn2_w_down", False), ("w_in", True), ("w_out", False))


SLAB_ROW_ALIGN = 256


def _pack_rows(parts, width):
    rows, counts = [], []
    for p in parts:
        flat = p.reshape(-1)
        nr = -(-flat.shape[0] // width)
        nr = -(-nr // SUBLANES) * SUBLANES
        rows.append(jnp.pad(flat, (0, nr * width - flat.shape[0])).reshape(nr, width))
        counts.append(nr)
    total = sum(counts)
    pad = -(-total // SLAB_ROW_ALIGN) * SLAB_ROW_ALIGN - total
    if pad:
        rows.append(jnp.zeros((pad, width), rows[0].dtype))
    return jnp.concatenate(rows, axis=0), counts


def _stack_rows(blocks):
    pieces, off = [], 0
    for b in blocks:
        pieces.append((off, b.shape[0]))
        off += b.shape[0]
    return jnp.concatenate(blocks, axis=0), pieces


def _unpack_rows(slab, counts, shapes):
    out, r = [], 0
    for nr, shape in zip(counts, shapes):
        size = math.prod(shape)
        out.append(slab[r:r + nr].reshape(-1)[:size].reshape(shape))
        r += nr
    return out


def kernel(x, ffn1_norm, ffn1_w_gate, ffn1_w_up, ffn1_w_down, mix_norm, w_in, lru_conv_w, lru_conv_b, lru_w_a, lru_b_a, lru_w_i, lru_b_i, lru_lambda, sc_conv_w, lru_out_norm, sc_out_norm, w_out, ffn2_norm, ffn2_w_gate, ffn2_w_up, ffn2_w_down, final_norm, loss_target, m_ffn1_norm, m_ffn1_w_gate, m_ffn1_w_up, m_ffn1_w_down, m_mix_norm, m_w_in, m_lru_conv_w, m_lru_conv_b, m_lru_w_a, m_lru_b_a, m_lru_w_i, m_lru_b_i, m_lru_lambda, m_sc_conv_w, m_lru_out_norm, m_sc_out_norm, m_w_out, m_ffn2_norm, m_ffn2_w_gate, m_ffn2_w_up, m_ffn2_w_down, m_final_norm, v_ffn1_norm, v_ffn1_w_gate, v_ffn1_w_up, v_ffn1_w_down, v_mix_norm, v_w_in, v_lru_conv_w, v_lru_conv_b, v_lru_w_a, v_lru_b_a, v_lru_w_i, v_lru_b_i, v_lru_lambda, v_sc_conv_w, v_lru_out_norm, v_sc_out_norm, v_w_out, v_ffn2_norm, v_ffn2_w_gate, v_ffn2_w_up, v_ffn2_w_down, v_final_norm):
    a = dict(locals())
    w = {n: a[n] for n in WEIGHT_NAMES}
    m = {n: a["m_" + n] for n in WEIGHT_NAMES}
    v = {n: a["v_" + n] for n in WEIGHT_NAMES}
    ax, ay, ac = _place()
    dev = 4 * ax + 2 * ay + ac
    chip = (2 * ax + ay).astype(jnp.int32).reshape(1)

    x0 = x[0]
    tgt = loss_target[0]
    T, D = x0.shape
    C = D // 2
    H, hd = lru_w_a.shape[1], lru_w_a.shape[2]
    CL = lru_conv_w.shape[2]

    shards = []
    for name, transposed in BIG:
        s = w[name][0]
        shards.append((s.T if transposed else s).astype(BF16))
    taps = jnp.concatenate([lru_conv_w[0], sc_conv_w[0], jnp.zeros((1, CL), F32)], axis=0)
    taps_row = lax.bitcast_convert_type(taps, BF16).reshape(1, -1)
    taps_blk = jnp.pad(taps_row, ((0, BF16_ROWS - 1), (0, D - taps_row.shape[1])))
    s_wg1, s_wu1, s_wd1, s_wg2, s_wu2, s_wd2, s_win, s_wout = shards
    slab_g1, pcs_g1 = _stack_rows([s_wg1])
    slab_u1, pcs_u1 = _stack_rows([s_wu1])
    slab_d1, pcs_d1 = _stack_rows([s_wd1])
    slab_mw, pcs_mw = _stack_rows([s_win, s_wout, taps_blk])
    slab_gu2, pcs_gu2 = _stack_rows([s_wg2, s_wu2])
    slab_d2, pcs_d2 = _stack_rows([s_wd2])
    (wg1,) = _allgather(slab_g1, pcs_g1, "allgather_ffn1_gate")

    g1, gm, g3 = ffn1_norm, mix_norm, ffn2_norm
    (n1, hg1), (wu1,) = _norm_proj(x0, g1, [wg1], [BF16], False, "ffn1_gate",
                                   carried=_ag_two_level_phase(slab_u1, pcs_u1))
    phase = _merge_phases(_ag_two_level_phase(slab_d1, pcs_d1), _ag_direct_phase(slab_mw, pcs_mw, SAME_CORE_AND_SIBLING))
    (hu1, act1), got = _up_act(n1, wu1, hg1, "ffn1_up", carried=phase)
    wd1, mixw = got[0], got[1:]
    phase = _merge_phases(_ag_forward_phase(mixw, pcs_mw), _ag_direct_phase(slab_gu2, pcs_gu2, SAME_CORE_AND_SIBLING))
    x1, got = _mm_res(act1, wd1, x0, 0.5, "ffn1_down", carried=phase)
    (win, wout, taps_all), gu2 = got[:3], got[3:]
    phase = _merge_phases(_ag_forward_phase(gu2, pcs_gu2), _ag_direct_phase(slab_d2, pcs_d2, SAME_CORE_AND_SIBLING))
    (n2, z), got = _norm_proj(x1, gm, [win], [F32], False, "in_proj", carried=phase)
    (wg2, wu2), d2 = got[:2], got[2:]
    taps_all = taps_all.reshape(N_DEV, BF16_ROWS, D)[:, 0, :2 * SUBLANES * CL].reshape(N_DEV, SUBLANES, CL, 2)
    taps_all = lax.bitcast_convert_type(taps_all, F32)
    taps_all = taps_all.transpose(1, 0, 2).reshape(SUBLANES, N_DEV * CL)
    cw, sw = taps_all[0:4], taps_all[4:7]

    gf = final_norm.reshape(1, D)
    cb = lru_conv_b
    wa, wi = lru_w_a[0].astype(BF16), lru_w_i[0].astype(BF16)
    ba, bi = lru_b_a.reshape(1, C), lru_b_i.reshape(1, C)
    lam, glo, gso = lru_lambda, lru_out_norm, sc_out_norm

    y, h = _mix_fwd(z, cw, cb, wa, ba, wi, bi, lam, sw, glo, gso, "mix_fwd")
    x2, (wd2,) = _mm_res(y, wout, x1, 1.0, "out_proj", carried=_ag_forward_phase(d2, pcs_d2))
    n3, hg2, hu2, act2 = _norm_proj(x2, g3, [wg2, wu2], [BF16, BF16], True, "ffn2_up")
    x3 = _mm_res(act2, wd2, x2, 0.5, "ffn2_down")
    dx3, df2, d_gf, loss_blk = _loss_head(x3, gf, tgt, "loss_head")

    F = wd1.shape[0]
    bm_f = F // 4 if (F // 4) % LANES == 0 else 512

    def reduce_group(gs, tag):
        pcs, off = [], 0
        for g_ in gs:
            pcs.append((off, g_.shape[0] // N_DEV))
            off += g_.shape[0] // N_DEV
        sb_ = _rs_sibling(gs, pcs, "rs_sibling_add_" + tag)
        return sb_, pcs

    dhg2, dhu2 = _ffn_bwd_act(df2, wd2, hg2, hu2, "ffn2_bwd_act")
    d_wd2 = _dw_tn(act2, df2, bm_f, "ffn2_dw_down")
    d_wg2 = _dw_tn(dhg2, n3, bm_f, "ffn2_dw_gate")
    d_wu2 = _dw_tn(dhu2, n3, bm_f, "ffn2_dw_up")
    sb_f2, pcs_f2 = reduce_group([d_wg2, d_wu2, d_wd2], "ffn2")
    (dx2, dx2b, d_g3), (lb_f2,) = _mm_rmsbwd([(dhg2, wg2), (dhu2, wu2)], x2, g3, dx3, 1.0, "ffn2_bwd_in",
                                             carried=_rs_chips_phase(sb_f2))
    dy = _mm_nt(dx2b, wout, "out_proj_bwd")
    d_wout = _dw_tn(y, dx2b, 1024, "out_proj_dw")
    dz, small, d_wa, d_wi = _mix_bwd(z, h, dy, cw, cb, wa, ba, wi, bi, lam, sw, glo, gso, "mix_bwd")
    d_win = _dw_tn(dz, n2, 1280, "in_proj_dw")
    sb_mx, pcs_mx = reduce_group([d_win, d_wout], "mix")
    (dx1, df1, d_gm), (lb_mx,) = _mm_rmsbwd([(dz, win)], x1, gm, dx2, 0.5, "in_proj_bwd",
                                            carried=_rs_chips_phase(sb_mx))
    dhg1, dhu1 = _ffn_bwd_act(df1, wd1, hg1, hu1, "ffn1_bwd_act")
    d_wd1 = _dw_tn(act1, df1, bm_f, "ffn1_dw_down")
    d_wg1 = _dw_tn(dhg1, n1, bm_f, "ffn1_dw_gate")
    d_wu1 = _dw_tn(dhu1, n1, bm_f, "ffn1_dw_up")
    sb_f1, pcs_f1 = reduce_group([d_wg1, d_wu1, d_wd1], "ffn1")
    (dx0, _, d_g1), (lb_f1,) = _mm_rmsbwd([(dhg1, wg1), (dhu1, wu1)], x0, g1, dx1, 1.0, "ffn1_bwd_in",
                                          carried=_rs_chips_phase(sb_f1))

    big_sum = {}
    for tag, names, sb_, lb_, pcs in (("ffn2", ("ffn2_w_gate", "ffn2_w_up", "ffn2_w_down"), sb_f2, lb_f2, pcs_f2),
                                      ("mix", ("w_in", "w_out"), sb_mx, lb_mx, pcs_mx),
                                      ("ffn1", ("ffn1_w_gate", "ffn1_w_up", "ffn1_w_down"), sb_f1, lb_f1, pcs_f1)):
        gsum = _final_grad(sb_, lb_, chip, "rs_final_sum_" + tag)
        for name, (off, rows) in zip(names, pcs):
            big_sum[name] = gsum[off:off + rows]

    small_parts = [d_g1, d_gm, d_g3, d_gf, small[R_CW:R_CW + 4], small[R_CB], d_wa, small[R_BA], d_wi, small[R_BI],
                   small[R_LAM], small[R_SW:R_SW + 3], small[R_GLO], small[R_GSO]]
    sslab, counts = _pack_rows(small_parts, LANES)
    RS = sslab.shape[0]
    (sg,) = _allgather(sslab, [(0, RS)], "allgather_small_grads")
    ssum = _add_slabs([sg[j * RS:(j + 1) * RS] for j in range(N_DEV)], F32, "small_grads_sum")
    full_shapes = [(1, D), (1, D), (1, D), (D,), (1, 4, C), (1, C), (1, H, hd, hd), (1, H, hd), (1, H, hd, hd), (1, H, hd),
                   (1, C), (1, 3, C), (1, C), (1, C)]
    small_full = dict(zip(SMALL_NAMES, _unpack_rows(ssum, counts, full_shapes)))

    grads = {}
    for name, transposed in BIG:
        gblk = big_sum[name]
        grads[name] = (gblk.T if transposed else gblk)[None]
    for name in SMALL_NAMES:
        gfull = small_full[name]
        if name in ("lru_conv_w", "sc_conv_w"):
            gfull = lax.dynamic_slice_in_dim(gfull, dev * CL, CL, axis=2)
        grads[name] = gfull

    delta, new_m, new_v = {}, {}, {}
    for name, transposed in BIG:
        flip = transposed and w[name].shape[2] % LANES != 0
        view = (lambda t: t[0].T) if flip else (lambda t: t[0])
        back = (lambda t: t.T[None]) if flip else (lambda t: t[None])
        gview = big_sum[name] if flip else grads[name][0]
        d_, m_, v_ = _adamw(view(w[name]), gview, view(m[name]), view(v[name]), "adamw_" + name)
        delta[name], new_m[name], new_v[name] = back(d_), back(m_), back(v_)
    packs = [_pack_rows([t[n_] for n_ in SMALL_NAMES], LANES) for t in (w, grads, m, v)]
    sd, sm, sv = _adamw(packs[0][0], packs[1][0], packs[2][0], packs[3][0], "adamw_small")
    shapes = [w[n_].shape for n_ in SMALL_NAMES]
    for tgt_dict, slab_ in ((delta, sd), (new_m, sm), (new_v, sv)):
        for n_, val in zip(SMALL_NAMES, _unpack_rows(slab_, packs[0][1], shapes)):
            tgt_dict[n_] = val

    loss = lax.psum(loss_blk[0, 0], ("x", "y", "c"))
    return (loss, dx0[None], *[grads[n_] for n_ in WEIGHT_NAMES], *[delta[n_] for n_ in WEIGHT_NAMES],
            *[new_m[n_] for n_ in WEIGHT_NAMES], *[new_v[n_] for n_ in WEIGHT_NAMES])
```

```python
import functools
import math

import jax
import jax.numpy as jnp
from jax import lax
from jax.experimental import pallas as pl
from jax.experimental.pallas import tpu as pltpu

F32 = jnp.float32
BF16 = jnp.bfloat16
SDS = jax.ShapeDtypeStruct
MESH = pl.DeviceIdType.MESH

NORM_EPS = 1e-6
LRU_C = 8.0
N_DEV = 8
N_CHIP = 4
ADAM_LR, ADAM_B1, ADAM_B2, ADAM_EPS, ADAM_WD, ADAM_STEP = 0.001, 0.9, 0.999, 1e-08, 0.01, 10

NN = (((1,), (0,)), ((), ()))
NT = (((1,), (1,)), ((), ()))
TN = (((0,), (0,)), ((), ()))

SUBLANES = 8
BF16_ROWS = 16
LANES = 128
MIB = 1 << 20


def _dot(a, b, dims):
    return lax.dot_general(a, b, dims, preferred_element_type=F32)


def _blk(n, pref, align):
    if n <= pref:
        return n
    b = (pref // align) * align
    while b >= align:
        if n % b == 0:
            return b
        b -= align
    raise ValueError(f"no block of {n} aligned to {align} under {pref}")


def _cp(sem, vmem_mib):
    return pltpu.CompilerParams(dimension_semantics=sem, vmem_limit_bytes=vmem_mib * MIB)


HBM_SPEC = pl.BlockSpec(memory_space=pltpu.HBM)
MID_EIGHTHS = 5

class _Carried:
    def __init__(self, inputs, out_shapes, aliases, sem_shapes, build, has_mid=False):
        self.inputs, self.out_shapes, self.aliases = list(inputs), list(out_shapes), dict(aliases)
        self.sem_shapes, self.build, self.has_mid = list(sem_shapes), build, has_mid


def _call(body, *, name, grid, in_specs, out_specs, out_shape, scratch_shapes, compiler_params, args, carried=None):
    if carried is None:
        return pl.pallas_call(body, name=name, grid=grid, in_specs=in_specs, out_specs=out_specs, out_shape=out_shape,
                              scratch_shapes=scratch_shapes, compiler_params=compiler_params)(*args)
    n_in, n_out, n_sc = len(in_specs), len(out_shape), len(scratch_shapes)
    c_in, c_out = len(carried.inputs), len(carried.out_shapes)

    def hosted(*refs):
        ins, refs = refs[:n_in], refs[n_in:]
        c_ins, refs = refs[:c_in], refs[c_in:]
        outs, refs = refs[:n_out], refs[n_out:]
        c_outs, refs = refs[:c_out], refs[c_out:]
        scratch, c_sems = refs[:n_sc], refs[n_sc:]
        first = functools.reduce(jnp.logical_and, [pl.program_id(a) == 0 for a in range(len(grid))])
        last = functools.reduce(jnp.logical_and, [pl.program_id(a) == g - 1 for a, g in enumerate(grid)])

        @pl.when(first)
        def _():
            for start in carried.build(c_ins, c_outs, c_sems, "start"):
                start()

        if carried.has_mid:
            mid = functools.reduce(jnp.logical_and, [pl.program_id(0) == (grid[0] * MID_EIGHTHS) // 8]
                                   + [pl.program_id(a) == 0 for a in range(1, len(grid))])

            @pl.when(mid)
            def _():
                for step in carried.build(c_ins, c_outs, c_sems, "mid"):
                    step()

        body(*ins, *outs, *scratch)

        @pl.when(last)
        def _():
            for wait in carried.build(c_ins, c_outs, c_sems, "end"):
                wait()

    out = pl.pallas_call(
        hosted, name=name, grid=grid, in_specs=list(in_specs) + [HBM_SPEC] * c_in,
        out_specs=list(out_specs) + [HBM_SPEC] * c_out, out_shape=list(out_shape) + carried.out_shapes,
        scratch_shapes=list(scratch_shapes) + carried.sem_shapes,
        input_output_aliases={n_in + a: n_out + b for a, b in carried.aliases.items()},
        compiler_params=compiler_params)(*args, *carried.inputs)
    return out[:n_out], out[n_out:]


ROW_CHUNK = 128


def _chunk_rows(c):
    return pl.ds(pl.multiple_of(c * ROW_CHUNK, ROW_CHUNK), ROW_CHUNK)


def _rstd(xv):
    return lax.rsqrt(jnp.mean(xv * xv, axis=-1, keepdims=True) + NORM_EPS)


def _rms_bwd(xv, g, dn):
    r = _rstd(xv)
    xr = xv * r
    gd = g * dn
    dx = r * (gd - xr * jnp.mean(gd * xr, axis=-1, keepdims=True))
    return dx, jnp.sum(dn * xr, axis=0, keepdims=True)


def _log1p(e):
    u = 1.0 + e
    return jnp.where(u == 1.0, e, jnp.log(u) * (e / (u - 1.0)))


def _one_minus_exp(v, exp_half_v):
    series = 1.0 / 5040.0
    for coeff in (1.0 / 720.0, 1.0 / 120.0, 1.0 / 24.0, 1.0 / 6.0, 0.5, 1.0):
        series = series * v + coeff
    return jnp.where(v > -0.5, -v * series, 1.0 - exp_half_v * exp_half_v)


def _sigmoid(v):
    return 0.5 * jnp.tanh(0.5 * v) + 0.5


def _gelu_parts(g):
    k0 = math.sqrt(2.0 / math.pi)
    g2 = g * g
    t = jnp.tanh(k0 * (g + 0.044715 * g * g2))
    gel = 0.5 * g * (1.0 + t)
    gelp = 0.5 * (1.0 + t) + 0.5 * g * (1.0 - t * t) * (k0 * (1.0 + 3.0 * 0.044715 * g2))
    return gel, gelp


def _norm_proj(x, gain, w_list, out_dtypes, swiglu, name, carried=None):
    T, D = x.shape
    N = w_list[0].shape[0]
    nw = len(w_list)
    bm = _blk(T, 1024, BF16_ROWS)
    bn = _blk(N, 512, LANES)

    def body(*refs):
        x_ref, g_ref = refs[:2]
        w_refs = refs[2:2 + nw]
        n_ref = refs[2 + nw]
        o_refs = refs[3 + nw:3 + 2 * nw]
        act_ref = refs[3 + 2 * nw] if swiglu else None
        n_sc = refs[-1]

        @pl.when(pl.program_id(1) == 0)
        def _():
            def chunk(c, _):
                r = _chunk_rows(c)
                xv = x_ref[r, :]
                nb = (xv * _rstd(xv) * g_ref[...]).astype(BF16)
                n_sc[r, :] = nb
                n_ref[r, :] = nb
                return 0
            lax.fori_loop(0, bm // ROW_CHUNK, chunk, 0)

        n = n_sc[...]
        outs = [_dot(n, w_ref[...], NT) for w_ref in w_refs]
        for o_ref, o in zip(o_refs, outs):
            o_ref[...] = o.astype(o_ref.dtype)
        if swiglu:
            hg, hu = outs
            act_ref[...] = (hg * _sigmoid(hg) * hu).astype(BF16)

    row = pl.BlockSpec((bm, D), lambda i, j: (i, 0))
    tile = pl.BlockSpec((bm, bn), lambda i, j: (i, j))
    n_extra = 1 if swiglu else 0
    return _call(
        body, name=name, grid=(T // bm, N // bn),
        in_specs=[row, pl.BlockSpec((1, D), lambda i, j: (0, 0))] + [pl.BlockSpec((bn, D), lambda i, j: (j, 0))] * nw,
        out_specs=[row] + [tile] * (nw + n_extra),
        out_shape=[SDS((T, D), BF16)] + [SDS((T, N), dt) for dt in out_dtypes] + [SDS((T, N), BF16)] * n_extra,
        scratch_shapes=[pltpu.VMEM((bm, D), BF16)],
        compiler_params=_cp(("arbitrary", "arbitrary"), 52),
        args=(x, gain, *w_list), carried=carried)


def _up_act(n, wu, hg, name, carried=None):
    T, D = n.shape
    F = wu.shape[0]
    bm = _blk(T, 1024, BF16_ROWS)
    bn = _blk(F, 512, LANES)

    def body(n_ref, wu_ref, hg_ref, hu_ref, act_ref):
        hu = _dot(n_ref[...], wu_ref[...], NT)
        hg = hg_ref[...].astype(F32)
        hu_ref[...] = hu.astype(BF16)
        act_ref[...] = (hg * _sigmoid(hg) * hu).astype(BF16)

    tile = pl.BlockSpec((bm, bn), lambda i, j: (i, j))
    return _call(
        body, name=name, grid=(T // bm, F // bn),
        in_specs=[pl.BlockSpec((bm, D), lambda i, j: (i, 0)), pl.BlockSpec((bn, D), lambda i, j: (j, 0)), tile],
        out_specs=[tile, tile], out_shape=[SDS((T, F), BF16)] * 2, scratch_shapes=[],
        compiler_params=_cp(("arbitrary", "arbitrary"), 40),
        args=(n, wu, hg), carried=carried)


def _mm_res(a, b, x, scale, name, carried=None):
    T, K = a.shape
    D = b.shape[1]
    bm = _blk(T, 1024, BF16_ROWS)
    bk = _blk(K, 1408, LANES)
    nk = K // bk

    def body(a_ref, b_ref, x_ref, o_ref):
        k = pl.program_id(1)

        @pl.when(k == 0)
        def _():
            o_ref[...] = jnp.zeros_like(o_ref)

        o_ref[...] += _dot(a_ref[...], b_ref[...], NN)

        @pl.when(k == nk - 1)
        def _():
            def chunk(c, _):
                r = _chunk_rows(c)
                o_ref[r, :] = x_ref[r, :] + scale * o_ref[r, :]
                return 0
            lax.fori_loop(0, bm // ROW_CHUNK, chunk, 0)

    row = pl.BlockSpec((bm, D), lambda i, k: (i, 0))
    out = _call(
        body, name=name, grid=(T // bm, nk),
        in_specs=[pl.BlockSpec((bm, bk), lambda i, k: (i, k)), pl.BlockSpec((bk, D), lambda i, k: (k, 0)), row],
        out_specs=[row], out_shape=[SDS((T, D), F32)], scratch_shapes=[],
        compiler_params=_cp(("arbitrary", "arbitrary"), 56),
        args=(a, b, x), carried=carried)
    return out[0] if carried is None else (out[0][0], out[1])


def _mm_nt(a, b, name):
    T, K = a.shape
    N = b.shape[0]
    bm = _blk(T, 1024, BF16_ROWS)
    bn = _blk(N, 512, LANES)

    def body(a_ref, b_ref, o_ref):
        o_ref[...] = _dot(a_ref[...], b_ref[...], NT)

    return pl.pallas_call(
        body, name=name, grid=(T // bm, N // bn),
        in_specs=[pl.BlockSpec((bm, K), lambda i, j: (i, 0)), pl.BlockSpec((bn, K), lambda i, j: (j, 0))],
        out_specs=pl.BlockSpec((bm, bn), lambda i, j: (i, j)), out_shape=SDS((T, N), F32),
        compiler_params=_cp(("arbitrary", "arbitrary"), 40),
    )(a, b)


def _ffn_bwd_act(dfb, wd, hg, hu, name):
    T, D = dfb.shape
    F = wd.shape[0]
    bm = _blk(T, 1024, BF16_ROWS)
    bn = _blk(F, 512, LANES)

    def body(df_ref, wd_ref, hg_ref, hu_ref, dhg_ref, dhu_ref):
        dact = _dot(df_ref[...], wd_ref[...], NT)
        hgv = hg_ref[...].astype(F32)
        huv = hu_ref[...].astype(F32)
        s = _sigmoid(hgv)
        dhu_ref[...] = (dact * (hgv * s)).astype(BF16)
        dhg_ref[...] = (dact * huv * (s * (1.0 + hgv * (1.0 - s)))).astype(BF16)

    tile = pl.BlockSpec((bm, bn), lambda i, j: (i, j))
    return pl.pallas_call(
        body, name=name, grid=(T // bm, F // bn),
        in_specs=[pl.BlockSpec((bm, D), lambda i, j: (i, 0)), pl.BlockSpec((bn, D), lambda i, j: (j, 0)), tile, tile],
        out_specs=[tile, tile], out_shape=[SDS((T, F), BF16)] * 2,
        compiler_params=_cp(("arbitrary", "arbitrary"), 40),
    )(dfb, wd, hg, hu)


def _dw_tn(a, b, bm_pref, name):
    T, M = a.shape
    N = b.shape[1]
    bm = _blk(M, bm_pref, LANES)
    tk = _blk(T, 1024, BF16_ROWS)
    nk = T // tk

    def body(a_ref, b_ref, o_ref, acc):
        k = pl.program_id(1)

        @pl.when(k == 0)
        def _():
            acc[...] = jnp.zeros_like(acc)

        acc[...] += _dot(a_ref[...], b_ref[...], TN)

        @pl.when(k == nk - 1)
        def _():
            o_ref[...] = acc[...].astype(BF16)

    return pl.pallas_call(
        body, name=name, grid=(M // bm, nk),
        in_specs=[pl.BlockSpec((tk, bm), lambda i, k: (k, i)), pl.BlockSpec((tk, N), lambda i, k: (k, 0))],
        out_specs=pl.BlockSpec((bm, N), lambda i, k: (i, 0)), out_shape=SDS((M, N), BF16),
        scratch_shapes=[pltpu.VMEM((bm, N), F32)],
        compiler_params=_cp(("arbitrary", "arbitrary"), 48),
    )(a, b)


def _mm_rmsbwd(pairs, x, gain, dx_in, bscale, name, carried=None):
    T, D = x.shape
    K = pairs[0][0].shape[1]
    npair = len(pairs)
    bm = _blk(T, 1024, BF16_ROWS)
    bk = _blk(K, 1024 // npair, LANES)
    nk = K // bk

    nchunk = bm // ROW_CHUNK

    def body(*refs):
        ab = refs[:2 * npair]
        x_hbm, g_ref, dxin_hbm, dx_ref, dxb_ref, dg_ref, x_buf, dxin_buf, sems = refs[2 * npair:]
        i = pl.program_id(0)
        k = pl.program_id(1)

        def fetch(c, slot):
            rows = pl.ds(i * bm + c * ROW_CHUNK, ROW_CHUNK)
            return (pltpu.make_async_copy(x_hbm.at[rows, :], x_buf.at[slot], sems.at[slot, 0]),
                    pltpu.make_async_copy(dxin_hbm.at[rows, :], dxin_buf.at[slot], sems.at[slot, 1]))

        @pl.when(k == 0)
        def _():
            dx_ref[...] = jnp.zeros_like(dx_ref)

        @pl.when(k == nk - 1)
        def _():
            for cp in fetch(0, 0):
                cp.start()

        for q in range(npair):
            dx_ref[...] += _dot(ab[2 * q][...], ab[2 * q + 1][...], NN)

        @pl.when(k == nk - 1)
        def _():
            @pl.when(i == 0)
            def _():
                dg_ref[...] = jnp.zeros_like(dg_ref)

            def chunk(c, _):
                slot = c % 2

                @pl.when(c + 1 < nchunk)
                def _():
                    for cp in fetch(c + 1, 1 - slot):
                        cp.start()

                for cp in fetch(c, slot):
                    cp.wait()

                r = _chunk_rows(c)
                dx, dg = _rms_bwd(x_buf[slot], g_ref[...], dx_ref[r, :])
                dxo = dxin_buf[slot] + dx
                dx_ref[r, :] = dxo
                dxb_ref[r, :] = (bscale * dxo).astype(BF16)
                dg_ref[...] += dg
                return 0
            lax.fori_loop(0, nchunk, chunk, 0)

    row = pl.BlockSpec((bm, D), lambda i, k: (i, 0))
    anywhere = pl.BlockSpec(memory_space=pl.ANY)
    vec = pl.BlockSpec((1, D), lambda i, k: (0, 0))
    in_specs = []
    args = []
    for a, b in pairs:
        in_specs += [pl.BlockSpec((bm, bk), lambda i, k: (i, k)), pl.BlockSpec((bk, D), lambda i, k: (k, 0))]
        args += [a, b]
    return _call(
        body, name=name, grid=(T // bm, nk),
        in_specs=in_specs + [anywhere, vec, anywhere], out_specs=[row, row, vec],
        out_shape=[SDS((T, D), F32), SDS((T, D), BF16), SDS((1, D), F32)],
        scratch_shapes=[pltpu.VMEM((2, ROW_CHUNK, D), F32), pltpu.VMEM((2, ROW_CHUNK, D), F32),
                        pltpu.SemaphoreType.DMA((2, 2))],
        compiler_params=_cp(("arbitrary", "arbitrary"), 52),
        args=(*args, x, gain, dx_in), carried=carried)


def _loss_head(x3, gain, tgt, name):
    T, D = x3.shape
    bm = _blk(T, 256, BF16_ROWS)

    def body(x_ref, g_ref, t_ref, dx_ref, dxb_ref, dg_ref, loss_ref):
        i = pl.program_id(0)
        xv = x_ref[...]
        g = g_ref[...]
        out = xv * _rstd(xv) * g
        e = out - t_ref[...]
        part = 0.5 * jnp.sum(jnp.mean(e * e, axis=-1, keepdims=True), axis=0, keepdims=True)
        dx, dg = _rms_bwd(xv, g, e * (1.0 / D))
        dx_ref[...] = dx
        dxb_ref[...] = (0.5 * dx).astype(BF16)

        @pl.when(i == 0)
        def _():
            dg_ref[...] = dg
            loss_ref[...] = jnp.broadcast_to(part, loss_ref.shape)

        @pl.when(i > 0)
        def _():
            dg_ref[...] += dg
            loss_ref[...] += jnp.broadcast_to(part, loss_ref.shape)

    row = pl.BlockSpec((bm, D), lambda i: (i, 0))
    vec = pl.BlockSpec((1, D), lambda i: (0, 0))
    return pl.pallas_call(
        body, name=name, grid=(T // bm,),
        in_specs=[row, vec, row], out_specs=[row, row, vec, pl.BlockSpec((SUBLANES, LANES), lambda i: (0, 0))],
        out_shape=[SDS((T, D), F32), SDS((T, D), BF16), SDS((1, D), F32), SDS((SUBLANES, LANES), F32)],
        compiler_params=_cp(("arbitrary",), 40),
    )(x3, gain, tgt)


R_CW, R_CB, R_BA, R_BI, R_LAM, R_SW, R_GLO, R_GSO, SMALL_ROWS = 0, 4, 5, 6, 7, 8, 11, 12, 16


def _rows(g):
    return pl.ds(pl.multiple_of(g * SUBLANES, SUBLANES), SUBLANES)


def _shift_back(prev, cur, d):
    row = lax.broadcasted_iota(jnp.int32, cur.shape, 0)
    return pltpu.roll(jnp.where(row >= SUBLANES - d, prev, cur), d, 0)


def _shift_fwd(cur, nxt, d):
    row = lax.broadcasted_iota(jnp.int32, cur.shape, 0)
    return pltpu.roll(jnp.where(row < d, nxt, cur), SUBLANES - d, 0)


def _causal_conv(ext, g, taps_ref, ntap):
    prev = ext[_rows(g), :]
    cur = ext[_rows(g + 1), :]
    out = _shift_back(prev, cur, ntap - 1) * taps_ref[0:1, :]
    for k in range(1, ntap - 1):
        out = out + _shift_back(prev, cur, ntap - 1 - k) * taps_ref[k:k + 1, :]
    return out + cur * taps_ref[ntap - 1:ntap, :]


def _scan8(A, U, reverse):
    row = lax.broadcasted_iota(jnp.int32, A.shape, 0)
    for s in (1, 2, 4):
        if reverse:
            A_sh = pltpu.roll(A, SUBLANES - s, 0)
            U_sh = pltpu.roll(U, SUBLANES - s, 0)
            m = row < SUBLANES - s
        else:
            A_sh = pltpu.roll(A, s, 0)
            U_sh = pltpu.roll(U, s, 0)
            m = row >= s
        U = jnp.where(m, A * U_sh + U, U)
        A = jnp.where(m, A * A_sh, A)
    return A, U


def _gate_pre(xc_s, w_ref, out_s, H, hd):
    for h in range(H):
        cs = slice(h * hd, (h + 1) * hd)
        out_s[:, cs] = _dot(xc_s[:, cs].astype(BF16), w_ref[h], NN)


def _lru_coeffs(pa, pi, xc, ba, bi, sp):
    ra = _sigmoid(pa + ba)
    ri = _sigmoid(pi + bi)
    log_a = (-LRU_C * ra) * sp
    a = jnp.exp(log_a)
    mult = jnp.sqrt(_one_minus_exp(2.0 * log_a, a))
    return ra, ri, a, mult


def _softplus_neg(lam):
    v = -lam
    return jnp.maximum(v, 0.0) + _log1p(jnp.exp(-jnp.abs(v)))


def _mix_fwd(z, cw, cb, wa, ba, wi, bi, lam, sw, glo, gso, name):
    T = z.shape[0]
    C = z.shape[1] // 5
    H = wa.shape[0]
    hd = C // H
    tb = _blk(T, 256, BF16_ROWS)
    ng = tb // SUBLANES
    HDR = SUBLANES

    def body(z_ref, cw_ref, cb_ref, wa_ref, ba_ref, wi_ref, bi_ref, lam_ref, sw_ref, glo_ref, gso_ref,
             y_ref, h_ref, xext, pext, xc_s, pa_s, pi_s, y_s, hcar):
        @pl.when(pl.program_id(0) == 0)
        def _():
            xext[0:HDR, :] = jnp.zeros((HDR, C), F32)
            pext[0:HDR, :] = jnp.zeros((HDR, C), F32)
            hcar[...] = jnp.zeros_like(hcar)

        def fill(g, _):
            r = _rows(g)
            re = _rows(g + 1)
            xext[re, :] = z_ref[r, 0:C]
            pext[re, :] = z_ref[r, 3 * C:4 * C] * z_ref[r, 4 * C:5 * C]
            return 0
        lax.fori_loop(0, ng, fill, 0)

        def conv(g, _):
            xc_s[_rows(g), :] = _causal_conv(xext, g, cw_ref, 4) + cb_ref[...]
            return 0
        lax.fori_loop(0, ng, conv, 0)

        _gate_pre(xc_s, wa_ref, pa_s, H, hd)
        _gate_pre(xc_s, wi_ref, pi_s, H, hd)
        sp = _softplus_neg(lam_ref[...])

        def group(g, hprev):
            r = _rows(g)
            xc = xc_s[r, :]
            _, ri, a, mult = _lru_coeffs(pa_s[r, :], pi_s[r, :], xc, ba_ref[...], bi_ref[...], sp)
            A, U = _scan8(a, mult * (ri * xc), reverse=False)
            hh = A * hprev + U
            h_ref[r, :] = hh
            gel, _ = _gelu_parts(z_ref[r, C:2 * C])
            y_lru = hh * gel
            y_s[r, 0:C] = y_lru * _rstd(y_lru) * glo_ref[...]
            y_sc = z_ref[r, 2 * C:3 * C] * _causal_conv(pext, g, sw_ref, 3)
            y_s[r, C:2 * C] = y_sc * _rstd(y_sc) * gso_ref[...]
            return jnp.broadcast_to(hh[SUBLANES - 1:SUBLANES, :], hh.shape)
        hcar[...] = lax.fori_loop(0, ng // 2, lambda t, hp: group(2 * t + 1, group(2 * t, hp)), hcar[...])

        xext[0:HDR, :] = xext[tb:tb + HDR, :]
        pext[0:HDR, :] = pext[tb:tb + HDR, :]

        def cast(g, _):
            r = pl.ds(pl.multiple_of(g * BF16_ROWS, BF16_ROWS), BF16_ROWS)
            y_ref[r, :] = y_s[r, :].astype(BF16)
            return 0
        lax.fori_loop(0, tb // BF16_ROWS, cast, 0)

    full = lambda shape: pl.BlockSpec(shape, lambda i: (0,) * len(shape))
    blk = lambda w: pl.BlockSpec((tb, w), lambda i: (i, 0))
    ext = pltpu.VMEM((tb + HDR, C), F32)
    tile = pltpu.VMEM((tb, C), F32)
    return pl.pallas_call(
        body, name=name, grid=(T // tb,),
        in_specs=[blk(5 * C), full((4, C)), full((1, C)), full((H, hd, hd)), full((1, C)), full((H, hd, hd)),
                  full((1, C)), full((1, C)), full((3, C)), full((1, C)), full((1, C))],
        out_specs=[blk(2 * C), blk(C)],
        out_shape=[SDS((T, 2 * C), BF16), SDS((T, C), F32)],
        scratch_shapes=[ext, ext, tile, tile, tile, pltpu.VMEM((tb, 2 * C), F32), pltpu.VMEM((SUBLANES, C), F32)],
        compiler_params=_cp(("arbitrary",), 40),
    )(z, cw, cb, wa, ba, wi, bi, lam, sw, glo, gso)


def _mix_bwd(z, h, dy, cw, cb, wa, ba, wi, bi, lam, sw, glo, gso, name):
    T = z.shape[0]
    C = z.shape[1] // 5
    H = wa.shape[0]
    hd = C // H
    tb = _blk(T, 256, BF16_ROWS)
    nb = T // tb
    ng = tb // SUBLANES
    HDR = SUBLANES
    N_ACC = 13

    def body(z_ref, zp_ref, h_ref, hp_ref, dy_ref, cw_ref, cb_ref, wa_ref, ba_ref, wi_ref, bi_ref, lam_ref,
             sw_ref, glo_ref, gso_ref, dz_ref, small_ref, dwa_ref, dwi_ref,
             xext, pext, hext, dqext, dxcext, bext, xc_s, pa_s, pi_s, a_s, m_s, ri_s, dh_s, dpa_s, dpi_s,
             dz_s, acc_s, bcar):
        i = pl.program_id(0)
        first_rows = i == nb - 1

        @pl.when(i == 0)
        def _():
            dqext[tb:tb + HDR, :] = jnp.zeros((HDR, C), F32)
            dxcext[tb:tb + HDR, :] = jnp.zeros((HDR, C), F32)
            bcar[...] = jnp.zeros_like(bcar)
            acc_s[...] = jnp.zeros_like(acc_s)
            dwa_ref[...] = jnp.zeros_like(dwa_ref)
            dwi_ref[...] = jnp.zeros_like(dwi_ref)

        zero = jnp.zeros((HDR, C), F32)
        xext[0:HDR, :] = jnp.where(first_rows, zero, zp_ref[:, 0:C])
        pext[0:HDR, :] = jnp.where(first_rows, zero, zp_ref[:, 3 * C:4 * C] * zp_ref[:, 4 * C:5 * C])
        hext[0:HDR, :] = jnp.where(first_rows, zero, hp_ref[...])

        def fill(g, _):
            r = _rows(g)
            re = _rows(g + 1)
            xext[re, :] = z_ref[r, 0:C]
            pext[re, :] = z_ref[r, 3 * C:4 * C] * z_ref[r, 4 * C:5 * C]
            hext[re, :] = h_ref[r, :]
            return 0
        lax.fori_loop(0, ng, fill, 0)

        def conv(g, _):
            xc_s[_rows(g), :] = _causal_conv(xext, g, cw_ref, 4) + cb_ref[...]
            return 0
        lax.fori_loop(0, ng, conv, 0)

        _gate_pre(xc_s, wa_ref, pa_s, H, hd)
        _gate_pre(xc_s, wi_ref, pi_s, H, hd)
        sp = _softplus_neg(lam_ref[...])
        dsp_dlam = -jax.nn.sigmoid(-lam_ref[...])

        def add_acc(k, v):
            acc_s[k] += v

        def p1(g, _):
            r = _rows(g)
            xc = xc_s[r, :]
            _, ri, a, mult = _lru_coeffs(pa_s[r, :], pi_s[r, :], xc, ba_ref[...], bi_ref[...], sp)
            a_s[r, :] = a
            m_s[r, :] = mult
            ri_s[r, :] = ri
            hh = h_ref[r, :]
            gel, gelp = _gelu_parts(z_ref[r, C:2 * C])
            y_lru = hh * gel
            dnl = dy_ref[r, 0:C]
            rl = _rstd(y_lru)
            ylr = y_lru * rl
            gd = glo_ref[...] * dnl
            dy_lru = rl * (gd - ylr * jnp.mean(gd * ylr, axis=-1, keepdims=True))
            add_acc(R_GLO, dnl * ylr)
            dz_s[r, C:2 * C] = dy_lru * hh * gelp
            dh = dy_lru * gel
            dh_s[r, :] = dh

            q = _causal_conv(pext, g, sw_ref, 3)
            scb = z_ref[r, 2 * C:3 * C]
            y_sc = scb * q
            dns = dy_ref[r, C:2 * C]
            rs = _rstd(y_sc)
            ysr = y_sc * rs
            gs = gso_ref[...] * dns
            dy_sc = rs * (gs - ysr * jnp.mean(gs * ysr, axis=-1, keepdims=True))
            add_acc(R_GSO, dns * ysr)
            dz_s[r, 2 * C:3 * C] = dy_sc * q
            dqext[r, :] = dy_sc * scb
            return 0
        lax.fori_loop(0, ng, p1, 0, unroll=2)

        bext[tb:tb + HDR, :] = bcar[...]

        def p2(j, carry):
            g = ng - 1 - j
            r = _rows(g)
            a = a_s[r, :]
            A, U = _scan8(a, a * dh_s[r, :], reverse=True)
            bb = A * carry + U
            bext[r, :] = bb
            return jnp.broadcast_to(bb[0:1, :], bb.shape)
        bcar[...] = lax.fori_loop(0, ng, p2, bcar[...])

        def p3(g, _):
            r = _rows(g)
            rn = _rows(g + 1)
            G = dh_s[r, :] + _shift_fwd(bext[r, :], bext[rn, :], 1)
            hm1 = _shift_back(hext[r, :], hext[rn, :], 1)
            a = a_s[r, :]
            mult = m_s[r, :]
            ri = ri_s[r, :]
            xc = xc_s[r, :]
            ra = _sigmoid(pa_s[r, :] + ba_ref[...])
            dxcext[r, :] = G * mult * ri
            dri = G * mult * xc
            dmult = G * ri * xc
            dlog_a = (G * hm1) * a - dmult * (a * a) / mult
            add_acc(R_LAM, dlog_a * (-LRU_C * ra) * dsp_dlam)
            dpa = dlog_a * (-LRU_C * sp) * ra * (1.0 - ra)
            dpi = dri * ri * (1.0 - ri)
            add_acc(R_BA, dpa)
            add_acc(R_BI, dpi)
            dpa_s[r, :] = dpa
            dpi_s[r, :] = dpi
            return 0
        lax.fori_loop(0, ng, p3, 0)

        for hh_ in range(H):
            cs = slice(hh_ * hd, (hh_ + 1) * hd)
            dpa_b = dpa_s[:, cs].astype(BF16)
            dpi_b = dpi_s[:, cs].astype(BF16)
            xc_b = xc_s[:, cs].astype(BF16)
            dxcext[0:tb, cs] += _dot(dpa_b, wa_ref[hh_], NT) + _dot(dpi_b, wi_ref[hh_], NT)
            dwa_ref[hh_] += _dot(xc_b, dpa_b, TN)
            dwi_ref[hh_] += _dot(xc_b, dpi_b, TN)

        def p4(g, _):
            r = _rows(g)
            rn = _rows(g + 1)
            dxc = dxcext[r, :]
            dxc_n = dxcext[rn, :]
            x_p = xext[r, :]
            x_c = xext[rn, :]
            add_acc(R_CB, dxc)
            dlx = dxc * cw_ref[3:4, :]
            add_acc(R_CW + 3, dxc * x_c)
            for d in range(1, 4):
                dlx = dlx + _shift_fwd(dxc, dxc_n, d) * cw_ref[3 - d:4 - d, :]
                add_acc(R_CW + 3 - d, dxc * _shift_back(x_p, x_c, d))
            dz_s[r, 0:C] = dlx

            dq = dqext[r, :]
            dq_n = dqext[rn, :]
            p_p = pext[r, :]
            p_c = pext[rn, :]
            dp = dq * sw_ref[2:3, :]
            add_acc(R_SW + 2, dq * p_c)
            for d in range(1, 3):
                dp = dp + _shift_fwd(dq, dq_n, d) * sw_ref[2 - d:3 - d, :]
                add_acc(R_SW + 2 - d, dq * _shift_back(p_p, p_c, d))
            dz_s[r, 3 * C:4 * C] = dp * z_ref[r, 4 * C:5 * C]
            dz_s[r, 4 * C:5 * C] = dp * z_ref[r, 3 * C:4 * C]
            return 0
        lax.fori_loop(0, ng, p4, 0)

        dqext[tb:tb + HDR, :] = dqext[0:HDR, :]
        dxcext[tb:tb + HDR, :] = dxcext[0:HDR, :]

        def cast(g, _):
            r = pl.ds(pl.multiple_of(g * BF16_ROWS, BF16_ROWS), BF16_ROWS)
            dz_ref[r, :] = dz_s[r, :].astype(BF16)
            return 0
        lax.fori_loop(0, tb // BF16_ROWS, cast, 0)

        @pl.when(i == nb - 1)
        def _():
            small_ref[...] = jnp.zeros_like(small_ref)
            for k in range(N_ACC):
                small_ref[k:k + 1, :] = jnp.sum(acc_s[k], axis=0, keepdims=True)

    tpg = tb // SUBLANES
    full = lambda shape: pl.BlockSpec(shape, lambda i: (0,) * len(shape))
    blk = lambda w: pl.BlockSpec((tb, w), lambda i: (nb - 1 - i, 0))
    prev = lambda w: pl.BlockSpec((SUBLANES, w), lambda i: (jnp.maximum((nb - 1 - i) * tpg - 1, 0), 0))
    ext = pltpu.VMEM((tb + HDR, C), F32)
    tile = pltpu.VMEM((tb, C), F32)
    return pl.pallas_call(
        body, name=name, grid=(nb,),
        in_specs=[blk(5 * C), prev(5 * C), blk(C), prev(C), blk(2 * C), full((4, C)), full((1, C)), full((H, hd, hd)),
                  full((1, C)), full((H, hd, hd)), full((1, C)), full((1, C)), full((3, C)), full((1, C)), full((1, C))],
        out_specs=[blk(5 * C), full((SMALL_ROWS, C)), full((H, hd, hd)), full((H, hd, hd))],
        out_shape=[SDS((T, 5 * C), BF16), SDS((SMALL_ROWS, C), F32), SDS((H, hd, hd), F32), SDS((H, hd, hd), F32)],
        scratch_shapes=[ext] * 6 + [tile] * 9 + [pltpu.VMEM((tb, 5 * C), F32), pltpu.VMEM((N_ACC, SUBLANES, C), F32),
                                                pltpu.VMEM((SUBLANES, C), F32)],
        compiler_params=_cp(("arbitrary",), 56),
    )(z, z, h, h, dy, cw, cb, wa, ba, wi, bi, lam, sw, glo, gso)


def _add_slabs(terms, out_dtype, name):
    R, Ccols = terms[0].shape
    br = _blk(R, 512, BF16_ROWS)
    n = len(terms)

    def body(*refs):
        s = refs[0][...].astype(F32)
        for t_ref in refs[1:n]:
            s = s + t_ref[...].astype(F32)
        refs[n][...] = s.astype(out_dtype)

    spec = pl.BlockSpec((br, Ccols), lambda i: (i, 0))
    return pl.pallas_call(
        body, name=name, grid=(R // br,), in_specs=[spec] * n, out_specs=spec, out_shape=SDS((R, Ccols), out_dtype),
        compiler_params=_cp(("arbitrary",), 40),
    )(*terms)


def _final_grad(sb, lb, chip, name):
    _, R, Ccols = sb.shape
    br = _blk(R, 512, BF16_ROWS)

    def body(chip_ref, sb_ref, l0, l1, l2, o_ref):
        s = sb_ref[0].astype(F32)
        for t_ref in (l0, l1, l2):
            s = s + t_ref[0].astype(F32)
        o_ref[...] = s

    lspec = lambda k: pl.BlockSpec((1, br, Ccols), lambda i, c: (k, i, 0))
    return pl.pallas_call(
        body, name=name,
        grid_spec=pltpu.PrefetchScalarGridSpec(
            num_scalar_prefetch=1, grid=(R // br,),
            in_specs=[pl.BlockSpec((1, br, Ccols), lambda i, c: (c[0], i, 0)), lspec(0), lspec(1), lspec(2)],
            out_specs=pl.BlockSpec((br, Ccols), lambda i, c: (i, 0))),
        out_shape=SDS((R, Ccols), F32),
        compiler_params=_cp(("arbitrary",), 40),
    )(chip, sb, lb, lb, lb)


def _adamw(w, g, m, v, name):
    R, Ccols = w.shape
    br = _blk(R, 256, SUBLANES)
    c1 = 1.0 - ADAM_B1 ** ADAM_STEP
    c2 = 1.0 - ADAM_B2 ** ADAM_STEP

    def body(w_ref, g_ref, m_ref, v_ref, d_ref, nm_ref, nv_ref):
        gv = g_ref[...]
        nm = ADAM_B1 * m_ref[...] + (1.0 - ADAM_B1) * gv
        nv = ADAM_B2 * v_ref[...] + (1.0 - ADAM_B2) * (gv * gv)
        nm_ref[...] = nm
        nv_ref[...] = nv
        d_ref[...] = -ADAM_LR * ((nm / c1) / (jnp.sqrt(nv / c2) + ADAM_EPS) + ADAM_WD * w_ref[...])

    spec = pl.BlockSpec((br, Ccols), lambda i: (i, 0))
    return pl.pallas_call(
        body, name=name, grid=(R // br,), in_specs=[spec] * 4, out_specs=[spec] * 3,
        out_shape=[SDS((R, Ccols), F32)] * 3, compiler_params=_cp(("arbitrary",), 40),
    )(w, g, m, v)


def _place():
    return lax.axis_index("x"), lax.axis_index("y"), lax.axis_index("c")


def _dev_rows(ref, dev, rows):
    return ref.at[pl.ds((4 * dev[0] + 2 * dev[1] + dev[2]) * rows, rows), :]


def _remote(src, dst, send_sem, recv_sem, to):
    return pltpu.make_async_remote_copy(src_ref=src, dst_ref=dst, send_sem=send_sem, recv_sem=recv_sem,
                                        device_id=to, device_id_type=MESH)


SAME_CORE_AND_SIBLING = ((0, 0, 1), (1, 0, 0), (0, 1, 0), (1, 1, 0))


def _merge_phases(a, b):
    na_in, na_out, na_sem = len(a.inputs), len(a.out_shapes), len(a.sem_shapes)

    def build(ins, outs, sems, stage):
        return (a.build(ins[:na_in], outs[:na_out], sems[:na_sem], stage)
                + b.build(ins[na_in:], outs[na_out:], sems[na_sem:], stage))

    aliases = dict(a.aliases)
    aliases.update({na_in + i: na_out + o for i, o in b.aliases.items()})
    return _Carried(a.inputs + b.inputs, a.out_shapes + b.out_shapes, aliases, a.sem_shapes + b.sem_shapes, build,
                    has_mid=a.has_mid or b.has_mid)


def _ag_direct_phase(slab, pieces, flips):
    W = slab.shape[1]
    n = len(pieces)
    npeer = len(flips)

    def build(ins, outs, sems, stage):
        if stage == "mid":
            return []
        starting = stage == "start"
        (slab_ref,) = ins
        send_sems, recv_sems, local_sems = sems
        x, y, c = _place()
        me = (x, y, c)
        peers = [tuple(1 - v if f else v for v, f in zip(me, flip)) for flip in flips]
        todo = []
        for p, (off, rows) in enumerate(pieces):
            src = slab_ref.at[pl.ds(off, rows), :]
            mine = pltpu.make_async_copy(src, _dev_rows(outs[p], me, rows), local_sems.at[p])
            todo.append(mine.start if starting else mine.wait)
            for k, peer in enumerate(peers):
                snd = _remote(src, _dev_rows(outs[p], me, rows), send_sems.at[k, p], recv_sems.at[k, p], peer)
                if starting:
                    todo.append(snd.start)
                else:
                    theirs = _dev_rows(outs[p], peer, rows)
                    rcv = _remote(theirs, theirs, send_sems.at[k, p], recv_sems.at[k, p], me)
                    todo += [rcv.wait_recv, snd.wait_send]
        return todo

    dma = pltpu.SemaphoreType.DMA
    return _Carried([slab], [SDS((N_DEV * rows, W), slab.dtype) for _, rows in pieces], {},
                    [dma((npeer, n)), dma((npeer, n)), dma((n,))], build)


def _ag_two_level_phase(slab, pieces):
    W = slab.shape[1]
    n = len(pieces)

    def build(ins, outs, sems, stage):
        (slab_ref,) = ins
        send_sems, recv_sems, local_sems = sems
        x, y, c = _place()
        me, sibling = (x, y, c), (x, y, 1 - c)
        chips = [(1 - x, y), (x, 1 - y), (1 - x, 1 - y)]
        todo = []
        for p, (off, rows) in enumerate(pieces):
            src = slab_ref.at[pl.ds(off, rows), :]
            own = _dev_rows(outs[p], me, rows)
            landed = [_dev_rows(outs[p], (*chip, c), rows) for chip in chips]

            def mine():
                return pltpu.make_async_copy(src, own, local_sems.at[p])

            def first():
                return [_remote(src, own, send_sems.at[k, p], recv_sems.at[k, p], to)
                        for k, to in enumerate([sibling] + [(*chip, c) for chip in chips])]

            def passed():
                return [_remote(blk, blk, send_sems.at[4 + j, p], recv_sems.at[4 + j, p], sibling)
                        for j, blk in enumerate(landed)]

            def arrival(k, blk):
                return _remote(blk, blk, send_sems.at[k, p], recv_sems.at[k, p], me).wait_recv

            if stage == "start":
                todo += [mine().start] + [cp.start for cp in first()]
            elif stage == "mid":
                for j, (blk, fwd) in enumerate(zip(landed, passed())):
                    todo += [arrival(1 + j, blk), fwd.start]
            else:
                theirs = [_dev_rows(outs[p], sibling, rows)] + [_dev_rows(outs[p], (*chip, 1 - c), rows) for chip in chips]
                todo += [arrival(k, blk) for k, blk in zip((0, 4, 5, 6), theirs)]
                todo += [cp.wait_send for cp in first() + passed()] + [mine().wait]
        return todo

    dma = pltpu.SemaphoreType.DMA
    return _Carried([slab], [SDS((N_DEV * rows, W), slab.dtype) for _, rows in pieces], {},
                    [dma((7, n)), dma((7, n)), dma((n,))], build, has_mid=True)


def _ag_forward_phase(gathered, pieces):
    n = len(pieces)

    def build(ins, outs, sems, stage):
        if stage == "mid":
            return []
        starting = stage == "start"
        send_sems, recv_sems = sems
        x, y, c = _place()
        me, sibling = (x, y, c), (x, y, 1 - c)
        chips = [(1 - x, y), (x, 1 - y), (1 - x, 1 - y)]
        todo = []
        for p, (_, rows) in enumerate(pieces):
            for j, chip in enumerate(chips):
                snd = _remote(_dev_rows(ins[p], (*chip, c), rows), _dev_rows(outs[p], (*chip, c), rows),
                              send_sems.at[j, p], recv_sems.at[j, p], sibling)
                if starting:
                    todo.append(snd.start)
                else:
                    theirs = _dev_rows(outs[p], (*chip, 1 - c), rows)
                    rcv = _remote(theirs, theirs, send_sems.at[j, p], recv_sems.at[j, p], me)
                    todo += [rcv.wait_recv, snd.wait_send]
        return todo

    dma = pltpu.SemaphoreType.DMA
    return _Carried(gathered, [SDS(g.shape, g.dtype) for g in gathered], {p: p for p in range(n)},
                    [dma((3, n)), dma((3, n))], build)


def _rs_chips_phase(sb):
    _, R, W = sb.shape

    def build(ins, outs, sems, stage):
        if stage == "mid":
            return []
        (sb_ref,), (land_ref,) = ins, outs
        send_sems, recv_sems = sems
        x, y, c = _place()
        chips = [(1 - x, y), (x, 1 - y), (1 - x, 1 - y)]
        cps = [_remote(sb_ref.at[2 * chip[0] + chip[1]], land_ref.at[j], send_sems.at[j], recv_sems.at[j], (*chip, c))
               for j, chip in enumerate(chips)]
        if stage == "start":
            return [cp.start for cp in cps]
        return [cp.wait_recv for cp in cps] + [cp.wait_send for cp in cps]

    dma = pltpu.SemaphoreType.DMA
    return _Carried([sb], [SDS((3, R, W), sb.dtype)], {}, [dma((3,)), dma((3,))], build)


def _allgather(slab, pieces, name):
    R, W = slab.shape
    n = len(pieces)
    assert sum(rows for _, rows in pieces) == R

    def body(slab_ref, *refs):
        outs = refs[:n]
        send_sems, recv_sems, local_sems = refs[n:]
        x, y, c = _place()
        me, sibling = (x, y, c), (x, y, 1 - c)
        chips = [(1 - x, y), (x, 1 - y), (1 - x, 1 - y)]

        def dst_rows(p, origin):
            rows = pieces[p][1]
            start = (4 * origin[0] + 2 * origin[1] + origin[2]) * rows
            return outs[p].at[pl.ds(start, rows), :]

        def copies(k, origin, to, from_slab):
            out = []
            for p, (off, rows) in enumerate(pieces):
                dst = dst_rows(p, origin)
                src = slab_ref.at[pl.ds(off, rows), :] if from_slab else dst
                out.append(pltpu.make_async_remote_copy(
                    src_ref=src, dst_ref=dst, send_sem=send_sems.at[k, p], recv_sem=recv_sems.at[k, p],
                    device_id=to, device_id_type=MESH))
            return out

        mine = [pltpu.make_async_copy(slab_ref.at[pl.ds(off, rows), :], dst_rows(p, me), local_sems.at[p])
                for p, (off, rows) in enumerate(pieces)]
        for cp in mine:
            cp.start()
        first = copies(0, me, sibling, True)
        for j, chip in enumerate(chips):
            first += copies(1 + j, me, (*chip, c), True)
        for cp in first:
            cp.start()
        passed = []
        for j, chip in enumerate(chips):
            for cp in copies(1 + j, (*chip, c), me, False):
                cp.wait_recv()
            fwd = copies(4 + j, (*chip, c), sibling, False)
            for cp in fwd:
                cp.start()
            passed += fwd
        for cp in copies(0, sibling, me, False):
            cp.wait_recv()
        for j, chip in enumerate(chips):
            for cp in copies(4 + j, (*chip, 1 - c), me, False):
                cp.wait_recv()
        for cp in first + passed:
            cp.wait_send()
        for cp in mine:
            cp.wait()

    return pl.pallas_call(
        body, name=name,
        in_specs=[HBM_SPEC], out_specs=[HBM_SPEC] * n,
        out_shape=[SDS((N_DEV * rows, W), slab.dtype) for _, rows in pieces],
        scratch_shapes=[pltpu.SemaphoreType.DMA((7, n)), pltpu.SemaphoreType.DMA((7, n)), pltpu.SemaphoreType.DMA((n,))],
    )(slab)


def _rs_sibling(grads, pieces, name):
    W = grads[0].shape[1]
    R = sum(rows for _, rows in pieces)
    n = len(pieces)
    dt = grads[0].dtype
    max_rows = max(rows for _, rows in pieces)
    steps = [(q, p) for q in range(N_CHIP) for p in range(n)]
    ns = len(steps)
    ADD_ROWS = 64
    SLOTS = 3
    assert all(rows % ADD_ROWS == 0 for _, rows in pieces)

    def body(*refs):
        g_refs = refs[:n]
        sb_ref, mine_buf, send_buf, land_buf, out_buf, in_sems, out_sems, send_sems, recv_sems, credit = refs[n:]
        x, y, c = _place()
        sibling = (x, y, 1 - c)

        def loads(s):
            q, p = steps[s]
            rows = pieces[p][1]
            slot = s % SLOTS
            mine = g_refs[p].at[pl.ds((2 * q + c) * rows, rows), :]
            theirs = g_refs[p].at[pl.ds((2 * q + 1 - c) * rows, rows), :]
            return (pltpu.make_async_copy(mine, mine_buf.at[slot, pl.ds(0, rows), :], in_sems.at[slot, 0]),
                    pltpu.make_async_copy(theirs, send_buf.at[slot, pl.ds(0, rows), :], in_sems.at[slot, 1]))

        def send(s):
            rows = pieces[steps[s][1]][1]
            slot = s % SLOTS
            return pltpu.make_async_remote_copy(
                src_ref=send_buf.at[slot, pl.ds(0, rows), :], dst_ref=land_buf.at[slot, pl.ds(0, rows), :],
                send_sem=send_sems.at[slot], recv_sem=recv_sems.at[slot], device_id=sibling, device_id_type=MESH)

        def store(s):
            q, p = steps[s]
            off, rows = pieces[p]
            slot = s % SLOTS
            return pltpu.make_async_copy(out_buf.at[slot, pl.ds(0, rows), :], sb_ref.at[q, pl.ds(off, rows), :],
                                         out_sems.at[slot])

        def start_send(s):
            for cp in loads(s):
                cp.wait()
            if s >= SLOTS:
                pl.semaphore_wait(credit.at[s % SLOTS], 1)
            send(s).start()

        for s in range(min(SLOTS, ns)):
            for cp in loads(s):
                cp.start()
        for s in range(min(SLOTS - 1, ns)):
            start_send(s)
        for s in range(ns):
            slot = s % SLOTS
            rows = pieces[steps[s][1]][1]
            if s + SLOTS - 1 < ns:
                start_send(s + SLOTS - 1)
            send(s).wait_recv()
            if s >= SLOTS:
                store(s - SLOTS).wait()

            def add(k, _, slot=slot):
                r = pl.ds(pl.multiple_of(k * ADD_ROWS, ADD_ROWS), ADD_ROWS)
                out_buf[slot, r, :] = (mine_buf[slot, r, :].astype(F32) + land_buf[slot, r, :].astype(F32)).astype(dt)
                return 0
            lax.fori_loop(0, rows // ADD_ROWS, add, 0)
            if s + SLOTS < ns:
                pl.semaphore_signal(credit.at[slot], inc=1, device_id=sibling, device_id_type=MESH)
            store(s).start()
            send(s).wait_send()
            if s + SLOTS < ns:
                for cp in loads(s + SLOTS):
                    cp.start()
        for s in range(max(ns - SLOTS, 0), ns):
            store(s).wait()

    buf = pltpu.VMEM((SLOTS, max_rows, W), dt)
    return pl.pallas_call(
        body, name=name,
        in_specs=[HBM_SPEC] * n, out_specs=HBM_SPEC,
        out_shape=SDS((N_CHIP, R, W), dt),
        scratch_shapes=[buf, buf, buf, buf, pltpu.SemaphoreType.DMA((SLOTS, 2)), pltpu.SemaphoreType.DMA((SLOTS,)),
                        pltpu.SemaphoreType.DMA((SLOTS,)), pltpu.SemaphoreType.DMA((SLOTS,)),
                        pltpu.SemaphoreType.REGULAR((SLOTS,))],
        compiler_params=pltpu.CompilerParams(vmem_limit_bytes=48 * MIB),
    )(*grads)


SMALL_NAMES = ("ffn1_norm", "mix_norm", "ffn2_norm", "final_norm", "lru_conv_w", "lru_conv_b", "lru_w_a", "lru_b_a",
               "lru_w_i", "lru_b_i", "lru_lambda", "sc_conv_w", "lru_out_norm", "sc_out_norm")
WEIGHT_NAMES = ("ffn1_norm", "ffn1_w_gate", "ffn1_w_up", "ffn1_w_down", "mix_norm", "w_in", "lru_conv_w", "lru_conv_b",
                "lru_w_a", "lru_b_a", "lru_w_i", "lru_b_i", "lru_lambda", "sc_conv_w", "lru_out_norm", "sc_out_norm",
                "w_out", "ffn2_norm", "ffn2_w_gate", "ffn2_w_up", "ffn2_w_down", "final_norm")
BIG = (("ffn1_w_gate", True), ("ffn1_w_up", True), ("ffn1_w_down", False), ("ffn2_w_gate", True), ("ffn2_w_up", True),
       ("ffn2_w_down", False), ("w_in", True), ("w_out", False))


SLAB_ROW_ALIGN = 256


def _pack_rows(parts, width):
    rows, counts = [], []
    for p in parts:
        flat = p.reshape(-1)
        nr = -(-flat.shape[0] // width)
        nr = -(-nr // SUBLANES) * SUBLANES
        rows.append(jnp.pad(flat, (0, nr * width - flat.shape[0])).reshape(nr, width))
        counts.append(nr)
    total = sum(counts)
    pad = -(-total // SLAB_ROW_ALIGN) * SLAB_ROW_ALIGN - total
    if pad:
        rows.append(jnp.zeros((pad, width), rows[0].dtype))
    return jnp.concatenate(rows, axis=0), counts


def _stack_rows(blocks):
    pieces, off = [], 0
    for b in blocks:
        pieces.append((off, b.shape[0]))
        off += b.shape[0]
    return jnp.concatenate(blocks, axis=0), pieces


def _unpack_rows(slab, counts, shapes):
    out, r = [], 0
    for nr, shape in zip(counts, shapes):
        size = math.prod(shape)
        out.append(slab[r:r + nr].reshape(-1)[:size].reshape(shape))
        r += nr
    return out


def kernel(x, ffn1_norm, ffn1_w_gate, ffn1_w_up, ffn1_w_down, mix_norm, w_in, lru_conv_w, lru_conv_b, lru_w_a, lru_b_a, lru_w_i, lru_b_i, lru_lambda, sc_conv_w, lru_out_norm, sc_out_norm, w_out, ffn2_norm, ffn2_w_gate, ffn2_w_up, ffn2_w_down, final_norm, loss_target, m_ffn1_norm, m_ffn1_w_gate, m_ffn1_w_up, m_ffn1_w_down, m_mix_norm, m_w_in, m_lru_conv_w, m_lru_conv_b, m_lru_w_a, m_lru_b_a, m_lru_w_i, m_lru_b_i, m_lru_lambda, m_sc_conv_w, m_lru_out_norm, m_sc_out_norm, m_w_out, m_ffn2_norm, m_ffn2_w_gate, m_ffn2_w_up, m_ffn2_w_down, m_final_norm, v_ffn1_norm, v_ffn1_w_gate, v_ffn1_w_up, v_ffn1_w_down, v_mix_norm, v_w_in, v_lru_conv_w, v_lru_conv_b, v_lru_w_a, v_lru_b_a, v_lru_w_i, v_lru_b_i, v_lru_lambda, v_sc_conv_w, v_lru_out_norm, v_sc_out_norm, v_w_out, v_ffn2_norm, v_ffn2_w_gate, v_ffn2_w_up, v_ffn2_w_down, v_final_norm):
    a = dict(locals())
    w = {n: a[n] for n in WEIGHT_NAMES}
    m = {n: a["m_" + n] for n in WEIGHT_NAMES}
    v = {n: a["v_" + n] for n in WEIGHT_NAMES}
    ax, ay, ac = _place()
    dev = 4 * ax + 2 * ay + ac
    chip = (2 * ax + ay).astype(jnp.int32).reshape(1)

    x0 = x[0]
    tgt = loss_target[0]
    T, D = x0.shape
    C = D // 2
    H, hd = lru_w_a.shape[1], lru_w_a.shape[2]
    CL = lru_conv_w.shape[2]

    shards = []
    for name, transposed in BIG:
        s = w[name][0]
        shards.append((s.T if transposed else s).astype(BF16))
    taps = jnp.concatenate([lru_conv_w[0], sc_conv_w[0], jnp.zeros((1, CL), F32)], axis=0)
    taps_row = lax.bitcast_convert_type(taps, BF16).reshape(1, -1)
    taps_blk = jnp.pad(taps_row, ((0, BF16_ROWS - 1), (0, D - taps_row.shape[1])))
    s_wg1, s_wu1, s_wd1, s_wg2, s_wu2, s_wd2, s_win, s_wout = shards
    slab_g1, pcs_g1 = _stack_rows([s_wg1])
    slab_u1, pcs_u1 = _stack_rows([s_wu1])
    slab_d1, pcs_d1 = _stack_rows([s_wd1])
    slab_mw, pcs_mw = _stack_rows([s_win, s_wout, taps_blk])
    slab_gu2, pcs_gu2 = _stack_rows([s_wg2, s_wu2])
    slab_d2, pcs_d2 = _stack_rows([s_wd2])
    (wg1,) = _allgather(slab_g1, pcs_g1, "allgather_ffn1_gate")

    g1, gm, g3 = ffn1_norm, mix_norm, ffn2_norm
    phase = _merge_phases(_ag_two_level_phase(slab_u1, pcs_u1), _ag_direct_phase(slab_d1, pcs_d1, SAME_CORE_AND_SIBLING))
    (n1, hg1), got = _norm_proj(x0, g1, [wg1], [BF16], False, "ffn1_gate", carried=phase)
    wu1, d1 = got[0], got[1:]
    phase = _merge_phases(_ag_forward_phase(d1, pcs_d1), _ag_direct_phase(slab_mw, pcs_mw, SAME_CORE_AND_SIBLING))
    (hu1, act1), got = _up_act(n1, wu1, hg1, "ffn1_up", carried=phase)
    wd1, mixw = got[0], got[1:]
    phase = _merge_phases(_ag_forward_phase(mixw, pcs_mw), _ag_direct_phase(slab_gu2, pcs_gu2, SAME_CORE_AND_SIBLING))
    x1, got = _mm_res(act1, wd1, x0, 0.5, "ffn1_down", carried=phase)
    (win, wout, taps_all), gu2 = got[:3], got[3:]
    phase = _merge_phases(_ag_forward_phase(gu2, pcs_gu2), _ag_direct_phase(slab_d2, pcs_d2, SAME_CORE_AND_SIBLING))
    (n2, z), got = _norm_proj(x1, gm, [win], [F32], False, "in_proj", carried=phase)
    (wg2, wu2), d2 = got[:2], got[2:]
    taps_all = taps_all.reshape(N_DEV, BF16_ROWS, D)[:, 0, :2 * SUBLANES * CL].reshape(N_DEV, SUBLANES, CL, 2)
    taps_all = lax.bitcast_convert_type(taps_all, F32)
    taps_all = taps_all.transpose(1, 0, 2).reshape(SUBLANES, N_DEV * CL)
    cw, sw = taps_all[0:4], taps_all[4:7]

    gf = final_norm.reshape(1, D)
    cb = lru_conv_b
    wa, wi = lru_w_a[0].astype(BF16), lru_w_i[0].astype(BF16)
    ba, bi = lru_b_a.reshape(1, C), lru_b_i.reshape(1, C)
    lam, glo, gso = lru_lambda, lru_out_norm, sc_out_norm

    y, h = _mix_fwd(z, cw, cb, wa, ba, wi, bi, lam, sw, glo, gso, "mix_fwd")
    x2, (wd2,) = _mm_res(y, wout, x1, 1.0, "out_proj", carried=_ag_forward_phase(d2, pcs_d2))
    n3, hg2, hu2, act2 = _norm_proj(x2, g3, [wg2, wu2], [BF16, BF16], True, "ffn2_up")
    x3 = _mm_res(act2, wd2, x2, 0.5, "ffn2_down")
    dx3, df2, d_gf, loss_blk = _loss_head(x3, gf, tgt, "loss_head")

    F = wd1.shape[0]
    bm_f = F // 4 if (F // 4) % LANES == 0 else 512

    def reduce_group(gs, tag):
        pcs, off = [], 0
        for g_ in gs:
            pcs.append((off, g_.shape[0] // N_DEV))
            off += g_.shape[0] // N_DEV
        sb_ = _rs_sibling(gs, pcs, "rs_sibling_add_" + tag)
        return sb_, pcs

    dhg2, dhu2 = _ffn_bwd_act(df2, wd2, hg2, hu2, "ffn2_bwd_act")
    d_wd2 = _dw_tn(act2, df2, bm_f, "ffn2_dw_down")
    d_wg2 = _dw_tn(dhg2, n3, bm_f, "ffn2_dw_gate")
    d_wu2 = _dw_tn(dhu2, n3, bm_f, "ffn2_dw_up")
    sb_f2, pcs_f2 = reduce_group([d_wg2, d_wu2, d_wd2], "ffn2")
    (dx2, dx2b, d_g3), (lb_f2,) = _mm_rmsbwd([(dhg2, wg2), (dhu2, wu2)], x2, g3, dx3, 1.0, "ffn2_bwd_in",
                                             carried=_rs_chips_phase(sb_f2))
    dy = _mm_nt(dx2b, wout, "out_proj_bwd")
    d_wout = _dw_tn(y, dx2b, 1024, "out_proj_dw")
    dz, small, d_wa, d_wi = _mix_bwd(z, h, dy, cw, cb, wa, ba, wi, bi, lam, sw, glo, gso, "mix_bwd")
    d_win = _dw_tn(dz, n2, 1280, "in_proj_dw")
    sb_mx, pcs_mx = reduce_group([d_win, d_wout], "mix")
    (dx1, df1, d_gm), (lb_mx,) = _mm_rmsbwd([(dz, win)], x1, gm, dx2, 0.5, "in_proj_bwd",
                                            carried=_rs_chips_phase(sb_mx))
    dhg1, dhu1 = _ffn_bwd_act(df1, wd1, hg1, hu1, "ffn1_bwd_act")
    d_wd1 = _dw_tn(act1, df1, bm_f, "ffn1_dw_down")
    d_wg1 = _dw_tn(dhg1, n1, bm_f, "ffn1_dw_gate")
    d_wu1 = _dw_tn(dhu1, n1, bm_f, "ffn1_dw_up")
    sb_f1, pcs_f1 = reduce_group([d_wg1, d_wu1, d_wd1], "ffn1")
    (dx0, _, d_g1), (lb_f1,) = _mm_rmsbwd([(dhg1, wg1), (dhu1, wu1)], x0, g1, dx1, 1.0, "ffn1_bwd_in",
                                          carried=_rs_chips_phase(sb_f1))

    big_sum = {}
    for tag, names, sb_, lb_, pcs in (("ffn2", ("ffn2_w_gate", "ffn2_w_up", "ffn2_w_down"), sb_f2, lb_f2, pcs_f2),
                                      ("mix", ("w_in", "w_out"), sb_mx, lb_mx, pcs_mx),
                                      ("ffn1", ("ffn1_w_gate", "ffn1_w_up", "ffn1_w_down"), sb_f1, lb_f1, pcs_f1)):
        gsum = _final_grad(sb_, lb_, chip, "rs_final_sum_" + tag)
        for name, (off, rows) in zip(names, pcs):
            big_sum[name] = gsum[off:off + rows]

    small_parts = [d_g1, d_gm, d_g3, d_gf, small[R_CW:R_CW + 4], small[R_CB], d_wa, small[R_BA], d_wi, small[R_BI],
                   small[R_LAM], small[R_SW:R_SW + 3], small[R_GLO], small[R_GSO]]
    sslab, counts = _pack_rows(small_parts, LANES)
    RS = sslab.shape[0]
    (sg,) = _allgather(sslab, [(0, RS)], "allgather_small_grads")
    ssum = _add_slabs([sg[j * RS:(j + 1) * RS] for j in range(N_DEV)], F32, "small_grads_sum")
    full_shapes = [(1, D), (1, D), (1, D), (D,), (1, 4, C), (1, C), (1, H, hd, hd), (1, H, hd), (1, H, hd, hd), (1, H, hd),
                   (1, C), (1, 3, C), (1, C), (1, C)]
    small_full = dict(zip(SMALL_NAMES, _unpack_rows(ssum, counts, full_shapes)))

    grads = {}
    for name, transposed in BIG:
        gblk = big_sum[name]
        grads[name] = (gblk.T if transposed else gblk)[None]
    for name in SMALL_NAMES:
        gfull = small_full[name]
        if name in ("lru_conv_w", "sc_conv_w"):
            gfull = lax.dynamic_slice_in_dim(gfull, dev * CL, CL, axis=2)
        grads[name] = gfull

    delta, new_m, new_v = {}, {}, {}
    for name, transposed in BIG:
        flip = transposed and w[name].shape[2] % LANES != 0
        view = (lambda t: t[0].T) if flip else (lambda t: t[0])
        back = (lambda t: t.T[None]) if flip else (lambda t: t[None])
        gview = big_sum[name] if flip else grads[name][0]
        d_, m_, v_ = _adamw(view(w[name]), gview, view(m[name]), view(v[name]), "adamw_" + name)
        delta[name], new_m[name], new_v[name] = back(d_), back(m_), back(v_)
    packs = [_pack_rows([t[n_] for n_ in SMALL_NAMES], LANES) for t in (w, grads, m, v)]
    sd, sm, sv = _adamw(packs[0][0], packs[1][0], packs[2][0], packs[3][0], "adamw_small")
    shapes = [w[n_].shape for n_ in SMALL_NAMES]
    for tgt_dict, slab_ in ((delta, sd), (new_m, sm), (new_v, sv)):
        for n_, val in zip(SMALL_NAMES, _unpack_rows(slab_, packs[0][1], shapes)):
            tgt_dict[n_] = val

    loss = lax.psum(loss_blk[0, 0], ("x", "y", "c"))
    return (loss, dx0[None], *[grads[n_] for n_ in WEIGHT_NAMES], *[delta[n_] for n_ in WEIGHT_NAMES],
            *[new_m[n_] for n_ in WEIGHT_NAMES], *[new_v[n_] for n_ in WEIGHT_NAMES])
```

```python
import functools
import math

import jax
import jax.numpy as jnp
from jax import lax
from jax.experimental import pallas as pl
from jax.experimental.pallas import tpu as pltpu

F32 = jnp.float32
BF16 = jnp.bfloat16
SDS = jax.ShapeDtypeStruct
MESH = pl.DeviceIdType.MESH

NORM_EPS = 1e-6
LRU_C = 8.0
N_DEV = 8
N_CHIP = 4
ADAM_LR, ADAM_B1, ADAM_B2, ADAM_EPS, ADAM_WD, ADAM_STEP = 0.001, 0.9, 0.999, 1e-08, 0.01, 10

NN = (((1,), (0,)), ((), ()))
NT = (((1,), (1,)), ((), ()))
TN = (((0,), (0,)), ((), ()))

SUBLANES = 8
BF16_ROWS = 16
LANES = 128
MIB = 1 << 20


def _dot(a, b, dims):
    return lax.dot_general(a, b, dims, preferred_element_type=F32)


def _blk(n, pref, align):
    if n <= pref:
        return n
    b = (pref // align) * align
    while b >= align:
        if n % b == 0:
            return b
        b -= align
    raise ValueError(f"no block of {n} aligned to {align} under {pref}")


def _cp(sem, vmem_mib):
    return pltpu.CompilerParams(dimension_semantics=sem, vmem_limit_bytes=vmem_mib * MIB)


HBM_SPEC = pl.BlockSpec(memory_space=pltpu.HBM)
MID_EIGHTHS = 5

class _Carried:
    def __init__(self, inputs, out_shapes, aliases, sem_shapes, build, has_mid=False):
        self.inputs, self.out_shapes, self.aliases = list(inputs), list(out_shapes), dict(aliases)
        self.sem_shapes, self.build, self.has_mid = list(sem_shapes), build, has_mid


def _call(body, *, name, grid, in_specs, out_specs, out_shape, scratch_shapes, compiler_params, args, carried=None):
    if carried is None:
        return pl.pallas_call(body, name=name, grid=grid, in_specs=in_specs, out_specs=out_specs, out_shape=out_shape,
                              scratch_shapes=scratch_shapes, compiler_params=compiler_params)(*args)
    n_in, n_out, n_sc = len(in_specs), len(out_shape), len(scratch_shapes)
    c_in, c_out = len(carried.inputs), len(carried.out_shapes)

    def hosted(*refs):
        ins, refs = refs[:n_in], refs[n_in:]
        c_ins, refs = refs[:c_in], refs[c_in:]
        outs, refs = refs[:n_out], refs[n_out:]
        c_outs, refs = refs[:c_out], refs[c_out:]
        scratch, c_sems = refs[:n_sc], refs[n_sc:]
        first = functools.reduce(jnp.logical_and, [pl.program_id(a) == 0 for a in range(len(grid))])
        last = functools.reduce(jnp.logical_and, [pl.program_id(a) == g - 1 for a, g in enumerate(grid)])

        @pl.when(first)
        def _():
            for start in carried.build(c_ins, c_outs, c_sems, "start"):
                start()

        if carried.has_mid:
            mid = functools.reduce(jnp.logical_and, [pl.program_id(0) == (grid[0] * MID_EIGHTHS) // 8]
                                   + [pl.program_id(a) == 0 for a in range(1, len(grid))])

            @pl.when(mid)
            def _():
                for step in carried.build(c_ins, c_outs, c_sems, "mid"):
                    step()

        body(*ins, *outs, *scratch)

        @pl.when(last)
        def _():
            for wait in carried.build(c_ins, c_outs, c_sems, "end"):
                wait()

    out = pl.pallas_call(
        hosted, name=name, grid=grid, in_specs=list(in_specs) + [HBM_SPEC] * c_in,
        out_specs=list(out_specs) + [HBM_SPEC] * c_out, out_shape=list(out_shape) + carried.out_shapes,
        scratch_shapes=list(scratch_shapes) + carried.sem_shapes,
        input_output_aliases={n_in + a: n_out + b for a, b in carried.aliases.items()},
        compiler_params=compiler_params)(*args, *carried.inputs)
    return out[:n_out], out[n_out:]


ROW_CHUNK = 128


def _chunk_rows(c):
    return pl.ds(pl.multiple_of(c * ROW_CHUNK, ROW_CHUNK), ROW_CHUNK)


def _rstd(xv):
    return lax.rsqrt(jnp.mean(xv * xv, axis=-1, keepdims=True) + NORM_EPS)


def _rms_bwd(xv, g, dn):
    r = _rstd(xv)
    xr = xv * r
    gd = g * dn
    dx = r * (gd - xr * jnp.mean(gd * xr, axis=-1, keepdims=True))
    return dx, jnp.sum(dn * xr, axis=0, keepdims=True)


def _log1p(e):
    u = 1.0 + e
    return jnp.where(u == 1.0, e, jnp.log(u) * (e / (u - 1.0)))


def _one_minus_exp(v, exp_half_v):
    series = 1.0 / 5040.0
    for coeff in (1.0 / 720.0, 1.0 / 120.0, 1.0 / 24.0, 1.0 / 6.0, 0.5, 1.0):
        series = series * v + coeff
    return jnp.where(v > -0.5, -v * series, 1.0 - exp_half_v * exp_half_v)


def _sigmoid(v):
    return 0.5 * jnp.tanh(0.5 * v) + 0.5


def _gelu_parts(g):
    k0 = math.sqrt(2.0 / math.pi)
    g2 = g * g
    t = jnp.tanh(k0 * (g + 0.044715 * g * g2))
    gel = 0.5 * g * (1.0 + t)
    gelp = 0.5 * (1.0 + t) + 0.5 * g * (1.0 - t * t) * (k0 * (1.0 + 3.0 * 0.044715 * g2))
    return gel, gelp


def _norm_proj(x, gain, w_list, out_dtypes, swiglu, name, carried=None):
    T, D = x.shape
    N = w_list[0].shape[0]
    nw = len(w_list)
    bm = _blk(T, 1024, BF16_ROWS)
    bn = _blk(N, 512, LANES)

    def body(*refs):
        x_ref, g_ref = refs[:2]
        w_refs = refs[2:2 + nw]
        n_ref = refs[2 + nw]
        o_refs = refs[3 + nw:3 + 2 * nw]
        act_ref = refs[3 + 2 * nw] if swiglu else None
        n_sc = refs[-1]

        @pl.when(pl.program_id(1) == 0)
        def _():
            def chunk(c, _):
                r = _chunk_rows(c)
                xv = x_ref[r, :]
                nb = (xv * _rstd(xv) * g_ref[...]).astype(BF16)
                n_sc[r, :] = nb
                n_ref[r, :] = nb
                return 0
            lax.fori_loop(0, bm // ROW_CHUNK, chunk, 0)

        n = n_sc[...]
        outs = [_dot(n, w_ref[...], NT) for w_ref in w_refs]
        for o_ref, o in zip(o_refs, outs):
            o_ref[...] = o.astype(o_ref.dtype)
        if swiglu:
            hg, hu = outs
            act_ref[...] = (hg * _sigmoid(hg) * hu).astype(BF16)

    row = pl.BlockSpec((bm, D), lambda i, j: (i, 0))
    tile = pl.BlockSpec((bm, bn), lambda i, j: (i, j))
    n_extra = 1 if swiglu else 0
    return _call(
        body, name=name, grid=(T // bm, N // bn),
        in_specs=[row, pl.BlockSpec((1, D), lambda i, j: (0, 0))] + [pl.BlockSpec((bn, D), lambda i, j: (j, 0))] * nw,
        out_specs=[row] + [tile] * (nw + n_extra),
        out_shape=[SDS((T, D), BF16)] + [SDS((T, N), dt) for dt in out_dtypes] + [SDS((T, N), BF16)] * n_extra,
        scratch_shapes=[pltpu.VMEM((bm, D), BF16)],
        compiler_params=_cp(("arbitrary", "arbitrary"), 52),
        args=(x, gain, *w_list), carried=carried)


def _up_act(n, wu, hg, name, carried=None):
    T, D = n.shape
    F = wu.shape[0]
    bm = _blk(T, 1024, BF16_ROWS)
    bn = _blk(F, 512, LANES)

    def body(n_ref, wu_ref, hg_ref, hu_ref, act_ref):
        hu = _dot(n_ref[...], wu_ref[...], NT)
        hg = hg_ref[...].astype(F32)
        hu_ref[...] = hu.astype(BF16)
        act_ref[...] = (hg * _sigmoid(hg) * hu).astype(BF16)

    tile = pl.BlockSpec((bm, bn), lambda i, j: (i, j))
    return _call(
        body, name=name, grid=(T // bm, F // bn),
        in_specs=[pl.BlockSpec((bm, D), lambda i, j: (i, 0)), pl.BlockSpec((bn, D), lambda i, j: (j, 0)), tile],
        out_specs=[tile, tile], out_shape=[SDS((T, F), BF16)] * 2, scratch_shapes=[],
        compiler_params=_cp(("arbitrary", "arbitrary"), 40),
        args=(n, wu, hg), carried=carried)


def _mm_res(a, b, x, scale, name, carried=None):
    T, K = a.shape
    D = b.shape[1]
    bm = _blk(T, 1024, BF16_ROWS)
    bk = _blk(K, 1408, LANES)
    nk = K // bk

    def body(a_ref, b_ref, x_ref, o_ref):
        k = pl.program_id(1)

        @pl.when(k == 0)
        def _():
            o_ref[...] = jnp.zeros_like(o_ref)

        o_ref[...] += _dot(a_ref[...], b_ref[...], NN)

        @pl.when(k == nk - 1)
        def _():
            def chunk(c, _):
                r = _chunk_rows(c)
                o_ref[r, :] = x_ref[r, :] + scale * o_ref[r, :]
                return 0
            lax.fori_loop(0, bm // ROW_CHUNK, chunk, 0)

    row = pl.BlockSpec((bm, D), lambda i, k: (i, 0))
    out = _call(
        body, name=name, grid=(T // bm, nk),
        in_specs=[pl.BlockSpec((bm, bk), lambda i, k: (i, k)), pl.BlockSpec((bk, D), lambda i, k: (k, 0)), row],
        out_specs=[row], out_shape=[SDS((T, D), F32)], scratch_shapes=[],
        compiler_params=_cp(("arbitrary", "arbitrary"), 56),
        args=(a, b, x), carried=carried)
    return out[0] if carried is None else (out[0][0], out[1])


def _mm_nt(a, b, name):
    T, K = a.shape
    N = b.shape[0]
    bm = _blk(T, 1024, BF16_ROWS)
    bn = _blk(N, 512, LANES)

    def body(a_ref, b_ref, o_ref):
        o_ref[...] = _dot(a_ref[...], b_ref[...], NT)

    return pl.pallas_call(
        body, name=name, grid=(T // bm, N // bn),
        in_specs=[pl.BlockSpec((bm, K), lambda i, j: (i, 0)), pl.BlockSpec((bn, K), lambda i, j: (j, 0))],
        out_specs=pl.BlockSpec((bm, bn), lambda i, j: (i, j)), out_shape=SDS((T, N), F32),
        compiler_params=_cp(("arbitrary", "arbitrary"), 40),
    )(a, b)


def _ffn_bwd_act(dfb, wd, hg, hu, name):
    T, D = dfb.shape
    F = wd.shape[0]
    bm = _blk(T, 1024, BF16_ROWS)
    bn = _blk(F, 512, LANES)

    def body(df_ref, wd_ref, hg_ref, hu_ref, dhg_ref, dhu_ref):
        dact = _dot(df_ref[...], wd_ref[...], NT)
        hgv = hg_ref[...].astype(F32)
        huv = hu_ref[...].astype(F32)
        s = _sigmoid(hgv)
        dhu_ref[...] = (dact * (hgv * s)).astype(BF16)
        dhg_ref[...] = (dact * huv * (s * (1.0 + hgv * (1.0 - s)))).astype(BF16)

    tile = pl.BlockSpec((bm, bn), lambda i, j: (i, j))
    return pl.pallas_call(
        body, name=name, grid=(T // bm, F // bn),
        in_specs=[pl.BlockSpec((bm, D), lambda i, j: (i, 0)), pl.BlockSpec((bn, D), lambda i, j: (j, 0)), tile, tile],
        out_specs=[tile, tile], out_shape=[SDS((T, F), BF16)] * 2,
        compiler_params=_cp(("arbitrary", "arbitrary"), 40),
    )(dfb, wd, hg, hu)


def _dw_tn(a, b, bm_pref, name):
    T, M = a.shape
    N = b.shape[1]
    bm = _blk(M, bm_pref, LANES)
    tk = _blk(T, 1024, BF16_ROWS)
    nk = T // tk

    def body(a_ref, b_ref, o_ref, acc):
        k = pl.program_id(1)

        @pl.when(k == 0)
        def _():
            acc[...] = jnp.zeros_like(acc)

        acc[...] += _dot(a_ref[...], b_ref[...], TN)

        @pl.when(k == nk - 1)
        def _():
            o_ref[...] = acc[...].astype(BF16)

    return pl.pallas_call(
        body, name=name, grid=(M // bm, nk),
        in_specs=[pl.BlockSpec((tk, bm), lambda i, k: (k, i)), pl.BlockSpec((tk, N), lambda i, k: (k, 0))],
        out_specs=pl.BlockSpec((bm, N), lambda i, k: (i, 0)), out_shape=SDS((M, N), BF16),
        scratch_shapes=[pltpu.VMEM((bm, N), F32)],
        compiler_params=_cp(("arbitrary", "arbitrary"), 48),
    )(a, b)


def _mm_rmsbwd(pairs, x, gain, dx_in, bscale, name, carried=None):
    T, D = x.shape
    K = pairs[0][0].shape[1]
    npair = len(pairs)
    bm = _blk(T, 1024, BF16_ROWS)
    bk = _blk(K, 1024 // npair, LANES)
    nk = K // bk

    nchunk = bm // ROW_CHUNK

    def body(*refs):
        ab = refs[:2 * npair]
        x_hbm, g_ref, dxin_hbm, dx_ref, dxb_ref, dg_ref, x_buf, dxin_buf, sems = refs[2 * npair:]
        i = pl.program_id(0)
        k = pl.program_id(1)

        def fetch(c, slot):
            rows = pl.ds(i * bm + c * ROW_CHUNK, ROW_CHUNK)
            return (pltpu.make_async_copy(x_hbm.at[rows, :], x_buf.at[slot], sems.at[slot, 0]),
                    pltpu.make_async_copy(dxin_hbm.at[rows, :], dxin_buf.at[slot], sems.at[slot, 1]))

        @pl.when(k == 0)
        def _():
            dx_ref[...] = jnp.zeros_like(dx_ref)

        @pl.when(k == nk - 1)
        def _():
            for cp in fetch(0, 0):
                cp.start()

        for q in range(npair):
            dx_ref[...] += _dot(ab[2 * q][...], ab[2 * q + 1][...], NN)

        @pl.when(k == nk - 1)
        def _():
            @pl.when(i == 0)
            def _():
                dg_ref[...] = jnp.zeros_like(dg_ref)

            def chunk(c, _):
                slot = c % 2

                @pl.when(c + 1 < nchunk)
                def _():
                    for cp in fetch(c + 1, 1 - slot):
                        cp.start()

                for cp in fetch(c, slot):
                    cp.wait()

                r = _chunk_rows(c)
                dx, dg = _rms_bwd(x_buf[slot], g_ref[...], dx_ref[r, :])
                dxo = dxin_buf[slot] + dx
                dx_ref[r, :] = dxo
                dxb_ref[r, :] = (bscale * dxo).astype(BF16)
                dg_ref[...] += dg
                return 0
            lax.fori_loop(0, nchunk, chunk, 0)

    row = pl.BlockSpec((bm, D), lambda i, k: (i, 0))
    anywhere = pl.BlockSpec(memory_space=pl.ANY)
    vec = pl.BlockSpec((1, D), lambda i, k: (0, 0))
    in_specs = []
    args = []
    for a, b in pairs:
        in_specs += [pl.BlockSpec((bm, bk), lambda i, k: (i, k)), pl.BlockSpec((bk, D), lambda i, k: (k, 0))]
        args += [a, b]
    return _call(
        body, name=name, grid=(T // bm, nk),
        in_specs=in_specs + [anywhere, vec, anywhere], out_specs=[row, row, vec],
        out_shape=[SDS((T, D), F32), SDS((T, D), BF16), SDS((1, D), F32)],
        scratch_shapes=[pltpu.VMEM((2, ROW_CHUNK, D), F32), pltpu.VMEM((2, ROW_CHUNK, D), F32),
                        pltpu.SemaphoreType.DMA((2, 2))],
        compiler_params=_cp(("arbitrary", "arbitrary"), 52),
        args=(*args, x, gain, dx_in), carried=carried)


def _loss_head(x3, gain, tgt, name):
    T, D = x3.shape
    bm = _blk(T, 256, BF16_ROWS)

    def body(x_ref, g_ref, t_ref, dx_ref, dxb_ref, dg_ref, loss_ref):
        i = pl.program_id(0)
        xv = x_ref[...]
        g = g_ref[...]
        out = xv * _rstd(xv) * g
        e = out - t_ref[...]
        part = 0.5 * jnp.sum(jnp.mean(e * e, axis=-1, keepdims=True), axis=0, keepdims=True)
        dx, dg = _rms_bwd(xv, g, e * (1.0 / D))
        dx_ref[...] = dx
        dxb_ref[...] = (0.5 * dx).astype(BF16)

        @pl.when(i == 0)
        def _():
            dg_ref[...] = dg
            loss_ref[...] = jnp.broadcast_to(part, loss_ref.shape)

        @pl.when(i > 0)
        def _():
            dg_ref[...] += dg
            loss_ref[...] += jnp.broadcast_to(part, loss_ref.shape)

    row = pl.BlockSpec((bm, D), lambda i: (i, 0))
    vec = pl.BlockSpec((1, D), lambda i: (0, 0))
    return pl.pallas_call(
        body, name=name, grid=(T // bm,),
        in_specs=[row, vec, row], out_specs=[row, row, vec, pl.BlockSpec((SUBLANES, LANES), lambda i: (0, 0))],
        out_shape=[SDS((T, D), F32), SDS((T, D), BF16), SDS((1, D), F32), SDS((SUBLANES, LANES), F32)],
        compiler_params=_cp(("arbitrary",), 40),
    )(x3, gain, tgt)


R_CW, R_CB, R_BA, R_BI, R_LAM, R_SW, R_GLO, R_GSO, SMALL_ROWS = 0, 4, 5, 6, 7, 8, 11, 12, 16


def _rows(g):
    return pl.ds(pl.multiple_of(g * SUBLANES, SUBLANES), SUBLANES)


def _shift_back(prev, cur, d):
    row = lax.broadcasted_iota(jnp.int32, cur.shape, 0)
    return pltpu.roll(jnp.where(row >= SUBLANES - d, prev, cur), d, 0)


def _shift_fwd(cur, nxt, d):
    row = lax.broadcasted_iota(jnp.int32, cur.shape, 0)
    return pltpu.roll(jnp.where(row < d, nxt, cur), SUBLANES - d, 0)


def _causal_conv(ext, g, taps_ref, ntap):
    prev = ext[_rows(g), :]
    cur = ext[_rows(g + 1), :]
    out = _shift_back(prev, cur, ntap - 1) * taps_ref[0:1, :]
    for k in range(1, ntap - 1):
        out = out + _shift_back(prev, cur, ntap - 1 - k) * taps_ref[k:k + 1, :]
    return out + cur * taps_ref[ntap - 1:ntap, :]


def _scan8(A, U, reverse):
    row = lax.broadcasted_iota(jnp.int32, A.shape, 0)
    for s in (1, 2, 4):
        if reverse:
            A_sh = pltpu.roll(A, SUBLANES - s, 0)
            U_sh = pltpu.roll(U, SUBLANES - s, 0)
            m = row < SUBLANES - s
        else:
            A_sh = pltpu.roll(A, s, 0)
            U_sh = pltpu.roll(U, s, 0)
            m = row >= s
        U = jnp.where(m, A * U_sh + U, U)
        A = jnp.where(m, A * A_sh, A)
    return A, U


def _gate_pre(xc_s, w_ref, out_s, H, hd):
    for h in range(H):
        cs = slice(h * hd, (h + 1) * hd)
        out_s[:, cs] = _dot(xc_s[:, cs].astype(BF16), w_ref[h], NN)


def _lru_coeffs(pa, pi, xc, ba, bi, sp):
    ra = _sigmoid(pa + ba)
    ri = _sigmoid(pi + bi)
    log_a = (-LRU_C * ra) * sp
    a = jnp.exp(log_a)
    mult = jnp.sqrt(_one_minus_exp(2.0 * log_a, a))
    return ra, ri, a, mult


def _softplus_neg(lam):
    v = -lam
    return jnp.maximum(v, 0.0) + _log1p(jnp.exp(-jnp.abs(v)))


def _mix_fwd(z, cw, cb, wa, ba, wi, bi, lam, sw, glo, gso, name):
    T = z.shape[0]
    C = z.shape[1] // 5
    H = wa.shape[0]
    hd = C // H
    tb = _blk(T, 256, BF16_ROWS)
    ng = tb // SUBLANES
    HDR = SUBLANES

    def body(z_ref, cw_ref, cb_ref, wa_ref, ba_ref, wi_ref, bi_ref, lam_ref, sw_ref, glo_ref, gso_ref,
             y_ref, h_ref, xext, pext, xc_s, pa_s, pi_s, y_s, hcar):
        @pl.when(pl.program_id(0) == 0)
        def _():
            xext[0:HDR, :] = jnp.zeros((HDR, C), F32)
            pext[0:HDR, :] = jnp.zeros((HDR, C), F32)
            hcar[...] = jnp.zeros_like(hcar)

        def fill(g, _):
            r = _rows(g)
            re = _rows(g + 1)
            xext[re, :] = z_ref[r, 0:C]
            pext[re, :] = z_ref[r, 3 * C:4 * C] * z_ref[r, 4 * C:5 * C]
            return 0
        lax.fori_loop(0, ng, fill, 0)

        def conv(g, _):
            xc_s[_rows(g), :] = _causal_conv(xext, g, cw_ref, 4) + cb_ref[...]
            return 0
        lax.fori_loop(0, ng, conv, 0)

        _gate_pre(xc_s, wa_ref, pa_s, H, hd)
        _gate_pre(xc_s, wi_ref, pi_s, H, hd)
        sp = _softplus_neg(lam_ref[...])

        def group(g, hprev):
            r = _rows(g)
            xc = xc_s[r, :]
            _, ri, a, mult = _lru_coeffs(pa_s[r, :], pi_s[r, :], xc, ba_ref[...], bi_ref[...], sp)
            A, U = _scan8(a, mult * (ri * xc), reverse=False)
            hh = A * hprev + U
            h_ref[r, :] = hh
            gel, _ = _gelu_parts(z_ref[r, C:2 * C])
            y_lru = hh * gel
            y_s[r, 0:C] = y_lru * _rstd(y_lru) * glo_ref[...]
            y_sc = z_ref[r, 2 * C:3 * C] * _causal_conv(pext, g, sw_ref, 3)
            y_s[r, C:2 * C] = y_sc * _rstd(y_sc) * gso_ref[...]
            return jnp.broadcast_to(hh[SUBLANES - 1:SUBLANES, :], hh.shape)
        hcar[...] = lax.fori_loop(0, ng // 2, lambda t, hp: group(2 * t + 1, group(2 * t, hp)), hcar[...])

        xext[0:HDR, :] = xext[tb:tb + HDR, :]
        pext[0:HDR, :] = pext[tb:tb + HDR, :]

        def cast(g, _):
            r = pl.ds(pl.multiple_of(g * BF16_ROWS, BF16_ROWS), BF16_ROWS)
            y_ref[r, :] = y_s[r, :].astype(BF16)
            return 0
        lax.fori_loop(0, tb // BF16_ROWS, cast, 0)

    full = lambda shape: pl.BlockSpec(shape, lambda i: (0,) * len(shape))
    blk = lambda w: pl.BlockSpec((tb, w), lambda i: (i, 0))
    ext = pltpu.VMEM((tb + HDR, C), F32)
    tile = pltpu.VMEM((tb, C), F32)
    return pl.pallas_call(
        body, name=name, grid=(T // tb,),
        in_specs=[blk(5 * C), full((4, C)), full((1, C)), full((H, hd, hd)), full((1, C)), full((H, hd, hd)),
                  full((1, C)), full((1, C)), full((3, C)), full((1, C)), full((1, C))],
        out_specs=[blk(2 * C), blk(C)],
        out_shape=[SDS((T, 2 * C), BF16), SDS((T, C), F32)],
        scratch_shapes=[ext, ext, tile, tile, tile, pltpu.VMEM((tb, 2 * C), F32), pltpu.VMEM((SUBLANES, C), F32)],
        compiler_params=_cp(("arbitrary",), 40),
    )(z, cw, cb, wa, ba, wi, bi, lam, sw, glo, gso)


def _mix_bwd(z, h, dy, cw, cb, wa, ba, wi, bi, lam, sw, glo, gso, name):
    T = z.shape[0]
    C = z.shape[1] // 5
    H = wa.shape[0]
    hd = C // H
    tb = _blk(T, 256, BF16_ROWS)
    nb = T // tb
    ng = tb // SUBLANES
    HDR = SUBLANES
    N_ACC = 13

    def body(z_ref, zp_ref, h_ref, hp_ref, dy_ref, cw_ref, cb_ref, wa_ref, ba_ref, wi_ref, bi_ref, lam_ref,
             sw_ref, glo_ref, gso_ref, dz_ref, small_ref, dwa_ref, dwi_ref,
             xext, pext, hext, dqext, dxcext, bext, xc_s, pa_s, pi_s, a_s, m_s, ri_s, dh_s, dpa_s, dpi_s,
             dz_s, acc_s, bcar):
        i = pl.program_id(0)
        first_rows = i == nb - 1

        @pl.when(i == 0)
        def _():
            dqext[tb:tb + HDR, :] = jnp.zeros((HDR, C), F32)
            dxcext[tb:tb + HDR, :] = jnp.zeros((HDR, C), F32)
            bcar[...] = jnp.zeros_like(bcar)
            acc_s[...] = jnp.zeros_like(acc_s)
            dwa_ref[...] = jnp.zeros_like(dwa_ref)
            dwi_ref[...] = jnp.zeros_like(dwi_ref)

        zero = jnp.zeros((HDR, C), F32)
        xext[0:HDR, :] = jnp.where(first_rows, zero, zp_ref[:, 0:C])
        pext[0:HDR, :] = jnp.where(first_rows, zero, zp_ref[:, 3 * C:4 * C] * zp_ref[:, 4 * C:5 * C])
        hext[0:HDR, :] = jnp.where(first_rows, zero, hp_ref[...])

        def fill(g, _):
            r = _rows(g)
            re = _rows(g + 1)
            xext[re, :] = z_ref[r, 0:C]
            pext[re, :] = z_ref[r, 3 * C:4 * C] * z_ref[r, 4 * C:5 * C]
            hext[re, :] = h_ref[r, :]
            return 0
        lax.fori_loop(0, ng, fill, 0)

        def conv(g, _):
            xc_s[_rows(g), :] = _causal_conv(xext, g, cw_ref, 4) + cb_ref[...]
            return 0
        lax.fori_loop(0, ng, conv, 0)

        _gate_pre(xc_s, wa_ref, pa_s, H, hd)
        _gate_pre(xc_s, wi_ref, pi_s, H, hd)
        sp = _softplus_neg(lam_ref[...])
        dsp_dlam = -jax.nn.sigmoid(-lam_ref[...])

        def add_acc(k, v):
            acc_s[k] += v

        def p1(g, _):
            r = _rows(g)
            xc = xc_s[r, :]
            _, ri, a, mult = _lru_coeffs(pa_s[r, :], pi_s[r, :], xc, ba_ref[...], bi_ref[...], sp)
            a_s[r, :] = a
            m_s[r, :] = mult
            ri_s[r, :] = ri
            hh = h_ref[r, :]
            gel, gelp = _gelu_parts(z_ref[r, C:2 * C])
            y_lru = hh * gel
            dnl = dy_ref[r, 0:C]
            rl = _rstd(y_lru)
            ylr = y_lru * rl
            gd = glo_ref[...] * dnl
            dy_lru = rl * (gd - ylr * jnp.mean(gd * ylr, axis=-1, keepdims=True))
            add_acc(R_GLO, dnl * ylr)
            dz_s[r, C:2 * C] = dy_lru * hh * gelp
            dh = dy_lru * gel
            dh_s[r, :] = dh

            q = _causal_conv(pext, g, sw_ref, 3)
            scb = z_ref[r, 2 * C:3 * C]
            y_sc = scb * q
            dns = dy_ref[r, C:2 * C]
            rs = _rstd(y_sc)
            ysr = y_sc * rs
            gs = gso_ref[...] * dns
            dy_sc = rs * (gs - ysr * jnp.mean(gs * ysr, axis=-1, keepdims=True))
            add_acc(R_GSO, dns * ysr)
            dz_s[r, 2 * C:3 * C] = dy_sc * q
            dqext[r, :] = dy_sc * scb
            return 0
        lax.fori_loop(0, ng, p1, 0, unroll=2)

        bext[tb:tb + HDR, :] = bcar[...]

        def p2(j, carry):
            g = ng - 1 - j
            r = _rows(g)
            a = a_s[r, :]
            A, U = _scan8(a, a * dh_s[r, :], reverse=True)
            bb = A * carry + U
            bext[r, :] = bb
            return jnp.broadcast_to(bb[0:1, :], bb.shape)
        bcar[...] = lax.fori_loop(0, ng, p2, bcar[...])

        def p3(g, _):
            r = _rows(g)
            rn = _rows(g + 1)
            G = dh_s[r, :] + _shift_fwd(bext[r, :], bext[rn, :], 1)
            hm1 = _shift_back(hext[r, :], hext[rn, :], 1)
            a = a_s[r, :]
            mult = m_s[r, :]
            ri = ri_s[r, :]
            xc = xc_s[r, :]
            ra = _sigmoid(pa_s[r, :] + ba_ref[...])
            dxcext[r, :] = G * mult * ri
            dri = G * mult * xc
            dmult = G * ri * xc
            dlog_a = (G * hm1) * a - dmult * (a * a) / mult
            add_acc(R_LAM, dlog_a * (-LRU_C * ra) * dsp_dlam)
            dpa = dlog_a * (-LRU_C * sp) * ra * (1.0 - ra)
            dpi = dri * ri * (1.0 - ri)
            add_acc(R_BA, dpa)
            add_acc(R_BI, dpi)
            dpa_s[r, :] = dpa
            dpi_s[r, :] = dpi
            return 0
        lax.fori_loop(0, ng, p3, 0)

        for hh_ in range(H):
            cs = slice(hh_ * hd, (hh_ + 1) * hd)
            dpa_b = dpa_s[:, cs].astype(BF16)
            dpi_b = dpi_s[:, cs].astype(BF16)
            xc_b = xc_s[:, cs].astype(BF16)
            dxcext[0:tb, cs] += _dot(dpa_b, wa_ref[hh_], NT) + _dot(dpi_b, wi_ref[hh_], NT)
            dwa_ref[hh_] += _dot(xc_b, dpa_b, TN)
            dwi_ref[hh_] += _dot(xc_b, dpi_b, TN)

        def p4(g, _):
            r = _rows(g)
            rn = _rows(g + 1)
            dxc = dxcext[r, :]
            dxc_n = dxcext[rn, :]
            x_p = xext[r, :]
            x_c = xext[rn, :]
            add_acc(R_CB, dxc)
            dlx = dxc * cw_ref[3:4, :]
            add_acc(R_CW + 3, dxc * x_c)
            for d in range(1, 4):
                dlx = dlx + _shift_fwd(dxc, dxc_n, d) * cw_ref[3 - d:4 - d, :]
                add_acc(R_CW + 3 - d, dxc * _shift_back(x_p, x_c, d))
            dz_s[r, 0:C] = dlx

            dq = dqext[r, :]
            dq_n = dqext[rn, :]
            p_p = pext[r, :]
            p_c = pext[rn, :]
            dp = dq * sw_ref[2:3, :]
            add_acc(R_SW + 2, dq * p_c)
            for d in range(1, 3):
                dp = dp + _shift_fwd(dq, dq_n, d) * sw_ref[2 - d:3 - d, :]
                add_acc(R_SW + 2 - d, dq * _shift_back(p_p, p_c, d))
            dz_s[r, 3 * C:4 * C] = dp * z_ref[r, 4 * C:5 * C]
            dz_s[r, 4 * C:5 * C] = dp * z_ref[r, 3 * C:4 * C]
            return 0
        lax.fori_loop(0, ng, p4, 0)

        dqext[tb:tb + HDR, :] = dqext[0:HDR, :]
        dxcext[tb:tb + HDR, :] = dxcext[0:HDR, :]

        def cast(g, _):
            r = pl.ds(pl.multiple_of(g * BF16_ROWS, BF16_ROWS), BF16_ROWS)
            dz_ref[r, :] = dz_s[r, :].astype(BF16)
            return 0
        lax.fori_loop(0, tb // BF16_ROWS, cast, 0)

        @pl.when(i == nb - 1)
        def _():
            small_ref[...] = jnp.zeros_like(small_ref)
            for k in range(N_ACC):
                small_ref[k:k + 1, :] = jnp.sum(acc_s[k], axis=0, keepdims=True)

    tpg = tb // SUBLANES
    full = lambda shape: pl.BlockSpec(shape, lambda i: (0,) * len(shape))
    blk = lambda w: pl.BlockSpec((tb, w), lambda i: (nb - 1 - i, 0))
    prev = lambda w: pl.BlockSpec((SUBLANES, w), lambda i: (jnp.maximum((nb - 1 - i) * tpg - 1, 0), 0))
    ext = pltpu.VMEM((tb + HDR, C), F32)
    tile = pltpu.VMEM((tb, C), F32)
    return pl.pallas_call(
        body, name=name, grid=(nb,),
        in_specs=[blk(5 * C), prev(5 * C), blk(C), prev(C), blk(2 * C), full((4, C)), full((1, C)), full((H, hd, hd)),
                  full((1, C)), full((H, hd, hd)), full((1, C)), full((1, C)), full((3, C)), full((1, C)), full((1, C))],
        out_specs=[blk(5 * C), full((SMALL_ROWS, C)), full((H, hd, hd)), full((H, hd, hd))],
        out_shape=[SDS((T, 5 * C), BF16), SDS((SMALL_ROWS, C), F32), SDS((H, hd, hd), F32), SDS((H, hd, hd), F32)],
        scratch_shapes=[ext] * 6 + [tile] * 9 + [pltpu.VMEM((tb, 5 * C), F32), pltpu.VMEM((N_ACC, SUBLANES, C), F32),
                                                pltpu.VMEM((SUBLANES, C), F32)],
        compiler_params=_cp(("arbitrary",), 56),
    )(z, z, h, h, dy, cw, cb, wa, ba, wi, bi, lam, sw, glo, gso)


def _add_slabs(terms, out_dtype, name):
    R, Ccols = terms[0].shape
    br = _blk(R, 512, BF16_ROWS)
    n = len(terms)

    def body(*refs):
        s = refs[0][...].astype(F32)
        for t_ref in refs[1:n]:
            s = s + t_ref[...].astype(F32)
        refs[n][...] = s.astype(out_dtype)

    spec = pl.BlockSpec((br, Ccols), lambda i: (i, 0))
    return pl.pallas_call(
        body, name=name, grid=(R // br,), in_specs=[spec] * n, out_specs=spec, out_shape=SDS((R, Ccols), out_dtype),
        compiler_params=_cp(("arbitrary",), 40),
    )(*terms)


def _final_grad(sb, lb, chip, name):
    _, R, Ccols = sb.shape
    br = _blk(R, 512, BF16_ROWS)

    def body(chip_ref, sb_ref, l0, l1, l2, o_ref):
        s = sb_ref[0].astype(F32)
        for t_ref in (l0, l1, l2):
            s = s + t_ref[0].astype(F32)
        o_ref[...] = s

    lspec = lambda k: pl.BlockSpec((1, br, Ccols), lambda i, c: (k, i, 0))
    return pl.pallas_call(
        body, name=name,
        grid_spec=pltpu.PrefetchScalarGridSpec(
            num_scalar_prefetch=1, grid=(R // br,),
            in_specs=[pl.BlockSpec((1, br, Ccols), lambda i, c: (c[0], i, 0)), lspec(0), lspec(1), lspec(2)],
            out_specs=pl.BlockSpec((br, Ccols), lambda i, c: (i, 0))),
        out_shape=SDS((R, Ccols), F32),
        compiler_params=_cp(("arbitrary",), 40),
    )(chip, sb, lb, lb, lb)


def _adamw(w, g, m, v, name):
    R, Ccols = w.shape
    br = _blk(R, 256, SUBLANES)
    c1 = 1.0 - ADAM_B1 ** ADAM_STEP
    c2 = 1.0 - ADAM_B2 ** ADAM_STEP

    def body(w_ref, g_ref, m_ref, v_ref, d_ref, nm_ref, nv_ref):
        gv = g_ref[...]
        nm = ADAM_B1 * m_ref[...] + (1.0 - ADAM_B1) * gv
        nv = ADAM_B2 * v_ref[...] + (1.0 - ADAM_B2) * (gv * gv)
        nm_ref[...] = nm
        nv_ref[...] = nv
        d_ref[...] = -ADAM_LR * ((nm / c1) / (jnp.sqrt(nv / c2) + ADAM_EPS) + ADAM_WD * w_ref[...])

    spec = pl.BlockSpec((br, Ccols), lambda i: (i, 0))
    return pl.pallas_call(
        body, name=name, grid=(R // br,), in_specs=[spec] * 4, out_specs=[spec] * 3,
        out_shape=[SDS((R, Ccols), F32)] * 3, compiler_params=_cp(("arbitrary",), 40),
    )(w, g, m, v)


def _place():
    return lax.axis_index("x"), lax.axis_index("y"), lax.axis_index("c")


def _dev_rows(ref, dev, rows):
    return ref.at[pl.ds((4 * dev[0] + 2 * dev[1] + dev[2]) * rows, rows), :]


def _remote(src, dst, send_sem, recv_sem, to):
    return pltpu.make_async_remote_copy(src_ref=src, dst_ref=dst, send_sem=send_sem, recv_sem=recv_sem,
                                        device_id=to, device_id_type=MESH)


SAME_CORE_AND_SIBLING = ((0, 0, 1), (1, 0, 0), (0, 1, 0), (1, 1, 0))


def _merge_phases(a, b):
    na_in, na_out, na_sem = len(a.inputs), len(a.out_shapes), len(a.sem_shapes)

    def build(ins, outs, sems, stage):
        return (a.build(ins[:na_in], outs[:na_out], sems[:na_sem], stage)
                + b.build(ins[na_in:], outs[na_out:], sems[na_sem:], stage))

    aliases = dict(a.aliases)
    aliases.update({na_in + i: na_out + o for i, o in b.aliases.items()})
    return _Carried(a.inputs + b.inputs, a.out_shapes + b.out_shapes, aliases, a.sem_shapes + b.sem_shapes, build,
                    has_mid=a.has_mid or b.has_mid)


def _ag_direct_phase(slab, pieces, flips, begin="start"):
    W = slab.shape[1]
    n = len(pieces)
    npeer = len(flips)

    def build(ins, outs, sems, stage):
        if stage not in (begin, "end"):
            return []
        starting = stage == begin
        (slab_ref,) = ins
        send_sems, recv_sems, local_sems = sems
        x, y, c = _place()
        me = (x, y, c)
        peers = [tuple(1 - v if f else v for v, f in zip(me, flip)) for flip in flips]
        todo = []
        for p, (off, rows) in enumerate(pieces):
            src = slab_ref.at[pl.ds(off, rows), :]
            mine = pltpu.make_async_copy(src, _dev_rows(outs[p], me, rows), local_sems.at[p])
            todo.append(mine.start if starting else mine.wait)
            for k, peer in enumerate(peers):
                snd = _remote(src, _dev_rows(outs[p], me, rows), send_sems.at[k, p], recv_sems.at[k, p], peer)
                if starting:
                    todo.append(snd.start)
                else:
                    theirs = _dev_rows(outs[p], peer, rows)
                    rcv = _remote(theirs, theirs, send_sems.at[k, p], recv_sems.at[k, p], me)
                    todo += [rcv.wait_recv, snd.wait_send]
        return todo

    dma = pltpu.SemaphoreType.DMA
    return _Carried([slab], [SDS((N_DEV * rows, W), slab.dtype) for _, rows in pieces], {},
                    [dma((npeer, n)), dma((npeer, n)), dma((n,))], build, has_mid=begin == "mid")


def _ag_two_level_phase(slab, pieces):
    W = slab.shape[1]
    n = len(pieces)

    def build(ins, outs, sems, stage):
        (slab_ref,) = ins
        send_sems, recv_sems, local_sems = sems
        x, y, c = _place()
        me, sibling = (x, y, c), (x, y, 1 - c)
        chips = [(1 - x, y), (x, 1 - y), (1 - x, 1 - y)]
        todo = []
        for p, (off, rows) in enumerate(pieces):
            src = slab_ref.at[pl.ds(off, rows), :]
            own = _dev_rows(outs[p], me, rows)
            landed = [_dev_rows(outs[p], (*chip, c), rows) for chip in chips]

            def mine():
                return pltpu.make_async_copy(src, own, local_sems.at[p])

            def first():
                return [_remote(src, own, send_sems.at[k, p], recv_sems.at[k, p], to)
                        for k, to in enumerate([sibling] + [(*chip, c) for chip in chips])]

            def passed():
                return [_remote(blk, blk, send_sems.at[4 + j, p], recv_sems.at[4 + j, p], sibling)
                        for j, blk in enumerate(landed)]

            def arrival(k, blk):
                return _remote(blk, blk, send_sems.at[k, p], recv_sems.at[k, p], me).wait_recv

            if stage == "start":
                todo += [mine().start] + [cp.start for cp in first()]
            elif stage == "mid":
                for j, (blk, fwd) in enumerate(zip(landed, passed())):
                    todo += [arrival(1 + j, blk), fwd.start]
            else:
                theirs = [_dev_rows(outs[p], sibling, rows)] + [_dev_rows(outs[p], (*chip, 1 - c), rows) for chip in chips]
                todo += [arrival(k, blk) for k, blk in zip((0, 4, 5, 6), theirs)]
                todo += [cp.wait_send for cp in first() + passed()] + [mine().wait]
        return todo

    dma = pltpu.SemaphoreType.DMA
    return _Carried([slab], [SDS((N_DEV * rows, W), slab.dtype) for _, rows in pieces], {},
                    [dma((7, n)), dma((7, n)), dma((n,))], build, has_mid=True)


def _ag_forward_phase(gathered, pieces):
    n = len(pieces)

    def build(ins, outs, sems, stage):
        if stage == "mid":
            return []
        starting = stage == "start"
        send_sems, recv_sems = sems
        x, y, c = _place()
        me, sibling = (x, y, c), (x, y, 1 - c)
        chips = [(1 - x, y), (x, 1 - y), (1 - x, 1 - y)]
        todo = []
        for p, (_, rows) in enumerate(pieces):
            for j, chip in enumerate(chips):
                snd = _remote(_dev_rows(ins[p], (*chip, c), rows), _dev_rows(outs[p], (*chip, c), rows),
                              send_sems.at[j, p], recv_sems.at[j, p], sibling)
                if starting:
                    todo.append(snd.start)
                else:
                    theirs = _dev_rows(outs[p], (*chip, 1 - c), rows)
                    rcv = _remote(theirs, theirs, send_sems.at[j, p], recv_sems.at[j, p], me)
                    todo += [rcv.wait_recv, snd.wait_send]
        return todo

    dma = pltpu.SemaphoreType.DMA
    return _Carried(gathered, [SDS(g.shape, g.dtype) for g in gathered], {p: p for p in range(n)},
                    [dma((3, n)), dma((3, n))], build)


def _rs_chips_phase(sb):
    _, R, W = sb.shape

    def build(ins, outs, sems, stage):
        if stage == "mid":
            return []
        (sb_ref,), (land_ref,) = ins, outs
        send_sems, recv_sems = sems
        x, y, c = _place()
        chips = [(1 - x, y), (x, 1 - y), (1 - x, 1 - y)]
        cps = [_remote(sb_ref.at[2 * chip[0] + chip[1]], land_ref.at[j], send_sems.at[j], recv_sems.at[j], (*chip, c))
               for j, chip in enumerate(chips)]
        if stage == "start":
            return [cp.start for cp in cps]
        return [cp.wait_recv for cp in cps] + [cp.wait_send for cp in cps]

    dma = pltpu.SemaphoreType.DMA
    return _Carried([sb], [SDS((3, R, W), sb.dtype)], {}, [dma((3,)), dma((3,))], build)


def _allgather(slab, pieces, name):
    R, W = slab.shape
    n = len(pieces)
    assert sum(rows for _, rows in pieces) == R

    def body(slab_ref, *refs):
        outs = refs[:n]
        send_sems, recv_sems, local_sems = refs[n:]
        x, y, c = _place()
        me, sibling = (x, y, c), (x, y, 1 - c)
        chips = [(1 - x, y), (x, 1 - y), (1 - x, 1 - y)]

        def dst_rows(p, origin):
            rows = pieces[p][1]
            start = (4 * origin[0] + 2 * origin[1] + origin[2]) * rows
            return outs[p].at[pl.ds(start, rows), :]

        def copies(k, origin, to, from_slab):
            out = []
            for p, (off, rows) in enumerate(pieces):
                dst = dst_rows(p, origin)
                src = slab_ref.at[pl.ds(off, rows), :] if from_slab else dst
                out.append(pltpu.make_async_remote_copy(
                    src_ref=src, dst_ref=dst, send_sem=send_sems.at[k, p], recv_sem=recv_sems.at[k, p],
                    device_id=to, device_id_type=MESH))
            return out

        mine = [pltpu.make_async_copy(slab_ref.at[pl.ds(off, rows), :], dst_rows(p, me), local_sems.at[p])
                for p, (off, rows) in enumerate(pieces)]
        for cp in mine:
            cp.start()
        first = copies(0, me, sibling, True)
        for j, chip in enumerate(chips):
            first += copies(1 + j, me, (*chip, c), True)
        for cp in first:
            cp.start()
        passed = []
        for j, chip in enumerate(chips):
            for cp in copies(1 + j, (*chip, c), me, False):
                cp.wait_recv()
            fwd = copies(4 + j, (*chip, c), sibling, False)
            for cp in fwd:
                cp.start()
            passed += fwd
        for cp in copies(0, sibling, me, False):
            cp.wait_recv()
        for j, chip in enumerate(chips):
            for cp in copies(4 + j, (*chip, 1 - c), me, False):
                cp.wait_recv()
        for cp in first + passed:
            cp.wait_send()
        for cp in mine:
            cp.wait()

    return pl.pallas_call(
        body, name=name,
        in_specs=[HBM_SPEC], out_specs=[HBM_SPEC] * n,
        out_shape=[SDS((N_DEV * rows, W), slab.dtype) for _, rows in pieces],
        scratch_shapes=[pltpu.SemaphoreType.DMA((7, n)), pltpu.SemaphoreType.DMA((7, n)), pltpu.SemaphoreType.DMA((n,))],
    )(slab)


def _rs_sibling(grads, pieces, name):
    W = grads[0].shape[1]
    R = sum(rows for _, rows in pieces)
    n = len(pieces)
    dt = grads[0].dtype
    max_rows = max(rows for _, rows in pieces)
    steps = [(q, p) for q in range(N_CHIP) for p in range(n)]
    ns = len(steps)
    ADD_ROWS = 64
    SLOTS = 4
    assert all(rows % ADD_ROWS == 0 for _, rows in pieces)

    def body(*refs):
        g_refs = refs[:n]
        sb_ref, mine_buf, send_buf, land_buf, out_buf, in_sems, out_sems, send_sems, recv_sems, credit = refs[n:]
        x, y, c = _place()
        sibling = (x, y, 1 - c)

        def loads(s):
            q, p = steps[s]
            rows = pieces[p][1]
            slot = s % SLOTS
            mine = g_refs[p].at[pl.ds((2 * q + c) * rows, rows), :]
            theirs = g_refs[p].at[pl.ds((2 * q + 1 - c) * rows, rows), :]
            return (pltpu.make_async_copy(mine, mine_buf.at[slot, pl.ds(0, rows), :], in_sems.at[slot, 0]),
                    pltpu.make_async_copy(theirs, send_buf.at[slot, pl.ds(0, rows), :], in_sems.at[slot, 1]))

        def send(s):
            rows = pieces[steps[s][1]][1]
            slot = s % SLOTS
            return pltpu.make_async_remote_copy(
                src_ref=send_buf.at[slot, pl.ds(0, rows), :], dst_ref=land_buf.at[slot, pl.ds(0, rows), :],
                send_sem=send_sems.at[slot], recv_sem=recv_sems.at[slot], device_id=sibling, device_id_type=MESH)

        def store(s):
            q, p = steps[s]
            off, rows = pieces[p]
            slot = s % SLOTS
            return pltpu.make_async_copy(out_buf.at[slot, pl.ds(0, rows), :], sb_ref.at[q, pl.ds(off, rows), :],
                                         out_sems.at[slot])

        def start_send(s):
            for cp in loads(s):
                cp.wait()
            if s >= SLOTS:
                pl.semaphore_wait(credit.at[s % SLOTS], 1)
            send(s).start()

        for s in range(min(SLOTS, ns)):
            for cp in loads(s):
                cp.start()
        for s in range(min(SLOTS - 1, ns)):
            start_send(s)
        for s in range(ns):
            slot = s % SLOTS
            rows = pieces[steps[s][1]][1]
            if s + SLOTS - 1 < ns:
                start_send(s + SLOTS - 1)
            send(s).wait_recv()
            if s >= SLOTS:
                store(s - SLOTS).wait()

            def add(k, _, slot=slot):
                r = pl.ds(pl.multiple_of(k * ADD_ROWS, ADD_ROWS), ADD_ROWS)
                out_buf[slot, r, :] = (mine_buf[slot, r, :].astype(F32) + land_buf[slot, r, :].astype(F32)).astype(dt)
                return 0
            lax.fori_loop(0, rows // ADD_ROWS, add, 0)
            if s + SLOTS < ns:
                pl.semaphore_signal(credit.at[slot], inc=1, device_id=sibling, device_id_type=MESH)
            store(s).start()
            send(s).wait_send()
            if s + SLOTS < ns:
                for cp in loads(s + SLOTS):
                    cp.start()
        for s in range(max(ns - SLOTS, 0), ns):
            store(s).wait()

    buf = pltpu.VMEM((SLOTS, max_rows, W), dt)
    return pl.pallas_call(
        body, name=name,
        in_specs=[HBM_SPEC] * n, out_specs=HBM_SPEC,
        out_shape=SDS((N_CHIP, R, W), dt),
        scratch_shapes=[buf, buf, buf, buf, pltpu.SemaphoreType.DMA((SLOTS, 2)), pltpu.SemaphoreType.DMA((SLOTS,)),
                        pltpu.SemaphoreType.DMA((SLOTS,)), pltpu.SemaphoreType.DMA((SLOTS,)),
                        pltpu.SemaphoreType.REGULAR((SLOTS,))],
        compiler_params=pltpu.CompilerParams(vmem_limit_bytes=56 * MIB),
    )(*grads)


SMALL_NAMES = ("ffn1_norm", "mix_norm", "ffn2_norm", "final_norm", "lru_conv_w", "lru_conv_b", "lru_w_a", "lru_b_a",
               "lru_w_i", "lru_b_i", "lru_lambda", "sc_conv_w", "lru_out_norm", "sc_out_norm")
WEIGHT_NAMES = ("ffn1_norm", "ffn1_w_gate", "ffn1_w_up", "ffn1_w_down", "mix_norm", "w_in", "lru_conv_w", "lru_conv_b",
                "lru_w_a", "lru_b_a", "lru_w_i", "lru_b_i", "lru_lambda", "sc_conv_w", "lru_out_norm", "sc_out_norm",
                "w_out", "ffn2_norm", "ffn2_w_gate", "ffn2_w_up", "ffn2_w_down", "final_norm")
BIG = (("ffn1_w_gate", True), ("ffn1_w_up", True), ("ffn1_w_down", False), ("ffn2_w_gate", True), ("ffn2_w_up", True),
       ("ffn2_w_down", False), ("w_in", True), ("w_out", False))


SLAB_ROW_ALIGN = 256


def _pack_rows(parts, width):
    rows, counts = [], []
    for p in parts:
        flat = p.reshape(-1)
        nr = -(-flat.shape[0] // width)
        nr = -(-nr // SUBLANES) * SUBLANES
        rows.append(jnp.pad(flat, (0, nr * width - flat.shape[0])).reshape(nr, width))
        counts.append(nr)
    total = sum(counts)
    pad = -(-total // SLAB_ROW_ALIGN) * SLAB_ROW_ALIGN - total
    if pad:
        rows.append(jnp.zeros((pad, width), rows[0].dtype))
    return jnp.concatenate(rows, axis=0), counts


def _stack_rows(blocks):
    pieces, off = [], 0
    for b in blocks:
        pieces.append((off, b.shape[0]))
        off += b.shape[0]
    return jnp.concatenate(blocks, axis=0), pieces


def _unpack_rows(slab, counts, shapes):
    out, r = [], 0
    for nr, shape in zip(counts, shapes):
        size = math.prod(shape)
        out.append(slab[r:r + nr].reshape(-1)[:size].reshape(shape))
        r += nr
    return out


def kernel(x, ffn1_norm, ffn1_w_gate, ffn1_w_up, ffn1_w_down, mix_norm, w_in, lru_conv_w, lru_conv_b, lru_w_a, lru_b_a, lru_w_i, lru_b_i, lru_lambda, sc_conv_w, lru_out_norm, sc_out_norm, w_out, ffn2_norm, ffn2_w_gate, ffn2_w_up, ffn2_w_down, final_norm, loss_target, m_ffn1_norm, m_ffn1_w_gate, m_ffn1_w_up, m_ffn1_w_down, m_mix_norm, m_w_in, m_lru_conv_w, m_lru_conv_b, m_lru_w_a, m_lru_b_a, m_lru_w_i, m_lru_b_i, m_lru_lambda, m_sc_conv_w, m_lru_out_norm, m_sc_out_norm, m_w_out, m_ffn2_norm, m_ffn2_w_gate, m_ffn2_w_up, m_ffn2_w_down, m_final_norm, v_ffn1_norm, v_ffn1_w_gate, v_ffn1_w_up, v_ffn1_w_down, v_mix_norm, v_w_in, v_lru_conv_w, v_lru_conv_b, v_lru_w_a, v_lru_b_a, v_lru_w_i, v_lru_b_i, v_lru_lambda, v_sc_conv_w, v_lru_out_norm, v_sc_out_norm, v_w_out, v_ffn2_norm, v_ffn2_w_gate, v_ffn2_w_up, v_ffn2_w_down, v_final_norm):
    a = dict(locals())
    w = {n: a[n] for n in WEIGHT_NAMES}
    m = {n: a["m_" + n] for n in WEIGHT_NAMES}
    v = {n: a["v_" + n] for n in WEIGHT_NAMES}
    ax, ay, ac = _place()
    dev = 4 * ax + 2 * ay + ac
    chip = (2 * ax + ay).astype(jnp.int32).reshape(1)

    x0 = x[0]
    tgt = loss_target[0]
    T, D = x0.shape
    C = D // 2
    H, hd = lru_w_a.shape[1], lru_w_a.shape[2]
    CL = lru_conv_w.shape[2]

    shards = []
    for name, transposed in BIG:
        s = w[name][0]
        shards.append((s.T if transposed else s).astype(BF16))
    taps = jnp.concatenate([lru_conv_w[0], sc_conv_w[0], jnp.zeros((1, CL), F32)], axis=0)
    taps_row = lax.bitcast_convert_type(taps, BF16).reshape(1, -1)
    taps_blk = jnp.pad(taps_row, ((0, BF16_ROWS - 1), (0, D - taps_row.shape[1])))
    s_wg1, s_wu1, s_wd1, s_wg2, s_wu2, s_wd2, s_win, s_wout = shards
    slab_g1, pcs_g1 = _stack_rows([s_wg1])
    slab_u1, pcs_u1 = _stack_rows([s_wu1])
    slab_d1, pcs_d1 = _stack_rows([s_wd1])
    slab_mw, pcs_mw = _stack_rows([s_win, s_wout, taps_blk])
    slab_gu2, pcs_gu2 = _stack_rows([s_wg2, s_wu2])
    slab_d2, pcs_d2 = _stack_rows([s_wd2])
    (wg1,) = _allgather(slab_g1, pcs_g1, "allgather_ffn1_gate")

    g1, gm, g3 = ffn1_norm, mix_norm, ffn2_norm
    phase = _merge_phases(_ag_two_level_phase(slab_u1, pcs_u1),
                          _ag_direct_phase(slab_d1, pcs_d1, SAME_CORE_AND_SIBLING, begin="mid"))
    (n1, hg1), got = _norm_proj(x0, g1, [wg1], [BF16], False, "ffn1_gate", carried=phase)
    wu1, d1 = got[0], got[1:]
    phase = _merge_phases(_ag_forward_phase(d1, pcs_d1), _ag_direct_phase(slab_mw, pcs_mw, SAME_CORE_AND_SIBLING))
    (hu1, act1), got = _up_act(n1, wu1, hg1, "ffn1_up", carried=phase)
    wd1, mixw = got[0], got[1:]
    phase = _merge_phases(_ag_forward_phase(mixw, pcs_mw), _ag_direct_phase(slab_gu2, pcs_gu2, SAME_CORE_AND_SIBLING))
    x1, got = _mm_res(act1, wd1, x0, 0.5, "ffn1_down", carried=phase)
    (win, wout, taps_all), gu2 = got[:3], got[3:]
    phase = _merge_phases(_ag_forward_phase(gu2, pcs_gu2), _ag_direct_phase(slab_d2, pcs_d2, SAME_CORE_AND_SIBLING))
    (n2, z), got = _norm_proj(x1, gm, [win], [F32], False, "in_proj", carried=phase)
    (wg2, wu2), d2 = got[:2], got[2:]
    taps_all = taps_all.reshape(N_DEV, BF16_ROWS, D)[:, 0, :2 * SUBLANES * CL].reshape(N_DEV, SUBLANES, CL, 2)
    taps_all = lax.bitcast_convert_type(taps_all, F32)
    taps_all = taps_all.transpose(1, 0, 2).reshape(SUBLANES, N_DEV * CL)
    cw, sw = taps_all[0:4], taps_all[4:7]

    gf = final_norm.reshape(1, D)
    cb = lru_conv_b
    wa, wi = lru_w_a[0].astype(BF16), lru_w_i[0].astype(BF16)
    ba, bi = lru_b_a.reshape(1, C), lru_b_i.reshape(1, C)
    lam, glo, gso = lru_lambda, lru_out_norm, sc_out_norm

    y, h = _mix_fwd(z, cw, cb, wa, ba, wi, bi, lam, sw, glo, gso, "mix_fwd")
    x2, (wd2,) = _mm_res(y, wout, x1, 1.0, "out_proj", carried=_ag_forward_phase(d2, pcs_d2))
    n3, hg2, hu2, act2 = _norm_proj(x2, g3, [wg2, wu2], [BF16, BF16], True, "ffn2_up")
    x3 = _mm_res(act2, wd2, x2, 0.5, "ffn2_down")
    dx3, df2, d_gf, loss_blk = _loss_head(x3, gf, tgt, "loss_head")

    F = wd1.shape[0]
    bm_f = F // 4 if (F // 4) % LANES == 0 else 512

    def reduce_group(gs, tag):
        pcs, off = [], 0
        for g_ in gs:
            pcs.append((off, g_.shape[0] // N_DEV))
            off += g_.shape[0] // N_DEV
        sb_ = _rs_sibling(gs, pcs, "rs_sibling_add_" + tag)
        return sb_, pcs

    dhg2, dhu2 = _ffn_bwd_act(df2, wd2, hg2, hu2, "ffn2_bwd_act")
    d_wd2 = _dw_tn(act2, df2, bm_f, "ffn2_dw_down")
    d_wg2 = _dw_tn(dhg2, n3, bm_f, "ffn2_dw_gate")
    d_wu2 = _dw_tn(dhu2, n3, bm_f, "ffn2_dw_up")
    sb_f2, pcs_f2 = reduce_group([d_wg2, d_wu2, d_wd2], "ffn2")
    (dx2, dx2b, d_g3), (lb_f2,) = _mm_rmsbwd([(dhg2, wg2), (dhu2, wu2)], x2, g3, dx3, 1.0, "ffn2_bwd_in",
                                             carried=_rs_chips_phase(sb_f2))
    dy = _mm_nt(dx2b, wout, "out_proj_bwd")
    d_wout = _dw_tn(y, dx2b, 1024, "out_proj_dw")
    dz, small, d_wa, d_wi = _mix_bwd(z, h, dy, cw, cb, wa, ba, wi, bi, lam, sw, glo, gso, "mix_bwd")
    d_win = _dw_tn(dz, n2, 1280, "in_proj_dw")
    sb_mx, pcs_mx = reduce_group([d_win, d_wout], "mix")
    (dx1, df1, d_gm), (lb_mx,) = _mm_rmsbwd([(dz, win)], x1, gm, dx2, 0.5, "in_proj_bwd",
                                            carried=_rs_chips_phase(sb_mx))
    dhg1, dhu1 = _ffn_bwd_act(df1, wd1, hg1, hu1, "ffn1_bwd_act")
    d_wd1 = _dw_tn(act1, df1, bm_f, "ffn1_dw_down")
    d_wg1 = _dw_tn(dhg1, n1, bm_f, "ffn1_dw_gate")
    d_wu1 = _dw_tn(dhu1, n1, bm_f, "ffn1_dw_up")
    sb_f1, pcs_f1 = reduce_group([d_wg1, d_wu1, d_wd1], "ffn1")
    (dx0, _, d_g1), (lb_f1,) = _mm_rmsbwd([(dhg1, wg1), (dhu1, wu1)], x0, g1, dx1, 1.0, "ffn1_bwd_in",
                                          carried=_rs_chips_phase(sb_f1))

    big_sum = {}
    for tag, names, sb_, lb_, pcs in (("ffn2", ("ffn2_w_gate", "ffn2_w_up", "ffn2_w_down"), sb_f2, lb_f2, pcs_f2),
                                      ("mix", ("w_in", "w_out"), sb_mx, lb_mx, pcs_mx),
                                      ("ffn1", ("ffn1_w_gate", "ffn1_w_up", "ffn1_w_down"), sb_f1, lb_f1, pcs_f1)):
        gsum = _final_grad(sb_, lb_, chip, "rs_final_sum_" + tag)
        for name, (off, rows) in zip(names, pcs):
            big_sum[name] = gsum[off:off + rows]

    vec_names = [n_ for n_ in SMALL_NAMES if n_ not in ("lru_w_a", "lru_w_i")]
    vec_parts = dict(zip(SMALL_NAMES, [d_g1, d_gm, d_g3, d_gf, small[R_CW:R_CW + 4], small[R_CB], None, small[R_BA], None,
                                       small[R_BI], small[R_LAM], small[R_SW:R_SW + 3], small[R_GLO], small[R_GSO]]))
    vslab, counts = _pack_rows([vec_parts[n_] for n_ in vec_names], LANES)
    mats = jnp.concatenate([d_wa.reshape(-1, LANES), d_wi.reshape(-1, LANES)], axis=0).astype(BF16)
    mwords = lax.bitcast_convert_type(mats.reshape(-1, LANES, 2), F32)
    RV, RM = vslab.shape[0], mwords.shape[0]
    RS = RV + RM
    (sg,) = _allgather(jnp.concatenate([vslab, mwords], axis=0), [(0, RS)], "allgather_small_grads")
    sg = sg.reshape(N_DEV, RS, LANES)
    vsum = _add_slabs([sg[j, :RV] for j in range(N_DEV)], F32, "small_grads_sum")
    mgot = lax.bitcast_convert_type(sg[:, RV:], BF16).reshape(N_DEV, 2 * RM, LANES)
    msum = _add_slabs([mgot[j] for j in range(N_DEV)], F32, "gate_grads_sum")
    shape_of = dict(zip(SMALL_NAMES, [(1, D), (1, D), (1, D), (D,), (1, 4, C), (1, C), (1, H, hd, hd), (1, H, hd),
                                      (1, H, hd, hd), (1, H, hd), (1, C), (1, 3, C), (1, C), (1, C)]))
    small_full = dict(zip(vec_names, _unpack_rows(vsum, counts, [shape_of[n_] for n_ in vec_names])))
    half = msum.shape[0] // 2
    small_full["lru_w_a"] = msum[:half].reshape(shape_of["lru_w_a"])
    small_full["lru_w_i"] = msum[half:].reshape(shape_of["lru_w_i"])

    grads = {}
    for name, transposed in BIG:
        gblk = big_sum[name]
        grads[name] = (gblk.T if transposed else gblk)[None]
    for name in SMALL_NAMES:
        gfull = small_full[name]
        if name in ("lru_conv_w", "sc_conv_w"):
            gfull = lax.dynamic_slice_in_dim(gfull, dev * CL, CL, axis=2)
        grads[name] = gfull

    delta, new_m, new_v = {}, {}, {}
    for name, transposed in BIG:
        flip = transposed and w[name].shape[2] % LANES != 0
        view = (lambda t: t[0].T) if flip else (lambda t: t[0])
        back = (lambda t: t.T[None]) if flip else (lambda t: t[None])
        gview = big_sum[name] if flip else grads[name][0]
        d_, m_, v_ = _adamw(view(w[name]), gview, view(m[name]), view(v[name]), "adamw_" + name)
        delta[name], new_m[name], new_v[name] = back(d_), back(m_), back(v_)
    packs = [_pack_rows([t[n_] for n_ in SMALL_NAMES], LANES) for t in (w, grads, m, v)]
    sd, sm, sv = _adamw(packs[0][0], packs[1][0], packs[2][0], packs[3][0], "adamw_small")
    shapes = [w[n_].shape for n_ in SMALL_NAMES]
    for tgt_dict, slab_ in ((delta, sd), (new_m, sm), (new_v, sv)):
        for n_, val in zip(SMALL_NAMES, _unpack_rows(slab_, packs[0][1], shapes)):
            tgt_dict[n_] = val

    loss = lax.psum(loss_blk[0, 0], ("x", "y", "c"))
    return (loss, dx0[None], *[grads[n_] for n_ in WEIGHT_NAMES], *[delta[n_] for n_ in WEIGHT_NAMES],
            *[new_m[n_] for n_ in WEIGHT_NAMES], *[new_v[n_] for n_ in WEIGHT_NAMES])
```

```python
import functools
import math

import jax
import jax.numpy as jnp
from jax import lax
from jax.experimental import pallas as pl
from jax.experimental.pallas import tpu as pltpu

F32 = jnp.float32
BF16 = jnp.bfloat16
SDS = jax.ShapeDtypeStruct
MESH = pl.DeviceIdType.MESH

NORM_EPS = 1e-6
LRU_C = 8.0
N_DEV = 8
N_CHIP = 4
ADAM_LR, ADAM_B1, ADAM_B2, ADAM_EPS, ADAM_WD, ADAM_STEP = 0.001, 0.9, 0.999, 1e-08, 0.01, 10

NN = (((1,), (0,)), ((), ()))
NT = (((1,), (1,)), ((), ()))
TN = (((0,), (0,)), ((), ()))

SUBLANES = 8
BF16_ROWS = 16
LANES = 128
MIB = 1 << 20


def _dot(a, b, dims):
    return lax.dot_general(a, b, dims, preferred_element_type=F32)


def _blk(n, pref, align):
    if n <= pref:
        return n
    b = (pref // align) * align
    while b >= align:
        if n % b == 0:
            return b
        b -= align
    raise ValueError(f"no block of {n} aligned to {align} under {pref}")


def _cp(sem, vmem_mib):
    return pltpu.CompilerParams(dimension_semantics=sem, vmem_limit_bytes=vmem_mib * MIB)


HBM_SPEC = pl.BlockSpec(memory_space=pltpu.HBM)
MID_EIGHTHS = 5

class _Carried:
    def __init__(self, inputs, out_shapes, aliases, sem_shapes, build, has_mid=False):
        self.inputs, self.out_shapes, self.aliases = list(inputs), list(out_shapes), dict(aliases)
        self.sem_shapes, self.build, self.has_mid = list(sem_shapes), build, has_mid


def _call(body, *, name, grid, in_specs, out_specs, out_shape, scratch_shapes, compiler_params, args, carried=None):
    if carried is None:
        return pl.pallas_call(body, name=name, grid=grid, in_specs=in_specs, out_specs=out_specs, out_shape=out_shape,
                              scratch_shapes=scratch_shapes, compiler_params=compiler_params)(*args)
    n_in, n_out, n_sc = len(in_specs), len(out_shape), len(scratch_shapes)
    c_in, c_out = len(carried.inputs), len(carried.out_shapes)

    def hosted(*refs):
        ins, refs = refs[:n_in], refs[n_in:]
        c_ins, refs = refs[:c_in], refs[c_in:]
        outs, refs = refs[:n_out], refs[n_out:]
        c_outs, refs = refs[:c_out], refs[c_out:]
        scratch, c_sems = refs[:n_sc], refs[n_sc:]
        first = functools.reduce(jnp.logical_and, [pl.program_id(a) == 0 for a in range(len(grid))])
        last = functools.reduce(jnp.logical_and, [pl.program_id(a) == g - 1 for a, g in enumerate(grid)])

        @pl.when(first)
        def _():
            for start in carried.build(c_ins, c_outs, c_sems, "start"):
                start()

        if carried.has_mid:
            mid = functools.reduce(jnp.logical_and, [pl.program_id(0) == (grid[0] * MID_EIGHTHS) // 8]
                                   + [pl.program_id(a) == 0 for a in range(1, len(grid))])

            @pl.when(mid)
            def _():
                for step in carried.build(c_ins, c_outs, c_sems, "mid"):
                    step()

        body(*ins, *outs, *scratch)

        @pl.when(last)
        def _():
            for wait in carried.build(c_ins, c_outs, c_sems, "end"):
                wait()

    out = pl.pallas_call(
        hosted, name=name, grid=grid, in_specs=list(in_specs) + [HBM_SPEC] * c_in,
        out_specs=list(out_specs) + [HBM_SPEC] * c_out, out_shape=list(out_shape) + carried.out_shapes,
        scratch_shapes=list(scratch_shapes) + carried.sem_shapes,
        input_output_aliases={n_in + a: n_out + b for a, b in carried.aliases.items()},
        compiler_params=compiler_params)(*args, *carried.inputs)
    return out[:n_out], out[n_out:]


ROW_CHUNK = 128


def _chunk_rows(c):
    return pl.ds(pl.multiple_of(c * ROW_CHUNK, ROW_CHUNK), ROW_CHUNK)


def _rstd(xv):
    return lax.rsqrt(jnp.mean(xv * xv, axis=-1, keepdims=True) + NORM_EPS)


def _rms_bwd(xv, g, dn):
    r = _rstd(xv)
    xr = xv * r
    gd = g * dn
    dx = r * (gd - xr * jnp.mean(gd * xr, axis=-1, keepdims=True))
    return dx, jnp.sum(dn * xr, axis=0, keepdims=True)


def _log1p(e):
    u = 1.0 + e
    return jnp.where(u == 1.0, e, jnp.log(u) * (e / (u - 1.0)))


def _one_minus_exp(v, exp_half_v):
    series = 1.0 / 5040.0
    for coeff in (1.0 / 720.0, 1.0 / 120.0, 1.0 / 24.0, 1.0 / 6.0, 0.5, 1.0):
        series = series * v + coeff
    return jnp.where(v > -0.5, -v * series, 1.0 - exp_half_v * exp_half_v)


def _sigmoid(v):
    return 0.5 * jnp.tanh(0.5 * v) + 0.5


def _gelu_parts(g):
    k0 = math.sqrt(2.0 / math.pi)
    g2 = g * g
    t = jnp.tanh(k0 * (g + 0.044715 * g * g2))
    gel = 0.5 * g * (1.0 + t)
    gelp = 0.5 * (1.0 + t) + 0.5 * g * (1.0 - t * t) * (k0 * (1.0 + 3.0 * 0.044715 * g2))
    return gel, gelp


def _norm_proj(x, gain, w_list, out_dtypes, swiglu, name, carried=None):
    T, D = x.shape
    N = w_list[0].shape[0]
    nw = len(w_list)
    bm = _blk(T, 1024, BF16_ROWS)
    bn = _blk(N, 512, LANES)

    def body(*refs):
        x_ref, g_ref = refs[:2]
        w_refs = refs[2:2 + nw]
        n_ref = refs[2 + nw]
        o_refs = refs[3 + nw:3 + 2 * nw]
        act_ref = refs[3 + 2 * nw] if swiglu else None
        n_sc = refs[-1]

        @pl.when(pl.program_id(1) == 0)
        def _():
            def chunk(c, _):
                r = _chunk_rows(c)
                xv = x_ref[r, :]
                nb = (xv * _rstd(xv) * g_ref[...]).astype(BF16)
                n_sc[r, :] = nb
                n_ref[r, :] = nb
                return 0
            lax.fori_loop(0, bm // ROW_CHUNK, chunk, 0)

        n = n_sc[...]
        outs = [_dot(n, w_ref[...], NT) for w_ref in w_refs]
        for o_ref, o in zip(o_refs, outs):
            o_ref[...] = o.astype(o_ref.dtype)
        if swiglu:
            hg, hu = outs
            act_ref[...] = (hg * _sigmoid(hg) * hu).astype(BF16)

    row = pl.BlockSpec((bm, D), lambda i, j: (i, 0))
    tile = pl.BlockSpec((bm, bn), lambda i, j: (i, j))
    n_extra = 1 if swiglu else 0
    return _call(
        body, name=name, grid=(T // bm, N // bn),
        in_specs=[row, pl.BlockSpec((1, D), lambda i, j: (0, 0))] + [pl.BlockSpec((bn, D), lambda i, j: (j, 0))] * nw,
        out_specs=[row] + [tile] * (nw + n_extra),
        out_shape=[SDS((T, D), BF16)] + [SDS((T, N), dt) for dt in out_dtypes] + [SDS((T, N), BF16)] * n_extra,
        scratch_shapes=[pltpu.VMEM((bm, D), BF16)],
        compiler_params=_cp(("arbitrary", "arbitrary"), 52),
        args=(x, gain, *w_list), carried=carried)


def _up_act(n, wu, hg, name, carried=None):
    T, D = n.shape
    F = wu.shape[0]
    bm = _blk(T, 1024, BF16_ROWS)
    bn = _blk(F, 512, LANES)

    def body(n_ref, wu_ref, hg_ref, hu_ref, act_ref):
        hu = _dot(n_ref[...], wu_ref[...], NT)
        hg = hg_ref[...].astype(F32)
        hu_ref[...] = hu.astype(BF16)
        act_ref[...] = (hg * _sigmoid(hg) * hu).astype(BF16)

    tile = pl.BlockSpec((bm, bn), lambda i, j: (i, j))
    return _call(
        body, name=name, grid=(T // bm, F // bn),
        in_specs=[pl.BlockSpec((bm, D), lambda i, j: (i, 0)), pl.BlockSpec((bn, D), lambda i, j: (j, 0)), tile],
        out_specs=[tile, tile], out_shape=[SDS((T, F), BF16)] * 2, scratch_shapes=[],
        compiler_params=_cp(("arbitrary", "arbitrary"), 40),
        args=(n, wu, hg), carried=carried)


def _mm_res(a, b, x, scale, name, carried=None):
    T, K = a.shape
    D = b.shape[1]
    bm = _blk(T, 1024, BF16_ROWS)
    bk = _blk(K, 1408, LANES)
    nk = K // bk

    def body(a_ref, b_ref, x_ref, o_ref):
        k = pl.program_id(1)

        @pl.when(k == 0)
        def _():
            o_ref[...] = jnp.zeros_like(o_ref)

        o_ref[...] += _dot(a_ref[...], b_ref[...], NN)

        @pl.when(k == nk - 1)
        def _():
            def chunk(c, _):
                r = _chunk_rows(c)
                o_ref[r, :] = x_ref[r, :] + scale * o_ref[r, :]
                return 0
            lax.fori_loop(0, bm // ROW_CHUNK, chunk, 0)

    row = pl.BlockSpec((bm, D), lambda i, k: (i, 0))
    out = _call(
        body, name=name, grid=(T // bm, nk),
        in_specs=[pl.BlockSpec((bm, bk), lambda i, k: (i, k)), pl.BlockSpec((bk, D), lambda i, k: (k, 0)), row],
        out_specs=[row], out_shape=[SDS((T, D), F32)], scratch_shapes=[],
        compiler_params=_cp(("arbitrary", "arbitrary"), 56),
        args=(a, b, x), carried=carried)
    return out[0] if carried is None else (out[0][0], out[1])


def _mm_nt(a, b, name):
    T, K = a.shape
    N = b.shape[0]
    bm = _blk(T, 1024, BF16_ROWS)
    bn = _blk(N, 512, LANES)

    def body(a_ref, b_ref, o_ref):
        o_ref[...] = _dot(a_ref[...], b_ref[...], NT)

    return pl.pallas_call(
        body, name=name, grid=(T // bm, N // bn),
        in_specs=[pl.BlockSpec((bm, K), lambda i, j: (i, 0)), pl.BlockSpec((bn, K), lambda i, j: (j, 0))],
        out_specs=pl.BlockSpec((bm, bn), lambda i, j: (i, j)), out_shape=SDS((T, N), F32),
        compiler_params=_cp(("arbitrary", "arbitrary"), 40),
    )(a, b)


def _ffn_bwd_act(dfb, wd, hg, hu, name):
    T, D = dfb.shape
    F = wd.shape[0]
    bm = _blk(T, 1024, BF16_ROWS)
    bn = _blk(F, 512, LANES)

    def body(df_ref, wd_ref, hg_ref, hu_ref, dhg_ref, dhu_ref):
        dact = _dot(df_ref[...], wd_ref[...], NT)
        hgv = hg_ref[...].astype(F32)
        huv = hu_ref[...].astype(F32)
        s = _sigmoid(hgv)
        dhu_ref[...] = (dact * (hgv * s)).astype(BF16)
        dhg_ref[...] = (dact * huv * (s * (1.0 + hgv * (1.0 - s)))).astype(BF16)

    tile = pl.BlockSpec((bm, bn), lambda i, j: (i, j))
    return pl.pallas_call(
        body, name=name, grid=(T // bm, F // bn),
        in_specs=[pl.BlockSpec((bm, D), lambda i, j: (i, 0)), pl.BlockSpec((bn, D), lambda i, j: (j, 0)), tile, tile],
        out_specs=[tile, tile], out_shape=[SDS((T, F), BF16)] * 2,
        compiler_params=_cp(("arbitrary", "arbitrary"), 40),
    )(dfb, wd, hg, hu)


def _dw_tn(a, b, bm_pref, name):
    T, M = a.shape
    N = b.shape[1]
    bm = _blk(M, bm_pref, LANES)
    tk = _blk(T, 1024, BF16_ROWS)
    nk = T // tk

    def body(a_ref, b_ref, o_ref, acc):
        k = pl.program_id(1)

        @pl.when(k == 0)
        def _():
            acc[...] = jnp.zeros_like(acc)

        acc[...] += _dot(a_ref[...], b_ref[...], TN)

        @pl.when(k == nk - 1)
        def _():
            o_ref[...] = acc[...].astype(BF16)

    return pl.pallas_call(
        body, name=name, grid=(M // bm, nk),
        in_specs=[pl.BlockSpec((tk, bm), lambda i, k: (k, i)), pl.BlockSpec((tk, N), lambda i, k: (k, 0))],
        out_specs=pl.BlockSpec((bm, N), lambda i, k: (i, 0)), out_shape=SDS((M, N), BF16),
        scratch_shapes=[pltpu.VMEM((bm, N), F32)],
        compiler_params=_cp(("arbitrary", "arbitrary"), 48),
    )(a, b)


def _mm_rmsbwd(pairs, x, gain, dx_in, bscale, name, carried=None):
    T, D = x.shape
    K = pairs[0][0].shape[1]
    npair = len(pairs)
    bm = _blk(T, 1024, BF16_ROWS)
    bk = _blk(K, 1024 // npair, LANES)
    nk = K // bk

    nchunk = bm // ROW_CHUNK

    def body(*refs):
        ab = refs[:2 * npair]
        x_hbm, g_ref, dxin_hbm, dx_ref, dxb_ref, dg_ref, x_buf, dxin_buf, sems = refs[2 * npair:]
        i = pl.program_id(0)
        k = pl.program_id(1)

        def fetch(c, slot):
            rows = pl.ds(i * bm + c * ROW_CHUNK, ROW_CHUNK)
            return (pltpu.make_async_copy(x_hbm.at[rows, :], x_buf.at[slot], sems.at[slot, 0]),
                    pltpu.make_async_copy(dxin_hbm.at[rows, :], dxin_buf.at[slot], sems.at[slot, 1]))

        @pl.when(k == 0)
        def _():
            dx_ref[...] = jnp.zeros_like(dx_ref)

        @pl.when(k == nk - 1)
        def _():
            for cp in fetch(0, 0):
                cp.start()

        for q in range(npair):
            dx_ref[...] += _dot(ab[2 * q][...], ab[2 * q + 1][...], NN)

        @pl.when(k == nk - 1)
        def _():
            @pl.when(i == 0)
            def _():
                dg_ref[...] = jnp.zeros_like(dg_ref)

            def chunk(c, _):
                slot = c % 2

                @pl.when(c + 1 < nchunk)
                def _():
                    for cp in fetch(c + 1, 1 - slot):
                        cp.start()

                for cp in fetch(c, slot):
                    cp.wait()

                r = _chunk_rows(c)
                dx, dg = _rms_bwd(x_buf[slot], g_ref[...], dx_ref[r, :])
                dxo = dxin_buf[slot] + dx
                dx_ref[r, :] = dxo
                dxb_ref[r, :] = (bscale * dxo).astype(BF16)
                dg_ref[...] += dg
                return 0
            lax.fori_loop(0, nchunk, chunk, 0)

    row = pl.BlockSpec((bm, D), lambda i, k: (i, 0))
    anywhere = pl.BlockSpec(memory_space=pl.ANY)
    vec = pl.BlockSpec((1, D), lambda i, k: (0, 0))
    in_specs = []
    args = []
    for a, b in pairs:
        in_specs += [pl.BlockSpec((bm, bk), lambda i, k: (i, k)), pl.BlockSpec((bk, D), lambda i, k: (k, 0))]
        args += [a, b]
    return _call(
        body, name=name, grid=(T // bm, nk),
        in_specs=in_specs + [anywhere, vec, anywhere], out_specs=[row, row, vec],
        out_shape=[SDS((T, D), F32), SDS((T, D), BF16), SDS((1, D), F32)],
        scratch_shapes=[pltpu.VMEM((2, ROW_CHUNK, D), F32), pltpu.VMEM((2, ROW_CHUNK, D), F32),
                        pltpu.SemaphoreType.DMA((2, 2))],
        compiler_params=_cp(("arbitrary", "arbitrary"), 52),
        args=(*args, x, gain, dx_in), carried=carried)


def _loss_head(x3, gain, tgt, name):
    T, D = x3.shape
    bm = _blk(T, 256, BF16_ROWS)

    def body(x_ref, g_ref, t_ref, dx_ref, dxb_ref, dg_ref, loss_ref):
        i = pl.program_id(0)
        xv = x_ref[...]
        g = g_ref[...]
        out = xv * _rstd(xv) * g
        e = out - t_ref[...]
        part = 0.5 * jnp.sum(jnp.mean(e * e, axis=-1, keepdims=True), axis=0, keepdims=True)
        dx, dg = _rms_bwd(xv, g, e * (1.0 / D))
        dx_ref[...] = dx
        dxb_ref[...] = (0.5 * dx).astype(BF16)

        @pl.when(i == 0)
        def _():
            dg_ref[...] = dg
            loss_ref[...] = jnp.broadcast_to(part, loss_ref.shape)

        @pl.when(i > 0)
        def _():
            dg_ref[...] += dg
            loss_ref[...] += jnp.broadcast_to(part, loss_ref.shape)

    row = pl.BlockSpec((bm, D), lambda i: (i, 0))
    vec = pl.BlockSpec((1, D), lambda i: (0, 0))
    return pl.pallas_call(
        body, name=name, grid=(T // bm,),
        in_specs=[row, vec, row], out_specs=[row, row, vec, pl.BlockSpec((SUBLANES, LANES), lambda i: (0, 0))],
        out_shape=[SDS((T, D), F32), SDS((T, D), BF16), SDS((1, D), F32), SDS((SUBLANES, LANES), F32)],
        compiler_params=_cp(("arbitrary",), 40),
    )(x3, gain, tgt)


R_CW, R_CB, R_BA, R_BI, R_LAM, R_SW, R_GLO, R_GSO, SMALL_ROWS = 0, 4, 5, 6, 7, 8, 11, 12, 16


def _rows(g):
    return pl.ds(pl.multiple_of(g * SUBLANES, SUBLANES), SUBLANES)


def _shift_back(prev, cur, d):
    row = lax.broadcasted_iota(jnp.int32, cur.shape, 0)
    return pltpu.roll(jnp.where(row >= SUBLANES - d, prev, cur), d, 0)


def _shift_fwd(cur, nxt, d):
    row = lax.broadcasted_iota(jnp.int32, cur.shape, 0)
    return pltpu.roll(jnp.where(row < d, nxt, cur), SUBLANES - d, 0)


def _causal_conv(ext, g, taps_ref, ntap):
    prev = ext[_rows(g), :]
    cur = ext[_rows(g + 1), :]
    out = _shift_back(prev, cur, ntap - 1) * taps_ref[0:1, :]
    for k in range(1, ntap - 1):
        out = out + _shift_back(prev, cur, ntap - 1 - k) * taps_ref[k:k + 1, :]
    return out + cur * taps_ref[ntap - 1:ntap, :]


def _scan8(A, U, reverse):
    row = lax.broadcasted_iota(jnp.int32, A.shape, 0)
    for s in (1, 2, 4):
        if reverse:
            A_sh = pltpu.roll(A, SUBLANES - s, 0)
            U_sh = pltpu.roll(U, SUBLANES - s, 0)
            m = row < SUBLANES - s
        else:
            A_sh = pltpu.roll(A, s, 0)
            U_sh = pltpu.roll(U, s, 0)
            m = row >= s
        U = jnp.where(m, A * U_sh + U, U)
        A = jnp.where(m, A * A_sh, A)
    return A, U


def _gate_pre(xc_s, w_ref, out_s, H, hd):
    for h in range(H):
        cs = slice(h * hd, (h + 1) * hd)
        out_s[:, cs] = _dot(xc_s[:, cs].astype(BF16), w_ref[h], NN)


def _lru_coeffs(pa, pi, xc, ba, bi, sp):
    ra = _sigmoid(pa + ba)
    ri = _sigmoid(pi + bi)
    log_a = (-LRU_C * ra) * sp
    a = jnp.exp(log_a)
    mult = jnp.sqrt(_one_minus_exp(2.0 * log_a, a))
    return ra, ri, a, mult


def _softplus_neg(lam):
    v = -lam
    return jnp.maximum(v, 0.0) + _log1p(jnp.exp(-jnp.abs(v)))


def _mix_fwd(z, cw, cb, wa, ba, wi, bi, lam, sw, glo, gso, name):
    T = z.shape[0]
    C = z.shape[1] // 5
    H = wa.shape[0]
    hd = C // H
    tb = _blk(T, 256, BF16_ROWS)
    ng = tb // SUBLANES
    HDR = SUBLANES

    def body(z_ref, cw_ref, cb_ref, wa_ref, ba_ref, wi_ref, bi_ref, lam_ref, sw_ref, glo_ref, gso_ref,
             y_ref, h_ref, ra_ref, ri_ref, a_ref, m_ref, xext, pext, xc_s, pa_s, pi_s, y_s, hcar):
        @pl.when(pl.program_id(0) == 0)
        def _():
            xext[0:HDR, :] = jnp.zeros((HDR, C), F32)
            pext[0:HDR, :] = jnp.zeros((HDR, C), F32)
            hcar[...] = jnp.zeros_like(hcar)

        def fill(g, _):
            r = _rows(g)
            re = _rows(g + 1)
            xext[re, :] = z_ref[r, 0:C]
            pext[re, :] = z_ref[r, 3 * C:4 * C] * z_ref[r, 4 * C:5 * C]
            return 0
        lax.fori_loop(0, ng, fill, 0)

        def conv(g, _):
            xc_s[_rows(g), :] = _causal_conv(xext, g, cw_ref, 4) + cb_ref[...]
            return 0
        lax.fori_loop(0, ng, conv, 0)

        _gate_pre(xc_s, wa_ref, pa_s, H, hd)
        _gate_pre(xc_s, wi_ref, pi_s, H, hd)
        sp = _softplus_neg(lam_ref[...])

        def group(g, hprev):
            r = _rows(g)
            xc = xc_s[r, :]
            ra, ri, a, mult = _lru_coeffs(pa_s[r, :], pi_s[r, :], xc, ba_ref[...], bi_ref[...], sp)
            ra_ref[r, :] = ra
            ri_ref[r, :] = ri
            a_ref[r, :] = a
            m_ref[r, :] = mult
            A, U = _scan8(a, mult * (ri * xc), reverse=False)
            hh = A * hprev + U
            h_ref[r, :] = hh
            gel, _ = _gelu_parts(z_ref[r, C:2 * C])
            y_lru = hh * gel
            y_s[r, 0:C] = y_lru * _rstd(y_lru) * glo_ref[...]
            y_sc = z_ref[r, 2 * C:3 * C] * _causal_conv(pext, g, sw_ref, 3)
            y_s[r, C:2 * C] = y_sc * _rstd(y_sc) * gso_ref[...]
            return jnp.broadcast_to(hh[SUBLANES - 1:SUBLANES, :], hh.shape)
        hcar[...] = lax.fori_loop(0, ng // 2, lambda t, hp: group(2 * t + 1, group(2 * t, hp)), hcar[...])

        xext[0:HDR, :] = xext[tb:tb + HDR, :]
        pext[0:HDR, :] = pext[tb:tb + HDR, :]

        def cast(g, _):
            r = pl.ds(pl.multiple_of(g * BF16_ROWS, BF16_ROWS), BF16_ROWS)
            y_ref[r, :] = y_s[r, :].astype(BF16)
            return 0
        lax.fori_loop(0, tb // BF16_ROWS, cast, 0)

    full = lambda shape: pl.BlockSpec(shape, lambda i: (0,) * len(shape))
    blk = lambda w: pl.BlockSpec((tb, w), lambda i: (i, 0))
    ext = pltpu.VMEM((tb + HDR, C), F32)
    tile = pltpu.VMEM((tb, C), F32)
    return pl.pallas_call(
        body, name=name, grid=(T // tb,),
        in_specs=[blk(5 * C), full((4, C)), full((1, C)), full((H, hd, hd)), full((1, C)), full((H, hd, hd)),
                  full((1, C)), full((1, C)), full((3, C)), full((1, C)), full((1, C))],
        out_specs=[blk(2 * C)] + [blk(C)] * 5,
        out_shape=[SDS((T, 2 * C), BF16)] + [SDS((T, C), F32)] * 5,
        scratch_shapes=[ext, ext, tile, tile, tile, pltpu.VMEM((tb, 2 * C), F32), pltpu.VMEM((SUBLANES, C), F32)],
        compiler_params=_cp(("arbitrary",), 48),
    )(z, cw, cb, wa, ba, wi, bi, lam, sw, glo, gso)


def _mix_bwd(z, h, gates, dy, cw, cb, wa, wi, lam, sw, glo, gso, name):
    T = z.shape[0]
    C = z.shape[1] // 5
    H = wa.shape[0]
    hd = C // H
    tb = _blk(T, 256, BF16_ROWS)
    nb = T // tb
    ng = tb // SUBLANES
    HDR = SUBLANES
    N_ACC = 13

    def body(z_ref, zp_ref, h_ref, hp_ref, ra_ref, ri_ref, a_ref, m_ref, dy_ref, cw_ref, cb_ref, wa_ref, wi_ref,
             lam_ref, sw_ref, glo_ref, gso_ref, dz_ref, small_ref, dwa_ref, dwi_ref,
             xext, pext, hext, dqext, dxcext, bext, xc_s, dh_s, dpa_s, dpi_s, dz_s, acc_s, bcar):
        i = pl.program_id(0)
        first_rows = i == nb - 1

        @pl.when(i == 0)
        def _():
            dqext[tb:tb + HDR, :] = jnp.zeros((HDR, C), F32)
            dxcext[tb:tb + HDR, :] = jnp.zeros((HDR, C), F32)
            bcar[...] = jnp.zeros_like(bcar)
            acc_s[...] = jnp.zeros_like(acc_s)
            dwa_ref[...] = jnp.zeros_like(dwa_ref)
            dwi_ref[...] = jnp.zeros_like(dwi_ref)

        zero = jnp.zeros((HDR, C), F32)
        xext[0:HDR, :] = jnp.where(first_rows, zero, zp_ref[:, 0:C])
        pext[0:HDR, :] = jnp.where(first_rows, zero, zp_ref[:, 3 * C:4 * C] * zp_ref[:, 4 * C:5 * C])
        hext[0:HDR, :] = jnp.where(first_rows, zero, hp_ref[...])

        def fill(g, _):
            r = _rows(g)
            re = _rows(g + 1)
            xext[re, :] = z_ref[r, 0:C]
            pext[re, :] = z_ref[r, 3 * C:4 * C] * z_ref[r, 4 * C:5 * C]
            hext[re, :] = h_ref[r, :]
            return 0
        lax.fori_loop(0, ng, fill, 0)

        def conv(g, _):
            xc_s[_rows(g), :] = _causal_conv(xext, g, cw_ref, 4) + cb_ref[...]
            return 0
        lax.fori_loop(0, ng, conv, 0)

        sp = _softplus_neg(lam_ref[...])
        dsp_dlam = -jax.nn.sigmoid(-lam_ref[...])

        def add_acc(k, v):
            acc_s[k] += v

        def p1(g, _):
            r = _rows(g)
            hh = h_ref[r, :]
            gel, gelp = _gelu_parts(z_ref[r, C:2 * C])
            y_lru = hh * gel
            dnl = dy_ref[r, 0:C]
            rl = _rstd(y_lru)
            ylr = y_lru * rl
            gd = glo_ref[...] * dnl
            dy_lru = rl * (gd - ylr * jnp.mean(gd * ylr, axis=-1, keepdims=True))
            add_acc(R_GLO, dnl * ylr)
            dz_s[r, C:2 * C] = dy_lru * hh * gelp
            dh = dy_lru * gel
            dh_s[r, :] = dh

            q = _causal_conv(pext, g, sw_ref, 3)
            scb = z_ref[r, 2 * C:3 * C]
            y_sc = scb * q
            dns = dy_ref[r, C:2 * C]
            rs = _rstd(y_sc)
            ysr = y_sc * rs
            gs = gso_ref[...] * dns
            dy_sc = rs * (gs - ysr * jnp.mean(gs * ysr, axis=-1, keepdims=True))
            add_acc(R_GSO, dns * ysr)
            dz_s[r, 2 * C:3 * C] = dy_sc * q
            dqext[r, :] = dy_sc * scb
            return 0
        lax.fori_loop(0, ng, p1, 0, unroll=2)

        bext[tb:tb + HDR, :] = bcar[...]

        def p2(j, carry):
            g = ng - 1 - j
            r = _rows(g)
            a = a_ref[r, :]
            A, U = _scan8(a, a * dh_s[r, :], reverse=True)
            bb = A * carry + U
            bext[r, :] = bb
            return jnp.broadcast_to(bb[0:1, :], bb.shape)
        bcar[...] = lax.fori_loop(0, ng, p2, bcar[...])

        def p3(g, _):
            r = _rows(g)
            rn = _rows(g + 1)
            G = dh_s[r, :] + _shift_fwd(bext[r, :], bext[rn, :], 1)
            hm1 = _shift_back(hext[r, :], hext[rn, :], 1)
            a = a_ref[r, :]
            mult = m_ref[r, :]
            ri = ri_ref[r, :]
            xc = xc_s[r, :]
            ra = ra_ref[r, :]
            dxcext[r, :] = G * mult * ri
            dri = G * mult * xc
            dmult = G * ri * xc
            dlog_a = (G * hm1) * a - dmult * (a * a) / mult
            add_acc(R_LAM, dlog_a * (-LRU_C * ra) * dsp_dlam)
            dpa = dlog_a * (-LRU_C * sp) * ra * (1.0 - ra)
            dpi = dri * ri * (1.0 - ri)
            add_acc(R_BA, dpa)
            add_acc(R_BI, dpi)
            dpa_s[r, :] = dpa
            dpi_s[r, :] = dpi
            return 0
        lax.fori_loop(0, ng, p3, 0)

        for hh_ in range(H):
            cs = slice(hh_ * hd, (hh_ + 1) * hd)
            dpa_b = dpa_s[:, cs].astype(BF16)
            dpi_b = dpi_s[:, cs].astype(BF16)
            xc_b = xc_s[:, cs].astype(BF16)
            dxcext[0:tb, cs] += _dot(dpa_b, wa_ref[hh_], NT) + _dot(dpi_b, wi_ref[hh_], NT)
            dwa_ref[hh_] += _dot(xc_b, dpa_b, TN)
            dwi_ref[hh_] += _dot(xc_b, dpi_b, TN)

        def p4(g, _):
            r = _rows(g)
            rn = _rows(g + 1)
            dxc = dxcext[r, :]
            dxc_n = dxcext[rn, :]
            x_p = xext[r, :]
            x_c = xext[rn, :]
            add_acc(R_CB, dxc)
            dlx = dxc * cw_ref[3:4, :]
            add_acc(R_CW + 3, dxc * x_c)
            for d in range(1, 4):
                dlx = dlx + _shift_fwd(dxc, dxc_n, d) * cw_ref[3 - d:4 - d, :]
                add_acc(R_CW + 3 - d, dxc * _shift_back(x_p, x_c, d))
            dz_s[r, 0:C] = dlx

            dq = dqext[r, :]
            dq_n = dqext[rn, :]
            p_p = pext[r, :]
            p_c = pext[rn, :]
            dp = dq * sw_ref[2:3, :]
            add_acc(R_SW + 2, dq * p_c)
            for d in range(1, 3):
                dp = dp + _shift_fwd(dq, dq_n, d) * sw_ref[2 - d:3 - d, :]
                add_acc(R_SW + 2 - d, dq * _shift_back(p_p, p_c, d))
            dz_s[r, 3 * C:4 * C] = dp * z_ref[r, 4 * C:5 * C]
            dz_s[r, 4 * C:5 * C] = dp * z_ref[r, 3 * C:4 * C]
            return 0
        lax.fori_loop(0, ng, p4, 0)

        dqext[tb:tb + HDR, :] = dqext[0:HDR, :]
        dxcext[tb:tb + HDR, :] = dxcext[0:HDR, :]

        def cast(g, _):
            r = pl.ds(pl.multiple_of(g * BF16_ROWS, BF16_ROWS), BF16_ROWS)
            dz_ref[r, :] = dz_s[r, :].astype(BF16)
            return 0
        lax.fori_loop(0, tb // BF16_ROWS, cast, 0)

        @pl.when(i == nb - 1)
        def _():
            small_ref[...] = jnp.zeros_like(small_ref)
            for k in range(N_ACC):
                small_ref[k:k + 1, :] = jnp.sum(acc_s[k], axis=0, keepdims=True)

    tpg = tb // SUBLANES
    full = lambda shape: pl.BlockSpec(shape, lambda i: (0,) * len(shape))
    blk = lambda w: pl.BlockSpec((tb, w), lambda i: (nb - 1 - i, 0))
    prev = lambda w: pl.BlockSpec((SUBLANES, w), lambda i: (jnp.maximum((nb - 1 - i) * tpg - 1, 0), 0))
    ext = pltpu.VMEM((tb + HDR, C), F32)
    tile = pltpu.VMEM((tb, C), F32)
    return pl.pallas_call(
        body, name=name, grid=(nb,),
        in_specs=[blk(5 * C), prev(5 * C), blk(C), prev(C)] + [blk(C)] * 4
        + [blk(2 * C), full((4, C)), full((1, C)), full((H, hd, hd)), full((H, hd, hd)), full((1, C)), full((3, C)),
           full((1, C)), full((1, C))],
        out_specs=[blk(5 * C), full((SMALL_ROWS, C)), full((H, hd, hd)), full((H, hd, hd))],
        out_shape=[SDS((T, 5 * C), BF16), SDS((SMALL_ROWS, C), F32), SDS((H, hd, hd), F32), SDS((H, hd, hd), F32)],
        scratch_shapes=[ext] * 6 + [tile] * 4 + [pltpu.VMEM((tb, 5 * C), F32), pltpu.VMEM((N_ACC, SUBLANES, C), F32),
                                                pltpu.VMEM((SUBLANES, C), F32)],
        compiler_params=_cp(("arbitrary",), 56),
    )(z, z, h, h, *gates, dy, cw, cb, wa, wi, lam, sw, glo, gso)


def _add_slabs(terms, out_dtype, name):
    R, Ccols = terms[0].shape
    br = _blk(R, 512, BF16_ROWS)
    n = len(terms)

    def body(*refs):
        s = refs[0][...].astype(F32)
        for t_ref in refs[1:n]:
            s = s + t_ref[...].astype(F32)
        refs[n][...] = s.astype(out_dtype)

    spec = pl.BlockSpec((br, Ccols), lambda i: (i, 0))
    return pl.pallas_call(
        body, name=name, grid=(R // br,), in_specs=[spec] * n, out_specs=spec, out_shape=SDS((R, Ccols), out_dtype),
        compiler_params=_cp(("arbitrary",), 40),
    )(*terms)


def _final_grad(sb, lb, chip, name):
    _, R, Ccols = sb.shape
    br = _blk(R, 512, BF16_ROWS)

    def body(chip_ref, sb_ref, l0, l1, l2, o_ref):
        s = sb_ref[0].astype(F32)
        for t_ref in (l0, l1, l2):
            s = s + t_ref[0].astype(F32)
        o_ref[...] = s

    lspec = lambda k: pl.BlockSpec((1, br, Ccols), lambda i, c: (k, i, 0))
    return pl.pallas_call(
        body, name=name,
        grid_spec=pltpu.PrefetchScalarGridSpec(
            num_scalar_prefetch=1, grid=(R // br,),
            in_specs=[pl.BlockSpec((1, br, Ccols), lambda i, c: (c[0], i, 0)), lspec(0), lspec(1), lspec(2)],
            out_specs=pl.BlockSpec((br, Ccols), lambda i, c: (i, 0))),
        out_shape=SDS((R, Ccols), F32),
        compiler_params=_cp(("arbitrary",), 40),
    )(chip, sb, lb, lb, lb)


def _adamw(w, g, m, v, name):
    R, Ccols = w.shape
    br = _blk(R, 256, SUBLANES)
    c1 = 1.0 - ADAM_B1 ** ADAM_STEP
    c2 = 1.0 - ADAM_B2 ** ADAM_STEP

    def body(w_ref, g_ref, m_ref, v_ref, d_ref, nm_ref, nv_ref):
        gv = g_ref[...]
        nm = ADAM_B1 * m_ref[...] + (1.0 - ADAM_B1) * gv
        nv = ADAM_B2 * v_ref[...] + (1.0 - ADAM_B2) * (gv * gv)
        nm_ref[...] = nm
        nv_ref[...] = nv
        d_ref[...] = -ADAM_LR * ((nm / c1) / (jnp.sqrt(nv / c2) + ADAM_EPS) + ADAM_WD * w_ref[...])

    spec = pl.BlockSpec((br, Ccols), lambda i: (i, 0))
    return pl.pallas_call(
        body, name=name, grid=(R // br,), in_specs=[spec] * 4, out_specs=[spec] * 3,
        out_shape=[SDS((R, Ccols), F32)] * 3, compiler_params=_cp(("arbitrary",), 40),
    )(w, g, m, v)


def _place():
    return lax.axis_index("x"), lax.axis_index("y"), lax.axis_index("c")


def _dev_rows(ref, dev, rows):
    return ref.at[pl.ds((4 * dev[0] + 2 * dev[1] + dev[2]) * rows, rows), :]


def _remote(src, dst, send_sem, recv_sem, to):
    return pltpu.make_async_remote_copy(src_ref=src, dst_ref=dst, send_sem=send_sem, recv_sem=recv_sem,
                                        device_id=to, device_id_type=MESH)


SAME_CORE_AND_SIBLING = ((0, 0, 1), (1, 0, 0), (0, 1, 0), (1, 1, 0))


def _merge_phases(a, b):
    na_in, na_out, na_sem = len(a.inputs), len(a.out_shapes), len(a.sem_shapes)

    def build(ins, outs, sems, stage):
        return (a.build(ins[:na_in], outs[:na_out], sems[:na_sem], stage)
                + b.build(ins[na_in:], outs[na_out:], sems[na_sem:], stage))

    aliases = dict(a.aliases)
    aliases.update({na_in + i: na_out + o for i, o in b.aliases.items()})
    return _Carried(a.inputs + b.inputs, a.out_shapes + b.out_shapes, aliases, a.sem_shapes + b.sem_shapes, build,
                    has_mid=a.has_mid or b.has_mid)


def _ag_direct_phase(slab, pieces, flips):
    W = slab.shape[1]
    n = len(pieces)
    npeer = len(flips)

    def build(ins, outs, sems, stage):
        if stage == "mid":
            return []
        starting = stage == "start"
        (slab_ref,) = ins
        send_sems, recv_sems, local_sems = sems
        x, y, c = _place()
        me = (x, y, c)
        peers = [tuple(1 - v if f else v for v, f in zip(me, flip)) for flip in flips]
        todo = []
        for p, (off, rows) in enumerate(pieces):
            src = slab_ref.at[pl.ds(off, rows), :]
            mine = pltpu.make_async_copy(src, _dev_rows(outs[p], me, rows), local_sems.at[p])
            todo.append(mine.start if starting else mine.wait)
            for k, peer in enumerate(peers):
                snd = _remote(src, _dev_rows(outs[p], me, rows), send_sems.at[k, p], recv_sems.at[k, p], peer)
                if starting:
                    todo.append(snd.start)
                else:
                    theirs = _dev_rows(outs[p], peer, rows)
                    rcv = _remote(theirs, theirs, send_sems.at[k, p], recv_sems.at[k, p], me)
                    todo += [rcv.wait_recv, snd.wait_send]
        return todo

    dma = pltpu.SemaphoreType.DMA
    return _Carried([slab], [SDS((N_DEV * rows, W), slab.dtype) for _, rows in pieces], {},
                    [dma((npeer, n)), dma((npeer, n)), dma((n,))], build)


def _ag_two_level_phase(slab, pieces):
    W = slab.shape[1]
    n = len(pieces)

    def build(ins, outs, sems, stage):
        (slab_ref,) = ins
        send_sems, recv_sems, local_sems = sems
        x, y, c = _place()
        me, sibling = (x, y, c), (x, y, 1 - c)
        chips = [(1 - x, y), (x, 1 - y), (1 - x, 1 - y)]
        todo = []
        for p, (off, rows) in enumerate(pieces):
            src = slab_ref.at[pl.ds(off, rows), :]
            own = _dev_rows(outs[p], me, rows)
            landed = [_dev_rows(outs[p], (*chip, c), rows) for chip in chips]

            def mine():
                return pltpu.make_async_copy(src, own, local_sems.at[p])

            def first():
                return [_remote(src, own, send_sems.at[k, p], recv_sems.at[k, p], to)
                        for k, to in enumerate([sibling] + [(*chip, c) for chip in chips])]

            def passed():
                return [_remote(blk, blk, send_sems.at[4 + j, p], recv_sems.at[4 + j, p], sibling)
                        for j, blk in enumerate(landed)]

            def arrival(k, blk):
                return _remote(blk, blk, send_sems.at[k, p], recv_sems.at[k, p], me).wait_recv

            if stage == "start":
                todo += [mine().start] + [cp.start for cp in first()]
            elif stage == "mid":
                for j, (blk, fwd) in enumerate(zip(landed, passed())):
                    todo += [arrival(1 + j, blk), fwd.start]
            else:
                theirs = [_dev_rows(outs[p], sibling, rows)] + [_dev_rows(outs[p], (*chip, 1 - c), rows) for chip in chips]
                todo += [arrival(k, blk) for k, blk in zip((0, 4, 5, 6), theirs)]
                todo += [cp.wait_send for cp in first() + passed()] + [mine().wait]
        return todo

    dma = pltpu.SemaphoreType.DMA
    return _Carried([slab], [SDS((N_DEV * rows, W), slab.dtype) for _, rows in pieces], {},
                    [dma((7, n)), dma((7, n)), dma((n,))], build, has_mid=True)


def _ag_forward_phase(gathered, pieces):
    n = len(pieces)

    def build(ins, outs, sems, stage):
        if stage == "mid":
            return []
        starting = stage == "start"
        send_sems, recv_sems = sems
        x, y, c = _place()
        me, sibling = (x, y, c), (x, y, 1 - c)
        chips = [(1 - x, y), (x, 1 - y), (1 - x, 1 - y)]
        todo = []
        for p, (_, rows) in enumerate(pieces):
            for j, chip in enumerate(chips):
                snd = _remote(_dev_rows(ins[p], (*chip, c), rows), _dev_rows(outs[p], (*chip, c), rows),
                              send_sems.at[j, p], recv_sems.at[j, p], sibling)
                if starting:
                    todo.append(snd.start)
                else:
                    theirs = _dev_rows(outs[p], (*chip, 1 - c), rows)
                    rcv = _remote(theirs, theirs, send_sems.at[j, p], recv_sems.at[j, p], me)
                    todo += [rcv.wait_recv, snd.wait_send]
        return todo

    dma = pltpu.SemaphoreType.DMA
    return _Carried(gathered, [SDS(g.shape, g.dtype) for g in gathered], {p: p for p in range(n)},
                    [dma((3, n)), dma((3, n))], build)


def _rs_chips_phase(sb):
    _, R, W = sb.shape

    def build(ins, outs, sems, stage):
        if stage == "mid":
            return []
        (sb_ref,), (land_ref,) = ins, outs
        send_sems, recv_sems = sems
        x, y, c = _place()
        chips = [(1 - x, y), (x, 1 - y), (1 - x, 1 - y)]
        cps = [_remote(sb_ref.at[2 * chip[0] + chip[1]], land_ref.at[j], send_sems.at[j], recv_sems.at[j], (*chip, c))
               for j, chip in enumerate(chips)]
        if stage == "start":
            return [cp.start for cp in cps]
        return [cp.wait_recv for cp in cps] + [cp.wait_send for cp in cps]

    dma = pltpu.SemaphoreType.DMA
    return _Carried([sb], [SDS((3, R, W), sb.dtype)], {}, [dma((3,)), dma((3,))], build)


def _allgather(slab, pieces, name):
    R, W = slab.shape
    n = len(pieces)
    assert sum(rows for _, rows in pieces) == R

    def body(slab_ref, *refs):
        outs = refs[:n]
        send_sems, recv_sems, local_sems = refs[n:]
        x, y, c = _place()
        me, sibling = (x, y, c), (x, y, 1 - c)
        chips = [(1 - x, y), (x, 1 - y), (1 - x, 1 - y)]

        def dst_rows(p, origin):
            rows = pieces[p][1]
            start = (4 * origin[0] + 2 * origin[1] + origin[2]) * rows
            return outs[p].at[pl.ds(start, rows), :]

        def copies(k, origin, to, from_slab):
            out = []
            for p, (off, rows) in enumerate(pieces):
                dst = dst_rows(p, origin)
                src = slab_ref.at[pl.ds(off, rows), :] if from_slab else dst
                out.append(pltpu.make_async_remote_copy(
                    src_ref=src, dst_ref=dst, send_sem=send_sems.at[k, p], recv_sem=recv_sems.at[k, p],
                    device_id=to, device_id_type=MESH))
            return out

        mine = [pltpu.make_async_copy(slab_ref.at[pl.ds(off, rows), :], dst_rows(p, me), local_sems.at[p])
                for p, (off, rows) in enumerate(pieces)]
        for cp in mine:
            cp.start()
        first = copies(0, me, sibling, True)
        for j, chip in enumerate(chips):
            first += copies(1 + j, me, (*chip, c), True)
        for cp in first:
            cp.start()
        passed = []
        for j, chip in enumerate(chips):
            for cp in copies(1 + j, (*chip, c), me, False):
                cp.wait_recv()
            fwd = copies(4 + j, (*chip, c), sibling, False)
            for cp in fwd:
                cp.start()
            passed += fwd
        for cp in copies(0, sibling, me, False):
            cp.wait_recv()
        for j, chip in enumerate(chips):
            for cp in copies(4 + j, (*chip, 1 - c), me, False):
                cp.wait_recv()
        for cp in first + passed:
            cp.wait_send()
        for cp in mine:
            cp.wait()

    return pl.pallas_call(
        body, name=name,
        in_specs=[HBM_SPEC], out_specs=[HBM_SPEC] * n,
        out_shape=[SDS((N_DEV * rows, W), slab.dtype) for _, rows in pieces],
        scratch_shapes=[pltpu.SemaphoreType.DMA((7, n)), pltpu.SemaphoreType.DMA((7, n)), pltpu.SemaphoreType.DMA((n,))],
    )(slab)


def _rs_sibling(grads, pieces, name):
    W = grads[0].shape[1]
    R = sum(rows for _, rows in pieces)
    n = len(pieces)
    dt = grads[0].dtype
    max_rows = max(rows for _, rows in pieces)
    steps = [(q, p) for q in range(N_CHIP) for p in range(n)]
    ns = len(steps)
    ADD_ROWS = 64
    SLOTS = 3
    assert all(rows % ADD_ROWS == 0 for _, rows in pieces)

    def body(*refs):
        g_refs = refs[:n]
        sb_ref, mine_buf, send_buf, land_buf, out_buf, in_sems, out_sems, send_sems, recv_sems, credit = refs[n:]
        x, y, c = _place()
        sibling = (x, y, 1 - c)

        def loads(s):
            q, p = steps[s]
            rows = pieces[p][1]
            slot = s % SLOTS
            mine = g_refs[p].at[pl.ds((2 * q + c) * rows, rows), :]
            theirs = g_refs[p].at[pl.ds((2 * q + 1 - c) * rows, rows), :]
            return (pltpu.make_async_copy(mine, mine_buf.at[slot, pl.ds(0, rows), :], in_sems.at[slot, 0]),
                    pltpu.make_async_copy(theirs, send_buf.at[slot, pl.ds(0, rows), :], in_sems.at[slot, 1]))

        def send(s):
            rows = pieces[steps[s][1]][1]
            slot = s % SLOTS
            return pltpu.make_async_remote_copy(
                src_ref=send_buf.at[slot, pl.ds(0, rows), :], dst_ref=land_buf.at[slot, pl.ds(0, rows), :],
                send_sem=send_sems.at[slot], recv_sem=recv_sems.at[slot], device_id=sibling, device_id_type=MESH)

        def store(s):
            q, p = steps[s]
            off, rows = pieces[p]
            slot = s % SLOTS
            return pltpu.make_async_copy(out_buf.at[slot, pl.ds(0, rows), :], sb_ref.at[q, pl.ds(off, rows), :],
                                         out_sems.at[slot])

        def start_send(s):
            for cp in loads(s):
                cp.wait()
            if s >= SLOTS:
                pl.semaphore_wait(credit.at[s % SLOTS], 1)
            send(s).start()

        for s in range(min(SLOTS, ns)):
            for cp in loads(s):
                cp.start()
        for s in range(min(SLOTS - 1, ns)):
            start_send(s)
        for s in range(ns):
            slot = s % SLOTS
            rows = pieces[steps[s][1]][1]
            if s + SLOTS - 1 < ns:
                start_send(s + SLOTS - 1)
            send(s).wait_recv()
            if s >= SLOTS:
                store(s - SLOTS).wait()

            def add(k, _, slot=slot):
                r = pl.ds(pl.multiple_of(k * ADD_ROWS, ADD_ROWS), ADD_ROWS)
                out_buf[slot, r, :] = (mine_buf[slot, r, :].astype(F32) + land_buf[slot, r, :].astype(F32)).astype(dt)
                return 0
            lax.fori_loop(0, rows // ADD_ROWS, add, 0)
            if s + SLOTS < ns:
                pl.semaphore_signal(credit.at[slot], inc=1, device_id=sibling, device_id_type=MESH)
            store(s).start()
            send(s).wait_send()
            if s + SLOTS < ns:
                for cp in loads(s + SLOTS):
                    cp.start()
        for s in range(max(ns - SLOTS, 0), ns):
            store(s).wait()

    buf = pltpu.VMEM((SLOTS, max_rows, W), dt)
    return pl.pallas_call(
        body, name=name,
        in_specs=[HBM_SPEC] * n, out_specs=HBM_SPEC,
        out_shape=SDS((N_CHIP, R, W), dt),
        scratch_shapes=[buf, buf, buf, buf, pltpu.SemaphoreType.DMA((SLOTS, 2)), pltpu.SemaphoreType.DMA((SLOTS,)),
                        pltpu.SemaphoreType.DMA((SLOTS,)), pltpu.SemaphoreType.DMA((SLOTS,)),
                        pltpu.SemaphoreType.REGULAR((SLOTS,))],
        compiler_params=pltpu.CompilerParams(vmem_limit_bytes=48 * MIB),
    )(*grads)


SMALL_NAMES = ("ffn1_norm", "mix_norm", "ffn2_norm", "final_norm", "lru_conv_w", "lru_conv_b", "lru_w_a", "lru_b_a",
               "lru_w_i", "lru_b_i", "lru_lambda", "sc_conv_w", "lru_out_norm", "sc_out_norm")
WEIGHT_NAMES = ("ffn1_norm", "ffn1_w_gate", "ffn1_w_up", "ffn1_w_down", "mix_norm", "w_in", "lru_conv_w", "lru_conv_b",
                "lru_w_a", "lru_b_a", "lru_w_i", "lru_b_i", "lru_lambda", "sc_conv_w", "lru_out_norm", "sc_out_norm",
                "w_out", "ffn2_norm", "ffn2_w_gate", "ffn2_w_up", "ffn2_w_down", "final_norm")
BIG = (("ffn1_w_gate", True), ("ffn1_w_up", True), ("ffn1_w_down", False), ("ffn2_w_gate", True), ("ffn2_w_up", True),
       ("ffn2_w_down", False), ("w_in", True), ("w_out", False))


SLAB_ROW_ALIGN = 256


def _pack_rows(parts, width):
    rows, counts = [], []
    for p in parts:
        flat = p.reshape(-1)
        nr = -(-flat.shape[0] // width)
        nr = -(-nr // SUBLANES) * SUBLANES
        rows.append(jnp.pad(flat, (0, nr * width - flat.shape[0])).reshape(nr, width))
        counts.append(nr)
    total = sum(counts)
    pad = -(-total // SLAB_ROW_ALIGN) * SLAB_ROW_ALIGN - total
    if pad:
        rows.append(jnp.zeros((pad, width), rows[0].dtype))
    return jnp.concatenate(rows, axis=0), counts


def _stack_rows(blocks):
    pieces, off = [], 0
    for b in blocks:
        pieces.append((off, b.shape[0]))
        off += b.shape[0]
    return jnp.concatenate(blocks, axis=0), pieces


def _unpack_rows(slab, counts, shapes):
    out, r = [], 0
    for nr, shape in zip(counts, shapes):
        size = math.prod(shape)
        out.append(slab[r:r + nr].reshape(-1)[:size].reshape(shape))
        r += nr
    return out


def kernel(x, ffn1_norm, ffn1_w_gate, ffn1_w_up, ffn1_w_down, mix_norm, w_in, lru_conv_w, lru_conv_b, lru_w_a, lru_b_a, lru_w_i, lru_b_i, lru_lambda, sc_conv_w, lru_out_norm, sc_out_norm, w_out, ffn2_norm, ffn2_w_gate, ffn2_w_up, ffn2_w_down, final_norm, loss_target, m_ffn1_norm, m_ffn1_w_gate, m_ffn1_w_up, m_ffn1_w_down, m_mix_norm, m_w_in, m_lru_conv_w, m_lru_conv_b, m_lru_w_a, m_lru_b_a, m_lru_w_i, m_lru_b_i, m_lru_lambda, m_sc_conv_w, m_lru_out_norm, m_sc_out_norm, m_w_out, m_ffn2_norm, m_ffn2_w_gate, m_ffn2_w_up, m_ffn2_w_down, m_final_norm, v_ffn1_norm, v_ffn1_w_gate, v_ffn1_w_up, v_ffn1_w_down, v_mix_norm, v_w_in, v_lru_conv_w, v_lru_conv_b, v_lru_w_a, v_lru_b_a, v_lru_w_i, v_lru_b_i, v_lru_lambda, v_sc_conv_w, v_lru_out_norm, v_sc_out_norm, v_w_out, v_ffn2_norm, v_ffn2_w_gate, v_ffn2_w_up, v_ffn2_w_down, v_final_norm):
    a = dict(locals())
    w = {n: a[n] for n in WEIGHT_NAMES}
    m = {n: a["m_" + n] for n in WEIGHT_NAMES}
    v = {n: a["v_" + n] for n in WEIGHT_NAMES}
    ax, ay, ac = _place()
    dev = 4 * ax + 2 * ay + ac
    chip = (2 * ax + ay).astype(jnp.int32).reshape(1)

    x0 = x[0]
    tgt = loss_target[0]
    T, D = x0.shape
    C = D // 2
    H, hd = lru_w_a.shape[1], lru_w_a.shape[2]
    CL = lru_conv_w.shape[2]

    shards = []
    for name, transposed in BIG:
        s = w[name][0]
        shards.append((s.T if transposed else s).astype(BF16))
    taps = jnp.concatenate([lru_conv_w[0], sc_conv_w[0], jnp.zeros((1, CL), F32)], axis=0)
    taps_row = lax.bitcast_convert_type(taps, BF16).reshape(1, -1)
    taps_blk = jnp.pad(taps_row, ((0, BF16_ROWS - 1), (0, D - taps_row.shape[1])))
    s_wg1, s_wu1, s_wd1, s_wg2, s_wu2, s_wd2, s_win, s_wout = shards
    slab_g1, pcs_g1 = _stack_rows([s_wg1])
    slab_u1, pcs_u1 = _stack_rows([s_wu1])
    slab_d1, pcs_d1 = _stack_rows([s_wd1])
    slab_mw, pcs_mw = _stack_rows([s_win, s_wout, taps_blk])
    slab_gu2, pcs_gu2 = _stack_rows([s_wg2, s_wu2])
    slab_d2, pcs_d2 = _stack_rows([s_wd2])
    (wg1,) = _allgather(slab_g1, pcs_g1, "allgather_ffn1_gate")

    g1, gm, g3 = ffn1_norm, mix_norm, ffn2_norm
    phase = _merge_phases(_ag_two_level_phase(slab_u1, pcs_u1), _ag_direct_phase(slab_d1, pcs_d1, SAME_CORE_AND_SIBLING))
    (n1, hg1), got = _norm_proj(x0, g1, [wg1], [BF16], False, "ffn1_gate", carried=phase)
    wu1, d1 = got[0], got[1:]
    phase = _merge_phases(_ag_forward_phase(d1, pcs_d1), _ag_direct_phase(slab_mw, pcs_mw, SAME_CORE_AND_SIBLING))
    (hu1, act1), got = _up_act(n1, wu1, hg1, "ffn1_up", carried=phase)
    wd1, mixw = got[0], got[1:]
    phase = _merge_phases(_ag_forward_phase(mixw, pcs_mw), _ag_direct_phase(slab_gu2, pcs_gu2, SAME_CORE_AND_SIBLING))
    x1, got = _mm_res(act1, wd1, x0, 0.5, "ffn1_down", carried=phase)
    (win, wout, taps_all), gu2 = got[:3], got[3:]
    phase = _merge_phases(_ag_forward_phase(gu2, pcs_gu2), _ag_direct_phase(slab_d2, pcs_d2, SAME_CORE_AND_SIBLING))
    (n2, z), got = _norm_proj(x1, gm, [win], [F32], False, "in_proj", carried=phase)
    (wg2, wu2), d2 = got[:2], got[2:]
    taps_all = taps_all.reshape(N_DEV, BF16_ROWS, D)[:, 0, :2 * SUBLANES * CL].reshape(N_DEV, SUBLANES, CL, 2)
    taps_all = lax.bitcast_convert_type(taps_all, F32)
    taps_all = taps_all.transpose(1, 0, 2).reshape(SUBLANES, N_DEV * CL)
    cw, sw = taps_all[0:4], taps_all[4:7]

    gf = final_norm.reshape(1, D)
    cb = lru_conv_b
    wa, wi = lru_w_a[0].astype(BF16), lru_w_i[0].astype(BF16)
    ba, bi = lru_b_a.reshape(1, C), lru_b_i.reshape(1, C)
    lam, glo, gso = lru_lambda, lru_out_norm, sc_out_norm

    y, h, *gates = _mix_fwd(z, cw, cb, wa, ba, wi, bi, lam, sw, glo, gso, "mix_fwd")
    x2, (wd2,) = _mm_res(y, wout, x1, 1.0, "out_proj", carried=_ag_forward_phase(d2, pcs_d2))
    n3, hg2, hu2, act2 = _norm_proj(x2, g3, [wg2, wu2], [BF16, BF16], True, "ffn2_up")
    x3 = _mm_res(act2, wd2, x2, 0.5, "ffn2_down")
    dx3, df2, d_gf, loss_blk = _loss_head(x3, gf, tgt, "loss_head")

    F = wd1.shape[0]
    bm_f = F // 4 if (F // 4) % LANES == 0 else 512

    def reduce_group(gs, tag):
        pcs, off = [], 0
        for g_ in gs:
            pcs.append((off, g_.shape[0] // N_DEV))
            off += g_.shape[0] // N_DEV
        sb_ = _rs_sibling(gs, pcs, "rs_sibling_add_" + tag)
        return sb_, pcs

    dhg2, dhu2 = _ffn_bwd_act(df2, wd2, hg2, hu2, "ffn2_bwd_act")
    d_wd2 = _dw_tn(act2, df2, bm_f, "ffn2_dw_down")
    d_wg2 = _dw_tn(dhg2, n3, bm_f, "ffn2_dw_gate")
    d_wu2 = _dw_tn(dhu2, n3, bm_f, "ffn2_dw_up")
    sb_f2, pcs_f2 = reduce_group([d_wg2, d_wu2, d_wd2], "ffn2")
    (dx2, dx2b, d_g3), (lb_f2,) = _mm_rmsbwd([(dhg2, wg2), (dhu2, wu2)], x2, g3, dx3, 1.0, "ffn2_bwd_in",
                                             carried=_rs_chips_phase(sb_f2))
    dy = _mm_nt(dx2b, wout, "out_proj_bwd")
    d_wout = _dw_tn(y, dx2b, 1024, "out_proj_dw")
    dz, small, d_wa, d_wi = _mix_bwd(z, h, gates, dy, cw, cb, wa, wi, lam, sw, glo, gso, "mix_bwd")
    d_win = _dw_tn(dz, n2, 1280, "in_proj_dw")
    sb_mx, pcs_mx = reduce_group([d_win, d_wout], "mix")
    (dx1, df1, d_gm), (lb_mx,) = _mm_rmsbwd([(dz, win)], x1, gm, dx2, 0.5, "in_proj_bwd",
                                            carried=_rs_chips_phase(sb_mx))
    dhg1, dhu1 = _ffn_bwd_act(df1, wd1, hg1, hu1, "ffn1_bwd_act")
    d_wd1 = _dw_tn(act1, df1, bm_f, "ffn1_dw_down")
    d_wg1 = _dw_tn(dhg1, n1, bm_f, "ffn1_dw_gate")
    d_wu1 = _dw_tn(dhu1, n1, bm_f, "ffn1_dw_up")
    sb_f1, pcs_f1 = reduce_group([d_wg1, d_wu1, d_wd1], "ffn1")
    (dx0, _, d_g1), (lb_f1,) = _mm_rmsbwd([(dhg1, wg1), (dhu1, wu1)], x0, g1, dx1, 1.0, "ffn1_bwd_in",
                                          carried=_rs_chips_phase(sb_f1))

    big_sum = {}
    for tag, names, sb_, lb_, pcs in (("ffn2", ("ffn2_w_gate", "ffn2_w_up", "ffn2_w_down"), sb_f2, lb_f2, pcs_f2),
                                      ("mix", ("w_in", "w_out"), sb_mx, lb_mx, pcs_mx),
                                      ("ffn1", ("ffn1_w_gate", "ffn1_w_up", "ffn1_w_down"), sb_f1, lb_f1, pcs_f1)):
        gsum = _final_grad(sb_, lb_, chip, "rs_final_sum_" + tag)
        for name, (off, rows) in zip(names, pcs):
            big_sum[name] = gsum[off:off + rows]

    small_parts = [d_g1, d_gm, d_g3, d_gf, small[R_CW:R_CW + 4], small[R_CB], d_wa, small[R_BA], d_wi, small[R_BI],
                   small[R_LAM], small[R_SW:R_SW + 3], small[R_GLO], small[R_GSO]]
    sslab, counts = _pack_rows(small_parts, LANES)
    RS = sslab.shape[0]
    (sg,) = _allgather(sslab, [(0, RS)], "allgather_small_grads")
    ssum = _add_slabs([sg[j * RS:(j + 1) * RS] for j in range(N_DEV)], F32, "small_grads_sum")
    full_shapes = [(1, D), (1, D), (1, D), (D,), (1, 4, C), (1, C), (1, H, hd, hd), (1, H, hd), (1, H, hd, hd), (1, H, hd),
                   (1, C), (1, 3, C), (1, C), (1, C)]
    small_full = dict(zip(SMALL_NAMES, _unpack_rows(ssum, counts, full_shapes)))

    grads = {}
    for name, transposed in BIG:
        gblk = big_sum[name]
        grads[name] = (gblk.T if transposed else gblk)[None]
    for name in SMALL_NAMES:
        gfull = small_full[name]
        if name in ("lru_conv_w", "sc_conv_w"):
            gfull = lax.dynamic_slice_in_dim(gfull, dev * CL, CL, axis=2)
        grads[name] = gfull

    delta, new_m, new_v = {}, {}, {}
    for name, transposed in BIG:
        flip = transposed and w[name].shape[2] % LANES != 0
        view = (lambda t: t[0].T) if flip else (lambda t: t[0])
        back = (lambda t: t.T[None]) if flip else (lambda t: t[None])
        gview = big_sum[name] if flip else grads[name][0]
        d_, m_, v_ = _adamw(view(w[name]), gview, view(m[name]), view(v[name]), "adamw_" + name)
        delta[name], new_m[name], new_v[name] = back(d_), back(m_), back(v_)
    packs = [_pack_rows([t[n_] for n_ in SMALL_NAMES], LANES) for t in (w, grads, m, v)]
    sd, sm, sv = _adamw(packs[0][0], packs[1][0], packs[2][0], packs[3][0], "adamw_small")
    shapes = [w[n_].shape for n_ in SMALL_NAMES]
    for tgt_dict, slab_ in ((delta, sd), (new_m, sm), (new_v, sv)):
        for n_, val in zip(SMALL_NAMES, _unpack_rows(slab_, packs[0][1], shapes)):
            tgt_dict[n_] = val

    loss = lax.psum(loss_blk[0, 0], ("x", "y", "c"))
    return (loss, dx0[None], *[grads[n_] for n_ in WEIGHT_NAMES], *[delta[n_] for n_ in WEIGHT_NAMES],
            *[new_m[n_] for n_ in WEIGHT_NAMES], *[new_v[n_] for n_ in WEIGHT_NAMES])
```

```python
import functools
import math

import jax
import jax.numpy as jnp
from jax import lax
from jax.experimental import pallas as pl
from jax.experimental.pallas import tpu as pltpu

F32 = jnp.float32
BF16 = jnp.bfloat16
SDS = jax.ShapeDtypeStruct
MESH = pl.DeviceIdType.MESH

NORM_EPS = 1e-6
LRU_C = 8.0
N_DEV = 8
N_CHIP = 4
ADAM_LR, ADAM_B1, ADAM_B2, ADAM_EPS, ADAM_WD, ADAM_STEP = 0.001, 0.9, 0.999, 1e-08, 0.01, 10

NN = (((1,), (0,)), ((), ()))
NT = (((1,), (1,)), ((), ()))
TN = (((0,), (0,)), ((), ()))

SUBLANES = 8
BF16_ROWS = 16
LANES = 128
MIB = 1 << 20


def _dot(a, b, dims):
    return lax.dot_general(a, b, dims, preferred_element_type=F32)


def _blk(n, pref, align):
    if n <= pref:
        return n
    b = (pref // align) * align
    while b >= align:
        if n % b == 0:
            return b
        b -= align
    raise ValueError(f"no block of {n} aligned to {align} under {pref}")


def _cp(sem, vmem_mib):
    return pltpu.CompilerParams(dimension_semantics=sem, vmem_limit_bytes=vmem_mib * MIB)


HBM_SPEC = pl.BlockSpec(memory_space=pltpu.HBM)
MID_EIGHTHS = 5

class _Carried:
    def __init__(self, inputs, out_shapes, aliases, sem_shapes, build, has_mid=False):
        self.inputs, self.out_shapes, self.aliases = list(inputs), list(out_shapes), dict(aliases)
        self.sem_shapes, self.build, self.has_mid = list(sem_shapes), build, has_mid


def _call(body, *, name, grid, in_specs, out_specs, out_shape, scratch_shapes, compiler_params, args, carried=None):
    if carried is None:
        return pl.pallas_call(body, name=name, grid=grid, in_specs=in_specs, out_specs=out_specs, out_shape=out_shape,
                              scratch_shapes=scratch_shapes, compiler_params=compiler_params)(*args)
    n_in, n_out, n_sc = len(in_specs), len(out_shape), len(scratch_shapes)
    c_in, c_out = len(carried.inputs), len(carried.out_shapes)

    def hosted(*refs):
        ins, refs = refs[:n_in], refs[n_in:]
        c_ins, refs = refs[:c_in], refs[c_in:]
        outs, refs = refs[:n_out], refs[n_out:]
        c_outs, refs = refs[:c_out], refs[c_out:]
        scratch, c_sems = refs[:n_sc], refs[n_sc:]
        first = functools.reduce(jnp.logical_and, [pl.program_id(a) == 0 for a in range(len(grid))])
        last = functools.reduce(jnp.logical_and, [pl.program_id(a) == g - 1 for a, g in enumerate(grid)])

        @pl.when(first)
        def _():
            for start in carried.build(c_ins, c_outs, c_sems, "start"):
                start()

        if carried.has_mid:
            mid = functools.reduce(jnp.logical_and, [pl.program_id(0) == (grid[0] * MID_EIGHTHS) // 8]
                                   + [pl.program_id(a) == 0 for a in range(1, len(grid))])

            @pl.when(mid)
            def _():
                for step in carried.build(c_ins, c_outs, c_sems, "mid"):
                    step()

        body(*ins, *outs, *scratch)

        @pl.when(last)
        def _():
            for wait in carried.build(c_ins, c_outs, c_sems, "end"):
                wait()

    out = pl.pallas_call(
        hosted, name=name, grid=grid, in_specs=list(in_specs) + [HBM_SPEC] * c_in,
        out_specs=list(out_specs) + [HBM_SPEC] * c_out, out_shape=list(out_shape) + carried.out_shapes,
        scratch_shapes=list(scratch_shapes) + carried.sem_shapes,
        input_output_aliases={n_in + a: n_out + b for a, b in carried.aliases.items()},
        compiler_params=compiler_params)(*args, *carried.inputs)
    return out[:n_out], out[n_out:]


ROW_CHUNK = 128


def _chunk_rows(c):
    return pl.ds(pl.multiple_of(c * ROW_CHUNK, ROW_CHUNK), ROW_CHUNK)


def _rstd(xv):
    return lax.rsqrt(jnp.mean(xv * xv, axis=-1, keepdims=True) + NORM_EPS)


def _rms_bwd(xv, g, dn):
    r = _rstd(xv)
    xr = xv * r
    gd = g * dn
    dx = r * (gd - xr * jnp.mean(gd * xr, axis=-1, keepdims=True))
    return dx, jnp.sum(dn * xr, axis=0, keepdims=True)


def _log1p(e):
    u = 1.0 + e
    return jnp.where(u == 1.0, e, jnp.log(u) * (e / (u - 1.0)))


def _one_minus_exp(v, exp_half_v):
    series = 1.0 / 5040.0
    for coeff in (1.0 / 720.0, 1.0 / 120.0, 1.0 / 24.0, 1.0 / 6.0, 0.5, 1.0):
        series = series * v + coeff
    return jnp.where(v > -0.5, -v * series, 1.0 - exp_half_v * exp_half_v)


def _sigmoid(v):
    return 0.5 * jnp.tanh(0.5 * v) + 0.5


def _gelu_parts(g):
    k0 = math.sqrt(2.0 / math.pi)
    g2 = g * g
    t = jnp.tanh(k0 * (g + 0.044715 * g * g2))
    gel = 0.5 * g * (1.0 + t)
    gelp = 0.5 * (1.0 + t) + 0.5 * g * (1.0 - t * t) * (k0 * (1.0 + 3.0 * 0.044715 * g2))
    return gel, gelp


def _norm_proj(x, gain, w_list, out_dtypes, swiglu, name, carried=None):
    T, D = x.shape
    N = w_list[0].shape[0]
    nw = len(w_list)
    bm = _blk(T, 1024, BF16_ROWS)
    bn = _blk(N, 512, LANES)

    def body(*refs):
        x_ref, g_ref = refs[:2]
        w_refs = refs[2:2 + nw]
        n_ref = refs[2 + nw]
        o_refs = refs[3 + nw:3 + 2 * nw]
        act_ref = refs[3 + 2 * nw] if swiglu else None
        n_sc = refs[-1]

        @pl.when(pl.program_id(1) == 0)
        def _():
            def chunk(c, _):
                r = _chunk_rows(c)
                xv = x_ref[r, :]
                nb = (xv * _rstd(xv) * g_ref[...]).astype(BF16)
                n_sc[r, :] = nb
                n_ref[r, :] = nb
                return 0
            lax.fori_loop(0, bm // ROW_CHUNK, chunk, 0)

        n = n_sc[...]
        outs = [_dot(n, w_ref[...], NT) for w_ref in w_refs]
        for o_ref, o in zip(o_refs, outs):
            o_ref[...] = o.astype(o_ref.dtype)
        if swiglu:
            hg, hu = outs
            act_ref[...] = (hg * _sigmoid(hg) * hu).astype(BF16)

    row = pl.BlockSpec((bm, D), lambda i, j: (i, 0))
    tile = pl.BlockSpec((bm, bn), lambda i, j: (i, j))
    n_extra = 1 if swiglu else 0
    return _call(
        body, name=name, grid=(T // bm, N // bn),
        in_specs=[row, pl.BlockSpec((1, D), lambda i, j: (0, 0))] + [pl.BlockSpec((bn, D), lambda i, j: (j, 0))] * nw,
        out_specs=[row] + [tile] * (nw + n_extra),
        out_shape=[SDS((T, D), BF16)] + [SDS((T, N), dt) for dt in out_dtypes] + [SDS((T, N), BF16)] * n_extra,
        scratch_shapes=[pltpu.VMEM((bm, D), BF16)],
        compiler_params=_cp(("arbitrary", "arbitrary"), 52),
        args=(x, gain, *w_list), carried=carried)


def _up_act(n, wu, hg, name, carried=None):
    T, D = n.shape
    F = wu.shape[0]
    bm = _blk(T, 1024, BF16_ROWS)
    bn = _blk(F, 512, LANES)

    def body(n_ref, wu_ref, hg_ref, hu_ref, act_ref):
        hu = _dot(n_ref[...], wu_ref[...], NT)
        hg = hg_ref[...].astype(F32)
        hu_ref[...] = hu.astype(BF16)
        act_ref[...] = (hg * _sigmoid(hg) * hu).astype(BF16)

    tile = pl.BlockSpec((bm, bn), lambda i, j: (i, j))
    return _call(
        body, name=name, grid=(T // bm, F // bn),
        in_specs=[pl.BlockSpec((bm, D), lambda i, j: (i, 0)), pl.BlockSpec((bn, D), lambda i, j: (j, 0)), tile],
        out_specs=[tile, tile], out_shape=[SDS((T, F), BF16)] * 2, scratch_shapes=[],
        compiler_params=_cp(("arbitrary", "arbitrary"), 40),
        args=(n, wu, hg), carried=carried)


def _mm_res(a, b, x, scale, name, carried=None):
    T, K = a.shape
    D = b.shape[1]
    bm = _blk(T, 1024, BF16_ROWS)
    bk = _blk(K, 1408, LANES)
    nk = K // bk

    def body(a_ref, b_ref, x_ref, o_ref):
        k = pl.program_id(1)

        @pl.when(k == 0)
        def _():
            o_ref[...] = jnp.zeros_like(o_ref)

        o_ref[...] += _dot(a_ref[...], b_ref[...], NN)

        @pl.when(k == nk - 1)
        def _():
            def chunk(c, _):
                r = _chunk_rows(c)
                o_ref[r, :] = x_ref[r, :] + scale * o_ref[r, :]
                return 0
            lax.fori_loop(0, bm // ROW_CHUNK, chunk, 0)

    row = pl.BlockSpec((bm, D), lambda i, k: (i, 0))
    out = _call(
        body, name=name, grid=(T // bm, nk),
        in_specs=[pl.BlockSpec((bm, bk), lambda i, k: (i, k)), pl.BlockSpec((bk, D), lambda i, k: (k, 0)), row],
        out_specs=[row], out_shape=[SDS((T, D), F32)], scratch_shapes=[],
        compiler_params=_cp(("arbitrary", "arbitrary"), 56),
        args=(a, b, x), carried=carried)
    return out[0] if carried is None else (out[0][0], out[1])


def _mm_nt(a, b, name):
    T, K = a.shape
    N = b.shape[0]
    bm = _blk(T, 1024, BF16_ROWS)
    bn = _blk(N, 512, LANES)

    def body(a_ref, b_ref, o_ref):
        o_ref[...] = _dot(a_ref[...], b_ref[...], NT)

    return pl.pallas_call(
        body, name=name, grid=(T // bm, N // bn),
        in_specs=[pl.BlockSpec((bm, K), lambda i, j: (i, 0)), pl.BlockSpec((bn, K), lambda i, j: (j, 0))],
        out_specs=pl.BlockSpec((bm, bn), lambda i, j: (i, j)), out_shape=SDS((T, N), F32),
        compiler_params=_cp(("arbitrary", "arbitrary"), 40),
    )(a, b)


def _ffn_bwd_act(dfb, wd, hg, hu, name):
    T, D = dfb.shape
    F = wd.shape[0]
    bm = _blk(T, 1024, BF16_ROWS)
    bn = _blk(F, 512, LANES)

    def body(df_ref, wd_ref, hg_ref, hu_ref, dhg_ref, dhu_ref):
        dact = _dot(df_ref[...], wd_ref[...], NT)
        hgv = hg_ref[...].astype(F32)
        huv = hu_ref[...].astype(F32)
        s = _sigmoid(hgv)
        dhu_ref[...] = (dact * (hgv * s)).astype(BF16)
        dhg_ref[...] = (dact * huv * (s * (1.0 + hgv * (1.0 - s)))).astype(BF16)

    tile = pl.BlockSpec((bm, bn), lambda i, j: (i, j))
    return pl.pallas_call(
        body, name=name, grid=(T // bm, F // bn),
        in_specs=[pl.BlockSpec((bm, D), lambda i, j: (i, 0)), pl.BlockSpec((bn, D), lambda i, j: (j, 0)), tile, tile],
        out_specs=[tile, tile], out_shape=[SDS((T, F), BF16)] * 2,
        compiler_params=_cp(("arbitrary", "arbitrary"), 40),
    )(dfb, wd, hg, hu)


def _dw_tn(a, b, bm_pref, name):
    T, M = a.shape
    N = b.shape[1]
    bm = _blk(M, bm_pref, LANES)
    tk = _blk(T, 1024, BF16_ROWS)
    nk = T // tk

    def body(a_ref, b_ref, o_ref, acc):
        k = pl.program_id(1)

        @pl.when(k == 0)
        def _():
            acc[...] = jnp.zeros_like(acc)

        acc[...] += _dot(a_ref[...], b_ref[...], TN)

        @pl.when(k == nk - 1)
        def _():
            o_ref[...] = acc[...].astype(BF16)

    return pl.pallas_call(
        body, name=name, grid=(M // bm, nk),
        in_specs=[pl.BlockSpec((tk, bm), lambda i, k: (k, i)), pl.BlockSpec((tk, N), lambda i, k: (k, 0))],
        out_specs=pl.BlockSpec((bm, N), lambda i, k: (i, 0)), out_shape=SDS((M, N), BF16),
        scratch_shapes=[pltpu.VMEM((bm, N), F32)],
        compiler_params=_cp(("arbitrary", "arbitrary"), 48),
    )(a, b)


def _mm_rmsbwd(pairs, x, gain, dx_in, bscale, name, carried=None):
    T, D = x.shape
    K = pairs[0][0].shape[1]
    npair = len(pairs)
    bm = _blk(T, 1024, BF16_ROWS)
    bk = _blk(K, 1024 // npair, LANES)
    nk = K // bk

    nchunk = bm // ROW_CHUNK

    def body(*refs):
        ab = refs[:2 * npair]
        x_hbm, g_ref, dxin_hbm, dx_ref, dxb_ref, dg_ref, x_buf, dxin_buf, sems = refs[2 * npair:]
        i = pl.program_id(0)
        k = pl.program_id(1)

        def fetch(c, slot):
            rows = pl.ds(i * bm + c * ROW_CHUNK, ROW_CHUNK)
            return (pltpu.make_async_copy(x_hbm.at[rows, :], x_buf.at[slot], sems.at[slot, 0]),
                    pltpu.make_async_copy(dxin_hbm.at[rows, :], dxin_buf.at[slot], sems.at[slot, 1]))

        @pl.when(k == 0)
        def _():
            dx_ref[...] = jnp.zeros_like(dx_ref)

        @pl.when(k == nk - 1)
        def _():
            for cp in fetch(0, 0):
                cp.start()

        for q in range(npair):
            dx_ref[...] += _dot(ab[2 * q][...], ab[2 * q + 1][...], NN)

        @pl.when(k == nk - 1)
        def _():
            @pl.when(i == 0)
            def _():
                dg_ref[...] = jnp.zeros_like(dg_ref)

            def chunk(c, _):
                slot = c % 2

                @pl.when(c + 1 < nchunk)
                def _():
                    for cp in fetch(c + 1, 1 - slot):
                        cp.start()

                for cp in fetch(c, slot):
                    cp.wait()

                r = _chunk_rows(c)
                dx, dg = _rms_bwd(x_buf[slot], g_ref[...], dx_ref[r, :])
                dxo = dxin_buf[slot] + dx
                dx_ref[r, :] = dxo
                dxb_ref[r, :] = (bscale * dxo).astype(BF16)
                dg_ref[...] += dg
                return 0
            lax.fori_loop(0, nchunk, chunk, 0)

    row = pl.BlockSpec((bm, D), lambda i, k: (i, 0))
    anywhere = pl.BlockSpec(memory_space=pl.ANY)
    vec = pl.BlockSpec((1, D), lambda i, k: (0, 0))
    in_specs = []
    args = []
    for a, b in pairs:
        in_specs += [pl.BlockSpec((bm, bk), lambda i, k: (i, k)), pl.BlockSpec((bk, D), lambda i, k: (k, 0))]
        args += [a, b]
    return _call(
        body, name=name, grid=(T // bm, nk),
        in_specs=in_specs + [anywhere, vec, anywhere], out_specs=[row, row, vec],
        out_shape=[SDS((T, D), F32), SDS((T, D), BF16), SDS((1, D), F32)],
        scratch_shapes=[pltpu.VMEM((2, ROW_CHUNK, D), F32), pltpu.VMEM((2, ROW_CHUNK, D), F32),
                        pltpu.SemaphoreType.DMA((2, 2))],
        compiler_params=_cp(("arbitrary", "arbitrary"), 52),
        args=(*args, x, gain, dx_in), carried=carried)


def _loss_head(x3, gain, tgt, name):
    T, D = x3.shape
    bm = _blk(T, 256, BF16_ROWS)

    def body(x_ref, g_ref, t_ref, dx_ref, dxb_ref, dg_ref, loss_ref):
        i = pl.program_id(0)
        xv = x_ref[...]
        g = g_ref[...]
        out = xv * _rstd(xv) * g
        e = out - t_ref[...]
        part = 0.5 * jnp.sum(jnp.mean(e * e, axis=-1, keepdims=True), axis=0, keepdims=True)
        dx, dg = _rms_bwd(xv, g, e * (1.0 / D))
        dx_ref[...] = dx
        dxb_ref[...] = (0.5 * dx).astype(BF16)

        @pl.when(i == 0)
        def _():
            dg_ref[...] = dg
            loss_ref[...] = jnp.broadcast_to(part, loss_ref.shape)

        @pl.when(i > 0)
        def _():
            dg_ref[...] += dg
            loss_ref[...] += jnp.broadcast_to(part, loss_ref.shape)

    row = pl.BlockSpec((bm, D), lambda i: (i, 0))
    vec = pl.BlockSpec((1, D), lambda i: (0, 0))
    return pl.pallas_call(
        body, name=name, grid=(T // bm,),
        in_specs=[row, vec, row], out_specs=[row, row, vec, pl.BlockSpec((SUBLANES, LANES), lambda i: (0, 0))],
        out_shape=[SDS((T, D), F32), SDS((T, D), BF16), SDS((1, D), F32), SDS((SUBLANES, LANES), F32)],
        compiler_params=_cp(("arbitrary",), 40),
    )(x3, gain, tgt)


R_CW, R_CB, R_BA, R_BI, R_LAM, R_SW, R_GLO, R_GSO, SMALL_ROWS = 0, 4, 5, 6, 7, 8, 11, 12, 16


def _rows(g):
    return pl.ds(pl.multiple_of(g * SUBLANES, SUBLANES), SUBLANES)


def _shift_back(prev, cur, d):
    row = lax.broadcasted_iota(jnp.int32, cur.shape, 0)
    return pltpu.roll(jnp.where(row >= SUBLANES - d, prev, cur), d, 0)


def _shift_fwd(cur, nxt, d):
    row = lax.broadcasted_iota(jnp.int32, cur.shape, 0)
    return pltpu.roll(jnp.where(row < d, nxt, cur), SUBLANES - d, 0)


def _causal_conv(ext, g, taps_ref, ntap):
    prev = ext[_rows(g), :]
    cur = ext[_rows(g + 1), :]
    out = _shift_back(prev, cur, ntap - 1) * taps_ref[0:1, :]
    for k in range(1, ntap - 1):
        out = out + _shift_back(prev, cur, ntap - 1 - k) * taps_ref[k:k + 1, :]
    return out + cur * taps_ref[ntap - 1:ntap, :]


def _scan8(A, U, reverse):
    row = lax.broadcasted_iota(jnp.int32, A.shape, 0)
    for s in (1, 2, 4):
        if reverse:
            A_sh = pltpu.roll(A, SUBLANES - s, 0)
            U_sh = pltpu.roll(U, SUBLANES - s, 0)
            m = row < SUBLANES - s
        else:
            A_sh = pltpu.roll(A, s, 0)
            U_sh = pltpu.roll(U, s, 0)
            m = row >= s
        U = jnp.where(m, A * U_sh + U, U)
        A = jnp.where(m, A * A_sh, A)
    return A, U


def _gate_pre(xc_s, w_ref, out_s, H, hd):
    for h in range(H):
        cs = slice(h * hd, (h + 1) * hd)
        out_s[:, cs] = _dot(xc_s[:, cs].astype(BF16), w_ref[h], NN)


def _lru_coeffs(pa, pi, xc, ba, bi, sp):
    ra = _sigmoid(pa + ba)
    ri = _sigmoid(pi + bi)
    log_a = (-LRU_C * ra) * sp
    a = jnp.exp(log_a)
    mult = jnp.sqrt(_one_minus_exp(2.0 * log_a, a))
    return ra, ri, a, mult


def _softplus_neg(lam):
    v = -lam
    return jnp.maximum(v, 0.0) + _log1p(jnp.exp(-jnp.abs(v)))


def _mix_fwd(z, cw, cb, wa, ba, wi, bi, lam, sw, glo, gso, name):
    T = z.shape[0]
    C = z.shape[1] // 5
    H = wa.shape[0]
    hd = C // H
    tb = _blk(T, 256, BF16_ROWS)
    ng = tb // SUBLANES
    HDR = SUBLANES

    def body(z_ref, cw_ref, cb_ref, wa_ref, ba_ref, wi_ref, bi_ref, lam_ref, sw_ref, glo_ref, gso_ref,
             y_ref, h_ref, ra_ref, ri_ref, a_ref, m_ref, xc_s, q_ref, xext, pext, pa_s, pi_s, y_s, hcar):
        @pl.when(pl.program_id(0) == 0)
        def _():
            xext[0:HDR, :] = jnp.zeros((HDR, C), F32)
            pext[0:HDR, :] = jnp.zeros((HDR, C), F32)
            hcar[...] = jnp.zeros_like(hcar)

        def fill(g, _):
            r = _rows(g)
            re = _rows(g + 1)
            xext[re, :] = z_ref[r, 0:C]
            pext[re, :] = z_ref[r, 3 * C:4 * C] * z_ref[r, 4 * C:5 * C]
            return 0
        lax.fori_loop(0, ng, fill, 0)

        def conv(g, _):
            xc_s[_rows(g), :] = _causal_conv(xext, g, cw_ref, 4) + cb_ref[...]
            return 0
        lax.fori_loop(0, ng, conv, 0)

        _gate_pre(xc_s, wa_ref, pa_s, H, hd)
        _gate_pre(xc_s, wi_ref, pi_s, H, hd)
        sp = _softplus_neg(lam_ref[...])

        def group(g, hprev):
            r = _rows(g)
            xc = xc_s[r, :]
            ra, ri, a, mult = _lru_coeffs(pa_s[r, :], pi_s[r, :], xc, ba_ref[...], bi_ref[...], sp)
            ra_ref[r, :] = ra
            ri_ref[r, :] = ri
            a_ref[r, :] = a
            m_ref[r, :] = mult
            A, U = _scan8(a, mult * (ri * xc), reverse=False)
            hh = A * hprev + U
            h_ref[r, :] = hh
            gel, _ = _gelu_parts(z_ref[r, C:2 * C])
            y_lru = hh * gel
            y_s[r, 0:C] = y_lru * _rstd(y_lru) * glo_ref[...]
            q = _causal_conv(pext, g, sw_ref, 3)
            q_ref[r, :] = q
            y_sc = z_ref[r, 2 * C:3 * C] * q
            y_s[r, C:2 * C] = y_sc * _rstd(y_sc) * gso_ref[...]
            return jnp.broadcast_to(hh[SUBLANES - 1:SUBLANES, :], hh.shape)
        hcar[...] = lax.fori_loop(0, ng // 2, lambda t, hp: group(2 * t + 1, group(2 * t, hp)), hcar[...])

        xext[0:HDR, :] = xext[tb:tb + HDR, :]
        pext[0:HDR, :] = pext[tb:tb + HDR, :]

        def cast(g, _):
            r = pl.ds(pl.multiple_of(g * BF16_ROWS, BF16_ROWS), BF16_ROWS)
            y_ref[r, :] = y_s[r, :].astype(BF16)
            return 0
        lax.fori_loop(0, tb // BF16_ROWS, cast, 0)

    full = lambda shape: pl.BlockSpec(shape, lambda i: (0,) * len(shape))
    blk = lambda w: pl.BlockSpec((tb, w), lambda i: (i, 0))
    ext = pltpu.VMEM((tb + HDR, C), F32)
    tile = pltpu.VMEM((tb, C), F32)
    return pl.pallas_call(
        body, name=name, grid=(T // tb,),
        in_specs=[blk(5 * C), full((4, C)), full((1, C)), full((H, hd, hd)), full((1, C)), full((H, hd, hd)),
                  full((1, C)), full((1, C)), full((3, C)), full((1, C)), full((1, C))],
        out_specs=[blk(2 * C)] + [blk(C)] * 7,
        out_shape=[SDS((T, 2 * C), BF16)] + [SDS((T, C), F32)] * 7,
        scratch_shapes=[ext, ext, tile, tile, pltpu.VMEM((tb, 2 * C), F32), pltpu.VMEM((SUBLANES, C), F32)],
        compiler_params=_cp(("arbitrary",), 48),
    )(z, cw, cb, wa, ba, wi, bi, lam, sw, glo, gso)


def _mix_bwd(z, h, saved, dy, cw, wa, wi, lam, sw, glo, gso, name):
    T = z.shape[0]
    C = z.shape[1] // 5
    H = wa.shape[0]
    hd = C // H
    tb = _blk(T, 256, BF16_ROWS)
    nb = T // tb
    ng = tb // SUBLANES
    HDR = SUBLANES
    N_ACC = 13

    def body(z_ref, zp_ref, h_ref, hp_ref, ra_ref, ri_ref, a_ref, m_ref, xc_s, q_ref, dy_ref, cw_ref, wa_ref, wi_ref,
             lam_ref, sw_ref, glo_ref, gso_ref, dz_ref, small_ref, dwa_ref, dwi_ref,
             xext, pext, hext, dqext, dxcext, bext, dh_s, dpa_s, dpi_s, dz_s, acc_s, bcar):
        i = pl.program_id(0)
        first_rows = i == nb - 1

        @pl.when(i == 0)
        def _():
            dqext[tb:tb + HDR, :] = jnp.zeros((HDR, C), F32)
            dxcext[tb:tb + HDR, :] = jnp.zeros((HDR, C), F32)
            bcar[...] = jnp.zeros_like(bcar)
            acc_s[...] = jnp.zeros_like(acc_s)
            dwa_ref[...] = jnp.zeros_like(dwa_ref)
            dwi_ref[...] = jnp.zeros_like(dwi_ref)

        zero = jnp.zeros((HDR, C), F32)
        xext[0:HDR, :] = jnp.where(first_rows, zero, zp_ref[:, 0:C])
        pext[0:HDR, :] = jnp.where(first_rows, zero, zp_ref[:, 3 * C:4 * C] * zp_ref[:, 4 * C:5 * C])
        hext[0:HDR, :] = jnp.where(first_rows, zero, hp_ref[...])

        def fill(g, _):
            r = _rows(g)
            re = _rows(g + 1)
            xext[re, :] = z_ref[r, 0:C]
            pext[re, :] = z_ref[r, 3 * C:4 * C] * z_ref[r, 4 * C:5 * C]
            hext[re, :] = h_ref[r, :]
            return 0
        lax.fori_loop(0, ng, fill, 0)

        sp = _softplus_neg(lam_ref[...])
        dsp_dlam = -jax.nn.sigmoid(-lam_ref[...])

        def add_acc(k, v):
            acc_s[k] += v

        def p1(g, _):
            r = _rows(g)
            hh = h_ref[r, :]
            gel, gelp = _gelu_parts(z_ref[r, C:2 * C])
            y_lru = hh * gel
            dnl = dy_ref[r, 0:C]
            rl = _rstd(y_lru)
            ylr = y_lru * rl
            gd = glo_ref[...] * dnl
            dy_lru = rl * (gd - ylr * jnp.mean(gd * ylr, axis=-1, keepdims=True))
            add_acc(R_GLO, dnl * ylr)
            dz_s[r, C:2 * C] = dy_lru * hh * gelp
            dh = dy_lru * gel
            dh_s[r, :] = dh

            q = q_ref[r, :]
            scb = z_ref[r, 2 * C:3 * C]
            y_sc = scb * q
            dns = dy_ref[r, C:2 * C]
            rs = _rstd(y_sc)
            ysr = y_sc * rs
            gs = gso_ref[...] * dns
            dy_sc = rs * (gs - ysr * jnp.mean(gs * ysr, axis=-1, keepdims=True))
            add_acc(R_GSO, dns * ysr)
            dz_s[r, 2 * C:3 * C] = dy_sc * q
            dqext[r, :] = dy_sc * scb
            return 0
        lax.fori_loop(0, ng, p1, 0, unroll=2)

        bext[tb:tb + HDR, :] = bcar[...]

        def p2(j, carry):
            g = ng - 1 - j
            r = _rows(g)
            a = a_ref[r, :]
            A, U = _scan8(a, a * dh_s[r, :], reverse=True)
            bb = A * carry + U
            bext[r, :] = bb
            return jnp.broadcast_to(bb[0:1, :], bb.shape)
        bcar[...] = lax.fori_loop(0, ng, p2, bcar[...])

        def p3(g, _):
            r = _rows(g)
            rn = _rows(g + 1)
            G = dh_s[r, :] + _shift_fwd(bext[r, :], bext[rn, :], 1)
            hm1 = _shift_back(hext[r, :], hext[rn, :], 1)
            a = a_ref[r, :]
            mult = m_ref[r, :]
            ri = ri_ref[r, :]
            xc = xc_s[r, :]
            ra = ra_ref[r, :]
            dxcext[r, :] = G * mult * ri
            dri = G * mult * xc
            dmult = G * ri * xc
            dlog_a = (G * hm1) * a - dmult * (a * a) / mult
            add_acc(R_LAM, dlog_a * (-LRU_C * ra) * dsp_dlam)
            dpa = dlog_a * (-LRU_C * sp) * ra * (1.0 - ra)
            dpi = dri * ri * (1.0 - ri)
            add_acc(R_BA, dpa)
            add_acc(R_BI, dpi)
            dpa_s[r, :] = dpa
            dpi_s[r, :] = dpi
            return 0
        lax.fori_loop(0, ng, p3, 0)

        for hh_ in range(H):
            cs = slice(hh_ * hd, (hh_ + 1) * hd)
            dpa_b = dpa_s[:, cs].astype(BF16)
            dpi_b = dpi_s[:, cs].astype(BF16)
            xc_b = xc_s[:, cs].astype(BF16)
            dxcext[0:tb, cs] += _dot(dpa_b, wa_ref[hh_], NT) + _dot(dpi_b, wi_ref[hh_], NT)
            dwa_ref[hh_] += _dot(xc_b, dpa_b, TN)
            dwi_ref[hh_] += _dot(xc_b, dpi_b, TN)

        def p4(g, _):
            r = _rows(g)
            rn = _rows(g + 1)
            dxc = dxcext[r, :]
            dxc_n = dxcext[rn, :]
            x_p = xext[r, :]
            x_c = xext[rn, :]
            add_acc(R_CB, dxc)
            dlx = dxc * cw_ref[3:4, :]
            add_acc(R_CW + 3, dxc * x_c)
            for d in range(1, 4):
                dlx = dlx + _shift_fwd(dxc, dxc_n, d) * cw_ref[3 - d:4 - d, :]
                add_acc(R_CW + 3 - d, dxc * _shift_back(x_p, x_c, d))
            dz_s[r, 0:C] = dlx

            dq = dqext[r, :]
            dq_n = dqext[rn, :]
            p_p = pext[r, :]
            p_c = pext[rn, :]
            dp = dq * sw_ref[2:3, :]
            add_acc(R_SW + 2, dq * p_c)
            for d in range(1, 3):
                dp = dp + _shift_fwd(dq, dq_n, d) * sw_ref[2 - d:3 - d, :]
                add_acc(R_SW + 2 - d, dq * _shift_back(p_p, p_c, d))
            dz_s[r, 3 * C:4 * C] = dp * z_ref[r, 4 * C:5 * C]
            dz_s[r, 4 * C:5 * C] = dp * z_ref[r, 3 * C:4 * C]
            return 0
        lax.fori_loop(0, ng, p4, 0)

        dqext[tb:tb + HDR, :] = dqext[0:HDR, :]
        dxcext[tb:tb + HDR, :] = dxcext[0:HDR, :]

        def cast(g, _):
            r = pl.ds(pl.multiple_of(g * BF16_ROWS, BF16_ROWS), BF16_ROWS)
            dz_ref[r, :] = dz_s[r, :].astype(BF16)
            return 0
        lax.fori_loop(0, tb // BF16_ROWS, cast, 0)

        @pl.when(i == nb - 1)
        def _():
            small_ref[...] = jnp.zeros_like(small_ref)
            for k in range(N_ACC):
                small_ref[k:k + 1, :] = jnp.sum(acc_s[k], axis=0, keepdims=True)

    tpg = tb // SUBLANES
    full = lambda shape: pl.BlockSpec(shape, lambda i: (0,) * len(shape))
    blk = lambda w: pl.BlockSpec((tb, w), lambda i: (nb - 1 - i, 0))
    prev = lambda w: pl.BlockSpec((SUBLANES, w), lambda i: (jnp.maximum((nb - 1 - i) * tpg - 1, 0), 0))
    ext = pltpu.VMEM((tb + HDR, C), F32)
    tile = pltpu.VMEM((tb, C), F32)
    return pl.pallas_call(
        body, name=name, grid=(nb,),
        in_specs=[blk(5 * C), prev(5 * C), blk(C), prev(C)] + [blk(C)] * 6
        + [blk(2 * C), full((4, C)), full((H, hd, hd)), full((H, hd, hd)), full((1, C)), full((3, C)),
           full((1, C)), full((1, C))],
        out_specs=[blk(5 * C), full((SMALL_ROWS, C)), full((H, hd, hd)), full((H, hd, hd))],
        out_shape=[SDS((T, 5 * C), BF16), SDS((SMALL_ROWS, C), F32), SDS((H, hd, hd), F32), SDS((H, hd, hd), F32)],
        scratch_shapes=[ext] * 6 + [tile] * 3 + [pltpu.VMEM((tb, 5 * C), F32), pltpu.VMEM((N_ACC, SUBLANES, C), F32),
                                                pltpu.VMEM((SUBLANES, C), F32)],
        compiler_params=_cp(("arbitrary",), 56),
    )(z, z, h, h, *saved, dy, cw, wa, wi, lam, sw, glo, gso)


def _add_slabs(terms, out_dtype, name):
    R, Ccols = terms[0].shape
    br = _blk(R, 512, BF16_ROWS)
    n = len(terms)

    def body(*refs):
        s = refs[0][...].astype(F32)
        for t_ref in refs[1:n]:
            s = s + t_ref[...].astype(F32)
        refs[n][...] = s.astype(out_dtype)

    spec = pl.BlockSpec((br, Ccols), lambda i: (i, 0))
    return pl.pallas_call(
        body, name=name, grid=(R // br,), in_specs=[spec] * n, out_specs=spec, out_shape=SDS((R, Ccols), out_dtype),
        compiler_params=_cp(("arbitrary",), 40),
    )(*terms)


def _final_grad(sb, lb, chip, name):
    _, R, Ccols = sb.shape
    br = _blk(R, 512, BF16_ROWS)

    def body(chip_ref, sb_ref, l0, l1, l2, o_ref):
        s = sb_ref[0].astype(F32)
        for t_ref in (l0, l1, l2):
            s = s + t_ref[0].astype(F32)
        o_ref[...] = s

    lspec = lambda k: pl.BlockSpec((1, br, Ccols), lambda i, c: (k, i, 0))
    return pl.pallas_call(
        body, name=name,
        grid_spec=pltpu.PrefetchScalarGridSpec(
            num_scalar_prefetch=1, grid=(R // br,),
            in_specs=[pl.BlockSpec((1, br, Ccols), lambda i, c: (c[0], i, 0)), lspec(0), lspec(1), lspec(2)],
            out_specs=pl.BlockSpec((br, Ccols), lambda i, c: (i, 0))),
        out_shape=SDS((R, Ccols), F32),
        compiler_params=_cp(("arbitrary",), 40),
    )(chip, sb, lb, lb, lb)


def _adamw(w, g, m, v, name):
    R, Ccols = w.shape
    br = _blk(R, 256, SUBLANES)
    c1 = 1.0 - ADAM_B1 ** ADAM_STEP
    c2 = 1.0 - ADAM_B2 ** ADAM_STEP

    def body(w_ref, g_ref, m_ref, v_ref, d_ref, nm_ref, nv_ref):
        gv = g_ref[...]
        nm = ADAM_B1 * m_ref[...] + (1.0 - ADAM_B1) * gv
        nv = ADAM_B2 * v_ref[...] + (1.0 - ADAM_B2) * (gv * gv)
        nm_ref[...] = nm
        nv_ref[...] = nv
        d_ref[...] = -ADAM_LR * ((nm / c1) / (jnp.sqrt(nv / c2) + ADAM_EPS) + ADAM_WD * w_ref[...])

    spec = pl.BlockSpec((br, Ccols), lambda i: (i, 0))
    return pl.pallas_call(
        body, name=name, grid=(R // br,), in_specs=[spec] * 4, out_specs=[spec] * 3,
        out_shape=[SDS((R, Ccols), F32)] * 3, compiler_params=_cp(("arbitrary",), 40),
    )(w, g, m, v)


def _place():
    return lax.axis_index("x"), lax.axis_index("y"), lax.axis_index("c")


def _dev_rows(ref, dev, rows):
    return ref.at[pl.ds((4 * dev[0] + 2 * dev[1] + dev[2]) * rows, rows), :]


def _remote(src, dst, send_sem, recv_sem, to):
    return pltpu.make_async_remote_copy(src_ref=src, dst_ref=dst, send_sem=send_sem, recv_sem=recv_sem,
                                        device_id=to, device_id_type=MESH)


SAME_CORE_AND_SIBLING = ((0, 0, 1), (1, 0, 0), (0, 1, 0), (1, 1, 0))


def _merge_phases(a, b):
    na_in, na_out, na_sem = len(a.inputs), len(a.out_shapes), len(a.sem_shapes)

    def build(ins, outs, sems, stage):
        return (a.build(ins[:na_in], outs[:na_out], sems[:na_sem], stage)
                + b.build(ins[na_in:], outs[na_out:], sems[na_sem:], stage))

    aliases = dict(a.aliases)
    aliases.update({na_in + i: na_out + o for i, o in b.aliases.items()})
    return _Carried(a.inputs + b.inputs, a.out_shapes + b.out_shapes, aliases, a.sem_shapes + b.sem_shapes, build,
                    has_mid=a.has_mid or b.has_mid)


def _ag_direct_phase(slab, pieces, flips):
    W = slab.shape[1]
    n = len(pieces)
    npeer = len(flips)

    def build(ins, outs, sems, stage):
        if stage == "mid":
            return []
        starting = stage == "start"
        (slab_ref,) = ins
        send_sems, recv_sems, local_sems = sems
        x, y, c = _place()
        me = (x, y, c)
        peers = [tuple(1 - v if f else v for v, f in zip(me, flip)) for flip in flips]
        todo = []
        for p, (off, rows) in enumerate(pieces):
            src = slab_ref.at[pl.ds(off, rows), :]
            mine = pltpu.make_async_copy(src, _dev_rows(outs[p], me, rows), local_sems.at[p])
            todo.append(mine.start if starting else mine.wait)
            for k, peer in enumerate(peers):
                snd = _remote(src, _dev_rows(outs[p], me, rows), send_sems.at[k, p], recv_sems.at[k, p], peer)
                if starting:
                    todo.append(snd.start)
                else:
                    theirs = _dev_rows(outs[p], peer, rows)
                    rcv = _remote(theirs, theirs, send_sems.at[k, p], recv_sems.at[k, p], me)
                    todo += [rcv.wait_recv, snd.wait_send]
        return todo

    dma = pltpu.SemaphoreType.DMA
    return _Carried([slab], [SDS((N_DEV * rows, W), slab.dtype) for _, rows in pieces], {},
                    [dma((npeer, n)), dma((npeer, n)), dma((n,))], build)


def _ag_two_level_phase(slab, pieces):
    W = slab.shape[1]
    n = len(pieces)

    def build(ins, outs, sems, stage):
        (slab_ref,) = ins
        send_sems, recv_sems, local_sems = sems
        x, y, c = _place()
        me, sibling = (x, y, c), (x, y, 1 - c)
        chips = [(1 - x, y), (x, 1 - y), (1 - x, 1 - y)]
        todo = []
        for p, (off, rows) in enumerate(pieces):
            src = slab_ref.at[pl.ds(off, rows), :]
            own = _dev_rows(outs[p], me, rows)
            landed = [_dev_rows(outs[p], (*chip, c), rows) for chip in chips]

            def mine():
                return pltpu.make_async_copy(src, own, local_sems.at[p])

            def first():
                return [_remote(src, own, send_sems.at[k, p], recv_sems.at[k, p], to)
                        for k, to in enumerate([sibling] + [(*chip, c) for chip in chips])]

            def passed():
                return [_remote(blk, blk, send_sems.at[4 + j, p], recv_sems.at[4 + j, p], sibling)
                        for j, blk in enumerate(landed)]

            def arrival(k, blk):
                return _remote(blk, blk, send_sems.at[k, p], recv_sems.at[k, p], me).wait_recv

            if stage == "start":
                todo += [mine().start] + [cp.start for cp in first()]
            elif stage == "mid":
                for j, (blk, fwd) in enumerate(zip(landed, passed())):
                    todo += [arrival(1 + j, blk), fwd.start]
            else:
                theirs = [_dev_rows(outs[p], sibling, rows)] + [_dev_rows(outs[p], (*chip, 1 - c), rows) for chip in chips]
                todo += [arrival(k, blk) for k, blk in zip((0, 4, 5, 6), theirs)]
                todo += [cp.wait_send for cp in first() + passed()] + [mine().wait]
        return todo

    dma = pltpu.SemaphoreType.DMA
    return _Carried([slab], [SDS((N_DEV * rows, W), slab.dtype) for _, rows in pieces], {},
                    [dma((7, n)), dma((7, n)), dma((n,))], build, has_mid=True)


def _ag_forward_phase(gathered, pieces):
    n = len(pieces)

    def build(ins, outs, sems, stage):
        if stage == "mid":
            return []
        starting = stage == "start"
        send_sems, recv_sems = sems
        x, y, c = _place()
        me, sibling = (x, y, c), (x, y, 1 - c)
        chips = [(1 - x, y), (x, 1 - y), (1 - x, 1 - y)]
        todo = []
        for p, (_, rows) in enumerate(pieces):
            for j, chip in enumerate(chips):
                snd = _remote(_dev_rows(ins[p], (*chip, c), rows), _dev_rows(outs[p], (*chip, c), rows),
                              send_sems.at[j, p], recv_sems.at[j, p], sibling)
                if starting:
                    todo.append(snd.start)
                else:
                    theirs = _dev_rows(outs[p], (*chip, 1 - c), rows)
                    rcv = _remote(theirs, theirs, send_sems.at[j, p], recv_sems.at[j, p], me)
                    todo += [rcv.wait_recv, snd.wait_send]
        return todo

    dma = pltpu.SemaphoreType.DMA
    return _Carried(gathered, [SDS(g.shape, g.dtype) for g in gathered], {p: p for p in range(n)},
                    [dma((3, n)), dma((3, n))], build)


def _rs_chips_phase(sb):
    _, R, W = sb.shape

    def build(ins, outs, sems, stage):
        if stage == "mid":
            return []
        (sb_ref,), (land_ref,) = ins, outs
        send_sems, recv_sems = sems
        x, y, c = _place()
        chips = [(1 - x, y), (x, 1 - y), (1 - x, 1 - y)]
        cps = [_remote(sb_ref.at[2 * chip[0] + chip[1]], land_ref.at[j], send_sems.at[j], recv_sems.at[j], (*chip, c))
               for j, chip in enumerate(chips)]
        if stage == "start":
            return [cp.start for cp in cps]
        return [cp.wait_recv for cp in cps] + [cp.wait_send for cp in cps]

    dma = pltpu.SemaphoreType.DMA
    return _Carried([sb], [SDS((3, R, W), sb.dtype)], {}, [dma((3,)), dma((3,))], build)


def _allgather(slab, pieces, name):
    R, W = slab.shape
    n = len(pieces)
    assert sum(rows for _, rows in pieces) == R

    def body(slab_ref, *refs):
        outs = refs[:n]
        send_sems, recv_sems, local_sems = refs[n:]
        x, y, c = _place()
        me, sibling = (x, y, c), (x, y, 1 - c)
        chips = [(1 - x, y), (x, 1 - y), (1 - x, 1 - y)]

        def dst_rows(p, origin):
            rows = pieces[p][1]
            start = (4 * origin[0] + 2 * origin[1] + origin[2]) * rows
            return outs[p].at[pl.ds(start, rows), :]

        def copies(k, origin, to, from_slab):
            out = []
            for p, (off, rows) in enumerate(pieces):
                dst = dst_rows(p, origin)
                src = slab_ref.at[pl.ds(off, rows), :] if from_slab else dst
                out.append(pltpu.make_async_remote_copy(
                    src_ref=src, dst_ref=dst, send_sem=send_sems.at[k, p], recv_sem=recv_sems.at[k, p],
                    device_id=to, device_id_type=MESH))
            return out

        mine = [pltpu.make_async_copy(slab_ref.at[pl.ds(off, rows), :], dst_rows(p, me), local_sems.at[p])
                for p, (off, rows) in enumerate(pieces)]
        for cp in mine:
            cp.start()
        first = copies(0, me, sibling, True)
        for j, chip in enumerate(chips):
            first += copies(1 + j, me, (*chip, c), True)
        for cp in first:
            cp.start()
        passed = []
        for j, chip in enumerate(chips):
            for cp in copies(1 + j, (*chip, c), me, False):
                cp.wait_recv()
            fwd = copies(4 + j, (*chip, c), sibling, False)
            for cp in fwd:
                cp.start()
            passed += fwd
        for cp in copies(0, sibling, me, False):
            cp.wait_recv()
        for j, chip in enumerate(chips):
            for cp in copies(4 + j, (*chip, 1 - c), me, False):
                cp.wait_recv()
        for cp in first + passed:
            cp.wait_send()
        for cp in mine:
            cp.wait()

    return pl.pallas_call(
        body, name=name,
        in_specs=[HBM_SPEC], out_specs=[HBM_SPEC] * n,
        out_shape=[SDS((N_DEV * rows, W), slab.dtype) for _, rows in pieces],
        scratch_shapes=[pltpu.SemaphoreType.DMA((7, n)), pltpu.SemaphoreType.DMA((7, n)), pltpu.SemaphoreType.DMA((n,))],
    )(slab)


def _rs_sibling(grads, pieces, name):
    W = grads[0].shape[1]
    R = sum(rows for _, rows in pieces)
    n = len(pieces)
    dt = grads[0].dtype
    max_rows = max(rows for _, rows in pieces)
    steps = [(q, p) for q in range(N_CHIP) for p in range(n)]
    ns = len(steps)
    ADD_ROWS = 64
    SLOTS = 3
    assert all(rows % ADD_ROWS == 0 for _, rows in pieces)

    def body(*refs):
        g_refs = refs[:n]
        sb_ref, mine_buf, send_buf, land_buf, out_buf, in_sems, out_sems, send_sems, recv_sems, credit = refs[n:]
        x, y, c = _place()
        sibling = (x, y, 1 - c)

        def loads(s):
            q, p = steps[s]
            rows = pieces[p][1]
            slot = s % SLOTS
            mine = g_refs[p].at[pl.ds((2 * q + c) * rows, rows), :]
            theirs = g_refs[p].at[pl.ds((2 * q + 1 - c) * rows, rows), :]
            return (pltpu.make_async_copy(mine, mine_buf.at[slot, pl.ds(0, rows), :], in_sems.at[slot, 0]),
                    pltpu.make_async_copy(theirs, send_buf.at[slot, pl.ds(0, rows), :], in_sems.at[slot, 1]))

        def send(s):
            rows = pieces[steps[s][1]][1]
            slot = s % SLOTS
            return pltpu.make_async_remote_copy(
                src_ref=send_buf.at[slot, pl.ds(0, rows), :], dst_ref=land_buf.at[slot, pl.ds(0, rows), :],
                send_sem=send_sems.at[slot], recv_sem=recv_sems.at[slot], device_id=sibling, device_id_type=MESH)

        def store(s):
            q, p = steps[s]
            off, rows = pieces[p]
            slot = s % SLOTS
            return pltpu.make_async_copy(out_buf.at[slot, pl.ds(0, rows), :], sb_ref.at[q, pl.ds(off, rows), :],
                                         out_sems.at[slot])

        def start_send(s):
            for cp in loads(s):
                cp.wait()
            if s >= SLOTS:
                pl.semaphore_wait(credit.at[s % SLOTS], 1)
            send(s).start()

        for s in range(min(SLOTS, ns)):
            for cp in loads(s):
                cp.start()
        for s in range(min(SLOTS - 1, ns)):
            start_send(s)
        for s in range(ns):
            slot = s % SLOTS
            rows = pieces[steps[s][1]][1]
            if s + SLOTS - 1 < ns:
                start_send(s + SLOTS - 1)
            send(s).wait_recv()
            if s >= SLOTS:
                store(s - SLOTS).wait()

            def add(k, _, slot=slot):
                r = pl.ds(pl.multiple_of(k * ADD_ROWS, ADD_ROWS), ADD_ROWS)
                out_buf[slot, r, :] = (mine_buf[slot, r, :].astype(F32) + land_buf[slot, r, :].astype(F32)).astype(dt)
                return 0
            lax.fori_loop(0, rows // ADD_ROWS, add, 0)
            if s + SLOTS < ns:
                pl.semaphore_signal(credit.at[slot], inc=1, device_id=sibling, device_id_type=MESH)
            store(s).start()
            send(s).wait_send()
            if s + SLOTS < ns:
                for cp in loads(s + SLOTS):
                    cp.start()
        for s in range(max(ns - SLOTS, 0), ns):
            store(s).wait()

    buf = pltpu.VMEM((SLOTS, max_rows, W), dt)
    return pl.pallas_call(
        body, name=name,
        in_specs=[HBM_SPEC] * n, out_specs=HBM_SPEC,
        out_shape=SDS((N_CHIP, R, W), dt),
        scratch_shapes=[buf, buf, buf, buf, pltpu.SemaphoreType.DMA((SLOTS, 2)), pltpu.SemaphoreType.DMA((SLOTS,)),
                        pltpu.SemaphoreType.DMA((SLOTS,)), pltpu.SemaphoreType.DMA((SLOTS,)),
                        pltpu.SemaphoreType.REGULAR((SLOTS,))],
        compiler_params=pltpu.CompilerParams(vmem_limit_bytes=48 * MIB),
    )(*grads)


SMALL_NAMES = ("ffn1_norm", "mix_norm", "ffn2_norm", "final_norm", "lru_conv_w", "lru_conv_b", "lru_w_a", "lru_b_a",
               "lru_w_i", "lru_b_i", "lru_lambda", "sc_conv_w", "lru_out_norm", "sc_out_norm")
WEIGHT_NAMES = ("ffn1_norm", "ffn1_w_gate", "ffn1_w_up", "ffn1_w_down", "mix_norm", "w_in", "lru_conv_w", "lru_conv_b",
                "lru_w_a", "lru_b_a", "lru_w_i", "lru_b_i", "lru_lambda", "sc_conv_w", "lru_out_norm", "sc_out_norm",
                "w_out", "ffn2_norm", "ffn2_w_gate", "ffn2_w_up", "ffn2_w_down", "final_norm")
BIG = (("ffn1_w_gate", True), ("ffn1_w_up", True), ("ffn1_w_down", False), ("ffn2_w_gate", True), ("ffn2_w_up", True),
       ("ffn2_w_down", False), ("w_in", True), ("w_out", False))


SLAB_ROW_ALIGN = 256


def _pack_rows(parts, width):
    rows, counts = [], []
    for p in parts:
        flat = p.reshape(-1)
        nr = -(-flat.shape[0] // width)
        nr = -(-nr // SUBLANES) * SUBLANES
        rows.append(jnp.pad(flat, (0, nr * width - flat.shape[0])).reshape(nr, width))
        counts.append(nr)
    total = sum(counts)
    pad = -(-total // SLAB_ROW_ALIGN) * SLAB_ROW_ALIGN - total
    if pad:
        rows.append(jnp.zeros((pad, width), rows[0].dtype))
    return jnp.concatenate(rows, axis=0), counts


def _stack_rows(blocks):
    pieces, off = [], 0
    for b in blocks:
        pieces.append((off, b.shape[0]))
        off += b.shape[0]
    return jnp.concatenate(blocks, axis=0), pieces


def _unpack_rows(slab, counts, shapes):
    out, r = [], 0
    for nr, shape in zip(counts, shapes):
        size = math.prod(shape)
        out.append(slab[r:r + nr].reshape(-1)[:size].reshape(shape))
        r += nr
    return out


def kernel(x, ffn1_norm, ffn1_w_gate, ffn1_w_up, ffn1_w_down, mix_norm, w_in, lru_conv_w, lru_conv_b, lru_w_a, lru_b_a, lru_w_i, lru_b_i, lru_lambda, sc_conv_w, lru_out_norm, sc_out_norm, w_out, ffn2_norm, ffn2_w_gate, ffn2_w_up, ffn2_w_down, final_norm, loss_target, m_ffn1_norm, m_ffn1_w_gate, m_ffn1_w_up, m_ffn1_w_down, m_mix_norm, m_w_in, m_lru_conv_w, m_lru_conv_b, m_lru_w_a, m_lru_b_a, m_lru_w_i, m_lru_b_i, m_lru_lambda, m_sc_conv_w, m_lru_out_norm, m_sc_out_norm, m_w_out, m_ffn2_norm, m_ffn2_w_gate, m_ffn2_w_up, m_ffn2_w_down, m_final_norm, v_ffn1_norm, v_ffn1_w_gate, v_ffn1_w_up, v_ffn1_w_down, v_mix_norm, v_w_in, v_lru_conv_w, v_lru_conv_b, v_lru_w_a, v_lru_b_a, v_lru_w_i, v_lru_b_i, v_lru_lambda, v_sc_conv_w, v_lru_out_norm, v_sc_out_norm, v_w_out, v_ffn2_norm, v_ffn2_w_gate, v_ffn2_w_up, v_ffn2_w_down, v_final_norm):
    a = dict(locals())
    w = {n: a[n] for n in WEIGHT_NAMES}
    m = {n: a["m_" + n] for n in WEIGHT_NAMES}
    v = {n: a["v_" + n] for n in WEIGHT_NAMES}
    ax, ay, ac = _place()
    dev = 4 * ax + 2 * ay + ac
    chip = (2 * ax + ay).astype(jnp.int32).reshape(1)

    x0 = x[0]
    tgt = loss_target[0]
    T, D = x0.shape
    C = D // 2
    H, hd = lru_w_a.shape[1], lru_w_a.shape[2]
    CL = lru_conv_w.shape[2]

    shards = []
    for name, transposed in BIG:
        s = w[name][0]
        shards.append((s.T if transposed else s).astype(BF16))
    taps = jnp.concatenate([lru_conv_w[0], sc_conv_w[0], jnp.zeros((1, CL), F32)], axis=0)
    taps_row = lax.bitcast_convert_type(taps, BF16).reshape(1, -1)
    taps_blk = jnp.pad(taps_row, ((0, BF16_ROWS - 1), (0, D - taps_row.shape[1])))
    s_wg1, s_wu1, s_wd1, s_wg2, s_wu2, s_wd2, s_win, s_wout = shards
    slab_g1, pcs_g1 = _stack_rows([s_wg1])
    slab_u1, pcs_u1 = _stack_rows([s_wu1])
    slab_d1, pcs_d1 = _stack_rows([s_wd1])
    slab_mw, pcs_mw = _stack_rows([s_win, s_wout, taps_blk])
    slab_gu2, pcs_gu2 = _stack_rows([s_wg2, s_wu2])
    slab_d2, pcs_d2 = _stack_rows([s_wd2])
    (wg1,) = _allgather(slab_g1, pcs_g1, "allgather_ffn1_gate")

    g1, gm, g3 = ffn1_norm, mix_norm, ffn2_norm
    phase = _merge_phases(_ag_two_level_phase(slab_u1, pcs_u1), _ag_direct_phase(slab_d1, pcs_d1, SAME_CORE_AND_SIBLING))
    (n1, hg1), got = _norm_proj(x0, g1, [wg1], [BF16], False, "ffn1_gate", carried=phase)
    wu1, d1 = got[0], got[1:]
    phase = _merge_phases(_ag_forward_phase(d1, pcs_d1), _ag_direct_phase(slab_mw, pcs_mw, SAME_CORE_AND_SIBLING))
    (hu1, act1), got = _up_act(n1, wu1, hg1, "ffn1_up", carried=phase)
    wd1, mixw = got[0], got[1:]
    phase = _merge_phases(_ag_forward_phase(mixw, pcs_mw), _ag_direct_phase(slab_gu2, pcs_gu2, SAME_CORE_AND_SIBLING))
    x1, got = _mm_res(act1, wd1, x0, 0.5, "ffn1_down", carried=phase)
    (win, wout, taps_all), gu2 = got[:3], got[3:]
    phase = _merge_phases(_ag_forward_phase(gu2, pcs_gu2), _ag_direct_phase(slab_d2, pcs_d2, SAME_CORE_AND_SIBLING))
    (n2, z), got = _norm_proj(x1, gm, [win], [F32], False, "in_proj", carried=phase)
    (wg2, wu2), d2 = got[:2], got[2:]
    taps_all = taps_all.reshape(N_DEV, BF16_ROWS, D)[:, 0, :2 * SUBLANES * CL].reshape(N_DEV, SUBLANES, CL, 2)
    taps_all = lax.bitcast_convert_type(taps_all, F32)
    taps_all = taps_all.transpose(1, 0, 2).reshape(SUBLANES, N_DEV * CL)
    cw, sw = taps_all[0:4], taps_all[4:7]

    gf = final_norm.reshape(1, D)
    cb = lru_conv_b
    wa, wi = lru_w_a[0].astype(BF16), lru_w_i[0].astype(BF16)
    ba, bi = lru_b_a.reshape(1, C), lru_b_i.reshape(1, C)
    lam, glo, gso = lru_lambda, lru_out_norm, sc_out_norm

    y, h, *saved = _mix_fwd(z, cw, cb, wa, ba, wi, bi, lam, sw, glo, gso, "mix_fwd")
    x2, (wd2,) = _mm_res(y, wout, x1, 1.0, "out_proj", carried=_ag_forward_phase(d2, pcs_d2))
    n3, hg2, hu2, act2 = _norm_proj(x2, g3, [wg2, wu2], [BF16, BF16], True, "ffn2_up")
    x3 = _mm_res(act2, wd2, x2, 0.5, "ffn2_down")
    dx3, df2, d_gf, loss_blk = _loss_head(x3, gf, tgt, "loss_head")

    F = wd1.shape[0]
    bm_f = F // 4 if (F // 4) % LANES == 0 else 512

    def reduce_group(gs, tag):
        pcs, off = [], 0
        for g_ in gs:
            pcs.append((off, g_.shape[0] // N_DEV))
            off += g_.shape[0] // N_DEV
        sb_ = _rs_sibling(gs, pcs, "rs_sibling_add_" + tag)
        return sb_, pcs

    dhg2, dhu2 = _ffn_bwd_act(df2, wd2, hg2, hu2, "ffn2_bwd_act")
    d_wd2 = _dw_tn(act2, df2, bm_f, "ffn2_dw_down")
    d_wg2 = _dw_tn(dhg2, n3, bm_f, "ffn2_dw_gate")
    d_wu2 = _dw_tn(dhu2, n3, bm_f, "ffn2_dw_up")
    sb_f2, pcs_f2 = reduce_group([d_wg2, d_wu2, d_wd2], "ffn2")
    (dx2, dx2b, d_g3), (lb_f2,) = _mm_rmsbwd([(dhg2, wg2), (dhu2, wu2)], x2, g3, dx3, 1.0, "ffn2_bwd_in",
                                             carried=_rs_chips_phase(sb_f2))
    dy = _mm_nt(dx2b, wout, "out_proj_bwd")
    d_wout = _dw_tn(y, dx2b, 1024, "out_proj_dw")
    dz, small, d_wa, d_wi = _mix_bwd(z, h, saved, dy, cw, wa, wi, lam, sw, glo, gso, "mix_bwd")
    d_win = _dw_tn(dz, n2, 1280, "in_proj_dw")
    sb_mx, pcs_mx = reduce_group([d_win, d_wout], "mix")
    (dx1, df1, d_gm), (lb_mx,) = _mm_rmsbwd([(dz, win)], x1, gm, dx2, 0.5, "in_proj_bwd",
                                            carried=_rs_chips_phase(sb_mx))
    dhg1, dhu1 = _ffn_bwd_act(df1, wd1, hg1, hu1, "ffn1_bwd_act")
    d_wd1 = _dw_tn(act1, df1, bm_f, "ffn1_dw_down")
    d_wg1 = _dw_tn(dhg1, n1, bm_f, "ffn1_dw_gate")
    d_wu1 = _dw_tn(dhu1, n1, bm_f, "ffn1_dw_up")
    sb_f1, pcs_f1 = reduce_group([d_wg1, d_wu1, d_wd1], "ffn1")
    (dx0, _, d_g1), (lb_f1,) = _mm_rmsbwd([(dhg1, wg1), (dhu1, wu1)], x0, g1, dx1, 1.0, "ffn1_bwd_in",
                                          carried=_rs_chips_phase(sb_f1))

    big_sum = {}
    for tag, names, sb_, lb_, pcs in (("ffn2", ("ffn2_w_gate", "ffn2_w_up", "ffn2_w_down"), sb_f2, lb_f2, pcs_f2),
                                      ("mix", ("w_in", "w_out"), sb_mx, lb_mx, pcs_mx),
                                      ("ffn1", ("ffn1_w_gate", "ffn1_w_up", "ffn1_w_down"), sb_f1, lb_f1, pcs_f1)):
        gsum = _final_grad(sb_, lb_, chip, "rs_final_sum_" + tag)
        for name, (off, rows) in zip(names, pcs):
            big_sum[name] = gsum[off:off + rows]

    small_parts = [d_g1, d_gm, d_g3, d_gf, small[R_CW:R_CW + 4], small[R_CB], d_wa, small[R_BA], d_wi, small[R_BI],
                   small[R_LAM], small[R_SW:R_SW + 3], small[R_GLO], small[R_GSO]]
    sslab, counts = _pack_rows(small_parts, LANES)
    RS = sslab.shape[0]
    (sg,) = _allgather(sslab, [(0, RS)], "allgather_small_grads")
    ssum = _add_slabs([sg[j * RS:(j + 1) * RS] for j in range(N_DEV)], F32, "small_grads_sum")
    full_shapes = [(1, D), (1, D), (1, D), (D,), (1, 4, C), (1, C), (1, H, hd, hd), (1, H, hd), (1, H, hd, hd), (1, H, hd),
                   (1, C), (1, 3, C), (1, C), (1, C)]
    small_full = dict(zip(SMALL_NAMES, _unpack_rows(ssum, counts, full_shapes)))

    grads = {}
    for name, transposed in BIG:
        gblk = big_sum[name]
        grads[name] = (gblk.T if transposed else gblk)[None]
    for name in SMALL_NAMES:
        gfull = small_full[name]
        if name in ("lru_conv_w", "sc_conv_w"):
            gfull = lax.dynamic_slice_in_dim(gfull, dev * CL, CL, axis=2)
        grads[name] = gfull

    delta, new_m, new_v = {}, {}, {}
    for name, transposed in BIG:
        flip = transposed and w[name].shape[2] % LANES != 0
        view = (lambda t: t[0].T) if flip else (lambda t: t[0])
        back = (lambda t: t.T[None]) if flip else (lambda t: t[None])
        gview = big_sum[name] if flip else grads[name][0]
        d_, m_, v_ = _adamw(view(w[name]), gview, view(m[name]), view(v[name]), "adamw_" + name)
        delta[name], new_m[name], new_v[name] = back(d_), back(m_), back(v_)
    packs = [_pack_rows([t[n_] for n_ in SMALL_NAMES], LANES) for t in (w, grads, m, v)]
    sd, sm, sv = _adamw(packs[0][0], packs[1][0], packs[2][0], packs[3][0], "adamw_small")
    shapes = [w[n_].shape for n_ in SMALL_NAMES]
    for tgt_dict, slab_ in ((delta, sd), (new_m, sm), (new_v, sv)):
        for n_, val in zip(SMALL_NAMES, _unpack_rows(slab_, packs[0][1], shapes)):
            tgt_dict[n_] = val

    loss = lax.psum(loss_blk[0, 0], ("x", "y", "c"))
    return (loss, dx0[None], *[grads[n_] for n_ in WEIGHT_NAMES], *[delta[n_] for n_ in WEIGHT_NAMES],
            *[new_m[n_] for n_ in WEIGHT_NAMES], *[new_v[n_] for n_ in WEIGHT_NAMES])
```

```python
import functools
import math

import jax
import jax.numpy as jnp
from jax import lax
from jax.experimental import pallas as pl
from jax.experimental.pallas import tpu as pltpu

F32 = jnp.float32
BF16 = jnp.bfloat16
SDS = jax.ShapeDtypeStruct
MESH = pl.DeviceIdType.MESH

NORM_EPS = 1e-6
LRU_C = 8.0
N_DEV = 8
N_CHIP = 4
ADAM_LR, ADAM_B1, ADAM_B2, ADAM_EPS, ADAM_WD, ADAM_STEP = 0.001, 0.9, 0.999, 1e-08, 0.01, 10

NN = (((1,), (0,)), ((), ()))
NT = (((1,), (1,)), ((), ()))
TN = (((0,), (0,)), ((), ()))

SUBLANES = 8
BF16_ROWS = 16
LANES = 128
MIB = 1 << 20


def _dot(a, b, dims):
    return lax.dot_general(a, b, dims, preferred_element_type=F32)


def _blk(n, pref, align):
    if n <= pref:
        return n
    b = (pref // align) * align
    while b >= align:
        if n % b == 0:
            return b
        b -= align
    raise ValueError(f"no block of {n} aligned to {align} under {pref}")


def _cp(sem, vmem_mib):
    return pltpu.CompilerParams(dimension_semantics=sem, vmem_limit_bytes=vmem_mib * MIB)


HBM_SPEC = pl.BlockSpec(memory_space=pltpu.HBM)
MID_EIGHTHS = 5

class _Carried:
    def __init__(self, inputs, out_shapes, aliases, sem_shapes, build, has_mid=False):
        self.inputs, self.out_shapes, self.aliases = list(inputs), list(out_shapes), dict(aliases)
        self.sem_shapes, self.build, self.has_mid = list(sem_shapes), build, has_mid


def _call(body, *, name, grid, in_specs, out_specs, out_shape, scratch_shapes, compiler_params, args, carried=None):
    if carried is None:
        return pl.pallas_call(body, name=name, grid=grid, in_specs=in_specs, out_specs=out_specs, out_shape=out_shape,
                              scratch_shapes=scratch_shapes, compiler_params=compiler_params)(*args)
    n_in, n_out, n_sc = len(in_specs), len(out_shape), len(scratch_shapes)
    c_in, c_out = len(carried.inputs), len(carried.out_shapes)

    def hosted(*refs):
        ins, refs = refs[:n_in], refs[n_in:]
        c_ins, refs = refs[:c_in], refs[c_in:]
        outs, refs = refs[:n_out], refs[n_out:]
        c_outs, refs = refs[:c_out], refs[c_out:]
        scratch, c_sems = refs[:n_sc], refs[n_sc:]
        first = functools.reduce(jnp.logical_and, [pl.program_id(a) == 0 for a in range(len(grid))])
        last = functools.reduce(jnp.logical_and, [pl.program_id(a) == g - 1 for a, g in enumerate(grid)])

        @pl.when(first)
        def _():
            for start in carried.build(c_ins, c_outs, c_sems, "start"):
                start()

        if carried.has_mid:
            mid = functools.reduce(jnp.logical_and, [pl.program_id(0) == (grid[0] * MID_EIGHTHS) // 8]
                                   + [pl.program_id(a) == 0 for a in range(1, len(grid))])

            @pl.when(mid)
            def _():
                for step in carried.build(c_ins, c_outs, c_sems, "mid"):
                    step()

        body(*ins, *outs, *scratch)

        @pl.when(last)
        def _():
            for wait in carried.build(c_ins, c_outs, c_sems, "end"):
                wait()

    out = pl.pallas_call(
        hosted, name=name, grid=grid, in_specs=list(in_specs) + [HBM_SPEC] * c_in,
        out_specs=list(out_specs) + [HBM_SPEC] * c_out, out_shape=list(out_shape) + carried.out_shapes,
        scratch_shapes=list(scratch_shapes) + carried.sem_shapes,
        input_output_aliases={n_in + a: n_out + b for a, b in carried.aliases.items()},
        compiler_params=compiler_params)(*args, *carried.inputs)
    return out[:n_out], out[n_out:]


ROW_CHUNK = 128


def _chunk_rows(c):
    return pl.ds(pl.multiple_of(c * ROW_CHUNK, ROW_CHUNK), ROW_CHUNK)


def _rstd(xv):
    return lax.rsqrt(jnp.mean(xv * xv, axis=-1, keepdims=True) + NORM_EPS)


def _rms_bwd(xv, g, dn):
    r = _rstd(xv)
    xr = xv * r
    gd = g * dn
    dx = r * (gd - xr * jnp.mean(gd * xr, axis=-1, keepdims=True))
    return dx, jnp.sum(dn * xr, axis=0, keepdims=True)


def _log1p(e):
    u = 1.0 + e
    return jnp.where(u == 1.0, e, jnp.log(u) * (e / (u - 1.0)))


def _one_minus_exp(v, exp_half_v):
    series = 1.0 / 5040.0
    for coeff in (1.0 / 720.0, 1.0 / 120.0, 1.0 / 24.0, 1.0 / 6.0, 0.5, 1.0):
        series = series * v + coeff
    return jnp.where(v > -0.5, -v * series, 1.0 - exp_half_v * exp_half_v)


def _sigmoid(v):
    return 0.5 * jnp.tanh(0.5 * v) + 0.5


def _gelu_parts(g):
    k0 = math.sqrt(2.0 / math.pi)
    g2 = g * g
    t = jnp.tanh(k0 * (g + 0.044715 * g * g2))
    gel = 0.5 * g * (1.0 + t)
    gelp = 0.5 * (1.0 + t) + 0.5 * g * (1.0 - t * t) * (k0 * (1.0 + 3.0 * 0.044715 * g2))
    return gel, gelp


def _norm_proj(x, gain, w_list, out_dtypes, swiglu, name, carried=None):
    T, D = x.shape
    N = w_list[0].shape[0]
    nw = len(w_list)
    bm = _blk(T, 1024, BF16_ROWS)
    bn = _blk(N, 512, LANES)

    def body(*refs):
        x_ref, g_ref = refs[:2]
        w_refs = refs[2:2 + nw]
        n_ref = refs[2 + nw]
        o_refs = refs[3 + nw:3 + 2 * nw]
        act_ref = refs[3 + 2 * nw] if swiglu else None
        n_sc = refs[-1]

        @pl.when(pl.program_id(1) == 0)
        def _():
            def chunk(c, _):
                r = _chunk_rows(c)
                xv = x_ref[r, :]
                nb = (xv * _rstd(xv) * g_ref[...]).astype(BF16)
                n_sc[r, :] = nb
                n_ref[r, :] = nb
                return 0
            lax.fori_loop(0, bm // ROW_CHUNK, chunk, 0)

        n = n_sc[...]
        outs = [_dot(n, w_ref[...], NT) for w_ref in w_refs]
        for o_ref, o in zip(o_refs, outs):
            o_ref[...] = o.astype(o_ref.dtype)
        if swiglu:
            hg, hu = outs
            act_ref[...] = (hg * _sigmoid(hg) * hu).astype(BF16)

    row = pl.BlockSpec((bm, D), lambda i, j: (i, 0))
    tile = pl.BlockSpec((bm, bn), lambda i, j: (i, j))
    n_extra = 1 if swiglu else 0
    return _call(
        body, name=name, grid=(T // bm, N // bn),
        in_specs=[row, pl.BlockSpec((1, D), lambda i, j: (0, 0))] + [pl.BlockSpec((bn, D), lambda i, j: (j, 0))] * nw,
        out_specs=[row] + [tile] * (nw + n_extra),
        out_shape=[SDS((T, D), BF16)] + [SDS((T, N), dt) for dt in out_dtypes] + [SDS((T, N), BF16)] * n_extra,
        scratch_shapes=[pltpu.VMEM((bm, D), BF16)],
        compiler_params=_cp(("arbitrary", "arbitrary"), 52),
        args=(x, gain, *w_list), carried=carried)


def _up_act(n, wu, hg, name, carried=None):
    T, D = n.shape
    F = wu.shape[0]
    bm = _blk(T, 1024, BF16_ROWS)
    bn = _blk(F, 512, LANES)

    def body(n_ref, wu_ref, hg_ref, hu_ref, act_ref):
        hu = _dot(n_ref[...], wu_ref[...], NT)
        hg = hg_ref[...].astype(F32)
        hu_ref[...] = hu.astype(BF16)
        act_ref[...] = (hg * _sigmoid(hg) * hu).astype(BF16)

    tile = pl.BlockSpec((bm, bn), lambda i, j: (i, j))
    return _call(
        body, name=name, grid=(T // bm, F // bn),
        in_specs=[pl.BlockSpec((bm, D), lambda i, j: (i, 0)), pl.BlockSpec((bn, D), lambda i, j: (j, 0)), tile],
        out_specs=[tile, tile], out_shape=[SDS((T, F), BF16)] * 2, scratch_shapes=[],
        compiler_params=_cp(("arbitrary", "arbitrary"), 40),
        args=(n, wu, hg), carried=carried)


def _mm_res(a, b, x, scale, name, carried=None):
    T, K = a.shape
    D = b.shape[1]
    bm = _blk(T, 1024, BF16_ROWS)
    bk = _blk(K, 1408, LANES)
    nk = K // bk

    def body(a_ref, b_ref, x_ref, o_ref):
        k = pl.program_id(1)

        @pl.when(k == 0)
        def _():
            o_ref[...] = jnp.zeros_like(o_ref)

        o_ref[...] += _dot(a_ref[...], b_ref[...], NN)

        @pl.when(k == nk - 1)
        def _():
            def chunk(c, _):
                r = _chunk_rows(c)
                o_ref[r, :] = x_ref[r, :] + scale * o_ref[r, :]
                return 0
            lax.fori_loop(0, bm // ROW_CHUNK, chunk, 0)

    row = pl.BlockSpec((bm, D), lambda i, k: (i, 0))
    out = _call(
        body, name=name, grid=(T // bm, nk),
        in_specs=[pl.BlockSpec((bm, bk), lambda i, k: (i, k)), pl.BlockSpec((bk, D), lambda i, k: (k, 0)), row],
        out_specs=[row], out_shape=[SDS((T, D), F32)], scratch_shapes=[],
        compiler_params=_cp(("arbitrary", "arbitrary"), 56),
        args=(a, b, x), carried=carried)
    return out[0] if carried is None else (out[0][0], out[1])


def _mm_nt(a, b, name):
    T, K = a.shape
    N = b.shape[0]
    bm = _blk(T, 1024, BF16_ROWS)
    bn = _blk(N, 512, LANES)

    def body(a_ref, b_ref, o_ref):
        o_ref[...] = _dot(a_ref[...], b_ref[...], NT)

    return pl.pallas_call(
        body, name=name, grid=(T // bm, N // bn),
        in_specs=[pl.BlockSpec((bm, K), lambda i, j: (i, 0)), pl.BlockSpec((bn, K), lambda i, j: (j, 0))],
        out_specs=pl.BlockSpec((bm, bn), lambda i, j: (i, j)), out_shape=SDS((T, N), F32),
        compiler_params=_cp(("arbitrary", "arbitrary"), 40),
    )(a, b)


def _ffn_bwd_act(dfb, wd, hg, hu, name):
    T, D = dfb.shape
    F = wd.shape[0]
    bm = _blk(T, 1024, BF16_ROWS)
    bn = _blk(F, 512, LANES)

    def body(df_ref, wd_ref, hg_ref, hu_ref, dhg_ref, dhu_ref):
        dact = _dot(df_ref[...], wd_ref[...], NT)
        hgv = hg_ref[...].astype(F32)
        huv = hu_ref[...].astype(F32)
        s = _sigmoid(hgv)
        dhu_ref[...] = (dact * (hgv * s)).astype(BF16)
        dhg_ref[...] = (dact * huv * (s * (1.0 + hgv * (1.0 - s)))).astype(BF16)

    tile = pl.BlockSpec((bm, bn), lambda i, j: (i, j))
    return pl.pallas_call(
        body, name=name, grid=(T // bm, F // bn),
        in_specs=[pl.BlockSpec((bm, D), lambda i, j: (i, 0)), pl.BlockSpec((bn, D), lambda i, j: (j, 0)), tile, tile],
        out_specs=[tile, tile], out_shape=[SDS((T, F), BF16)] * 2,
        compiler_params=_cp(("arbitrary", "arbitrary"), 40),
    )(dfb, wd, hg, hu)


def _dw_tn(a, b, bm_pref, name):
    T, M = a.shape
    N = b.shape[1]
    bm = _blk(M, bm_pref, LANES)
    tk = _blk(T, 1024, BF16_ROWS)
    nk = T // tk

    def body(a_ref, b_ref, o_ref, acc):
        k = pl.program_id(1)

        @pl.when(k == 0)
        def _():
            acc[...] = jnp.zeros_like(acc)

        acc[...] += _dot(a_ref[...], b_ref[...], TN)

        @pl.when(k == nk - 1)
        def _():
            o_ref[...] = acc[...].astype(BF16)

    return pl.pallas_call(
        body, name=name, grid=(M // bm, nk),
        in_specs=[pl.BlockSpec((tk, bm), lambda i, k: (k, i)), pl.BlockSpec((tk, N), lambda i, k: (k, 0))],
        out_specs=pl.BlockSpec((bm, N), lambda i, k: (i, 0)), out_shape=SDS((M, N), BF16),
        scratch_shapes=[pltpu.VMEM((bm, N), F32)],
        compiler_params=_cp(("arbitrary", "arbitrary"), 48),
    )(a, b)


def _mm_rmsbwd(pairs, x, gain, dx_in, bscale, name, carried=None):
    T, D = x.shape
    K = pairs[0][0].shape[1]
    npair = len(pairs)
    bm = _blk(T, 1024, BF16_ROWS)
    bk = _blk(K, 1024 // npair, LANES)
    nk = K // bk

    nchunk = bm // ROW_CHUNK

    def body(*refs):
        ab = refs[:2 * npair]
        x_hbm, g_ref, dxin_hbm, dx_ref, dxb_ref, dg_ref, x_buf, dxin_buf, sems = refs[2 * npair:]
        i = pl.program_id(0)
        k = pl.program_id(1)

        def fetch(c, slot):
            rows = pl.ds(i * bm + c * ROW_CHUNK, ROW_CHUNK)
            return (pltpu.make_async_copy(x_hbm.at[rows, :], x_buf.at[slot], sems.at[slot, 0]),
                    pltpu.make_async_copy(dxin_hbm.at[rows, :], dxin_buf.at[slot], sems.at[slot, 1]))

        @pl.when(k == 0)
        def _():
            dx_ref[...] = jnp.zeros_like(dx_ref)

        @pl.when(k == nk - 1)
        def _():
            for cp in fetch(0, 0):
                cp.start()

        for q in range(npair):
            dx_ref[...] += _dot(ab[2 * q][...], ab[2 * q + 1][...], NN)

        @pl.when(k == nk - 1)
        def _():
            @pl.when(i == 0)
            def _():
                dg_ref[...] = jnp.zeros_like(dg_ref)

            def chunk(c, _):
                slot = c % 2

                @pl.when(c + 1 < nchunk)
                def _():
                    for cp in fetch(c + 1, 1 - slot):
                        cp.start()

                for cp in fetch(c, slot):
                    cp.wait()

                r = _chunk_rows(c)
                dx, dg = _rms_bwd(x_buf[slot], g_ref[...], dx_ref[r, :])
                dxo = dxin_buf[slot] + dx
                dx_ref[r, :] = dxo
                dxb_ref[r, :] = (bscale * dxo).astype(BF16)
                dg_ref[...] += dg
                return 0
            lax.fori_loop(0, nchunk, chunk, 0)

    row = pl.BlockSpec((bm, D), lambda i, k: (i, 0))
    anywhere = pl.BlockSpec(memory_space=pl.ANY)
    vec = pl.BlockSpec((1, D), lambda i, k: (0, 0))
    in_specs = []
    args = []
    for a, b in pairs:
        in_specs += [pl.BlockSpec((bm, bk), lambda i, k: (i, k)), pl.BlockSpec((bk, D), lambda i, k: (k, 0))]
        args += [a, b]
    return _call(
        body, name=name, grid=(T // bm, nk),
        in_specs=in_specs + [anywhere, vec, anywhere], out_specs=[row, row, vec],
        out_shape=[SDS((T, D), F32), SDS((T, D), BF16), SDS((1, D), F32)],
        scratch_shapes=[pltpu.VMEM((2, ROW_CHUNK, D), F32), pltpu.VMEM((2, ROW_CHUNK, D), F32),
                        pltpu.SemaphoreType.DMA((2, 2))],
        compiler_params=_cp(("arbitrary", "arbitrary"), 52),
        args=(*args, x, gain, dx_in), carried=carried)


def _loss_head(x3, gain, tgt, name):
    T, D = x3.shape
    bm = _blk(T, 256, BF16_ROWS)

    def body(x_ref, g_ref, t_ref, dx_ref, dxb_ref, dg_ref, loss_ref):
        i = pl.program_id(0)
        xv = x_ref[...]
        g = g_ref[...]
        out = xv * _rstd(xv) * g
        e = out - t_ref[...]
        part = 0.5 * jnp.sum(jnp.mean(e * e, axis=-1, keepdims=True), axis=0, keepdims=True)
        dx, dg = _rms_bwd(xv, g, e * (1.0 / D))
        dx_ref[...] = dx
        dxb_ref[...] = (0.5 * dx).astype(BF16)

        @pl.when(i == 0)
        def _():
            dg_ref[...] = dg
            loss_ref[...] = jnp.broadcast_to(part, loss_ref.shape)

        @pl.when(i > 0)
        def _():
            dg_ref[...] += dg
            loss_ref[...] += jnp.broadcast_to(part, loss_ref.shape)

    row = pl.BlockSpec((bm, D), lambda i: (i, 0))
    vec = pl.BlockSpec((1, D), lambda i: (0, 0))
    return pl.pallas_call(
        body, name=name, grid=(T // bm,),
        in_specs=[row, vec, row], out_specs=[row, row, vec, pl.BlockSpec((SUBLANES, LANES), lambda i: (0, 0))],
        out_shape=[SDS((T, D), F32), SDS((T, D), BF16), SDS((1, D), F32), SDS((SUBLANES, LANES), F32)],
        compiler_params=_cp(("arbitrary",), 40),
    )(x3, gain, tgt)


R_CW, R_CB, R_BA, R_BI, R_LAM, R_SW, R_GLO, R_GSO, SMALL_ROWS = 0, 4, 5, 6, 7, 8, 11, 12, 16


def _rows(g):
    return pl.ds(pl.multiple_of(g * SUBLANES, SUBLANES), SUBLANES)


def _shift_back(prev, cur, d):
    row = lax.broadcasted_iota(jnp.int32, cur.shape, 0)
    return pltpu.roll(jnp.where(row >= SUBLANES - d, prev, cur), d, 0)


def _shift_fwd(cur, nxt, d):
    row = lax.broadcasted_iota(jnp.int32, cur.shape, 0)
    return pltpu.roll(jnp.where(row < d, nxt, cur), SUBLANES - d, 0)


def _causal_conv(ext, g, taps_ref, ntap):
    prev = ext[_rows(g), :]
    cur = ext[_rows(g + 1), :]
    out = _shift_back(prev, cur, ntap - 1) * taps_ref[0:1, :]
    for k in range(1, ntap - 1):
        out = out + _shift_back(prev, cur, ntap - 1 - k) * taps_ref[k:k + 1, :]
    return out + cur * taps_ref[ntap - 1:ntap, :]


def _scan8(A, U, reverse):
    row = lax.broadcasted_iota(jnp.int32, A.shape, 0)
    for s in (1, 2, 4):
        if reverse:
            A_sh = pltpu.roll(A, SUBLANES - s, 0)
            U_sh = pltpu.roll(U, SUBLANES - s, 0)
            m = row < SUBLANES - s
        else:
            A_sh = pltpu.roll(A, s, 0)
            U_sh = pltpu.roll(U, s, 0)
            m = row >= s
        U = jnp.where(m, A * U_sh + U, U)
        A = jnp.where(m, A * A_sh, A)
    return A, U


def _gate_pre(xc_s, w_ref, out_s, H, hd):
    for h in range(H):
        cs = slice(h * hd, (h + 1) * hd)
        out_s[:, cs] = _dot(xc_s[:, cs].astype(BF16), w_ref[h], NN)


def _lru_coeffs(pa, pi, xc, ba, bi, sp):
    ra = _sigmoid(pa + ba)
    ri = _sigmoid(pi + bi)
    log_a = (-LRU_C * ra) * sp
    a = jnp.exp(log_a)
    mult = jnp.sqrt(_one_minus_exp(2.0 * log_a, a))
    return ra, ri, a, mult


def _softplus_neg(lam):
    v = -lam
    return jnp.maximum(v, 0.0) + _log1p(jnp.exp(-jnp.abs(v)))


def _mix_fwd(z, cw, cb, wa, ba, wi, bi, lam, sw, glo, gso, name):
    T = z.shape[0]
    C = z.shape[1] // 5
    H = wa.shape[0]
    hd = C // H
    tb = _blk(T, 256, BF16_ROWS)
    ng = tb // SUBLANES
    HDR = SUBLANES

    def body(z_ref, cw_ref, cb_ref, wa_ref, ba_ref, wi_ref, bi_ref, lam_ref, sw_ref, glo_ref, gso_ref,
             y_ref, h_ref, ra_ref, ri_ref, a_ref, m_ref, xc_s, q_ref, xext, pext, pa_s, pi_s, y_s, hcar):
        @pl.when(pl.program_id(0) == 0)
        def _():
            xext[0:HDR, :] = jnp.zeros((HDR, C), F32)
            pext[0:HDR, :] = jnp.zeros((HDR, C), F32)
            hcar[...] = jnp.zeros_like(hcar)

        def fill(g, _):
            r = _rows(g)
            re = _rows(g + 1)
            xext[re, :] = z_ref[r, 0:C]
            pext[re, :] = z_ref[r, 3 * C:4 * C] * z_ref[r, 4 * C:5 * C]
            return 0
        lax.fori_loop(0, ng, fill, 0)

        def conv(g, _):
            xc_s[_rows(g), :] = _causal_conv(xext, g, cw_ref, 4) + cb_ref[...]
            return 0
        lax.fori_loop(0, ng, conv, 0)

        _gate_pre(xc_s, wa_ref, pa_s, H, hd)
        _gate_pre(xc_s, wi_ref, pi_s, H, hd)
        sp = _softplus_neg(lam_ref[...])

        def group(g, hprev):
            r = _rows(g)
            xc = xc_s[r, :]
            ra, ri, a, mult = _lru_coeffs(pa_s[r, :], pi_s[r, :], xc, ba_ref[...], bi_ref[...], sp)
            ra_ref[r, :] = ra
            ri_ref[r, :] = ri
            a_ref[r, :] = a
            m_ref[r, :] = mult
            A, U = _scan8(a, mult * (ri * xc), reverse=False)
            hh = A * hprev + U
            h_ref[r, :] = hh
            gel, _ = _gelu_parts(z_ref[r, C:2 * C])
            y_lru = hh * gel
            y_s[r, 0:C] = y_lru * _rstd(y_lru) * glo_ref[...]
            q = _causal_conv(pext, g, sw_ref, 3)
            q_ref[r, :] = q
            y_sc = z_ref[r, 2 * C:3 * C] * q
            y_s[r, C:2 * C] = y_sc * _rstd(y_sc) * gso_ref[...]
            return jnp.broadcast_to(hh[SUBLANES - 1:SUBLANES, :], hh.shape)
        hcar[...] = lax.fori_loop(0, ng // 2, lambda t, hp: group(2 * t + 1, group(2 * t, hp)), hcar[...])

        xext[0:HDR, :] = xext[tb:tb + HDR, :]
        pext[0:HDR, :] = pext[tb:tb + HDR, :]

        def cast(g, _):
            r = pl.ds(pl.multiple_of(g * BF16_ROWS, BF16_ROWS), BF16_ROWS)
            y_ref[r, :] = y_s[r, :].astype(BF16)
            return 0
        lax.fori_loop(0, tb // BF16_ROWS, cast, 0)

    full = lambda shape: pl.BlockSpec(shape, lambda i: (0,) * len(shape))
    blk = lambda w: pl.BlockSpec((tb, w), lambda i: (i, 0))
    ext = pltpu.VMEM((tb + HDR, C), F32)
    tile = pltpu.VMEM((tb, C), F32)
    return pl.pallas_call(
        body, name=name, grid=(T // tb,),
        in_specs=[blk(5 * C), full((4, C)), full((1, C)), full((H, hd, hd)), full((1, C)), full((H, hd, hd)),
                  full((1, C)), full((1, C)), full((3, C)), full((1, C)), full((1, C))],
        out_specs=[blk(2 * C)] + [blk(C)] * 7,
        out_shape=[SDS((T, 2 * C), BF16)] + [SDS((T, C), F32)] * 7,
        scratch_shapes=[ext, ext, tile, tile, pltpu.VMEM((tb, 2 * C), F32), pltpu.VMEM((SUBLANES, C), F32)],
        compiler_params=_cp(("arbitrary",), 48),
    )(z, cw, cb, wa, ba, wi, bi, lam, sw, glo, gso)


def _mix_bwd(z, h, saved, dy, cw, wa, wi, lam, sw, glo, gso, name):
    T = z.shape[0]
    C = z.shape[1] // 5
    H = wa.shape[0]
    hd = C // H
    tb = _blk(T, 256, BF16_ROWS)
    nb = T // tb
    ng = tb // SUBLANES
    HDR = SUBLANES
    N_ACC = 13

    def body(z_ref, zp_ref, h_ref, hp_ref, ra_ref, ri_ref, a_ref, m_ref, xc_s, q_ref, dy_ref, cw_ref, wa_ref, wi_ref,
             lam_ref, sw_ref, glo_ref, gso_ref, dz_ref, small_ref, dwa_ref, dwi_ref,
             xext, pext, hext, dqext, dxcext, bext, dh_s, dpa_s, dpi_s, dz_s, acc_s, bcar):
        i = pl.program_id(0)
        first_rows = i == nb - 1

        @pl.when(i == 0)
        def _():
            dqext[tb:tb + HDR, :] = jnp.zeros((HDR, C), F32)
            dxcext[tb:tb + HDR, :] = jnp.zeros((HDR, C), F32)
            bcar[...] = jnp.zeros_like(bcar)
            acc_s[...] = jnp.zeros_like(acc_s)
            dwa_ref[...] = jnp.zeros_like(dwa_ref)
            dwi_ref[...] = jnp.zeros_like(dwi_ref)

        zero = jnp.zeros((HDR, C), F32)
        xext[0:HDR, :] = jnp.where(first_rows, zero, zp_ref[:, 0:C])
        pext[0:HDR, :] = jnp.where(first_rows, zero, zp_ref[:, 3 * C:4 * C] * zp_ref[:, 4 * C:5 * C])
        hext[0:HDR, :] = jnp.where(first_rows, zero, hp_ref[...])

        def fill(g, _):
            r = _rows(g)
            re = _rows(g + 1)
            xext[re, :] = z_ref[r, 0:C]
            pext[re, :] = z_ref[r, 3 * C:4 * C] * z_ref[r, 4 * C:5 * C]
            hext[re, :] = h_ref[r, :]
            return 0
        lax.fori_loop(0, ng, fill, 0)

        sp = _softplus_neg(lam_ref[...])
        dsp_dlam = -jax.nn.sigmoid(-lam_ref[...])

        def add_acc(k, v):
            acc_s[k] += v

        def p1(g, _):
            r = _rows(g)
            hh = h_ref[r, :]
            gel, gelp = _gelu_parts(z_ref[r, C:2 * C])
            y_lru = hh * gel
            dnl = dy_ref[r, 0:C]
            rl = _rstd(y_lru)
            ylr = y_lru * rl
            gd = glo_ref[...] * dnl
            dy_lru = rl * (gd - ylr * jnp.mean(gd * ylr, axis=-1, keepdims=True))
            add_acc(R_GLO, dnl * ylr)
            dz_s[r, C:2 * C] = dy_lru * hh * gelp
            dh = dy_lru * gel
            dh_s[r, :] = dh

            q = q_ref[r, :]
            scb = z_ref[r, 2 * C:3 * C]
            y_sc = scb * q
            dns = dy_ref[r, C:2 * C]
            rs = _rstd(y_sc)
            ysr = y_sc * rs
            gs = gso_ref[...] * dns
            dy_sc = rs * (gs - ysr * jnp.mean(gs * ysr, axis=-1, keepdims=True))
            add_acc(R_GSO, dns * ysr)
            dz_s[r, 2 * C:3 * C] = dy_sc * q
            dqext[r, :] = dy_sc * scb
            return 0
        lax.fori_loop(0, ng, p1, 0, unroll=2)

        bext[tb:tb + HDR, :] = bcar[...]

        def p2(j, carry):
            g = ng - 1 - j
            r = _rows(g)
            a = a_ref[r, :]
            A, U = _scan8(a, a * dh_s[r, :], reverse=True)
            bb = A * carry + U
            bext[r, :] = bb
            return jnp.broadcast_to(bb[0:1, :], bb.shape)
        bcar[...] = lax.fori_loop(0, ng, p2, bcar[...])

        def p3(g, _):
            r = _rows(g)
            rn = _rows(g + 1)
            G = dh_s[r, :] + _shift_fwd(bext[r, :], bext[rn, :], 1)
            hm1 = _shift_back(hext[r, :], hext[rn, :], 1)
            a = a_ref[r, :]
            mult = m_ref[r, :]
            ri = ri_ref[r, :]
            xc = xc_s[r, :]
            ra = ra_ref[r, :]
            dxcext[r, :] = G * mult * ri
            dri = G * mult * xc
            dmult = G * ri * xc
            dlog_a = (G * hm1) * a - dmult * (a * a) / mult
            add_acc(R_LAM, dlog_a * (-LRU_C * ra) * dsp_dlam)
            dpa = dlog_a * (-LRU_C * sp) * ra * (1.0 - ra)
            dpi = dri * ri * (1.0 - ri)
            add_acc(R_BA, dpa)
            add_acc(R_BI, dpi)
            dpa_s[r, :] = dpa
            dpi_s[r, :] = dpi
            return 0
        lax.fori_loop(0, ng, p3, 0)

        for hh_ in range(H):
            cs = slice(hh_ * hd, (hh_ + 1) * hd)
            dpa_b = dpa_s[:, cs].astype(BF16)
            dpi_b = dpi_s[:, cs].astype(BF16)
            xc_b = xc_s[:, cs].astype(BF16)
            dxcext[0:tb, cs] += _dot(dpa_b, wa_ref[hh_], NT) + _dot(dpi_b, wi_ref[hh_], NT)
            dwa_ref[hh_] += _dot(xc_b, dpa_b, TN)
            dwi_ref[hh_] += _dot(xc_b, dpi_b, TN)

        def p4(g, _):
            r = _rows(g)
            rn = _rows(g + 1)
            dxc = dxcext[r, :]
            dxc_n = dxcext[rn, :]
            x_p = xext[r, :]
            x_c = xext[rn, :]
            add_acc(R_CB, dxc)
            dlx = dxc * cw_ref[3:4, :]
            add_acc(R_CW + 3, dxc * x_c)
            for d in range(1, 4):
                dlx = dlx + _shift_fwd(dxc, dxc_n, d) * cw_ref[3 - d:4 - d, :]
                add_acc(R_CW + 3 - d, dxc * _shift_back(x_p, x_c, d))
            dz_s[r, 0:C] = dlx

            dq = dqext[r, :]
            dq_n = dqext[rn, :]
            p_p = pext[r, :]
            p_c = pext[rn, :]
            dp = dq * sw_ref[2:3, :]
            add_acc(R_SW + 2, dq * p_c)
            for d in range(1, 3):
                dp = dp + _shift_fwd(dq, dq_n, d) * sw_ref[2 - d:3 - d, :]
                add_acc(R_SW + 2 - d, dq * _shift_back(p_p, p_c, d))
            dz_s[r, 3 * C:4 * C] = dp * z_ref[r, 4 * C:5 * C]
            dz_s[r, 4 * C:5 * C] = dp * z_ref[r, 3 * C:4 * C]
            return 0
        lax.fori_loop(0, ng, p4, 0)

        dqext[tb:tb + HDR, :] = dqext[0:HDR, :]
        dxcext[tb:tb + HDR, :] = dxcext[0:HDR, :]

        def cast(g, _):
            r = pl.ds(pl.multiple_of(g * BF16_ROWS, BF16_ROWS), BF16_ROWS)
            dz_ref[r, :] = dz_s[r, :].astype(BF16)
            return 0
        lax.fori_loop(0, tb // BF16_ROWS, cast, 0)

        @pl.when(i == nb - 1)
        def _():
            small_ref[...] = jnp.zeros_like(small_ref)
            for k in range(N_ACC):
                small_ref[k:k + 1, :] = jnp.sum(acc_s[k], axis=0, keepdims=True)

    tpg = tb // SUBLANES
    full = lambda shape: pl.BlockSpec(shape, lambda i: (0,) * len(shape))
    blk = lambda w: pl.BlockSpec((tb, w), lambda i: (nb - 1 - i, 0))
    prev = lambda w: pl.BlockSpec((SUBLANES, w), lambda i: (jnp.maximum((nb - 1 - i) * tpg - 1, 0), 0))
    ext = pltpu.VMEM((tb + HDR, C), F32)
    tile = pltpu.VMEM((tb, C), F32)
    return pl.pallas_call(
        body, name=name, grid=(nb,),
        in_specs=[blk(5 * C), prev(5 * C), blk(C), prev(C)] + [blk(C)] * 6
        + [blk(2 * C), full((4, C)), full((H, hd, hd)), full((H, hd, hd)), full((1, C)), full((3, C)),
           full((1, C)), full((1, C))],
        out_specs=[blk(5 * C), full((SMALL_ROWS, C)), full((H, hd, hd)), full((H, hd, hd))],
        out_shape=[SDS((T, 5 * C), BF16), SDS((SMALL_ROWS, C), F32), SDS((H, hd, hd), F32), SDS((H, hd, hd), F32)],
        scratch_shapes=[ext] * 6 + [tile] * 3 + [pltpu.VMEM((tb, 5 * C), F32), pltpu.VMEM((N_ACC, SUBLANES, C), F32),
                                                pltpu.VMEM((SUBLANES, C), F32)],
        compiler_params=_cp(("arbitrary",), 56),
    )(z, z, h, h, *saved, dy, cw, wa, wi, lam, sw, glo, gso)


def _add_slabs(terms, out_dtype, name):
    R, Ccols = terms[0].shape
    br = _blk(R, 512, BF16_ROWS)
    n = len(terms)

    def body(*refs):
        s = refs[0][...].astype(F32)
        for t_ref in refs[1:n]:
            s = s + t_ref[...].astype(F32)
        refs[n][...] = s.astype(out_dtype)

    spec = pl.BlockSpec((br, Ccols), lambda i: (i, 0))
    return pl.pallas_call(
        body, name=name, grid=(R // br,), in_specs=[spec] * n, out_specs=spec, out_shape=SDS((R, Ccols), out_dtype),
        compiler_params=_cp(("arbitrary",), 40),
    )(*terms)


def _reduced_rows(sb, lb, off, rows, extra_in, n_out, body, name, chip):
    Ccols = sb.shape[2]
    br = _blk(math.gcd(off, rows) if off else rows, 192, BF16_ROWS)
    ob = off // br
    src = lambda pick: pl.BlockSpec((1, br, Ccols), lambda i, c: (pick(c), ob + i, 0))
    own = pl.BlockSpec((br, Ccols), lambda i, c: (i, 0))
    return pl.pallas_call(
        body, name=name,
        grid_spec=pltpu.PrefetchScalarGridSpec(
            num_scalar_prefetch=1, grid=(rows // br,),
            in_specs=[src(lambda c: c[0]), src(lambda c: 0), src(lambda c: 1), src(lambda c: 2)] + [own] * len(extra_in),
            out_specs=[own] * n_out),
        out_shape=[SDS((rows, Ccols), F32)] * n_out,
        compiler_params=_cp(("arbitrary",), 40),
    )(chip, sb, lb, lb, lb, *extra_in)


def _sum4(sb_ref, l0, l1, l2):
    s = sb_ref[0].astype(F32)
    for t_ref in (l0, l1, l2):
        s = s + t_ref[0].astype(F32)
    return s


def _final_grad(sb, lb, chip, off, rows, name):
    def body(chip_ref, sb_ref, l0, l1, l2, o_ref):
        o_ref[...] = _sum4(sb_ref, l0, l1, l2)

    return _reduced_rows(sb, lb, off, rows, [], 1, body, name, chip)[0]


def _adamw_math(w, g, m, v):
    nm = ADAM_B1 * m + (1.0 - ADAM_B1) * g
    nv = ADAM_B2 * v + (1.0 - ADAM_B2) * (g * g)
    c1 = 1.0 - ADAM_B1 ** ADAM_STEP
    c2 = 1.0 - ADAM_B2 ** ADAM_STEP
    return -ADAM_LR * ((nm / c1) / (jnp.sqrt(nv / c2) + ADAM_EPS) + ADAM_WD * w), nm, nv


def _sum_adamw(sb, lb, chip, off, rows, w, m, v, name):
    def body(chip_ref, sb_ref, l0, l1, l2, w_ref, m_ref, v_ref, g_ref, d_ref, nm_ref, nv_ref):
        g = _sum4(sb_ref, l0, l1, l2)
        g_ref[...] = g
        d_ref[...], nm_ref[...], nv_ref[...] = _adamw_math(w_ref[...], g, m_ref[...], v_ref[...])

    return _reduced_rows(sb, lb, off, rows, [w, m, v], 4, body, name, chip)


def _adamw(w, g, m, v, name):
    R, Ccols = w.shape
    br = _blk(R, 256, SUBLANES)

    def body(w_ref, g_ref, m_ref, v_ref, d_ref, nm_ref, nv_ref):
        d_ref[...], nm_ref[...], nv_ref[...] = _adamw_math(w_ref[...], g_ref[...], m_ref[...], v_ref[...])

    spec = pl.BlockSpec((br, Ccols), lambda i: (i, 0))
    return pl.pallas_call(
        body, name=name, grid=(R // br,), in_specs=[spec] * 4, out_specs=[spec] * 3,
        out_shape=[SDS((R, Ccols), F32)] * 3, compiler_params=_cp(("arbitrary",), 40),
    )(w, g, m, v)


def _place():
    return lax.axis_index("x"), lax.axis_index("y"), lax.axis_index("c")


def _dev_rows(ref, dev, rows):
    return ref.at[pl.ds((4 * dev[0] + 2 * dev[1] + dev[2]) * rows, rows), :]


def _remote(src, dst, send_sem, recv_sem, to):
    return pltpu.make_async_remote_copy(src_ref=src, dst_ref=dst, send_sem=send_sem, recv_sem=recv_sem,
                                        device_id=to, device_id_type=MESH)


SAME_CORE_AND_SIBLING = ((0, 0, 1), (1, 0, 0), (0, 1, 0), (1, 1, 0))


def _merge_phases(a, b):
    na_in, na_out, na_sem = len(a.inputs), len(a.out_shapes), len(a.sem_shapes)

    def build(ins, outs, sems, stage):
        return (a.build(ins[:na_in], outs[:na_out], sems[:na_sem], stage)
                + b.build(ins[na_in:], outs[na_out:], sems[na_sem:], stage))

    aliases = dict(a.aliases)
    aliases.update({na_in + i: na_out + o for i, o in b.aliases.items()})
    return _Carried(a.inputs + b.inputs, a.out_shapes + b.out_shapes, aliases, a.sem_shapes + b.sem_shapes, build,
                    has_mid=a.has_mid or b.has_mid)


def _ag_direct_phase(slab, pieces, flips):
    W = slab.shape[1]
    n = len(pieces)
    npeer = len(flips)

    def build(ins, outs, sems, stage):
        if stage == "mid":
            return []
        starting = stage == "start"
        (slab_ref,) = ins
        send_sems, recv_sems, local_sems = sems
        x, y, c = _place()
        me = (x, y, c)
        peers = [tuple(1 - v if f else v for v, f in zip(me, flip)) for flip in flips]
        todo = []
        for p, (off, rows) in enumerate(pieces):
            src = slab_ref.at[pl.ds(off, rows), :]
            mine = pltpu.make_async_copy(src, _dev_rows(outs[p], me, rows), local_sems.at[p])
            todo.append(mine.start if starting else mine.wait)
            for k, peer in enumerate(peers):
                snd = _remote(src, _dev_rows(outs[p], me, rows), send_sems.at[k, p], recv_sems.at[k, p], peer)
                if starting:
                    todo.append(snd.start)
                else:
                    theirs = _dev_rows(outs[p], peer, rows)
                    rcv = _remote(theirs, theirs, send_sems.at[k, p], recv_sems.at[k, p], me)
                    todo += [rcv.wait_recv, snd.wait_send]
        return todo

    dma = pltpu.SemaphoreType.DMA
    return _Carried([slab], [SDS((N_DEV * rows, W), slab.dtype) for _, rows in pieces], {},
                    [dma((npeer, n)), dma((npeer, n)), dma((n,))], build)


def _ag_two_level_phase(slab, pieces):
    W = slab.shape[1]
    n = len(pieces)

    def build(ins, outs, sems, stage):
        (slab_ref,) = ins
        send_sems, recv_sems, local_sems = sems
        x, y, c = _place()
        me, sibling = (x, y, c), (x, y, 1 - c)
        chips = [(1 - x, y), (x, 1 - y), (1 - x, 1 - y)]
        todo = []
        for p, (off, rows) in enumerate(pieces):
            src = slab_ref.at[pl.ds(off, rows), :]
            own = _dev_rows(outs[p], me, rows)
            landed = [_dev_rows(outs[p], (*chip, c), rows) for chip in chips]

            def mine():
                return pltpu.make_async_copy(src, own, local_sems.at[p])

            def first():
                return [_remote(src, own, send_sems.at[k, p], recv_sems.at[k, p], to)
                        for k, to in enumerate([sibling] + [(*chip, c) for chip in chips])]

            def passed():
                return [_remote(blk, blk, send_sems.at[4 + j, p], recv_sems.at[4 + j, p], sibling)
                        for j, blk in enumerate(landed)]

            def arrival(k, blk):
                return _remote(blk, blk, send_sems.at[k, p], recv_sems.at[k, p], me).wait_recv

            if stage == "start":
                todo += [mine().start] + [cp.start for cp in first()]
            elif stage == "mid":
                for j, (blk, fwd) in enumerate(zip(landed, passed())):
                    todo += [arrival(1 + j, blk), fwd.start]
            else:
                theirs = [_dev_rows(outs[p], sibling, rows)] + [_dev_rows(outs[p], (*chip, 1 - c), rows) for chip in chips]
                todo += [arrival(k, blk) for k, blk in zip((0, 4, 5, 6), theirs)]
                todo += [cp.wait_send for cp in first() + passed()] + [mine().wait]
        return todo

    dma = pltpu.SemaphoreType.DMA
    return _Carried([slab], [SDS((N_DEV * rows, W), slab.dtype) for _, rows in pieces], {},
                    [dma((7, n)), dma((7, n)), dma((n,))], build, has_mid=True)


def _ag_forward_phase(gathered, pieces):
    n = len(pieces)

    def build(ins, outs, sems, stage):
        if stage == "mid":
            return []
        starting = stage == "start"
        send_sems, recv_sems = sems
        x, y, c = _place()
        me, sibling = (x, y, c), (x, y, 1 - c)
        chips = [(1 - x, y), (x, 1 - y), (1 - x, 1 - y)]
        todo = []
        for p, (_, rows) in enumerate(pieces):
            for j, chip in enumerate(chips):
                snd = _remote(_dev_rows(ins[p], (*chip, c), rows), _dev_rows(outs[p], (*chip, c), rows),
                              send_sems.at[j, p], recv_sems.at[j, p], sibling)
                if starting:
                    todo.append(snd.start)
                else:
                    theirs = _dev_rows(outs[p], (*chip, 1 - c), rows)
                    rcv = _remote(theirs, theirs, send_sems.at[j, p], recv_sems.at[j, p], me)
                    todo += [rcv.wait_recv, snd.wait_send]
        return todo

    dma = pltpu.SemaphoreType.DMA
    return _Carried(gathered, [SDS(g.shape, g.dtype) for g in gathered], {p: p for p in range(n)},
                    [dma((3, n)), dma((3, n))], build)


def _rs_chips_phase(sb):
    _, R, W = sb.shape

    def build(ins, outs, sems, stage):
        if stage == "mid":
            return []
        (sb_ref,), (land_ref,) = ins, outs
        send_sems, recv_sems = sems
        x, y, c = _place()
        chips = [(1 - x, y), (x, 1 - y), (1 - x, 1 - y)]
        cps = [_remote(sb_ref.at[2 * chip[0] + chip[1]], land_ref.at[j], send_sems.at[j], recv_sems.at[j], (*chip, c))
               for j, chip in enumerate(chips)]
        if stage == "start":
            return [cp.start for cp in cps]
        return [cp.wait_recv for cp in cps] + [cp.wait_send for cp in cps]

    dma = pltpu.SemaphoreType.DMA
    return _Carried([sb], [SDS((3, R, W), sb.dtype)], {}, [dma((3,)), dma((3,))], build)


def _allgather(slab, pieces, name):
    R, W = slab.shape
    n = len(pieces)
    assert sum(rows for _, rows in pieces) == R

    def body(slab_ref, *refs):
        outs = refs[:n]
        send_sems, recv_sems, local_sems = refs[n:]
        x, y, c = _place()
        me, sibling = (x, y, c), (x, y, 1 - c)
        chips = [(1 - x, y), (x, 1 - y), (1 - x, 1 - y)]

        def dst_rows(p, origin):
            rows = pieces[p][1]
            start = (4 * origin[0] + 2 * origin[1] + origin[2]) * rows
            return outs[p].at[pl.ds(start, rows), :]

        def copies(k, origin, to, from_slab):
            out = []
            for p, (off, rows) in enumerate(pieces):
                dst = dst_rows(p, origin)
                src = slab_ref.at[pl.ds(off, rows), :] if from_slab else dst
                out.append(pltpu.make_async_remote_copy(
                    src_ref=src, dst_ref=dst, send_sem=send_sems.at[k, p], recv_sem=recv_sems.at[k, p],
                    device_id=to, device_id_type=MESH))
            return out

        mine = [pltpu.make_async_copy(slab_ref.at[pl.ds(off, rows), :], dst_rows(p, me), local_sems.at[p])
                for p, (off, rows) in enumerate(pieces)]
        for cp in mine:
            cp.start()
        first = copies(0, me, sibling, True)
        for j, chip in enumerate(chips):
            first += copies(1 + j, me, (*chip, c), True)
        for cp in first:
            cp.start()
        passed = []
        for j, chip in enumerate(chips):
            for cp in copies(1 + j, (*chip, c), me, False):
                cp.wait_recv()
            fwd = copies(4 + j, (*chip, c), sibling, False)
            for cp in fwd:
                cp.start()
            passed += fwd
        for cp in copies(0, sibling, me, False):
            cp.wait_recv()
        for j, chip in enumerate(chips):
            for cp in copies(4 + j, (*chip, 1 - c), me, False):
                cp.wait_recv()
        for cp in first + passed:
            cp.wait_send()
        for cp in mine:
            cp.wait()

    return pl.pallas_call(
        body, name=name,
        in_specs=[HBM_SPEC], out_specs=[HBM_SPEC] * n,
        out_shape=[SDS((N_DEV * rows, W), slab.dtype) for _, rows in pieces],
        scratch_shapes=[pltpu.SemaphoreType.DMA((7, n)), pltpu.SemaphoreType.DMA((7, n)), pltpu.SemaphoreType.DMA((n,))],
    )(slab)


def _rs_sibling(grads, pieces, name):
    W = grads[0].shape[1]
    R = sum(rows for _, rows in pieces)
    n = len(pieces)
    dt = grads[0].dtype
    max_rows = max(rows for _, rows in pieces)
    steps = [(q, p) for q in range(N_CHIP) for p in range(n)]
    ns = len(steps)
    ADD_ROWS = 64
    SLOTS = 3
    assert all(rows % ADD_ROWS == 0 for _, rows in pieces)

    def body(*refs):
        g_refs = refs[:n]
        sb_ref, mine_buf, send_buf, land_buf, out_buf, in_sems, out_sems, send_sems, recv_sems, credit = refs[n:]
        x, y, c = _place()
        sibling = (x, y, 1 - c)

        def loads(s):
            q, p = steps[s]
            rows = pieces[p][1]
            slot = s % SLOTS
            mine = g_refs[p].at[pl.ds((2 * q + c) * rows, rows), :]
            theirs = g_refs[p].at[pl.ds((2 * q + 1 - c) * rows, rows), :]
            return (pltpu.make_async_copy(mine, mine_buf.at[slot, pl.ds(0, rows), :], in_sems.at[slot, 0]),
                    pltpu.make_async_copy(theirs, send_buf.at[slot, pl.ds(0, rows), :], in_sems.at[slot, 1]))

        def send(s):
            rows = pieces[steps[s][1]][1]
            slot = s % SLOTS
            return pltpu.make_async_remote_copy(
                src_ref=send_buf.at[slot, pl.ds(0, rows), :], dst_ref=land_buf.at[slot, pl.ds(0, rows), :],
                send_sem=send_sems.at[slot], recv_sem=recv_sems.at[slot], device_id=sibling, device_id_type=MESH)

        def store(s):
            q, p = steps[s]
            off, rows = pieces[p]
            slot = s % SLOTS
            return pltpu.make_async_copy(out_buf.at[slot, pl.ds(0, rows), :], sb_ref.at[q, pl.ds(off, rows), :],
                                         out_sems.at[slot])

        def start_send(s):
            for cp in loads(s):
                cp.wait()
            if s >= SLOTS:
                pl.semaphore_wait(credit.at[s % SLOTS], 1)
            send(s).start()

        for s in range(min(SLOTS, ns)):
            for cp in loads(s):
                cp.start()
        for s in range(min(SLOTS - 1, ns)):
            start_send(s)
        for s in range(ns):
            slot = s % SLOTS
            rows = pieces[steps[s][1]][1]
            if s + SLOTS - 1 < ns:
                start_send(s + SLOTS - 1)
            send(s).wait_recv()
            if s >= SLOTS:
                store(s - SLOTS).wait()

            def add(k, _, slot=slot):
                r = pl.ds(pl.multiple_of(k * ADD_ROWS, ADD_ROWS), ADD_ROWS)
                out_buf[slot, r, :] = (mine_buf[slot, r, :].astype(F32) + land_buf[slot, r, :].astype(F32)).astype(dt)
                return 0
            lax.fori_loop(0, rows // ADD_ROWS, add, 0)
            if s + SLOTS < ns:
                pl.semaphore_signal(credit.at[slot], inc=1, device_id=sibling, device_id_type=MESH)
            store(s).start()
            send(s).wait_send()
            if s + SLOTS < ns:
                for cp in loads(s + SLOTS):
                    cp.start()
        for s in range(max(ns - SLOTS, 0), ns):
            store(s).wait()

    buf = pltpu.VMEM((SLOTS, max_rows, W), dt)
    return pl.pallas_call(
        body, name=name,
        in_specs=[HBM_SPEC] * n, out_specs=HBM_SPEC,
        out_shape=SDS((N_CHIP, R, W), dt),
        scratch_shapes=[buf, buf, buf, buf, pltpu.SemaphoreType.DMA((SLOTS, 2)), pltpu.SemaphoreType.DMA((SLOTS,)),
                        pltpu.SemaphoreType.DMA((SLOTS,)), pltpu.SemaphoreType.DMA((SLOTS,)),
                        pltpu.SemaphoreType.REGULAR((SLOTS,))],
        compiler_params=pltpu.CompilerParams(vmem_limit_bytes=48 * MIB),
    )(*grads)


SMALL_NAMES = ("ffn1_norm", "mix_norm", "ffn2_norm", "final_norm", "lru_conv_w", "lru_conv_b", "lru_w_a", "lru_b_a",
               "lru_w_i", "lru_b_i", "lru_lambda", "sc_conv_w", "lru_out_norm", "sc_out_norm")
WEIGHT_NAMES = ("ffn1_norm", "ffn1_w_gate", "ffn1_w_up", "ffn1_w_down", "mix_norm", "w_in", "lru_conv_w", "lru_conv_b",
                "lru_w_a", "lru_b_a", "lru_w_i", "lru_b_i", "lru_lambda", "sc_conv_w", "lru_out_norm", "sc_out_norm",
                "w_out", "ffn2_norm", "ffn2_w_gate", "ffn2_w_up", "ffn2_w_down", "final_norm")
BIG = (("ffn1_w_gate", True), ("ffn1_w_up", True), ("ffn1_w_down", False), ("ffn2_w_gate", True), ("ffn2_w_up", True),
       ("ffn2_w_down", False), ("w_in", True), ("w_out", False))


SLAB_ROW_ALIGN = 256


def _pack_rows(parts, width):
    rows, counts = [], []
    for p in parts:
        flat = p.reshape(-1)
        nr = -(-flat.shape[0] // width)
        nr = -(-nr // SUBLANES) * SUBLANES
        rows.append(jnp.pad(flat, (0, nr * width - flat.shape[0])).reshape(nr, width))
        counts.append(nr)
    total = sum(counts)
    pad = -(-total // SLAB_ROW_ALIGN) * SLAB_ROW_ALIGN - total
    if pad:
        rows.append(jnp.zeros((pad, width), rows[0].dtype))
    return jnp.concatenate(rows, axis=0), counts


def _stack_rows(blocks):
    pieces, off = [], 0
    for b in blocks:
        pieces.append((off, b.shape[0]))
        off += b.shape[0]
    return jnp.concatenate(blocks, axis=0), pieces


def _unpack_rows(slab, counts, shapes):
    out, r = [], 0
    for nr, shape in zip(counts, shapes):
        size = math.prod(shape)
        out.append(slab[r:r + nr].reshape(-1)[:size].reshape(shape))
        r += nr
    return out


def kernel(x, ffn1_norm, ffn1_w_gate, ffn1_w_up, ffn1_w_down, mix_norm, w_in, lru_conv_w, lru_conv_b, lru_w_a, lru_b_a, lru_w_i, lru_b_i, lru_lambda, sc_conv_w, lru_out_norm, sc_out_norm, w_out, ffn2_norm, ffn2_w_gate, ffn2_w_up, ffn2_w_down, final_norm, loss_target, m_ffn1_norm, m_ffn1_w_gate, m_ffn1_w_up, m_ffn1_w_down, m_mix_norm, m_w_in, m_lru_conv_w, m_lru_conv_b, m_lru_w_a, m_lru_b_a, m_lru_w_i, m_lru_b_i, m_lru_lambda, m_sc_conv_w, m_lru_out_norm, m_sc_out_norm, m_w_out, m_ffn2_norm, m_ffn2_w_gate, m_ffn2_w_up, m_ffn2_w_down, m_final_norm, v_ffn1_norm, v_ffn1_w_gate, v_ffn1_w_up, v_ffn1_w_down, v_mix_norm, v_w_in, v_lru_conv_w, v_lru_conv_b, v_lru_w_a, v_lru_b_a, v_lru_w_i, v_lru_b_i, v_lru_lambda, v_sc_conv_w, v_lru_out_norm, v_sc_out_norm, v_w_out, v_ffn2_norm, v_ffn2_w_gate, v_ffn2_w_up, v_ffn2_w_down, v_final_norm):
    a = dict(locals())
    w = {n: a[n] for n in WEIGHT_NAMES}
    m = {n: a["m_" + n] for n in WEIGHT_NAMES}
    v = {n: a["v_" + n] for n in WEIGHT_NAMES}
    ax, ay, ac = _place()
    dev = 4 * ax + 2 * ay + ac
    chip = (2 * ax + ay).astype(jnp.int32).reshape(1)

    x0 = x[0]
    tgt = loss_target[0]
    T, D = x0.shape
    C = D // 2
    H, hd = lru_w_a.shape[1], lru_w_a.shape[2]
    CL = lru_conv_w.shape[2]

    shards = []
    for name, transposed in BIG:
        s = w[name][0]
        shards.append((s.T if transposed else s).astype(BF16))
    taps = jnp.concatenate([lru_conv_w[0], sc_conv_w[0], jnp.zeros((1, CL), F32)], axis=0)
    taps_row = lax.bitcast_convert_type(taps, BF16).reshape(1, -1)
    taps_blk = jnp.pad(taps_row, ((0, BF16_ROWS - 1), (0, D - taps_row.shape[1])))
    s_wg1, s_wu1, s_wd1, s_wg2, s_wu2, s_wd2, s_win, s_wout = shards
    slab_g1, pcs_g1 = _stack_rows([s_wg1])
    slab_u1, pcs_u1 = _stack_rows([s_wu1])
    slab_d1, pcs_d1 = _stack_rows([s_wd1])
    slab_mw, pcs_mw = _stack_rows([s_win, s_wout, taps_blk])
    slab_gu2, pcs_gu2 = _stack_rows([s_wg2, s_wu2])
    slab_d2, pcs_d2 = _stack_rows([s_wd2])
    (wg1,) = _allgather(slab_g1, pcs_g1, "allgather_ffn1_gate")

    g1, gm, g3 = ffn1_norm, mix_norm, ffn2_norm
    phase = _merge_phases(_ag_two_level_phase(slab_u1, pcs_u1), _ag_direct_phase(slab_d1, pcs_d1, SAME_CORE_AND_SIBLING))
    (n1, hg1), got = _norm_proj(x0, g1, [wg1], [BF16], False, "ffn1_gate", carried=phase)
    wu1, d1 = got[0], got[1:]
    phase = _merge_phases(_ag_forward_phase(d1, pcs_d1), _ag_direct_phase(slab_mw, pcs_mw, SAME_CORE_AND_SIBLING))
    (hu1, act1), got = _up_act(n1, wu1, hg1, "ffn1_up", carried=phase)
    wd1, mixw = got[0], got[1:]
    phase = _merge_phases(_ag_forward_phase(mixw, pcs_mw), _ag_direct_phase(slab_gu2, pcs_gu2, SAME_CORE_AND_SIBLING))
    x1, got = _mm_res(act1, wd1, x0, 0.5, "ffn1_down", carried=phase)
    (win, wout, taps_all), gu2 = got[:3], got[3:]
    phase = _merge_phases(_ag_forward_phase(gu2, pcs_gu2), _ag_direct_phase(slab_d2, pcs_d2, SAME_CORE_AND_SIBLING))
    (n2, z), got = _norm_proj(x1, gm, [win], [F32], False, "in_proj", carried=phase)
    (wg2, wu2), d2 = got[:2], got[2:]
    taps_all = taps_all.reshape(N_DEV, BF16_ROWS, D)[:, 0, :2 * SUBLANES * CL].reshape(N_DEV, SUBLANES, CL, 2)
    taps_all = lax.bitcast_convert_type(taps_all, F32)
    taps_all = taps_all.transpose(1, 0, 2).reshape(SUBLANES, N_DEV * CL)
    cw, sw = taps_all[0:4], taps_all[4:7]

    gf = final_norm.reshape(1, D)
    cb = lru_conv_b
    wa, wi = lru_w_a[0].astype(BF16), lru_w_i[0].astype(BF16)
    ba, bi = lru_b_a.reshape(1, C), lru_b_i.reshape(1, C)
    lam, glo, gso = lru_lambda, lru_out_norm, sc_out_norm

    y, h, *saved = _mix_fwd(z, cw, cb, wa, ba, wi, bi, lam, sw, glo, gso, "mix_fwd")
    x2, (wd2,) = _mm_res(y, wout, x1, 1.0, "out_proj", carried=_ag_forward_phase(d2, pcs_d2))
    n3, hg2, hu2, act2 = _norm_proj(x2, g3, [wg2, wu2], [BF16, BF16], True, "ffn2_up")
    x3 = _mm_res(act2, wd2, x2, 0.5, "ffn2_down")
    dx3, df2, d_gf, loss_blk = _loss_head(x3, gf, tgt, "loss_head")

    F = wd1.shape[0]
    bm_f = F // 4 if (F // 4) % LANES == 0 else 512

    def reduce_group(gs, tag):
        pcs, off = [], 0
        for g_ in gs:
            pcs.append((off, g_.shape[0] // N_DEV))
            off += g_.shape[0] // N_DEV
        sb_ = _rs_sibling(gs, pcs, "rs_sibling_add_" + tag)
        return sb_, pcs

    dhg2, dhu2 = _ffn_bwd_act(df2, wd2, hg2, hu2, "ffn2_bwd_act")
    d_wd2 = _dw_tn(act2, df2, bm_f, "ffn2_dw_down")
    d_wg2 = _dw_tn(dhg2, n3, bm_f, "ffn2_dw_gate")
    d_wu2 = _dw_tn(dhu2, n3, bm_f, "ffn2_dw_up")
    sb_f2, pcs_f2 = reduce_group([d_wg2, d_wu2, d_wd2], "ffn2")
    (dx2, dx2b, d_g3), (lb_f2,) = _mm_rmsbwd([(dhg2, wg2), (dhu2, wu2)], x2, g3, dx3, 1.0, "ffn2_bwd_in",
                                             carried=_rs_chips_phase(sb_f2))
    dy = _mm_nt(dx2b, wout, "out_proj_bwd")
    d_wout = _dw_tn(y, dx2b, 1024, "out_proj_dw")
    dz, small, d_wa, d_wi = _mix_bwd(z, h, saved, dy, cw, wa, wi, lam, sw, glo, gso, "mix_bwd")
    d_win = _dw_tn(dz, n2, 1280, "in_proj_dw")
    sb_mx, pcs_mx = reduce_group([d_win, d_wout], "mix")
    (dx1, df1, d_gm), (lb_mx,) = _mm_rmsbwd([(dz, win)], x1, gm, dx2, 0.5, "in_proj_bwd",
                                            carried=_rs_chips_phase(sb_mx))
    dhg1, dhu1 = _ffn_bwd_act(df1, wd1, hg1, hu1, "ffn1_bwd_act")
    d_wd1 = _dw_tn(act1, df1, bm_f, "ffn1_dw_down")
    d_wg1 = _dw_tn(dhg1, n1, bm_f, "ffn1_dw_gate")
    d_wu1 = _dw_tn(dhu1, n1, bm_f, "ffn1_dw_up")
    sb_f1, pcs_f1 = reduce_group([d_wg1, d_wu1, d_wd1], "ffn1")
    (dx0, _, d_g1), (lb_f1,) = _mm_rmsbwd([(dhg1, wg1), (dhu1, wu1)], x0, g1, dx1, 1.0, "ffn1_bwd_in",
                                          carried=_rs_chips_phase(sb_f1))

    grads, delta, new_m, new_v = {}, {}, {}, {}
    transposed_shard = dict(BIG)
    for names, sb_, lb_, pcs in ((("ffn2_w_gate", "ffn2_w_up", "ffn2_w_down"), sb_f2, lb_f2, pcs_f2),
                                 (("w_in", "w_out"), sb_mx, lb_mx, pcs_mx),
                                 (("ffn1_w_gate", "ffn1_w_up", "ffn1_w_down"), sb_f1, lb_f1, pcs_f1)):
        for name, (off, rows) in zip(names, pcs):
            flip = transposed_shard[name] and w[name].shape[2] % LANES != 0
            if transposed_shard[name] and not flip:
                g_ = _final_grad(sb_, lb_, chip, off, rows, "rs_final_sum_" + name).T
                d_, m_, v_ = _adamw(w[name][0], g_, m[name][0], v[name][0], "adamw_" + name)
            else:
                view = (lambda t: t[0].T) if flip else (lambda t: t[0])
                g_, d_, m_, v_ = _sum_adamw(sb_, lb_, chip, off, rows, view(w[name]), view(m[name]), view(v[name]),
                                            "sum_adamw_" + name)
            back = (lambda t: t.T[None]) if flip else (lambda t: t[None])
            grads[name], delta[name], new_m[name], new_v[name] = back(g_), back(d_), back(m_), back(v_)

    small_parts = [d_g1, d_gm, d_g3, d_gf, small[R_CW:R_CW + 4], small[R_CB], d_wa, small[R_BA], d_wi, small[R_BI],
                   small[R_LAM], small[R_SW:R_SW + 3], small[R_GLO], small[R_GSO]]
    sslab, counts = _pack_rows(small_parts, LANES)
    RS = sslab.shape[0]
    (sg,) = _allgather(sslab, [(0, RS)], "allgather_small_grads")
    ssum = _add_slabs([sg[j * RS:(j + 1) * RS] for j in range(N_DEV)], F32, "small_grads_sum")
    full_shapes = [(1, D), (1, D), (1, D), (D,), (1, 4, C), (1, C), (1, H, hd, hd), (1, H, hd), (1, H, hd, hd), (1, H, hd),
                   (1, C), (1, 3, C), (1, C), (1, C)]
    small_full = dict(zip(SMALL_NAMES, _unpack_rows(ssum, counts, full_shapes)))

    for name in SMALL_NAMES:
        gfull = small_full[name]
        if name in ("lru_conv_w", "sc_conv_w"):
            gfull = lax.dynamic_slice_in_dim(gfull, dev * CL, CL, axis=2)
        grads[name] = gfull

    packs = [_pack_rows([t[n_] for n_ in SMALL_NAMES], LANES) for t in (w, grads, m, v)]
    sd, sm, sv = _adamw(packs[0][0], packs[1][0], packs[2][0], packs[3][0], "adamw_small")
    shapes = [w[n_].shape for n_ in SMALL_NAMES]
    for tgt_dict, slab_ in ((delta, sd), (new_m, sm), (new_v, sv)):
        for n_, val in zip(SMALL_NAMES, _unpack_rows(slab_, packs[0][1], shapes)):
            tgt_dict[n_] = val

    loss = lax.psum(loss_blk[0, 0], ("x", "y", "c"))
    return (loss, dx0[None], *[grads[n_] for n_ in WEIGHT_NAMES], *[delta[n_] for n_ in WEIGHT_NAMES],
            *[new_m[n_] for n_ in WEIGHT_NAMES], *[new_v[n_] for n_ in WEIGHT_NAMES])
```

```python
import functools
import math

import jax
import jax.numpy as jnp
from jax import lax
from jax.experimental import pallas as pl
from jax.experimental.pallas import tpu as pltpu

F32 = jnp.float32
BF16 = jnp.bfloat16
SDS = jax.ShapeDtypeStruct
MESH = pl.DeviceIdType.MESH

NORM_EPS = 1e-6
LRU_C = 8.0
N_DEV = 8
N_CHIP = 4
ADAM_LR, ADAM_B1, ADAM_B2, ADAM_EPS, ADAM_WD, ADAM_STEP = 0.001, 0.9, 0.999, 1e-08, 0.01, 10

NN = (((1,), (0,)), ((), ()))
NT = (((1,), (1,)), ((), ()))
TN = (((0,), (0,)), ((), ()))

SUBLANES = 8
BF16_ROWS = 16
LANES = 128
MIB = 1 << 20


def _dot(a, b, dims):
    return lax.dot_general(a, b, dims, preferred_element_type=F32)


def _blk(n, pref, align):
    if n <= pref:
        return n
    b = (pref // align) * align
    while b >= align:
        if n % b == 0:
            return b
        b -= align
    raise ValueError(f"no block of {n} aligned to {align} under {pref}")


def _cp(sem, vmem_mib):
    return pltpu.CompilerParams(dimension_semantics=sem, vmem_limit_bytes=vmem_mib * MIB)


HBM_SPEC = pl.BlockSpec(memory_space=pltpu.HBM)
MID_EIGHTHS = 5

class _Carried:
    def __init__(self, inputs, out_shapes, aliases, sem_shapes, build, has_mid=False):
        self.inputs, self.out_shapes, self.aliases = list(inputs), list(out_shapes), dict(aliases)
        self.sem_shapes, self.build, self.has_mid = list(sem_shapes), build, has_mid


def _call(body, *, name, grid, in_specs, out_specs, out_shape, scratch_shapes, compiler_params, args, carried=None):
    if carried is None:
        return pl.pallas_call(body, name=name, grid=grid, in_specs=in_specs, out_specs=out_specs, out_shape=out_shape,
                              scratch_shapes=scratch_shapes, compiler_params=compiler_params)(*args)
    n_in, n_out, n_sc = len(in_specs), len(out_shape), len(scratch_shapes)
    c_in, c_out = len(carried.inputs), len(carried.out_shapes)

    def hosted(*refs):
        ins, refs = refs[:n_in], refs[n_in:]
        c_ins, refs = refs[:c_in], refs[c_in:]
        outs, refs = refs[:n_out], refs[n_out:]
        c_outs, refs = refs[:c_out], refs[c_out:]
        scratch, c_sems = refs[:n_sc], refs[n_sc:]
        first = functools.reduce(jnp.logical_and, [pl.program_id(a) == 0 for a in range(len(grid))])
        last = functools.reduce(jnp.logical_and, [pl.program_id(a) == g - 1 for a, g in enumerate(grid)])

        @pl.when(first)
        def _():
            for start in carried.build(c_ins, c_outs, c_sems, "start"):
                start()

        if carried.has_mid:
            mid = functools.reduce(jnp.logical_and, [pl.program_id(0) == (grid[0] * MID_EIGHTHS) // 8]
                                   + [pl.program_id(a) == 0 for a in range(1, len(grid))])

            @pl.when(mid)
            def _():
                for step in carried.build(c_ins, c_outs, c_sems, "mid"):
                    step()

        body(*ins, *outs, *scratch)

        @pl.when(last)
        def _():
            for wait in carried.build(c_ins, c_outs, c_sems, "end"):
                wait()

    out = pl.pallas_call(
        hosted, name=name, grid=grid, in_specs=list(in_specs) + [HBM_SPEC] * c_in,
        out_specs=list(out_specs) + [HBM_SPEC] * c_out, out_shape=list(out_shape) + carried.out_shapes,
        scratch_shapes=list(scratch_shapes) + carried.sem_shapes,
        input_output_aliases={n_in + a: n_out + b for a, b in carried.aliases.items()},
        compiler_params=compiler_params)(*args, *carried.inputs)
    return out[:n_out], out[n_out:]


ROW_CHUNK = 128


def _chunk_rows(c):
    return pl.ds(pl.multiple_of(c * ROW_CHUNK, ROW_CHUNK), ROW_CHUNK)


def _rstd(xv):
    return lax.rsqrt(jnp.mean(xv * xv, axis=-1, keepdims=True) + NORM_EPS)


def _rms_bwd(xv, g, dn):
    r = _rstd(xv)
    xr = xv * r
    gd = g * dn
    dx = r * (gd - xr * jnp.mean(gd * xr, axis=-1, keepdims=True))
    return dx, jnp.sum(dn * xr, axis=0, keepdims=True)


def _log1p(e):
    u = 1.0 + e
    return jnp.where(u == 1.0, e, jnp.log(u) * (e / (u - 1.0)))


def _one_minus_exp(v, exp_half_v):
    series = 1.0 / 5040.0
    for coeff in (1.0 / 720.0, 1.0 / 120.0, 1.0 / 24.0, 1.0 / 6.0, 0.5, 1.0):
        series = series * v + coeff
    return jnp.where(v > -0.5, -v * series, 1.0 - exp_half_v * exp_half_v)


def _sigmoid(v):
    return 0.5 * jnp.tanh(0.5 * v) + 0.5


def _gelu_parts(g):
    k0 = math.sqrt(2.0 / math.pi)
    g2 = g * g
    t = jnp.tanh(k0 * (g + 0.044715 * g * g2))
    gel = 0.5 * g * (1.0 + t)
    gelp = 0.5 * (1.0 + t) + 0.5 * g * (1.0 - t * t) * (k0 * (1.0 + 3.0 * 0.044715 * g2))
    return gel, gelp


def _norm_proj(x, gain, w_list, out_dtypes, swiglu, name, carried=None):
    T, D = x.shape
    N = w_list[0].shape[0]
    nw = len(w_list)
    bm = _blk(T, 1024, BF16_ROWS)
    bn = _blk(N, 1024 // nw, LANES)

    def body(*refs):
        x_ref, g_ref = refs[:2]
        w_refs = refs[2:2 + nw]
        n_ref = refs[2 + nw]
        o_refs = refs[3 + nw:3 + 2 * nw]
        act_ref = refs[3 + 2 * nw] if swiglu else None
        n_sc = refs[-1]

        @pl.when(pl.program_id(1) == 0)
        def _():
            def chunk(c, _):
                r = _chunk_rows(c)
                xv = x_ref[r, :]
                nb = (xv * _rstd(xv) * g_ref[...]).astype(BF16)
                n_sc[r, :] = nb
                n_ref[r, :] = nb
                return 0
            lax.fori_loop(0, bm // ROW_CHUNK, chunk, 0)

        n = n_sc[...]
        outs = [_dot(n, w_ref[...], NT) for w_ref in w_refs]
        for o_ref, o in zip(o_refs, outs):
            o_ref[...] = o.astype(o_ref.dtype)
        if swiglu:
            hg, hu = outs
            act_ref[...] = (hg * _sigmoid(hg) * hu).astype(BF16)

    row = pl.BlockSpec((bm, D), lambda i, j: (i, 0))
    tile = pl.BlockSpec((bm, bn), lambda i, j: (i, j))
    n_extra = 1 if swiglu else 0
    return _call(
        body, name=name, grid=(T // bm, N // bn),
        in_specs=[row, pl.BlockSpec((1, D), lambda i, j: (0, 0))] + [pl.BlockSpec((bn, D), lambda i, j: (j, 0))] * nw,
        out_specs=[row] + [tile] * (nw + n_extra),
        out_shape=[SDS((T, D), BF16)] + [SDS((T, N), dt) for dt in out_dtypes] + [SDS((T, N), BF16)] * n_extra,
        scratch_shapes=[pltpu.VMEM((bm, D), BF16)],
        compiler_params=_cp(("arbitrary", "arbitrary"), 52),
        args=(x, gain, *w_list), carried=carried)


def _up_act(n, wu, hg, name, carried=None):
    T, D = n.shape
    F = wu.shape[0]
    bm = _blk(T, 1024, BF16_ROWS)
    bn = _blk(F, 512, LANES)

    def body(n_ref, wu_ref, hg_ref, hu_ref, act_ref):
        hu = _dot(n_ref[...], wu_ref[...], NT)
        hg = hg_ref[...].astype(F32)
        hu_ref[...] = hu.astype(BF16)
        act_ref[...] = (hg * _sigmoid(hg) * hu).astype(BF16)

    tile = pl.BlockSpec((bm, bn), lambda i, j: (i, j))
    return _call(
        body, name=name, grid=(T // bm, F // bn),
        in_specs=[pl.BlockSpec((bm, D), lambda i, j: (i, 0)), pl.BlockSpec((bn, D), lambda i, j: (j, 0)), tile],
        out_specs=[tile, tile], out_shape=[SDS((T, F), BF16)] * 2, scratch_shapes=[],
        compiler_params=_cp(("arbitrary", "arbitrary"), 40),
        args=(n, wu, hg), carried=carried)


def _mm_res(a, b, x, scale, name, carried=None):
    T, K = a.shape
    D = b.shape[1]
    bm = _blk(T, 1024, BF16_ROWS)
    bk = _blk(K, 1408, LANES)
    nk = K // bk

    def body(a_ref, b_ref, x_ref, o_ref):
        k = pl.program_id(1)

        @pl.when(k == 0)
        def _():
            o_ref[...] = jnp.zeros_like(o_ref)

        o_ref[...] += _dot(a_ref[...], b_ref[...], NN)

        @pl.when(k == nk - 1)
        def _():
            def chunk(c, _):
                r = _chunk_rows(c)
                o_ref[r, :] = x_ref[r, :] + scale * o_ref[r, :]
                return 0
            lax.fori_loop(0, bm // ROW_CHUNK, chunk, 0)

    row = pl.BlockSpec((bm, D), lambda i, k: (i, 0))
    out = _call(
        body, name=name, grid=(T // bm, nk),
        in_specs=[pl.BlockSpec((bm, bk), lambda i, k: (i, k)), pl.BlockSpec((bk, D), lambda i, k: (k, 0)), row],
        out_specs=[row], out_shape=[SDS((T, D), F32)], scratch_shapes=[],
        compiler_params=_cp(("arbitrary", "arbitrary"), 56),
        args=(a, b, x), carried=carried)
    return out[0] if carried is None else (out[0][0], out[1])


def _mm_nt(a, b, name):
    T, K = a.shape
    N = b.shape[0]
    bm = _blk(T, 1024, BF16_ROWS)
    bn = _blk(N, 512, LANES)

    def body(a_ref, b_ref, o_ref):
        o_ref[...] = _dot(a_ref[...], b_ref[...], NT)

    return pl.pallas_call(
        body, name=name, grid=(T // bm, N // bn),
        in_specs=[pl.BlockSpec((bm, K), lambda i, j: (i, 0)), pl.BlockSpec((bn, K), lambda i, j: (j, 0))],
        out_specs=pl.BlockSpec((bm, bn), lambda i, j: (i, j)), out_shape=SDS((T, N), F32),
        compiler_params=_cp(("arbitrary", "arbitrary"), 40),
    )(a, b)


def _ffn_bwd_act(dfb, wd, hg, hu, name):
    T, D = dfb.shape
    F = wd.shape[0]
    bm = _blk(T, 1024, BF16_ROWS)
    bn = _blk(F, 512, LANES)

    def body(df_ref, wd_ref, hg_ref, hu_ref, dhg_ref, dhu_ref):
        dact = _dot(df_ref[...], wd_ref[...], NT)
        hgv = hg_ref[...].astype(F32)
        huv = hu_ref[...].astype(F32)
        s = _sigmoid(hgv)
        dhu_ref[...] = (dact * (hgv * s)).astype(BF16)
        dhg_ref[...] = (dact * huv * (s * (1.0 + hgv * (1.0 - s)))).astype(BF16)

    tile = pl.BlockSpec((bm, bn), lambda i, j: (i, j))
    return pl.pallas_call(
        body, name=name, grid=(T // bm, F // bn),
        in_specs=[pl.BlockSpec((bm, D), lambda i, j: (i, 0)), pl.BlockSpec((bn, D), lambda i, j: (j, 0)), tile, tile],
        out_specs=[tile, tile], out_shape=[SDS((T, F), BF16)] * 2,
        compiler_params=_cp(("arbitrary", "arbitrary"), 40),
    )(dfb, wd, hg, hu)


def _dw_tn(a, b, bm_pref, name):
    T, M = a.shape
    N = b.shape[1]
    bm = _blk(M, bm_pref, LANES)
    tk = _blk(T, 1024, BF16_ROWS)
    nk = T // tk

    def body(a_ref, b_ref, o_ref, acc):
        k = pl.program_id(1)

        @pl.when(k == 0)
        def _():
            acc[...] = jnp.zeros_like(acc)

        acc[...] += _dot(a_ref[...], b_ref[...], TN)

        @pl.when(k == nk - 1)
        def _():
            o_ref[...] = acc[...].astype(BF16)

    return pl.pallas_call(
        body, name=name, grid=(M // bm, nk),
        in_specs=[pl.BlockSpec((tk, bm), lambda i, k: (k, i)), pl.BlockSpec((tk, N), lambda i, k: (k, 0))],
        out_specs=pl.BlockSpec((bm, N), lambda i, k: (i, 0)), out_shape=SDS((M, N), BF16),
        scratch_shapes=[pltpu.VMEM((bm, N), F32)],
        compiler_params=_cp(("arbitrary", "arbitrary"), 48),
    )(a, b)


def _mm_rmsbwd(pairs, x, gain, dx_in, bscale, name, carried=None):
    T, D = x.shape
    K = pairs[0][0].shape[1]
    npair = len(pairs)
    bm = _blk(T, 1024, BF16_ROWS)
    bk = _blk(K, 1024 // npair, LANES)
    nk = K // bk

    nchunk = bm // ROW_CHUNK

    def body(*refs):
        ab = refs[:2 * npair]
        x_hbm, g_ref, dxin_hbm, dx_ref, dxb_ref, dg_ref, x_buf, dxin_buf, sems = refs[2 * npair:]
        i = pl.program_id(0)
        k = pl.program_id(1)

        def fetch(c, slot):
            rows = pl.ds(i * bm + c * ROW_CHUNK, ROW_CHUNK)
            return (pltpu.make_async_copy(x_hbm.at[rows, :], x_buf.at[slot], sems.at[slot, 0]),
                    pltpu.make_async_copy(dxin_hbm.at[rows, :], dxin_buf.at[slot], sems.at[slot, 1]))

        @pl.when(k == 0)
        def _():
            dx_ref[...] = jnp.zeros_like(dx_ref)

        @pl.when(k == nk - 1)
        def _():
            for cp in fetch(0, 0):
                cp.start()

        for q in range(npair):
            dx_ref[...] += _dot(ab[2 * q][...], ab[2 * q + 1][...], NN)

        @pl.when(k == nk - 1)
        def _():
            @pl.when(i == 0)
            def _():
                dg_ref[...] = jnp.zeros_like(dg_ref)

            def chunk(c, _):
                slot = c % 2

                @pl.when(c + 1 < nchunk)
                def _():
                    for cp in fetch(c + 1, 1 - slot):
                        cp.start()

                for cp in fetch(c, slot):
                    cp.wait()

                r = _chunk_rows(c)
                dx, dg = _rms_bwd(x_buf[slot], g_ref[...], dx_ref[r, :])
                dxo = dxin_buf[slot] + dx
                dx_ref[r, :] = dxo
                dxb_ref[r, :] = (bscale * dxo).astype(BF16)
                dg_ref[...] += dg
                return 0
            lax.fori_loop(0, nchunk, chunk, 0)

    row = pl.BlockSpec((bm, D), lambda i, k: (i, 0))
    anywhere = pl.BlockSpec(memory_space=pl.ANY)
    vec = pl.BlockSpec((1, D), lambda i, k: (0, 0))
    in_specs = []
    args = []
    for a, b in pairs:
        in_specs += [pl.BlockSpec((bm, bk), lambda i, k: (i, k)), pl.BlockSpec((bk, D), lambda i, k: (k, 0))]
        args += [a, b]
    return _call(
        body, name=name, grid=(T // bm, nk),
        in_specs=in_specs + [anywhere, vec, anywhere], out_specs=[row, row, vec],
        out_shape=[SDS((T, D), F32), SDS((T, D), BF16), SDS((1, D), F32)],
        scratch_shapes=[pltpu.VMEM((2, ROW_CHUNK, D), F32), pltpu.VMEM((2, ROW_CHUNK, D), F32),
                        pltpu.SemaphoreType.DMA((2, 2))],
        compiler_params=_cp(("arbitrary", "arbitrary"), 52),
        args=(*args, x, gain, dx_in), carried=carried)


def _loss_head(x3, gain, tgt, name):
    T, D = x3.shape
    bm = _blk(T, 256, BF16_ROWS)

    def body(x_ref, g_ref, t_ref, dx_ref, dxb_ref, dg_ref, loss_ref):
        i = pl.program_id(0)
        xv = x_ref[...]
        g = g_ref[...]
        out = xv * _rstd(xv) * g
        e = out - t_ref[...]
        part = 0.5 * jnp.sum(jnp.mean(e * e, axis=-1, keepdims=True), axis=0, keepdims=True)
        dx, dg = _rms_bwd(xv, g, e * (1.0 / D))
        dx_ref[...] = dx
        dxb_ref[...] = (0.5 * dx).astype(BF16)

        @pl.when(i == 0)
        def _():
            dg_ref[...] = dg
            loss_ref[...] = jnp.broadcast_to(part, loss_ref.shape)

        @pl.when(i > 0)
        def _():
            dg_ref[...] += dg
            loss_ref[...] += jnp.broadcast_to(part, loss_ref.shape)

    row = pl.BlockSpec((bm, D), lambda i: (i, 0))
    vec = pl.BlockSpec((1, D), lambda i: (0, 0))
    return pl.pallas_call(
        body, name=name, grid=(T // bm,),
        in_specs=[row, vec, row], out_specs=[row, row, vec, pl.BlockSpec((SUBLANES, LANES), lambda i: (0, 0))],
        out_shape=[SDS((T, D), F32), SDS((T, D), BF16), SDS((1, D), F32), SDS((SUBLANES, LANES), F32)],
        compiler_params=_cp(("arbitrary",), 40),
    )(x3, gain, tgt)


R_CW, R_CB, R_BA, R_BI, R_LAM, R_SW, R_GLO, R_GSO, SMALL_ROWS = 0, 4, 5, 6, 7, 8, 11, 12, 16


def _rows(g):
    return pl.ds(pl.multiple_of(g * SUBLANES, SUBLANES), SUBLANES)


def _shift_back(prev, cur, d):
    row = lax.broadcasted_iota(jnp.int32, cur.shape, 0)
    return pltpu.roll(jnp.where(row >= SUBLANES - d, prev, cur), d, 0)


def _shift_fwd(cur, nxt, d):
    row = lax.broadcasted_iota(jnp.int32, cur.shape, 0)
    return pltpu.roll(jnp.where(row < d, nxt, cur), SUBLANES - d, 0)


def _causal_conv(ext, g, taps_ref, ntap):
    prev = ext[_rows(g), :]
    cur = ext[_rows(g + 1), :]
    out = _shift_back(prev, cur, ntap - 1) * taps_ref[0:1, :]
    for k in range(1, ntap - 1):
        out = out + _shift_back(prev, cur, ntap - 1 - k) * taps_ref[k:k + 1, :]
    return out + cur * taps_ref[ntap - 1:ntap, :]


def _scan8(A, U, reverse):
    row = lax.broadcasted_iota(jnp.int32, A.shape, 0)
    for s in (1, 2, 4):
        if reverse:
            A_sh = pltpu.roll(A, SUBLANES - s, 0)
            U_sh = pltpu.roll(U, SUBLANES - s, 0)
            m = row < SUBLANES - s
        else:
            A_sh = pltpu.roll(A, s, 0)
            U_sh = pltpu.roll(U, s, 0)
            m = row >= s
        U = jnp.where(m, A * U_sh + U, U)
        A = jnp.where(m, A * A_sh, A)
    return A, U


def _gate_pre(xc_s, w_ref, out_s, H, hd):
    for h in range(H):
        cs = slice(h * hd, (h + 1) * hd)
        out_s[:, cs] = _dot(xc_s[:, cs].astype(BF16), w_ref[h], NN)


def _lru_coeffs(pa, pi, xc, ba, bi, sp):
    ra = _sigmoid(pa + ba)
    ri = _sigmoid(pi + bi)
    log_a = (-LRU_C * ra) * sp
    a = jnp.exp(log_a)
    mult = jnp.sqrt(_one_minus_exp(2.0 * log_a, a))
    return ra, ri, a, mult


def _softplus_neg(lam):
    v = -lam
    return jnp.maximum(v, 0.0) + _log1p(jnp.exp(-jnp.abs(v)))


def _mix_fwd(z, cw, cb, wa, ba, wi, bi, lam, sw, glo, gso, name):
    T = z.shape[0]
    C = z.shape[1] // 5
    H = wa.shape[0]
    hd = C // H
    tb = _blk(T, 256, BF16_ROWS)
    ng = tb // SUBLANES
    HDR = SUBLANES

    def body(z_ref, cw_ref, cb_ref, wa_ref, ba_ref, wi_ref, bi_ref, lam_ref, sw_ref, glo_ref, gso_ref,
             y_ref, h_ref, ra_ref, ri_ref, a_ref, m_ref, xc_s, q_ref, xext, pext, pa_s, pi_s, y_s, hcar):
        @pl.when(pl.program_id(0) == 0)
        def _():
            xext[0:HDR, :] = jnp.zeros((HDR, C), F32)
            pext[0:HDR, :] = jnp.zeros((HDR, C), F32)
            hcar[...] = jnp.zeros_like(hcar)

        def fill(g, _):
            r = _rows(g)
            re = _rows(g + 1)
            xext[re, :] = z_ref[r, 0:C]
            pext[re, :] = z_ref[r, 3 * C:4 * C] * z_ref[r, 4 * C:5 * C]
            return 0
        lax.fori_loop(0, ng, fill, 0)

        def conv(g, _):
            xc_s[_rows(g), :] = _causal_conv(xext, g, cw_ref, 4) + cb_ref[...]
            return 0
        lax.fori_loop(0, ng, conv, 0)

        _gate_pre(xc_s, wa_ref, pa_s, H, hd)
        _gate_pre(xc_s, wi_ref, pi_s, H, hd)
        sp = _softplus_neg(lam_ref[...])

        def group(g, hprev):
            r = _rows(g)
            xc = xc_s[r, :]
            ra, ri, a, mult = _lru_coeffs(pa_s[r, :], pi_s[r, :], xc, ba_ref[...], bi_ref[...], sp)
            ra_ref[r, :] = ra
            ri_ref[r, :] = ri
            a_ref[r, :] = a
            m_ref[r, :] = mult
            A, U = _scan8(a, mult * (ri * xc), reverse=False)
            hh = A * hprev + U
            h_ref[r, :] = hh
            gel, _ = _gelu_parts(z_ref[r, C:2 * C])
            y_lru = hh * gel
            y_s[r, 0:C] = y_lru * _rstd(y_lru) * glo_ref[...]
            q = _causal_conv(pext, g, sw_ref, 3)
            q_ref[r, :] = q
            y_sc = z_ref[r, 2 * C:3 * C] * q
            y_s[r, C:2 * C] = y_sc * _rstd(y_sc) * gso_ref[...]
            return jnp.broadcast_to(hh[SUBLANES - 1:SUBLANES, :], hh.shape)
        hcar[...] = lax.fori_loop(0, ng // 2, lambda t, hp: group(2 * t + 1, group(2 * t, hp)), hcar[...])

        xext[0:HDR, :] = xext[tb:tb + HDR, :]
        pext[0:HDR, :] = pext[tb:tb + HDR, :]

        def cast(g, _):
            r = pl.ds(pl.multiple_of(g * BF16_ROWS, BF16_ROWS), BF16_ROWS)
            y_ref[r, :] = y_s[r, :].astype(BF16)
            return 0
        lax.fori_loop(0, tb // BF16_ROWS, cast, 0)

    full = lambda shape: pl.BlockSpec(shape, lambda i: (0,) * len(shape))
    blk = lambda w: pl.BlockSpec((tb, w), lambda i: (i, 0))
    ext = pltpu.VMEM((tb + HDR, C), F32)
    tile = pltpu.VMEM((tb, C), F32)
    return pl.pallas_call(
        body, name=name, grid=(T // tb,),
        in_specs=[blk(5 * C), full((4, C)), full((1, C)), full((H, hd, hd)), full((1, C)), full((H, hd, hd)),
                  full((1, C)), full((1, C)), full((3, C)), full((1, C)), full((1, C))],
        out_specs=[blk(2 * C)] + [blk(C)] * 7,
        out_shape=[SDS((T, 2 * C), BF16)] + [SDS((T, C), F32)] * 7,
        scratch_shapes=[ext, ext, tile, tile, pltpu.VMEM((tb, 2 * C), F32), pltpu.VMEM((SUBLANES, C), F32)],
        compiler_params=_cp(("arbitrary",), 48),
    )(z, cw, cb, wa, ba, wi, bi, lam, sw, glo, gso)


def _mix_bwd(z, h, saved, dy, cw, wa, wi, lam, sw, glo, gso, name):
    T = z.shape[0]
    C = z.shape[1] // 5
    H = wa.shape[0]
    hd = C // H
    tb = _blk(T, 256, BF16_ROWS)
    nb = T // tb
    ng = tb // SUBLANES
    HDR = SUBLANES
    N_ACC = 13

    def body(z_ref, zp_ref, h_ref, hp_ref, ra_ref, ri_ref, a_ref, m_ref, xc_s, q_ref, dy_ref, cw_ref, wa_ref, wi_ref,
             lam_ref, sw_ref, glo_ref, gso_ref, dz_ref, small_ref, dwa_ref, dwi_ref,
             xext, pext, hext, dqext, dxcext, bext, dh_s, dpa_s, dpi_s, dz_s, acc_s, bcar):
        i = pl.program_id(0)
        first_rows = i == nb - 1

        @pl.when(i == 0)
        def _():
            dqext[tb:tb + HDR, :] = jnp.zeros((HDR, C), F32)
            dxcext[tb:tb + HDR, :] = jnp.zeros((HDR, C), F32)
            bcar[...] = jnp.zeros_like(bcar)
            acc_s[...] = jnp.zeros_like(acc_s)
            dwa_ref[...] = jnp.zeros_like(dwa_ref)
            dwi_ref[...] = jnp.zeros_like(dwi_ref)

        zero = jnp.zeros((HDR, C), F32)
        xext[0:HDR, :] = jnp.where(first_rows, zero, zp_ref[:, 0:C])
        pext[0:HDR, :] = jnp.where(first_rows, zero, zp_ref[:, 3 * C:4 * C] * zp_ref[:, 4 * C:5 * C])
        hext[0:HDR, :] = jnp.where(first_rows, zero, hp_ref[...])

        def fill(g, _):
            r = _rows(g)
            re = _rows(g + 1)
            xext[re, :] = z_ref[r, 0:C]
            pext[re, :] = z_ref[r, 3 * C:4 * C] * z_ref[r, 4 * C:5 * C]
            hext[re, :] = h_ref[r, :]
            return 0
        lax.fori_loop(0, ng, fill, 0)

        sp = _softplus_neg(lam_ref[...])
        dsp_dlam = -jax.nn.sigmoid(-lam_ref[...])

        def add_acc(k, v):
            acc_s[k] += v

        def p1(g, _):
            r = _rows(g)
            hh = h_ref[r, :]
            gel, gelp = _gelu_parts(z_ref[r, C:2 * C])
            y_lru = hh * gel
            dnl = dy_ref[r, 0:C]
            rl = _rstd(y_lru)
            ylr = y_lru * rl
            gd = glo_ref[...] * dnl
            dy_lru = rl * (gd - ylr * jnp.mean(gd * ylr, axis=-1, keepdims=True))
            add_acc(R_GLO, dnl * ylr)
            dz_s[r, C:2 * C] = dy_lru * hh * gelp
            dh = dy_lru * gel
            dh_s[r, :] = dh

            q = q_ref[r, :]
            scb = z_ref[r, 2 * C:3 * C]
            y_sc = scb * q
            dns = dy_ref[r, C:2 * C]
            rs = _rstd(y_sc)
            ysr = y_sc * rs
            gs = gso_ref[...] * dns
            dy_sc = rs * (gs - ysr * jnp.mean(gs * ysr, axis=-1, keepdims=True))
            add_acc(R_GSO, dns * ysr)
            dz_s[r, 2 * C:3 * C] = dy_sc * q
            dqext[r, :] = dy_sc * scb
            return 0
        lax.fori_loop(0, ng, p1, 0, unroll=2)

        bext[tb:tb + HDR, :] = bcar[...]

        def p2(j, carry):
            g = ng - 1 - j
            r = _rows(g)
            a = a_ref[r, :]
            A, U = _scan8(a, a * dh_s[r, :], reverse=True)
            bb = A * carry + U
            bext[r, :] = bb
            return jnp.broadcast_to(bb[0:1, :], bb.shape)
        bcar[...] = lax.fori_loop(0, ng, p2, bcar[...])

        def p3(g, _):
            r = _rows(g)
            rn = _rows(g + 1)
            G = dh_s[r, :] + _shift_fwd(bext[r, :], bext[rn, :], 1)
            hm1 = _shift_back(hext[r, :], hext[rn, :], 1)
            a = a_ref[r, :]
            mult = m_ref[r, :]
            ri = ri_ref[r, :]
            xc = xc_s[r, :]
            ra = ra_ref[r, :]
            dxcext[r, :] = G * mult * ri
            dri = G * mult * xc
            dmult = G * ri * xc
            dlog_a = (G * hm1) * a - dmult * (a * a) / mult
            add_acc(R_LAM, dlog_a * (-LRU_C * ra) * dsp_dlam)
            dpa = dlog_a * (-LRU_C * sp) * ra * (1.0 - ra)
            dpi = dri * ri * (1.0 - ri)
            add_acc(R_BA, dpa)
            add_acc(R_BI, dpi)
            dpa_s[r, :] = dpa
            dpi_s[r, :] = dpi
            return 0
        lax.fori_loop(0, ng, p3, 0)

        for hh_ in range(H):
            cs = slice(hh_ * hd, (hh_ + 1) * hd)
            dpa_b = dpa_s[:, cs].astype(BF16)
            dpi_b = dpi_s[:, cs].astype(BF16)
            xc_b = xc_s[:, cs].astype(BF16)
            dxcext[0:tb, cs] += _dot(dpa_b, wa_ref[hh_], NT) + _dot(dpi_b, wi_ref[hh_], NT)
            dwa_ref[hh_] += _dot(xc_b, dpa_b, TN)
            dwi_ref[hh_] += _dot(xc_b, dpi_b, TN)

        def p4(g, _):
            r = _rows(g)
            rn = _rows(g + 1)
            dxc = dxcext[r, :]
            dxc_n = dxcext[rn, :]
            x_p = xext[r, :]
            x_c = xext[rn, :]
            add_acc(R_CB, dxc)
            dlx = dxc * cw_ref[3:4, :]
            add_acc(R_CW + 3, dxc * x_c)
            for d in range(1, 4):
                dlx = dlx + _shift_fwd(dxc, dxc_n, d) * cw_ref[3 - d:4 - d, :]
                add_acc(R_CW + 3 - d, dxc * _shift_back(x_p, x_c, d))
            dz_s[r, 0:C] = dlx

            dq = dqext[r, :]
            dq_n = dqext[rn, :]
            p_p = pext[r, :]
            p_c = pext[rn, :]
            dp = dq * sw_ref[2:3, :]
            add_acc(R_SW + 2, dq * p_c)
            for d in range(1, 3):
                dp = dp + _shift_fwd(dq, dq_n, d) * sw_ref[2 - d:3 - d, :]
                add_acc(R_SW + 2 - d, dq * _shift_back(p_p, p_c, d))
            dz_s[r, 3 * C:4 * C] = dp * z_ref[r, 4 * C:5 * C]
            dz_s[r, 4 * C:5 * C] = dp * z_ref[r, 3 * C:4 * C]
            return 0
        lax.fori_loop(0, ng, p4, 0)

        dqext[tb:tb + HDR, :] = dqext[0:HDR, :]
        dxcext[tb:tb + HDR, :] = dxcext[0:HDR, :]

        def cast(g, _):
            r = pl.ds(pl.multiple_of(g * BF16_ROWS, BF16_ROWS), BF16_ROWS)
            dz_ref[r, :] = dz_s[r, :].astype(BF16)
            return 0
        lax.fori_loop(0, tb // BF16_ROWS, cast, 0)

        @pl.when(i == nb - 1)
        def _():
            small_ref[...] = jnp.zeros_like(small_ref)
            for k in range(N_ACC):
                small_ref[k:k + 1, :] = jnp.sum(acc_s[k], axis=0, keepdims=True)

    tpg = tb // SUBLANES
    full = lambda shape: pl.BlockSpec(shape, lambda i: (0,) * len(shape))
    blk = lambda w: pl.BlockSpec((tb, w), lambda i: (nb - 1 - i, 0))
    prev = lambda w: pl.BlockSpec((SUBLANES, w), lambda i: (jnp.maximum((nb - 1 - i) * tpg - 1, 0), 0))
    ext = pltpu.VMEM((tb + HDR, C), F32)
    tile = pltpu.VMEM((tb, C), F32)
    return pl.pallas_call(
        body, name=name, grid=(nb,),
        in_specs=[blk(5 * C), prev(5 * C), blk(C), prev(C)] + [blk(C)] * 6
        + [blk(2 * C), full((4, C)), full((H, hd, hd)), full((H, hd, hd)), full((1, C)), full((3, C)),
           full((1, C)), full((1, C))],
        out_specs=[blk(5 * C), full((SMALL_ROWS, C)), full((H, hd, hd)), full((H, hd, hd))],
        out_shape=[SDS((T, 5 * C), BF16), SDS((SMALL_ROWS, C), F32), SDS((H, hd, hd), F32), SDS((H, hd, hd), F32)],
        scratch_shapes=[ext] * 6 + [tile] * 3 + [pltpu.VMEM((tb, 5 * C), F32), pltpu.VMEM((N_ACC, SUBLANES, C), F32),
                                                pltpu.VMEM((SUBLANES, C), F32)],
        compiler_params=_cp(("arbitrary",), 56),
    )(z, z, h, h, *saved, dy, cw, wa, wi, lam, sw, glo, gso)


def _add_slabs(terms, out_dtype, name):
    R, Ccols = terms[0].shape
    br = _blk(R, 512, BF16_ROWS)
    n = len(terms)

    def body(*refs):
        s = refs[0][...].astype(F32)
        for t_ref in refs[1:n]:
            s = s + t_ref[...].astype(F32)
        refs[n][...] = s.astype(out_dtype)

    spec = pl.BlockSpec((br, Ccols), lambda i: (i, 0))
    return pl.pallas_call(
        body, name=name, grid=(R // br,), in_specs=[spec] * n, out_specs=spec, out_shape=SDS((R, Ccols), out_dtype),
        compiler_params=_cp(("arbitrary",), 40),
    )(*terms)


def _reduced_rows(sb, lb, off, rows, extra_in, n_out, body, name, chip):
    Ccols = sb.shape[2]
    br = _blk(math.gcd(off, rows) if off else rows, 192, BF16_ROWS)
    ob = off // br
    src = lambda pick: pl.BlockSpec((1, br, Ccols), lambda i, c: (pick(c), ob + i, 0))
    own = pl.BlockSpec((br, Ccols), lambda i, c: (i, 0))
    return pl.pallas_call(
        body, name=name,
        grid_spec=pltpu.PrefetchScalarGridSpec(
            num_scalar_prefetch=1, grid=(rows // br,),
            in_specs=[src(lambda c: c[0]), src(lambda c: 0), src(lambda c: 1), src(lambda c: 2)] + [own] * len(extra_in),
            out_specs=[own] * n_out),
        out_shape=[SDS((rows, Ccols), F32)] * n_out,
        compiler_params=_cp(("arbitrary",), 40),
    )(chip, sb, lb, lb, lb, *extra_in)


def _sum4(sb_ref, l0, l1, l2):
    s = sb_ref[0].astype(F32)
    for t_ref in (l0, l1, l2):
        s = s + t_ref[0].astype(F32)
    return s


def _final_grad(sb, lb, chip, off, rows, name):
    def body(chip_ref, sb_ref, l0, l1, l2, o_ref):
        o_ref[...] = _sum4(sb_ref, l0, l1, l2)

    return _reduced_rows(sb, lb, off, rows, [], 1, body, name, chip)[0]


def _adamw_math(w, g, m, v):
    nm = ADAM_B1 * m + (1.0 - ADAM_B1) * g
    nv = ADAM_B2 * v + (1.0 - ADAM_B2) * (g * g)
    c1 = 1.0 - ADAM_B1 ** ADAM_STEP
    c2 = 1.0 - ADAM_B2 ** ADAM_STEP
    return -ADAM_LR * ((nm / c1) / (jnp.sqrt(nv / c2) + ADAM_EPS) + ADAM_WD * w), nm, nv


def _sum_adamw(sb, lb, chip, off, rows, w, m, v, name):
    def body(chip_ref, sb_ref, l0, l1, l2, w_ref, m_ref, v_ref, g_ref, d_ref, nm_ref, nv_ref):
        g = _sum4(sb_ref, l0, l1, l2)
        g_ref[...] = g
        d_ref[...], nm_ref[...], nv_ref[...] = _adamw_math(w_ref[...], g, m_ref[...], v_ref[...])

    return _reduced_rows(sb, lb, off, rows, [w, m, v], 4, body, name, chip)


def _adamw(w, g, m, v, name):
    R, Ccols = w.shape
    br = _blk(R, 256, SUBLANES)

    def body(w_ref, g_ref, m_ref, v_ref, d_ref, nm_ref, nv_ref):
        d_ref[...], nm_ref[...], nv_ref[...] = _adamw_math(w_ref[...], g_ref[...], m_ref[...], v_ref[...])

    spec = pl.BlockSpec((br, Ccols), lambda i: (i, 0))
    return pl.pallas_call(
        body, name=name, grid=(R // br,), in_specs=[spec] * 4, out_specs=[spec] * 3,
        out_shape=[SDS((R, Ccols), F32)] * 3, compiler_params=_cp(("arbitrary",), 40),
    )(w, g, m, v)


def _place():
    return lax.axis_index("x"), lax.axis_index("y"), lax.axis_index("c")


def _dev_rows(ref, dev, rows):
    return ref.at[pl.ds((4 * dev[0] + 2 * dev[1] + dev[2]) * rows, rows), :]


def _remote(src, dst, send_sem, recv_sem, to):
    return pltpu.make_async_remote_copy(src_ref=src, dst_ref=dst, send_sem=send_sem, recv_sem=recv_sem,
                                        device_id=to, device_id_type=MESH)


SAME_CORE_AND_SIBLING = ((0, 0, 1), (1, 0, 0), (0, 1, 0), (1, 1, 0))


def _merge_phases(a, b):
    na_in, na_out, na_sem = len(a.inputs), len(a.out_shapes), len(a.sem_shapes)

    def build(ins, outs, sems, stage):
        return (a.build(ins[:na_in], outs[:na_out], sems[:na_sem], stage)
                + b.build(ins[na_in:], outs[na_out:], sems[na_sem:], stage))

    aliases = dict(a.aliases)
    aliases.update({na_in + i: na_out + o for i, o in b.aliases.items()})
    return _Carried(a.inputs + b.inputs, a.out_shapes + b.out_shapes, aliases, a.sem_shapes + b.sem_shapes, build,
                    has_mid=a.has_mid or b.has_mid)


def _ag_direct_phase(slab, pieces, flips):
    W = slab.shape[1]
    n = len(pieces)
    npeer = len(flips)

    def build(ins, outs, sems, stage):
        if stage == "mid":
            return []
        starting = stage == "start"
        (slab_ref,) = ins
        send_sems, recv_sems, local_sems = sems
        x, y, c = _place()
        me = (x, y, c)
        peers = [tuple(1 - v if f else v for v, f in zip(me, flip)) for flip in flips]
        todo = []
        for p, (off, rows) in enumerate(pieces):
            src = slab_ref.at[pl.ds(off, rows), :]
            mine = pltpu.make_async_copy(src, _dev_rows(outs[p], me, rows), local_sems.at[p])
            todo.append(mine.start if starting else mine.wait)
            for k, peer in enumerate(peers):
                snd = _remote(src, _dev_rows(outs[p], me, rows), send_sems.at[k, p], recv_sems.at[k, p], peer)
                if starting:
                    todo.append(snd.start)
                else:
                    theirs = _dev_rows(outs[p], peer, rows)
                    rcv = _remote(theirs, theirs, send_sems.at[k, p], recv_sems.at[k, p], me)
                    todo += [rcv.wait_recv, snd.wait_send]
        return todo

    dma = pltpu.SemaphoreType.DMA
    return _Carried([slab], [SDS((N_DEV * rows, W), slab.dtype) for _, rows in pieces], {},
                    [dma((npeer, n)), dma((npeer, n)), dma((n,))], build)


def _ag_two_level_phase(slab, pieces):
    W = slab.shape[1]
    n = len(pieces)

    def build(ins, outs, sems, stage):
        (slab_ref,) = ins
        send_sems, recv_sems, local_sems = sems
        x, y, c = _place()
        me, sibling = (x, y, c), (x, y, 1 - c)
        chips = [(1 - x, y), (x, 1 - y), (1 - x, 1 - y)]
        todo = []
        for p, (off, rows) in enumerate(pieces):
            src = slab_ref.at[pl.ds(off, rows), :]
            own = _dev_rows(outs[p], me, rows)
            landed = [_dev_rows(outs[p], (*chip, c), rows) for chip in chips]

            def mine():
                return pltpu.make_async_copy(src, own, local_sems.at[p])

            def first():
                return [_remote(src, own, send_sems.at[k, p], recv_sems.at[k, p], to)
                        for k, to in enumerate([sibling] + [(*chip, c) for chip in chips])]

            def passed():
                return [_remote(blk, blk, send_sems.at[4 + j, p], recv_sems.at[4 + j, p], sibling)
                        for j, blk in enumerate(landed)]

            def arrival(k, blk):
                return _remote(blk, blk, send_sems.at[k, p], recv_sems.at[k, p], me).wait_recv

            if stage == "start":
                todo += [mine().start] + [cp.start for cp in first()]
            elif stage == "mid":
                for j, (blk, fwd) in enumerate(zip(landed, passed())):
                    todo += [arrival(1 + j, blk), fwd.start]
            else:
                theirs = [_dev_rows(outs[p], sibling, rows)] + [_dev_rows(outs[p], (*chip, 1 - c), rows) for chip in chips]
                todo += [arrival(k, blk) for k, blk in zip((0, 4, 5, 6), theirs)]
                todo += [cp.wait_send for cp in first() + passed()] + [mine().wait]
        return todo

    dma = pltpu.SemaphoreType.DMA
    return _Carried([slab], [SDS((N_DEV * rows, W), slab.dtype) for _, rows in pieces], {},
                    [dma((7, n)), dma((7, n)), dma((n,))], build, has_mid=True)


def _ag_forward_phase(gathered, pieces):
    n = len(pieces)

    def build(ins, outs, sems, stage):
        if stage == "mid":
            return []
        starting = stage == "start"
        send_sems, recv_sems = sems
        x, y, c = _place()
        me, sibling = (x, y, c), (x, y, 1 - c)
        chips = [(1 - x, y), (x, 1 - y), (1 - x, 1 - y)]
        todo = []
        for p, (_, rows) in enumerate(pieces):
            for j, chip in enumerate(chips):
                snd = _remote(_dev_rows(ins[p], (*chip, c), rows), _dev_rows(outs[p], (*chip, c), rows),
                              send_sems.at[j, p], recv_sems.at[j, p], sibling)
                if starting:
                    todo.append(snd.start)
                else:
                    theirs = _dev_rows(outs[p], (*chip, 1 - c), rows)
                    rcv = _remote(theirs, theirs, send_sems.at[j, p], recv_sems.at[j, p], me)
                    todo += [rcv.wait_recv, snd.wait_send]
        return todo

    dma = pltpu.SemaphoreType.DMA
    return _Carried(gathered, [SDS(g.shape, g.dtype) for g in gathered], {p: p for p in range(n)},
                    [dma((3, n)), dma((3, n))], build)


def _rs_chips_phase(sb):
    _, R, W = sb.shape

    def build(ins, outs, sems, stage):
        if stage == "mid":
            return []
        (sb_ref,), (land_ref,) = ins, outs
        send_sems, recv_sems = sems
        x, y, c = _place()
        chips = [(1 - x, y), (x, 1 - y), (1 - x, 1 - y)]
        cps = [_remote(sb_ref.at[2 * chip[0] + chip[1]], land_ref.at[j], send_sems.at[j], recv_sems.at[j], (*chip, c))
               for j, chip in enumerate(chips)]
        if stage == "start":
            return [cp.start for cp in cps]
        return [cp.wait_recv for cp in cps] + [cp.wait_send for cp in cps]

    dma = pltpu.SemaphoreType.DMA
    return _Carried([sb], [SDS((3, R, W), sb.dtype)], {}, [dma((3,)), dma((3,))], build)


def _allgather(slab, pieces, name):
    R, W = slab.shape
    n = len(pieces)
    assert sum(rows for _, rows in pieces) == R

    def body(slab_ref, *refs):
        outs = refs[:n]
        send_sems, recv_sems, local_sems = refs[n:]
        x, y, c = _place()
        me, sibling = (x, y, c), (x, y, 1 - c)
        chips = [(1 - x, y), (x, 1 - y), (1 - x, 1 - y)]

        def dst_rows(p, origin):
            rows = pieces[p][1]
            start = (4 * origin[0] + 2 * origin[1] + origin[2]) * rows
            return outs[p].at[pl.ds(start, rows), :]

        def copies(k, origin, to, from_slab):
            out = []
            for p, (off, rows) in enumerate(pieces):
                dst = dst_rows(p, origin)
                src = slab_ref.at[pl.ds(off, rows), :] if from_slab else dst
                out.append(pltpu.make_async_remote_copy(
                    src_ref=src, dst_ref=dst, send_sem=send_sems.at[k, p], recv_sem=recv_sems.at[k, p],
                    device_id=to, device_id_type=MESH))
            return out

        mine = [pltpu.make_async_copy(slab_ref.at[pl.ds(off, rows), :], dst_rows(p, me), local_sems.at[p])
                for p, (off, rows) in enumerate(pieces)]
        for cp in mine:
            cp.start()
        first = copies(0, me, sibling, True)
        for j, chip in enumerate(chips):
            first += copies(1 + j, me, (*chip, c), True)
        for cp in first:
            cp.start()
        passed = []
        for j, chip in enumerate(chips):
            for cp in copies(1 + j, (*chip, c), me, False):
                cp.wait_recv()
            fwd = copies(4 + j, (*chip, c), sibling, False)
            for cp in fwd:
                cp.start()
            passed += fwd
        for cp in copies(0, sibling, me, False):
            cp.wait_recv()
        for j, chip in enumerate(chips):
            for cp in copies(4 + j, (*chip, 1 - c), me, False):
                cp.wait_recv()
        for cp in first + passed:
            cp.wait_send()
        for cp in mine:
            cp.wait()

    return pl.pallas_call(
        body, name=name,
        in_specs=[HBM_SPEC], out_specs=[HBM_SPEC] * n,
        out_shape=[SDS((N_DEV * rows, W), slab.dtype) for _, rows in pieces],
        scratch_shapes=[pltpu.SemaphoreType.DMA((7, n)), pltpu.SemaphoreType.DMA((7, n)), pltpu.SemaphoreType.DMA((n,))],
    )(slab)


def _rs_sibling(grads, pieces, name):
    W = grads[0].shape[1]
    R = sum(rows for _, rows in pieces)
    n = len(pieces)
    dt = grads[0].dtype
    max_rows = max(rows for _, rows in pieces)
    steps = [(q, p) for q in range(N_CHIP) for p in range(n)]
    ns = len(steps)
    ADD_ROWS = 64
    SLOTS = 3
    assert all(rows % ADD_ROWS == 0 for _, rows in pieces)

    def body(*refs):
        g_refs = refs[:n]
        sb_ref, mine_buf, send_buf, land_buf, out_buf, in_sems, out_sems, send_sems, recv_sems, credit = refs[n:]
        x, y, c = _place()
        sibling = (x, y, 1 - c)

        def loads(s):
            q, p = steps[s]
            rows = pieces[p][1]
            slot = s % SLOTS
            mine = g_refs[p].at[pl.ds((2 * q + c) * rows, rows), :]
            theirs = g_refs[p].at[pl.ds((2 * q + 1 - c) * rows, rows), :]
            return (pltpu.make_async_copy(mine, mine_buf.at[slot, pl.ds(0, rows), :], in_sems.at[slot, 0]),
                    pltpu.make_async_copy(theirs, send_buf.at[slot, pl.ds(0, rows), :], in_sems.at[slot, 1]))

        def send(s):
            rows = pieces[steps[s][1]][1]
            slot = s % SLOTS
            return pltpu.make_async_remote_copy(
                src_ref=send_buf.at[slot, pl.ds(0, rows), :], dst_ref=land_buf.at[slot, pl.ds(0, rows), :],
                send_sem=send_sems.at[slot], recv_sem=recv_sems.at[slot], device_id=sibling, device_id_type=MESH)

        def store(s):
            q, p = steps[s]
            off, rows = pieces[p]
            slot = s % SLOTS
            return pltpu.make_async_copy(out_buf.at[slot, pl.ds(0, rows), :], sb_ref.at[q, pl.ds(off, rows), :],
                                         out_sems.at[slot])

        def start_send(s):
            for cp in loads(s):
                cp.wait()
            if s >= SLOTS:
                pl.semaphore_wait(credit.at[s % SLOTS], 1)
            send(s).start()

        for s in range(min(SLOTS, ns)):
            for cp in loads(s):
                cp.start()
        for s in range(min(SLOTS - 1, ns)):
            start_send(s)
        for s in range(ns):
            slot = s % SLOTS
            rows = pieces[steps[s][1]][1]
            if s + SLOTS - 1 < ns:
                start_send(s + SLOTS - 1)
            send(s).wait_recv()
            if s >= SLOTS:
                store(s - SLOTS).wait()

            def add(k, _, slot=slot):
                r = pl.ds(pl.multiple_of(k * ADD_ROWS, ADD_ROWS), ADD_ROWS)
                out_buf[slot, r, :] = (mine_buf[slot, r, :].astype(F32) + land_buf[slot, r, :].astype(F32)).astype(dt)
                return 0
            lax.fori_loop(0, rows // ADD_ROWS, add, 0)
            if s + SLOTS < ns:
                pl.semaphore_signal(credit.at[slot], inc=1, device_id=sibling, device_id_type=MESH)
            store(s).start()
            send(s).wait_send()
            if s + SLOTS < ns:
                for cp in loads(s + SLOTS):
                    cp.start()
        for s in range(max(ns - SLOTS, 0), ns):
            store(s).wait()

    buf = pltpu.VMEM((SLOTS, max_rows, W), dt)
    return pl.pallas_call(
        body, name=name,
        in_specs=[HBM_SPEC] * n, out_specs=HBM_SPEC,
        out_shape=SDS((N_CHIP, R, W), dt),
        scratch_shapes=[buf, buf, buf, buf, pltpu.SemaphoreType.DMA((SLOTS, 2)), pltpu.SemaphoreType.DMA((SLOTS,)),
                        pltpu.SemaphoreType.DMA((SLOTS,)), pltpu.SemaphoreType.DMA((SLOTS,)),
                        pltpu.SemaphoreType.REGULAR((SLOTS,))],
        compiler_params=pltpu.CompilerParams(vmem_limit_bytes=48 * MIB),
    )(*grads)


SMALL_NAMES = ("ffn1_norm", "mix_norm", "ffn2_norm", "final_norm", "lru_conv_w", "lru_conv_b", "lru_w_a", "lru_b_a",
               "lru_w_i", "lru_b_i", "lru_lambda", "sc_conv_w", "lru_out_norm", "sc_out_norm")
WEIGHT_NAMES = ("ffn1_norm", "ffn1_w_gate", "ffn1_w_up", "ffn1_w_down", "mix_norm", "w_in", "lru_conv_w", "lru_conv_b",
                "lru_w_a", "lru_b_a", "lru_w_i", "lru_b_i", "lru_lambda", "sc_conv_w", "lru_out_norm", "sc_out_norm",
                "w_out", "ffn2_norm", "ffn2_w_gate", "ffn2_w_up", "ffn2_w_down", "final_norm")
BIG = (("ffn1_w_gate", True), ("ffn1_w_up", True), ("ffn1_w_down", False), ("ffn2_w_gate", True), ("ffn2_w_up", True),
       ("ffn2_w_down", False), ("w_in", True), ("w_out", False))


SLAB_ROW_ALIGN = 256


def _pack_rows(parts, width):
    rows, counts = [], []
    for p in parts:
        flat = p.reshape(-1)
        nr = -(-flat.shape[0] // width)
        nr = -(-nr // SUBLANES) * SUBLANES
        rows.append(jnp.pad(flat, (0, nr * width - flat.shape[0])).reshape(nr, width))
        counts.append(nr)
    total = sum(counts)
    pad = -(-total // SLAB_ROW_ALIGN) * SLAB_ROW_ALIGN - total
    if pad:
        rows.append(jnp.zeros((pad, width), rows[0].dtype))
    return jnp.concatenate(rows, axis=0), counts


def _stack_rows(blocks):
    pieces, off = [], 0
    for b in blocks:
        pieces.append((off, b.shape[0]))
        off += b.shape[0]
    return jnp.concatenate(blocks, axis=0), pieces


def _unpack_rows(slab, counts, shapes):
    out, r = [], 0
    for nr, shape in zip(counts, shapes):
        size = math.prod(shape)
        out.append(slab[r:r + nr].reshape(-1)[:size].reshape(shape))
        r += nr
    return out


def kernel(x, ffn1_norm, ffn1_w_gate, ffn1_w_up, ffn1_w_down, mix_norm, w_in, lru_conv_w, lru_conv_b, lru_w_a, lru_b_a, lru_w_i, lru_b_i, lru_lambda, sc_conv_w, lru_out_norm, sc_out_norm, w_out, ffn2_norm, ffn2_w_gate, ffn2_w_up, ffn2_w_down, final_norm, loss_target, m_ffn1_norm, m_ffn1_w_gate, m_ffn1_w_up, m_ffn1_w_down, m_mix_norm, m_w_in, m_lru_conv_w, m_lru_conv_b, m_lru_w_a, m_lru_b_a, m_lru_w_i, m_lru_b_i, m_lru_lambda, m_sc_conv_w, m_lru_out_norm, m_sc_out_norm, m_w_out, m_ffn2_norm, m_ffn2_w_gate, m_ffn2_w_up, m_ffn2_w_down, m_final_norm, v_ffn1_norm, v_ffn1_w_gate, v_ffn1_w_up, v_ffn1_w_down, v_mix_norm, v_w_in, v_lru_conv_w, v_lru_conv_b, v_lru_w_a, v_lru_b_a, v_lru_w_i, v_lru_b_i, v_lru_lambda, v_sc_conv_w, v_lru_out_norm, v_sc_out_norm, v_w_out, v_ffn2_norm, v_ffn2_w_gate, v_ffn2_w_up, v_ffn2_w_down, v_final_norm):
    a = dict(locals())
    w = {n: a[n] for n in WEIGHT_NAMES}
    m = {n: a["m_" + n] for n in WEIGHT_NAMES}
    v = {n: a["v_" + n] for n in WEIGHT_NAMES}
    ax, ay, ac = _place()
    dev = 4 * ax + 2 * ay + ac
    chip = (2 * ax + ay).astype(jnp.int32).reshape(1)

    x0 = x[0]
    tgt = loss_target[0]
    T, D = x0.shape
    C = D // 2
    H, hd = lru_w_a.shape[1], lru_w_a.shape[2]
    CL = lru_conv_w.shape[2]

    shards = []
    for name, transposed in BIG:
        s = w[name][0]
        shards.append((s.T if transposed else s).astype(BF16))
    taps = jnp.concatenate([lru_conv_w[0], sc_conv_w[0], jnp.zeros((1, CL), F32)], axis=0)
    taps_row = lax.bitcast_convert_type(taps, BF16).reshape(1, -1)
    taps_blk = jnp.pad(taps_row, ((0, BF16_ROWS - 1), (0, D - taps_row.shape[1])))
    s_wg1, s_wu1, s_wd1, s_wg2, s_wu2, s_wd2, s_win, s_wout = shards
    slab_g1, pcs_g1 = _stack_rows([s_wg1])
    slab_u1, pcs_u1 = _stack_rows([s_wu1])
    slab_d1, pcs_d1 = _stack_rows([s_wd1])
    slab_mw, pcs_mw = _stack_rows([s_win, s_wout, taps_blk])
    slab_g2, pcs_g2 = _stack_rows([s_wg2])
    slab_ud2, pcs_ud2 = _stack_rows([s_wu2, s_wd2])
    (wg1,) = _allgather(slab_g1, pcs_g1, "allgather_ffn1_gate")

    g1, gm, g3 = ffn1_norm, mix_norm, ffn2_norm
    phase = _merge_phases(_ag_two_level_phase(slab_u1, pcs_u1), _ag_direct_phase(slab_d1, pcs_d1, SAME_CORE_AND_SIBLING))
    (n1, hg1), got = _norm_proj(x0, g1, [wg1], [BF16], False, "ffn1_gate", carried=phase)
    wu1, d1 = got[0], got[1:]
    phase = _merge_phases(_ag_forward_phase(d1, pcs_d1), _ag_direct_phase(slab_mw, pcs_mw, SAME_CORE_AND_SIBLING))
    (hu1, act1), got = _up_act(n1, wu1, hg1, "ffn1_up", carried=phase)
    wd1, mixw = got[0], got[1:]
    phase = _merge_phases(_ag_forward_phase(mixw, pcs_mw), _ag_direct_phase(slab_g2, pcs_g2, SAME_CORE_AND_SIBLING))
    x1, got = _mm_res(act1, wd1, x0, 0.5, "ffn1_down", carried=phase)
    (win, wout, taps_all), g2 = got[:3], got[3:]
    phase = _merge_phases(_ag_forward_phase(g2, pcs_g2), _ag_direct_phase(slab_ud2, pcs_ud2, SAME_CORE_AND_SIBLING))
    (n2, z), got = _norm_proj(x1, gm, [win], [F32], False, "in_proj", carried=phase)
    (wg2,), ud2 = got[:1], got[1:]
    taps_all = taps_all.reshape(N_DEV, BF16_ROWS, D)[:, 0, :2 * SUBLANES * CL].reshape(N_DEV, SUBLANES, CL, 2)
    taps_all = lax.bitcast_convert_type(taps_all, F32)
    taps_all = taps_all.transpose(1, 0, 2).reshape(SUBLANES, N_DEV * CL)
    cw, sw = taps_all[0:4], taps_all[4:7]

    gf = final_norm.reshape(1, D)
    cb = lru_conv_b
    wa, wi = lru_w_a[0].astype(BF16), lru_w_i[0].astype(BF16)
    ba, bi = lru_b_a.reshape(1, C), lru_b_i.reshape(1, C)
    lam, glo, gso = lru_lambda, lru_out_norm, sc_out_norm

    y, h, *saved = _mix_fwd(z, cw, cb, wa, ba, wi, bi, lam, sw, glo, gso, "mix_fwd")
    x2, (wu2, wd2) = _mm_res(y, wout, x1, 1.0, "out_proj", carried=_ag_forward_phase(ud2, pcs_ud2))
    n3, hg2, hu2, act2 = _norm_proj(x2, g3, [wg2, wu2], [BF16, BF16], True, "ffn2_up")
    x3 = _mm_res(act2, wd2, x2, 0.5, "ffn2_down")
    dx3, df2, d_gf, loss_blk = _loss_head(x3, gf, tgt, "loss_head")

    F = wd1.shape[0]
    bm_f = F // 4 if (F // 4) % LANES == 0 else 512

    def reduce_group(gs, tag):
        pcs, off = [], 0
        for g_ in gs:
            pcs.append((off, g_.shape[0] // N_DEV))
            off += g_.shape[0] // N_DEV
        sb_ = _rs_sibling(gs, pcs, "rs_sibling_add_" + tag)
        return sb_, pcs

    dhg2, dhu2 = _ffn_bwd_act(df2, wd2, hg2, hu2, "ffn2_bwd_act")
    d_wd2 = _dw_tn(act2, df2, bm_f, "ffn2_dw_down")
    d_wg2 = _dw_tn(dhg2, n3, bm_f, "ffn2_dw_gate")
    d_wu2 = _dw_tn(dhu2, n3, bm_f, "ffn2_dw_up")
    sb_f2, pcs_f2 = reduce_group([d_wg2, d_wu2, d_wd2], "ffn2")
    (dx2, dx2b, d_g3), (lb_f2,) = _mm_rmsbwd([(dhg2, wg2), (dhu2, wu2)], x2, g3, dx3, 1.0, "ffn2_bwd_in",
                                             carried=_rs_chips_phase(sb_f2))
    dy = _mm_nt(dx2b, wout, "out_proj_bwd")
    d_wout = _dw_tn(y, dx2b, 1024, "out_proj_dw")
    dz, small, d_wa, d_wi = _mix_bwd(z, h, saved, dy, cw, wa, wi, lam, sw, glo, gso, "mix_bwd")
    d_win = _dw_tn(dz, n2, 1280, "in_proj_dw")
    sb_mx, pcs_mx = reduce_group([d_win, d_wout], "mix")
    (dx1, df1, d_gm), (lb_mx,) = _mm_rmsbwd([(dz, win)], x1, gm, dx2, 0.5, "in_proj_bwd",
                                            carried=_rs_chips_phase(sb_mx))
    dhg1, dhu1 = _ffn_bwd_act(df1, wd1, hg1, hu1, "ffn1_bwd_act")
    d_wd1 = _dw_tn(act1, df1, bm_f, "ffn1_dw_down")
    d_wg1 = _dw_tn(dhg1, n1, bm_f, "ffn1_dw_gate")
    d_wu1 = _dw_tn(dhu1, n1, bm_f, "ffn1_dw_up")
    sb_f1, pcs_f1 = reduce_group([d_wg1, d_wu1, d_wd1], "ffn1")
    (dx0, _, d_g1), (lb_f1,) = _mm_rmsbwd([(dhg1, wg1), (dhu1, wu1)], x0, g1, dx1, 1.0, "ffn1_bwd_in",
                                          carried=_rs_chips_phase(sb_f1))

    grads, delta, new_m, new_v = {}, {}, {}, {}
    transposed_shard = dict(BIG)
    for names, sb_, lb_, pcs in ((("ffn2_w_gate", "ffn2_w_up", "ffn2_w_down"), sb_f2, lb_f2, pcs_f2),
                                 (("w_in", "w_out"), sb_mx, lb_mx, pcs_mx),
                                 (("ffn1_w_gate", "ffn1_w_up", "ffn1_w_down"), sb_f1, lb_f1, pcs_f1)):
        for name, (off, rows) in zip(names, pcs):
            flip = transposed_shard[name] and w[name].shape[2] % LANES != 0
            if transposed_shard[name] and not flip:
                g_ = _final_grad(sb_, lb_, chip, off, rows, "rs_final_sum_" + name).T
                d_, m_, v_ = _adamw(w[name][0], g_, m[name][0], v[name][0], "adamw_" + name)
            else:
                view = (lambda t: t[0].T) if flip else (lambda t: t[0])
                g_, d_, m_, v_ = _sum_adamw(sb_, lb_, chip, off, rows, view(w[name]), view(m[name]), view(v[name]),
                                            "sum_adamw_" + name)
            back = (lambda t: t.T[None]) if flip else (lambda t: t[None])
            grads[name], delta[name], new_m[name], new_v[name] = back(g_), back(d_), back(m_), back(v_)

    small_parts = [d_g1, d_gm, d_g3, d_gf, small[R_CW:R_CW + 4], small[R_CB], d_wa, small[R_BA], d_wi, small[R_BI],
                   small[R_LAM], small[R_SW:R_SW + 3], small[R_GLO], small[R_GSO]]
    sslab, counts = _pack_rows(small_parts, LANES)
    RS = sslab.shape[0]
    (sg,) = _allgather(sslab, [(0, RS)], "allgather_small_grads")
    ssum = _add_slabs([sg[j * RS:(j + 1) * RS] for j in range(N_DEV)], F32, "small_grads_sum")
    full_shapes = [(1, D), (1, D), (1, D), (D,), (1, 4, C), (1, C), (1, H, hd, hd), (1, H, hd), (1, H, hd, hd), (1, H, hd),
                   (1, C), (1, 3, C), (1, C), (1, C)]
    small_full = dict(zip(SMALL_NAMES, _unpack_rows(ssum, counts, full_shapes)))

    for name in SMALL_NAMES:
        gfull = small_full[name]
        if name in ("lru_conv_w", "sc_conv_w"):
            gfull = lax.dynamic_slice_in_dim(gfull, dev * CL, CL, axis=2)
        grads[name] = gfull

    packs = [_pack_rows([t[n_] for n_ in SMALL_NAMES], LANES) for t in (w, grads, m, v)]
    sd, sm, sv = _adamw(packs[0][0], packs[1][0], packs[2][0], packs[3][0], "adamw_small")
    shapes = [w[n_].shape for n_ in SMALL_NAMES]
    for tgt_dict, slab_ in ((delta, sd), (new_m, sm), (new_v, sv)):
        for n_, val in zip(SMALL_NAMES, _unpack_rows(slab_, packs[0][1], shapes)):
            tgt_dict[n_] = val

    loss = lax.psum(loss_blk[0, 0], ("x", "y", "c"))
    return (loss, dx0[None], *[grads[n_] for n_ in WEIGHT_NAMES], *[delta[n_] for n_ in WEIGHT_NAMES],
            *[new_m[n_] for n_ in WEIGHT_NAMES], *[new_v[n_] for n_ in WEIGHT_NAMES])
```

```python
import functools
import math

import jax
import jax.numpy as jnp
from jax import lax
from jax.experimental import pallas as pl
from jax.experimental.pallas import tpu as pltpu

F32 = jnp.float32
BF16 = jnp.bfloat16
SDS = jax.ShapeDtypeStruct
MESH = pl.DeviceIdType.MESH

NORM_EPS = 1e-6
LRU_C = 8.0
N_DEV = 8
N_CHIP = 4
ADAM_LR, ADAM_B1, ADAM_B2, ADAM_EPS, ADAM_WD, ADAM_STEP = 0.001, 0.9, 0.999, 1e-08, 0.01, 10

NN = (((1,), (0,)), ((), ()))
NT = (((1,), (1,)), ((), ()))
TN = (((0,), (0,)), ((), ()))

SUBLANES = 8
BF16_ROWS = 16
LANES = 128
MIB = 1 << 20


def _dot(a, b, dims):
    return lax.dot_general(a, b, dims, preferred_element_type=F32)


def _blk(n, pref, align):
    if n <= pref:
        return n
    b = (pref // align) * align
    while b >= align:
        if n % b == 0:
            return b
        b -= align
    raise ValueError(f"no block of {n} aligned to {align} under {pref}")


def _cp(sem, vmem_mib):
    return pltpu.CompilerParams(dimension_semantics=sem, vmem_limit_bytes=vmem_mib * MIB)


HBM_SPEC = pl.BlockSpec(memory_space=pltpu.HBM)
MID_EIGHTHS = 5

class _Carried:
    def __init__(self, inputs, out_shapes, aliases, sem_shapes, build, has_mid=False):
        self.inputs, self.out_shapes, self.aliases = list(inputs), list(out_shapes), dict(aliases)
        self.sem_shapes, self.build, self.has_mid = list(sem_shapes), build, has_mid


def _call(body, *, name, grid, in_specs, out_specs, out_shape, scratch_shapes, compiler_params, args, carried=None):
    if carried is None:
        return pl.pallas_call(body, name=name, grid=grid, in_specs=in_specs, out_specs=out_specs, out_shape=out_shape,
                              scratch_shapes=scratch_shapes, compiler_params=compiler_params)(*args)
    n_in, n_out, n_sc = len(in_specs), len(out_shape), len(scratch_shapes)
    c_in, c_out = len(carried.inputs), len(carried.out_shapes)

    def hosted(*refs):
        ins, refs = refs[:n_in], refs[n_in:]
        c_ins, refs = refs[:c_in], refs[c_in:]
        outs, refs = refs[:n_out], refs[n_out:]
        c_outs, refs = refs[:c_out], refs[c_out:]
        scratch, c_sems = refs[:n_sc], refs[n_sc:]
        first = functools.reduce(jnp.logical_and, [pl.program_id(a) == 0 for a in range(len(grid))])
        last = functools.reduce(jnp.logical_and, [pl.program_id(a) == g - 1 for a, g in enumerate(grid)])

        @pl.when(first)
        def _():
            for start in carried.build(c_ins, c_outs, c_sems, "start"):
                start()

        if carried.has_mid:
            mid = functools.reduce(jnp.logical_and, [pl.program_id(0) == (grid[0] * MID_EIGHTHS) // 8]
                                   + [pl.program_id(a) == 0 for a in range(1, len(grid))])

            @pl.when(mid)
            def _():
                for step in carried.build(c_ins, c_outs, c_sems, "mid"):
                    step()

        body(*ins, *outs, *scratch)

        @pl.when(last)
        def _():
            for wait in carried.build(c_ins, c_outs, c_sems, "end"):
                wait()

    out = pl.pallas_call(
        hosted, name=name, grid=grid, in_specs=list(in_specs) + [HBM_SPEC] * c_in,
        out_specs=list(out_specs) + [HBM_SPEC] * c_out, out_shape=list(out_shape) + carried.out_shapes,
        scratch_shapes=list(scratch_shapes) + carried.sem_shapes,
        input_output_aliases={n_in + a: n_out + b for a, b in carried.aliases.items()},
        compiler_params=compiler_params)(*args, *carried.inputs)
    return out[:n_out], out[n_out:]


ROW_CHUNK = 128


def _chunk_rows(c):
    return pl.ds(pl.multiple_of(c * ROW_CHUNK, ROW_CHUNK), ROW_CHUNK)


def _rstd(xv):
    return lax.rsqrt(jnp.mean(xv * xv, axis=-1, keepdims=True) + NORM_EPS)


def _rms_bwd(xv, g, dn):
    r = _rstd(xv)
    xr = xv * r
    gd = g * dn
    dx = r * (gd - xr * jnp.mean(gd * xr, axis=-1, keepdims=True))
    return dx, jnp.sum(dn * xr, axis=0, keepdims=True)


def _log1p(e):
    u = 1.0 + e
    return jnp.where(u == 1.0, e, jnp.log(u) * (e / (u - 1.0)))


def _one_minus_exp(v, exp_half_v):
    series = 1.0 / 5040.0
    for coeff in (1.0 / 720.0, 1.0 / 120.0, 1.0 / 24.0, 1.0 / 6.0, 0.5, 1.0):
        series = series * v + coeff
    return jnp.where(v > -0.5, -v * series, 1.0 - exp_half_v * exp_half_v)


def _sigmoid(v):
    return 0.5 * jnp.tanh(0.5 * v) + 0.5


def _gelu_parts(g):
    k0 = math.sqrt(2.0 / math.pi)
    g2 = g * g
    t = jnp.tanh(k0 * (g + 0.044715 * g * g2))
    gel = 0.5 * g * (1.0 + t)
    gelp = 0.5 * (1.0 + t) + 0.5 * g * (1.0 - t * t) * (k0 * (1.0 + 3.0 * 0.044715 * g2))
    return gel, gelp


def _norm_proj(x, gain, w_list, out_dtypes, swiglu, name, carried=None):
    T, D = x.shape
    N = w_list[0].shape[0]
    nw = len(w_list)
    bm = _blk(T, 1024, BF16_ROWS)
    bn = _blk(N, 1024 // nw, LANES)

    def body(*refs):
        x_ref, g_ref = refs[:2]
        w_refs = refs[2:2 + nw]
        n_ref = refs[2 + nw]
        o_refs = refs[3 + nw:3 + 2 * nw]
        act_ref = refs[3 + 2 * nw] if swiglu else None
        n_sc = refs[-1]

        @pl.when(pl.program_id(1) == 0)
        def _():
            def chunk(c, _):
                r = _chunk_rows(c)
                xv = x_ref[r, :]
                nb = (xv * _rstd(xv) * g_ref[...]).astype(BF16)
                n_sc[r, :] = nb
                n_ref[r, :] = nb
                return 0
            lax.fori_loop(0, bm // ROW_CHUNK, chunk, 0)

        n = n_sc[...]
        outs = [_dot(n, w_ref[...], NT) for w_ref in w_refs]
        for o_ref, o in zip(o_refs, outs):
            o_ref[...] = o.astype(o_ref.dtype)
        if swiglu:
            hg, hu = outs
            act_ref[...] = (hg * _sigmoid(hg) * hu).astype(BF16)

    row = pl.BlockSpec((bm, D), lambda i, j: (i, 0))
    tile = pl.BlockSpec((bm, bn), lambda i, j: (i, j))
    n_extra = 1 if swiglu else 0
    return _call(
        body, name=name, grid=(T // bm, N // bn),
        in_specs=[row, pl.BlockSpec((1, D), lambda i, j: (0, 0))] + [pl.BlockSpec((bn, D), lambda i, j: (j, 0))] * nw,
        out_specs=[row] + [tile] * (nw + n_extra),
        out_shape=[SDS((T, D), BF16)] + [SDS((T, N), dt) for dt in out_dtypes] + [SDS((T, N), BF16)] * n_extra,
        scratch_shapes=[pltpu.VMEM((bm, D), BF16)],
        compiler_params=_cp(("arbitrary", "arbitrary"), 52),
        args=(x, gain, *w_list), carried=carried)


def _up_act(n, wu, hg, name, carried=None):
    T, D = n.shape
    F = wu.shape[0]
    bm = _blk(T, 1024, BF16_ROWS)
    bn = _blk(F, 512, LANES)

    def body(n_ref, wu_ref, hg_ref, hu_ref, act_ref):
        hu = _dot(n_ref[...], wu_ref[...], NT)
        hg = hg_ref[...].astype(F32)
        hu_ref[...] = hu.astype(BF16)
        act_ref[...] = (hg * _sigmoid(hg) * hu).astype(BF16)

    tile = pl.BlockSpec((bm, bn), lambda i, j: (i, j))
    return _call(
        body, name=name, grid=(T // bm, F // bn),
        in_specs=[pl.BlockSpec((bm, D), lambda i, j: (i, 0)), pl.BlockSpec((bn, D), lambda i, j: (j, 0)), tile],
        out_specs=[tile, tile], out_shape=[SDS((T, F), BF16)] * 2, scratch_shapes=[],
        compiler_params=_cp(("arbitrary", "arbitrary"), 40),
        args=(n, wu, hg), carried=carried)


def _mm_res(a, b, x, scale, name, carried=None):
    T, K = a.shape
    D = b.shape[1]
    bm = _blk(T, 1024, BF16_ROWS)
    bk = _blk(K, 1408, LANES)
    nk = K // bk

    def body(a_ref, b_ref, x_ref, o_ref):
        k = pl.program_id(1)

        @pl.when(k == 0)
        def _():
            o_ref[...] = jnp.zeros_like(o_ref)

        o_ref[...] += _dot(a_ref[...], b_ref[...], NN)

        @pl.when(k == nk - 1)
        def _():
            def chunk(c, _):
                r = _chunk_rows(c)
                o_ref[r, :] = x_ref[r, :] + scale * o_ref[r, :]
                return 0
            lax.fori_loop(0, bm // ROW_CHUNK, chunk, 0)

    row = pl.BlockSpec((bm, D), lambda i, k: (i, 0))
    out = _call(
        body, name=name, grid=(T // bm, nk),
        in_specs=[pl.BlockSpec((bm, bk), lambda i, k: (i, k)), pl.BlockSpec((bk, D), lambda i, k: (k, 0)), row],
        out_specs=[row], out_shape=[SDS((T, D), F32)], scratch_shapes=[],
        compiler_params=_cp(("arbitrary", "arbitrary"), 56),
        args=(a, b, x), carried=carried)
    return out[0] if carried is None else (out[0][0], out[1])


def _mm_nt(a, b, name):
    T, K = a.shape
    N = b.shape[0]
    bm = _blk(T, 1024, BF16_ROWS)
    bn = _blk(N, 512, LANES)

    def body(a_ref, b_ref, o_ref):
        o_ref[...] = _dot(a_ref[...], b_ref[...], NT)

    return pl.pallas_call(
        body, name=name, grid=(T // bm, N // bn),
        in_specs=[pl.BlockSpec((bm, K), lambda i, j: (i, 0)), pl.BlockSpec((bn, K), lambda i, j: (j, 0))],
        out_specs=pl.BlockSpec((bm, bn), lambda i, j: (i, j)), out_shape=SDS((T, N), F32),
        compiler_params=_cp(("arbitrary", "arbitrary"), 40),
    )(a, b)


def _ffn_bwd_act(dfb, wd, hg, hu, name):
    T, D = dfb.shape
    F = wd.shape[0]
    bm = _blk(T, 1024, BF16_ROWS)
    bn = _blk(F, 512, LANES)

    def body(df_ref, wd_ref, hg_ref, hu_ref, dhg_ref, dhu_ref):
        dact = _dot(df_ref[...], wd_ref[...], NT)
        hgv = hg_ref[...].astype(F32)
        huv = hu_ref[...].astype(F32)
        s = _sigmoid(hgv)
        dhu_ref[...] = (dact * (hgv * s)).astype(BF16)
        dhg_ref[...] = (dact * huv * (s * (1.0 + hgv * (1.0 - s)))).astype(BF16)

    tile = pl.BlockSpec((bm, bn), lambda i, j: (i, j))
    return pl.pallas_call(
        body, name=name, grid=(T // bm, F // bn),
        in_specs=[pl.BlockSpec((bm, D), lambda i, j: (i, 0)), pl.BlockSpec((bn, D), lambda i, j: (j, 0)), tile, tile],
        out_specs=[tile, tile], out_shape=[SDS((T, F), BF16)] * 2,
        compiler_params=_cp(("arbitrary", "arbitrary"), 40),
    )(dfb, wd, hg, hu)


def _dw_tn(a, b, bm_pref, name):
    T, M = a.shape
    N = b.shape[1]
    bm = _blk(M, bm_pref, LANES)
    tk = _blk(T, 1024, BF16_ROWS)
    nk = T // tk

    def body(a_ref, b_ref, o_ref, acc):
        k = pl.program_id(1)

        @pl.when(k == 0)
        def _():
            acc[...] = jnp.zeros_like(acc)

        acc[...] += _dot(a_ref[...], b_ref[...], TN)

        @pl.when(k == nk - 1)
        def _():
            o_ref[...] = acc[...].astype(BF16)

    return pl.pallas_call(
        body, name=name, grid=(M // bm, nk),
        in_specs=[pl.BlockSpec((tk, bm), lambda i, k: (k, i)), pl.BlockSpec((tk, N), lambda i, k: (k, 0))],
        out_specs=pl.BlockSpec((bm, N), lambda i, k: (i, 0)), out_shape=SDS((M, N), BF16),
        scratch_shapes=[pltpu.VMEM((bm, N), F32)],
        compiler_params=_cp(("arbitrary", "arbitrary"), 48),
    )(a, b)


def _mm_rmsbwd(pairs, x, gain, dx_in, bscale, name, carried=None):
    T, D = x.shape
    K = pairs[0][0].shape[1]
    npair = len(pairs)
    bm = _blk(T, 1024, BF16_ROWS)
    bk = _blk(K, 1024 // npair, LANES)
    nk = K // bk

    nchunk = bm // ROW_CHUNK

    def body(*refs):
        ab = refs[:2 * npair]
        x_hbm, g_ref, dxin_hbm, dx_ref, dxb_ref, dg_ref, x_buf, dxin_buf, sems = refs[2 * npair:]
        i = pl.program_id(0)
        k = pl.program_id(1)

        def fetch(c, slot):
            rows = pl.ds(i * bm + c * ROW_CHUNK, ROW_CHUNK)
            return (pltpu.make_async_copy(x_hbm.at[rows, :], x_buf.at[slot], sems.at[slot, 0]),
                    pltpu.make_async_copy(dxin_hbm.at[rows, :], dxin_buf.at[slot], sems.at[slot, 1]))

        @pl.when(k == 0)
        def _():
            dx_ref[...] = jnp.zeros_like(dx_ref)

        @pl.when(k == nk - 1)
        def _():
            for cp in fetch(0, 0):
                cp.start()

        for q in range(npair):
            dx_ref[...] += _dot(ab[2 * q][...], ab[2 * q + 1][...], NN)

        @pl.when(k == nk - 1)
        def _():
            @pl.when(i == 0)
            def _():
                dg_ref[...] = jnp.zeros_like(dg_ref)

            def chunk(c, _):
                slot = c % 2

                @pl.when(c + 1 < nchunk)
                def _():
                    for cp in fetch(c + 1, 1 - slot):
                        cp.start()

                for cp in fetch(c, slot):
                    cp.wait()

                r = _chunk_rows(c)
                dx, dg = _rms_bwd(x_buf[slot], g_ref[...], dx_ref[r, :])
                dxo = dxin_buf[slot] + dx
                dx_ref[r, :] = dxo
                dxb_ref[r, :] = (bscale * dxo).astype(BF16)
                dg_ref[...] += dg
                return 0
            lax.fori_loop(0, nchunk, chunk, 0)

    row = pl.BlockSpec((bm, D), lambda i, k: (i, 0))
    anywhere = pl.BlockSpec(memory_space=pl.ANY)
    vec = pl.BlockSpec((1, D), lambda i, k: (0, 0))
    in_specs = []
    args = []
    for a, b in pairs:
        in_specs += [pl.BlockSpec((bm, bk), lambda i, k: (i, k)), pl.BlockSpec((bk, D), lambda i, k: (k, 0))]
        args += [a, b]
    return _call(
        body, name=name, grid=(T // bm, nk),
        in_specs=in_specs + [anywhere, vec, anywhere], out_specs=[row, row, vec],
        out_shape=[SDS((T, D), F32), SDS((T, D), BF16), SDS((1, D), F32)],
        scratch_shapes=[pltpu.VMEM((2, ROW_CHUNK, D), F32), pltpu.VMEM((2, ROW_CHUNK, D), F32),
                        pltpu.SemaphoreType.DMA((2, 2))],
        compiler_params=_cp(("arbitrary", "arbitrary"), 52),
        args=(*args, x, gain, dx_in), carried=carried)


def _loss_head(x3, gain, tgt, name):
    T, D = x3.shape
    bm = _blk(T, 256, BF16_ROWS)

    def body(x_ref, g_ref, t_ref, dx_ref, dxb_ref, dg_ref, loss_ref):
        i = pl.program_id(0)
        xv = x_ref[...]
        g = g_ref[...]
        out = xv * _rstd(xv) * g
        e = out - t_ref[...]
        part = 0.5 * jnp.sum(jnp.mean(e * e, axis=-1, keepdims=True), axis=0, keepdims=True)
        dx, dg = _rms_bwd(xv, g, e * (1.0 / D))
        dx_ref[...] = dx
        dxb_ref[...] = (0.5 * dx).astype(BF16)

        @pl.when(i == 0)
        def _():
            dg_ref[...] = dg
            loss_ref[...] = jnp.broadcast_to(part, loss_ref.shape)

        @pl.when(i > 0)
        def _():
            dg_ref[...] += dg
            loss_ref[...] += jnp.broadcast_to(part, loss_ref.shape)

    row = pl.BlockSpec((bm, D), lambda i: (i, 0))
    vec = pl.BlockSpec((1, D), lambda i: (0, 0))
    return pl.pallas_call(
        body, name=name, grid=(T // bm,),
        in_specs=[row, vec, row], out_specs=[row, row, vec, pl.BlockSpec((SUBLANES, LANES), lambda i: (0, 0))],
        out_shape=[SDS((T, D), F32), SDS((T, D), BF16), SDS((1, D), F32), SDS((SUBLANES, LANES), F32)],
        compiler_params=_cp(("arbitrary",), 40),
    )(x3, gain, tgt)


R_CW, R_CB, R_BA, R_BI, R_LAM, R_SW, R_GLO, R_GSO, SMALL_ROWS = 0, 4, 5, 6, 7, 8, 11, 12, 16


def _rows(g):
    return pl.ds(pl.multiple_of(g * SUBLANES, SUBLANES), SUBLANES)


def _shift_back(prev, cur, d):
    row = lax.broadcasted_iota(jnp.int32, cur.shape, 0)
    return pltpu.roll(jnp.where(row >= SUBLANES - d, prev, cur), d, 0)


def _shift_fwd(cur, nxt, d):
    row = lax.broadcasted_iota(jnp.int32, cur.shape, 0)
    return pltpu.roll(jnp.where(row < d, nxt, cur), SUBLANES - d, 0)


def _causal_conv(ext, g, taps_ref, ntap):
    prev = ext[_rows(g), :]
    cur = ext[_rows(g + 1), :]
    out = _shift_back(prev, cur, ntap - 1) * taps_ref[0:1, :]
    for k in range(1, ntap - 1):
        out = out + _shift_back(prev, cur, ntap - 1 - k) * taps_ref[k:k + 1, :]
    return out + cur * taps_ref[ntap - 1:ntap, :]


def _scan8(A, U, reverse):
    row = lax.broadcasted_iota(jnp.int32, A.shape, 0)
    for s in (1, 2, 4):
        if reverse:
            A_sh = pltpu.roll(A, SUBLANES - s, 0)
            U_sh = pltpu.roll(U, SUBLANES - s, 0)
            m = row < SUBLANES - s
        else:
            A_sh = pltpu.roll(A, s, 0)
            U_sh = pltpu.roll(U, s, 0)
            m = row >= s
        U = jnp.where(m, A * U_sh + U, U)
        A = jnp.where(m, A * A_sh, A)
    return A, U


def _gate_pre(xc_s, w_ref, out_s, H, hd):
    for h in range(H):
        cs = slice(h * hd, (h + 1) * hd)
        out_s[:, cs] = _dot(xc_s[:, cs].astype(BF16), w_ref[h], NN)


def _lru_coeffs(pa, pi, xc, ba, bi, sp):
    ra = _sigmoid(pa + ba)
    ri = _sigmoid(pi + bi)
    log_a = (-LRU_C * ra) * sp
    a = jnp.exp(log_a)
    mult = jnp.sqrt(_one_minus_exp(2.0 * log_a, a))
    return ra, ri, a, mult


def _softplus_neg(lam):
    v = -lam
    return jnp.maximum(v, 0.0) + _log1p(jnp.exp(-jnp.abs(v)))


def _mix_fwd(z, cw, cb, wa, ba, wi, bi, lam, sw, glo, gso, name):
    T = z.shape[0]
    C = z.shape[1] // 5
    H = wa.shape[0]
    hd = C // H
    tb = _blk(T, 256, BF16_ROWS)
    ng = tb // SUBLANES
    HDR = SUBLANES

    def body(z_ref, cw_ref, cb_ref, wa_ref, ba_ref, wi_ref, bi_ref, lam_ref, sw_ref, glo_ref, gso_ref,
             y_ref, h_ref, ra_ref, ri_ref, a_ref, m_ref, xc_s, q_ref, xext, pext, pa_s, pi_s, y_s, hcar):
        @pl.when(pl.program_id(0) == 0)
        def _():
            xext[0:HDR, :] = jnp.zeros((HDR, C), F32)
            pext[0:HDR, :] = jnp.zeros((HDR, C), F32)
            hcar[...] = jnp.zeros_like(hcar)

        def fill(g, _):
            r = _rows(g)
            re = _rows(g + 1)
            xext[re, :] = z_ref[r, 0:C]
            pext[re, :] = z_ref[r, 3 * C:4 * C] * z_ref[r, 4 * C:5 * C]
            return 0
        lax.fori_loop(0, ng, fill, 0)

        def conv(g, _):
            xc_s[_rows(g), :] = _causal_conv(xext, g, cw_ref, 4) + cb_ref[...]
            return 0
        lax.fori_loop(0, ng, conv, 0)

        _gate_pre(xc_s, wa_ref, pa_s, H, hd)
        _gate_pre(xc_s, wi_ref, pi_s, H, hd)
        sp = _softplus_neg(lam_ref[...])

        def group(g, hprev):
            r = _rows(g)
            xc = xc_s[r, :]
            ra, ri, a, mult = _lru_coeffs(pa_s[r, :], pi_s[r, :], xc, ba_ref[...], bi_ref[...], sp)
            ra_ref[r, :] = ra
            ri_ref[r, :] = ri
            a_ref[r, :] = a
            m_ref[r, :] = mult
            A, U = _scan8(a, mult * (ri * xc), reverse=False)
            hh = A * hprev + U
            h_ref[r, :] = hh
            gel, _ = _gelu_parts(z_ref[r, C:2 * C])
            y_lru = hh * gel
            y_s[r, 0:C] = y_lru * _rstd(y_lru) * glo_ref[...]
            q = _causal_conv(pext, g, sw_ref, 3)
            q_ref[r, :] = q
            y_sc = z_ref[r, 2 * C:3 * C] * q
            y_s[r, C:2 * C] = y_sc * _rstd(y_sc) * gso_ref[...]
            return jnp.broadcast_to(hh[SUBLANES - 1:SUBLANES, :], hh.shape)
        hcar[...] = lax.fori_loop(0, ng // 2, lambda t, hp: group(2 * t + 1, group(2 * t, hp)), hcar[...])

        xext[0:HDR, :] = xext[tb:tb + HDR, :]
        pext[0:HDR, :] = pext[tb:tb + HDR, :]

        def cast(g, _):
            r = pl.ds(pl.multiple_of(g * BF16_ROWS, BF16_ROWS), BF16_ROWS)
            y_ref[r, :] = y_s[r, :].astype(BF16)
            return 0
        lax.fori_loop(0, tb // BF16_ROWS, cast, 0)

    full = lambda shape: pl.BlockSpec(shape, lambda i: (0,) * len(shape))
    blk = lambda w: pl.BlockSpec((tb, w), lambda i: (i, 0))
    ext = pltpu.VMEM((tb + HDR, C), F32)
    tile = pltpu.VMEM((tb, C), F32)
    return pl.pallas_call(
        body, name=name, grid=(T // tb,),
        in_specs=[blk(5 * C), full((4, C)), full((1, C)), full((H, hd, hd)), full((1, C)), full((H, hd, hd)),
                  full((1, C)), full((1, C)), full((3, C)), full((1, C)), full((1, C))],
        out_specs=[blk(2 * C)] + [blk(C)] * 7,
        out_shape=[SDS((T, 2 * C), BF16)] + [SDS((T, C), F32)] * 7,
        scratch_shapes=[ext, ext, tile, tile, pltpu.VMEM((tb, 2 * C), F32), pltpu.VMEM((SUBLANES, C), F32)],
        compiler_params=_cp(("arbitrary",), 48),
    )(z, cw, cb, wa, ba, wi, bi, lam, sw, glo, gso)


def _mix_bwd(z, h, saved, dy, cw, wa, wi, lam, sw, glo, gso, name):
    T = z.shape[0]
    C = z.shape[1] // 5
    H = wa.shape[0]
    hd = C // H
    tb = _blk(T, 256, BF16_ROWS)
    nb = T // tb
    ng = tb // SUBLANES
    HDR = SUBLANES
    N_ACC = 13

    def body(z_ref, zp_ref, h_ref, hp_ref, ra_ref, ri_ref, a_ref, m_ref, xc_s, q_ref, dy_ref, cw_ref, wa_ref, wi_ref,
             lam_ref, sw_ref, glo_ref, gso_ref, dz_ref, small_ref, dwa_ref, dwi_ref,
             xext, pext, hext, dqext, dxcext, bext, dh_s, dpa_s, dpi_s, dz_s, acc_s, bcar):
        i = pl.program_id(0)
        first_rows = i == nb - 1

        @pl.when(i == 0)
        def _():
            dqext[tb:tb + HDR, :] = jnp.zeros((HDR, C), F32)
            dxcext[tb:tb + HDR, :] = jnp.zeros((HDR, C), F32)
            bcar[...] = jnp.zeros_like(bcar)
            acc_s[...] = jnp.zeros_like(acc_s)
            dwa_ref[...] = jnp.zeros_like(dwa_ref)
            dwi_ref[...] = jnp.zeros_like(dwi_ref)

        zero = jnp.zeros((HDR, C), F32)
        xext[0:HDR, :] = jnp.where(first_rows, zero, zp_ref[:, 0:C])
        pext[0:HDR, :] = jnp.where(first_rows, zero, zp_ref[:, 3 * C:4 * C] * zp_ref[:, 4 * C:5 * C])
        hext[0:HDR, :] = jnp.where(first_rows, zero, hp_ref[...])

        def fill(g, _):
            r = _rows(g)
            re = _rows(g + 1)
            xext[re, :] = z_ref[r, 0:C]
            pext[re, :] = z_ref[r, 3 * C:4 * C] * z_ref[r, 4 * C:5 * C]
            hext[re, :] = h_ref[r, :]
            return 0
        lax.fori_loop(0, ng, fill, 0)

        sp = _softplus_neg(lam_ref[...])
        dsp_dlam = -jax.nn.sigmoid(-lam_ref[...])

        def add_acc(k, v):
            acc_s[k] += v

        def p1(g, _):
            r = _rows(g)
            hh = h_ref[r, :]
            gel, gelp = _gelu_parts(z_ref[r, C:2 * C])
            y_lru = hh * gel
            dnl = dy_ref[r, 0:C]
            rl = _rstd(y_lru)
            ylr = y_lru * rl
            gd = glo_ref[...] * dnl
            dy_lru = rl * (gd - ylr * jnp.mean(gd * ylr, axis=-1, keepdims=True))
            add_acc(R_GLO, dnl * ylr)
            dz_s[r, C:2 * C] = dy_lru * hh * gelp
            dh = dy_lru * gel
            dh_s[r, :] = dh

            q = q_ref[r, :]
            scb = z_ref[r, 2 * C:3 * C]
            y_sc = scb * q
            dns = dy_ref[r, C:2 * C]
            rs = _rstd(y_sc)
            ysr = y_sc * rs
            gs = gso_ref[...] * dns
            dy_sc = rs * (gs - ysr * jnp.mean(gs * ysr, axis=-1, keepdims=True))
            add_acc(R_GSO, dns * ysr)
            dz_s[r, 2 * C:3 * C] = dy_sc * q
            dqext[r, :] = dy_sc * scb
            return 0
        lax.fori_loop(0, ng, p1, 0, unroll=2)

        bext[tb:tb + HDR, :] = bcar[...]

        def p2(j, carry):
            g = ng - 1 - j
            r = _rows(g)
            a = a_ref[r, :]
            A, U = _scan8(a, a * dh_s[r, :], reverse=True)
            bb = A * carry + U
            bext[r, :] = bb
            return jnp.broadcast_to(bb[0:1, :], bb.shape)
        bcar[...] = lax.fori_loop(0, ng, p2, bcar[...])

        def p3(g, _):
            r = _rows(g)
            rn = _rows(g + 1)
            G = dh_s[r, :] + _shift_fwd(bext[r, :], bext[rn, :], 1)
            hm1 = _shift_back(hext[r, :], hext[rn, :], 1)
            a = a_ref[r, :]
            mult = m_ref[r, :]
            ri = ri_ref[r, :]
            xc = xc_s[r, :]
            ra = ra_ref[r, :]
            dxcext[r, :] = G * mult * ri
            dri = G * mult * xc
            dmult = G * ri * xc
            dlog_a = (G * hm1) * a - dmult * (a * a) / mult
            add_acc(R_LAM, dlog_a * (-LRU_C * ra) * dsp_dlam)
            dpa = dlog_a * (-LRU_C * sp) * ra * (1.0 - ra)
            dpi = dri * ri * (1.0 - ri)
            add_acc(R_BA, dpa)
            add_acc(R_BI, dpi)
            dpa_s[r, :] = dpa
            dpi_s[r, :] = dpi
            return 0
        lax.fori_loop(0, ng, p3, 0)

        for hh_ in range(H):
            cs = slice(hh_ * hd, (hh_ + 1) * hd)
            dpa_b = dpa_s[:, cs].astype(BF16)
            dpi_b = dpi_s[:, cs].astype(BF16)
            xc_b = xc_s[:, cs].astype(BF16)
            dxcext[0:tb, cs] += _dot(dpa_b, wa_ref[hh_], NT) + _dot(dpi_b, wi_ref[hh_], NT)
            dwa_ref[hh_] += _dot(xc_b, dpa_b, TN)
            dwi_ref[hh_] += _dot(xc_b, dpi_b, TN)

        def p4(g, _):
            r = _rows(g)
            rn = _rows(g + 1)
            dxc = dxcext[r, :]
            dxc_n = dxcext[rn, :]
            x_p = xext[r, :]
            x_c = xext[rn, :]
            add_acc(R_CB, dxc)
            dlx = dxc * cw_ref[3:4, :]
            add_acc(R_CW + 3, dxc * x_c)
            for d in range(1, 4):
                dlx = dlx + _shift_fwd(dxc, dxc_n, d) * cw_ref[3 - d:4 - d, :]
                add_acc(R_CW + 3 - d, dxc * _shift_back(x_p, x_c, d))
            dz_s[r, 0:C] = dlx

            dq = dqext[r, :]
            dq_n = dqext[rn, :]
            p_p = pext[r, :]
            p_c = pext[rn, :]
            dp = dq * sw_ref[2:3, :]
            add_acc(R_SW + 2, dq * p_c)
            for d in range(1, 3):
                dp = dp + _shift_fwd(dq, dq_n, d) * sw_ref[2 - d:3 - d, :]
                add_acc(R_SW + 2 - d, dq * _shift_back(p_p, p_c, d))
            dz_s[r, 3 * C:4 * C] = dp * z_ref[r, 4 * C:5 * C]
            dz_s[r, 4 * C:5 * C] = dp * z_ref[r, 3 * C:4 * C]
            return 0
        lax.fori_loop(0, ng, p4, 0)

        dqext[tb:tb + HDR, :] = dqext[0:HDR, :]
        dxcext[tb:tb + HDR, :] = dxcext[0:HDR, :]

        def cast(g, _):
            r = pl.ds(pl.multiple_of(g * BF16_ROWS, BF16_ROWS), BF16_ROWS)
            dz_ref[r, :] = dz_s[r, :].astype(BF16)
            return 0
        lax.fori_loop(0, tb // BF16_ROWS, cast, 0)

        @pl.when(i == nb - 1)
        def _():
            small_ref[...] = jnp.zeros_like(small_ref)
            for k in range(N_ACC):
                small_ref[k:k + 1, :] = jnp.sum(acc_s[k], axis=0, keepdims=True)

    tpg = tb // SUBLANES
    full = lambda shape: pl.BlockSpec(shape, lambda i: (0,) * len(shape))
    blk = lambda w: pl.BlockSpec((tb, w), lambda i: (nb - 1 - i, 0))
    prev = lambda w: pl.BlockSpec((SUBLANES, w), lambda i: (jnp.maximum((nb - 1 - i) * tpg - 1, 0), 0))
    ext = pltpu.VMEM((tb + HDR, C), F32)
    tile = pltpu.VMEM((tb, C), F32)
    return pl.pallas_call(
        body, name=name, grid=(nb,),
        in_specs=[blk(5 * C), prev(5 * C), blk(C), prev(C)] + [blk(C)] * 6
        + [blk(2 * C), full((4, C)), full((H, hd, hd)), full((H, hd, hd)), full((1, C)), full((3, C)),
           full((1, C)), full((1, C))],
        out_specs=[blk(5 * C), full((SMALL_ROWS, C)), full((H, hd, hd)), full((H, hd, hd))],
        out_shape=[SDS((T, 5 * C), BF16), SDS((SMALL_ROWS, C), F32), SDS((H, hd, hd), F32), SDS((H, hd, hd), F32)],
        scratch_shapes=[ext] * 6 + [tile] * 3 + [pltpu.VMEM((tb, 5 * C), F32), pltpu.VMEM((N_ACC, SUBLANES, C), F32),
                                                pltpu.VMEM((SUBLANES, C), F32)],
        compiler_params=_cp(("arbitrary",), 56),
    )(z, z, h, h, *saved, dy, cw, wa, wi, lam, sw, glo, gso)


def _add_slabs(terms, out_dtype, name):
    R, Ccols = terms[0].shape
    br = _blk(R, 512, BF16_ROWS)
    n = len(terms)

    def body(*refs):
        s = refs[0][...].astype(F32)
        for t_ref in refs[1:n]:
            s = s + t_ref[...].astype(F32)
        refs[n][...] = s.astype(out_dtype)

    spec = pl.BlockSpec((br, Ccols), lambda i: (i, 0))
    return pl.pallas_call(
        body, name=name, grid=(R // br,), in_specs=[spec] * n, out_specs=spec, out_shape=SDS((R, Ccols), out_dtype),
        compiler_params=_cp(("arbitrary",), 40),
    )(*terms)


def _reduced_rows(sb, lb, off, rows, extra_in, n_out, body, name, chip):
    Ccols = sb.shape[2]
    br = _blk(math.gcd(off, rows) if off else rows, 192, BF16_ROWS)
    ob = off // br
    src = lambda pick: pl.BlockSpec((1, br, Ccols), lambda i, c: (pick(c), ob + i, 0))
    own = pl.BlockSpec((br, Ccols), lambda i, c: (i, 0))
    return pl.pallas_call(
        body, name=name,
        grid_spec=pltpu.PrefetchScalarGridSpec(
            num_scalar_prefetch=1, grid=(rows // br,),
            in_specs=[src(lambda c: c[0]), src(lambda c: 0), src(lambda c: 1), src(lambda c: 2)] + [own] * len(extra_in),
            out_specs=[own] * n_out),
        out_shape=[SDS((rows, Ccols), F32)] * n_out,
        compiler_params=_cp(("arbitrary",), 40),
    )(chip, sb, lb, lb, lb, *extra_in)


def _sum4(sb_ref, l0, l1, l2):
    s = sb_ref[0].astype(F32)
    for t_ref in (l0, l1, l2):
        s = s + t_ref[0].astype(F32)
    return s


def _final_grad(sb, lb, chip, off, rows, name):
    def body(chip_ref, sb_ref, l0, l1, l2, o_ref):
        o_ref[...] = _sum4(sb_ref, l0, l1, l2)

    return _reduced_rows(sb, lb, off, rows, [], 1, body, name, chip)[0]


def _adamw_math(w, g, m, v):
    nm = ADAM_B1 * m + (1.0 - ADAM_B1) * g
    nv = ADAM_B2 * v + (1.0 - ADAM_B2) * (g * g)
    c1 = 1.0 - ADAM_B1 ** ADAM_STEP
    c2 = 1.0 - ADAM_B2 ** ADAM_STEP
    return -ADAM_LR * ((nm / c1) / (jnp.sqrt(nv / c2) + ADAM_EPS) + ADAM_WD * w), nm, nv


def _sum_adamw(sb, lb, chip, off, rows, w, m, v, name):
    def body(chip_ref, sb_ref, l0, l1, l2, w_ref, m_ref, v_ref, g_ref, d_ref, nm_ref, nv_ref):
        g = _sum4(sb_ref, l0, l1, l2)
        g_ref[...] = g
        d_ref[...], nm_ref[...], nv_ref[...] = _adamw_math(w_ref[...], g, m_ref[...], v_ref[...])

    return _reduced_rows(sb, lb, off, rows, [w, m, v], 4, body, name, chip)


def _adamw(w, g, m, v, name):
    R, Ccols = w.shape
    br = _blk(R, 256, SUBLANES)

    def body(w_ref, g_ref, m_ref, v_ref, d_ref, nm_ref, nv_ref):
        d_ref[...], nm_ref[...], nv_ref[...] = _adamw_math(w_ref[...], g_ref[...], m_ref[...], v_ref[...])

    spec = pl.BlockSpec((br, Ccols), lambda i: (i, 0))
    return pl.pallas_call(
        body, name=name, grid=(R // br,), in_specs=[spec] * 4, out_specs=[spec] * 3,
        out_shape=[SDS((R, Ccols), F32)] * 3, compiler_params=_cp(("arbitrary",), 40),
    )(w, g, m, v)


def _place():
    return lax.axis_index("x"), lax.axis_index("y"), lax.axis_index("c")


def _dev_rows(ref, dev, rows):
    return ref.at[pl.ds((4 * dev[0] + 2 * dev[1] + dev[2]) * rows, rows), :]


def _remote(src, dst, send_sem, recv_sem, to):
    return pltpu.make_async_remote_copy(src_ref=src, dst_ref=dst, send_sem=send_sem, recv_sem=recv_sem,
                                        device_id=to, device_id_type=MESH)


SAME_CORE_AND_SIBLING = ((0, 0, 1), (1, 0, 0), (0, 1, 0), (1, 1, 0))


def _merge_phases(a, b):
    na_in, na_out, na_sem = len(a.inputs), len(a.out_shapes), len(a.sem_shapes)

    def build(ins, outs, sems, stage):
        return (a.build(ins[:na_in], outs[:na_out], sems[:na_sem], stage)
                + b.build(ins[na_in:], outs[na_out:], sems[na_sem:], stage))

    aliases = dict(a.aliases)
    aliases.update({na_in + i: na_out + o for i, o in b.aliases.items()})
    return _Carried(a.inputs + b.inputs, a.out_shapes + b.out_shapes, aliases, a.sem_shapes + b.sem_shapes, build,
                    has_mid=a.has_mid or b.has_mid)


def _ag_direct_phase(slab, pieces, flips):
    W = slab.shape[1]
    n = len(pieces)
    npeer = len(flips)

    def build(ins, outs, sems, stage):
        if stage == "mid":
            return []
        starting = stage == "start"
        (slab_ref,) = ins
        send_sems, recv_sems, local_sems = sems
        x, y, c = _place()
        me = (x, y, c)
        peers = [tuple(1 - v if f else v for v, f in zip(me, flip)) for flip in flips]
        todo = []
        for p, (off, rows) in enumerate(pieces):
            src = slab_ref.at[pl.ds(off, rows), :]
            mine = pltpu.make_async_copy(src, _dev_rows(outs[p], me, rows), local_sems.at[p])
            todo.append(mine.start if starting else mine.wait)
            for k, peer in enumerate(peers):
                snd = _remote(src, _dev_rows(outs[p], me, rows), send_sems.at[k, p], recv_sems.at[k, p], peer)
                if starting:
                    todo.append(snd.start)
                else:
                    theirs = _dev_rows(outs[p], peer, rows)
                    rcv = _remote(theirs, theirs, send_sems.at[k, p], recv_sems.at[k, p], me)
                    todo += [rcv.wait_recv, snd.wait_send]
        return todo

    dma = pltpu.SemaphoreType.DMA
    return _Carried([slab], [SDS((N_DEV * rows, W), slab.dtype) for _, rows in pieces], {},
                    [dma((npeer, n)), dma((npeer, n)), dma((n,))], build)


def _ag_two_level_phase(slab, pieces):
    W = slab.shape[1]
    n = len(pieces)

    def build(ins, outs, sems, stage):
        (slab_ref,) = ins
        send_sems, recv_sems, local_sems = sems
        x, y, c = _place()
        me, sibling = (x, y, c), (x, y, 1 - c)
        chips = [(1 - x, y), (x, 1 - y), (1 - x, 1 - y)]
        todo = []
        for p, (off, rows) in enumerate(pieces):
            src = slab_ref.at[pl.ds(off, rows), :]
            own = _dev_rows(outs[p], me, rows)
            landed = [_dev_rows(outs[p], (*chip, c), rows) for chip in chips]

            def mine():
                return pltpu.make_async_copy(src, own, local_sems.at[p])

            def first():
                return [_remote(src, own, send_sems.at[k, p], recv_sems.at[k, p], to)
                        for k, to in enumerate([sibling] + [(*chip, c) for chip in chips])]

            def passed():
                return [_remote(blk, blk, send_sems.at[4 + j, p], recv_sems.at[4 + j, p], sibling)
                        for j, blk in enumerate(landed)]

            def arrival(k, blk):
                return _remote(blk, blk, send_sems.at[k, p], recv_sems.at[k, p], me).wait_recv

            if stage == "start":
                todo += [mine().start] + [cp.start for cp in first()]
            elif stage == "mid":
                for j, (blk, fwd) in enumerate(zip(landed, passed())):
                    todo += [arrival(1 + j, blk), fwd.start]
            else:
                theirs = [_dev_rows(outs[p], sibling, rows)] + [_dev_rows(outs[p], (*chip, 1 - c), rows) for chip in chips]
                todo += [arrival(k, blk) for k, blk in zip((0, 4, 5, 6), theirs)]
                todo += [cp.wait_send for cp in first() + passed()] + [mine().wait]
        return todo

    dma = pltpu.SemaphoreType.DMA
    return _Carried([slab], [SDS((N_DEV * rows, W), slab.dtype) for _, rows in pieces], {},
                    [dma((7, n)), dma((7, n)), dma((n,))], build, has_mid=True)


def _ag_forward_phase(gathered, pieces):
    n = len(pieces)

    def build(ins, outs, sems, stage):
        if stage == "mid":
            return []
        starting = stage == "start"
        send_sems, recv_sems = sems
        x, y, c = _place()
        me, sibling = (x, y, c), (x, y, 1 - c)
        chips = [(1 - x, y), (x, 1 - y), (1 - x, 1 - y)]
        todo = []
        for p, (_, rows) in enumerate(pieces):
            for j, chip in enumerate(chips):
                snd = _remote(_dev_rows(ins[p], (*chip, c), rows), _dev_rows(outs[p], (*chip, c), rows),
                              send_sems.at[j, p], recv_sems.at[j, p], sibling)
                if starting:
                    todo.append(snd.start)
                else:
                    theirs = _dev_rows(outs[p], (*chip, 1 - c), rows)
                    rcv = _remote(theirs, theirs, send_sems.at[j, p], recv_sems.at[j, p], me)
                    todo += [rcv.wait_recv, snd.wait_send]
        return todo

    dma = pltpu.SemaphoreType.DMA
    return _Carried(gathered, [SDS(g.shape, g.dtype) for g in gathered], {p: p for p in range(n)},
                    [dma((3, n)), dma((3, n))], build)


def _rs_chips_phase(sb):
    _, R, W = sb.shape

    def build(ins, outs, sems, stage):
        if stage == "mid":
            return []
        (sb_ref,), (land_ref,) = ins, outs
        send_sems, recv_sems = sems
        x, y, c = _place()
        chips = [(1 - x, y), (x, 1 - y), (1 - x, 1 - y)]
        cps = [_remote(sb_ref.at[2 * chip[0] + chip[1]], land_ref.at[j], send_sems.at[j], recv_sems.at[j], (*chip, c))
               for j, chip in enumerate(chips)]
        if stage == "start":
            return [cp.start for cp in cps]
        return [cp.wait_recv for cp in cps] + [cp.wait_send for cp in cps]

    dma = pltpu.SemaphoreType.DMA
    return _Carried([sb], [SDS((3, R, W), sb.dtype)], {}, [dma((3,)), dma((3,))], build)


def _allgather(slab, pieces, name):
    R, W = slab.shape
    n = len(pieces)
    assert sum(rows for _, rows in pieces) == R

    def body(slab_ref, *refs):
        outs = refs[:n]
        send_sems, recv_sems, local_sems = refs[n:]
        x, y, c = _place()
        me, sibling = (x, y, c), (x, y, 1 - c)
        chips = [(1 - x, y), (x, 1 - y), (1 - x, 1 - y)]

        def dst_rows(p, origin):
            rows = pieces[p][1]
            start = (4 * origin[0] + 2 * origin[1] + origin[2]) * rows
            return outs[p].at[pl.ds(start, rows), :]

        def copies(k, origin, to, from_slab):
            out = []
            for p, (off, rows) in enumerate(pieces):
                dst = dst_rows(p, origin)
                src = slab_ref.at[pl.ds(off, rows), :] if from_slab else dst
                out.append(pltpu.make_async_remote_copy(
                    src_ref=src, dst_ref=dst, send_sem=send_sems.at[k, p], recv_sem=recv_sems.at[k, p],
                    device_id=to, device_id_type=MESH))
            return out

        mine = [pltpu.make_async_copy(slab_ref.at[pl.ds(off, rows), :], dst_rows(p, me), local_sems.at[p])
                for p, (off, rows) in enumerate(pieces)]
        for cp in mine:
            cp.start()
        first = copies(0, me, sibling, True)
        for j, chip in enumerate(chips):
            first += copies(1 + j, me, (*chip, c), True)
        for cp in first:
            cp.start()
        passed = []
        for j, chip in enumerate(chips):
            for cp in copies(1 + j, (*chip, c), me, False):
                cp.wait_recv()
            fwd = copies(4 + j, (*chip, c), sibling, False)
            for cp in fwd:
                cp.start()
            passed += fwd
        for cp in copies(0, sibling, me, False):
            cp.wait_recv()
        for j, chip in enumerate(chips):
            for cp in copies(4 + j, (*chip, 1 - c), me, False):
                cp.wait_recv()
        for cp in first + passed:
            cp.wait_send()
        for cp in mine:
            cp.wait()

    return pl.pallas_call(
        body, name=name,
        in_specs=[HBM_SPEC], out_specs=[HBM_SPEC] * n,
        out_shape=[SDS((N_DEV * rows, W), slab.dtype) for _, rows in pieces],
        scratch_shapes=[pltpu.SemaphoreType.DMA((7, n)), pltpu.SemaphoreType.DMA((7, n)), pltpu.SemaphoreType.DMA((n,))],
    )(slab)


def _rs_sibling(grads, pieces, name):
    W = grads[0].shape[1]
    R = sum(rows for _, rows in pieces)
    n = len(pieces)
    dt = grads[0].dtype
    max_rows = max(rows for _, rows in pieces)
    steps = [(q, p) for q in range(N_CHIP) for p in range(n)]
    ns = len(steps)
    ADD_ROWS = 64
    SLOTS = 3
    assert all(rows % ADD_ROWS == 0 for _, rows in pieces)

    def body(*refs):
        g_refs = refs[:n]
        sb_ref, mine_buf, send_buf, land_buf, out_buf, in_sems, out_sems, send_sems, recv_sems, credit = refs[n:]
        x, y, c = _place()
        sibling = (x, y, 1 - c)

        def loads(s):
            q, p = steps[s]
            rows = pieces[p][1]
            slot = s % SLOTS
            mine = g_refs[p].at[pl.ds((2 * q + c) * rows, rows), :]
            theirs = g_refs[p].at[pl.ds((2 * q + 1 - c) * rows, rows), :]
            return (pltpu.make_async_copy(mine, mine_buf.at[slot, pl.ds(0, rows), :], in_sems.at[slot, 0]),
                    pltpu.make_async_copy(theirs, send_buf.at[slot, pl.ds(0, rows), :], in_sems.at[slot, 1]))

        def send(s):
            rows = pieces[steps[s][1]][1]
            slot = s % SLOTS
            return pltpu.make_async_remote_copy(
                src_ref=send_buf.at[slot, pl.ds(0, rows), :], dst_ref=land_buf.at[slot, pl.ds(0, rows), :],
                send_sem=send_sems.at[slot], recv_sem=recv_sems.at[slot], device_id=sibling, device_id_type=MESH)

        def store(s):
            q, p = steps[s]
            off, rows = pieces[p]
            slot = s % SLOTS
            return pltpu.make_async_copy(out_buf.at[slot, pl.ds(0, rows), :], sb_ref.at[q, pl.ds(off, rows), :],
                                         out_sems.at[slot])

        def start_send(s):
            for cp in loads(s):
                cp.wait()
            if s >= SLOTS:
                pl.semaphore_wait(credit.at[s % SLOTS], 1)
            send(s).start()

        for s in range(min(SLOTS, ns)):
            for cp in loads(s):
                cp.start()
        for s in range(min(SLOTS - 1, ns)):
            start_send(s)
        for s in range(ns):
            slot = s % SLOTS
            rows = pieces[steps[s][1]][1]
            if s + SLOTS - 1 < ns:
                start_send(s + SLOTS - 1)
            send(s).wait_recv()
            if s >= SLOTS:
                store(s - SLOTS).wait()

            def add(k, _, slot=slot):
                r = pl.ds(pl.multiple_of(k * ADD_ROWS, ADD_ROWS), ADD_ROWS)
                out_buf[slot, r, :] = (mine_buf[slot, r, :].astype(F32) + land_buf[slot, r, :].astype(F32)).astype(dt)
                return 0
            lax.fori_loop(0, rows // ADD_ROWS, add, 0)
            if s + SLOTS < ns:
                pl.semaphore_signal(credit.at[slot], inc=1, device_id=sibling, device_id_type=MESH)
            store(s).start()
            send(s).wait_send()
            if s + SLOTS < ns:
                for cp in loads(s + SLOTS):
                    cp.start()
        for s in range(max(ns - SLOTS, 0), ns):
            store(s).wait()

    buf = pltpu.VMEM((SLOTS, max_rows, W), dt)
    return pl.pallas_call(
        body, name=name,
        in_specs=[HBM_SPEC] * n, out_specs=HBM_SPEC,
        out_shape=SDS((N_CHIP, R, W), dt),
        scratch_shapes=[buf, buf, buf, buf, pltpu.SemaphoreType.DMA((SLOTS, 2)), pltpu.SemaphoreType.DMA((SLOTS,)),
                        pltpu.SemaphoreType.DMA((SLOTS,)), pltpu.SemaphoreType.DMA((SLOTS,)),
                        pltpu.SemaphoreType.REGULAR((SLOTS,))],
        compiler_params=pltpu.CompilerParams(vmem_limit_bytes=48 * MIB),
    )(*grads)


SMALL_NAMES = ("ffn1_norm", "mix_norm", "ffn2_norm", "final_norm", "lru_conv_w", "lru_conv_b", "lru_w_a", "lru_b_a",
               "lru_w_i", "lru_b_i", "lru_lambda", "sc_conv_w", "lru_out_norm", "sc_out_norm")
WEIGHT_NAMES = ("ffn1_norm", "ffn1_w_gate", "ffn1_w_up", "ffn1_w_down", "mix_norm", "w_in", "lru_conv_w", "lru_conv_b",
                "lru_w_a", "lru_b_a", "lru_w_i", "lru_b_i", "lru_lambda", "sc_conv_w", "lru_out_norm", "sc_out_norm",
                "w_out", "ffn2_norm", "ffn2_w_gate", "ffn2_w_up", "ffn2_w_down", "final_norm")
BIG = (("ffn1_w_gate", True), ("ffn1_w_up", True), ("ffn1_w_down", False), ("ffn2_w_gate", True), ("ffn2_w_up", True),
       ("ffn2_w_down", False), ("w_in", True), ("w_out", False))


SLAB_ROW_ALIGN = 256


def _pack_rows(parts, width):
    rows, counts = [], []
    for p in parts:
        flat = p.reshape(-1)
        nr = -(-flat.shape[0] // width)
        nr = -(-nr // SUBLANES) * SUBLANES
        rows.append(jnp.pad(flat, (0, nr * width - flat.shape[0])).reshape(nr, width))
        counts.append(nr)
    total = sum(counts)
    pad = -(-total // SLAB_ROW_ALIGN) * SLAB_ROW_ALIGN - total
    if pad:
        rows.append(jnp.zeros((pad, width), rows[0].dtype))
    return jnp.concatenate(rows, axis=0), counts


def _stack_rows(blocks):
    pieces, off = [], 0
    for b in blocks:
        pieces.append((off, b.shape[0]))
        off += b.shape[0]
    return jnp.concatenate(blocks, axis=0), pieces


def _unpack_rows(slab, counts, shapes):
    out, r = [], 0
    for nr, shape in zip(counts, shapes):
        size = math.prod(shape)
        out.append(slab[r:r + nr].reshape(-1)[:size].reshape(shape))
        r += nr
    return out


def kernel(x, ffn1_norm, ffn1_w_gate, ffn1_w_up, ffn1_w_down, mix_norm, w_in, lru_conv_w, lru_conv_b, lru_w_a, lru_b_a, lru_w_i, lru_b_i, lru_lambda, sc_conv_w, lru_out_norm, sc_out_norm, w_out, ffn2_norm, ffn2_w_gate, ffn2_w_up, ffn2_w_down, final_norm, loss_target, m_ffn1_norm, m_ffn1_w_gate, m_ffn1_w_up, m_ffn1_w_down, m_mix_norm, m_w_in, m_lru_conv_w, m_lru_conv_b, m_lru_w_a, m_lru_b_a, m_lru_w_i, m_lru_b_i, m_lru_lambda, m_sc_conv_w, m_lru_out_norm, m_sc_out_norm, m_w_out, m_ffn2_norm, m_ffn2_w_gate, m_ffn2_w_up, m_ffn2_w_down, m_final_norm, v_ffn1_norm, v_ffn1_w_gate, v_ffn1_w_up, v_ffn1_w_down, v_mix_norm, v_w_in, v_lru_conv_w, v_lru_conv_b, v_lru_w_a, v_lru_b_a, v_lru_w_i, v_lru_b_i, v_lru_lambda, v_sc_conv_w, v_lru_out_norm, v_sc_out_norm, v_w_out, v_ffn2_norm, v_ffn2_w_gate, v_ffn2_w_up, v_ffn2_w_down, v_final_norm):
    a = dict(locals())
    w = {n: a[n] for n in WEIGHT_NAMES}
    m = {n: a["m_" + n] for n in WEIGHT_NAMES}
    v = {n: a["v_" + n] for n in WEIGHT_NAMES}
    ax, ay, ac = _place()
    dev = 4 * ax + 2 * ay + ac
    chip = (2 * ax + ay).astype(jnp.int32).reshape(1)

    x0 = x[0]
    tgt = loss_target[0]
    T, D = x0.shape
    C = D // 2
    H, hd = lru_w_a.shape[1], lru_w_a.shape[2]
    CL = lru_conv_w.shape[2]

    shards = []
    for name, transposed in BIG:
        s = w[name][0]
        shards.append((s.T if transposed else s).astype(BF16))
    taps = jnp.concatenate([lru_conv_w[0], sc_conv_w[0], jnp.zeros((1, CL), F32)], axis=0)
    taps_row = lax.bitcast_convert_type(taps, BF16).reshape(1, -1)
    taps_blk = jnp.pad(taps_row, ((0, BF16_ROWS - 1), (0, D - taps_row.shape[1])))
    s_wg1, s_wu1, s_wd1, s_wg2, s_wu2, s_wd2, s_win, s_wout = shards
    slab_g1, pcs_g1 = _stack_rows([s_wg1])
    slab_u1, pcs_u1 = _stack_rows([s_wu1])
    slab_d1, pcs_d1 = _stack_rows([s_wd1])
    slab_mw, pcs_mw = _stack_rows([s_win, s_wout, taps_blk])
    slab_g2, pcs_g2 = _stack_rows([s_wg2])
    slab_ud2, pcs_ud2 = _stack_rows([s_wu2, s_wd2])
    (wg1,) = _allgather(slab_g1, pcs_g1, "allgather_ffn1_gate")

    g1, gm, g3 = ffn1_norm, mix_norm, ffn2_norm
    phase = _merge_phases(_ag_two_level_phase(slab_u1, pcs_u1), _ag_direct_phase(slab_d1, pcs_d1, SAME_CORE_AND_SIBLING))
    (n1, hg1), got = _norm_proj(x0, g1, [wg1], [BF16], False, "ffn1_gate", carried=phase)
    wu1, d1 = got[0], got[1:]
    phase = _merge_phases(_ag_forward_phase(d1, pcs_d1), _ag_direct_phase(slab_mw, pcs_mw, SAME_CORE_AND_SIBLING))
    (hu1, act1), got = _up_act(n1, wu1, hg1, "ffn1_up", carried=phase)
    wd1, mixw = got[0], got[1:]
    phase = _merge_phases(_ag_forward_phase(mixw, pcs_mw), _ag_direct_phase(slab_g2, pcs_g2, SAME_CORE_AND_SIBLING))
    x1, got = _mm_res(act1, wd1, x0, 0.5, "ffn1_down", carried=phase)
    (win, wout, taps_all), g2 = got[:3], got[3:]
    phase = _merge_phases(_ag_forward_phase(g2, pcs_g2), _ag_direct_phase(slab_ud2, pcs_ud2, SAME_CORE_AND_SIBLING))
    (n2, z), got = _norm_proj(x1, gm, [win], [F32], False, "in_proj", carried=phase)
    (wg2,), ud2 = got[:1], got[1:]
    taps_all = taps_all.reshape(N_DEV, BF16_ROWS, D)[:, 0, :2 * SUBLANES * CL].reshape(N_DEV, SUBLANES, CL, 2)
    taps_all = lax.bitcast_convert_type(taps_all, F32)
    taps_all = taps_all.transpose(1, 0, 2).reshape(SUBLANES, N_DEV * CL)
    cw, sw = taps_all[0:4], taps_all[4:7]

    gf = final_norm.reshape(1, D)
    cb = lru_conv_b
    wa, wi = lru_w_a[0].astype(BF16), lru_w_i[0].astype(BF16)
    ba, bi = lru_b_a.reshape(1, C), lru_b_i.reshape(1, C)
    lam, glo, gso = lru_lambda, lru_out_norm, sc_out_norm

    y, h, *saved = _mix_fwd(z, cw, cb, wa, ba, wi, bi, lam, sw, glo, gso, "mix_fwd")
    x2, (wu2, wd2) = _mm_res(y, wout, x1, 1.0, "out_proj", carried=_ag_forward_phase(ud2, pcs_ud2))
    n3, hg2, hu2, act2 = _norm_proj(x2, g3, [wg2, wu2], [BF16, BF16], True, "ffn2_up")
    x3 = _mm_res(act2, wd2, x2, 0.5, "ffn2_down")
    dx3, df2, d_gf, loss_blk = _loss_head(x3, gf, tgt, "loss_head")

    F = wd1.shape[0]
    bm_f = F // 4 if (F // 4) % LANES == 0 else 512

    def reduce_group(gs, tag):
        pcs, off = [], 0
        for g_ in gs:
            pcs.append((off, g_.shape[0] // N_DEV))
            off += g_.shape[0] // N_DEV
        sb_ = _rs_sibling(gs, pcs, "rs_sibling_add_" + tag)
        return sb_, pcs

    dhg2, dhu2 = _ffn_bwd_act(df2, wd2, hg2, hu2, "ffn2_bwd_act")
    d_wd2 = _dw_tn(act2, df2, bm_f, "ffn2_dw_down")
    d_wg2 = _dw_tn(dhg2, n3, bm_f, "ffn2_dw_gate")
    d_wu2 = _dw_tn(dhu2, n3, bm_f, "ffn2_dw_up")
    sb_f2, pcs_f2 = reduce_group([d_wg2, d_wu2, d_wd2], "ffn2")
    (dx2, dx2b, d_g3), (lb_f2,) = _mm_rmsbwd([(dhg2, wg2), (dhu2, wu2)], x2, g3, dx3, 1.0, "ffn2_bwd_in",
                                             carried=_rs_chips_phase(sb_f2))
    dy = _mm_nt(dx2b, wout, "out_proj_bwd")
    d_wout = _dw_tn(y, dx2b, 1024, "out_proj_dw")
    dz, small, d_wa, d_wi = _mix_bwd(z, h, saved, dy, cw, wa, wi, lam, sw, glo, gso, "mix_bwd")
    d_win = _dw_tn(dz, n2, 1280, "in_proj_dw")
    sb_mx, pcs_mx = reduce_group([d_win, d_wout], "mix")
    (dx1, df1, d_gm), (lb_mx,) = _mm_rmsbwd([(dz, win)], x1, gm, dx2, 0.5, "in_proj_bwd",
                                            carried=_rs_chips_phase(sb_mx))
    dhg1, dhu1 = _ffn_bwd_act(df1, wd1, hg1, hu1, "ffn1_bwd_act")
    d_wd1 = _dw_tn(act1, df1, bm_f, "ffn1_dw_down")
    d_wg1 = _dw_tn(dhg1, n1, bm_f, "ffn1_dw_gate")
    d_wu1 = _dw_tn(dhu1, n1, bm_f, "ffn1_dw_up")
    sb_f1, pcs_f1 = reduce_group([d_wg1, d_wu1, d_wd1], "ffn1")
    early_names = [n_ for n_ in SMALL_NAMES if n_ != "ffn1_norm"]
    early_parts = dict(zip(SMALL_NAMES, [None, d_gm, d_g3, d_gf, small[R_CW:R_CW + 4], small[R_CB], d_wa, small[R_BA], d_wi,
                                         small[R_BI], small[R_LAM], small[R_SW:R_SW + 3], small[R_GLO], small[R_GSO]]))
    early_slab, early_counts = _pack_rows([early_parts[n_] for n_ in early_names], LANES)
    RE = early_slab.shape[0]
    phase = _merge_phases(_ag_two_level_phase(early_slab, [(0, RE)]), _rs_chips_phase(sb_f1))
    (dx0, _, d_g1), (early_all, lb_f1) = _mm_rmsbwd([(dhg1, wg1), (dhu1, wu1)], x0, g1, dx1, 1.0, "ffn1_bwd_in",
                                                    carried=phase)

    grads, delta, new_m, new_v = {}, {}, {}, {}
    transposed_shard = dict(BIG)
    for names, sb_, lb_, pcs in ((("ffn2_w_gate", "ffn2_w_up", "ffn2_w_down"), sb_f2, lb_f2, pcs_f2),
                                 (("w_in", "w_out"), sb_mx, lb_mx, pcs_mx),
                                 (("ffn1_w_gate", "ffn1_w_up", "ffn1_w_down"), sb_f1, lb_f1, pcs_f1)):
        for name, (off, rows) in zip(names, pcs):
            flip = transposed_shard[name] and w[name].shape[2] % LANES != 0
            if transposed_shard[name] and not flip:
                g_ = _final_grad(sb_, lb_, chip, off, rows, "rs_final_sum_" + name).T
                d_, m_, v_ = _adamw(w[name][0], g_, m[name][0], v[name][0], "adamw_" + name)
            else:
                view = (lambda t: t[0].T) if flip else (lambda t: t[0])
                g_, d_, m_, v_ = _sum_adamw(sb_, lb_, chip, off, rows, view(w[name]), view(m[name]), view(v[name]),
                                            "sum_adamw_" + name)
            back = (lambda t: t.T[None]) if flip else (lambda t: t[None])
            grads[name], delta[name], new_m[name], new_v[name] = back(g_), back(d_), back(m_), back(v_)

    shape_of = dict(zip(SMALL_NAMES, [(1, D), (1, D), (1, D), (D,), (1, 4, C), (1, C), (1, H, hd, hd), (1, H, hd),
                                      (1, H, hd, hd), (1, H, hd), (1, C), (1, 3, C), (1, C), (1, C)]))
    early_sum = _add_slabs([early_all[j * RE:(j + 1) * RE] for j in range(N_DEV)], F32, "small_grads_sum")
    small_full = dict(zip(early_names, _unpack_rows(early_sum, early_counts, [shape_of[n_] for n_ in early_names])))
    late_slab, late_counts = _pack_rows([d_g1], LANES)
    RL = late_slab.shape[0]
    (late_all,) = _allgather(late_slab, [(0, RL)], "allgather_ffn1_norm_grad")
    late_sum = _add_slabs([late_all[j * RL:(j + 1) * RL] for j in range(N_DEV)], F32, "ffn1_norm_grad_sum")
    (small_full["ffn1_norm"],) = _unpack_rows(late_sum, late_counts, [shape_of["ffn1_norm"]])

    for name in SMALL_NAMES:
        gfull = small_full[name]
        if name in ("lru_conv_w", "sc_conv_w"):
            gfull = lax.dynamic_slice_in_dim(gfull, dev * CL, CL, axis=2)
        grads[name] = gfull

    packs = [_pack_rows([t[n_] for n_ in SMALL_NAMES], LANES) for t in (w, grads, m, v)]
    sd, sm, sv = _adamw(packs[0][0], packs[1][0], packs[2][0], packs[3][0], "adamw_small")
    shapes = [w[n_].shape for n_ in SMALL_NAMES]
    for tgt_dict, slab_ in ((delta, sd), (new_m, sm), (new_v, sv)):
        for n_, val in zip(SMALL_NAMES, _unpack_rows(slab_, packs[0][1], shapes)):
            tgt_dict[n_] = val

    loss = lax.psum(loss_blk[0, 0], ("x", "y", "c"))
    return (loss, dx0[None], *[grads[n_] for n_ in WEIGHT_NAMES], *[delta[n_] for n_ in WEIGHT_NAMES],
            *[new_m[n_] for n_ in WEIGHT_NAMES], *[new_v[n_] for n_ in WEIGHT_NAMES])
```

```python
import functools
import math

import jax
import jax.numpy as jnp
from jax import lax
from jax.experimental import pallas as pl
from jax.experimental.pallas import tpu as pltpu

F32 = jnp.float32
BF16 = jnp.bfloat16
SDS = jax.ShapeDtypeStruct
MESH = pl.DeviceIdType.MESH

NORM_EPS = 1e-6
LRU_C = 8.0
N_DEV = 8
N_CHIP = 4
ADAM_LR, ADAM_B1, ADAM_B2, ADAM_EPS, ADAM_WD, ADAM_STEP = 0.001, 0.9, 0.999, 1e-08, 0.01, 10

NN = (((1,), (0,)), ((), ()))
NT = (((1,), (1,)), ((), ()))
TN = (((0,), (0,)), ((), ()))

SUBLANES = 8
BF16_ROWS = 16
LANES = 128
MIB = 1 << 20


def _dot(a, b, dims):
    return lax.dot_general(a, b, dims, preferred_element_type=F32)


def _blk(n, pref, align):
    if n <= pref:
        return n
    b = (pref // align) * align
    while b >= align:
        if n % b == 0:
            return b
        b -= align
    raise ValueError(f"no block of {n} aligned to {align} under {pref}")


def _cp(sem, vmem_mib):
    return pltpu.CompilerParams(dimension_semantics=sem, vmem_limit_bytes=vmem_mib * MIB)


HBM_SPEC = pl.BlockSpec(memory_space=pltpu.HBM)
MID_EIGHTHS = 5

class _Carried:
    def __init__(self, inputs, out_shapes, aliases, sem_shapes, build, has_mid=False):
        self.inputs, self.out_shapes, self.aliases = list(inputs), list(out_shapes), dict(aliases)
        self.sem_shapes, self.build, self.has_mid = list(sem_shapes), build, has_mid


def _call(body, *, name, grid, in_specs, out_specs, out_shape, scratch_shapes, compiler_params, args, carried=None):
    if carried is None:
        return pl.pallas_call(body, name=name, grid=grid, in_specs=in_specs, out_specs=out_specs, out_shape=out_shape,
                              scratch_shapes=scratch_shapes, compiler_params=compiler_params)(*args)
    n_in, n_out, n_sc = len(in_specs), len(out_shape), len(scratch_shapes)
    c_in, c_out = len(carried.inputs), len(carried.out_shapes)

    def hosted(*refs):
        ins, refs = refs[:n_in], refs[n_in:]
        c_ins, refs = refs[:c_in], refs[c_in:]
        outs, refs = refs[:n_out], refs[n_out:]
        c_outs, refs = refs[:c_out], refs[c_out:]
        scratch, c_sems = refs[:n_sc], refs[n_sc:]
        first = functools.reduce(jnp.logical_and, [pl.program_id(a) == 0 for a in range(len(grid))])
        last = functools.reduce(jnp.logical_and, [pl.program_id(a) == g - 1 for a, g in enumerate(grid)])

        @pl.when(first)
        def _():
            for start in carried.build(c_ins, c_outs, c_sems, "start"):
                start()

        if carried.has_mid:
            mid = functools.reduce(jnp.logical_and, [pl.program_id(0) == (grid[0] * MID_EIGHTHS) // 8]
                                   + [pl.program_id(a) == 0 for a in range(1, len(grid))])

            @pl.when(mid)
            def _():
                for step in carried.build(c_ins, c_outs, c_sems, "mid"):
                    step()

        body(*ins, *outs, *scratch)

        @pl.when(last)
        def _():
            for wait in carried.build(c_ins, c_outs, c_sems, "end"):
                wait()

    out = pl.pallas_call(
        hosted, name=name, grid=grid, in_specs=list(in_specs) + [HBM_SPEC] * c_in,
        out_specs=list(out_specs) + [HBM_SPEC] * c_out, out_shape=list(out_shape) + carried.out_shapes,
        scratch_shapes=list(scratch_shapes) + carried.sem_shapes,
        input_output_aliases={n_in + a: n_out + b for a, b in carried.aliases.items()},
        compiler_params=compiler_params)(*args, *carried.inputs)
    return out[:n_out], out[n_out:]


ROW_CHUNK = 128


def _chunk_rows(c):
    return pl.ds(pl.multiple_of(c * ROW_CHUNK, ROW_CHUNK), ROW_CHUNK)


def _rstd(xv):
    return lax.rsqrt(jnp.mean(xv * xv, axis=-1, keepdims=True) + NORM_EPS)


def _rms_bwd(xv, g, dn):
    r = _rstd(xv)
    xr = xv * r
    gd = g * dn
    dx = r * (gd - xr * jnp.mean(gd * xr, axis=-1, keepdims=True))
    return dx, jnp.sum(dn * xr, axis=0, keepdims=True)


def _log1p(e):
    u = 1.0 + e
    return jnp.where(u == 1.0, e, jnp.log(u) * (e / (u - 1.0)))


def _one_minus_exp(v, exp_half_v):
    series = 1.0 / 5040.0
    for coeff in (1.0 / 720.0, 1.0 / 120.0, 1.0 / 24.0, 1.0 / 6.0, 0.5, 1.0):
        series = series * v + coeff
    return jnp.where(v > -0.5, -v * series, 1.0 - exp_half_v * exp_half_v)


def _sigmoid(v):
    return 0.5 * jnp.tanh(0.5 * v) + 0.5


def _gelu_parts(g):
    k0 = math.sqrt(2.0 / math.pi)
    g2 = g * g
    t = jnp.tanh(k0 * (g + 0.044715 * g * g2))
    gel = 0.5 * g * (1.0 + t)
    gelp = 0.5 * (1.0 + t) + 0.5 * g * (1.0 - t * t) * (k0 * (1.0 + 3.0 * 0.044715 * g2))
    return gel, gelp


def _norm_proj(x, gain, w_list, out_dtypes, swiglu, name, carried=None):
    T, D = x.shape
    N = w_list[0].shape[0]
    nw = len(w_list)
    bm = _blk(T, 1024, BF16_ROWS)
    bn = _blk(N, 1024 // nw, LANES)

    def body(*refs):
        x_ref, g_ref = refs[:2]
        w_refs = refs[2:2 + nw]
        n_ref = refs[2 + nw]
        o_refs = refs[3 + nw:3 + 2 * nw]
        act_ref = refs[3 + 2 * nw] if swiglu else None
        n_sc = refs[-1]

        @pl.when(pl.program_id(1) == 0)
        def _():
            def chunk(c, _):
                r = _chunk_rows(c)
                xv = x_ref[r, :]
                nb = (xv * _rstd(xv) * g_ref[...]).astype(BF16)
                n_sc[r, :] = nb
                n_ref[r, :] = nb
                return 0
            lax.fori_loop(0, bm // ROW_CHUNK, chunk, 0)

        n = n_sc[...]
        outs = [_dot(n, w_ref[...], NT) for w_ref in w_refs]
        for o_ref, o in zip(o_refs, outs):
            o_ref[...] = o.astype(o_ref.dtype)
        if swiglu:
            hg, hu = outs
            act_ref[...] = (hg * _sigmoid(hg) * hu).astype(BF16)

    row = pl.BlockSpec((bm, D), lambda i, j: (i, 0))
    tile = pl.BlockSpec((bm, bn), lambda i, j: (i, j))
    n_extra = 1 if swiglu else 0
    return _call(
        body, name=name, grid=(T // bm, N // bn),
        in_specs=[row, pl.BlockSpec((1, D), lambda i, j: (0, 0))] + [pl.BlockSpec((bn, D), lambda i, j: (j, 0))] * nw,
        out_specs=[row] + [tile] * (nw + n_extra),
        out_shape=[SDS((T, D), BF16)] + [SDS((T, N), dt) for dt in out_dtypes] + [SDS((T, N), BF16)] * n_extra,
        scratch_shapes=[pltpu.VMEM((bm, D), BF16)],
        compiler_params=_cp(("arbitrary", "arbitrary"), 52),
        args=(x, gain, *w_list), carried=carried)


def _up_act(n, wu, hg, name, carried=None):
    T, D = n.shape
    F = wu.shape[0]
    bm = _blk(T, 1024, BF16_ROWS)
    bn = _blk(F, 512, LANES)

    def body(n_ref, wu_ref, hg_ref, hu_ref, act_ref):
        hu = _dot(n_ref[...], wu_ref[...], NT)
        hg = hg_ref[...].astype(F32)
        hu_ref[...] = hu.astype(BF16)
        act_ref[...] = (hg * _sigmoid(hg) * hu).astype(BF16)

    tile = pl.BlockSpec((bm, bn), lambda i, j: (i, j))
    return _call(
        body, name=name, grid=(T // bm, F // bn),
        in_specs=[pl.BlockSpec((bm, D), lambda i, j: (i, 0)), pl.BlockSpec((bn, D), lambda i, j: (j, 0)), tile],
        out_specs=[tile, tile], out_shape=[SDS((T, F), BF16)] * 2, scratch_shapes=[],
        compiler_params=_cp(("arbitrary", "arbitrary"), 40),
        args=(n, wu, hg), carried=carried)


def _mm_res(a, b, x, scale, name, carried=None):
    T, K = a.shape
    D = b.shape[1]
    bm = _blk(T, 1024, BF16_ROWS)
    bk = _blk(K, 1408, LANES)
    nk = K // bk

    def body(a_ref, b_ref, x_ref, o_ref):
        k = pl.program_id(1)

        @pl.when(k == 0)
        def _():
            o_ref[...] = jnp.zeros_like(o_ref)

        o_ref[...] += _dot(a_ref[...], b_ref[...], NN)

        @pl.when(k == nk - 1)
        def _():
            def chunk(c, _):
                r = _chunk_rows(c)
                o_ref[r, :] = x_ref[r, :] + scale * o_ref[r, :]
                return 0
            lax.fori_loop(0, bm // ROW_CHUNK, chunk, 0)

    row = pl.BlockSpec((bm, D), lambda i, k: (i, 0))
    out = _call(
        body, name=name, grid=(T // bm, nk),
        in_specs=[pl.BlockSpec((bm, bk), lambda i, k: (i, k)), pl.BlockSpec((bk, D), lambda i, k: (k, 0)), row],
        out_specs=[row], out_shape=[SDS((T, D), F32)], scratch_shapes=[],
        compiler_params=_cp(("arbitrary", "arbitrary"), 56),
        args=(a, b, x), carried=carried)
    return out[0] if carried is None else (out[0][0], out[1])


def _mm_nt(a, b, name):
    T, K = a.shape
    N = b.shape[0]
    bm = _blk(T, 1024, BF16_ROWS)
    bn = _blk(N, 512, LANES)

    def body(a_ref, b_ref, o_ref):
        o_ref[...] = _dot(a_ref[...], b_ref[...], NT)

    return pl.pallas_call(
        body, name=name, grid=(T // bm, N // bn),
        in_specs=[pl.BlockSpec((bm, K), lambda i, j: (i, 0)), pl.BlockSpec((bn, K), lambda i, j: (j, 0))],
        out_specs=pl.BlockSpec((bm, bn), lambda i, j: (i, j)), out_shape=SDS((T, N), F32),
        compiler_params=_cp(("arbitrary", "arbitrary"), 40),
    )(a, b)


def _ffn_bwd_act(dfb, wd, hg, hu, name):
    T, D = dfb.shape
    F = wd.shape[0]
    bm = _blk(T, 1024, BF16_ROWS)
    bn = _blk(F, 512, LANES)

    def body(df_ref, wd_ref, hg_ref, hu_ref, dhg_ref, dhu_ref):
        dact = _dot(df_ref[...], wd_ref[...], NT)
        hgv = hg_ref[...].astype(F32)
        huv = hu_ref[...].astype(F32)
        s = _sigmoid(hgv)
        dhu_ref[...] = (dact * (hgv * s)).astype(BF16)
        dhg_ref[...] = (dact * huv * (s * (1.0 + hgv * (1.0 - s)))).astype(BF16)

    tile = pl.BlockSpec((bm, bn), lambda i, j: (i, j))
    return pl.pallas_call(
        body, name=name, grid=(T // bm, F // bn),
        in_specs=[pl.BlockSpec((bm, D), lambda i, j: (i, 0)), pl.BlockSpec((bn, D), lambda i, j: (j, 0)), tile, tile],
        out_specs=[tile, tile], out_shape=[SDS((T, F), BF16)] * 2,
        compiler_params=_cp(("arbitrary", "arbitrary"), 40),
    )(dfb, wd, hg, hu)


def _dw_tn(a, b, bm_pref, name):
    T, M = a.shape
    N = b.shape[1]
    bm = _blk(M, bm_pref, LANES)
    tk = _blk(T, 1024, BF16_ROWS)
    nk = T // tk

    def body(a_ref, b_ref, o_ref, acc):
        k = pl.program_id(1)

        @pl.when(k == 0)
        def _():
            acc[...] = jnp.zeros_like(acc)

        acc[...] += _dot(a_ref[...], b_ref[...], TN)

        @pl.when(k == nk - 1)
        def _():
            o_ref[...] = acc[...].astype(BF16)

    return pl.pallas_call(
        body, name=name, grid=(M // bm, nk),
        in_specs=[pl.BlockSpec((tk, bm), lambda i, k: (k, i)), pl.BlockSpec((tk, N), lambda i, k: (k, 0))],
        out_specs=pl.BlockSpec((bm, N), lambda i, k: (i, 0)), out_shape=SDS((M, N), BF16),
        scratch_shapes=[pltpu.VMEM((bm, N), F32)],
        compiler_params=_cp(("arbitrary", "arbitrary"), 48),
    )(a, b)


def _mm_rmsbwd(pairs, x, gain, dx_in, bscale, name, carried=None):
    T, D = x.shape
    K = pairs[0][0].shape[1]
    npair = len(pairs)
    bm = _blk(T, 1024, BF16_ROWS)
    bk = _blk(K, 1024 // npair, LANES)
    nk = K // bk

    nchunk = bm // ROW_CHUNK

    def body(*refs):
        ab = refs[:2 * npair]
        x_hbm, g_ref, dxin_hbm, dx_ref, dxb_ref, dg_ref, x_buf, dxin_buf, sems = refs[2 * npair:]
        i = pl.program_id(0)
        k = pl.program_id(1)

        def fetch(c, slot):
            rows = pl.ds(i * bm + c * ROW_CHUNK, ROW_CHUNK)
            return (pltpu.make_async_copy(x_hbm.at[rows, :], x_buf.at[slot], sems.at[slot, 0]),
                    pltpu.make_async_copy(dxin_hbm.at[rows, :], dxin_buf.at[slot], sems.at[slot, 1]))

        @pl.when(k == 0)
        def _():
            dx_ref[...] = jnp.zeros_like(dx_ref)

        @pl.when(k == nk - 1)
        def _():
            for cp in fetch(0, 0):
                cp.start()

        for q in range(npair):
            dx_ref[...] += _dot(ab[2 * q][...], ab[2 * q + 1][...], NN)

        @pl.when(k == nk - 1)
        def _():
            @pl.when(i == 0)
            def _():
                dg_ref[...] = jnp.zeros_like(dg_ref)

            def chunk(c, _):
                slot = c % 2

                @pl.when(c + 1 < nchunk)
                def _():
                    for cp in fetch(c + 1, 1 - slot):
                        cp.start()

                for cp in fetch(c, slot):
                    cp.wait()

                r = _chunk_rows(c)
                dx, dg = _rms_bwd(x_buf[slot], g_ref[...], dx_ref[r, :])
                dxo = dxin_buf[slot] + dx
                dx_ref[r, :] = dxo
                dxb_ref[r, :] = (bscale * dxo).astype(BF16)
                dg_ref[...] += dg
                return 0
            lax.fori_loop(0, nchunk, chunk, 0)

    row = pl.BlockSpec((bm, D), lambda i, k: (i, 0))
    anywhere = pl.BlockSpec(memory_space=pl.ANY)
    vec = pl.BlockSpec((1, D), lambda i, k: (0, 0))
    in_specs = []
    args = []
    for a, b in pairs:
        in_specs += [pl.BlockSpec((bm, bk), lambda i, k: (i, k)), pl.BlockSpec((bk, D), lambda i, k: (k, 0))]
        args += [a, b]
    return _call(
        body, name=name, grid=(T // bm, nk),
        in_specs=in_specs + [anywhere, vec, anywhere], out_specs=[row, row, vec],
        out_shape=[SDS((T, D), F32), SDS((T, D), BF16), SDS((1, D), F32)],
        scratch_shapes=[pltpu.VMEM((2, ROW_CHUNK, D), F32), pltpu.VMEM((2, ROW_CHUNK, D), F32),
                        pltpu.SemaphoreType.DMA((2, 2))],
        compiler_params=_cp(("arbitrary", "arbitrary"), 52),
        args=(*args, x, gain, dx_in), carried=carried)


def _loss_head(x3, gain, tgt, name):
    T, D = x3.shape
    bm = _blk(T, 256, BF16_ROWS)

    def body(x_ref, g_ref, t_ref, dx_ref, dxb_ref, dg_ref, loss_ref):
        i = pl.program_id(0)
        xv = x_ref[...]
        g = g_ref[...]
        out = xv * _rstd(xv) * g
        e = out - t_ref[...]
        part = 0.5 * jnp.sum(jnp.mean(e * e, axis=-1, keepdims=True), axis=0, keepdims=True)
        dx, dg = _rms_bwd(xv, g, e * (1.0 / D))
        dx_ref[...] = dx
        dxb_ref[...] = (0.5 * dx).astype(BF16)

        @pl.when(i == 0)
        def _():
            dg_ref[...] = dg
            loss_ref[...] = jnp.broadcast_to(part, loss_ref.shape)

        @pl.when(i > 0)
        def _():
            dg_ref[...] += dg
            loss_ref[...] += jnp.broadcast_to(part, loss_ref.shape)

    row = pl.BlockSpec((bm, D), lambda i: (i, 0))
    vec = pl.BlockSpec((1, D), lambda i: (0, 0))
    return pl.pallas_call(
        body, name=name, grid=(T // bm,),
        in_specs=[row, vec, row], out_specs=[row, row, vec, pl.BlockSpec((SUBLANES, LANES), lambda i: (0, 0))],
        out_shape=[SDS((T, D), F32), SDS((T, D), BF16), SDS((1, D), F32), SDS((SUBLANES, LANES), F32)],
        compiler_params=_cp(("arbitrary",), 40),
    )(x3, gain, tgt)


R_CW, R_CB, R_BA, R_BI, R_LAM, R_SW, R_GLO, R_GSO, SMALL_ROWS = 0, 4, 5, 6, 7, 8, 11, 12, 16


def _rows(g):
    return pl.ds(pl.multiple_of(g * SUBLANES, SUBLANES), SUBLANES)


def _shift_back(prev, cur, d):
    row = lax.broadcasted_iota(jnp.int32, cur.shape, 0)
    return pltpu.roll(jnp.where(row >= SUBLANES - d, prev, cur), d, 0)


def _shift_fwd(cur, nxt, d):
    row = lax.broadcasted_iota(jnp.int32, cur.shape, 0)
    return pltpu.roll(jnp.where(row < d, nxt, cur), SUBLANES - d, 0)


def _causal_conv(ext, g, taps_ref, ntap):
    prev = ext[_rows(g), :]
    cur = ext[_rows(g + 1), :]
    out = _shift_back(prev, cur, ntap - 1) * taps_ref[0:1, :]
    for k in range(1, ntap - 1):
        out = out + _shift_back(prev, cur, ntap - 1 - k) * taps_ref[k:k + 1, :]
    return out + cur * taps_ref[ntap - 1:ntap, :]


def _scan8(A, U, reverse):
    row = lax.broadcasted_iota(jnp.int32, A.shape, 0)
    for s in (1, 2, 4):
        if reverse:
            A_sh = pltpu.roll(A, SUBLANES - s, 0)
            U_sh = pltpu.roll(U, SUBLANES - s, 0)
            m = row < SUBLANES - s
        else:
            A_sh = pltpu.roll(A, s, 0)
            U_sh = pltpu.roll(U, s, 0)
            m = row >= s
        U = jnp.where(m, A * U_sh + U, U)
        A = jnp.where(m, A * A_sh, A)
    return A, U


def _gate_pre(xc_s, w_ref, out_s, H, hd):
    for h in range(H):
        cs = slice(h * hd, (h + 1) * hd)
        out_s[:, cs] = _dot(xc_s[:, cs].astype(BF16), w_ref[h], NN)


def _lru_coeffs(pa, pi, xc, ba, bi, sp):
    ra = _sigmoid(pa + ba)
    ri = _sigmoid(pi + bi)
    log_a = (-LRU_C * ra) * sp
    a = jnp.exp(log_a)
    mult = jnp.sqrt(_one_minus_exp(2.0 * log_a, a))
    return ra, ri, a, mult


def _softplus_neg(lam):
    v = -lam
    return jnp.maximum(v, 0.0) + _log1p(jnp.exp(-jnp.abs(v)))


def _mix_fwd(z, cw, cb, wa, ba, wi, bi, lam, sw, glo, gso, name):
    T = z.shape[0]
    C = z.shape[1] // 5
    H = wa.shape[0]
    hd = C // H
    tb = _blk(T, 256, BF16_ROWS)
    ng = tb // SUBLANES
    HDR = SUBLANES

    def body(z_ref, cw_ref, cb_ref, wa_ref, ba_ref, wi_ref, bi_ref, lam_ref, sw_ref, glo_ref, gso_ref,
             y_ref, h_ref, ra_ref, ri_ref, a_ref, m_ref, xc_s, q_ref, gel_ref, gelp_ref,
             xext, pext, pa_s, pi_s, y_s, hcar):
        @pl.when(pl.program_id(0) == 0)
        def _():
            xext[0:HDR, :] = jnp.zeros((HDR, C), F32)
            pext[0:HDR, :] = jnp.zeros((HDR, C), F32)
            hcar[...] = jnp.zeros_like(hcar)

        def fill(g, _):
            r = _rows(g)
            re = _rows(g + 1)
            xext[re, :] = z_ref[r, 0:C]
            pext[re, :] = z_ref[r, 3 * C:4 * C] * z_ref[r, 4 * C:5 * C]
            return 0
        lax.fori_loop(0, ng, fill, 0)

        def conv(g, _):
            xc_s[_rows(g), :] = _causal_conv(xext, g, cw_ref, 4) + cb_ref[...]
            return 0
        lax.fori_loop(0, ng, conv, 0)

        _gate_pre(xc_s, wa_ref, pa_s, H, hd)
        _gate_pre(xc_s, wi_ref, pi_s, H, hd)
        sp = _softplus_neg(lam_ref[...])

        def group(g, hprev):
            r = _rows(g)
            xc = xc_s[r, :]
            ra, ri, a, mult = _lru_coeffs(pa_s[r, :], pi_s[r, :], xc, ba_ref[...], bi_ref[...], sp)
            ra_ref[r, :] = ra
            ri_ref[r, :] = ri
            a_ref[r, :] = a
            m_ref[r, :] = mult
            A, U = _scan8(a, mult * (ri * xc), reverse=False)
            hh = A * hprev + U
            h_ref[r, :] = hh
            gel, gelp = _gelu_parts(z_ref[r, C:2 * C])
            gel_ref[r, :] = gel
            gelp_ref[r, :] = gelp
            y_lru = hh * gel
            y_s[r, 0:C] = y_lru * _rstd(y_lru) * glo_ref[...]
            q = _causal_conv(pext, g, sw_ref, 3)
            q_ref[r, :] = q
            y_sc = z_ref[r, 2 * C:3 * C] * q
            y_s[r, C:2 * C] = y_sc * _rstd(y_sc) * gso_ref[...]
            return jnp.broadcast_to(hh[SUBLANES - 1:SUBLANES, :], hh.shape)
        hcar[...] = lax.fori_loop(0, ng // 2, lambda t, hp: group(2 * t + 1, group(2 * t, hp)), hcar[...])

        xext[0:HDR, :] = xext[tb:tb + HDR, :]
        pext[0:HDR, :] = pext[tb:tb + HDR, :]

        def cast(g, _):
            r = pl.ds(pl.multiple_of(g * BF16_ROWS, BF16_ROWS), BF16_ROWS)
            y_ref[r, :] = y_s[r, :].astype(BF16)
            return 0
        lax.fori_loop(0, tb // BF16_ROWS, cast, 0)

    full = lambda shape: pl.BlockSpec(shape, lambda i: (0,) * len(shape))
    blk = lambda w: pl.BlockSpec((tb, w), lambda i: (i, 0))
    ext = pltpu.VMEM((tb + HDR, C), F32)
    tile = pltpu.VMEM((tb, C), F32)
    return pl.pallas_call(
        body, name=name, grid=(T // tb,),
        in_specs=[blk(5 * C), full((4, C)), full((1, C)), full((H, hd, hd)), full((1, C)), full((H, hd, hd)),
                  full((1, C)), full((1, C)), full((3, C)), full((1, C)), full((1, C))],
        out_specs=[blk(2 * C)] + [blk(C)] * 9,
        out_shape=[SDS((T, 2 * C), BF16)] + [SDS((T, C), F32)] * 9,
        scratch_shapes=[ext, ext, tile, tile, pltpu.VMEM((tb, 2 * C), F32), pltpu.VMEM((SUBLANES, C), F32)],
        compiler_params=_cp(("arbitrary",), 48),
    )(z, cw, cb, wa, ba, wi, bi, lam, sw, glo, gso)


def _mix_bwd(z, h, saved, dy, cw, wa, wi, lam, sw, glo, gso, name):
    T = z.shape[0]
    C = z.shape[1] // 5
    H = wa.shape[0]
    hd = C // H
    tb = _blk(T, 256, BF16_ROWS)
    nb = T // tb
    ng = tb // SUBLANES
    HDR = SUBLANES
    N_ACC = 13

    def body(z_ref, zp_ref, h_ref, hp_ref, ra_ref, ri_ref, a_ref, m_ref, xc_s, q_ref, gel_ref, gelp_ref, dy_ref, cw_ref,
             wa_ref, wi_ref,
             lam_ref, sw_ref, glo_ref, gso_ref, dz_ref, small_ref, dwa_ref, dwi_ref,
             xext, pext, hext, dqext, dxcext, bext, dh_s, dpa_s, dpi_s, dz_s, acc_s, bcar):
        i = pl.program_id(0)
        first_rows = i == nb - 1

        @pl.when(i == 0)
        def _():
            dqext[tb:tb + HDR, :] = jnp.zeros((HDR, C), F32)
            dxcext[tb:tb + HDR, :] = jnp.zeros((HDR, C), F32)
            bcar[...] = jnp.zeros_like(bcar)
            acc_s[...] = jnp.zeros_like(acc_s)
            dwa_ref[...] = jnp.zeros_like(dwa_ref)
            dwi_ref[...] = jnp.zeros_like(dwi_ref)

        zero = jnp.zeros((HDR, C), F32)
        xext[0:HDR, :] = jnp.where(first_rows, zero, zp_ref[:, 0:C])
        pext[0:HDR, :] = jnp.where(first_rows, zero, zp_ref[:, 3 * C:4 * C] * zp_ref[:, 4 * C:5 * C])
        hext[0:HDR, :] = jnp.where(first_rows, zero, hp_ref[...])

        def fill(g, _):
            r = _rows(g)
            re = _rows(g + 1)
            xext[re, :] = z_ref[r, 0:C]
            pext[re, :] = z_ref[r, 3 * C:4 * C] * z_ref[r, 4 * C:5 * C]
            hext[re, :] = h_ref[r, :]
            return 0
        lax.fori_loop(0, ng, fill, 0)

        sp = _softplus_neg(lam_ref[...])
        dsp_dlam = -jax.nn.sigmoid(-lam_ref[...])

        def add_acc(k, v):
            acc_s[k] += v

        def p1(g, _):
            r = _rows(g)
            hh = h_ref[r, :]
            gel = gel_ref[r, :]
            gelp = gelp_ref[r, :]
            y_lru = hh * gel
            dnl = dy_ref[r, 0:C]
            rl = _rstd(y_lru)
            ylr = y_lru * rl
            gd = glo_ref[...] * dnl
            dy_lru = rl * (gd - ylr * jnp.mean(gd * ylr, axis=-1, keepdims=True))
            add_acc(R_GLO, dnl * ylr)
            dz_s[r, C:2 * C] = dy_lru * hh * gelp
            dh = dy_lru * gel
            dh_s[r, :] = dh

            q = q_ref[r, :]
            scb = z_ref[r, 2 * C:3 * C]
            y_sc = scb * q
            dns = dy_ref[r, C:2 * C]
            rs = _rstd(y_sc)
            ysr = y_sc * rs
            gs = gso_ref[...] * dns
            dy_sc = rs * (gs - ysr * jnp.mean(gs * ysr, axis=-1, keepdims=True))
            add_acc(R_GSO, dns * ysr)
            dz_s[r, 2 * C:3 * C] = dy_sc * q
            dqext[r, :] = dy_sc * scb
            return 0
        lax.fori_loop(0, ng, p1, 0, unroll=2)

        bext[tb:tb + HDR, :] = bcar[...]

        def p2(j, carry):
            g = ng - 1 - j
            r = _rows(g)
            a = a_ref[r, :]
            A, U = _scan8(a, a * dh_s[r, :], reverse=True)
            bb = A * carry + U
            bext[r, :] = bb
            return jnp.broadcast_to(bb[0:1, :], bb.shape)
        bcar[...] = lax.fori_loop(0, ng, p2, bcar[...])

        def p3(g, _):
            r = _rows(g)
            rn = _rows(g + 1)
            G = dh_s[r, :] + _shift_fwd(bext[r, :], bext[rn, :], 1)
            hm1 = _shift_back(hext[r, :], hext[rn, :], 1)
            a = a_ref[r, :]
            mult = m_ref[r, :]
            ri = ri_ref[r, :]
            xc = xc_s[r, :]
            ra = ra_ref[r, :]
            dxcext[r, :] = G * mult * ri
            dri = G * mult * xc
            dmult = G * ri * xc
            dlog_a = (G * hm1) * a - dmult * (a * a) / mult
            add_acc(R_LAM, dlog_a * (-LRU_C * ra) * dsp_dlam)
            dpa = dlog_a * (-LRU_C * sp) * ra * (1.0 - ra)
            dpi = dri * ri * (1.0 - ri)
            add_acc(R_BA, dpa)
            add_acc(R_BI, dpi)
            dpa_s[r, :] = dpa
            dpi_s[r, :] = dpi
            return 0
        lax.fori_loop(0, ng, p3, 0)

        for hh_ in range(H):
            cs = slice(hh_ * hd, (hh_ + 1) * hd)
            dpa_b = dpa_s[:, cs].astype(BF16)
            dpi_b = dpi_s[:, cs].astype(BF16)
            xc_b = xc_s[:, cs].astype(BF16)
            dxcext[0:tb, cs] += _dot(dpa_b, wa_ref[hh_], NT) + _dot(dpi_b, wi_ref[hh_], NT)
            dwa_ref[hh_] += _dot(xc_b, dpa_b, TN)
            dwi_ref[hh_] += _dot(xc_b, dpi_b, TN)

        def p4(g, _):
            r = _rows(g)
            rn = _rows(g + 1)
            dxc = dxcext[r, :]
            dxc_n = dxcext[rn, :]
            x_p = xext[r, :]
            x_c = xext[rn, :]
            add_acc(R_CB, dxc)
            dlx = dxc * cw_ref[3:4, :]
            add_acc(R_CW + 3, dxc * x_c)
            for d in range(1, 4):
                dlx = dlx + _shift_fwd(dxc, dxc_n, d) * cw_ref[3 - d:4 - d, :]
                add_acc(R_CW + 3 - d, dxc * _shift_back(x_p, x_c, d))
            dz_s[r, 0:C] = dlx

            dq = dqext[r, :]
            dq_n = dqext[rn, :]
            p_p = pext[r, :]
            p_c = pext[rn, :]
            dp = dq * sw_ref[2:3, :]
            add_acc(R_SW + 2, dq * p_c)
            for d in range(1, 3):
                dp = dp + _shift_fwd(dq, dq_n, d) * sw_ref[2 - d:3 - d, :]
                add_acc(R_SW + 2 - d, dq * _shift_back(p_p, p_c, d))
            dz_s[r, 3 * C:4 * C] = dp * z_ref[r, 4 * C:5 * C]
            dz_s[r, 4 * C:5 * C] = dp * z_ref[r, 3 * C:4 * C]
            return 0
        lax.fori_loop(0, ng, p4, 0)

        dqext[tb:tb + HDR, :] = dqext[0:HDR, :]
        dxcext[tb:tb + HDR, :] = dxcext[0:HDR, :]

        def cast(g, _):
            r = pl.ds(pl.multiple_of(g * BF16_ROWS, BF16_ROWS), BF16_ROWS)
            dz_ref[r, :] = dz_s[r, :].astype(BF16)
            return 0
        lax.fori_loop(0, tb // BF16_ROWS, cast, 0)

        @pl.when(i == nb - 1)
        def _():
            small_ref[...] = jnp.zeros_like(small_ref)
            for k in range(N_ACC):
                small_ref[k:k + 1, :] = jnp.sum(acc_s[k], axis=0, keepdims=True)

    tpg = tb // SUBLANES
    full = lambda shape: pl.BlockSpec(shape, lambda i: (0,) * len(shape))
    blk = lambda w: pl.BlockSpec((tb, w), lambda i: (nb - 1 - i, 0))
    prev = lambda w: pl.BlockSpec((SUBLANES, w), lambda i: (jnp.maximum((nb - 1 - i) * tpg - 1, 0), 0))
    ext = pltpu.VMEM((tb + HDR, C), F32)
    tile = pltpu.VMEM((tb, C), F32)
    return pl.pallas_call(
        body, name=name, grid=(nb,),
        in_specs=[blk(5 * C), prev(5 * C), blk(C), prev(C)] + [blk(C)] * 8
        + [blk(2 * C), full((4, C)), full((H, hd, hd)), full((H, hd, hd)), full((1, C)), full((3, C)),
           full((1, C)), full((1, C))],
        out_specs=[blk(5 * C), full((SMALL_ROWS, C)), full((H, hd, hd)), full((H, hd, hd))],
        out_shape=[SDS((T, 5 * C), BF16), SDS((SMALL_ROWS, C), F32), SDS((H, hd, hd), F32), SDS((H, hd, hd), F32)],
        scratch_shapes=[ext] * 6 + [tile] * 3 + [pltpu.VMEM((tb, 5 * C), F32), pltpu.VMEM((N_ACC, SUBLANES, C), F32),
                                                pltpu.VMEM((SUBLANES, C), F32)],
        compiler_params=_cp(("arbitrary",), 56),
    )(z, z, h, h, *saved, dy, cw, wa, wi, lam, sw, glo, gso)


def _add_slabs(terms, out_dtype, name):
    R, Ccols = terms[0].shape
    br = _blk(R, 512, BF16_ROWS)
    n = len(terms)

    def body(*refs):
        s = refs[0][...].astype(F32)
        for t_ref in refs[1:n]:
            s = s + t_ref[...].astype(F32)
        refs[n][...] = s.astype(out_dtype)

    spec = pl.BlockSpec((br, Ccols), lambda i: (i, 0))
    return pl.pallas_call(
        body, name=name, grid=(R // br,), in_specs=[spec] * n, out_specs=spec, out_shape=SDS((R, Ccols), out_dtype),
        compiler_params=_cp(("arbitrary",), 40),
    )(*terms)


def _reduced_rows(sb, lb, off, rows, extra_in, n_out, body, name, chip):
    Ccols = sb.shape[2]
    br = _blk(math.gcd(off, rows) if off else rows, 192, BF16_ROWS)
    ob = off // br
    src = lambda pick: pl.BlockSpec((1, br, Ccols), lambda i, c: (pick(c), ob + i, 0))
    own = pl.BlockSpec((br, Ccols), lambda i, c: (i, 0))
    return pl.pallas_call(
        body, name=name,
        grid_spec=pltpu.PrefetchScalarGridSpec(
            num_scalar_prefetch=1, grid=(rows // br,),
            in_specs=[src(lambda c: c[0]), src(lambda c: 0), src(lambda c: 1), src(lambda c: 2)] + [own] * len(extra_in),
            out_specs=[own] * n_out),
        out_shape=[SDS((rows, Ccols), F32)] * n_out,
        compiler_params=_cp(("arbitrary",), 40),
    )(chip, sb, lb, lb, lb, *extra_in)


def _sum4(sb_ref, l0, l1, l2):
    s = sb_ref[0].astype(F32)
    for t_ref in (l0, l1, l2):
        s = s + t_ref[0].astype(F32)
    return s


def _final_grad(sb, lb, chip, off, rows, name):
    def body(chip_ref, sb_ref, l0, l1, l2, o_ref):
        o_ref[...] = _sum4(sb_ref, l0, l1, l2)

    return _reduced_rows(sb, lb, off, rows, [], 1, body, name, chip)[0]


def _adamw_math(w, g, m, v):
    nm = ADAM_B1 * m + (1.0 - ADAM_B1) * g
    nv = ADAM_B2 * v + (1.0 - ADAM_B2) * (g * g)
    c1 = 1.0 - ADAM_B1 ** ADAM_STEP
    c2 = 1.0 - ADAM_B2 ** ADAM_STEP
    return -ADAM_LR * ((nm / c1) / (jnp.sqrt(nv / c2) + ADAM_EPS) + ADAM_WD * w), nm, nv


def _sum_adamw(sb, lb, chip, off, rows, w, m, v, name):
    def body(chip_ref, sb_ref, l0, l1, l2, w_ref, m_ref, v_ref, g_ref, d_ref, nm_ref, nv_ref):
        g = _sum4(sb_ref, l0, l1, l2)
        g_ref[...] = g
        d_ref[...], nm_ref[...], nv_ref[...] = _adamw_math(w_ref[...], g, m_ref[...], v_ref[...])

    return _reduced_rows(sb, lb, off, rows, [w, m, v], 4, body, name, chip)


def _adamw(w, g, m, v, name):
    R, Ccols = w.shape
    br = _blk(R, 256, SUBLANES)

    def body(w_ref, g_ref, m_ref, v_ref, d_ref, nm_ref, nv_ref):
        d_ref[...], nm_ref[...], nv_ref[...] = _adamw_math(w_ref[...], g_ref[...], m_ref[...], v_ref[...])

    spec = pl.BlockSpec((br, Ccols), lambda i: (i, 0))
    return pl.pallas_call(
        body, name=name, grid=(R // br,), in_specs=[spec] * 4, out_specs=[spec] * 3,
        out_shape=[SDS((R, Ccols), F32)] * 3, compiler_params=_cp(("arbitrary",), 40),
    )(w, g, m, v)


def _place():
    return lax.axis_index("x"), lax.axis_index("y"), lax.axis_index("c")


def _dev_rows(ref, dev, rows):
    return ref.at[pl.ds((4 * dev[0] + 2 * dev[1] + dev[2]) * rows, rows), :]


def _remote(src, dst, send_sem, recv_sem, to):
    return pltpu.make_async_remote_copy(src_ref=src, dst_ref=dst, send_sem=send_sem, recv_sem=recv_sem,
                                        device_id=to, device_id_type=MESH)


SAME_CORE_AND_SIBLING = ((0, 0, 1), (1, 0, 0), (0, 1, 0), (1, 1, 0))


def _merge_phases(a, b):
    na_in, na_out, na_sem = len(a.inputs), len(a.out_shapes), len(a.sem_shapes)

    def build(ins, outs, sems, stage):
        return (a.build(ins[:na_in], outs[:na_out], sems[:na_sem], stage)
                + b.build(ins[na_in:], outs[na_out:], sems[na_sem:], stage))

    aliases = dict(a.aliases)
    aliases.update({na_in + i: na_out + o for i, o in b.aliases.items()})
    return _Carried(a.inputs + b.inputs, a.out_shapes + b.out_shapes, aliases, a.sem_shapes + b.sem_shapes, build,
                    has_mid=a.has_mid or b.has_mid)


def _ag_direct_phase(slab, pieces, flips):
    W = slab.shape[1]
    n = len(pieces)
    npeer = len(flips)

    def build(ins, outs, sems, stage):
        if stage == "mid":
            return []
        starting = stage == "start"
        (slab_ref,) = ins
        send_sems, recv_sems, local_sems = sems
        x, y, c = _place()
        me = (x, y, c)
        peers = [tuple(1 - v if f else v for v, f in zip(me, flip)) for flip in flips]
        todo = []
        for p, (off, rows) in enumerate(pieces):
            src = slab_ref.at[pl.ds(off, rows), :]
            mine = pltpu.make_async_copy(src, _dev_rows(outs[p], me, rows), local_sems.at[p])
            todo.append(mine.start if starting else mine.wait)
            for k, peer in enumerate(peers):
                snd = _remote(src, _dev_rows(outs[p], me, rows), send_sems.at[k, p], recv_sems.at[k, p], peer)
                if starting:
                    todo.append(snd.start)
                else:
                    theirs = _dev_rows(outs[p], peer, rows)
                    rcv = _remote(theirs, theirs, send_sems.at[k, p], recv_sems.at[k, p], me)
                    todo += [rcv.wait_recv, snd.wait_send]
        return todo

    dma = pltpu.SemaphoreType.DMA
    return _Carried([slab], [SDS((N_DEV * rows, W), slab.dtype) for _, rows in pieces], {},
                    [dma((npeer, n)), dma((npeer, n)), dma((n,))], build)


def _ag_two_level_phase(slab, pieces):
    W = slab.shape[1]
    n = len(pieces)

    def build(ins, outs, sems, stage):
        (slab_ref,) = ins
        send_sems, recv_sems, local_sems = sems
        x, y, c = _place()
        me, sibling = (x, y, c), (x, y, 1 - c)
        chips = [(1 - x, y), (x, 1 - y), (1 - x, 1 - y)]
        todo = []
        for p, (off, rows) in enumerate(pieces):
            src = slab_ref.at[pl.ds(off, rows), :]
            own = _dev_rows(outs[p], me, rows)
            landed = [_dev_rows(outs[p], (*chip, c), rows) for chip in chips]

            def mine():
                return pltpu.make_async_copy(src, own, local_sems.at[p])

            def first():
                return [_remote(src, own, send_sems.at[k, p], recv_sems.at[k, p], to)
                        for k, to in enumerate([sibling] + [(*chip, c) for chip in chips])]

            def passed():
                return [_remote(blk, blk, send_sems.at[4 + j, p], recv_sems.at[4 + j, p], sibling)
                        for j, blk in enumerate(landed)]

            def arrival(k, blk):
                return _remote(blk, blk, send_sems.at[k, p], recv_sems.at[k, p], me).wait_recv

            if stage == "start":
                todo += [mine().start] + [cp.start for cp in first()]
            elif stage == "mid":
                for j, (blk, fwd) in enumerate(zip(landed, passed())):
                    todo += [arrival(1 + j, blk), fwd.start]
            else:
                theirs = [_dev_rows(outs[p], sibling, rows)] + [_dev_rows(outs[p], (*chip, 1 - c), rows) for chip in chips]
                todo += [arrival(k, blk) for k, blk in zip((0, 4, 5, 6), theirs)]
                todo += [cp.wait_send for cp in first() + passed()] + [mine().wait]
        return todo

    dma = pltpu.SemaphoreType.DMA
    return _Carried([slab], [SDS((N_DEV * rows, W), slab.dtype) for _, rows in pieces], {},
                    [dma((7, n)), dma((7, n)), dma((n,))], build, has_mid=True)


def _ag_forward_phase(gathered, pieces):
    n = len(pieces)

    def build(ins, outs, sems, stage):
        if stage == "mid":
            return []
        starting = stage == "start"
        send_sems, recv_sems = sems
        x, y, c = _place()
        me, sibling = (x, y, c), (x, y, 1 - c)
        chips = [(1 - x, y), (x, 1 - y), (1 - x, 1 - y)]
        todo = []
        for p, (_, rows) in enumerate(pieces):
            for j, chip in enumerate(chips):
                snd = _remote(_dev_rows(ins[p], (*chip, c), rows), _dev_rows(outs[p], (*chip, c), rows),
                              send_sems.at[j, p], recv_sems.at[j, p], sibling)
                if starting:
                    todo.append(snd.start)
                else:
                    theirs = _dev_rows(outs[p], (*chip, 1 - c), rows)
                    rcv = _remote(theirs, theirs, send_sems.at[j, p], recv_sems.at[j, p], me)
                    todo += [rcv.wait_recv, snd.wait_send]
        return todo

    dma = pltpu.SemaphoreType.DMA
    return _Carried(gathered, [SDS(g.shape, g.dtype) for g in gathered], {p: p for p in range(n)},
                    [dma((3, n)), dma((3, n))], build)


def _rs_chips_phase(sb):
    _, R, W = sb.shape

    def build(ins, outs, sems, stage):
        if stage == "mid":
            return []
        (sb_ref,), (land_ref,) = ins, outs
        send_sems, recv_sems = sems
        x, y, c = _place()
        chips = [(1 - x, y), (x, 1 - y), (1 - x, 1 - y)]
        cps = [_remote(sb_ref.at[2 * chip[0] + chip[1]], land_ref.at[j], send_sems.at[j], recv_sems.at[j], (*chip, c))
               for j, chip in enumerate(chips)]
        if stage == "start":
            return [cp.start for cp in cps]
        return [cp.wait_recv for cp in cps] + [cp.wait_send for cp in cps]

    dma = pltpu.SemaphoreType.DMA
    return _Carried([sb], [SDS((3, R, W), sb.dtype)], {}, [dma((3,)), dma((3,))], build)


def _allgather(slab, pieces, name):
    R, W = slab.shape
    n = len(pieces)
    assert sum(rows for _, rows in pieces) == R

    def body(slab_ref, *refs):
        outs = refs[:n]
        send_sems, recv_sems, local_sems = refs[n:]
        x, y, c = _place()
        me, sibling = (x, y, c), (x, y, 1 - c)
        chips = [(1 - x, y), (x, 1 - y), (1 - x, 1 - y)]

        def dst_rows(p, origin):
            rows = pieces[p][1]
            start = (4 * origin[0] + 2 * origin[1] + origin[2]) * rows
            return outs[p].at[pl.ds(start, rows), :]

        def copies(k, origin, to, from_slab):
            out = []
            for p, (off, rows) in enumerate(pieces):
                dst = dst_rows(p, origin)
                src = slab_ref.at[pl.ds(off, rows), :] if from_slab else dst
                out.append(pltpu.make_async_remote_copy(
                    src_ref=src, dst_ref=dst, send_sem=send_sems.at[k, p], recv_sem=recv_sems.at[k, p],
                    device_id=to, device_id_type=MESH))
            return out

        mine = [pltpu.make_async_copy(slab_ref.at[pl.ds(off, rows), :], dst_rows(p, me), local_sems.at[p])
                for p, (off, rows) in enumerate(pieces)]
        for cp in mine:
            cp.start()
        first = copies(0, me, sibling, True)
        for j, chip in enumerate(chips):
            first += copies(1 + j, me, (*chip, c), True)
        for cp in first:
            cp.start()
        passed = []
        for j, chip in enumerate(chips):
            for cp in copies(1 + j, (*chip, c), me, False):
                cp.wait_recv()
            fwd = copies(4 + j, (*chip, c), sibling, False)
            for cp in fwd:
                cp.start()
            passed += fwd
        for cp in copies(0, sibling, me, False):
            cp.wait_recv()
        for j, chip in enumerate(chips):
            for cp in copies(4 + j, (*chip, 1 - c), me, False):
                cp.wait_recv()
        for cp in first + passed:
            cp.wait_send()
        for cp in mine:
            cp.wait()

    return pl.pallas_call(
        body, name=name,
        in_specs=[HBM_SPEC], out_specs=[HBM_SPEC] * n,
        out_shape=[SDS((N_DEV * rows, W), slab.dtype) for _, rows in pieces],
        scratch_shapes=[pltpu.SemaphoreType.DMA((7, n)), pltpu.SemaphoreType.DMA((7, n)), pltpu.SemaphoreType.DMA((n,))],
    )(slab)


def _rs_sibling(grads, pieces, name):
    W = grads[0].shape[1]
    R = sum(rows for _, rows in pieces)
    n = len(pieces)
    dt = grads[0].dtype
    max_rows = max(rows for _, rows in pieces)
    steps = [(q, p) for q in range(N_CHIP) for p in range(n)]
    ns = len(steps)
    ADD_ROWS = 64
    SLOTS = 3
    assert all(rows % ADD_ROWS == 0 for _, rows in pieces)

    def body(*refs):
        g_refs = refs[:n]
        sb_ref, mine_buf, send_buf, land_buf, out_buf, in_sems, out_sems, send_sems, recv_sems, credit = refs[n:]
        x, y, c = _place()
        sibling = (x, y, 1 - c)

        def loads(s):
            q, p = steps[s]
            rows = pieces[p][1]
            slot = s % SLOTS
            mine = g_refs[p].at[pl.ds((2 * q + c) * rows, rows), :]
            theirs = g_refs[p].at[pl.ds((2 * q + 1 - c) * rows, rows), :]
            return (pltpu.make_async_copy(mine, mine_buf.at[slot, pl.ds(0, rows), :], in_sems.at[slot, 0]),
                    pltpu.make_async_copy(theirs, send_buf.at[slot, pl.ds(0, rows), :], in_sems.at[slot, 1]))

        def send(s):
            rows = pieces[steps[s][1]][1]
            slot = s % SLOTS
            return pltpu.make_async_remote_copy(
                src_ref=send_buf.at[slot, pl.ds(0, rows), :], dst_ref=land_buf.at[slot, pl.ds(0, rows), :],
                send_sem=send_sems.at[slot], recv_sem=recv_sems.at[slot], device_id=sibling, device_id_type=MESH)

        def store(s):
            q, p = steps[s]
            off, rows = pieces[p]
            slot = s % SLOTS
            return pltpu.make_async_copy(out_buf.at[slot, pl.ds(0, rows), :], sb_ref.at[q, pl.ds(off, rows), :],
                                         out_sems.at[slot])

        def start_send(s):
            for cp in loads(s):
                cp.wait()
            if s >= SLOTS:
                pl.semaphore_wait(credit.at[s % SLOTS], 1)
            send(s).start()

        for s in range(min(SLOTS, ns)):
            for cp in loads(s):
                cp.start()
        for s in range(min(SLOTS - 1, ns)):
            start_send(s)
        for s in range(ns):
            slot = s % SLOTS
            rows = pieces[steps[s][1]][1]
            if s + SLOTS - 1 < ns:
                start_send(s + SLOTS - 1)
            send(s).wait_recv()
            if s >= SLOTS:
                store(s - SLOTS).wait()

            def add(k, _, slot=slot):
                r = pl.ds(pl.multiple_of(k * ADD_ROWS, ADD_ROWS), ADD_ROWS)
                out_buf[slot, r, :] = (mine_buf[slot, r, :].astype(F32) + land_buf[slot, r, :].astype(F32)).astype(dt)
                return 0
            lax.fori_loop(0, rows // ADD_ROWS, add, 0)
            if s + SLOTS < ns:
                pl.semaphore_signal(credit.at[slot], inc=1, device_id=sibling, device_id_type=MESH)
            store(s).start()
            send(s).wait_send()
            if s + SLOTS < ns:
                for cp in loads(s + SLOTS):
                    cp.start()
        for s in range(max(ns - SLOTS, 0), ns):
            store(s).wait()

    buf = pltpu.VMEM((SLOTS, max_rows, W), dt)
    return pl.pallas_call(
        body, name=name,
        in_specs=[HBM_SPEC] * n, out_specs=HBM_SPEC,
        out_shape=SDS((N_CHIP, R, W), dt),
        scratch_shapes=[buf, buf, buf, buf, pltpu.SemaphoreType.DMA((SLOTS, 2)), pltpu.SemaphoreType.DMA((SLOTS,)),
                        pltpu.SemaphoreType.DMA((SLOTS,)), pltpu.SemaphoreType.DMA((SLOTS,)),
                        pltpu.SemaphoreType.REGULAR((SLOTS,))],
        compiler_params=pltpu.CompilerParams(vmem_limit_bytes=48 * MIB),
    )(*grads)


SMALL_NAMES = ("ffn1_norm", "mix_norm", "ffn2_norm", "final_norm", "lru_conv_w", "lru_conv_b", "lru_w_a", "lru_b_a",
               "lru_w_i", "lru_b_i", "lru_lambda", "sc_conv_w", "lru_out_norm", "sc_out_norm")
WEIGHT_NAMES = ("ffn1_norm", "ffn1_w_gate", "ffn1_w_up", "ffn1_w_down", "mix_norm", "w_in", "lru_conv_w", "lru_conv_b",
                "lru_w_a", "lru_b_a", "lru_w_i", "lru_b_i", "lru_lambda", "sc_conv_w", "lru_out_norm", "sc_out_norm",
                "w_out", "ffn2_norm", "ffn2_w_gate", "ffn2_w_up", "ffn2_w_down", "final_norm")
BIG = (("ffn1_w_gate", True), ("ffn1_w_up", True), ("ffn1_w_down", False), ("ffn2_w_gate", True), ("ffn2_w_up", True),
       ("ffn2_w_down", False), ("w_in", True), ("w_out", False))


SLAB_ROW_ALIGN = 256


def _pack_rows(parts, width):
    rows, counts = [], []
    for p in parts:
        flat = p.reshape(-1)
        nr = -(-flat.shape[0] // width)
        nr = -(-nr // SUBLANES) * SUBLANES
        rows.append(jnp.pad(flat, (0, nr * width - flat.shape[0])).reshape(nr, width))
        counts.append(nr)
    total = sum(counts)
    pad = -(-total // SLAB_ROW_ALIGN) * SLAB_ROW_ALIGN - total
    if pad:
        rows.append(jnp.zeros((pad, width), rows[0].dtype))
    return jnp.concatenate(rows, axis=0), counts


def _stack_rows(blocks):
    pieces, off = [], 0
    for b in blocks:
        pieces.append((off, b.shape[0]))
        off += b.shape[0]
    return jnp.concatenate(blocks, axis=0), pieces


def _unpack_rows(slab, counts, shapes):
    out, r = [], 0
    for nr, shape in zip(counts, shapes):
        size = math.prod(shape)
        out.append(slab[r:r + nr].reshape(-1)[:size].reshape(shape))
        r += nr
    return out


def kernel(x, ffn1_norm, ffn1_w_gate, ffn1_w_up, ffn1_w_down, mix_norm, w_in, lru_conv_w, lru_conv_b, lru_w_a, lru_b_a, lru_w_i, lru_b_i, lru_lambda, sc_conv_w, lru_out_norm, sc_out_norm, w_out, ffn2_norm, ffn2_w_gate, ffn2_w_up, ffn2_w_down, final_norm, loss_target, m_ffn1_norm, m_ffn1_w_gate, m_ffn1_w_up, m_ffn1_w_down, m_mix_norm, m_w_in, m_lru_conv_w, m_lru_conv_b, m_lru_w_a, m_lru_b_a, m_lru_w_i, m_lru_b_i, m_lru_lambda, m_sc_conv_w, m_lru_out_norm, m_sc_out_norm, m_w_out, m_ffn2_norm, m_ffn2_w_gate, m_ffn2_w_up, m_ffn2_w_down, m_final_norm, v_ffn1_norm, v_ffn1_w_gate, v_ffn1_w_up, v_ffn1_w_down, v_mix_norm, v_w_in, v_lru_conv_w, v_lru_conv_b, v_lru_w_a, v_lru_b_a, v_lru_w_i, v_lru_b_i, v_lru_lambda, v_sc_conv_w, v_lru_out_norm, v_sc_out_norm, v_w_out, v_ffn2_norm, v_ffn2_w_gate, v_ffn2_w_up, v_ffn2_w_down, v_final_norm):
    a = dict(locals())
    w = {n: a[n] for n in WEIGHT_NAMES}
    m = {n: a["m_" + n] for n in WEIGHT_NAMES}
    v = {n: a["v_" + n] for n in WEIGHT_NAMES}
    ax, ay, ac = _place()
    dev = 4 * ax + 2 * ay + ac
    chip = (2 * ax + ay).astype(jnp.int32).reshape(1)

    x0 = x[0]
    tgt = loss_target[0]
    T, D = x0.shape
    C = D // 2
    H, hd = lru_w_a.shape[1], lru_w_a.shape[2]
    CL = lru_conv_w.shape[2]

    shards = []
    for name, transposed in BIG:
        s = w[name][0]
        shards.append((s.T if transposed else s).astype(BF16))
    taps = jnp.concatenate([lru_conv_w[0], sc_conv_w[0], jnp.zeros((1, CL), F32)], axis=0)
    taps_row = lax.bitcast_convert_type(taps, BF16).reshape(1, -1)
    taps_blk = jnp.pad(taps_row, ((0, BF16_ROWS - 1), (0, D - taps_row.shape[1])))
    s_wg1, s_wu1, s_wd1, s_wg2, s_wu2, s_wd2, s_win, s_wout = shards
    slab_g1, pcs_g1 = _stack_rows([s_wg1])
    slab_u1, pcs_u1 = _stack_rows([s_wu1])
    slab_d1, pcs_d1 = _stack_rows([s_wd1])
    slab_mw, pcs_mw = _stack_rows([s_win, s_wout, taps_blk])
    slab_g2, pcs_g2 = _stack_rows([s_wg2])
    slab_ud2, pcs_ud2 = _stack_rows([s_wu2, s_wd2])
    (wg1,) = _allgather(slab_g1, pcs_g1, "allgather_ffn1_gate")

    g1, gm, g3 = ffn1_norm, mix_norm, ffn2_norm
    phase = _merge_phases(_ag_two_level_phase(slab_u1, pcs_u1), _ag_direct_phase(slab_d1, pcs_d1, SAME_CORE_AND_SIBLING))
    (n1, hg1), got = _norm_proj(x0, g1, [wg1], [BF16], False, "ffn1_gate", carried=phase)
    wu1, d1 = got[0], got[1:]
    phase = _merge_phases(_ag_forward_phase(d1, pcs_d1), _ag_direct_phase(slab_mw, pcs_mw, SAME_CORE_AND_SIBLING))
    (hu1, act1), got = _up_act(n1, wu1, hg1, "ffn1_up", carried=phase)
    wd1, mixw = got[0], got[1:]
    phase = _merge_phases(_ag_forward_phase(mixw, pcs_mw), _ag_direct_phase(slab_g2, pcs_g2, SAME_CORE_AND_SIBLING))
    x1, got = _mm_res(act1, wd1, x0, 0.5, "ffn1_down", carried=phase)
    (win, wout, taps_all), g2 = got[:3], got[3:]
    phase = _merge_phases(_ag_forward_phase(g2, pcs_g2), _ag_direct_phase(slab_ud2, pcs_ud2, SAME_CORE_AND_SIBLING))
    (n2, z), got = _norm_proj(x1, gm, [win], [F32], False, "in_proj", carried=phase)
    (wg2,), ud2 = got[:1], got[1:]
    taps_all = taps_all.reshape(N_DEV, BF16_ROWS, D)[:, 0, :2 * SUBLANES * CL].reshape(N_DEV, SUBLANES, CL, 2)
    taps_all = lax.bitcast_convert_type(taps_all, F32)
    taps_all = taps_all.transpose(1, 0, 2).reshape(SUBLANES, N_DEV * CL)
    cw, sw = taps_all[0:4], taps_all[4:7]

    gf = final_norm.reshape(1, D)
    cb = lru_conv_b
    wa, wi = lru_w_a[0].astype(BF16), lru_w_i[0].astype(BF16)
    ba, bi = lru_b_a.reshape(1, C), lru_b_i.reshape(1, C)
    lam, glo, gso = lru_lambda, lru_out_norm, sc_out_norm

    y, h, *saved = _mix_fwd(z, cw, cb, wa, ba, wi, bi, lam, sw, glo, gso, "mix_fwd")
    x2, (wu2, wd2) = _mm_res(y, wout, x1, 1.0, "out_proj", carried=_ag_forward_phase(ud2, pcs_ud2))
    n3, hg2, hu2, act2 = _norm_proj(x2, g3, [wg2, wu2], [BF16, BF16], True, "ffn2_up")
    x3 = _mm_res(act2, wd2, x2, 0.5, "ffn2_down")
    dx3, df2, d_gf, loss_blk = _loss_head(x3, gf, tgt, "loss_head")

    F = wd1.shape[0]
    bm_f = F // 4 if (F // 4) % LANES == 0 else 512

    def reduce_group(gs, tag):
        pcs, off = [], 0
        for g_ in gs:
            pcs.append((off, g_.shape[0] // N_DEV))
            off += g_.shape[0] // N_DEV
        sb_ = _rs_sibling(gs, pcs, "rs_sibling_add_" + tag)
        return sb_, pcs

    dhg2, dhu2 = _ffn_bwd_act(df2, wd2, hg2, hu2, "ffn2_bwd_act")
    d_wd2 = _dw_tn(act2, df2, bm_f, "ffn2_dw_down")
    d_wg2 = _dw_tn(dhg2, n3, bm_f, "ffn2_dw_gate")
    d_wu2 = _dw_tn(dhu2, n3, bm_f, "ffn2_dw_up")
    sb_f2, pcs_f2 = reduce_group([d_wg2, d_wu2, d_wd2], "ffn2")
    (dx2, dx2b, d_g3), (lb_f2,) = _mm_rmsbwd([(dhg2, wg2), (dhu2, wu2)], x2, g3, dx3, 1.0, "ffn2_bwd_in",
                                             carried=_rs_chips_phase(sb_f2))
    dy = _mm_nt(dx2b, wout, "out_proj_bwd")
    d_wout = _dw_tn(y, dx2b, 1024, "out_proj_dw")
    dz, small, d_wa, d_wi = _mix_bwd(z, h, saved, dy, cw, wa, wi, lam, sw, glo, gso, "mix_bwd")
    d_win = _dw_tn(dz, n2, 1280, "in_proj_dw")
    sb_mx, pcs_mx = reduce_group([d_win, d_wout], "mix")
    (dx1, df1, d_gm), (lb_mx,) = _mm_rmsbwd([(dz, win)], x1, gm, dx2, 0.5, "in_proj_bwd",
                                            carried=_rs_chips_phase(sb_mx))
    dhg1, dhu1 = _ffn_bwd_act(df1, wd1, hg1, hu1, "ffn1_bwd_act")
    d_wd1 = _dw_tn(act1, df1, bm_f, "ffn1_dw_down")
    d_wg1 = _dw_tn(dhg1, n1, bm_f, "ffn1_dw_gate")
    d_wu1 = _dw_tn(dhu1, n1, bm_f, "ffn1_dw_up")
    sb_f1, pcs_f1 = reduce_group([d_wg1, d_wu1, d_wd1], "ffn1")
    early_names = [n_ for n_ in SMALL_NAMES if n_ != "ffn1_norm"]
    early_parts = dict(zip(SMALL_NAMES, [None, d_gm, d_g3, d_gf, small[R_CW:R_CW + 4], small[R_CB], d_wa, small[R_BA], d_wi,
                                         small[R_BI], small[R_LAM], small[R_SW:R_SW + 3], small[R_GLO], small[R_GSO]]))
    early_slab, early_counts = _pack_rows([early_parts[n_] for n_ in early_names], LANES)
    RE = early_slab.shape[0]
    phase = _merge_phases(_ag_two_level_phase(early_slab, [(0, RE)]), _rs_chips_phase(sb_f1))
    (dx0, _, d_g1), (early_all, lb_f1) = _mm_rmsbwd([(dhg1, wg1), (dhu1, wu1)], x0, g1, dx1, 1.0, "ffn1_bwd_in",
                                                    carried=phase)

    grads, delta, new_m, new_v = {}, {}, {}, {}
    transposed_shard = dict(BIG)
    for names, sb_, lb_, pcs in ((("ffn2_w_gate", "ffn2_w_up", "ffn2_w_down"), sb_f2, lb_f2, pcs_f2),
                                 (("w_in", "w_out"), sb_mx, lb_mx, pcs_mx),
                                 (("ffn1_w_gate", "ffn1_w_up", "ffn1_w_down"), sb_f1, lb_f1, pcs_f1)):
        for name, (off, rows) in zip(names, pcs):
            flip = transposed_shard[name] and w[name].shape[2] % LANES != 0
            if transposed_shard[name] and not flip:
                g_ = _final_grad(sb_, lb_, chip, off, rows, "rs_final_sum_" + name).T
                d_, m_, v_ = _adamw(w[name][0], g_, m[name][0], v[name][0], "adamw_" + name)
            else:
                view = (lambda t: t[0].T) if flip else (lambda t: t[0])
                g_, d_, m_, v_ = _sum_adamw(sb_, lb_, chip, off, rows, view(w[name]), view(m[name]), view(v[name]),
                                            "sum_adamw_" + name)
            back = (lambda t: t.T[None]) if flip else (lambda t: t[None])
            grads[name], delta[name], new_m[name], new_v[name] = back(g_), back(d_), back(m_), back(v_)

    shape_of = dict(zip(SMALL_NAMES, [(1, D), (1, D), (1, D), (D,), (1, 4, C), (1, C), (1, H, hd, hd), (1, H, hd),
                                      (1, H, hd, hd), (1, H, hd), (1, C), (1, 3, C), (1, C), (1, C)]))
    early_sum = _add_slabs([early_all[j * RE:(j + 1) * RE] for j in range(N_DEV)], F32, "small_grads_sum")
    small_full = dict(zip(early_names, _unpack_rows(early_sum, early_counts, [shape_of[n_] for n_ in early_names])))
    late_slab, late_counts = _pack_rows([d_g1], LANES)
    RL = late_slab.shape[0]
    (late_all,) = _allgather(late_slab, [(0, RL)], "allgather_ffn1_norm_grad")
    late_sum = _add_slabs([late_all[j * RL:(j + 1) * RL] for j in range(N_DEV)], F32, "ffn1_norm_grad_sum")
    (small_full["ffn1_norm"],) = _unpack_rows(late_sum, late_counts, [shape_of["ffn1_norm"]])

    for name in SMALL_NAMES:
        gfull = small_full[name]
        if name in ("lru_conv_w", "sc_conv_w"):
            gfull = lax.dynamic_slice_in_dim(gfull, dev * CL, CL, axis=2)
        grads[name] = gfull

    packs = [_pack_rows([t[n_] for n_ in SMALL_NAMES], LANES) for t in (w, grads, m, v)]
    sd, sm, sv = _adamw(packs[0][0], packs[1][0], packs[2][0], packs[3][0], "adamw_small")
    shapes = [w[n_].shape for n_ in SMALL_NAMES]
    for tgt_dict, slab_ in ((delta, sd), (new_m, sm), (new_v, sv)):
        for n_, val in zip(SMALL_NAMES, _unpack_rows(slab_, packs[0][1], shapes)):
            tgt_dict[n_] = val

    loss = lax.psum(loss_blk[0, 0], ("x", "y", "c"))
    return (loss, dx0[None], *[grads[n_] for n_ in WEIGHT_NAMES], *[delta[n_] for n_ in WEIGHT_NAMES],
            *[new_m[n_] for n_ in WEIGHT_NAMES], *[new_v[n_] for n_ in WEIGHT_NAMES])
```

```python
import functools
import math

import jax
import jax.numpy as jnp
from jax import lax
from jax.experimental import pallas as pl
from jax.experimental.pallas import tpu as pltpu

F32 = jnp.float32
BF16 = jnp.bfloat16
SDS = jax.ShapeDtypeStruct
MESH = pl.DeviceIdType.MESH

NORM_EPS = 1e-6
LRU_C = 8.0
N_DEV = 8
N_CHIP = 4
ADAM_LR, ADAM_B1, ADAM_B2, ADAM_EPS, ADAM_WD, ADAM_STEP = 0.001, 0.9, 0.999, 1e-08, 0.01, 10

NN = (((1,), (0,)), ((), ()))
NT = (((1,), (1,)), ((), ()))
TN = (((0,), (0,)), ((), ()))

SUBLANES = 8
BF16_ROWS = 16
LANES = 128
MIB = 1 << 20


def _dot(a, b, dims):
    return lax.dot_general(a, b, dims, preferred_element_type=F32)


def _blk(n, pref, align):
    if n <= pref:
        return n
    b = (pref // align) * align
    while b >= align:
        if n % b == 0:
            return b
        b -= align
    raise ValueError(f"no block of {n} aligned to {align} under {pref}")


def _cp(sem, vmem_mib):
    return pltpu.CompilerParams(dimension_semantics=sem, vmem_limit_bytes=vmem_mib * MIB)


HBM_SPEC = pl.BlockSpec(memory_space=pltpu.HBM)
MID_EIGHTHS = 6

class _Carried:
    def __init__(self, inputs, out_shapes, aliases, sem_shapes, build, has_mid=False):
        self.inputs, self.out_shapes, self.aliases = list(inputs), list(out_shapes), dict(aliases)
        self.sem_shapes, self.build, self.has_mid = list(sem_shapes), build, has_mid


def _call(body, *, name, grid, in_specs, out_specs, out_shape, scratch_shapes, compiler_params, args, carried=None):
    if carried is None:
        return pl.pallas_call(body, name=name, grid=grid, in_specs=in_specs, out_specs=out_specs, out_shape=out_shape,
                              scratch_shapes=scratch_shapes, compiler_params=compiler_params)(*args)
    n_in, n_out, n_sc = len(in_specs), len(out_shape), len(scratch_shapes)
    c_in, c_out = len(carried.inputs), len(carried.out_shapes)

    def hosted(*refs):
        ins, refs = refs[:n_in], refs[n_in:]
        c_ins, refs = refs[:c_in], refs[c_in:]
        outs, refs = refs[:n_out], refs[n_out:]
        c_outs, refs = refs[:c_out], refs[c_out:]
        scratch, c_sems = refs[:n_sc], refs[n_sc:]
        first = functools.reduce(jnp.logical_and, [pl.program_id(a) == 0 for a in range(len(grid))])
        last = functools.reduce(jnp.logical_and, [pl.program_id(a) == g - 1 for a, g in enumerate(grid)])

        @pl.when(first)
        def _():
            for start in carried.build(c_ins, c_outs, c_sems, "start"):
                start()

        if carried.has_mid:
            mid = functools.reduce(jnp.logical_and, [pl.program_id(0) == (grid[0] * MID_EIGHTHS) // 8]
                                   + [pl.program_id(a) == 0 for a in range(1, len(grid))])

            @pl.when(mid)
            def _():
                for step in carried.build(c_ins, c_outs, c_sems, "mid"):
                    step()

        body(*ins, *outs, *scratch)

        @pl.when(last)
        def _():
            for wait in carried.build(c_ins, c_outs, c_sems, "end"):
                wait()

    out = pl.pallas_call(
        hosted, name=name, grid=grid, in_specs=list(in_specs) + [HBM_SPEC] * c_in,
        out_specs=list(out_specs) + [HBM_SPEC] * c_out, out_shape=list(out_shape) + carried.out_shapes,
        scratch_shapes=list(scratch_shapes) + carried.sem_shapes,
        input_output_aliases={n_in + a: n_out + b for a, b in carried.aliases.items()},
        compiler_params=compiler_params)(*args, *carried.inputs)
    return out[:n_out], out[n_out:]


ROW_CHUNK = 128


def _chunk_rows(c):
    return pl.ds(pl.multiple_of(c * ROW_CHUNK, ROW_CHUNK), ROW_CHUNK)


def _rstd(xv):
    return lax.rsqrt(jnp.mean(xv * xv, axis=-1, keepdims=True) + NORM_EPS)


def _rms_bwd(xv, g, dn):
    r = _rstd(xv)
    xr = xv * r
    gd = g * dn
    dx = r * (gd - xr * jnp.mean(gd * xr, axis=-1, keepdims=True))
    return dx, jnp.sum(dn * xr, axis=0, keepdims=True)


def _log1p(e):
    u = 1.0 + e
    return jnp.where(u == 1.0, e, jnp.log(u) * (e / (u - 1.0)))


def _one_minus_exp(v, exp_half_v):
    series = 1.0 / 5040.0
    for coeff in (1.0 / 720.0, 1.0 / 120.0, 1.0 / 24.0, 1.0 / 6.0, 0.5, 1.0):
        series = series * v + coeff
    return jnp.where(v > -0.5, -v * series, 1.0 - exp_half_v * exp_half_v)


def _sigmoid(v):
    return 0.5 * jnp.tanh(0.5 * v) + 0.5


def _gelu_parts(g):
    k0 = math.sqrt(2.0 / math.pi)
    g2 = g * g
    t = jnp.tanh(k0 * (g + 0.044715 * g * g2))
    gel = 0.5 * g * (1.0 + t)
    gelp = 0.5 * (1.0 + t) + 0.5 * g * (1.0 - t * t) * (k0 * (1.0 + 3.0 * 0.044715 * g2))
    return gel, gelp


def _norm_proj(x, gain, w_list, out_dtypes, swiglu, name, carried=None):
    T, D = x.shape
    N = w_list[0].shape[0]
    nw = len(w_list)
    bm = _blk(T, 1024, BF16_ROWS)
    bn = _blk(N, 1024 // nw, LANES)

    def body(*refs):
        x_ref, g_ref = refs[:2]
        w_refs = refs[2:2 + nw]
        n_ref = refs[2 + nw]
        o_refs = refs[3 + nw:3 + 2 * nw]
        act_ref = refs[3 + 2 * nw] if swiglu else None
        n_sc = refs[-1]

        @pl.when(pl.program_id(1) == 0)
        def _():
            def chunk(c, _):
                r = _chunk_rows(c)
                xv = x_ref[r, :]
                nb = (xv * _rstd(xv) * g_ref[...]).astype(BF16)
                n_sc[r, :] = nb
                n_ref[r, :] = nb
                return 0
            lax.fori_loop(0, bm // ROW_CHUNK, chunk, 0)

        n = n_sc[...]
        outs = [_dot(n, w_ref[...], NT) for w_ref in w_refs]
        for o_ref, o in zip(o_refs, outs):
            o_ref[...] = o.astype(o_ref.dtype)
        if swiglu:
            hg, hu = outs
            act_ref[...] = (hg * _sigmoid(hg) * hu).astype(BF16)

    row = pl.BlockSpec((bm, D), lambda i, j: (i, 0))
    tile = pl.BlockSpec((bm, bn), lambda i, j: (i, j))
    n_extra = 1 if swiglu else 0
    return _call(
        body, name=name, grid=(T // bm, N // bn),
        in_specs=[row, pl.BlockSpec((1, D), lambda i, j: (0, 0))] + [pl.BlockSpec((bn, D), lambda i, j: (j, 0))] * nw,
        out_specs=[row] + [tile] * (nw + n_extra),
        out_shape=[SDS((T, D), BF16)] + [SDS((T, N), dt) for dt in out_dtypes] + [SDS((T, N), BF16)] * n_extra,
        scratch_shapes=[pltpu.VMEM((bm, D), BF16)],
        compiler_params=_cp(("arbitrary", "arbitrary"), 52),
        args=(x, gain, *w_list), carried=carried)


def _up_act(n, wu, hg, name, carried=None):
    T, D = n.shape
    F = wu.shape[0]
    bm = _blk(T, 1024, BF16_ROWS)
    bn = _blk(F, 512, LANES)

    def body(n_ref, wu_ref, hg_ref, hu_ref, act_ref):
        hu = _dot(n_ref[...], wu_ref[...], NT)
        hg = hg_ref[...].astype(F32)
        hu_ref[...] = hu.astype(BF16)
        act_ref[...] = (hg * _sigmoid(hg) * hu).astype(BF16)

    tile = pl.BlockSpec((bm, bn), lambda i, j: (i, j))
    return _call(
        body, name=name, grid=(T // bm, F // bn),
        in_specs=[pl.BlockSpec((bm, D), lambda i, j: (i, 0)), pl.BlockSpec((bn, D), lambda i, j: (j, 0)), tile],
        out_specs=[tile, tile], out_shape=[SDS((T, F), BF16)] * 2, scratch_shapes=[],
        compiler_params=_cp(("arbitrary", "arbitrary"), 40),
        args=(n, wu, hg), carried=carried)


def _mm_res(a, b, x, scale, name, carried=None):
    T, K = a.shape
    D = b.shape[1]
    bm = _blk(T, 1024, BF16_ROWS)
    bk = _blk(K, 1408, LANES)
    nk = K // bk

    def body(a_ref, b_ref, x_ref, o_ref):
        k = pl.program_id(1)

        @pl.when(k == 0)
        def _():
            o_ref[...] = jnp.zeros_like(o_ref)

        o_ref[...] += _dot(a_ref[...], b_ref[...], NN)

        @pl.when(k == nk - 1)
        def _():
            def chunk(c, _):
                r = _chunk_rows(c)
                o_ref[r, :] = x_ref[r, :] + scale * o_ref[r, :]
                return 0
            lax.fori_loop(0, bm // ROW_CHUNK, chunk, 0)

    row = pl.BlockSpec((bm, D), lambda i, k: (i, 0))
    out = _call(
        body, name=name, grid=(T // bm, nk),
        in_specs=[pl.BlockSpec((bm, bk), lambda i, k: (i, k)), pl.BlockSpec((bk, D), lambda i, k: (k, 0)), row],
        out_specs=[row], out_shape=[SDS((T, D), F32)], scratch_shapes=[],
        compiler_params=_cp(("arbitrary", "arbitrary"), 56),
        args=(a, b, x), carried=carried)
    return out[0] if carried is None else (out[0][0], out[1])


def _mm_nt(a, b, name):
    T, K = a.shape
    N = b.shape[0]
    bm = _blk(T, 1024, BF16_ROWS)
    bn = _blk(N, 512, LANES)

    def body(a_ref, b_ref, o_ref):
        o_ref[...] = _dot(a_ref[...], b_ref[...], NT)

    return pl.pallas_call(
        body, name=name, grid=(T // bm, N // bn),
        in_specs=[pl.BlockSpec((bm, K), lambda i, j: (i, 0)), pl.BlockSpec((bn, K), lambda i, j: (j, 0))],
        out_specs=pl.BlockSpec((bm, bn), lambda i, j: (i, j)), out_shape=SDS((T, N), F32),
        compiler_params=_cp(("arbitrary", "arbitrary"), 40),
    )(a, b)


def _ffn_bwd_act(dfb, wd, hg, hu, name):
    T, D = dfb.shape
    F = wd.shape[0]
    bm = _blk(T, 1024, BF16_ROWS)
    bn = _blk(F, 512, LANES)

    def body(df_ref, wd_ref, hg_ref, hu_ref, dhg_ref, dhu_ref):
        dact = _dot(df_ref[...], wd_ref[...], NT)
        hgv = hg_ref[...].astype(F32)
        huv = hu_ref[...].astype(F32)
        s = _sigmoid(hgv)
        dhu_ref[...] = (dact * (hgv * s)).astype(BF16)
        dhg_ref[...] = (dact * huv * (s * (1.0 + hgv * (1.0 - s)))).astype(BF16)

    tile = pl.BlockSpec((bm, bn), lambda i, j: (i, j))
    return pl.pallas_call(
        body, name=name, grid=(T // bm, F // bn),
        in_specs=[pl.BlockSpec((bm, D), lambda i, j: (i, 0)), pl.BlockSpec((bn, D), lambda i, j: (j, 0)), tile, tile],
        out_specs=[tile, tile], out_shape=[SDS((T, F), BF16)] * 2,
        compiler_params=_cp(("arbitrary", "arbitrary"), 40),
    )(dfb, wd, hg, hu)


def _dw_tn(a, b, bm_pref, name):
    T, M = a.shape
    N = b.shape[1]
    bm = _blk(M, bm_pref, LANES)
    tk = _blk(T, 1024, BF16_ROWS)
    nk = T // tk

    def body(a_ref, b_ref, o_ref, acc):
        k = pl.program_id(1)

        @pl.when(k == 0)
        def _():
            acc[...] = jnp.zeros_like(acc)

        acc[...] += _dot(a_ref[...], b_ref[...], TN)

        @pl.when(k == nk - 1)
        def _():
            o_ref[...] = acc[...].astype(BF16)

    return pl.pallas_call(
        body, name=name, grid=(M // bm, nk),
        in_specs=[pl.BlockSpec((tk, bm), lambda i, k: (k, i)), pl.BlockSpec((tk, N), lambda i, k: (k, 0))],
        out_specs=pl.BlockSpec((bm, N), lambda i, k: (i, 0)), out_shape=SDS((M, N), BF16),
        scratch_shapes=[pltpu.VMEM((bm, N), F32)],
        compiler_params=_cp(("arbitrary", "arbitrary"), 48),
    )(a, b)


def _mm_rmsbwd(pairs, x, gain, dx_in, bscale, name, carried=None):
    T, D = x.shape
    K = pairs[0][0].shape[1]
    npair = len(pairs)
    bm = _blk(T, 1024, BF16_ROWS)
    bk = _blk(K, 1024 // npair, LANES)
    nk = K // bk

    nchunk = bm // ROW_CHUNK

    def body(*refs):
        ab = refs[:2 * npair]
        x_hbm, g_ref, dxin_hbm, dx_ref, dxb_ref, dg_ref, x_buf, dxin_buf, sems = refs[2 * npair:]
        i = pl.program_id(0)
        k = pl.program_id(1)

        def fetch(c, slot):
            rows = pl.ds(i * bm + c * ROW_CHUNK, ROW_CHUNK)
            return (pltpu.make_async_copy(x_hbm.at[rows, :], x_buf.at[slot], sems.at[slot, 0]),
                    pltpu.make_async_copy(dxin_hbm.at[rows, :], dxin_buf.at[slot], sems.at[slot, 1]))

        @pl.when(k == 0)
        def _():
            dx_ref[...] = jnp.zeros_like(dx_ref)

        @pl.when(k == nk - 1)
        def _():
            for cp in fetch(0, 0):
                cp.start()

        for q in range(npair):
            dx_ref[...] += _dot(ab[2 * q][...], ab[2 * q + 1][...], NN)

        @pl.when(k == nk - 1)
        def _():
            @pl.when(i == 0)
            def _():
                dg_ref[...] = jnp.zeros_like(dg_ref)

            def chunk(c, _):
                slot = c % 2

                @pl.when(c + 1 < nchunk)
                def _():
                    for cp in fetch(c + 1, 1 - slot):
                        cp.start()

                for cp in fetch(c, slot):
                    cp.wait()

                r = _chunk_rows(c)
                dx, dg = _rms_bwd(x_buf[slot], g_ref[...], dx_ref[r, :])
                dxo = dxin_buf[slot] + dx
                dx_ref[r, :] = dxo
                dxb_ref[r, :] = (bscale * dxo).astype(BF16)
                dg_ref[...] += dg
                return 0
            lax.fori_loop(0, nchunk, chunk, 0)

    row = pl.BlockSpec((bm, D), lambda i, k: (i, 0))
    anywhere = pl.BlockSpec(memory_space=pl.ANY)
    vec = pl.BlockSpec((1, D), lambda i, k: (0, 0))
    in_specs = []
    args = []
    for a, b in pairs:
        in_specs += [pl.BlockSpec((bm, bk), lambda i, k: (i, k)), pl.BlockSpec((bk, D), lambda i, k: (k, 0))]
        args += [a, b]
    return _call(
        body, name=name, grid=(T // bm, nk),
        in_specs=in_specs + [anywhere, vec, anywhere], out_specs=[row, row, vec],
        out_shape=[SDS((T, D), F32), SDS((T, D), BF16), SDS((1, D), F32)],
        scratch_shapes=[pltpu.VMEM((2, ROW_CHUNK, D), F32), pltpu.VMEM((2, ROW_CHUNK, D), F32),
                        pltpu.SemaphoreType.DMA((2, 2))],
        compiler_params=_cp(("arbitrary", "arbitrary"), 52),
        args=(*args, x, gain, dx_in), carried=carried)


def _loss_head(x3, gain, tgt, name):
    T, D = x3.shape
    bm = _blk(T, 256, BF16_ROWS)

    def body(x_ref, g_ref, t_ref, dx_ref, dxb_ref, dg_ref, loss_ref):
        i = pl.program_id(0)
        xv = x_ref[...]
        g = g_ref[...]
        out = xv * _rstd(xv) * g
        e = out - t_ref[...]
        part = 0.5 * jnp.sum(jnp.mean(e * e, axis=-1, keepdims=True), axis=0, keepdims=True)
        dx, dg = _rms_bwd(xv, g, e * (1.0 / D))
        dx_ref[...] = dx
        dxb_ref[...] = (0.5 * dx).astype(BF16)

        @pl.when(i == 0)
        def _():
            dg_ref[...] = dg
            loss_ref[...] = jnp.broadcast_to(part, loss_ref.shape)

        @pl.when(i > 0)
        def _():
            dg_ref[...] += dg
            loss_ref[...] += jnp.broadcast_to(part, loss_ref.shape)

    row = pl.BlockSpec((bm, D), lambda i: (i, 0))
    vec = pl.BlockSpec((1, D), lambda i: (0, 0))
    return pl.pallas_call(
        body, name=name, grid=(T // bm,),
        in_specs=[row, vec, row], out_specs=[row, row, vec, pl.BlockSpec((SUBLANES, LANES), lambda i: (0, 0))],
        out_shape=[SDS((T, D), F32), SDS((T, D), BF16), SDS((1, D), F32), SDS((SUBLANES, LANES), F32)],
        compiler_params=_cp(("arbitrary",), 40),
    )(x3, gain, tgt)


R_CW, R_CB, R_BA, R_BI, R_LAM, R_SW, R_GLO, R_GSO, SMALL_ROWS = 0, 4, 5, 6, 7, 8, 11, 12, 16


def _rows(g):
    return pl.ds(pl.multiple_of(g * SUBLANES, SUBLANES), SUBLANES)


def _shift_back(prev, cur, d):
    row = lax.broadcasted_iota(jnp.int32, cur.shape, 0)
    return pltpu.roll(jnp.where(row >= SUBLANES - d, prev, cur), d, 0)


def _shift_fwd(cur, nxt, d):
    row = lax.broadcasted_iota(jnp.int32, cur.shape, 0)
    return pltpu.roll(jnp.where(row < d, nxt, cur), SUBLANES - d, 0)


def _causal_conv(ext, g, taps_ref, ntap):
    prev = ext[_rows(g), :]
    cur = ext[_rows(g + 1), :]
    out = _shift_back(prev, cur, ntap - 1) * taps_ref[0:1, :]
    for k in range(1, ntap - 1):
        out = out + _shift_back(prev, cur, ntap - 1 - k) * taps_ref[k:k + 1, :]
    return out + cur * taps_ref[ntap - 1:ntap, :]


def _scan8(A, U, reverse):
    row = lax.broadcasted_iota(jnp.int32, A.shape, 0)
    for s in (1, 2, 4):
        if reverse:
            A_sh = pltpu.roll(A, SUBLANES - s, 0)
            U_sh = pltpu.roll(U, SUBLANES - s, 0)
            m = row < SUBLANES - s
        else:
            A_sh = pltpu.roll(A, s, 0)
            U_sh = pltpu.roll(U, s, 0)
            m = row >= s
        U = jnp.where(m, A * U_sh + U, U)
        A = jnp.where(m, A * A_sh, A)
    return A, U


def _gate_pre(xc_s, w_ref, out_s, H, hd):
    for h in range(H):
        cs = slice(h * hd, (h + 1) * hd)
        out_s[:, cs] = _dot(xc_s[:, cs].astype(BF16), w_ref[h], NN)


def _lru_coeffs(pa, pi, xc, ba, bi, sp):
    ra = _sigmoid(pa + ba)
    ri = _sigmoid(pi + bi)
    log_a = (-LRU_C * ra) * sp
    a = jnp.exp(log_a)
    mult = jnp.sqrt(_one_minus_exp(2.0 * log_a, a))
    return ra, ri, a, mult


def _softplus_neg(lam):
    v = -lam
    return jnp.maximum(v, 0.0) + _log1p(jnp.exp(-jnp.abs(v)))


def _mix_fwd(z, cw, cb, wa, ba, wi, bi, lam, sw, glo, gso, name):
    T = z.shape[0]
    C = z.shape[1] // 5
    H = wa.shape[0]
    hd = C // H
    tb = _blk(T, 256, BF16_ROWS)
    ng = tb // SUBLANES
    HDR = SUBLANES

    def body(z_ref, cw_ref, cb_ref, wa_ref, ba_ref, wi_ref, bi_ref, lam_ref, sw_ref, glo_ref, gso_ref,
             y_ref, h_ref, ra_ref, ri_ref, a_ref, m_ref, xc_s, q_ref, gel_ref, gelp_ref,
             xext, pext, pa_s, pi_s, y_s, hcar):
        @pl.when(pl.program_id(0) == 0)
        def _():
            xext[0:HDR, :] = jnp.zeros((HDR, C), F32)
            pext[0:HDR, :] = jnp.zeros((HDR, C), F32)
            hcar[...] = jnp.zeros_like(hcar)

        def fill(g, _):
            r = _rows(g)
            re = _rows(g + 1)
            xext[re, :] = z_ref[r, 0:C]
            pext[re, :] = z_ref[r, 3 * C:4 * C] * z_ref[r, 4 * C:5 * C]
            return 0
        lax.fori_loop(0, ng, fill, 0)

        def conv(g, _):
            xc_s[_rows(g), :] = _causal_conv(xext, g, cw_ref, 4) + cb_ref[...]
            return 0
        lax.fori_loop(0, ng, conv, 0)

        _gate_pre(xc_s, wa_ref, pa_s, H, hd)
        _gate_pre(xc_s, wi_ref, pi_s, H, hd)
        sp = _softplus_neg(lam_ref[...])

        def group(g, hprev):
            r = _rows(g)
            xc = xc_s[r, :]
            ra, ri, a, mult = _lru_coeffs(pa_s[r, :], pi_s[r, :], xc, ba_ref[...], bi_ref[...], sp)
            ra_ref[r, :] = ra
            ri_ref[r, :] = ri
            a_ref[r, :] = a
            m_ref[r, :] = mult
            A, U = _scan8(a, mult * (ri * xc), reverse=False)
            hh = A * hprev + U
            h_ref[r, :] = hh
            gel, gelp = _gelu_parts(z_ref[r, C:2 * C])
            gel_ref[r, :] = gel
            gelp_ref[r, :] = gelp
            y_lru = hh * gel
            y_s[r, 0:C] = y_lru * _rstd(y_lru) * glo_ref[...]
            q = _causal_conv(pext, g, sw_ref, 3)
            q_ref[r, :] = q
            y_sc = z_ref[r, 2 * C:3 * C] * q
            y_s[r, C:2 * C] = y_sc * _rstd(y_sc) * gso_ref[...]
            return jnp.broadcast_to(hh[SUBLANES - 1:SUBLANES, :], hh.shape)
        hcar[...] = lax.fori_loop(0, ng // 2, lambda t, hp: group(2 * t + 1, group(2 * t, hp)), hcar[...])

        xext[0:HDR, :] = xext[tb:tb + HDR, :]
        pext[0:HDR, :] = pext[tb:tb + HDR, :]

        def cast(g, _):
            r = pl.ds(pl.multiple_of(g * BF16_ROWS, BF16_ROWS), BF16_ROWS)
            y_ref[r, :] = y_s[r, :].astype(BF16)
            return 0
        lax.fori_loop(0, tb // BF16_ROWS, cast, 0)

    full = lambda shape: pl.BlockSpec(shape, lambda i: (0,) * len(shape))
    blk = lambda w: pl.BlockSpec((tb, w), lambda i: (i, 0))
    ext = pltpu.VMEM((tb + HDR, C), F32)
    tile = pltpu.VMEM((tb, C), F32)
    return pl.pallas_call(
        body, name=name, grid=(T // tb,),
        in_specs=[blk(5 * C), full((4, C)), full((1, C)), full((H, hd, hd)), full((1, C)), full((H, hd, hd)),
                  full((1, C)), full((1, C)), full((3, C)), full((1, C)), full((1, C))],
        out_specs=[blk(2 * C)] + [blk(C)] * 9,
        out_shape=[SDS((T, 2 * C), BF16)] + [SDS((T, C), F32)] * 9,
        scratch_shapes=[ext, ext, tile, tile, pltpu.VMEM((tb, 2 * C), F32), pltpu.VMEM((SUBLANES, C), F32)],
        compiler_params=_cp(("arbitrary",), 48),
    )(z, cw, cb, wa, ba, wi, bi, lam, sw, glo, gso)


def _mix_bwd(z, h, saved, dy, cw, wa, wi, lam, sw, glo, gso, name):
    T = z.shape[0]
    C = z.shape[1] // 5
    H = wa.shape[0]
    hd = C // H
    tb = _blk(T, 256, BF16_ROWS)
    nb = T // tb
    ng = tb // SUBLANES
    HDR = SUBLANES
    N_ACC = 13

    def body(z_ref, zp_ref, h_ref, hp_ref, ra_ref, ri_ref, a_ref, m_ref, xc_s, q_ref, gel_ref, gelp_ref, dy_ref, cw_ref,
             wa_ref, wi_ref,
             lam_ref, sw_ref, glo_ref, gso_ref, dz_ref, small_ref, dwa_ref, dwi_ref,
             xext, pext, hext, dqext, dxcext, bext, dh_s, dpa_s, dpi_s, dz_s, acc_s, bcar):
        i = pl.program_id(0)
        first_rows = i == nb - 1

        @pl.when(i == 0)
        def _():
            dqext[tb:tb + HDR, :] = jnp.zeros((HDR, C), F32)
            dxcext[tb:tb + HDR, :] = jnp.zeros((HDR, C), F32)
            bcar[...] = jnp.zeros_like(bcar)
            acc_s[...] = jnp.zeros_like(acc_s)
            dwa_ref[...] = jnp.zeros_like(dwa_ref)
            dwi_ref[...] = jnp.zeros_like(dwi_ref)

        zero = jnp.zeros((HDR, C), F32)
        xext[0:HDR, :] = jnp.where(first_rows, zero, zp_ref[:, 0:C])
        pext[0:HDR, :] = jnp.where(first_rows, zero, zp_ref[:, 3 * C:4 * C] * zp_ref[:, 4 * C:5 * C])
        hext[0:HDR, :] = jnp.where(first_rows, zero, hp_ref[...])

        def fill(g, _):
            r = _rows(g)
            re = _rows(g + 1)
            xext[re, :] = z_ref[r, 0:C]
            pext[re, :] = z_ref[r, 3 * C:4 * C] * z_ref[r, 4 * C:5 * C]
            hext[re, :] = h_ref[r, :]
            return 0
        lax.fori_loop(0, ng, fill, 0)

        sp = _softplus_neg(lam_ref[...])
        dsp_dlam = -jax.nn.sigmoid(-lam_ref[...])

        def add_acc(k, v):
            acc_s[k] += v

        def p1(g, _):
            r = _rows(g)
            hh = h_ref[r, :]
            gel = gel_ref[r, :]
            gelp = gelp_ref[r, :]
            y_lru = hh * gel
            dnl = dy_ref[r, 0:C]
            rl = _rstd(y_lru)
            ylr = y_lru * rl
            gd = glo_ref[...] * dnl
            dy_lru = rl * (gd - ylr * jnp.mean(gd * ylr, axis=-1, keepdims=True))
            add_acc(R_GLO, dnl * ylr)
            dz_s[r, C:2 * C] = dy_lru * hh * gelp
            dh = dy_lru * gel
            dh_s[r, :] = dh

            q = q_ref[r, :]
            scb = z_ref[r, 2 * C:3 * C]
            y_sc = scb * q
            dns = dy_ref[r, C:2 * C]
            rs = _rstd(y_sc)
            ysr = y_sc * rs
            gs = gso_ref[...] * dns
            dy_sc = rs * (gs - ysr * jnp.mean(gs * ysr, axis=-1, keepdims=True))
            add_acc(R_GSO, dns * ysr)
            dz_s[r, 2 * C:3 * C] = dy_sc * q
            dqext[r, :] = dy_sc * scb
            return 0
        lax.fori_loop(0, ng, p1, 0, unroll=2)

        bext[tb:tb + HDR, :] = bcar[...]

        def p2(j, carry):
            g = ng - 1 - j
            r = _rows(g)
            a = a_ref[r, :]
            A, U = _scan8(a, a * dh_s[r, :], reverse=True)
            bb = A * carry + U
            bext[r, :] = bb
            return jnp.broadcast_to(bb[0:1, :], bb.shape)
        bcar[...] = lax.fori_loop(0, ng, p2, bcar[...])

        def p3(g, _):
            r = _rows(g)
            rn = _rows(g + 1)
            G = dh_s[r, :] + _shift_fwd(bext[r, :], bext[rn, :], 1)
            hm1 = _shift_back(hext[r, :], hext[rn, :], 1)
            a = a_ref[r, :]
            mult = m_ref[r, :]
            ri = ri_ref[r, :]
            xc = xc_s[r, :]
            ra = ra_ref[r, :]
            dxcext[r, :] = G * mult * ri
            dri = G * mult * xc
            dmult = G * ri * xc
            dlog_a = (G * hm1) * a - dmult * (a * a) / mult
            add_acc(R_LAM, dlog_a * (-LRU_C * ra) * dsp_dlam)
            dpa = dlog_a * (-LRU_C * sp) * ra * (1.0 - ra)
            dpi = dri * ri * (1.0 - ri)
            add_acc(R_BA, dpa)
            add_acc(R_BI, dpi)
            dpa_s[r, :] = dpa
            dpi_s[r, :] = dpi
            return 0
        lax.fori_loop(0, ng, p3, 0)

        for hh_ in range(H):
            cs = slice(hh_ * hd, (hh_ + 1) * hd)
            dpa_b = dpa_s[:, cs].astype(BF16)
            dpi_b = dpi_s[:, cs].astype(BF16)
            xc_b = xc_s[:, cs].astype(BF16)
            dxcext[0:tb, cs] += _dot(dpa_b, wa_ref[hh_], NT) + _dot(dpi_b, wi_ref[hh_], NT)
            dwa_ref[hh_] += _dot(xc_b, dpa_b, TN)
            dwi_ref[hh_] += _dot(xc_b, dpi_b, TN)

        def p4(g, _):
            r = _rows(g)
            rn = _rows(g + 1)
            dxc = dxcext[r, :]
            dxc_n = dxcext[rn, :]
            x_p = xext[r, :]
            x_c = xext[rn, :]
            add_acc(R_CB, dxc)
            dlx = dxc * cw_ref[3:4, :]
            add_acc(R_CW + 3, dxc * x_c)
            for d in range(1, 4):
                dlx = dlx + _shift_fwd(dxc, dxc_n, d) * cw_ref[3 - d:4 - d, :]
                add_acc(R_CW + 3 - d, dxc * _shift_back(x_p, x_c, d))
            dz_s[r, 0:C] = dlx

            dq = dqext[r, :]
            dq_n = dqext[rn, :]
            p_p = pext[r, :]
            p_c = pext[rn, :]
            dp = dq * sw_ref[2:3, :]
            add_acc(R_SW + 2, dq * p_c)
            for d in range(1, 3):
                dp = dp + _shift_fwd(dq, dq_n, d) * sw_ref[2 - d:3 - d, :]
                add_acc(R_SW + 2 - d, dq * _shift_back(p_p, p_c, d))
            dz_s[r, 3 * C:4 * C] = dp * z_ref[r, 4 * C:5 * C]
            dz_s[r, 4 * C:5 * C] = dp * z_ref[r, 3 * C:4 * C]
            return 0
        lax.fori_loop(0, ng, p4, 0)

        dqext[tb:tb + HDR, :] = dqext[0:HDR, :]
        dxcext[tb:tb + HDR, :] = dxcext[0:HDR, :]

        def cast(g, _):
            r = pl.ds(pl.multiple_of(g * BF16_ROWS, BF16_ROWS), BF16_ROWS)
            dz_ref[r, :] = dz_s[r, :].astype(BF16)
            return 0
        lax.fori_loop(0, tb // BF16_ROWS, cast, 0)

        @pl.when(i == nb - 1)
        def _():
            small_ref[...] = jnp.zeros_like(small_ref)
            for k in range(N_ACC):
                small_ref[k:k + 1, :] = jnp.sum(acc_s[k], axis=0, keepdims=True)

    tpg = tb // SUBLANES
    full = lambda shape: pl.BlockSpec(shape, lambda i: (0,) * len(shape))
    blk = lambda w: pl.BlockSpec((tb, w), lambda i: (nb - 1 - i, 0))
    prev = lambda w: pl.BlockSpec((SUBLANES, w), lambda i: (jnp.maximum((nb - 1 - i) * tpg - 1, 0), 0))
    ext = pltpu.VMEM((tb + HDR, C), F32)
    tile = pltpu.VMEM((tb, C), F32)
    return pl.pallas_call(
        body, name=name, grid=(nb,),
        in_specs=[blk(5 * C), prev(5 * C), blk(C), prev(C)] + [blk(C)] * 8
        + [blk(2 * C), full((4, C)), full((H, hd, hd)), full((H, hd, hd)), full((1, C)), full((3, C)),
           full((1, C)), full((1, C))],
        out_specs=[blk(5 * C), full((SMALL_ROWS, C)), full((H, hd, hd)), full((H, hd, hd))],
        out_shape=[SDS((T, 5 * C), BF16), SDS((SMALL_ROWS, C), F32), SDS((H, hd, hd), F32), SDS((H, hd, hd), F32)],
        scratch_shapes=[ext] * 6 + [tile] * 3 + [pltpu.VMEM((tb, 5 * C), F32), pltpu.VMEM((N_ACC, SUBLANES, C), F32),
                                                pltpu.VMEM((SUBLANES, C), F32)],
        compiler_params=_cp(("arbitrary",), 56),
    )(z, z, h, h, *saved, dy, cw, wa, wi, lam, sw, glo, gso)


def _add_slabs(terms, out_dtype, name):
    R, Ccols = terms[0].shape
    br = _blk(R, 512, BF16_ROWS)
    n = len(terms)

    def body(*refs):
        s = refs[0][...].astype(F32)
        for t_ref in refs[1:n]:
            s = s + t_ref[...].astype(F32)
        refs[n][...] = s.astype(out_dtype)

    spec = pl.BlockSpec((br, Ccols), lambda i: (i, 0))
    return pl.pallas_call(
        body, name=name, grid=(R // br,), in_specs=[spec] * n, out_specs=spec, out_shape=SDS((R, Ccols), out_dtype),
        compiler_params=_cp(("arbitrary",), 40),
    )(*terms)


def _reduced_rows(sb, lb, off, rows, extra_in, n_out, body, name, chip):
    Ccols = sb.shape[2]
    br = _blk(math.gcd(off, rows) if off else rows, 192, BF16_ROWS)
    ob = off // br
    src = lambda pick: pl.BlockSpec((1, br, Ccols), lambda i, c: (pick(c), ob + i, 0))
    own = pl.BlockSpec((br, Ccols), lambda i, c: (i, 0))
    return pl.pallas_call(
        body, name=name,
        grid_spec=pltpu.PrefetchScalarGridSpec(
            num_scalar_prefetch=1, grid=(rows // br,),
            in_specs=[src(lambda c: c[0]), src(lambda c: 0), src(lambda c: 1), src(lambda c: 2)] + [own] * len(extra_in),
            out_specs=[own] * n_out),
        out_shape=[SDS((rows, Ccols), F32)] * n_out,
        compiler_params=_cp(("arbitrary",), 40),
    )(chip, sb, lb, lb, lb, *extra_in)


def _sum4(sb_ref, l0, l1, l2):
    s = sb_ref[0].astype(F32)
    for t_ref in (l0, l1, l2):
        s = s + t_ref[0].astype(F32)
    return s


def _final_grad(sb, lb, chip, off, rows, name):
    def body(chip_ref, sb_ref, l0, l1, l2, o_ref):
        o_ref[...] = _sum4(sb_ref, l0, l1, l2)

    return _reduced_rows(sb, lb, off, rows, [], 1, body, name, chip)[0]


def _adamw_math(w, g, m, v):
    nm = ADAM_B1 * m + (1.0 - ADAM_B1) * g
    nv = ADAM_B2 * v + (1.0 - ADAM_B2) * (g * g)
    c1 = 1.0 - ADAM_B1 ** ADAM_STEP
    c2 = 1.0 - ADAM_B2 ** ADAM_STEP
    return -ADAM_LR * ((nm / c1) / (jnp.sqrt(nv / c2) + ADAM_EPS) + ADAM_WD * w), nm, nv


def _sum_adamw(sb, lb, chip, off, rows, w, m, v, name):
    def body(chip_ref, sb_ref, l0, l1, l2, w_ref, m_ref, v_ref, g_ref, d_ref, nm_ref, nv_ref):
        g = _sum4(sb_ref, l0, l1, l2)
        g_ref[...] = g
        d_ref[...], nm_ref[...], nv_ref[...] = _adamw_math(w_ref[...], g, m_ref[...], v_ref[...])

    return _reduced_rows(sb, lb, off, rows, [w, m, v], 4, body, name, chip)


def _adamw(w, g, m, v, name):
    R, Ccols = w.shape
    br = _blk(R, 256, SUBLANES)

    def body(w_ref, g_ref, m_ref, v_ref, d_ref, nm_ref, nv_ref):
        d_ref[...], nm_ref[...], nv_ref[...] = _adamw_math(w_ref[...], g_ref[...], m_ref[...], v_ref[...])

    spec = pl.BlockSpec((br, Ccols), lambda i: (i, 0))
    return pl.pallas_call(
        body, name=name, grid=(R // br,), in_specs=[spec] * 4, out_specs=[spec] * 3,
        out_shape=[SDS((R, Ccols), F32)] * 3, compiler_params=_cp(("arbitrary",), 40),
    )(w, g, m, v)


def _place():
    return lax.axis_index("x"), lax.axis_index("y"), lax.axis_index("c")


def _dev_rows(ref, dev, rows):
    return ref.at[pl.ds((4 * dev[0] + 2 * dev[1] + dev[2]) * rows, rows), :]


def _remote(src, dst, send_sem, recv_sem, to):
    return pltpu.make_async_remote_copy(src_ref=src, dst_ref=dst, send_sem=send_sem, recv_sem=recv_sem,
                                        device_id=to, device_id_type=MESH)


SAME_CORE_AND_SIBLING = ((0, 0, 1), (1, 0, 0), (0, 1, 0), (1, 1, 0))


def _merge_phases(a, b):
    na_in, na_out, na_sem = len(a.inputs), len(a.out_shapes), len(a.sem_shapes)

    def build(ins, outs, sems, stage):
        return (a.build(ins[:na_in], outs[:na_out], sems[:na_sem], stage)
                + b.build(ins[na_in:], outs[na_out:], sems[na_sem:], stage))

    aliases = dict(a.aliases)
    aliases.update({na_in + i: na_out + o for i, o in b.aliases.items()})
    return _Carried(a.inputs + b.inputs, a.out_shapes + b.out_shapes, aliases, a.sem_shapes + b.sem_shapes, build,
                    has_mid=a.has_mid or b.has_mid)


def _ag_direct_phase(slab, pieces, flips):
    W = slab.shape[1]
    n = len(pieces)
    npeer = len(flips)

    def build(ins, outs, sems, stage):
        if stage == "mid":
            return []
        starting = stage == "start"
        (slab_ref,) = ins
        send_sems, recv_sems, local_sems = sems
        x, y, c = _place()
        me = (x, y, c)
        peers = [tuple(1 - v if f else v for v, f in zip(me, flip)) for flip in flips]
        todo = []
        for p, (off, rows) in enumerate(pieces):
            src = slab_ref.at[pl.ds(off, rows), :]
            mine = pltpu.make_async_copy(src, _dev_rows(outs[p], me, rows), local_sems.at[p])
            todo.append(mine.start if starting else mine.wait)
            for k, peer in enumerate(peers):
                snd = _remote(src, _dev_rows(outs[p], me, rows), send_sems.at[k, p], recv_sems.at[k, p], peer)
                if starting:
                    todo.append(snd.start)
                else:
                    theirs = _dev_rows(outs[p], peer, rows)
                    rcv = _remote(theirs, theirs, send_sems.at[k, p], recv_sems.at[k, p], me)
                    todo += [rcv.wait_recv, snd.wait_send]
        return todo

    dma = pltpu.SemaphoreType.DMA
    return _Carried([slab], [SDS((N_DEV * rows, W), slab.dtype) for _, rows in pieces], {},
                    [dma((npeer, n)), dma((npeer, n)), dma((n,))], build)


def _ag_two_level_phase(slab, pieces):
    W = slab.shape[1]
    n = len(pieces)

    def build(ins, outs, sems, stage):
        (slab_ref,) = ins
        send_sems, recv_sems, local_sems = sems
        x, y, c = _place()
        me, sibling = (x, y, c), (x, y, 1 - c)
        chips = [(1 - x, y), (x, 1 - y), (1 - x, 1 - y)]
        todo = []
        for p, (off, rows) in enumerate(pieces):
            src = slab_ref.at[pl.ds(off, rows), :]
            own = _dev_rows(outs[p], me, rows)
            landed = [_dev_rows(outs[p], (*chip, c), rows) for chip in chips]

            def mine():
                return pltpu.make_async_copy(src, own, local_sems.at[p])

            def first():
                return [_remote(src, own, send_sems.at[k, p], recv_sems.at[k, p], to)
                        for k, to in enumerate([sibling] + [(*chip, c) for chip in chips])]

            def passed():
                return [_remote(blk, blk, send_sems.at[4 + j, p], recv_sems.at[4 + j, p], sibling)
                        for j, blk in enumerate(landed)]

            def arrival(k, blk):
                return _remote(blk, blk, send_sems.at[k, p], recv_sems.at[k, p], me).wait_recv

            if stage == "start":
                todo += [mine().start] + [cp.start for cp in first()]
            elif stage == "mid":
                for j, (blk, fwd) in enumerate(zip(landed, passed())):
                    todo += [arrival(1 + j, blk), fwd.start]
            else:
                theirs = [_dev_rows(outs[p], sibling, rows)] + [_dev_rows(outs[p], (*chip, 1 - c), rows) for chip in chips]
                todo += [arrival(k, blk) for k, blk in zip((0, 4, 5, 6), theirs)]
                todo += [cp.wait_send for cp in first() + passed()] + [mine().wait]
        return todo

    dma = pltpu.SemaphoreType.DMA
    return _Carried([slab], [SDS((N_DEV * rows, W), slab.dtype) for _, rows in pieces], {},
                    [dma((7, n)), dma((7, n)), dma((n,))], build, has_mid=True)


def _ag_forward_phase(gathered, pieces):
    n = len(pieces)

    def build(ins, outs, sems, stage):
        if stage == "mid":
            return []
        starting = stage == "start"
        send_sems, recv_sems = sems
        x, y, c = _place()
        me, sibling = (x, y, c), (x, y, 1 - c)
        chips = [(1 - x, y), (x, 1 - y), (1 - x, 1 - y)]
        todo = []
        for p, (_, rows) in enumerate(pieces):
            for j, chip in enumerate(chips):
                snd = _remote(_dev_rows(ins[p], (*chip, c), rows), _dev_rows(outs[p], (*chip, c), rows),
                              send_sems.at[j, p], recv_sems.at[j, p], sibling)
                if starting:
                    todo.append(snd.start)
                else:
                    theirs = _dev_rows(outs[p], (*chip, 1 - c), rows)
                    rcv = _remote(theirs, theirs, send_sems.at[j, p], recv_sems.at[j, p], me)
                    todo += [rcv.wait_recv, snd.wait_send]
        return todo

    dma = pltpu.SemaphoreType.DMA
    return _Carried(gathered, [SDS(g.shape, g.dtype) for g in gathered], {p: p for p in range(n)},
                    [dma((3, n)), dma((3, n))], build)


def _rs_chips_phase(sb):
    _, R, W = sb.shape

    def build(ins, outs, sems, stage):
        if stage == "mid":
            return []
        (sb_ref,), (land_ref,) = ins, outs
        send_sems, recv_sems = sems
        x, y, c = _place()
        chips = [(1 - x, y), (x, 1 - y), (1 - x, 1 - y)]
        cps = [_remote(sb_ref.at[2 * chip[0] + chip[1]], land_ref.at[j], send_sems.at[j], recv_sems.at[j], (*chip, c))
               for j, chip in enumerate(chips)]
        if stage == "start":
            return [cp.start for cp in cps]
        return [cp.wait_recv for cp in cps] + [cp.wait_send for cp in cps]

    dma = pltpu.SemaphoreType.DMA
    return _Carried([sb], [SDS((3, R, W), sb.dtype)], {}, [dma((3,)), dma((3,))], build)


def _allgather(slab, pieces, name):
    R, W = slab.shape
    n = len(pieces)
    assert sum(rows for _, rows in pieces) == R

    def body(slab_ref, *refs):
        outs = refs[:n]
        send_sems, recv_sems, local_sems = refs[n:]
        x, y, c = _place()
        me, sibling = (x, y, c), (x, y, 1 - c)
        chips = [(1 - x, y), (x, 1 - y), (1 - x, 1 - y)]

        def dst_rows(p, origin):
            rows = pieces[p][1]
            start = (4 * origin[0] + 2 * origin[1] + origin[2]) * rows
            return outs[p].at[pl.ds(start, rows), :]

        def copies(k, origin, to, from_slab):
            out = []
            for p, (off, rows) in enumerate(pieces):
                dst = dst_rows(p, origin)
                src = slab_ref.at[pl.ds(off, rows), :] if from_slab else dst
                out.append(pltpu.make_async_remote_copy(
                    src_ref=src, dst_ref=dst, send_sem=send_sems.at[k, p], recv_sem=recv_sems.at[k, p],
                    device_id=to, device_id_type=MESH))
            return out

        mine = [pltpu.make_async_copy(slab_ref.at[pl.ds(off, rows), :], dst_rows(p, me), local_sems.at[p])
                for p, (off, rows) in enumerate(pieces)]
        for cp in mine:
            cp.start()
        first = copies(0, me, sibling, True)
        for j, chip in enumerate(chips):
            first += copies(1 + j, me, (*chip, c), True)
        for cp in first:
            cp.start()
        passed = []
        for j, chip in enumerate(chips):
            for cp in copies(1 + j, (*chip, c), me, False):
                cp.wait_recv()
            fwd = copies(4 + j, (*chip, c), sibling, False)
            for cp in fwd:
                cp.start()
            passed += fwd
        for cp in copies(0, sibling, me, False):
            cp.wait_recv()
        for j, chip in enumerate(chips):
            for cp in copies(4 + j, (*chip, 1 - c), me, False):
                cp.wait_recv()
        for cp in first + passed:
            cp.wait_send()
        for cp in mine:
            cp.wait()

    return pl.pallas_call(
        body, name=name,
        in_specs=[HBM_SPEC], out_specs=[HBM_SPEC] * n,
        out_shape=[SDS((N_DEV * rows, W), slab.dtype) for _, rows in pieces],
        scratch_shapes=[pltpu.SemaphoreType.DMA((7, n)), pltpu.SemaphoreType.DMA((7, n)), pltpu.SemaphoreType.DMA((n,))],
    )(slab)


def _rs_sibling(grads, pieces, name):
    W = grads[0].shape[1]
    R = sum(rows for _, rows in pieces)
    n = len(pieces)
    dt = grads[0].dtype
    max_rows = max(rows for _, rows in pieces)
    steps = [(q, p) for q in range(N_CHIP) for p in range(n)]
    ns = len(steps)
    ADD_ROWS = 64
    SLOTS = 3
    assert all(rows % ADD_ROWS == 0 for _, rows in pieces)

    def body(*refs):
        g_refs = refs[:n]
        sb_ref, mine_buf, send_buf, land_buf, out_buf, in_sems, out_sems, send_sems, recv_sems, credit = refs[n:]
        x, y, c = _place()
        sibling = (x, y, 1 - c)

        def loads(s):
            q, p = steps[s]
            rows = pieces[p][1]
            slot = s % SLOTS
            mine = g_refs[p].at[pl.ds((2 * q + c) * rows, rows), :]
            theirs = g_refs[p].at[pl.ds((2 * q + 1 - c) * rows, rows), :]
            return (pltpu.make_async_copy(mine, mine_buf.at[slot, pl.ds(0, rows), :], in_sems.at[slot, 0]),
                    pltpu.make_async_copy(theirs, send_buf.at[slot, pl.ds(0, rows), :], in_sems.at[slot, 1]))

        def send(s):
            rows = pieces[steps[s][1]][1]
            slot = s % SLOTS
            return pltpu.make_async_remote_copy(
                src_ref=send_buf.at[slot, pl.ds(0, rows), :], dst_ref=land_buf.at[slot, pl.ds(0, rows), :],
                send_sem=send_sems.at[slot], recv_sem=recv_sems.at[slot], device_id=sibling, device_id_type=MESH)

        def store(s):
            q, p = steps[s]
            off, rows = pieces[p]
            slot = s % SLOTS
            return pltpu.make_async_copy(out_buf.at[slot, pl.ds(0, rows), :], sb_ref.at[q, pl.ds(off, rows), :],
                                         out_sems.at[slot])

        def start_send(s):
            for cp in loads(s):
                cp.wait()
            if s >= SLOTS:
                pl.semaphore_wait(credit.at[s % SLOTS], 1)
            send(s).start()

        for s in range(min(SLOTS, ns)):
            for cp in loads(s):
                cp.start()
        for s in range(min(SLOTS - 1, ns)):
            start_send(s)
        for s in range(ns):
            slot = s % SLOTS
            rows = pieces[steps[s][1]][1]
            if s + SLOTS - 1 < ns:
                start_send(s + SLOTS - 1)
            send(s).wait_recv()
            if s >= SLOTS:
                store(s - SLOTS).wait()

            def add(k, _, slot=slot):
                r = pl.ds(pl.multiple_of(k * ADD_ROWS, ADD_ROWS), ADD_ROWS)
                out_buf[slot, r, :] = (mine_buf[slot, r, :].astype(F32) + land_buf[slot, r, :].astype(F32)).astype(dt)
                return 0
            lax.fori_loop(0, rows // ADD_ROWS, add, 0)
            if s + SLOTS < ns:
                pl.semaphore_signal(credit.at[slot], inc=1, device_id=sibling, device_id_type=MESH)
            store(s).start()
            send(s).wait_send()
            if s + SLOTS < ns:
                for cp in loads(s + SLOTS):
                    cp.start()
        for s in range(max(ns - SLOTS, 0), ns):
            store(s).wait()

    buf = pltpu.VMEM((SLOTS, max_rows, W), dt)
    return pl.pallas_call(
        body, name=name,
        in_specs=[HBM_SPEC] * n, out_specs=HBM_SPEC,
        out_shape=SDS((N_CHIP, R, W), dt),
        scratch_shapes=[buf, buf, buf, buf, pltpu.SemaphoreType.DMA((SLOTS, 2)), pltpu.SemaphoreType.DMA((SLOTS,)),
                        pltpu.SemaphoreType.DMA((SLOTS,)), pltpu.SemaphoreType.DMA((SLOTS,)),
                        pltpu.SemaphoreType.REGULAR((SLOTS,))],
        compiler_params=pltpu.CompilerParams(vmem_limit_bytes=48 * MIB),
    )(*grads)


SMALL_NAMES = ("ffn1_norm", "mix_norm", "ffn2_norm", "final_norm", "lru_conv_w", "lru_conv_b", "lru_w_a", "lru_b_a",
               "lru_w_i", "lru_b_i", "lru_lambda", "sc_conv_w", "lru_out_norm", "sc_out_norm")
WEIGHT_NAMES = ("ffn1_norm", "ffn1_w_gate", "ffn1_w_up", "ffn1_w_down", "mix_norm", "w_in", "lru_conv_w", "lru_conv_b",
                "lru_w_a", "lru_b_a", "lru_w_i", "lru_b_i", "lru_lambda", "sc_conv_w", "lru_out_norm", "sc_out_norm",
                "w_out", "ffn2_norm", "ffn2_w_gate", "ffn2_w_up", "ffn2_w_down", "final_norm")
BIG = (("ffn1_w_gate", True), ("ffn1_w_up", True), ("ffn1_w_down", False), ("ffn2_w_gate", True), ("ffn2_w_up", True),
       ("ffn2_w_down", False), ("w_in", True), ("w_out", False))


SLAB_ROW_ALIGN = 256


def _pack_rows(parts, width):
    rows, counts = [], []
    for p in parts:
        flat = p.reshape(-1)
        nr = -(-flat.shape[0] // width)
        nr = -(-nr // SUBLANES) * SUBLANES
        rows.append(jnp.pad(flat, (0, nr * width - flat.shape[0])).reshape(nr, width))
        counts.append(nr)
    total = sum(counts)
    pad = -(-total // SLAB_ROW_ALIGN) * SLAB_ROW_ALIGN - total
    if pad:
        rows.append(jnp.zeros((pad, width), rows[0].dtype))
    return jnp.concatenate(rows, axis=0), counts


def _stack_rows(blocks):
    pieces, off = [], 0
    for b in blocks:
        pieces.append((off, b.shape[0]))
        off += b.shape[0]
    return jnp.concatenate(blocks, axis=0), pieces


def _unpack_rows(slab, counts, shapes):
    out, r = [], 0
    for nr, shape in zip(counts, shapes):
        size = math.prod(shape)
        out.append(slab[r:r + nr].reshape(-1)[:size].reshape(shape))
        r += nr
    return out


def kernel(x, ffn1_norm, ffn1_w_gate, ffn1_w_up, ffn1_w_down, mix_norm, w_in, lru_conv_w, lru_conv_b, lru_w_a, lru_b_a, lru_w_i, lru_b_i, lru_lambda, sc_conv_w, lru_out_norm, sc_out_norm, w_out, ffn2_norm, ffn2_w_gate, ffn2_w_up, ffn2_w_down, final_norm, loss_target, m_ffn1_norm, m_ffn1_w_gate, m_ffn1_w_up, m_ffn1_w_down, m_mix_norm, m_w_in, m_lru_conv_w, m_lru_conv_b, m_lru_w_a, m_lru_b_a, m_lru_w_i, m_lru_b_i, m_lru_lambda, m_sc_conv_w, m_lru_out_norm, m_sc_out_norm, m_w_out, m_ffn2_norm, m_ffn2_w_gate, m_ffn2_w_up, m_ffn2_w_down, m_final_norm, v_ffn1_norm, v_ffn1_w_gate, v_ffn1_w_up, v_ffn1_w_down, v_mix_norm, v_w_in, v_lru_conv_w, v_lru_conv_b, v_lru_w_a, v_lru_b_a, v_lru_w_i, v_lru_b_i, v_lru_lambda, v_sc_conv_w, v_lru_out_norm, v_sc_out_norm, v_w_out, v_ffn2_norm, v_ffn2_w_gate, v_ffn2_w_up, v_ffn2_w_down, v_final_norm):
    a = dict(locals())
    w = {n: a[n] for n in WEIGHT_NAMES}
    m = {n: a["m_" + n] for n in WEIGHT_NAMES}
    v = {n: a["v_" + n] for n in WEIGHT_NAMES}
    ax, ay, ac = _place()
    dev = 4 * ax + 2 * ay + ac
    chip = (2 * ax + ay).astype(jnp.int32).reshape(1)

    x0 = x[0]
    tgt = loss_target[0]
    T, D = x0.shape
    C = D // 2
    H, hd = lru_w_a.shape[1], lru_w_a.shape[2]
    CL = lru_conv_w.shape[2]

    shards = []
    for name, transposed in BIG:
        s = w[name][0]
        shards.append((s.T if transposed else s).astype(BF16))
    taps = jnp.concatenate([lru_conv_w[0], sc_conv_w[0], jnp.zeros((1, CL), F32)], axis=0)
    taps_row = lax.bitcast_convert_type(taps, BF16).reshape(1, -1)
    taps_blk = jnp.pad(taps_row, ((0, BF16_ROWS - 1), (0, D - taps_row.shape[1])))
    s_wg1, s_wu1, s_wd1, s_wg2, s_wu2, s_wd2, s_win, s_wout = shards
    slab_g1, pcs_g1 = _stack_rows([s_wg1])
    slab_u1, pcs_u1 = _stack_rows([s_wu1])
    slab_d1, pcs_d1 = _stack_rows([s_wd1])
    slab_mw, pcs_mw = _stack_rows([s_win, s_wout, taps_blk])
    slab_g2, pcs_g2 = _stack_rows([s_wg2])
    slab_ud2, pcs_ud2 = _stack_rows([s_wu2, s_wd2])
    (wg1,) = _allgather(slab_g1, pcs_g1, "allgather_ffn1_gate")

    g1, gm, g3 = ffn1_norm, mix_norm, ffn2_norm
    phase = _merge_phases(_ag_two_level_phase(slab_u1, pcs_u1), _ag_direct_phase(slab_d1, pcs_d1, SAME_CORE_AND_SIBLING))
    (n1, hg1), got = _norm_proj(x0, g1, [wg1], [BF16], False, "ffn1_gate", carried=phase)
    wu1, d1 = got[0], got[1:]
    phase = _merge_phases(_ag_forward_phase(d1, pcs_d1), _ag_direct_phase(slab_mw, pcs_mw, SAME_CORE_AND_SIBLING))
    (hu1, act1), got = _up_act(n1, wu1, hg1, "ffn1_up", carried=phase)
    wd1, mixw = got[0], got[1:]
    phase = _merge_phases(_ag_forward_phase(mixw, pcs_mw), _ag_direct_phase(slab_g2, pcs_g2, SAME_CORE_AND_SIBLING))
    x1, got = _mm_res(act1, wd1, x0, 0.5, "ffn1_down", carried=phase)
    (win, wout, taps_all), g2 = got[:3], got[3:]
    phase = _merge_phases(_ag_forward_phase(g2, pcs_g2), _ag_direct_phase(slab_ud2, pcs_ud2, SAME_CORE_AND_SIBLING))
    (n2, z), got = _norm_proj(x1, gm, [win], [F32], False, "in_proj", carried=phase)
    (wg2,), ud2 = got[:1], got[1:]
    taps_all = taps_all.reshape(N_DEV, BF16_ROWS, D)[:, 0, :2 * SUBLANES * CL].reshape(N_DEV, SUBLANES, CL, 2)
    taps_all = lax.bitcast_convert_type(taps_all, F32)
    taps_all = taps_all.transpose(1, 0, 2).reshape(SUBLANES, N_DEV * CL)
    cw, sw = taps_all[0:4], taps_all[4:7]

    gf = final_norm.reshape(1, D)
    cb = lru_conv_b
    wa, wi = lru_w_a[0].astype(BF16), lru_w_i[0].astype(BF16)
    ba, bi = lru_b_a.reshape(1, C), lru_b_i.reshape(1, C)
    lam, glo, gso = lru_lambda, lru_out_norm, sc_out_norm

    y, h, *saved = _mix_fwd(z, cw, cb, wa, ba, wi, bi, lam, sw, glo, gso, "mix_fwd")
    x2, (wu2, wd2) = _mm_res(y, wout, x1, 1.0, "out_proj", carried=_ag_forward_phase(ud2, pcs_ud2))
    n3, hg2, hu2, act2 = _norm_proj(x2, g3, [wg2, wu2], [BF16, BF16], True, "ffn2_up")
    x3 = _mm_res(act2, wd2, x2, 0.5, "ffn2_down")
    dx3, df2, d_gf, loss_blk = _loss_head(x3, gf, tgt, "loss_head")

    F = wd1.shape[0]
    bm_f = F // 4 if (F // 4) % LANES == 0 else 512

    def reduce_group(gs, tag):
        pcs, off = [], 0
        for g_ in gs:
            pcs.append((off, g_.shape[0] // N_DEV))
            off += g_.shape[0] // N_DEV
        sb_ = _rs_sibling(gs, pcs, "rs_sibling_add_" + tag)
        return sb_, pcs

    dhg2, dhu2 = _ffn_bwd_act(df2, wd2, hg2, hu2, "ffn2_bwd_act")
    d_wd2 = _dw_tn(act2, df2, bm_f, "ffn2_dw_down")
    d_wg2 = _dw_tn(dhg2, n3, bm_f, "ffn2_dw_gate")
    d_wu2 = _dw_tn(dhu2, n3, bm_f, "ffn2_dw_up")
    sb_f2, pcs_f2 = reduce_group([d_wg2, d_wu2, d_wd2], "ffn2")
    (dx2, dx2b, d_g3), (lb_f2,) = _mm_rmsbwd([(dhg2, wg2), (dhu2, wu2)], x2, g3, dx3, 1.0, "ffn2_bwd_in",
                                             carried=_rs_chips_phase(sb_f2))
    dy = _mm_nt(dx2b, wout, "out_proj_bwd")
    d_wout = _dw_tn(y, dx2b, 1024, "out_proj_dw")
    dz, small, d_wa, d_wi = _mix_bwd(z, h, saved, dy, cw, wa, wi, lam, sw, glo, gso, "mix_bwd")
    d_win = _dw_tn(dz, n2, 1280, "in_proj_dw")
    sb_mx, pcs_mx = reduce_group([d_win, d_wout], "mix")
    (dx1, df1, d_gm), (lb_mx,) = _mm_rmsbwd([(dz, win)], x1, gm, dx2, 0.5, "in_proj_bwd",
                                            carried=_rs_chips_phase(sb_mx))
    dhg1, dhu1 = _ffn_bwd_act(df1, wd1, hg1, hu1, "ffn1_bwd_act")
    d_wd1 = _dw_tn(act1, df1, bm_f, "ffn1_dw_down")
    d_wg1 = _dw_tn(dhg1, n1, bm_f, "ffn1_dw_gate")
    d_wu1 = _dw_tn(dhu1, n1, bm_f, "ffn1_dw_up")
    sb_f1, pcs_f1 = reduce_group([d_wg1, d_wu1, d_wd1], "ffn1")
    early_names = [n_ for n_ in SMALL_NAMES if n_ != "ffn1_norm"]
    early_parts = dict(zip(SMALL_NAMES, [None, d_gm, d_g3, d_gf, small[R_CW:R_CW + 4], small[R_CB], d_wa, small[R_BA], d_wi,
                                         small[R_BI], small[R_LAM], small[R_SW:R_SW + 3], small[R_GLO], small[R_GSO]]))
    early_slab, early_counts = _pack_rows([early_parts[n_] for n_ in early_names], LANES)
    RE = early_slab.shape[0]
    phase = _merge_phases(_ag_two_level_phase(early_slab, [(0, RE)]), _rs_chips_phase(sb_f1))
    (dx0, _, d_g1), (early_all, lb_f1) = _mm_rmsbwd([(dhg1, wg1), (dhu1, wu1)], x0, g1, dx1, 1.0, "ffn1_bwd_in",
                                                    carried=phase)

    grads, delta, new_m, new_v = {}, {}, {}, {}
    transposed_shard = dict(BIG)
    for names, sb_, lb_, pcs in ((("ffn2_w_gate", "ffn2_w_up", "ffn2_w_down"), sb_f2, lb_f2, pcs_f2),
                                 (("w_in", "w_out"), sb_mx, lb_mx, pcs_mx),
                                 (("ffn1_w_gate", "ffn1_w_up", "ffn1_w_down"), sb_f1, lb_f1, pcs_f1)):
        for name, (off, rows) in zip(names, pcs):
            flip = transposed_shard[name] and w[name].shape[2] % LANES != 0
            if transposed_shard[name] and not flip:
                g_ = _final_grad(sb_, lb_, chip, off, rows, "rs_final_sum_" + name).T
                d_, m_, v_ = _adamw(w[name][0], g_, m[name][0], v[name][0], "adamw_" + name)
            else:
                view = (lambda t: t[0].T) if flip else (lambda t: t[0])
                g_, d_, m_, v_ = _sum_adamw(sb_, lb_, chip, off, rows, view(w[name]), view(m[name]), view(v[name]),
                                            "sum_adamw_" + name)
            back = (lambda t: t.T[None]) if flip else (lambda t: t[None])
            grads[name], delta[name], new_m[name], new_v[name] = back(g_), back(d_), back(m_), back(v_)

    shape_of = dict(zip(SMALL_NAMES, [(1, D), (1, D), (1, D), (D,), (1, 4, C), (1, C), (1, H, hd, hd), (1, H, hd),
                                      (1, H, hd, hd), (1, H, hd), (1, C), (1, 3, C), (1, C), (1, C)]))
    early_sum = _add_slabs([early_all[j * RE:(j + 1) * RE] for j in range(N_DEV)], F32, "small_grads_sum")
    small_full = dict(zip(early_names, _unpack_rows(early_sum, early_counts, [shape_of[n_] for n_ in early_names])))
    late_slab, late_counts = _pack_rows([d_g1], LANES)
    RL = late_slab.shape[0]
    (late_all,) = _allgather(late_slab, [(0, RL)], "allgather_ffn1_norm_grad")
    late_sum = _add_slabs([late_all[j * RL:(j + 1) * RL] for j in range(N_DEV)], F32, "ffn1_norm_grad_sum")
    (small_full["ffn1_norm"],) = _unpack_rows(late_sum, late_counts, [shape_of["ffn1_norm"]])

    for name in SMALL_NAMES:
        gfull = small_full[name]
        if name in ("lru_conv_w", "sc_conv_w"):
            gfull = lax.dynamic_slice_in_dim(gfull, dev * CL, CL, axis=2)
        grads[name] = gfull

    packs = [_pack_rows([t[n_] for n_ in SMALL_NAMES], LANES) for t in (w, grads, m, v)]
    sd, sm, sv = _adamw(packs[0][0], packs[1][0], packs[2][0], packs[3][0], "adamw_small")
    shapes = [w[n_].shape for n_ in SMALL_NAMES]
    for tgt_dict, slab_ in ((delta, sd), (new_m, sm), (new_v, sv)):
        for n_, val in zip(SMALL_NAMES, _unpack_rows(slab_, packs[0][1], shapes)):
            tgt_dict[n_] = val

    loss = lax.psum(loss_blk[0, 0], ("x", "y", "c"))
    return (loss, dx0[None], *[grads[n_] for n_ in WEIGHT_NAMES], *[delta[n_] for n_ in WEIGHT_NAMES],
            *[new_m[n_] for n_ in WEIGHT_NAMES], *[new_v[n_] for n_ in WEIGHT_NAMES])
```

```python
import functools
import math

import jax
import jax.numpy as jnp
from jax import lax
from jax.experimental import pallas as pl
from jax.experimental.pallas import tpu as pltpu

F32 = jnp.float32
BF16 = jnp.bfloat16
SDS = jax.ShapeDtypeStruct
MESH = pl.DeviceIdType.MESH

NORM_EPS = 1e-6
LRU_C = 8.0
N_DEV = 8
N_CHIP = 4
ADAM_LR, ADAM_B1, ADAM_B2, ADAM_EPS, ADAM_WD, ADAM_STEP = 0.001, 0.9, 0.999, 1e-08, 0.01, 10

NN = (((1,), (0,)), ((), ()))
NT = (((1,), (1,)), ((), ()))
TN = (((0,), (0,)), ((), ()))

SUBLANES = 8
BF16_ROWS = 16
LANES = 128
MIB = 1 << 20


def _dot(a, b, dims):
    return lax.dot_general(a, b, dims, preferred_element_type=F32)


def _blk(n, pref, align):
    if n <= pref:
        return n
    b = (pref // align) * align
    while b >= align:
        if n % b == 0:
            return b
        b -= align
    raise ValueError(f"no block of {n} aligned to {align} under {pref}")


def _cp(sem, vmem_mib):
    return pltpu.CompilerParams(dimension_semantics=sem, vmem_limit_bytes=vmem_mib * MIB)


HBM_SPEC = pl.BlockSpec(memory_space=pltpu.HBM)
MID_EIGHTHS = 5

class _Carried:
    def __init__(self, inputs, out_shapes, aliases, sem_shapes, build, has_mid=False):
        self.inputs, self.out_shapes, self.aliases = list(inputs), list(out_shapes), dict(aliases)
        self.sem_shapes, self.build, self.has_mid = list(sem_shapes), build, has_mid


def _call(body, *, name, grid, in_specs, out_specs, out_shape, scratch_shapes, compiler_params, args, carried=None):
    if carried is None:
        return pl.pallas_call(body, name=name, grid=grid, in_specs=in_specs, out_specs=out_specs, out_shape=out_shape,
                              scratch_shapes=scratch_shapes, compiler_params=compiler_params)(*args)
    n_in, n_out, n_sc = len(in_specs), len(out_shape), len(scratch_shapes)
    c_in, c_out = len(carried.inputs), len(carried.out_shapes)

    def hosted(*refs):
        ins, refs = refs[:n_in], refs[n_in:]
        c_ins, refs = refs[:c_in], refs[c_in:]
        outs, refs = refs[:n_out], refs[n_out:]
        c_outs, refs = refs[:c_out], refs[c_out:]
        scratch, c_sems = refs[:n_sc], refs[n_sc:]
        first = functools.reduce(jnp.logical_and, [pl.program_id(a) == 0 for a in range(len(grid))])
        last = functools.reduce(jnp.logical_and, [pl.program_id(a) == g - 1 for a, g in enumerate(grid)])

        @pl.when(first)
        def _():
            for start in carried.build(c_ins, c_outs, c_sems, "start"):
                start()

        if carried.has_mid:
            mid = functools.reduce(jnp.logical_and, [pl.program_id(0) == (grid[0] * MID_EIGHTHS) // 8]
                                   + [pl.program_id(a) == 0 for a in range(1, len(grid))])

            @pl.when(mid)
            def _():
                for step in carried.build(c_ins, c_outs, c_sems, "mid"):
                    step()

        body(*ins, *outs, *scratch)

        @pl.when(last)
        def _():
            for wait in carried.build(c_ins, c_outs, c_sems, "end"):
                wait()

    out = pl.pallas_call(
        hosted, name=name, grid=grid, in_specs=list(in_specs) + [HBM_SPEC] * c_in,
        out_specs=list(out_specs) + [HBM_SPEC] * c_out, out_shape=list(out_shape) + carried.out_shapes,
        scratch_shapes=list(scratch_shapes) + carried.sem_shapes,
        input_output_aliases={n_in + a: n_out + b for a, b in carried.aliases.items()},
        compiler_params=compiler_params)(*args, *carried.inputs)
    return out[:n_out], out[n_out:]


ROW_CHUNK = 128


def _chunk_rows(c):
    return pl.ds(pl.multiple_of(c * ROW_CHUNK, ROW_CHUNK), ROW_CHUNK)


def _rstd(xv):
    return lax.rsqrt(jnp.mean(xv * xv, axis=-1, keepdims=True) + NORM_EPS)


def _rms_bwd(xv, g, dn):
    r = _rstd(xv)
    xr = xv * r
    gd = g * dn
    dx = r * (gd - xr * jnp.mean(gd * xr, axis=-1, keepdims=True))
    return dx, jnp.sum(dn * xr, axis=0, keepdims=True)


def _log1p(e):
    u = 1.0 + e
    return jnp.where(u == 1.0, e, jnp.log(u) * (e / (u - 1.0)))


def _one_minus_exp(v, exp_half_v):
    series = 1.0 / 5040.0
    for coeff in (1.0 / 720.0, 1.0 / 120.0, 1.0 / 24.0, 1.0 / 6.0, 0.5, 1.0):
        series = series * v + coeff
    return jnp.where(v > -0.5, -v * series, 1.0 - exp_half_v * exp_half_v)


def _sigmoid(v):
    return 0.5 * jnp.tanh(0.5 * v) + 0.5


def _gelu_parts(g):
    k0 = math.sqrt(2.0 / math.pi)
    g2 = g * g
    t = jnp.tanh(k0 * (g + 0.044715 * g * g2))
    gel = 0.5 * g * (1.0 + t)
    gelp = 0.5 * (1.0 + t) + 0.5 * g * (1.0 - t * t) * (k0 * (1.0 + 3.0 * 0.044715 * g2))
    return gel, gelp


def _norm_proj(x, gain, w_list, out_dtypes, swiglu, name, carried=None):
    T, D = x.shape
    N = w_list[0].shape[0]
    nw = len(w_list)
    bm = _blk(T, 1024, BF16_ROWS)
    bn = _blk(N, 1024 // nw, LANES)

    def body(*refs):
        x_ref, g_ref = refs[:2]
        w_refs = refs[2:2 + nw]
        n_ref = refs[2 + nw]
        o_refs = refs[3 + nw:3 + 2 * nw]
        act_ref = refs[3 + 2 * nw] if swiglu else None
        n_sc = refs[-1]

        @pl.when(pl.program_id(1) == 0)
        def _():
            def chunk(c, _):
                r = _chunk_rows(c)
                xv = x_ref[r, :]
                nb = (xv * _rstd(xv) * g_ref[...]).astype(BF16)
                n_sc[r, :] = nb
                n_ref[r, :] = nb
                return 0
            lax.fori_loop(0, bm // ROW_CHUNK, chunk, 0)

        n = n_sc[...]
        outs = [_dot(n, w_ref[...], NT) for w_ref in w_refs]
        for o_ref, o in zip(o_refs, outs):
            o_ref[...] = o.astype(o_ref.dtype)
        if swiglu:
            hg, hu = outs
            act_ref[...] = (hg * _sigmoid(hg) * hu).astype(BF16)

    row = pl.BlockSpec((bm, D), lambda i, j: (i, 0))
    tile = pl.BlockSpec((bm, bn), lambda i, j: (i, j))
    n_extra = 1 if swiglu else 0
    return _call(
        body, name=name, grid=(T // bm, N // bn),
        in_specs=[row, pl.BlockSpec((1, D), lambda i, j: (0, 0))] + [pl.BlockSpec((bn, D), lambda i, j: (j, 0))] * nw,
        out_specs=[row] + [tile] * (nw + n_extra),
        out_shape=[SDS((T, D), BF16)] + [SDS((T, N), dt) for dt in out_dtypes] + [SDS((T, N), BF16)] * n_extra,
        scratch_shapes=[pltpu.VMEM((bm, D), BF16)],
        compiler_params=_cp(("arbitrary", "arbitrary"), 52),
        args=(x, gain, *w_list), carried=carried)


def _up_act(n, wu, hg, name, carried=None):
    T, D = n.shape
    F = wu.shape[0]
    bm = _blk(T, 1024, BF16_ROWS)
    bn = _blk(F, 512, LANES)

    def body(n_ref, wu_ref, hg_ref, hu_ref, act_ref):
        hu = _dot(n_ref[...], wu_ref[...], NT)
        hg = hg_ref[...].astype(F32)
        hu_ref[...] = hu.astype(BF16)
        act_ref[...] = (hg * _sigmoid(hg) * hu).astype(BF16)

    tile = pl.BlockSpec((bm, bn), lambda i, j: (i, j))
    return _call(
        body, name=name, grid=(T // bm, F // bn),
        in_specs=[pl.BlockSpec((bm, D), lambda i, j: (i, 0)), pl.BlockSpec((bn, D), lambda i, j: (j, 0)), tile],
        out_specs=[tile, tile], out_shape=[SDS((T, F), BF16)] * 2, scratch_shapes=[],
        compiler_params=_cp(("arbitrary", "arbitrary"), 40),
        args=(n, wu, hg), carried=carried)


def _mm_res(a, b, x, scale, name, carried=None):
    T, K = a.shape
    D = b.shape[1]
    bm = _blk(T, 1024, BF16_ROWS)
    bk = _blk(K, 1408, LANES)
    nk = K // bk

    def body(a_ref, b_ref, x_ref, o_ref):
        k = pl.program_id(1)

        @pl.when(k == 0)
        def _():
            o_ref[...] = jnp.zeros_like(o_ref)

        o_ref[...] += _dot(a_ref[...], b_ref[...], NN)

        @pl.when(k == nk - 1)
        def _():
            def chunk(c, _):
                r = _chunk_rows(c)
                o_ref[r, :] = x_ref[r, :] + scale * o_ref[r, :]
                return 0
            lax.fori_loop(0, bm // ROW_CHUNK, chunk, 0)

    row = pl.BlockSpec((bm, D), lambda i, k: (i, 0))
    out = _call(
        body, name=name, grid=(T // bm, nk),
        in_specs=[pl.BlockSpec((bm, bk), lambda i, k: (i, k)), pl.BlockSpec((bk, D), lambda i, k: (k, 0)), row],
        out_specs=[row], out_shape=[SDS((T, D), F32)], scratch_shapes=[],
        compiler_params=_cp(("arbitrary", "arbitrary"), 56),
        args=(a, b, x), carried=carried)
    return out[0] if carried is None else (out[0][0], out[1])


def _mm_nt(a, b, name):
    T, K = a.shape
    N = b.shape[0]
    bm = _blk(T, 1024, BF16_ROWS)
    bn = _blk(N, 512, LANES)

    def body(a_ref, b_ref, o_ref):
        o_ref[...] = _dot(a_ref[...], b_ref[...], NT)

    return pl.pallas_call(
        body, name=name, grid=(T // bm, N // bn),
        in_specs=[pl.BlockSpec((bm, K), lambda i, j: (i, 0)), pl.BlockSpec((bn, K), lambda i, j: (j, 0))],
        out_specs=pl.BlockSpec((bm, bn), lambda i, j: (i, j)), out_shape=SDS((T, N), F32),
        compiler_params=_cp(("arbitrary", "arbitrary"), 40),
    )(a, b)


def _ffn_bwd_act(dfb, wd, hg, hu, name):
    T, D = dfb.shape
    F = wd.shape[0]
    bm = _blk(T, 1024, BF16_ROWS)
    bn = _blk(F, 512, LANES)

    def body(df_ref, wd_ref, hg_ref, hu_ref, dhg_ref, dhu_ref):
        dact = _dot(df_ref[...], wd_ref[...], NT)
        hgv = hg_ref[...].astype(F32)
        huv = hu_ref[...].astype(F32)
        s = _sigmoid(hgv)
        dhu_ref[...] = (dact * (hgv * s)).astype(BF16)
        dhg_ref[...] = (dact * huv * (s * (1.0 + hgv * (1.0 - s)))).astype(BF16)

    tile = pl.BlockSpec((bm, bn), lambda i, j: (i, j))
    return pl.pallas_call(
        body, name=name, grid=(T // bm, F // bn),
        in_specs=[pl.BlockSpec((bm, D), lambda i, j: (i, 0)), pl.BlockSpec((bn, D), lambda i, j: (j, 0)), tile, tile],
        out_specs=[tile, tile], out_shape=[SDS((T, F), BF16)] * 2,
        compiler_params=_cp(("arbitrary", "arbitrary"), 40),
    )(dfb, wd, hg, hu)


def _dw_tn(a, b, bm_pref, name):
    T, M = a.shape
    N = b.shape[1]
    bm = _blk(M, bm_pref, LANES)
    tk = _blk(T, 1024, BF16_ROWS)
    nk = T // tk

    def body(a_ref, b_ref, o_ref, acc):
        k = pl.program_id(1)

        @pl.when(k == 0)
        def _():
            acc[...] = jnp.zeros_like(acc)

        acc[...] += _dot(a_ref[...], b_ref[...], TN)

        @pl.when(k == nk - 1)
        def _():
            o_ref[...] = acc[...].astype(BF16)

    return pl.pallas_call(
        body, name=name, grid=(M // bm, nk),
        in_specs=[pl.BlockSpec((tk, bm), lambda i, k: (k, i)), pl.BlockSpec((tk, N), lambda i, k: (k, 0))],
        out_specs=pl.BlockSpec((bm, N), lambda i, k: (i, 0)), out_shape=SDS((M, N), BF16),
        scratch_shapes=[pltpu.VMEM((bm, N), F32)],
        compiler_params=_cp(("arbitrary", "arbitrary"), 48),
    )(a, b)


def _mm_rmsbwd(pairs, x, gain, dx_in, bscale, name, carried=None):
    T, D = x.shape
    K = pairs[0][0].shape[1]
    npair = len(pairs)
    bm = _blk(T, 1024, BF16_ROWS)
    bk = _blk(K, 1024 // npair, LANES)
    nk = K // bk

    nchunk = bm // ROW_CHUNK

    def body(*refs):
        ab = refs[:2 * npair]
        x_hbm, g_ref, dxin_hbm, dx_ref, dxb_ref, dg_ref, x_buf, dxin_buf, sems = refs[2 * npair:]
        i = pl.program_id(0)
        k = pl.program_id(1)

        def fetch(c, slot):
            rows = pl.ds(i * bm + c * ROW_CHUNK, ROW_CHUNK)
            return (pltpu.make_async_copy(x_hbm.at[rows, :], x_buf.at[slot], sems.at[slot, 0]),
                    pltpu.make_async_copy(dxin_hbm.at[rows, :], dxin_buf.at[slot], sems.at[slot, 1]))

        @pl.when(k == 0)
        def _():
            dx_ref[...] = jnp.zeros_like(dx_ref)

        @pl.when(k == nk - 1)
        def _():
            for cp in fetch(0, 0):
                cp.start()

        for q in range(npair):
            dx_ref[...] += _dot(ab[2 * q][...], ab[2 * q + 1][...], NN)

        @pl.when(k == nk - 1)
        def _():
            @pl.when(i == 0)
            def _():
                dg_ref[...] = jnp.zeros_like(dg_ref)

            def chunk(c, _):
                slot = c % 2

                @pl.when(c + 1 < nchunk)
                def _():
                    for cp in fetch(c + 1, 1 - slot):
                        cp.start()

                for cp in fetch(c, slot):
                    cp.wait()

                for half in range(2):
                    hr = ROW_CHUNK // 2
                    rb = pl.ds(half * hr, hr)
                    r = pl.ds(pl.multiple_of(c * ROW_CHUNK + half * hr, hr), hr)
                    dx, dg = _rms_bwd(x_buf[slot, rb, :], g_ref[...], dx_ref[r, :])
                    dxo = dxin_buf[slot, rb, :] + dx
                    dx_ref[r, :] = dxo
                    dxb_ref[r, :] = (bscale * dxo).astype(BF16)
                    dg_ref[...] += dg
                return 0
            lax.fori_loop(0, nchunk, chunk, 0)

    row = pl.BlockSpec((bm, D), lambda i, k: (i, 0))
    anywhere = pl.BlockSpec(memory_space=pl.ANY)
    vec = pl.BlockSpec((1, D), lambda i, k: (0, 0))
    in_specs = []
    args = []
    for a, b in pairs:
        in_specs += [pl.BlockSpec((bm, bk), lambda i, k: (i, k)), pl.BlockSpec((bk, D), lambda i, k: (k, 0))]
        args += [a, b]
    return _call(
        body, name=name, grid=(T // bm, nk),
        in_specs=in_specs + [anywhere, vec, anywhere], out_specs=[row, row, vec],
        out_shape=[SDS((T, D), F32), SDS((T, D), BF16), SDS((1, D), F32)],
        scratch_shapes=[pltpu.VMEM((2, ROW_CHUNK, D), F32), pltpu.VMEM((2, ROW_CHUNK, D), F32),
                        pltpu.SemaphoreType.DMA((2, 2))],
        compiler_params=_cp(("arbitrary", "arbitrary"), 52),
        args=(*args, x, gain, dx_in), carried=carried)


def _loss_head(x3, gain, tgt, name):
    T, D = x3.shape
    bm = _blk(T, 256, BF16_ROWS)

    def body(x_ref, g_ref, t_ref, dx_ref, dxb_ref, dg_ref, loss_ref):
        i = pl.program_id(0)
        xv = x_ref[...]
        g = g_ref[...]
        out = xv * _rstd(xv) * g
        e = out - t_ref[...]
        part = 0.5 * jnp.sum(jnp.mean(e * e, axis=-1, keepdims=True), axis=0, keepdims=True)
        dx, dg = _rms_bwd(xv, g, e * (1.0 / D))
        dx_ref[...] = dx
        dxb_ref[...] = (0.5 * dx).astype(BF16)

        @pl.when(i == 0)
        def _():
            dg_ref[...] = dg
            loss_ref[...] = jnp.broadcast_to(part, loss_ref.shape)

        @pl.when(i > 0)
        def _():
            dg_ref[...] += dg
            loss_ref[...] += jnp.broadcast_to(part, loss_ref.shape)

    row = pl.BlockSpec((bm, D), lambda i: (i, 0))
    vec = pl.BlockSpec((1, D), lambda i: (0, 0))
    return pl.pallas_call(
        body, name=name, grid=(T // bm,),
        in_specs=[row, vec, row], out_specs=[row, row, vec, pl.BlockSpec((SUBLANES, LANES), lambda i: (0, 0))],
        out_shape=[SDS((T, D), F32), SDS((T, D), BF16), SDS((1, D), F32), SDS((SUBLANES, LANES), F32)],
        compiler_params=_cp(("arbitrary",), 40),
    )(x3, gain, tgt)


R_CW, R_CB, R_BA, R_BI, R_LAM, R_SW, R_GLO, R_GSO, SMALL_ROWS = 0, 4, 5, 6, 7, 8, 11, 12, 16


def _rows(g):
    return pl.ds(pl.multiple_of(g * SUBLANES, SUBLANES), SUBLANES)


def _shift_back(prev, cur, d):
    row = lax.broadcasted_iota(jnp.int32, cur.shape, 0)
    return pltpu.roll(jnp.where(row >= SUBLANES - d, prev, cur), d, 0)


def _shift_fwd(cur, nxt, d):
    row = lax.broadcasted_iota(jnp.int32, cur.shape, 0)
    return pltpu.roll(jnp.where(row < d, nxt, cur), SUBLANES - d, 0)


def _causal_conv(ext, g, taps_ref, ntap):
    prev = ext[_rows(g), :]
    cur = ext[_rows(g + 1), :]
    out = _shift_back(prev, cur, ntap - 1) * taps_ref[0:1, :]
    for k in range(1, ntap - 1):
        out = out + _shift_back(prev, cur, ntap - 1 - k) * taps_ref[k:k + 1, :]
    return out + cur * taps_ref[ntap - 1:ntap, :]


def _scan8(A, U, reverse):
    row = lax.broadcasted_iota(jnp.int32, A.shape, 0)
    for s in (1, 2, 4):
        if reverse:
            A_sh = pltpu.roll(A, SUBLANES - s, 0)
            U_sh = pltpu.roll(U, SUBLANES - s, 0)
            m = row < SUBLANES - s
        else:
            A_sh = pltpu.roll(A, s, 0)
            U_sh = pltpu.roll(U, s, 0)
            m = row >= s
        U = jnp.where(m, A * U_sh + U, U)
        A = jnp.where(m, A * A_sh, A)
    return A, U


def _gate_pre(xc_s, w_ref, out_s, H, hd):
    for h in range(H):
        cs = slice(h * hd, (h + 1) * hd)
        out_s[:, cs] = _dot(xc_s[:, cs].astype(BF16), w_ref[h], NN)


def _lru_coeffs(pa, pi, xc, ba, bi, sp):
    ra = _sigmoid(pa + ba)
    ri = _sigmoid(pi + bi)
    log_a = (-LRU_C * ra) * sp
    a = jnp.exp(log_a)
    mult = jnp.sqrt(_one_minus_exp(2.0 * log_a, a))
    return ra, ri, a, mult


def _softplus_neg(lam):
    v = -lam
    return jnp.maximum(v, 0.0) + _log1p(jnp.exp(-jnp.abs(v)))


def _mix_fwd(z, cw, cb, wa, ba, wi, bi, lam, sw, glo, gso, name):
    T = z.shape[0]
    C = z.shape[1] // 5
    H = wa.shape[0]
    hd = C // H
    tb = _blk(T, 256, BF16_ROWS)
    ng = tb // SUBLANES
    HDR = SUBLANES

    def body(z_ref, cw_ref, cb_ref, wa_ref, ba_ref, wi_ref, bi_ref, lam_ref, sw_ref, glo_ref, gso_ref,
             y_ref, h_ref, ra_ref, ri_ref, a_ref, m_ref, xc_s, q_ref, gel_ref, gelp_ref,
             xext, pext, pa_s, pi_s, y_s, hcar):
        @pl.when(pl.program_id(0) == 0)
        def _():
            xext[0:HDR, :] = jnp.zeros((HDR, C), F32)
            pext[0:HDR, :] = jnp.zeros((HDR, C), F32)
            hcar[...] = jnp.zeros_like(hcar)

        def fill(g, _):
            r = _rows(g)
            re = _rows(g + 1)
            xext[re, :] = z_ref[r, 0:C]
            pext[re, :] = z_ref[r, 3 * C:4 * C] * z_ref[r, 4 * C:5 * C]
            return 0
        lax.fori_loop(0, ng, fill, 0)

        def conv(g, _):
            xc_s[_rows(g), :] = _causal_conv(xext, g, cw_ref, 4) + cb_ref[...]
            return 0
        lax.fori_loop(0, ng, conv, 0)

        _gate_pre(xc_s, wa_ref, pa_s, H, hd)
        _gate_pre(xc_s, wi_ref, pi_s, H, hd)
        sp = _softplus_neg(lam_ref[...])

        def group(g, hprev):
            r = _rows(g)
            xc = xc_s[r, :]
            ra, ri, a, mult = _lru_coeffs(pa_s[r, :], pi_s[r, :], xc, ba_ref[...], bi_ref[...], sp)
            ra_ref[r, :] = ra
            ri_ref[r, :] = ri
            a_ref[r, :] = a
            m_ref[r, :] = mult
            A, U = _scan8(a, mult * (ri * xc), reverse=False)
            hh = A * hprev + U
            h_ref[r, :] = hh
            gel, gelp = _gelu_parts(z_ref[r, C:2 * C])
            gel_ref[r, :] = gel
            gelp_ref[r, :] = gelp
            y_lru = hh * gel
            y_s[r, 0:C] = y_lru * _rstd(y_lru) * glo_ref[...]
            q = _causal_conv(pext, g, sw_ref, 3)
            q_ref[r, :] = q
            y_sc = z_ref[r, 2 * C:3 * C] * q
            y_s[r, C:2 * C] = y_sc * _rstd(y_sc) * gso_ref[...]
            return jnp.broadcast_to(hh[SUBLANES - 1:SUBLANES, :], hh.shape)
        hcar[...] = lax.fori_loop(0, ng // 2, lambda t, hp: group(2 * t + 1, group(2 * t, hp)), hcar[...])

        xext[0:HDR, :] = xext[tb:tb + HDR, :]
        pext[0:HDR, :] = pext[tb:tb + HDR, :]

        def cast(g, _):
            r = pl.ds(pl.multiple_of(g * BF16_ROWS, BF16_ROWS), BF16_ROWS)
            y_ref[r, :] = y_s[r, :].astype(BF16)
            return 0
        lax.fori_loop(0, tb // BF16_ROWS, cast, 0)

    full = lambda shape: pl.BlockSpec(shape, lambda i: (0,) * len(shape))
    blk = lambda w: pl.BlockSpec((tb, w), lambda i: (i, 0))
    ext = pltpu.VMEM((tb + HDR, C), F32)
    tile = pltpu.VMEM((tb, C), F32)
    return pl.pallas_call(
        body, name=name, grid=(T // tb,),
        in_specs=[blk(5 * C), full((4, C)), full((1, C)), full((H, hd, hd)), full((1, C)), full((H, hd, hd)),
                  full((1, C)), full((1, C)), full((3, C)), full((1, C)), full((1, C))],
        out_specs=[blk(2 * C)] + [blk(C)] * 9,
        out_shape=[SDS((T, 2 * C), BF16)] + [SDS((T, C), F32)] * 9,
        scratch_shapes=[ext, ext, tile, tile, pltpu.VMEM((tb, 2 * C), F32), pltpu.VMEM((SUBLANES, C), F32)],
        compiler_params=_cp(("arbitrary",), 48),
    )(z, cw, cb, wa, ba, wi, bi, lam, sw, glo, gso)


def _mix_bwd(z, h, saved, dy, cw, wa, wi, lam, sw, glo, gso, name):
    T = z.shape[0]
    C = z.shape[1] // 5
    H = wa.shape[0]
    hd = C // H
    tb = _blk(T, 256, BF16_ROWS)
    nb = T // tb
    ng = tb // SUBLANES
    HDR = SUBLANES
    N_ACC = 13

    def body(z_ref, zp_ref, h_ref, hp_ref, ra_ref, ri_ref, a_ref, m_ref, xc_s, q_ref, gel_ref, gelp_ref, dy_ref, cw_ref,
             wa_ref, wi_ref,
             lam_ref, sw_ref, glo_ref, gso_ref, dz_ref, small_ref, dwa_ref, dwi_ref,
             xext, pext, hext, dqext, dxcext, bext, dh_s, dpa_s, dpi_s, dz_s, acc_s, bcar):
        i = pl.program_id(0)
        first_rows = i == nb - 1

        @pl.when(i == 0)
        def _():
            dqext[tb:tb + HDR, :] = jnp.zeros((HDR, C), F32)
            dxcext[tb:tb + HDR, :] = jnp.zeros((HDR, C), F32)
            bcar[...] = jnp.zeros_like(bcar)
            acc_s[...] = jnp.zeros_like(acc_s)
            dwa_ref[...] = jnp.zeros_like(dwa_ref)
            dwi_ref[...] = jnp.zeros_like(dwi_ref)

        zero = jnp.zeros((HDR, C), F32)
        xext[0:HDR, :] = jnp.where(first_rows, zero, zp_ref[:, 0:C])
        pext[0:HDR, :] = jnp.where(first_rows, zero, zp_ref[:, 3 * C:4 * C] * zp_ref[:, 4 * C:5 * C])
        hext[0:HDR, :] = jnp.where(first_rows, zero, hp_ref[...])

        def fill(g, _):
            r = _rows(g)
            re = _rows(g + 1)
            xext[re, :] = z_ref[r, 0:C]
            pext[re, :] = z_ref[r, 3 * C:4 * C] * z_ref[r, 4 * C:5 * C]
            hext[re, :] = h_ref[r, :]
            return 0
        lax.fori_loop(0, ng, fill, 0)

        sp = _softplus_neg(lam_ref[...])
        dsp_dlam = -jax.nn.sigmoid(-lam_ref[...])

        def add_acc(k, v):
            acc_s[k] += v

        def p1(g, _):
            r = _rows(g)
            hh = h_ref[r, :]
            gel = gel_ref[r, :]
            gelp = gelp_ref[r, :]
            y_lru = hh * gel
            dnl = dy_ref[r, 0:C]
            rl = _rstd(y_lru)
            ylr = y_lru * rl
            gd = glo_ref[...] * dnl
            dy_lru = rl * (gd - ylr * jnp.mean(gd * ylr, axis=-1, keepdims=True))
            add_acc(R_GLO, dnl * ylr)
            dz_s[r, C:2 * C] = dy_lru * hh * gelp
            dh = dy_lru * gel
            dh_s[r, :] = dh

            q = q_ref[r, :]
            scb = z_ref[r, 2 * C:3 * C]
            y_sc = scb * q
            dns = dy_ref[r, C:2 * C]
            rs = _rstd(y_sc)
            ysr = y_sc * rs
            gs = gso_ref[...] * dns
            dy_sc = rs * (gs - ysr * jnp.mean(gs * ysr, axis=-1, keepdims=True))
            add_acc(R_GSO, dns * ysr)
            dz_s[r, 2 * C:3 * C] = dy_sc * q
            dqext[r, :] = dy_sc * scb
            return 0
        lax.fori_loop(0, ng, p1, 0, unroll=2)

        bext[tb:tb + HDR, :] = bcar[...]

        def p2(j, carry):
            g = ng - 1 - j
            r = _rows(g)
            a = a_ref[r, :]
            A, U = _scan8(a, a * dh_s[r, :], reverse=True)
            bb = A * carry + U
            bext[r, :] = bb
            return jnp.broadcast_to(bb[0:1, :], bb.shape)
        bcar[...] = lax.fori_loop(0, ng, p2, bcar[...])

        def p3(g, _):
            r = _rows(g)
            rn = _rows(g + 1)
            G = dh_s[r, :] + _shift_fwd(bext[r, :], bext[rn, :], 1)
            hm1 = _shift_back(hext[r, :], hext[rn, :], 1)
            a = a_ref[r, :]
            mult = m_ref[r, :]
            ri = ri_ref[r, :]
            xc = xc_s[r, :]
            ra = ra_ref[r, :]
            dxcext[r, :] = G * mult * ri
            dri = G * mult * xc
            dmult = G * ri * xc
            dlog_a = (G * hm1) * a - dmult * (a * a) / mult
            add_acc(R_LAM, dlog_a * (-LRU_C * ra) * dsp_dlam)
            dpa = dlog_a * (-LRU_C * sp) * ra * (1.0 - ra)
            dpi = dri * ri * (1.0 - ri)
            add_acc(R_BA, dpa)
            add_acc(R_BI, dpi)
            dpa_s[r, :] = dpa
            dpi_s[r, :] = dpi
            return 0
        lax.fori_loop(0, ng, p3, 0)

        for hh_ in range(H):
            cs = slice(hh_ * hd, (hh_ + 1) * hd)
            dpa_b = dpa_s[:, cs].astype(BF16)
            dpi_b = dpi_s[:, cs].astype(BF16)
            xc_b = xc_s[:, cs].astype(BF16)
            dxcext[0:tb, cs] += _dot(dpa_b, wa_ref[hh_], NT) + _dot(dpi_b, wi_ref[hh_], NT)
            dwa_ref[hh_] += _dot(xc_b, dpa_b, TN)
            dwi_ref[hh_] += _dot(xc_b, dpi_b, TN)

        def p4(g, _):
            r = _rows(g)
            rn = _rows(g + 1)
            dxc = dxcext[r, :]
            dxc_n = dxcext[rn, :]
            x_p = xext[r, :]
            x_c = xext[rn, :]
            add_acc(R_CB, dxc)
            dlx = dxc * cw_ref[3:4, :]
            add_acc(R_CW + 3, dxc * x_c)
            for d in range(1, 4):
                dlx = dlx + _shift_fwd(dxc, dxc_n, d) * cw_ref[3 - d:4 - d, :]
                add_acc(R_CW + 3 - d, dxc * _shift_back(x_p, x_c, d))
            dz_s[r, 0:C] = dlx

            dq = dqext[r, :]
            dq_n = dqext[rn, :]
            p_p = pext[r, :]
            p_c = pext[rn, :]
            dp = dq * sw_ref[2:3, :]
            add_acc(R_SW + 2, dq * p_c)
            for d in range(1, 3):
                dp = dp + _shift_fwd(dq, dq_n, d) * sw_ref[2 - d:3 - d, :]
                add_acc(R_SW + 2 - d, dq * _shift_back(p_p, p_c, d))
            dz_s[r, 3 * C:4 * C] = dp * z_ref[r, 4 * C:5 * C]
            dz_s[r, 4 * C:5 * C] = dp * z_ref[r, 3 * C:4 * C]
            return 0
        lax.fori_loop(0, ng, p4, 0)

        dqext[tb:tb + HDR, :] = dqext[0:HDR, :]
        dxcext[tb:tb + HDR, :] = dxcext[0:HDR, :]

        def cast(g, _):
            r = pl.ds(pl.multiple_of(g * BF16_ROWS, BF16_ROWS), BF16_ROWS)
            dz_ref[r, :] = dz_s[r, :].astype(BF16)
            return 0
        lax.fori_loop(0, tb // BF16_ROWS, cast, 0)

        @pl.when(i == nb - 1)
        def _():
            small_ref[...] = jnp.zeros_like(small_ref)
            for k in range(N_ACC):
                small_ref[k:k + 1, :] = jnp.sum(acc_s[k], axis=0, keepdims=True)

    tpg = tb // SUBLANES
    full = lambda shape: pl.BlockSpec(shape, lambda i: (0,) * len(shape))
    blk = lambda w: pl.BlockSpec((tb, w), lambda i: (nb - 1 - i, 0))
    prev = lambda w: pl.BlockSpec((SUBLANES, w), lambda i: (jnp.maximum((nb - 1 - i) * tpg - 1, 0), 0))
    ext = pltpu.VMEM((tb + HDR, C), F32)
    tile = pltpu.VMEM((tb, C), F32)
    return pl.pallas_call(
        body, name=name, grid=(nb,),
        in_specs=[blk(5 * C), prev(5 * C), blk(C), prev(C)] + [blk(C)] * 8
        + [blk(2 * C), full((4, C)), full((H, hd, hd)), full((H, hd, hd)), full((1, C)), full((3, C)),
           full((1, C)), full((1, C))],
        out_specs=[blk(5 * C), full((SMALL_ROWS, C)), full((H, hd, hd)), full((H, hd, hd))],
        out_shape=[SDS((T, 5 * C), BF16), SDS((SMALL_ROWS, C), F32), SDS((H, hd, hd), F32), SDS((H, hd, hd), F32)],
        scratch_shapes=[ext] * 6 + [tile] * 3 + [pltpu.VMEM((tb, 5 * C), F32), pltpu.VMEM((N_ACC, SUBLANES, C), F32),
                                                pltpu.VMEM((SUBLANES, C), F32)],
        compiler_params=_cp(("arbitrary",), 56),
    )(z, z, h, h, *saved, dy, cw, wa, wi, lam, sw, glo, gso)


def _add_slabs(terms, out_dtype, name):
    R, Ccols = terms[0].shape
    br = _blk(R, 512, BF16_ROWS)
    n = len(terms)

    def body(*refs):
        s = refs[0][...].astype(F32)
        for t_ref in refs[1:n]:
            s = s + t_ref[...].astype(F32)
        refs[n][...] = s.astype(out_dtype)

    spec = pl.BlockSpec((br, Ccols), lambda i: (i, 0))
    return pl.pallas_call(
        body, name=name, grid=(R // br,), in_specs=[spec] * n, out_specs=spec, out_shape=SDS((R, Ccols), out_dtype),
        compiler_params=_cp(("arbitrary",), 40),
    )(*terms)


def _reduced_rows(sb, lb, off, rows, extra_in, n_out, body, name, chip):
    Ccols = sb.shape[2]
    br = _blk(math.gcd(off, rows) if off else rows, 192, BF16_ROWS)
    ob = off // br
    src = lambda pick: pl.BlockSpec((1, br, Ccols), lambda i, c: (pick(c), ob + i, 0))
    own = pl.BlockSpec((br, Ccols), lambda i, c: (i, 0))
    return pl.pallas_call(
        body, name=name,
        grid_spec=pltpu.PrefetchScalarGridSpec(
            num_scalar_prefetch=1, grid=(rows // br,),
            in_specs=[src(lambda c: c[0]), src(lambda c: 0), src(lambda c: 1), src(lambda c: 2)] + [own] * len(extra_in),
            out_specs=[own] * n_out),
        out_shape=[SDS((rows, Ccols), F32)] * n_out,
        compiler_params=_cp(("arbitrary",), 40),
    )(chip, sb, lb, lb, lb, *extra_in)


def _sum4(sb_ref, l0, l1, l2):
    s = sb_ref[0].astype(F32)
    for t_ref in (l0, l1, l2):
        s = s + t_ref[0].astype(F32)
    return s


def _final_grad(sb, lb, chip, off, rows, name):
    def body(chip_ref, sb_ref, l0, l1, l2, o_ref):
        o_ref[...] = _sum4(sb_ref, l0, l1, l2)

    return _reduced_rows(sb, lb, off, rows, [], 1, body, name, chip)[0]


def _adamw_math(w, g, m, v):
    nm = ADAM_B1 * m + (1.0 - ADAM_B1) * g
    nv = ADAM_B2 * v + (1.0 - ADAM_B2) * (g * g)
    c1 = 1.0 - ADAM_B1 ** ADAM_STEP
    c2 = 1.0 - ADAM_B2 ** ADAM_STEP
    return -ADAM_LR * ((nm / c1) / (jnp.sqrt(nv / c2) + ADAM_EPS) + ADAM_WD * w), nm, nv


def _sum_adamw(sb, lb, chip, off, rows, w, m, v, name):
    def body(chip_ref, sb_ref, l0, l1, l2, w_ref, m_ref, v_ref, g_ref, d_ref, nm_ref, nv_ref):
        g = _sum4(sb_ref, l0, l1, l2)
        g_ref[...] = g
        d_ref[...], nm_ref[...], nv_ref[...] = _adamw_math(w_ref[...], g, m_ref[...], v_ref[...])

    return _reduced_rows(sb, lb, off, rows, [w, m, v], 4, body, name, chip)


def _adamw(w, g, m, v, name):
    R, Ccols = w.shape
    br = _blk(R, 256, SUBLANES)

    def body(w_ref, g_ref, m_ref, v_ref, d_ref, nm_ref, nv_ref):
        d_ref[...], nm_ref[...], nv_ref[...] = _adamw_math(w_ref[...], g_ref[...], m_ref[...], v_ref[...])

    spec = pl.BlockSpec((br, Ccols), lambda i: (i, 0))
    return pl.pallas_call(
        body, name=name, grid=(R // br,), in_specs=[spec] * 4, out_specs=[spec] * 3,
        out_shape=[SDS((R, Ccols), F32)] * 3, compiler_params=_cp(("arbitrary",), 40),
    )(w, g, m, v)


def _place():
    return lax.axis_index("x"), lax.axis_index("y"), lax.axis_index("c")


def _dev_rows(ref, dev, rows):
    return ref.at[pl.ds((4 * dev[0] + 2 * dev[1] + dev[2]) * rows, rows), :]


def _remote(src, dst, send_sem, recv_sem, to):
    return pltpu.make_async_remote_copy(src_ref=src, dst_ref=dst, send_sem=send_sem, recv_sem=recv_sem,
                                        device_id=to, device_id_type=MESH)


SAME_CORE_AND_SIBLING = ((0, 0, 1), (1, 0, 0), (0, 1, 0), (1, 1, 0))


def _merge_phases(a, b):
    na_in, na_out, na_sem = len(a.inputs), len(a.out_shapes), len(a.sem_shapes)

    def build(ins, outs, sems, stage):
        return (a.build(ins[:na_in], outs[:na_out], sems[:na_sem], stage)
                + b.build(ins[na_in:], outs[na_out:], sems[na_sem:], stage))

    aliases = dict(a.aliases)
    aliases.update({na_in + i: na_out + o for i, o in b.aliases.items()})
    return _Carried(a.inputs + b.inputs, a.out_shapes + b.out_shapes, aliases, a.sem_shapes + b.sem_shapes, build,
                    has_mid=a.has_mid or b.has_mid)


def _ag_direct_phase(slab, pieces, flips):
    W = slab.shape[1]
    n = len(pieces)
    npeer = len(flips)

    def build(ins, outs, sems, stage):
        if stage == "mid":
            return []
        starting = stage == "start"
        (slab_ref,) = ins
        send_sems, recv_sems, local_sems = sems
        x, y, c = _place()
        me = (x, y, c)
        peers = [tuple(1 - v if f else v for v, f in zip(me, flip)) for flip in flips]
        todo = []
        for p, (off, rows) in enumerate(pieces):
            src = slab_ref.at[pl.ds(off, rows), :]
            mine = pltpu.make_async_copy(src, _dev_rows(outs[p], me, rows), local_sems.at[p])
            todo.append(mine.start if starting else mine.wait)
            for k, peer in enumerate(peers):
                snd = _remote(src, _dev_rows(outs[p], me, rows), send_sems.at[k, p], recv_sems.at[k, p], peer)
                if starting:
                    todo.append(snd.start)
                else:
                    theirs = _dev_rows(outs[p], peer, rows)
                    rcv = _remote(theirs, theirs, send_sems.at[k, p], recv_sems.at[k, p], me)
                    todo += [rcv.wait_recv, snd.wait_send]
        return todo

    dma = pltpu.SemaphoreType.DMA
    return _Carried([slab], [SDS((N_DEV * rows, W), slab.dtype) for _, rows in pieces], {},
                    [dma((npeer, n)), dma((npeer, n)), dma((n,))], build)


def _ag_two_level_phase(slab, pieces):
    W = slab.shape[1]
    n = len(pieces)

    def build(ins, outs, sems, stage):
        (slab_ref,) = ins
        send_sems, recv_sems, local_sems = sems
        x, y, c = _place()
        me, sibling = (x, y, c), (x, y, 1 - c)
        chips = [(1 - x, y), (x, 1 - y), (1 - x, 1 - y)]
        todo = []
        for p, (off, rows) in enumerate(pieces):
            src = slab_ref.at[pl.ds(off, rows), :]
            own = _dev_rows(outs[p], me, rows)
            landed = [_dev_rows(outs[p], (*chip, c), rows) for chip in chips]

            def mine():
                return pltpu.make_async_copy(src, own, local_sems.at[p])

            def first():
                return [_remote(src, own, send_sems.at[k, p], recv_sems.at[k, p], to)
                        for k, to in enumerate([sibling] + [(*chip, c) for chip in chips])]

            def passed():
                return [_remote(blk, blk, send_sems.at[4 + j, p], recv_sems.at[4 + j, p], sibling)
                        for j, blk in enumerate(landed)]

            def arrival(k, blk):
                return _remote(blk, blk, send_sems.at[k, p], recv_sems.at[k, p], me).wait_recv

            if stage == "start":
                todo += [mine().start] + [cp.start for cp in first()]
            elif stage == "mid":
                for j, (blk, fwd) in enumerate(zip(landed, passed())):
                    todo += [arrival(1 + j, blk), fwd.start]
            else:
                theirs = [_dev_rows(outs[p], sibling, rows)] + [_dev_rows(outs[p], (*chip, 1 - c), rows) for chip in chips]
                todo += [arrival(k, blk) for k, blk in zip((0, 4, 5, 6), theirs)]
                todo += [cp.wait_send for cp in first() + passed()] + [mine().wait]
        return todo

    dma = pltpu.SemaphoreType.DMA
    return _Carried([slab], [SDS((N_DEV * rows, W), slab.dtype) for _, rows in pieces], {},
                    [dma((7, n)), dma((7, n)), dma((n,))], build, has_mid=True)


def _ag_forward_phase(gathered, pieces):
    n = len(pieces)

    def build(ins, outs, sems, stage):
        if stage == "mid":
            return []
        starting = stage == "start"
        send_sems, recv_sems = sems
        x, y, c = _place()
        me, sibling = (x, y, c), (x, y, 1 - c)
        chips = [(1 - x, y), (x, 1 - y), (1 - x, 1 - y)]
        todo = []
        for p, (_, rows) in enumerate(pieces):
            for j, chip in enumerate(chips):
                snd = _remote(_dev_rows(ins[p], (*chip, c), rows), _dev_rows(outs[p], (*chip, c), rows),
                              send_sems.at[j, p], recv_sems.at[j, p], sibling)
                if starting:
                    todo.append(snd.start)
                else:
                    theirs = _dev_rows(outs[p], (*chip, 1 - c), rows)
                    rcv = _remote(theirs, theirs, send_sems.at[j, p], recv_sems.at[j, p], me)
                    todo += [rcv.wait_recv, snd.wait_send]
        return todo

    dma = pltpu.SemaphoreType.DMA
    return _Carried(gathered, [SDS(g.shape, g.dtype) for g in gathered], {p: p for p in range(n)},
                    [dma((3, n)), dma((3, n))], build)


def _rs_chips_phase(sb):
    _, R, W = sb.shape

    def build(ins, outs, sems, stage):
        if stage == "mid":
            return []
        (sb_ref,), (land_ref,) = ins, outs
        send_sems, recv_sems = sems
        x, y, c = _place()
        chips = [(1 - x, y), (x, 1 - y), (1 - x, 1 - y)]
        cps = [_remote(sb_ref.at[2 * chip[0] + chip[1]], land_ref.at[j], send_sems.at[j], recv_sems.at[j], (*chip, c))
               for j, chip in enumerate(chips)]
        if stage == "start":
            return [cp.start for cp in cps]
        return [cp.wait_recv for cp in cps] + [cp.wait_send for cp in cps]

    dma = pltpu.SemaphoreType.DMA
    return _Carried([sb], [SDS((3, R, W), sb.dtype)], {}, [dma((3,)), dma((3,))], build)


def _allgather(slab, pieces, name):
    R, W = slab.shape
    n = len(pieces)
    assert sum(rows for _, rows in pieces) == R

    def body(slab_ref, *refs):
        outs = refs[:n]
        send_sems, recv_sems, local_sems = refs[n:]
        x, y, c = _place()
        me, sibling = (x, y, c), (x, y, 1 - c)
        chips = [(1 - x, y), (x, 1 - y), (1 - x, 1 - y)]

        def dst_rows(p, origin):
            rows = pieces[p][1]
            start = (4 * origin[0] + 2 * origin[1] + origin[2]) * rows
            return outs[p].at[pl.ds(start, rows), :]

        def copies(k, origin, to, from_slab):
            out = []
            for p, (off, rows) in enumerate(pieces):
                dst = dst_rows(p, origin)
                src = slab_ref.at[pl.ds(off, rows), :] if from_slab else dst
                out.append(pltpu.make_async_remote_copy(
                    src_ref=src, dst_ref=dst, send_sem=send_sems.at[k, p], recv_sem=recv_sems.at[k, p],
                    device_id=to, device_id_type=MESH))
            return out

        mine = [pltpu.make_async_copy(slab_ref.at[pl.ds(off, rows), :], dst_rows(p, me), local_sems.at[p])
                for p, (off, rows) in enumerate(pieces)]
        for cp in mine:
            cp.start()
        first = copies(0, me, sibling, True)
        for j, chip in enumerate(chips):
            first += copies(1 + j, me, (*chip, c), True)
        for cp in first:
            cp.start()
        passed = []
        for j, chip in enumerate(chips):
            for cp in copies(1 + j, (*chip, c), me, False):
                cp.wait_recv()
            fwd = copies(4 + j, (*chip, c), sibling, False)
            for cp in fwd:
                cp.start()
            passed += fwd
        for cp in copies(0, sibling, me, False):
            cp.wait_recv()
        for j, chip in enumerate(chips):
            for cp in copies(4 + j, (*chip, 1 - c), me, False):
                cp.wait_recv()
        for cp in first + passed:
            cp.wait_send()
        for cp in mine:
            cp.wait()

    return pl.pallas_call(
        body, name=name,
        in_specs=[HBM_SPEC], out_specs=[HBM_SPEC] * n,
        out_shape=[SDS((N_DEV * rows, W), slab.dtype) for _, rows in pieces],
        scratch_shapes=[pltpu.SemaphoreType.DMA((7, n)), pltpu.SemaphoreType.DMA((7, n)), pltpu.SemaphoreType.DMA((n,))],
    )(slab)


def _rs_sibling(grads, pieces, name):
    W = grads[0].shape[1]
    R = sum(rows for _, rows in pieces)
    n = len(pieces)
    dt = grads[0].dtype
    max_rows = max(rows for _, rows in pieces)
    steps = [(q, p) for q in range(N_CHIP) for p in range(n)]
    ns = len(steps)
    ADD_ROWS = 64
    SLOTS = 3
    assert all(rows % ADD_ROWS == 0 for _, rows in pieces)

    def body(*refs):
        g_refs = refs[:n]
        sb_ref, mine_buf, send_buf, land_buf, out_buf, in_sems, out_sems, send_sems, recv_sems, credit = refs[n:]
        x, y, c = _place()
        sibling = (x, y, 1 - c)

        def loads(s):
            q, p = steps[s]
            rows = pieces[p][1]
            slot = s % SLOTS
            mine = g_refs[p].at[pl.ds((2 * q + c) * rows, rows), :]
            theirs = g_refs[p].at[pl.ds((2 * q + 1 - c) * rows, rows), :]
            return (pltpu.make_async_copy(mine, mine_buf.at[slot, pl.ds(0, rows), :], in_sems.at[slot, 0]),
                    pltpu.make_async_copy(theirs, send_buf.at[slot, pl.ds(0, rows), :], in_sems.at[slot, 1]))

        def send(s):
            rows = pieces[steps[s][1]][1]
            slot = s % SLOTS
            return pltpu.make_async_remote_copy(
                src_ref=send_buf.at[slot, pl.ds(0, rows), :], dst_ref=land_buf.at[slot, pl.ds(0, rows), :],
                send_sem=send_sems.at[slot], recv_sem=recv_sems.at[slot], device_id=sibling, device_id_type=MESH)

        def store(s):
            q, p = steps[s]
            off, rows = pieces[p]
            slot = s % SLOTS
            return pltpu.make_async_copy(out_buf.at[slot, pl.ds(0, rows), :], sb_ref.at[q, pl.ds(off, rows), :],
                                         out_sems.at[slot])

        def start_send(s):
            for cp in loads(s):
                cp.wait()
            if s >= SLOTS:
                pl.semaphore_wait(credit.at[s % SLOTS], 1)
            send(s).start()

        for s in range(min(SLOTS, ns)):
            for cp in loads(s):
                cp.start()
        for s in range(min(SLOTS - 1, ns)):
            start_send(s)
        for s in range(ns):
            slot = s % SLOTS
            rows = pieces[steps[s][1]][1]
            if s + SLOTS - 1 < ns:
                start_send(s + SLOTS - 1)
            send(s).wait_recv()
            if s >= SLOTS:
                store(s - SLOTS).wait()

            def add(k, _, slot=slot):
                r = pl.ds(pl.multiple_of(k * ADD_ROWS, ADD_ROWS), ADD_ROWS)
                out_buf[slot, r, :] = (mine_buf[slot, r, :].astype(F32) + land_buf[slot, r, :].astype(F32)).astype(dt)
                return 0
            lax.fori_loop(0, rows // ADD_ROWS, add, 0)
            if s + SLOTS < ns:
                pl.semaphore_signal(credit.at[slot], inc=1, device_id=sibling, device_id_type=MESH)
            store(s).start()
            send(s).wait_send()
            if s + SLOTS < ns:
                for cp in loads(s + SLOTS):
                    cp.start()
        for s in range(max(ns - SLOTS, 0), ns):
            store(s).wait()

    buf = pltpu.VMEM((SLOTS, max_rows, W), dt)
    return pl.pallas_call(
        body, name=name,
        in_specs=[HBM_SPEC] * n, out_specs=HBM_SPEC,
        out_shape=SDS((N_CHIP, R, W), dt),
        scratch_shapes=[buf, buf, buf, buf, pltpu.SemaphoreType.DMA((SLOTS, 2)), pltpu.SemaphoreType.DMA((SLOTS,)),
                        pltpu.SemaphoreType.DMA((SLOTS,)), pltpu.SemaphoreType.DMA((SLOTS,)),
                        pltpu.SemaphoreType.REGULAR((SLOTS,))],
        compiler_params=pltpu.CompilerParams(vmem_limit_bytes=48 * MIB),
    )(*grads)


SMALL_NAMES = ("ffn1_norm", "mix_norm", "ffn2_norm", "final_norm", "lru_conv_w", "lru_conv_b", "lru_w_a", "lru_b_a",
               "lru_w_i", "lru_b_i", "lru_lambda", "sc_conv_w", "lru_out_norm", "sc_out_norm")
WEIGHT_NAMES = ("ffn1_norm", "ffn1_w_gate", "ffn1_w_up", "ffn1_w_down", "mix_norm", "w_in", "lru_conv_w", "lru_conv_b",
                "lru_w_a", "lru_b_a", "lru_w_i", "lru_b_i", "lru_lambda", "sc_conv_w", "lru_out_norm", "sc_out_norm",
                "w_out", "ffn2_norm", "ffn2_w_gate", "ffn2_w_up", "ffn2_w_down", "final_norm")
BIG = (("ffn1_w_gate", True), ("ffn1_w_up", True), ("ffn1_w_down", False), ("ffn2_w_gate", True), ("ffn2_w_up", True),
       ("ffn2_w_down", False), ("w_in", True), ("w_out", False))


SLAB_ROW_ALIGN = 256


def _pack_rows(parts, width):
    rows, counts = [], []
    for p in parts:
        flat = p.reshape(-1)
        nr = -(-flat.shape[0] // width)
        nr = -(-nr // SUBLANES) * SUBLANES
        rows.append(jnp.pad(flat, (0, nr * width - flat.shape[0])).reshape(nr, width))
        counts.append(nr)
    total = sum(counts)
    pad = -(-total // SLAB_ROW_ALIGN) * SLAB_ROW_ALIGN - total
    if pad:
        rows.append(jnp.zeros((pad, width), rows[0].dtype))
    return jnp.concatenate(rows, axis=0), counts


def _stack_rows(blocks):
    pieces, off = [], 0
    for b in blocks:
        pieces.append((off, b.shape[0]))
        off += b.shape[0]
    return jnp.concatenate(blocks, axis=0), pieces


def _unpack_rows(slab, counts, shapes):
    out, r = [], 0
    for nr, shape in zip(counts, shapes):
        size = math.prod(shape)
        out.append(slab[r:r + nr].reshape(-1)[:size].reshape(shape))
        r += nr
    return out


def kernel(x, ffn1_norm, ffn1_w_gate, ffn1_w_up, ffn1_w_down, mix_norm, w_in, lru_conv_w, lru_conv_b, lru_w_a, lru_b_a, lru_w_i, lru_b_i, lru_lambda, sc_conv_w, lru_out_norm, sc_out_norm, w_out, ffn2_norm, ffn2_w_gate, ffn2_w_up, ffn2_w_down, final_norm, loss_target, m_ffn1_norm, m_ffn1_w_gate, m_ffn1_w_up, m_ffn1_w_down, m_mix_norm, m_w_in, m_lru_conv_w, m_lru_conv_b, m_lru_w_a, m_lru_b_a, m_lru_w_i, m_lru_b_i, m_lru_lambda, m_sc_conv_w, m_lru_out_norm, m_sc_out_norm, m_w_out, m_ffn2_norm, m_ffn2_w_gate, m_ffn2_w_up, m_ffn2_w_down, m_final_norm, v_ffn1_norm, v_ffn1_w_gate, v_ffn1_w_up, v_ffn1_w_down, v_mix_norm, v_w_in, v_lru_conv_w, v_lru_conv_b, v_lru_w_a, v_lru_b_a, v_lru_w_i, v_lru_b_i, v_lru_lambda, v_sc_conv_w, v_lru_out_norm, v_sc_out_norm, v_w_out, v_ffn2_norm, v_ffn2_w_gate, v_ffn2_w_up, v_ffn2_w_down, v_final_norm):
    a = dict(locals())
    w = {n: a[n] for n in WEIGHT_NAMES}
    m = {n: a["m_" + n] for n in WEIGHT_NAMES}
    v = {n: a["v_" + n] for n in WEIGHT_NAMES}
    ax, ay, ac = _place()
    dev = 4 * ax + 2 * ay + ac
    chip = (2 * ax + ay).astype(jnp.int32).reshape(1)

    x0 = x[0]
    tgt = loss_target[0]
    T, D = x0.shape
    C = D // 2
    H, hd = lru_w_a.shape[1], lru_w_a.shape[2]
    CL = lru_conv_w.shape[2]

    shards = []
    for name, transposed in BIG:
        s = w[name][0]
        shards.append((s.T if transposed else s).astype(BF16))
    taps = jnp.concatenate([lru_conv_w[0], sc_conv_w[0], jnp.zeros((1, CL), F32)], axis=0)
    taps_row = lax.bitcast_convert_type(taps, BF16).reshape(1, -1)
    taps_blk = jnp.pad(taps_row, ((0, BF16_ROWS - 1), (0, D - taps_row.shape[1])))
    s_wg1, s_wu1, s_wd1, s_wg2, s_wu2, s_wd2, s_win, s_wout = shards
    slab_g1, pcs_g1 = _stack_rows([s_wg1])
    slab_u1, pcs_u1 = _stack_rows([s_wu1])
    slab_d1, pcs_d1 = _stack_rows([s_wd1])
    slab_mw, pcs_mw = _stack_rows([s_win, s_wout, taps_blk])
    slab_g2, pcs_g2 = _stack_rows([s_wg2])
    slab_ud2, pcs_ud2 = _stack_rows([s_wu2, s_wd2])
    (wg1,) = _allgather(slab_g1, pcs_g1, "allgather_ffn1_gate")

    g1, gm, g3 = ffn1_norm, mix_norm, ffn2_norm
    phase = _merge_phases(_ag_two_level_phase(slab_u1, pcs_u1), _ag_direct_phase(slab_d1, pcs_d1, SAME_CORE_AND_SIBLING))
    (n1, hg1), got = _norm_proj(x0, g1, [wg1], [BF16], False, "ffn1_gate", carried=phase)
    wu1, d1 = got[0], got[1:]
    phase = _merge_phases(_ag_forward_phase(d1, pcs_d1), _ag_direct_phase(slab_mw, pcs_mw, SAME_CORE_AND_SIBLING))
    (hu1, act1), got = _up_act(n1, wu1, hg1, "ffn1_up", carried=phase)
    wd1, mixw = got[0], got[1:]
    phase = _merge_phases(_ag_forward_phase(mixw, pcs_mw), _ag_direct_phase(slab_g2, pcs_g2, SAME_CORE_AND_SIBLING))
    x1, got = _mm_res(act1, wd1, x0, 0.5, "ffn1_down", carried=phase)
    (win, wout, taps_all), g2 = got[:3], got[3:]
    phase = _merge_phases(_ag_forward_phase(g2, pcs_g2), _ag_direct_phase(slab_ud2, pcs_ud2, SAME_CORE_AND_SIBLING))
    (n2, z), got = _norm_proj(x1, gm, [win], [F32], False, "in_proj", carried=phase)
    (wg2,), ud2 = got[:1], got[1:]
    taps_all = taps_all.reshape(N_DEV, BF16_ROWS, D)[:, 0, :2 * SUBLANES * CL].reshape(N_DEV, SUBLANES, CL, 2)
    taps_all = lax.bitcast_convert_type(taps_all, F32)
    taps_all = taps_all.transpose(1, 0, 2).reshape(SUBLANES, N_DEV * CL)
    cw, sw = taps_all[0:4], taps_all[4:7]

    gf = final_norm.reshape(1, D)
    cb = lru_conv_b
    wa, wi = lru_w_a[0].astype(BF16), lru_w_i[0].astype(BF16)
    ba, bi = lru_b_a.reshape(1, C), lru_b_i.reshape(1, C)
    lam, glo, gso = lru_lambda, lru_out_norm, sc_out_norm

    y, h, *saved = _mix_fwd(z, cw, cb, wa, ba, wi, bi, lam, sw, glo, gso, "mix_fwd")
    x2, (wu2, wd2) = _mm_res(y, wout, x1, 1.0, "out_proj", carried=_ag_forward_phase(ud2, pcs_ud2))
    n3, hg2, hu2, act2 = _norm_proj(x2, g3, [wg2, wu2], [BF16, BF16], True, "ffn2_up")
    x3 = _mm_res(act2, wd2, x2, 0.5, "ffn2_down")
    dx3, df2, d_gf, loss_blk = _loss_head(x3, gf, tgt, "loss_head")

    F = wd1.shape[0]
    bm_f = F // 4 if (F // 4) % LANES == 0 else 512

    def reduce_group(gs, tag):
        pcs, off = [], 0
        for g_ in gs:
            pcs.append((off, g_.shape[0] // N_DEV))
            off += g_.shape[0] // N_DEV
        sb_ = _rs_sibling(gs, pcs, "rs_sibling_add_" + tag)
        return sb_, pcs

    dhg2, dhu2 = _ffn_bwd_act(df2, wd2, hg2, hu2, "ffn2_bwd_act")
    d_wd2 = _dw_tn(act2, df2, bm_f, "ffn2_dw_down")
    d_wg2 = _dw_tn(dhg2, n3, bm_f, "ffn2_dw_gate")
    d_wu2 = _dw_tn(dhu2, n3, bm_f, "ffn2_dw_up")
    sb_f2, pcs_f2 = reduce_group([d_wg2, d_wu2, d_wd2], "ffn2")
    (dx2, dx2b, d_g3), (lb_f2,) = _mm_rmsbwd([(dhg2, wg2), (dhu2, wu2)], x2, g3, dx3, 1.0, "ffn2_bwd_in",
                                             carried=_rs_chips_phase(sb_f2))
    dy = _mm_nt(dx2b, wout, "out_proj_bwd")
    d_wout = _dw_tn(y, dx2b, 1024, "out_proj_dw")
    dz, small, d_wa, d_wi = _mix_bwd(z, h, saved, dy, cw, wa, wi, lam, sw, glo, gso, "mix_bwd")
    d_win = _dw_tn(dz, n2, 1280, "in_proj_dw")
    sb_mx, pcs_mx = reduce_group([d_win, d_wout], "mix")
    (dx1, df1, d_gm), (lb_mx,) = _mm_rmsbwd([(dz, win)], x1, gm, dx2, 0.5, "in_proj_bwd",
                                            carried=_rs_chips_phase(sb_mx))
    dhg1, dhu1 = _ffn_bwd_act(df1, wd1, hg1, hu1, "ffn1_bwd_act")
    d_wd1 = _dw_tn(act1, df1, bm_f, "ffn1_dw_down")
    d_wg1 = _dw_tn(dhg1, n1, bm_f, "ffn1_dw_gate")
    d_wu1 = _dw_tn(dhu1, n1, bm_f, "ffn1_dw_up")
    sb_f1, pcs_f1 = reduce_group([d_wg1, d_wu1, d_wd1], "ffn1")
    early_names = [n_ for n_ in SMALL_NAMES if n_ != "ffn1_norm"]
    early_parts = dict(zip(SMALL_NAMES, [None, d_gm, d_g3, d_gf, small[R_CW:R_CW + 4], small[R_CB], d_wa, small[R_BA], d_wi,
                                         small[R_BI], small[R_LAM], small[R_SW:R_SW + 3], small[R_GLO], small[R_GSO]]))
    early_slab, early_counts = _pack_rows([early_parts[n_] for n_ in early_names], LANES)
    RE = early_slab.shape[0]
    phase = _merge_phases(_ag_two_level_phase(early_slab, [(0, RE)]), _rs_chips_phase(sb_f1))
    (dx0, _, d_g1), (early_all, lb_f1) = _mm_rmsbwd([(dhg1, wg1), (dhu1, wu1)], x0, g1, dx1, 1.0, "ffn1_bwd_in",
                                                    carried=phase)

    grads, delta, new_m, new_v = {}, {}, {}, {}
    transposed_shard = dict(BIG)
    for names, sb_, lb_, pcs in ((("ffn2_w_gate", "ffn2_w_up", "ffn2_w_down"), sb_f2, lb_f2, pcs_f2),
                                 (("w_in", "w_out"), sb_mx, lb_mx, pcs_mx),
                                 (("ffn1_w_gate", "ffn1_w_up", "ffn1_w_down"), sb_f1, lb_f1, pcs_f1)):
        for name, (off, rows) in zip(names, pcs):
            flip = transposed_shard[name] and w[name].shape[2] % LANES != 0
            if transposed_shard[name] and not flip:
                g_ = _final_grad(sb_, lb_, chip, off, rows, "rs_final_sum_" + name).T
                d_, m_, v_ = _adamw(w[name][0], g_, m[name][0], v[name][0], "adamw_" + name)
            else:
                view = (lambda t: t[0].T) if flip else (lambda t: t[0])
                g_, d_, m_, v_ = _sum_adamw(sb_, lb_, chip, off, rows, view(w[name]), view(m[name]), view(v[name]),
                                            "sum_adamw_" + name)
            back = (lambda t: t.T[None]) if flip else (lambda t: t[None])
            grads[name], delta[name], new_m[name], new_v[name] = back(g_), back(d_), back(m_), back(v_)

    shape_of = dict(zip(SMALL_NAMES, [(1, D), (1, D), (1, D), (D,), (1, 4, C), (1, C), (1, H, hd, hd), (1, H, hd),
                                      (1, H, hd, hd), (1, H, hd), (1, C), (1, 3, C), (1, C), (1, C)]))
    early_sum = _add_slabs([early_all[j * RE:(j + 1) * RE] for j in range(N_DEV)], F32, "small_grads_sum")
    small_full = dict(zip(early_names, _unpack_rows(early_sum, early_counts, [shape_of[n_] for n_ in early_names])))
    late_slab, late_counts = _pack_rows([d_g1], LANES)
    RL = late_slab.shape[0]
    (late_all,) = _allgather(late_slab, [(0, RL)], "allgather_ffn1_norm_grad")
    late_sum = _add_slabs([late_all[j * RL:(j + 1) * RL] for j in range(N_DEV)], F32, "ffn1_norm_grad_sum")
    (small_full["ffn1_norm"],) = _unpack_rows(late_sum, late_counts, [shape_of["ffn1_norm"]])

    for name in SMALL_NAMES:
        gfull = small_full[name]
        if name in ("lru_conv_w", "sc_conv_w"):
            gfull = lax.dynamic_slice_in_dim(gfull, dev * CL, CL, axis=2)
        grads[name] = gfull

    packs = [_pack_rows([t[n_] for n_ in SMALL_NAMES], LANES) for t in (w, grads, m, v)]
    sd, sm, sv = _adamw(packs[0][0], packs[1][0], packs[2][0], packs[3][0], "adamw_small")
    shapes = [w[n_].shape for n_ in SMALL_NAMES]
    for tgt_dict, slab_ in ((delta, sd), (new_m, sm), (new_v, sv)):
        for n_, val in zip(SMALL_NAMES, _unpack_rows(slab_, packs[0][1], shapes)):
            tgt_dict[n_] = val

    loss = lax.psum(loss_blk[0, 0], ("x", "y", "c"))
    return (loss, dx0[None], *[grads[n_] for n_ in WEIGHT_NAMES], *[delta[n_] for n_ in WEIGHT_NAMES],
            *[new_m[n_] for n_ in WEIGHT_NAMES], *[new_v[n_] for n_ in WEIGHT_NAMES])
```
